```python
import math
import jax, jax.numpy as jnp
from jax import lax
import numpy as np

D_MODEL = 1024
BATCH = 4
SEQ = 4096
DEPTH = 2

HEAD_DIM = 64
ROPE_THETA = 500000.0
ROPE_FRAC = 4
EPS = 1e-6
Q_BLOCK = 128

A_WIDTH = D_MODEL // 2
A_GROUPS = 4
A_GDIM = A_WIDTH // A_GROUPS
A_CHUNK = 128

B_HEADS = 8
B_QK_DIM = 32
B_V_DIM = 2 * B_QK_DIM
B_WIDTH = B_HEADS * B_V_DIM

C_WIDTH = D_MODEL // 2
C_KERNEL = 31

D_HEADS = 8
D_KV_GROUPS = 2
D_WIDTH = D_HEADS * HEAD_DIM
CMP_BLOCK = 32
CMP_STRIDE = 16
CMP_HIDDEN = 2 * HEAD_DIM
SLC_BLOCK = 64
SLC_TOPK = 16
WINDOW = 512

MEM_TOKENS = 256
M_HEADS = 4
M_WIDTH = M_HEADS * HEAD_DIM

MIX_WIDTH = A_WIDTH + B_WIDTH + M_WIDTH
EVEN_SPLITS = (A_WIDTH, A_WIDTH, A_WIDTH,
               2 * B_HEADS * B_QK_DIM, 2 * B_HEADS * B_QK_DIM, B_WIDTH, B_WIDTH,
               M_WIDTH, M_WIDTH)
ODD_SPLITS = (C_WIDTH, C_WIDTH, C_WIDTH,
              D_WIDTH, 6 * D_KV_GROUPS * HEAD_DIM, 3 * D_HEADS, D_WIDTH,
              M_WIDTH, M_WIDTH)
EVEN_IN = sum(EVEN_SPLITS)
ODD_IN = sum(ODD_SPLITS)

kernel_name = "hybrid_sgu_diffattn_conformer_nsa_block"


def rms_norm(x, g):
    xf = x.astype(jnp.float32)
    y = xf * lax.rsqrt(jnp.mean(xf * xf, axis=-1, keepdims=True) + EPS)
    return (y * g.astype(jnp.float32)).astype(x.dtype)


def rope_partial(x, pos):
    d = x.shape[-1]
    rd = d // ROPE_FRAC
    half = rd // 2
    inv = ROPE_THETA ** (-jnp.arange(half, dtype=jnp.float32) / half)
    ang = pos.astype(jnp.float32)[:, None] * inv[None, :]
    cos = jnp.cos(ang).astype(x.dtype)
    sin = jnp.sin(ang).astype(x.dtype)
    x1, x2, rest = x[..., :half], x[..., half:rd], x[..., rd:]
    return jnp.concatenate([x1 * cos - x2 * sin, x1 * sin + x2 * cos, rest], axis=-1)


def masked_softmax(s, mask):
    s = jnp.where(mask, s, -jnp.inf)
    m = jnp.max(s, axis=-1, keepdims=True)
    m = jnp.where(jnp.isfinite(m), m, 0.0)
    e = jnp.where(mask, jnp.exp(s - m), 0.0)
    return e / jnp.maximum(jnp.sum(e, axis=-1, keepdims=True), jnp.finfo(jnp.float32).tiny)


def split_cols(x, sizes):
    return jnp.split(x, [int(v) for v in np.cumsum(sizes)[:-1]], axis=-1)


def chunked_sgu(u, v, w_s, b_s, v_gain):
    Bz, S, _ = v.shape
    v = rms_norm(v.reshape(Bz, S, A_GROUPS, A_GDIM), v_gain.reshape(A_GROUPS, A_GDIM))
    v = v.reshape(Bz, S // A_CHUNK, A_CHUNK, A_GROUPS, A_GDIM)
    causal = jnp.tril(jnp.ones((A_CHUNK, A_CHUNK), dtype=bool))
    w = jnp.where(causal[None], w_s, 0.0)
    z = jnp.einsum('gts,bcsgd->bctgd', w, v) + b_s.T[:, :, None]
    return u * z.reshape(Bz, S, A_WIDTH)


def diff_attention(q, k, v, lam):
    Bz, H, _, S, dk = q.shape
    dv = v.shape[-1]
    nblk = S // Q_BLOCK
    scale = dk ** -0.5
    kpos = jnp.arange(S)
    qb = q.reshape(Bz, H, 2, nblk, Q_BLOCK, dk).transpose(3, 0, 1, 2, 4, 5)

    def block(args):
        qi, c = args
        qpos = c * Q_BLOCK + jnp.arange(Q_BLOCK)
        s = jnp.einsum('bhmqd,bhmkd->bhmqk', qi, k).astype(jnp.float32) * scale
        s = jnp.where(kpos[None, :] <= qpos[:, None], s, -jnp.inf)
        p = jax.nn.softmax(s, axis=-1)
        pd = p[:, :, 0] - lam * p[:, :, 1]
        return jnp.einsum('bhqk,bhkd->bhqd', pd.astype(v.dtype), v)

    o = lax.map(block, (qb, jnp.arange(nblk)))
    return o.transpose(1, 0, 3, 2, 4).reshape(Bz, S, H, dv)


def memory_attention(q, mem_n, w_kv, qnorm, knorm):
    Bz, S, _ = q.shape
    k, v = jnp.split(mem_n @ w_kv, 2, axis=-1)
    q = rms_norm(q.reshape(Bz, S, M_HEADS, HEAD_DIM), qnorm)
    k = rms_norm(k.reshape(Bz, -1, M_HEADS, HEAD_DIM), knorm)
    v = v.reshape(Bz, -1, M_HEADS, HEAD_DIM)
    s = jnp.einsum('bshd,bmhd->bhsm', q, k).astype(jnp.float32) * (HEAD_DIM ** -0.5)
    p = jax.nn.softmax(s, axis=-1)
    return jnp.einsum('bhsm,bmhd->bshd', p.astype(v.dtype), v).reshape(Bz, S, M_WIDTH)


def conformer_conv(a, b, conv_w, conv_b, norm_g):
    h = a * jax.nn.sigmoid(b)
    hp = jnp.pad(h, ((0, 0), (C_KERNEL - 1, 0), (0, 0)))
    y = lax.conv_general_dilated(hp, conv_w[:, None, :], window_strides=(1,), padding='VALID',
                                 dimension_numbers=('NWC', 'WIO', 'NWC'),
                                 feature_group_count=C_WIDTH) + conv_b
    return jax.nn.silu(rms_norm(y, norm_g))


def compress(t, cmp_idx, pos_emb, w1, w2):
    Bz, _, G, hd = t.shape
    blocks = t[:, cmp_idx] + pos_emb[:, None, :]
    flat = blocks.transpose(0, 1, 3, 2, 4).reshape(Bz, cmp_idx.shape[0], G, CMP_BLOCK * hd)
    return jax.nn.silu(flat @ w1) @ w2


def cmp_to_slc_overlap(n_cmp, n_slc):
    start = np.arange(n_cmp) * CMP_STRIDE
    s0 = np.arange(n_slc) * SLC_BLOCK
    lo = np.maximum(start[:, None], s0[None, :])
    hi = np.minimum(start[:, None] + CMP_BLOCK, s0[None, :] + SLC_BLOCK)
    return (np.clip(hi - lo, 0, None) / CMP_BLOCK).astype(np.float32)


def nsa(q, kv, gates, pos, qnorm, knorm, cmp_pos_k, cmp_w1_k, cmp_w2_k,
        cmp_pos_v, cmp_w1_v, cmp_w2_v):
    Bz, S = q.shape[:2]
    G, R, hd = D_KV_GROUPS, D_HEADS // D_KV_GROUPS, HEAD_DIM
    nblk = S // Q_BLOCK
    n_cmp = (S - CMP_BLOCK) // CMP_STRIDE + 1
    n_slc = S // SLC_BLOCK
    n_sel = min(SLC_TOPK, n_slc)
    scale = hd ** -0.5

    q = rope_partial(rms_norm(q, qnorm).transpose(0, 2, 1, 3), pos).reshape(Bz, G, R, S, hd)
    k_cmp_raw, v_cmp_raw, k_slc, v_slc, k_win, v_win = [kv[:, :, i] for i in range(6)]

    cmp_idx = np.arange(n_cmp)[:, None] * CMP_STRIDE + np.arange(CMP_BLOCK)[None, :]
    cmp_end = jnp.asarray(cmp_idx[:, -1])
    kc = compress(k_cmp_raw, cmp_idx, cmp_pos_k, cmp_w1_k, cmp_w2_k)
    vc = compress(v_cmp_raw, cmp_idx, cmp_pos_v, cmp_w1_v, cmp_w2_v).transpose(0, 2, 1, 3)
    kc = rope_partial(rms_norm(kc, knorm).transpose(0, 2, 1, 3), cmp_end)

    ks = rope_partial(rms_norm(k_slc, knorm).transpose(0, 2, 1, 3), pos)
    kb = ks.reshape(Bz, G, n_slc, SLC_BLOCK, hd)
    vb = v_slc.transpose(0, 2, 1, 3).reshape(Bz, G, n_slc, SLC_BLOCK, hd)

    pad = ((0, 0), (0, 0), (WINDOW, 0), (0, 0))
    kw = jnp.pad(rope_partial(rms_norm(k_win, knorm).transpose(0, 2, 1, 3), pos), pad)
    vw = jnp.pad(v_win.transpose(0, 2, 1, 3), pad)

    overlap = jnp.asarray(cmp_to_slc_overlap(n_cmp, n_slc))
    b_ix = jnp.arange(Bz)[:, None, None, None]
    g_ix = jnp.arange(G)[None, :, None, None]
    blk_tok = jnp.arange(SLC_BLOCK)
    slc_ids = jnp.arange(n_slc)
    win_off = jnp.arange(WINDOW + Q_BLOCK) - WINDOW

    qb = q.reshape(Bz, G, R, nblk, Q_BLOCK, hd).transpose(3, 0, 1, 2, 4, 5)
    gb = gates.transpose(0, 2, 1, 3).reshape(Bz, G, R, nblk, Q_BLOCK, 3).transpose(3, 0, 1, 2, 4, 5)

    def block(args):
        qi, gt, c = args
        qpos = c * Q_BLOCK + jnp.arange(Q_BLOCK)
        s = jnp.einsum('bgrqd,bgnd->bgrqn', qi, kc).astype(jnp.float32) * scale
        p_c = masked_softmax(s, cmp_end[None, :] <= qpos[:, None])
        o_c = jnp.einsum('bgrqn,bgnd->bgrqd', p_c.astype(vc.dtype), vc)
        imp = jnp.einsum('bgrqn,nj->bgqj', p_c, overlap)
        cur = (qpos // SLC_BLOCK)[:, None]
        imp = jnp.where(slc_ids[None, :] > cur, -jnp.inf, imp)
        forced = (slc_ids[None, :] == 0) | (slc_ids[None, :] == cur) | (slc_ids[None, :] == cur - 1)
        imp = jnp.where(forced, jnp.inf, imp)
        _, sel = lax.top_k(imp, n_sel)
        ksel = kb[b_ix, g_ix, sel].reshape(Bz, G, Q_BLOCK, n_sel * SLC_BLOCK, hd)
        vsel = vb[b_ix, g_ix, sel].reshape(Bz, G, Q_BLOCK, n_sel * SLC_BLOCK, hd)
        kpos = (sel[..., None] * SLC_BLOCK + blk_tok).reshape(Bz, G, Q_BLOCK, n_sel * SLC_BLOCK)
        s = jnp.einsum('bgrqd,bgqkd->bgrqk', qi, ksel).astype(jnp.float32) * scale
        p_s = masked_softmax(s, (kpos <= qpos[:, None])[:, :, None])
        o_s = jnp.einsum('bgrqk,bgqkd->bgrqd', p_s.astype(vsel.dtype), vsel)
        kwi = lax.dynamic_slice_in_dim(kw, c * Q_BLOCK, WINDOW + Q_BLOCK, axis=2)
        vwi = lax.dynamic_slice_in_dim(vw, c * Q_BLOCK, WINDOW + Q_BLOCK, axis=2)
        wpos = c * Q_BLOCK + win_off
        dpos = qpos[:, None] - wpos[None, :]
        s = jnp.einsum('bgrqd,bgkd->bgrqk', qi, kwi).astype(jnp.float32) * scale
        p_w = masked_softmax(s, (wpos[None, :] >= 0) & (dpos >= 0) & (dpos < WINDOW))
        o_w = jnp.einsum('bgrqk,bgkd->bgrqd', p_w.astype(vwi.dtype), vwi)
        return gt[..., 0:1] * o_c + gt[..., 1:2] * o_s + gt[..., 2:3] * o_w

    o = lax.map(block, (qb, gb, jnp.arange(nblk)))
    return o.transpose(1, 0, 4, 2, 3, 5).reshape(Bz, S, D_WIDTH)


def even_layer(x, mem_n, pos, layer, norm_g, w_in, a_vnorm, a_ws, a_bs, b_qnorm, b_knorm,
               b_lq1, b_lk1, b_lq2, b_lk2, b_subln, m_wkv, m_qnorm, m_knorm, w_out):
    Bz, S, _ = x.shape
    h = rms_norm(x, norm_g)
    a_u, a_v, a_g, b_q, b_k, b_v, b_g, m_q, m_g = split_cols(h @ w_in, EVEN_SPLITS)
    y_a = chunked_sgu(jax.nn.gelu(a_u), jax.nn.gelu(a_v), a_ws, a_bs, a_vnorm) * jax.nn.silu(a_g)
    lam_init = 0.8 - 0.6 * math.exp(-0.3 * layer)
    lam = (jnp.exp(jnp.sum(b_lq1.astype(jnp.float32) * b_lk1.astype(jnp.float32)))
           - jnp.exp(jnp.sum(b_lq2.astype(jnp.float32) * b_lk2.astype(jnp.float32))) + lam_init)
    q = rms_norm(b_q.reshape(Bz, S, B_HEADS, 2, B_QK_DIM), b_qnorm).transpose(0, 2, 3, 1, 4)
    k = rms_norm(b_k.reshape(Bz, S, B_HEADS, 2, B_QK_DIM), b_knorm).transpose(0, 2, 3, 1, 4)
    q, k = rope_partial(q, pos), rope_partial(k, pos)
    v = b_v.reshape(Bz, S, B_HEADS, B_V_DIM).transpose(0, 2, 1, 3)
    o_b = rms_norm(diff_attention(q, k, v, lam), b_subln) * (1.0 - lam_init)
    y_b = o_b.reshape(Bz, S, B_WIDTH) * jax.nn.silu(b_g)
    y_m = memory_attention(m_q, mem_n, m_wkv, m_qnorm, m_knorm) * jax.nn.silu(m_g)
    return x + jnp.concatenate([y_a, y_b, y_m], axis=-1) @ w_out


def odd_layer(x, mem_n, pos, norm_g, w_in, c_conv_w, c_conv_b, c_norm, d_qnorm, d_knorm,
              cmp_pos_k, cmp_w1_k, cmp_w2_k, cmp_pos_v, cmp_w1_v, cmp_w2_v,
              m_wkv, m_qnorm, m_knorm, w_out):
    Bz, S, _ = x.shape
    h = rms_norm(x, norm_g)
    c_a, c_b, c_g, d_q, d_kv, d_bg, d_g, m_q, m_g = split_cols(h @ w_in, ODD_SPLITS)
    y_c = conformer_conv(c_a, c_b, c_conv_w, c_conv_b, c_norm) * jax.nn.silu(c_g)
    kv = d_kv.reshape(Bz, S, 6, D_KV_GROUPS, HEAD_DIM)
    gates = jax.nn.sigmoid(d_bg.reshape(Bz, S, D_HEADS, 3))
    y_d = nsa(d_q.reshape(Bz, S, D_HEADS, HEAD_DIM), kv, gates, pos, d_qnorm, d_knorm,
              cmp_pos_k, cmp_w1_k, cmp_w2_k, cmp_pos_v, cmp_w1_v, cmp_w2_v) * jax.nn.silu(d_g)
    y_m = memory_attention(m_q, mem_n, m_wkv, m_qnorm, m_knorm) * jax.nn.silu(m_g)
    return x + jnp.concatenate([y_c, y_d, y_m], axis=-1) @ w_out


def setup_inputs(seed: int = 0) -> dict:
    key = jax.random.key(seed)
    keys = iter(jax.random.split(key, 48))

    def nrm(shape, scale):
        return jax.random.normal(next(keys), shape, jnp.float32) * scale

    def gain(n):
        return 1.0 + nrm((n,), 0.05)

    return {
        "x": nrm((BATCH, SEQ, D_MODEL), 1.0),
        "mem": nrm((BATCH, MEM_TOKENS, D_MODEL), 1.0),
        "mem_norm": gain(D_MODEL),
        "l0_norm": gain(D_MODEL),
        "l0_w_in": nrm((D_MODEL, EVEN_IN), D_MODEL ** -0.5),
        "l0_a_vnorm": gain(A_WIDTH),
        "l0_a_ws": nrm((A_GROUPS, A_CHUNK, A_CHUNK), A_CHUNK ** -0.5),
        "l0_a_bs": 1.0 + nrm((A_GROUPS, A_CHUNK), 0.1),
        "l0_b_qnorm": gain(B_QK_DIM),
        "l0_b_knorm": gain(B_QK_DIM),
        "l0_b_lq1": nrm((B_QK_DIM,), 0.1),
        "l0_b_lk1": nrm((B_QK_DIM,), 0.1),
        "l0_b_lq2": nrm((B_QK_DIM,), 0.1),
        "l0_b_lk2": nrm((B_QK_DIM,), 0.1),
        "l0_b_subln": gain(B_V_DIM),
        "l0_m_wkv": nrm((D_MODEL, 2 * M_WIDTH), D_MODEL ** -0.5),
        "l0_m_qnorm": gain(HEAD_DIM),
        "l0_m_knorm": gain(HEAD_DIM),
        "l0_w_out": nrm((MIX_WIDTH, D_MODEL), MIX_WIDTH ** -0.5),
        "l1_norm": gain(D_MODEL),
        "l1_w_in": nrm((D_MODEL, ODD_IN), D_MODEL ** -0.5),
        "l1_c_conv_w": nrm((C_KERNEL, C_WIDTH), C_KERNEL ** -0.5),
        "l1_c_conv_b": nrm((C_WIDTH,), 0.02),
        "l1_c_norm": gain(C_WIDTH),
        "l1_d_qnorm": gain(HEAD_DIM),
        "l1_d_knorm": gain(HEAD_DIM),
        "l1_d_cmp_pos_k": nrm((CMP_BLOCK, HEAD_DIM), 0.02),
        "l1_d_cmp_w1_k": nrm((CMP_BLOCK * HEAD_DIM, CMP_HIDDEN), (CMP_BLOCK * HEAD_DIM) ** -0.5),
        "l1_d_cmp_w2_k": nrm((CMP_HIDDEN, HEAD_DIM), CMP_HIDDEN ** -0.5),
        "l1_d_cmp_pos_v": nrm((CMP_BLOCK, HEAD_DIM), 0.02),
        "l1_d_cmp_w1_v": nrm((CMP_BLOCK * HEAD_DIM, CMP_HIDDEN), (CMP_BLOCK * HEAD_DIM) ** -0.5),
        "l1_d_cmp_w2_v": nrm((CMP_HIDDEN, HEAD_DIM), CMP_HIDDEN ** -0.5),
        "l1_m_wkv": nrm((D_MODEL, 2 * M_WIDTH), D_MODEL ** -0.5),
        "l1_m_qnorm": gain(HEAD_DIM),
        "l1_m_knorm": gain(HEAD_DIM),
        "l1_w_out": nrm((MIX_WIDTH, D_MODEL), MIX_WIDTH ** -0.5),
    }


def reference(x, mem, mem_norm,
              l0_norm, l0_w_in, l0_a_vnorm, l0_a_ws, l0_a_bs, l0_b_qnorm, l0_b_knorm,
              l0_b_lq1, l0_b_lk1, l0_b_lq2, l0_b_lk2, l0_b_subln, l0_m_wkv, l0_m_qnorm,
              l0_m_knorm, l0_w_out,
              l1_norm, l1_w_in, l1_c_conv_w, l1_c_conv_b, l1_c_norm, l1_d_qnorm, l1_d_knorm,
              l1_d_cmp_pos_k, l1_d_cmp_w1_k, l1_d_cmp_w2_k, l1_d_cmp_pos_v, l1_d_cmp_w1_v,
              l1_d_cmp_w2_v, l1_m_wkv, l1_m_qnorm, l1_m_knorm, l1_w_out):
    pos = jnp.arange(x.shape[1], dtype=jnp.int32)
    mem_n = rms_norm(mem, mem_norm)
    even_params = (l0_norm, l0_w_in, l0_a_vnorm, l0_a_ws, l0_a_bs, l0_b_qnorm, l0_b_knorm,
                   l0_b_lq1, l0_b_lk1, l0_b_lq2, l0_b_lk2, l0_b_subln, l0_m_wkv, l0_m_qnorm,
                   l0_m_knorm, l0_w_out)
    odd_params = (l1_norm, l1_w_in, l1_c_conv_w, l1_c_conv_b, l1_c_norm, l1_d_qnorm, l1_d_knorm,
                  l1_d_cmp_pos_k, l1_d_cmp_w1_k, l1_d_cmp_w2_k, l1_d_cmp_pos_v, l1_d_cmp_w1_v,
                  l1_d_cmp_w2_v, l1_m_wkv, l1_m_qnorm, l1_m_knorm, l1_w_out)
    for layer in range(DEPTH):
        if layer % 2 == 0:
            x = even_layer(x, mem_n, pos, layer + 1, *even_params)
        else:
            x = odd_layer(x, mem_n, pos, *odd_params)
    return x
```

```python
import functools
import math

import numpy as np
import jax
import jax.numpy as jnp
from jax import lax
from jax.experimental import pallas as pl
from jax.experimental.pallas import tpu as pltpu

F32 = jnp.float32
BF16 = jnp.bfloat16

EPS = 1e-6
ROPE_THETA = 500000.0
HEAD_DIM = 64
A_CHUNK = 128
A_GROUPS = 4
B_HEADS = 8
B_QK_DIM = 32
C_KERNEL = 31
D_HEADS = 8
D_KV_GROUPS = 2
CMP_BLOCK = 32
CMP_STRIDE = 16
SLC_BLOCK = 64
SLC_TOPK = 16
WINDOW = 512
M_HEADS = 4

LANES = 128
MXU_DIM = 256
HALO = 32
NSLC_PAD = 128

NT_DIMS = (((1,), (1,)), ((), ()))


def _cparams(semantics, vmem_mb):
    return pltpu.CompilerParams(dimension_semantics=semantics,
                                vmem_limit_bytes=vmem_mb * 1024 * 1024)


def _split_bf16(x):
    hi = x.astype(BF16)
    lo = (x - hi.astype(F32)).astype(BF16)
    return hi, lo


def _group_mean(x2, gsize):
    t, c = x2.shape
    w = min(c, MXU_DIM)
    r = lax.broadcasted_iota(jnp.int32, (w, w), 0) // gsize
    cc = lax.broadcasted_iota(jnp.int32, (w, w), 1) // gsize
    ones = jnp.where(r == cc, 1.0, 0.0).astype(BF16)
    outs = []
    for s in range(c // w):
        hi, lo = _split_bf16(x2[:, s * w:(s + 1) * w])
        outs.append(jnp.dot(hi, ones, preferred_element_type=F32)
                    + jnp.dot(lo, ones, preferred_element_type=F32))
    out = outs[0] if len(outs) == 1 else jnp.concatenate(outs, axis=1)
    return out * (1.0 / gsize)


def _tile_lanes(t, width):
    rep = width // t.shape[1]
    return t if rep == 1 else jnp.concatenate([t] * rep, axis=1)


def _rope(x, cos_t, s1_t, s2_t, half):
    width = x.shape[1]
    up = pltpu.roll(x, width - half, 1)
    dn = pltpu.roll(x, half, 1)
    return (x * _tile_lanes(cos_t, width) + up * _tile_lanes(s1_t, width)
            + dn * _tile_lanes(s2_t, width))


def _flash_update(s, m, l, acc_ref, idx, v):
    m_new = jnp.maximum(m, jnp.max(s, axis=-1, keepdims=True))
    m_safe = jnp.where(m_new == -jnp.inf, 0.0, m_new)
    alpha = jnp.exp(m - m_safe)
    p = jnp.exp(s - m_safe)
    l_new = alpha * l + jnp.sum(p, axis=-1, keepdims=True)
    acc_ref[idx] = alpha * acc_ref[idx] + jnp.dot(p.astype(BF16), v, preferred_element_type=F32)
    return m_new, l_new


def _norm_matmul_kernel(x_ref, g_ref, w_ref, o_ref):
    x = x_ref[...]
    ms = jnp.mean(x * x, axis=-1, keepdims=True)
    h = (x * lax.rsqrt(ms + EPS) * g_ref[...]).astype(BF16)
    o_ref[...] = jnp.dot(h, w_ref[...], preferred_element_type=F32)


def _norm_matmul(x, g, w, tm):
    n, d = x.shape
    c = w.shape[1]
    return pl.pallas_call(
        _norm_matmul_kernel,
        grid=(n // tm,),
        in_specs=[pl.BlockSpec((tm, d), lambda i: (i, 0)),
                  pl.BlockSpec((1, d), lambda i: (0, 0)),
                  pl.BlockSpec((d, c), lambda i: (0, 0))],
        out_specs=pl.BlockSpec((tm, c), lambda i: (i, 0)),
        out_shape=jax.ShapeDtypeStruct((n, c), F32),
        compiler_params=_cparams(("parallel",), 48),
        name="norm_in_proj",
    )(x, g.reshape(1, d), w)


def _out_proj_kernel(x_ref, ya_ref, yb_ref, ym_ref, w_ref, o_ref):
    wa = ya_ref.shape[1]
    wb = yb_ref.shape[1]
    acc = jnp.dot(ya_ref[...], w_ref[0:wa, :], preferred_element_type=F32)
    acc += jnp.dot(yb_ref[...], w_ref[wa:wa + wb, :], preferred_element_type=F32)
    acc += jnp.dot(ym_ref[...], w_ref[wa + wb:, :], preferred_element_type=F32)
    o_ref[...] = x_ref[...] + acc


def _out_proj(x, ya, yb, ym, w, tm):
    n, d = x.shape
    return pl.pallas_call(
        _out_proj_kernel,
        grid=(n // tm,),
        in_specs=[pl.BlockSpec((tm, d), lambda i: (i, 0)),
                  pl.BlockSpec((tm, ya.shape[1]), lambda i: (i, 0)),
                  pl.BlockSpec((tm, yb.shape[1]), lambda i: (i, 0)),
                  pl.BlockSpec((tm, ym.shape[1]), lambda i: (i, 0)),
                  pl.BlockSpec(w.shape, lambda i: (0, 0))],
        out_specs=pl.BlockSpec((tm, d), lambda i: (i, 0)),
        out_shape=jax.ShapeDtypeStruct((n, d), F32),
        compiler_params=_cparams(("parallel",), 32),
        name="out_proj",
    )(x, ya, yb, ym, w)


def _sgu_kernel(u_ref, v_ref, g_ref, ws_ref, bs_ref, vg_ref, o_ref):
    tm = u_ref.shape[0]
    gdim = u_ref.shape[1] // A_GROUPS
    row = lax.broadcasted_iota(jnp.int32, (A_CHUNK, A_CHUNK), 0)
    col = lax.broadcasted_iota(jnp.int32, (A_CHUNK, A_CHUNK), 1)
    causal = col <= row
    for g in range(A_GROUPS):
        cols = slice(g * gdim, (g + 1) * gdim)
        gv = jax.nn.gelu(v_ref[:, cols])
        ms = jnp.mean(gv * gv, axis=-1, keepdims=True)
        vn = (gv * lax.rsqrt(ms + EPS) * vg_ref[:, cols]).astype(BF16)
        w = jnp.where(causal, ws_ref[g], 0.0).astype(BF16)
        for c in range(tm // A_CHUNK):
            rows = slice(c * A_CHUNK, (c + 1) * A_CHUNK)
            z = jnp.dot(w, vn[rows, :], preferred_element_type=F32) + bs_ref[:, cols]
            y = jax.nn.gelu(u_ref[rows, cols]) * z * jax.nn.silu(g_ref[rows, cols])
            o_ref[rows, cols] = y.astype(BF16)


def _sgu(p0, a_ws, bs_exp, vgain, tm):
    n = p0.shape[0]
    width = vgain.shape[1]
    return pl.pallas_call(
        _sgu_kernel,
        grid=(n // tm,),
        in_specs=[pl.BlockSpec((tm, width), lambda i: (i, 0)),
                  pl.BlockSpec((tm, width), lambda i: (i, 1)),
                  pl.BlockSpec((tm, width), lambda i: (i, 2)),
                  pl.BlockSpec(a_ws.shape, lambda i: (0, 0, 0)),
                  pl.BlockSpec(bs_exp.shape, lambda i: (0, 0)),
                  pl.BlockSpec(vgain.shape, lambda i: (0, 0))],
        out_specs=pl.BlockSpec((tm, width), lambda i: (i, 0)),
        out_shape=jax.ShapeDtypeStruct((n, width), BF16),
        compiler_params=_cparams(("parallel",), 32),
        name="sgu",
    )(p0, p0, p0, a_ws, bs_exp, vgain)


def _diff_prep_kernel(q_ref, k_ref, v_ref, c_ref, s1_ref, s2_ref, qg_ref, kg_ref,
                      qo_ref, ko_ref, vo_ref):
    half = B_QK_DIM // 4 // 2
    for src, gain, dst in ((q_ref, qg_ref, qo_ref), (k_ref, kg_ref, ko_ref)):
        x = src[...]
        xn = x * lax.rsqrt(_group_mean(x * x, B_QK_DIM) + EPS) * gain[...]
        dst[...] = _rope(xn, c_ref[...], s1_ref[...], s2_ref[...], half).astype(BF16)
    vo_ref[...] = v_ref[...].astype(BF16)


def _diff_prep(p0, tables, qgain, kgain, seq, tm):
    n = p0.shape[0]
    width = qgain.shape[1]
    tiles_per_seq = seq // tm
    tab_spec = pl.BlockSpec((tm, LANES), lambda i: (i % tiles_per_seq, 0))
    out = jax.ShapeDtypeStruct((n, width), BF16)
    return pl.pallas_call(
        _diff_prep_kernel,
        grid=(n // tm,),
        in_specs=[pl.BlockSpec((tm, width), lambda i: (i, 3)),
                  pl.BlockSpec((tm, width), lambda i: (i, 4)),
                  pl.BlockSpec((tm, width), lambda i: (i, 5)),
                  tab_spec, tab_spec, tab_spec,
                  pl.BlockSpec((1, width), lambda i: (0, 0)),
                  pl.BlockSpec((1, width), lambda i: (0, 0))],
        out_specs=[pl.BlockSpec((tm, width), lambda i: (i, 0))] * 3,
        out_shape=[out, out, out],
        compiler_params=_cparams(("parallel",), 32),
        name="diff_prep",
    )(p0, p0, p0, *tables, qgain, kgain)


def _diff_attn_kernel(lam_ref, q_ref, k_ref, v_ref, g_ref, sub_ref, o_ref, acc_ref,
                      *, tq, lam_init):
    qi = pl.program_id(2)
    scale = B_QK_DIM ** -0.5
    q = q_ref[...]
    lane = lax.broadcasted_iota(jnp.int32, (1, LANES), 1)
    zero = jnp.zeros_like(q)
    qm = [jnp.where(lane // B_QK_DIM == j, q, zero) for j in range(4)]
    acc_ref[...] = jnp.zeros_like(acc_ref)

    def step(ki, carry, diag):
        ms, ls = carry
        off = pl.multiple_of(ki * tq, tq)
        k = k_ref[pl.ds(off, tq), :]
        v = v_ref[pl.ds(off, tq), :]
        if diag:
            rows = lax.broadcasted_iota(jnp.int32, (tq, tq), 0)
            cols = lax.broadcasted_iota(jnp.int32, (tq, tq), 1)
            causal = cols <= rows
        new_m, new_l = [], []
        for j in range(4):
            s = lax.dot_general(qm[j], k, NT_DIMS, preferred_element_type=F32) * scale
            if diag:
                s = jnp.where(causal, s, -jnp.inf)
            m, l = _flash_update(s, ms[j], ls[j], acc_ref, j, v)
            new_m.append(m)
            new_l.append(l)
        return tuple(new_m), tuple(new_l)

    init = (tuple(jnp.full((tq, 1), -jnp.inf, F32) for _ in range(4)),
            tuple(jnp.zeros((tq, 1), F32) for _ in range(4)))
    carry = lax.fori_loop(0, qi, functools.partial(step, diag=False), init)
    ms, ls = step(qi, carry, True)

    lam_p = lam_ref[...]
    lam = (jnp.exp(jnp.sum(lam_p[0:1] * lam_p[1:2], axis=-1, keepdims=True))
           - jnp.exp(jnp.sum(lam_p[2:3] * lam_p[3:4], axis=-1, keepdims=True)) + lam_init)
    o = [acc_ref[j] / ls[j] for j in range(4)]
    first_head = lane < 2 * B_QK_DIM
    pd = jnp.where(first_head, o[0] - lam * o[1], o[2] - lam * o[3])
    ms2 = _group_mean(pd * pd, 2 * B_QK_DIM)
    ob = pd * lax.rsqrt(ms2 + EPS) * sub_ref[...] * (1.0 - lam_init)
    o_ref[...] = (ob * jax.nn.silu(g_ref[...])).astype(BF16)


def _diff_attn(lam_p, qn, kn, vb, p0, subln, batch, seq, tq, lam_init):
    n, width = qn.shape
    pairs = width // LANES
    qt = seq // tq
    g_off = 3072 // LANES
    kernel = functools.partial(_diff_attn_kernel, tq=tq, lam_init=lam_init)
    return pl.pallas_call(
        kernel,
        grid=(batch, pairs, qt),
        in_specs=[pl.BlockSpec(lam_p.shape, lambda b, p, i: (0, 0)),
                  pl.BlockSpec((tq, LANES), lambda b, p, i: (b * qt + i, p)),
                  pl.BlockSpec((seq, LANES), lambda b, p, i: (b, p)),
                  pl.BlockSpec((seq, LANES), lambda b, p, i: (b, p)),
                  pl.BlockSpec((tq, LANES), lambda b, p, i: (b * qt + i, g_off + p)),
                  pl.BlockSpec((1, LANES), lambda b, p, i: (0, 0))],
        out_specs=pl.BlockSpec((tq, LANES), lambda b, p, i: (b * qt + i, p)),
        out_shape=jax.ShapeDtypeStruct((n, width), BF16),
        scratch_shapes=[pltpu.VMEM((4, tq, LANES), F32)],
        compiler_params=_cparams(("parallel", "parallel", "parallel"), 32),
        name="diff_attn",
    )(lam_p, qn, kn, vb, p0, subln)


def _mem_kv_kernel(mem_ref, g_ref, w_ref, kg_ref, o_ref):
    x = mem_ref[0]
    ms = jnp.mean(x * x, axis=-1, keepdims=True)
    h = (x * lax.rsqrt(ms + EPS) * g_ref[...]).astype(BF16)
    kv = jnp.dot(h, w_ref[...], preferred_element_type=F32)
    mw = kv.shape[1] // 4
    for layer in range(2):
        k = kv[:, (2 * layer) * mw:(2 * layer + 1) * mw]
        kn = k * lax.rsqrt(_group_mean(k * k, HEAD_DIM) + EPS) * kg_ref[layer:layer + 1, :]
        o_ref[2 * layer, 0] = kn.astype(BF16)
        o_ref[2 * layer + 1, 0] = kv[:, (2 * layer + 1) * mw:(2 * layer + 2) * mw].astype(BF16)


def _mem_kv(mem, mem_norm, wkv_both, kgains):
    batch, mtok, d = mem.shape
    mw = wkv_both.shape[1] // 4
    return pl.pallas_call(
        _mem_kv_kernel,
        grid=(batch,),
        in_specs=[pl.BlockSpec((1, mtok, d), lambda b: (b, 0, 0)),
                  pl.BlockSpec((1, d), lambda b: (0, 0)),
                  pl.BlockSpec(wkv_both.shape, lambda b: (0, 0)),
                  pl.BlockSpec(kgains.shape, lambda b: (0, 0))],
        out_specs=pl.BlockSpec((4, 1, mtok, mw), lambda b: (0, b, 0, 0)),
        out_shape=jax.ShapeDtypeStruct((4, batch, mtok, mw), BF16),
        compiler_params=_cparams(("parallel",), 32),
        name="mem_kv",
    )(mem, mem_norm.reshape(1, d), wkv_both, kgains)


def _mem_attn_kernel(q_ref, g_ref, k_ref, v_ref, qg_ref, o_ref):
    scale = HEAD_DIM ** -0.5
    x = q_ref[...]
    width = x.shape[1]
    qn = (x * lax.rsqrt(_group_mean(x * x, HEAD_DIM) + EPS) * qg_ref[...]).astype(BF16)
    k = k_ref[0, 0]
    v = v_ref[0, 0]
    lane = lax.broadcasted_iota(jnp.int32, (1, width), 1)
    zero = jnp.zeros_like(qn)
    out = jnp.zeros(x.shape, F32)
    for h in range(width // HEAD_DIM):
        hmask = lane // HEAD_DIM == h
        s = lax.dot_general(jnp.where(hmask, qn, zero), k, NT_DIMS,
                            preferred_element_type=F32) * scale
        p = jnp.exp(s - jnp.max(s, axis=-1, keepdims=True))
        l = jnp.sum(p, axis=-1, keepdims=True)
        o = jnp.dot(p.astype(BF16), v, preferred_element_type=F32) / l
        out = jnp.where(hmask, o, out)
    o_ref[...] = (out * jax.nn.silu(g_ref[...])).astype(BF16)


def _mem_attn(p, kv4, layer, qgain, q_blk, g_blk, seq, tq):
    n = p.shape[0]
    _, batch, mtok, mw = kv4.shape
    qt = seq // tq
    return pl.pallas_call(
        _mem_attn_kernel,
        grid=(n // tq,),
        in_specs=[pl.BlockSpec((tq, mw), lambda i: (i, q_blk)),
                  pl.BlockSpec((tq, mw), lambda i: (i, g_blk)),
                  pl.BlockSpec((1, 1, mtok, mw), lambda i: (2 * layer, i // qt, 0, 0)),
                  pl.BlockSpec((1, 1, mtok, mw), lambda i: (2 * layer + 1, i // qt, 0, 0)),
                  pl.BlockSpec((1, mw), lambda i: (0, 0))],
        out_specs=pl.BlockSpec((tq, mw), lambda i: (i, 0)),
        out_shape=jax.ShapeDtypeStruct((n, mw), BF16),
        compiler_params=_cparams(("parallel",), 32),
        name="mem_attn",
    )(p, p, kv4, kv4, qgain)


def _conv_kernel(a_ref, b_ref, g_ref, w_ref, cb_ref, ng_ref, o_ref, hbuf_ref):
    t = pl.program_id(1)
    ts = a_ref.shape[0]

    @pl.when(t == 0)
    def _():
        hbuf_ref[0:HALO, :] = jnp.zeros((HALO, hbuf_ref.shape[1]), F32)

    hbuf_ref[HALO:HALO + ts, :] = a_ref[...] * jax.nn.sigmoid(b_ref[...])
    first = HALO - (C_KERNEL - 1)
    y = hbuf_ref[first:first + ts, :] * w_ref[0:1, :]
    for j in range(1, C_KERNEL):
        y += hbuf_ref[first + j:first + j + ts, :] * w_ref[j:j + 1, :]
    y += cb_ref[...]
    ms = jnp.mean(y * y, axis=-1, keepdims=True)
    yn = y * lax.rsqrt(ms + EPS) * ng_ref[...]
    o_ref[...] = (jax.nn.silu(yn) * jax.nn.silu(g_ref[...])).astype(BF16)
    hbuf_ref[0:HALO, :] = hbuf_ref[ts:ts + HALO, :]


def _conv_module(p1, conv_w, conv_b, norm_g, batch, seq, ts):
    n = p1.shape[0]
    width = conv_w.shape[1]
    st = seq // ts
    return pl.pallas_call(
        _conv_kernel,
        grid=(batch, st),
        in_specs=[pl.BlockSpec((ts, width), lambda b, t: (b * st + t, 0)),
                  pl.BlockSpec((ts, width), lambda b, t: (b * st + t, 1)),
                  pl.BlockSpec((ts, width), lambda b, t: (b * st + t, 2)),
                  pl.BlockSpec(conv_w.shape, lambda b, t: (0, 0)),
                  pl.BlockSpec((1, width), lambda b, t: (0, 0)),
                  pl.BlockSpec((1, width), lambda b, t: (0, 0))],
        out_specs=pl.BlockSpec((ts, width), lambda b, t: (b * st + t, 0)),
        out_shape=jax.ShapeDtypeStruct((n, width), BF16),
        scratch_shapes=[pltpu.VMEM((ts + HALO, width), F32)],
        compiler_params=_cparams(("arbitrary", "arbitrary"), 32),
        name="conv_module",
    )(p1, p1, p1, conv_w, conv_b.reshape(1, width), norm_g.reshape(1, width))


def _nsa_prep_kernel(q_ref, ks_ref, vs_ref, kw_ref, vw_ref, c_ref, s1_ref, s2_ref,
                     qg_ref, kg_ref, rep_ref, qo_ref, kso_ref, vso_ref, kwo_ref, vwo_ref):
    half = HEAD_DIM // 4 // 2
    tabs = (c_ref[...], s1_ref[...], s2_ref[...])
    x = q_ref[...]
    xn = x * lax.rsqrt(_group_mean(x * x, HEAD_DIM) + EPS) * qg_ref[...]
    qo_ref[...] = _rope(xn, *tabs, half).astype(BF16)
    rep = rep_ref[...]
    for src, dst in ((ks_ref, kso_ref), (kw_ref, kwo_ref)):
        k = src[...]
        kn = k * lax.rsqrt(_group_mean(k * k, HEAD_DIM) + EPS) * kg_ref[...]
        kr = _rope(kn, *tabs, half).astype(BF16)
        dst[...] = jnp.dot(kr, rep, preferred_element_type=F32).astype(BF16)
    for src, dst in ((vs_ref, vso_ref), (vw_ref, vwo_ref)):
        dst[...] = jnp.dot(src[...].astype(BF16), rep, preferred_element_type=F32).astype(BF16)


def _nsa_prep(p1, tables, qgain, kgain, rep, seq, tm, kv_blk):
    n = p1.shape[0]
    width = qgain.shape[1]
    tiles_per_seq = seq // tm
    tab_spec = pl.BlockSpec((tm, LANES), lambda i: (i % tiles_per_seq, 0))
    out = jax.ShapeDtypeStruct((n, width), BF16)

    def col(j):
        return pl.BlockSpec((tm, LANES), lambda i: (i, kv_blk + j))

    return pl.pallas_call(
        _nsa_prep_kernel,
        grid=(n // tm,),
        in_specs=[pl.BlockSpec((tm, width), lambda i: (i, 3)),
                  col(2), col(3), col(4), col(5),
                  tab_spec, tab_spec, tab_spec,
                  pl.BlockSpec((1, width), lambda i: (0, 0)),
                  pl.BlockSpec((1, LANES), lambda i: (0, 0)),
                  pl.BlockSpec(rep.shape, lambda i: (0, 0))],
        out_specs=[pl.BlockSpec((tm, width), lambda i: (i, 0))] * 5,
        out_shape=[out] * 5,
        compiler_params=_cparams(("parallel",), 32),
        name="nsa_prep",
    )(p1, p1, p1, p1, p1, *tables, qgain, kgain, rep)


def _compress_kernel(rk_ref, rv_ref, pk_ref, pv_ref, w1k_ref, w2k_ref, w1v_ref, w2v_ref,
                     kg_ref, c_ref, s1_ref, s2_ref, kc_ref, vc_ref):
    half = HEAD_DIM // 4 // 2
    ncp = rk_ref.shape[2]
    span = rk_ref.shape[3]

    def mlp(r_ref, p_ref, w1_ref, w2_ref):
        r = r_ref[0, 0]
        a = jnp.dot((r + p_ref[0:1, :]).astype(BF16), w1_ref[0:span, :],
                    preferred_element_type=F32)
        b = jnp.dot((r + p_ref[1:2, :]).astype(BF16), w1_ref[span:2 * span, :],
                    preferred_element_type=F32)
        h = a + pltpu.roll(b, ncp - 1, 0)
        return jnp.dot(jax.nn.silu(h).astype(BF16), w2_ref[...], preferred_element_type=F32)

    kc = mlp(rk_ref, pk_ref, w1k_ref, w2k_ref)
    kn = kc * lax.rsqrt(_group_mean(kc * kc, HEAD_DIM) + EPS) * kg_ref[...]
    kc_ref[0, 0] = _rope(kn, c_ref[...], s1_ref[...], s2_ref[...], half).astype(BF16)
    vc_ref[0, 0] = mlp(rv_ref, pv_ref, w1v_ref, w2v_ref).astype(BF16)


def _compress(rk, rv, pk, pv, w1k, w2k, w1v, w2v, kgain, tables):
    batch, groups, ncp, span = rk.shape
    width = w2k.shape[1]
    r_spec = pl.BlockSpec((1, 1, ncp, span), lambda b, g: (b, g, 0, 0))
    o_spec = pl.BlockSpec((1, 1, ncp, width), lambda b, g: (b, g, 0, 0))

    def full(a):
        return pl.BlockSpec(a.shape, lambda b, g: (0,) * a.ndim)

    out = jax.ShapeDtypeStruct((batch, groups, ncp, width), BF16)
    return pl.pallas_call(
        _compress_kernel,
        grid=(batch, groups),
        in_specs=[r_spec, r_spec, full(pk), full(pv), full(w1k), full(w2k), full(w1v), full(w2v),
                  full(kgain), full(tables[0]), full(tables[1]), full(tables[2])],
        out_specs=[o_spec, o_spec],
        out_shape=[out, out],
        compiler_params=_cparams(("parallel", "parallel"), 32),
        name="nsa_compress",
    )(rk, rv, pk, pv, w1k, w2k, w1v, w2v, kgain, *tables)


def _nsa_kernel(q_ref, kc_ref, vc_ref, ks_ref, vs_ref, kw_ref, vw_ref, bg_ref, dg_ref,
                ovt_ref, eblk_ref, egate_ref, o_ref, acc_ref, sel_ref, imp_ref,
                *, tq, seq, nslc, nsel):
    qi = pl.program_id(2)
    scale = HEAD_DIM ** -0.5
    rheads = D_HEADS // D_KV_GROUPS
    width = rheads * HEAD_DIM
    tiny = float(np.finfo(np.float32).tiny)
    q = q_ref[...]
    lane = lax.broadcasted_iota(jnp.int32, (1, width), 1)
    hmask = [lane // HEAD_DIM == r for r in range(rheads)]
    zero = jnp.zeros_like(q)
    qm = [jnp.where(hmask[r], q, zero) for r in range(rheads)]
    qpos = qi * tq + lax.broadcasted_iota(jnp.int32, (tq, 1), 0)

    kc = kc_ref[0, 0]
    vc = vc_ref[0, 0]
    ncp = kc.shape[0]
    cend = lax.broadcasted_iota(jnp.int32, (1, ncp), 1) * CMP_STRIDE + (CMP_BLOCK - 1)
    cvalid = cend <= qpos
    psum = jnp.zeros((tq, ncp), F32)
    o_c = jnp.zeros((tq, width), F32)
    for r in range(rheads):
        s = lax.dot_general(qm[r], kc, NT_DIMS, preferred_element_type=F32) * scale
        s = jnp.where(cvalid, s, -jnp.inf)
        m = jnp.max(s, axis=-1, keepdims=True)
        m = jnp.where(jnp.isfinite(m), m, 0.0)
        e = jnp.where(cvalid, jnp.exp(s - m), 0.0)
        p = e / jnp.maximum(jnp.sum(e, axis=-1, keepdims=True), tiny)
        psum += p
        o_c = jnp.where(hmask[r], jnp.dot(p.astype(BF16), vc, preferred_element_type=F32), o_c)

    hi, lo = _split_bf16(psum)
    ovt = ovt_ref[...]
    imp = (lax.dot_general(ovt, hi, NT_DIMS, preferred_element_type=F32)
           + lax.dot_general(ovt, lo, NT_DIMS, preferred_element_type=F32))
    blk = lax.broadcasted_iota(jnp.int32, (NSLC_PAD, tq), 0)
    cur = (qi * tq + lax.broadcasted_iota(jnp.int32, (NSLC_PAD, tq), 1)) // SLC_BLOCK
    imp = jnp.where(blk > cur, -jnp.inf, imp)
    forced = (blk == 0) | (blk == cur) | (blk == cur - 1)
    imp = jnp.where(forced, jnp.inf, imp)
    imp_ref[...] = imp
    rank = jnp.zeros((NSLC_PAD, tq), F32)
    for jp in range(nslc):
        row = imp_ref[jp:jp + 1, :]
        beats = (row > imp) | ((row == imp) & (blk > jp))
        rank += jnp.where(beats, 1.0, 0.0)
    sel_t = jnp.where(rank < nsel, 1.0, 0.0)
    sel = sel_t.T.astype(BF16)
    for c in range(seq // tq):
        sel_ref[c] = jnp.dot(sel, eblk_ref[:, c * tq:(c + 1) * tq], preferred_element_type=F32)

    acc_ref[...] = jnp.zeros_like(acc_ref)
    rows = lax.broadcasted_iota(jnp.int32, (tq, tq), 0)
    cols = lax.broadcasted_iota(jnp.int32, (tq, tq), 1)

    def init():
        return (tuple(jnp.full((tq, 1), -jnp.inf, F32) for _ in range(rheads)),
                tuple(jnp.zeros((tq, 1), F32) for _ in range(rheads)))

    def attend(k, v, valid, carry, base):
        ms, ls = carry
        new_m, new_l = [], []
        for r in range(rheads):
            s = lax.dot_general(qm[r], k, NT_DIMS, preferred_element_type=F32) * scale
            if valid is not None:
                s = jnp.where(valid, s, -jnp.inf)
            m, l = _flash_update(s, ms[r], ls[r], acc_ref, base + r, v)
            new_m.append(m)
            new_l.append(l)
        return tuple(new_m), tuple(new_l)

    def sel_step(ki, carry, diag):
        off = pl.multiple_of(ki * tq, tq)
        valid = sel_ref[ki] > 0.5
        if diag:
            valid = valid & (cols <= rows)
        return attend(ks_ref[pl.ds(off, tq), :], vs_ref[pl.ds(off, tq), :], valid, carry, 0)

    carry = lax.fori_loop(0, qi, functools.partial(sel_step, diag=False), init())
    _, l_s = sel_step(qi, carry, True)

    def win_step(ki, carry):
        off = pl.multiple_of(ki * tq, tq)
        dpos = (qi - ki) * tq + rows - cols
        valid = (dpos >= 0) & (dpos < WINDOW)
        return attend(kw_ref[pl.ds(off, tq), :], vw_ref[pl.ds(off, tq), :], valid, carry, rheads)

    first = jnp.maximum(qi - (WINDOW + tq - 1) // tq, 0)
    _, l_w = lax.fori_loop(first, qi + 1, win_step, init())

    o_s = jnp.zeros((tq, width), F32)
    o_w = jnp.zeros((tq, width), F32)
    for r in range(rheads):
        o_s = jnp.where(hmask[r], acc_ref[r] / jnp.maximum(l_s[r], tiny), o_s)
        o_w = jnp.where(hmask[r], acc_ref[rheads + r] / jnp.maximum(l_w[r], tiny), o_w)

    ghi, glo = _split_bf16(jax.nn.sigmoid(bg_ref[...]))
    gates = [jnp.dot(ghi, egate_ref[0, j], preferred_element_type=F32)
             + jnp.dot(glo, egate_ref[0, j], preferred_element_type=F32) for j in range(3)]
    out = gates[0] * o_c + gates[1] * o_s + gates[2] * o_w
    o_ref[...] = (out * jax.nn.silu(dg_ref[...])).astype(BF16)


def _nsa_attn(qn, kc, vc, ks, vs, kw, vw, p1, ovt, eblk, egate, batch, seq, tq, bg_blk, dg_blk):
    n = qn.shape[0]
    groups = D_KV_GROUPS
    width = qn.shape[1] // groups
    qt = seq // tq
    ncp = kc.shape[2]
    nslc = seq // SLC_BLOCK
    kernel = functools.partial(_nsa_kernel, tq=tq, seq=seq, nslc=nslc, nsel=min(SLC_TOPK, nslc))
    seq_spec = pl.BlockSpec((seq, width), lambda b, g, i: (b, g))
    cmp_spec = pl.BlockSpec((1, 1, ncp, width), lambda b, g, i: (b, g, 0, 0))
    return pl.pallas_call(
        kernel,
        grid=(batch, groups, qt),
        in_specs=[pl.BlockSpec((tq, width), lambda b, g, i: (b * qt + i, g)),
                  cmp_spec, cmp_spec, seq_spec, seq_spec, seq_spec, seq_spec,
                  pl.BlockSpec((tq, LANES), lambda b, g, i: (b * qt + i, bg_blk)),
                  pl.BlockSpec((tq, width), lambda b, g, i: (b * qt + i, dg_blk + g)),
                  pl.BlockSpec(ovt.shape, lambda b, g, i: (0, 0)),
                  pl.BlockSpec(eblk.shape, lambda b, g, i: (0, 0)),
                  pl.BlockSpec((1,) + egate.shape[1:], lambda b, g, i: (g, 0, 0, 0))],
        out_specs=pl.BlockSpec((tq, width), lambda b, g, i: (b * qt + i, g)),
        out_shape=jax.ShapeDtypeStruct((n, qn.shape[1]), BF16),
        scratch_shapes=[pltpu.VMEM((2 * (D_HEADS // D_KV_GROUPS), tq, width), F32),
                        pltpu.VMEM((qt, tq, tq), F32),
                        pltpu.VMEM((NSLC_PAD, tq), F32)],
        compiler_params=_cparams(("parallel", "parallel", "parallel"), 56),
        name="nsa_attn",
    )(qn, kc, vc, ks, vs, kw, vw, p1, p1, ovt, eblk, egate)


def _rope_tables(pos, d):
    rd = d // 4
    half = rd // 2
    inv = ROPE_THETA ** (-jnp.arange(half, dtype=F32) / half)
    ang = pos.astype(F32)[:, None] * inv[None, :]
    cos, sin = jnp.cos(ang), jnp.sin(ang)
    npos = pos.shape[0]
    zeros = jnp.zeros((npos, d), F32)
    c = jnp.concatenate([cos, cos, jnp.ones((npos, d - rd), F32)], axis=1)
    s1 = zeros.at[:, :half].set(-sin)
    s2 = zeros.at[:, half:rd].set(sin)
    return tuple(jnp.tile(t, (1, LANES // d)) for t in (c, s1, s2))


def _overlap_t(ncp, nslc):
    start = np.arange(ncp) * CMP_STRIDE
    s0 = np.arange(nslc) * SLC_BLOCK
    lo = np.maximum(start[:, None], s0[None, :])
    hi = np.minimum(start[:, None] + CMP_BLOCK, s0[None, :] + SLC_BLOCK)
    ov = np.clip(hi - lo, 0, None) / CMP_BLOCK
    ov[ncp - 1] = 0.0
    out = np.zeros((NSLC_PAD, ncp), np.float32)
    out[:nslc] = ov.T
    return jnp.asarray(out, BF16)


def kernel(x, mem, mem_norm, l0_norm, l0_w_in, l0_a_vnorm, l0_a_ws, l0_a_bs, l0_b_qnorm, l0_b_knorm, l0_b_lq1, l0_b_lk1, l0_b_lq2, l0_b_lk2, l0_b_subln, l0_m_wkv, l0_m_qnorm, l0_m_knorm, l0_w_out, l1_norm, l1_w_in, l1_c_conv_w, l1_c_conv_b, l1_c_norm, l1_d_qnorm, l1_d_knorm, l1_d_cmp_pos_k, l1_d_cmp_w1_k, l1_d_cmp_w2_k, l1_d_cmp_pos_v, l1_d_cmp_w1_v, l1_d_cmp_w2_v, l1_m_wkv, l1_m_qnorm, l1_m_knorm, l1_w_out):
    batch, seq, d_model = x.shape
    n = batch * seq
    tm = 256
    tq = 256
    mw = M_HEADS * HEAD_DIM
    pos = jnp.arange(seq, dtype=jnp.int32)
    x0 = x.reshape(n, d_model)

    wkv_both = jnp.concatenate([l0_m_wkv, l1_m_wkv], axis=1).astype(BF16)
    kgains = jnp.stack([jnp.tile(l0_m_knorm, M_HEADS), jnp.tile(l1_m_knorm, M_HEADS)])
    kv4 = _mem_kv(mem, mem_norm, wkv_both, kgains)

    p0 = _norm_matmul(x0, l0_norm, l0_w_in.astype(BF16), tm)
    a_width = l0_a_vnorm.shape[0]
    bs_exp = jnp.repeat(l0_a_bs.T, a_width // A_GROUPS, axis=1)
    y_a = _sgu(p0, l0_a_ws, bs_exp, l0_a_vnorm.reshape(1, a_width), 512)

    tab32 = _rope_tables(pos, B_QK_DIM)
    b_width = 2 * B_HEADS * B_QK_DIM
    qn, kn, vb = _diff_prep(p0, tab32,
                            jnp.tile(l0_b_qnorm, b_width // B_QK_DIM).reshape(1, b_width),
                            jnp.tile(l0_b_knorm, b_width // B_QK_DIM).reshape(1, b_width), seq, tm)
    lam_init = 0.8 - 0.6 * math.exp(-0.3 * 1)
    lam_p = jnp.stack([l0_b_lq1, l0_b_lk1, l0_b_lq2, l0_b_lk2])
    y_b = _diff_attn(lam_p, qn, kn, vb, p0, jnp.tile(l0_b_subln, 2).reshape(1, LANES),
                     batch, seq, tq, lam_init)
    y_m = _mem_attn(p0, kv4, 0, jnp.tile(l0_m_qnorm, M_HEADS).reshape(1, mw),
                    3584 // mw, 3840 // mw, seq, tq)
    x1 = _out_proj(x0, y_a, y_b, y_m, l0_w_out.astype(BF16), tm)

    w = l1_w_in
    n_bg = 3 * D_HEADS
    w1p = jnp.concatenate([w[:, 0:2048], w[:, 2840:3352], w[:, 2048:2816], w[:, 3352:3864],
                           w[:, 2816:2840], jnp.zeros((d_model, LANES - n_bg), F32)], axis=1)
    p1 = _norm_matmul(x1, l1_norm, w1p.astype(BF16), tm)
    y_c = _conv_module(p1, l1_c_conv_w, l1_c_conv_b, l1_c_norm, batch, seq, 512)

    groups, rheads = D_KV_GROUPS, D_HEADS // D_KV_GROUPS
    d_width = D_HEADS * HEAD_DIM
    g_width = rheads * HEAD_DIM
    kv_blk = 2560 // LANES
    rep = np.zeros((LANES, d_width), np.float32)
    for g in range(groups):
        for r in range(rheads):
            for dd in range(HEAD_DIM):
                rep[g * HEAD_DIM + dd, g * g_width + r * HEAD_DIM + dd] = 1.0
    tab64 = _rope_tables(pos, HEAD_DIM)
    dqn, ks, vs, kw, vw = _nsa_prep(
        p1, tab64, jnp.tile(l1_d_qnorm, D_HEADS).reshape(1, d_width),
        jnp.tile(l1_d_knorm, LANES // HEAD_DIM).reshape(1, LANES), jnp.asarray(rep, BF16),
        seq, tm, kv_blk)

    ncp = seq // CMP_STRIDE
    span = CMP_STRIDE * HEAD_DIM

    def regroup(cols):
        t = p1[:, cols:cols + LANES].reshape(batch, seq, groups, HEAD_DIM)
        return t.transpose(0, 2, 1, 3).reshape(batch, groups, ncp, span)

    cmp_pos = jnp.arange(ncp, dtype=jnp.int32) * CMP_STRIDE + (CMP_BLOCK - 1)
    kc, vc = _compress(
        regroup(2560), regroup(2560 + LANES),
        l1_d_cmp_pos_k.reshape(2, span), l1_d_cmp_pos_v.reshape(2, span),
        l1_d_cmp_w1_k.astype(BF16), jnp.tile(l1_d_cmp_w2_k, (1, rheads)).astype(BF16),
        l1_d_cmp_w1_v.astype(BF16), jnp.tile(l1_d_cmp_w2_v, (1, rheads)).astype(BF16),
        jnp.tile(l1_d_knorm, rheads).reshape(1, g_width), _rope_tables(cmp_pos, HEAD_DIM))

    nslc = seq // SLC_BLOCK
    eblk = np.zeros((NSLC_PAD, seq), np.float32)
    eblk[np.arange(seq) // SLC_BLOCK, np.arange(seq)] = 1.0
    egate = np.zeros((groups, 3, LANES, g_width), np.float32)
    for g in range(groups):
        for j in range(3):
            for r in range(rheads):
                egate[g, j, (g * rheads + r) * 3 + j, r * HEAD_DIM:(r + 1) * HEAD_DIM] = 1.0
    y_d = _nsa_attn(dqn, kc, vc, ks, vs, kw, vw, p1, _overlap_t(ncp, nslc),
                    jnp.asarray(eblk, BF16), jnp.asarray(egate, BF16),
                    batch, seq, tq, 3840 // LANES, 2048 // g_width)
    y_m1 = _mem_attn(p1, kv4, 1, jnp.tile(l1_m_qnorm, M_HEADS).reshape(1, mw),
                     3328 // mw, 3584 // mw, seq, tq)
    out = _out_proj(x1, y_c, y_d, y_m1, l1_w_out.astype(BF16), tm)
    return out.reshape(batch, seq, d_model)
```

```python
import functools
import math

import numpy as np
import jax
import jax.numpy as jnp
from jax import lax
from jax.experimental import pallas as pl
from jax.experimental.pallas import tpu as pltpu

F32 = jnp.float32
BF16 = jnp.bfloat16

EPS = 1e-6
ROPE_THETA = 500000.0
HEAD_DIM = 64
A_CHUNK = 128
A_GROUPS = 4
B_HEADS = 8
B_QK_DIM = 32
C_KERNEL = 31
D_HEADS = 8
D_KV_GROUPS = 2
CMP_BLOCK = 32
CMP_STRIDE = 16
SLC_BLOCK = 64
SLC_TOPK = 16
WINDOW = 512
M_HEADS = 4

LANES = 128
MXU_DIM = 256
HALO = 32
NSLC_PAD = 128

NT_DIMS = (((1,), (1,)), ((), ()))
LOG2E = math.log2(math.e)


def _cparams(semantics, vmem_mb):
    return pltpu.CompilerParams(dimension_semantics=semantics,
                                vmem_limit_bytes=vmem_mb * 1024 * 1024)


def _split_bf16(x):
    hi = x.astype(BF16)
    lo = (x - hi.astype(F32)).astype(BF16)
    return hi, lo


def _group_mean(x2, gsize):
    t, c = x2.shape
    w = min(c, MXU_DIM)
    r = lax.broadcasted_iota(jnp.int32, (w, w), 0) // gsize
    cc = lax.broadcasted_iota(jnp.int32, (w, w), 1) // gsize
    ones = jnp.where(r == cc, 1.0, 0.0).astype(BF16)
    outs = []
    for s in range(c // w):
        hi, lo = _split_bf16(x2[:, s * w:(s + 1) * w])
        outs.append(jnp.dot(hi, ones, preferred_element_type=F32)
                    + jnp.dot(lo, ones, preferred_element_type=F32))
    out = outs[0] if len(outs) == 1 else jnp.concatenate(outs, axis=1)
    return out * (1.0 / gsize)


def _tile_lanes(t, width):
    rep = width // t.shape[1]
    return t if rep == 1 else jnp.concatenate([t] * rep, axis=1)


def _rope(x, cos_t, s1_t, s2_t, half):
    width = x.shape[1]
    up = pltpu.roll(x, width - half, 1)
    dn = pltpu.roll(x, half, 1)
    return (x * _tile_lanes(cos_t, width) + up * _tile_lanes(s1_t, width)
            + dn * _tile_lanes(s2_t, width))


def _scores_t(k, q, valid, st_ref, idx):
    st = lax.dot_general(k, q, NT_DIMS, preferred_element_type=F32)
    if valid is not None:
        st = jnp.where(valid, st, -jnp.inf)
    st_ref[idx] = st
    return jnp.max(st, axis=0, keepdims=True)


def _accumulate_t(st_ref, idx, cmax, m, l, acc_ref, aidx, vt, guard_empty):
    m_new = jnp.maximum(m, cmax)
    m_use = jnp.where(m_new == -jnp.inf, 0.0, m_new) if guard_empty else m_new
    alpha = jnp.exp2(m - m_use)
    pt = jnp.exp2(st_ref[idx] - m_use)
    l_new = alpha * l + jnp.sum(pt, axis=0, keepdims=True)
    acc_ref[aidx] = alpha * acc_ref[aidx] + jnp.dot(vt, pt.astype(BF16), preferred_element_type=F32)
    return m_new, l_new


def _norm_matmul_kernel(x_ref, g_ref, w_ref, o_ref):
    x = x_ref[...]
    ms = jnp.mean(x * x, axis=-1, keepdims=True)
    h = (x * lax.rsqrt(ms + EPS) * g_ref[...]).astype(BF16)
    o_ref[...] = jnp.dot(h, w_ref[...], preferred_element_type=F32)


def _norm_matmul(x, g, w, tm):
    n, d = x.shape
    c = w.shape[1]
    return pl.pallas_call(
        _norm_matmul_kernel,
        grid=(n // tm,),
        in_specs=[pl.BlockSpec((tm, d), lambda i: (i, 0)),
                  pl.BlockSpec((1, d), lambda i: (0, 0)),
                  pl.BlockSpec((d, c), lambda i: (0, 0))],
        out_specs=pl.BlockSpec((tm, c), lambda i: (i, 0)),
        out_shape=jax.ShapeDtypeStruct((n, c), F32),
        compiler_params=_cparams(("parallel",), 48),
        name="norm_in_proj",
    )(x, g.reshape(1, d), w)


def _out_proj_kernel(x_ref, ya_ref, yb_ref, ym_ref, w_ref, o_ref):
    wa = ya_ref.shape[1]
    wb = yb_ref.shape[1]
    acc = jnp.dot(ya_ref[...], w_ref[0:wa, :], preferred_element_type=F32)
    acc += jnp.dot(yb_ref[...], w_ref[wa:wa + wb, :], preferred_element_type=F32)
    acc += jnp.dot(ym_ref[...], w_ref[wa + wb:, :], preferred_element_type=F32)
    o_ref[...] = x_ref[...] + acc


def _out_proj(x, ya, yb, ym, w, tm):
    n, d = x.shape
    return pl.pallas_call(
        _out_proj_kernel,
        grid=(n // tm,),
        in_specs=[pl.BlockSpec((tm, d), lambda i: (i, 0)),
                  pl.BlockSpec((tm, ya.shape[1]), lambda i: (i, 0)),
                  pl.BlockSpec((tm, yb.shape[1]), lambda i: (i, 0)),
                  pl.BlockSpec((tm, ym.shape[1]), lambda i: (i, 0)),
                  pl.BlockSpec(w.shape, lambda i: (0, 0))],
        out_specs=pl.BlockSpec((tm, d), lambda i: (i, 0)),
        out_shape=jax.ShapeDtypeStruct((n, d), F32),
        compiler_params=_cparams(("parallel",), 32),
        name="out_proj",
    )(x, ya, yb, ym, w)


def _sgu_kernel(u_ref, v_ref, g_ref, ws_ref, bs_ref, vg_ref, o_ref):
    tm = u_ref.shape[0]
    gdim = u_ref.shape[1] // A_GROUPS
    row = lax.broadcasted_iota(jnp.int32, (A_CHUNK, A_CHUNK), 0)
    col = lax.broadcasted_iota(jnp.int32, (A_CHUNK, A_CHUNK), 1)
    causal = col <= row
    for g in range(A_GROUPS):
        cols = slice(g * gdim, (g + 1) * gdim)
        gv = jax.nn.gelu(v_ref[:, cols])
        ms = jnp.mean(gv * gv, axis=-1, keepdims=True)
        vn = (gv * lax.rsqrt(ms + EPS) * vg_ref[:, cols]).astype(BF16)
        w = jnp.where(causal, ws_ref[g], 0.0).astype(BF16)
        for c in range(tm // A_CHUNK):
            rows = slice(c * A_CHUNK, (c + 1) * A_CHUNK)
            z = jnp.dot(w, vn[rows, :], preferred_element_type=F32) + bs_ref[:, cols]
            y = jax.nn.gelu(u_ref[rows, cols]) * z * jax.nn.silu(g_ref[rows, cols])
            o_ref[rows, cols] = y.astype(BF16)


def _sgu(p0, a_ws, bs_exp, vgain, tm):
    n = p0.shape[0]
    width = vgain.shape[1]
    return pl.pallas_call(
        _sgu_kernel,
        grid=(n // tm,),
        in_specs=[pl.BlockSpec((tm, width), lambda i: (i, 0)),
                  pl.BlockSpec((tm, width), lambda i: (i, 1)),
                  pl.BlockSpec((tm, width), lambda i: (i, 2)),
                  pl.BlockSpec(a_ws.shape, lambda i: (0, 0, 0)),
                  pl.BlockSpec(bs_exp.shape, lambda i: (0, 0)),
                  pl.BlockSpec(vgain.shape, lambda i: (0, 0))],
        out_specs=pl.BlockSpec((tm, width), lambda i: (i, 0)),
        out_shape=jax.ShapeDtypeStruct((n, width), BF16),
        compiler_params=_cparams(("parallel",), 32),
        name="sgu",
    )(p0, p0, p0, a_ws, bs_exp, vgain)


def _diff_prep_kernel(q_ref, k_ref, v_ref, c_ref, s1_ref, s2_ref, qg_ref, kg_ref,
                      qo_ref, ko_ref, vto_ref):
    half = B_QK_DIM // 4 // 2
    fold = (B_QK_DIM ** -0.5 * LOG2E, 1.0)
    for src, gain, dst, mul in ((q_ref, qg_ref, qo_ref, fold[0]), (k_ref, kg_ref, ko_ref, fold[1])):
        x = src[...]
        xn = x * lax.rsqrt(_group_mean(x * x, B_QK_DIM) + EPS) * gain[...]
        xr = _rope(xn, c_ref[...], s1_ref[...], s2_ref[...], half)
        dst[...] = (xr * mul if mul != 1.0 else xr).astype(BF16)
    vto_ref[0, 0] = v_ref[...].T.astype(BF16)


def _diff_prep(p0, tables, qgain, kgain, batch, seq, tm):
    n = p0.shape[0]
    width = qgain.shape[1]
    tiles_per_seq = seq // tm
    tab_spec = pl.BlockSpec((tm, LANES), lambda i: (i % tiles_per_seq, 0))
    out = jax.ShapeDtypeStruct((n, width), BF16)
    out_t = jax.ShapeDtypeStruct((batch, tiles_per_seq, width, tm), BF16)
    return pl.pallas_call(
        _diff_prep_kernel,
        grid=(n // tm,),
        in_specs=[pl.BlockSpec((tm, width), lambda i: (i, 3)),
                  pl.BlockSpec((tm, width), lambda i: (i, 4)),
                  pl.BlockSpec((tm, width), lambda i: (i, 5)),
                  tab_spec, tab_spec, tab_spec,
                  pl.BlockSpec((1, width), lambda i: (0, 0)),
                  pl.BlockSpec((1, width), lambda i: (0, 0))],
        out_specs=[pl.BlockSpec((tm, width), lambda i: (i, 0)),
                   pl.BlockSpec((tm, width), lambda i: (i, 0)),
                   pl.BlockSpec((1, 1, width, tm),
                                lambda i: (i // tiles_per_seq, i % tiles_per_seq, 0, 0))],
        out_shape=[out, out, out_t],
        compiler_params=_cparams(("parallel",), 32),
        name="diff_prep",
    )(p0, p0, p0, *tables, qgain, kgain)


def _diff_attn_kernel(lam_ref, q_ref, k_ref, vt_ref, g_ref, sub_ref, o_ref, acc_ref, st_ref,
                      *, tq, lam_init):
    qi = pl.program_id(2)
    dv = 2 * B_QK_DIM
    q = q_ref[...]
    lane = lax.broadcasted_iota(jnp.int32, (1, LANES), 1)
    zero = jnp.zeros_like(q)
    qm = [jnp.where(lane // B_QK_DIM == j, q, zero) for j in range(4)]
    acc_ref[...] = jnp.zeros_like(acc_ref)

    def step(ki, carry, diag):
        ms, ls = carry
        off = pl.multiple_of(ki * tq, tq)
        k = k_ref[pl.ds(off, tq), :]
        vt = vt_ref[0, ki]
        if diag:
            krow = lax.broadcasted_iota(jnp.int32, (tq, tq), 0)
            qcol = lax.broadcasted_iota(jnp.int32, (tq, tq), 1)
            causal = krow <= qcol
        cmax = [_scores_t(k, qm[j], causal if diag else None, st_ref, j) for j in range(4)]
        out = [_accumulate_t(st_ref, j, cmax[j], ms[j], ls[j], acc_ref, j, vt, False)
               for j in range(4)]
        return tuple(o[0] for o in out), tuple(o[1] for o in out)

    init = (tuple(jnp.full((1, tq), -jnp.inf, F32) for _ in range(4)),
            tuple(jnp.zeros((1, tq), F32) for _ in range(4)))
    carry = lax.fori_loop(0, qi, functools.partial(step, diag=False), init)
    _, ls = step(qi, carry, True)

    lam_p = lam_ref[...]
    lam = (jnp.exp(jnp.sum(lam_p[0:1] * lam_p[1:2], axis=-1, keepdims=True))
           - jnp.exp(jnp.sum(lam_p[2:3] * lam_p[3:4], axis=-1, keepdims=True)) + lam_init)
    halves = []
    for h in range(2):
        rows = slice(h * dv, (h + 1) * dv)
        pd = acc_ref[2 * h, rows, :] / ls[2 * h] - lam * (acc_ref[2 * h + 1, rows, :] / ls[2 * h + 1])
        ms2 = jnp.mean(pd * pd, axis=0, keepdims=True)
        halves.append(pd * lax.rsqrt(ms2 + EPS))
    ob = jnp.concatenate(halves, axis=0).T * sub_ref[...] * (1.0 - lam_init)
    o_ref[...] = (ob * jax.nn.silu(g_ref[...])).astype(BF16)


def _diff_attn(lam_p, qn, kn, vt, p0, subln, batch, seq, tq, lam_init):
    n, width = qn.shape
    pairs = width // LANES
    qt = seq // tq
    g_off = 3072 // LANES
    kernel = functools.partial(_diff_attn_kernel, tq=tq, lam_init=lam_init)
    return pl.pallas_call(
        kernel,
        grid=(batch, pairs, qt),
        in_specs=[pl.BlockSpec(lam_p.shape, lambda b, p, i: (0, 0)),
                  pl.BlockSpec((tq, LANES), lambda b, p, i: (b * qt + i, p)),
                  pl.BlockSpec((seq, LANES), lambda b, p, i: (b, p)),
                  pl.BlockSpec((1, qt, LANES, tq), lambda b, p, i: (b, 0, p, 0)),
                  pl.BlockSpec((tq, LANES), lambda b, p, i: (b * qt + i, g_off + p)),
                  pl.BlockSpec((1, LANES), lambda b, p, i: (0, 0))],
        out_specs=pl.BlockSpec((tq, LANES), lambda b, p, i: (b * qt + i, p)),
        out_shape=jax.ShapeDtypeStruct((n, width), BF16),
        scratch_shapes=[pltpu.VMEM((4, LANES, tq), F32), pltpu.VMEM((4, tq, tq), F32)],
        compiler_params=_cparams(("parallel", "parallel", "parallel"), 32),
        name="diff_attn",
    )(lam_p, qn, kn, vt, p0, subln)


def _mem_kv_kernel(mem_ref, g_ref, w_ref, kg_ref, o_ref):
    x = mem_ref[0]
    ms = jnp.mean(x * x, axis=-1, keepdims=True)
    h = (x * lax.rsqrt(ms + EPS) * g_ref[...]).astype(BF16)
    kv = jnp.dot(h, w_ref[...], preferred_element_type=F32)
    mw = kv.shape[1] // 4
    for layer in range(2):
        k = kv[:, (2 * layer) * mw:(2 * layer + 1) * mw]
        kn = k * lax.rsqrt(_group_mean(k * k, HEAD_DIM) + EPS) * kg_ref[layer:layer + 1, :]
        o_ref[2 * layer, 0] = kn.astype(BF16)
        o_ref[2 * layer + 1, 0] = kv[:, (2 * layer + 1) * mw:(2 * layer + 2) * mw].astype(BF16)


def _mem_kv(mem, mem_norm, wkv_both, kgains):
    batch, mtok, d = mem.shape
    mw = wkv_both.shape[1] // 4
    return pl.pallas_call(
        _mem_kv_kernel,
        grid=(batch,),
        in_specs=[pl.BlockSpec((1, mtok, d), lambda b: (b, 0, 0)),
                  pl.BlockSpec((1, d), lambda b: (0, 0)),
                  pl.BlockSpec(wkv_both.shape, lambda b: (0, 0)),
                  pl.BlockSpec(kgains.shape, lambda b: (0, 0))],
        out_specs=pl.BlockSpec((4, 1, mtok, mw), lambda b: (0, b, 0, 0)),
        out_shape=jax.ShapeDtypeStruct((4, batch, mtok, mw), BF16),
        compiler_params=_cparams(("parallel",), 32),
        name="mem_kv",
    )(mem, mem_norm.reshape(1, d), wkv_both, kgains)


def _mem_attn_kernel(q_ref, g_ref, k_ref, v_ref, qg_ref, o_ref):
    scale = HEAD_DIM ** -0.5
    x = q_ref[...]
    width = x.shape[1]
    qn = (x * lax.rsqrt(_group_mean(x * x, HEAD_DIM) + EPS) * qg_ref[...]).astype(BF16)
    k = k_ref[0, 0]
    v = v_ref[0, 0]
    lane = lax.broadcasted_iota(jnp.int32, (1, width), 1)
    zero = jnp.zeros_like(qn)
    out = jnp.zeros(x.shape, F32)
    for h in range(width // HEAD_DIM):
        hmask = lane // HEAD_DIM == h
        s = lax.dot_general(jnp.where(hmask, qn, zero), k, NT_DIMS,
                            preferred_element_type=F32) * scale
        p = jnp.exp(s - jnp.max(s, axis=-1, keepdims=True))
        l = jnp.sum(p, axis=-1, keepdims=True)
        o = jnp.dot(p.astype(BF16), v, preferred_element_type=F32) / l
        out = jnp.where(hmask, o, out)
    o_ref[...] = (out * jax.nn.silu(g_ref[...])).astype(BF16)


def _mem_attn(p, kv4, layer, qgain, q_blk, g_blk, seq, tq):
    n = p.shape[0]
    _, batch, mtok, mw = kv4.shape
    qt = seq // tq
    return pl.pallas_call(
        _mem_attn_kernel,
        grid=(n // tq,),
        in_specs=[pl.BlockSpec((tq, mw), lambda i: (i, q_blk)),
                  pl.BlockSpec((tq, mw), lambda i: (i, g_blk)),
                  pl.BlockSpec((1, 1, mtok, mw), lambda i: (2 * layer, i // qt, 0, 0)),
                  pl.BlockSpec((1, 1, mtok, mw), lambda i: (2 * layer + 1, i // qt, 0, 0)),
                  pl.BlockSpec((1, mw), lambda i: (0, 0))],
        out_specs=pl.BlockSpec((tq, mw), lambda i: (i, 0)),
        out_shape=jax.ShapeDtypeStruct((n, mw), BF16),
        compiler_params=_cparams(("parallel",), 32),
        name="mem_attn",
    )(p, p, kv4, kv4, qgain)


def _conv_kernel(a_ref, b_ref, g_ref, w_ref, cb_ref, ng_ref, o_ref, hbuf_ref):
    t = pl.program_id(1)
    ts = a_ref.shape[0]

    @pl.when(t == 0)
    def _():
        hbuf_ref[0:HALO, :] = jnp.zeros((HALO, hbuf_ref.shape[1]), F32)

    hbuf_ref[HALO:HALO + ts, :] = a_ref[...] * jax.nn.sigmoid(b_ref[...])
    first = HALO - (C_KERNEL - 1)
    y = hbuf_ref[first:first + ts, :] * w_ref[0:1, :]
    for j in range(1, C_KERNEL):
        y += hbuf_ref[first + j:first + j + ts, :] * w_ref[j:j + 1, :]
    y += cb_ref[...]
    ms = jnp.mean(y * y, axis=-1, keepdims=True)
    yn = y * lax.rsqrt(ms + EPS) * ng_ref[...]
    o_ref[...] = (jax.nn.silu(yn) * jax.nn.silu(g_ref[...])).astype(BF16)
    hbuf_ref[0:HALO, :] = hbuf_ref[ts:ts + HALO, :]


def _conv_module(p1, conv_w, conv_b, norm_g, batch, seq, ts):
    n = p1.shape[0]
    width = conv_w.shape[1]
    st = seq // ts
    return pl.pallas_call(
        _conv_kernel,
        grid=(batch, st),
        in_specs=[pl.BlockSpec((ts, width), lambda b, t: (b * st + t, 0)),
                  pl.BlockSpec((ts, width), lambda b, t: (b * st + t, 1)),
                  pl.BlockSpec((ts, width), lambda b, t: (b * st + t, 2)),
                  pl.BlockSpec(conv_w.shape, lambda b, t: (0, 0)),
                  pl.BlockSpec((1, width), lambda b, t: (0, 0)),
                  pl.BlockSpec((1, width), lambda b, t: (0, 0))],
        out_specs=pl.BlockSpec((ts, width), lambda b, t: (b * st + t, 0)),
        out_shape=jax.ShapeDtypeStruct((n, width), BF16),
        scratch_shapes=[pltpu.VMEM((ts + HALO, width), F32)],
        compiler_params=_cparams(("arbitrary", "arbitrary"), 32),
        name="conv_module",
    )(p1, p1, p1, conv_w, conv_b.reshape(1, width), norm_g.reshape(1, width))


def _nsa_prep_kernel(q_ref, ks_ref, vs_ref, kw_ref, vw_ref, bg_ref, c_ref, s1_ref, s2_ref,
                     qg_ref, kg_ref, qexp_ref, qo_ref, kso_ref, vsto_ref, kwo_ref, vwto_ref, gto_ref):
    half = HEAD_DIM // 4 // 2
    tabs = (c_ref[...], s1_ref[...], s2_ref[...])
    x = q_ref[...]
    xn = x * lax.rsqrt(_group_mean(x * x, HEAD_DIM) + EPS) * qg_ref[...]
    qr = (_rope(xn, *tabs, half) * (HEAD_DIM ** -0.5 * LOG2E)).astype(BF16)
    qo_ref[...] = jnp.dot(qr, qexp_ref[...], preferred_element_type=F32).astype(BF16)
    for src, dst in ((ks_ref, kso_ref), (kw_ref, kwo_ref)):
        k = src[...]
        kn = k * lax.rsqrt(_group_mean(k * k, HEAD_DIM) + EPS) * kg_ref[...]
        dst[...] = _rope(kn, *tabs, half).astype(BF16)
    for src, dst in ((vs_ref, vsto_ref), (vw_ref, vwto_ref)):
        dst[0, 0] = src[...].T.astype(BF16)
    gto_ref[0, 0] = jax.nn.sigmoid(bg_ref[...]).T


def _nsa_prep(p1, tables, qgain, kgain, qexp, batch, seq, tm, kv_blk, bg_blk):
    n = p1.shape[0]
    width = qgain.shape[1]
    tiles_per_seq = seq // tm
    tab_spec = pl.BlockSpec((tm, LANES), lambda i: (i % tiles_per_seq, 0))
    row_spec = pl.BlockSpec((tm, LANES), lambda i: (i, 0))
    t_spec = pl.BlockSpec((1, 1, LANES, tm), lambda i: (i // tiles_per_seq, i % tiles_per_seq, 0, 0))
    row_out = jax.ShapeDtypeStruct((n, LANES), BF16)
    t_out = jax.ShapeDtypeStruct((batch, tiles_per_seq, LANES, tm), BF16)

    def col(j):
        return pl.BlockSpec((tm, LANES), lambda i: (i, j))

    return pl.pallas_call(
        _nsa_prep_kernel,
        grid=(n // tm,),
        in_specs=[pl.BlockSpec((tm, width), lambda i: (i, 3)),
                  col(kv_blk + 2), col(kv_blk + 3), col(kv_blk + 4), col(kv_blk + 5), col(bg_blk),
                  tab_spec, tab_spec, tab_spec,
                  pl.BlockSpec((1, width), lambda i: (0, 0)),
                  pl.BlockSpec((1, LANES), lambda i: (0, 0)),
                  pl.BlockSpec(qexp.shape, lambda i: (0, 0))],
        out_specs=[pl.BlockSpec((tm, qexp.shape[1]), lambda i: (i, 0)),
                   row_spec, t_spec, row_spec, t_spec, t_spec],
        out_shape=[jax.ShapeDtypeStruct((n, qexp.shape[1]), BF16), row_out, t_out, row_out, t_out,
                   jax.ShapeDtypeStruct((batch, tiles_per_seq, LANES, tm), F32)],
        compiler_params=_cparams(("parallel",), 32),
        name="nsa_prep",
    )(p1, p1, p1, p1, p1, p1, *tables, qgain, kgain, qexp)


def _compress_kernel(rk_ref, rv_ref, pk_ref, pv_ref, w1k_ref, w2k_ref, w1v_ref, w2v_ref,
                     kg_ref, c_ref, s1_ref, s2_ref, kc_ref, vct_ref):
    half = HEAD_DIM // 4 // 2
    ncp = rk_ref.shape[2]
    span = rk_ref.shape[3]

    def mlp(r_ref, p_ref, w1_ref, w2):
        r = r_ref[0, 0]
        a = jnp.dot((r + p_ref[0:1, :]).astype(BF16), w1_ref[0:span, :],
                    preferred_element_type=F32)
        b = jnp.dot((r + p_ref[1:2, :]).astype(BF16), w1_ref[span:2 * span, :],
                    preferred_element_type=F32)
        h = a + pltpu.roll(b, ncp - 1, 0)
        return jnp.dot(jax.nn.silu(h).astype(BF16), w2, preferred_element_type=F32)

    kc = mlp(rk_ref, pk_ref, w1k_ref, w2k_ref[0])
    kn = kc * lax.rsqrt(_group_mean(kc * kc, HEAD_DIM) + EPS) * kg_ref[...]
    kc_ref[0, 0] = _rope(kn, c_ref[...], s1_ref[...], s2_ref[...], half).astype(BF16)
    vc = mlp(rv_ref, pv_ref, w1v_ref, w2v_ref[...])
    vct_ref[0, 0] = vc.T[0:HEAD_DIM, :].astype(BF16)


def _compress(rk, rv, pk, pv, w1k, w2k, w1v, w2v, kgain, tables):
    batch, groups, ncp, span = rk.shape
    r_spec = pl.BlockSpec((1, 1, ncp, span), lambda b, g: (b, g, 0, 0))

    def full(a):
        return pl.BlockSpec(a.shape, lambda b, g: (0,) * a.ndim)

    return pl.pallas_call(
        _compress_kernel,
        grid=(batch, groups),
        in_specs=[r_spec, r_spec, full(pk), full(pv), full(w1k),
                  pl.BlockSpec((1,) + w2k.shape[1:], lambda b, g: (g, 0, 0)),
                  full(w1v), full(w2v), full(kgain),
                  full(tables[0]), full(tables[1]), full(tables[2])],
        out_specs=[pl.BlockSpec((1, 1, ncp, LANES), lambda b, g: (b, g, 0, 0)),
                   pl.BlockSpec((1, 1, HEAD_DIM, ncp), lambda b, g: (b, g, 0, 0))],
        out_shape=[jax.ShapeDtypeStruct((batch, groups, ncp, LANES), BF16),
                   jax.ShapeDtypeStruct((batch, groups, HEAD_DIM, ncp), BF16)],
        compiler_params=_cparams(("parallel", "parallel"), 32),
        name="nsa_compress",
    )(rk, rv, pk, pv, w1k, w2k, w1v, w2v, kgain, *tables)


def _nsa_kernel(q_ref, kc_ref, vct_ref, ks_ref, vst_ref, kw_ref, vwt_ref, gt_ref, dg_ref,
                ovt_ref, o_ref, acc_ref, imp_ref, sel_ref, st_ref, *, tq, nslc, nsel):
    g = pl.program_id(1)
    qi = pl.program_id(2)
    rheads = D_HEADS // D_KV_GROUPS
    tiny = float(np.finfo(np.float32).tiny)
    q = [q_ref[:, r * LANES:(r + 1) * LANES] for r in range(rheads)]
    qcol = lax.broadcasted_iota(jnp.int32, (tq, tq), 1)
    krow = lax.broadcasted_iota(jnp.int32, (tq, tq), 0)

    kc = kc_ref[0, 0]
    vct = vct_ref[0, 0]
    ncp = kc.shape[0]
    cend = lax.broadcasted_iota(jnp.int32, (ncp, tq), 0) * CMP_STRIDE + (CMP_BLOCK - 1)
    cvalid = cend <= qi * tq + lax.broadcasted_iota(jnp.int32, (ncp, tq), 1)
    psum = jnp.zeros((ncp, tq), F32)
    for r in range(rheads):
        st = lax.dot_general(kc, q[r], NT_DIMS, preferred_element_type=F32)
        st = jnp.where(cvalid, st, -jnp.inf)
        m = jnp.max(st, axis=0, keepdims=True)
        m = jnp.where(jnp.isfinite(m), m, 0.0)
        e = jnp.where(cvalid, jnp.exp2(st - m), 0.0)
        p = e / jnp.maximum(jnp.sum(e, axis=0, keepdims=True), tiny)
        psum += p
        acc_ref[2 * rheads + r] = jnp.dot(vct, p.astype(BF16), preferred_element_type=F32)

    hi, lo = _split_bf16(psum)
    ovt = ovt_ref[...]
    imp = (jnp.dot(ovt, hi, preferred_element_type=F32)
           + jnp.dot(ovt, lo, preferred_element_type=F32))
    blk = lax.broadcasted_iota(jnp.int32, (NSLC_PAD, tq), 0)
    cur = (qi * tq + lax.broadcasted_iota(jnp.int32, (NSLC_PAD, tq), 1)) // SLC_BLOCK
    imp = jnp.where(blk > cur, -jnp.inf, imp)
    forced = (blk == 0) | (blk == cur) | (blk == cur - 1)
    imp = jnp.where(forced, jnp.inf, imp)
    imp_ref[...] = imp
    rank = jnp.zeros((NSLC_PAD, tq), F32)
    for jp in range(nslc):
        row = imp_ref[jp:jp + 1, :]
        beats = (row > imp) | ((row == imp) & (blk > jp))
        rank += jnp.where(beats, 1.0, 0.0)
    sel_ref[...] = jnp.where(rank < nsel, 1.0, 0.0)

    acc_ref[0:2 * rheads] = jnp.zeros((2 * rheads,) + acc_ref.shape[1:], F32)

    def init():
        return (tuple(jnp.full((1, tq), -jnp.inf, F32) for _ in range(rheads)),
                tuple(jnp.zeros((1, tq), F32) for _ in range(rheads)))

    def attend(k, vt, valid, carry, base):
        ms, ls = carry
        cmax = [_scores_t(k, q[r], valid, st_ref, r) for r in range(rheads)]
        out = [_accumulate_t(st_ref, r, cmax[r], ms[r], ls[r], acc_ref, base + r, vt, True)
               for r in range(rheads)]
        return tuple(o[0] for o in out), tuple(o[1] for o in out)

    blocks_per_tile = tq // SLC_BLOCK

    def sel_step(ki, carry, diag):
        off = pl.multiple_of(ki * tq, tq)
        chosen = jnp.concatenate(
            [jnp.broadcast_to(sel_ref[pl.ds(ki * blocks_per_tile + c, 1), :], (SLC_BLOCK, tq))
             for c in range(blocks_per_tile)], axis=0)
        valid = chosen > 0.5
        if diag:
            valid = valid & (krow <= qcol)
        return attend(ks_ref[pl.ds(off, tq), :], vst_ref[0, ki], valid, carry, 0)

    carry = lax.fori_loop(0, qi, functools.partial(sel_step, diag=False), init())
    _, l_s = sel_step(qi, carry, True)

    def win_step(ki, carry):
        off = pl.multiple_of(ki * tq, tq)
        dpos = (qi - ki) * tq + qcol - krow
        valid = (dpos >= 0) & (dpos < WINDOW)
        return attend(kw_ref[pl.ds(off, tq), :], vwt_ref[0, ki], valid, carry, rheads)

    first = jnp.maximum(qi - (WINDOW + tq - 1) // tq, 0)
    _, l_w = lax.fori_loop(first, qi + 1, win_step, init())

    outs = []
    for r in range(rheads):
        row0 = (g * rheads + r) * 3
        o_c = acc_ref[2 * rheads + r]
        o_s = acc_ref[r] / jnp.maximum(l_s[r], tiny)
        o_w = acc_ref[rheads + r] / jnp.maximum(l_w[r], tiny)
        outs.append(gt_ref[0, 0, pl.ds(row0, 1), :] * o_c + gt_ref[0, 0, pl.ds(row0 + 1, 1), :] * o_s
                    + gt_ref[0, 0, pl.ds(row0 + 2, 1), :] * o_w)
    out = jnp.concatenate(outs, axis=0).T
    o_ref[...] = (out * jax.nn.silu(dg_ref[...])).astype(BF16)


def _nsa_attn(qe, kc, vct, ks, vst, kw, vwt, gt, p1, ovt, batch, seq, tq, dg_blk):
    n = qe.shape[0]
    groups = D_KV_GROUPS
    rheads = D_HEADS // groups
    width = rheads * HEAD_DIM
    qt = seq // tq
    ncp = kc.shape[2]
    nslc = seq // SLC_BLOCK
    kernel = functools.partial(_nsa_kernel, tq=tq, nslc=nslc, nsel=min(SLC_TOPK, nslc))
    k_spec = pl.BlockSpec((seq, LANES), lambda b, g, i: (b, 0))
    vt_spec = pl.BlockSpec((1, qt, HEAD_DIM, tq), lambda b, g, i: (b, 0, g, 0))
    return pl.pallas_call(
        kernel,
        grid=(batch, groups, qt),
        in_specs=[pl.BlockSpec((tq, rheads * LANES), lambda b, g, i: (b * qt + i, g)),
                  pl.BlockSpec((1, 1, ncp, LANES), lambda b, g, i: (b, g, 0, 0)),
                  pl.BlockSpec((1, 1, HEAD_DIM, ncp), lambda b, g, i: (b, g, 0, 0)),
                  k_spec, vt_spec, k_spec, vt_spec,
                  pl.BlockSpec((1, 1, LANES, tq), lambda b, g, i: (b, i, 0, 0)),
                  pl.BlockSpec((tq, width), lambda b, g, i: (b * qt + i, dg_blk + g)),
                  pl.BlockSpec(ovt.shape, lambda b, g, i: (0, 0))],
        out_specs=pl.BlockSpec((tq, width), lambda b, g, i: (b * qt + i, g)),
        out_shape=jax.ShapeDtypeStruct((n, groups * width), BF16),
        scratch_shapes=[pltpu.VMEM((3 * rheads, HEAD_DIM, tq), F32),
                        pltpu.VMEM((NSLC_PAD, tq), F32),
                        pltpu.VMEM((NSLC_PAD, tq), F32),
                        pltpu.VMEM((rheads, tq, tq), F32)],
        compiler_params=_cparams(("parallel", "parallel", "parallel"), 32),
        name="nsa_attn",
    )(qe, kc, vct, ks, vst, kw, vwt, gt, p1, ovt)


def _rope_tables(pos, d):
    rd = d // 4
    half = rd // 2
    inv = ROPE_THETA ** (-jnp.arange(half, dtype=F32) / half)
    ang = pos.astype(F32)[:, None] * inv[None, :]
    cos, sin = jnp.cos(ang), jnp.sin(ang)
    npos = pos.shape[0]
    zeros = jnp.zeros((npos, d), F32)
    c = jnp.concatenate([cos, cos, jnp.ones((npos, d - rd), F32)], axis=1)
    s1 = zeros.at[:, :half].set(-sin)
    s2 = zeros.at[:, half:rd].set(sin)
    return tuple(jnp.tile(t, (1, LANES // d)) for t in (c, s1, s2))


def _overlap_t(ncp, nslc):
    start = np.arange(ncp) * CMP_STRIDE
    s0 = np.arange(nslc) * SLC_BLOCK
    lo = np.maximum(start[:, None], s0[None, :])
    hi = np.minimum(start[:, None] + CMP_BLOCK, s0[None, :] + SLC_BLOCK)
    ov = np.clip(hi - lo, 0, None) / CMP_BLOCK
    ov[ncp - 1] = 0.0
    out = np.zeros((NSLC_PAD, ncp), np.float32)
    out[:nslc] = ov.T
    return jnp.asarray(out, BF16)


def kernel(x, mem, mem_norm, l0_norm, l0_w_in, l0_a_vnorm, l0_a_ws, l0_a_bs, l0_b_qnorm, l0_b_knorm, l0_b_lq1, l0_b_lk1, l0_b_lq2, l0_b_lk2, l0_b_subln, l0_m_wkv, l0_m_qnorm, l0_m_knorm, l0_w_out, l1_norm, l1_w_in, l1_c_conv_w, l1_c_conv_b, l1_c_norm, l1_d_qnorm, l1_d_knorm, l1_d_cmp_pos_k, l1_d_cmp_w1_k, l1_d_cmp_w2_k, l1_d_cmp_pos_v, l1_d_cmp_w1_v, l1_d_cmp_w2_v, l1_m_wkv, l1_m_qnorm, l1_m_knorm, l1_w_out):
    batch, seq, d_model = x.shape
    n = batch * seq
    tm = 256
    tq = 256
    mw = M_HEADS * HEAD_DIM
    pos = jnp.arange(seq, dtype=jnp.int32)
    x0 = x.reshape(n, d_model)

    wkv_both = jnp.concatenate([l0_m_wkv, l1_m_wkv], axis=1).astype(BF16)
    kgains = jnp.stack([jnp.tile(l0_m_knorm, M_HEADS), jnp.tile(l1_m_knorm, M_HEADS)])
    kv4 = _mem_kv(mem, mem_norm, wkv_both, kgains)

    p0 = _norm_matmul(x0, l0_norm, l0_w_in.astype(BF16), tm)
    a_width = l0_a_vnorm.shape[0]
    bs_exp = jnp.repeat(l0_a_bs.T, a_width // A_GROUPS, axis=1)
    y_a = _sgu(p0, l0_a_ws, bs_exp, l0_a_vnorm.reshape(1, a_width), 512)

    tab32 = _rope_tables(pos, B_QK_DIM)
    b_width = 2 * B_HEADS * B_QK_DIM
    qn, kn, vt = _diff_prep(p0, tab32,
                            jnp.tile(l0_b_qnorm, b_width // B_QK_DIM).reshape(1, b_width),
                            jnp.tile(l0_b_knorm, b_width // B_QK_DIM).reshape(1, b_width),
                            batch, seq, tq)
    lam_init = 0.8 - 0.6 * math.exp(-0.3 * 1)
    lam_p = jnp.stack([l0_b_lq1, l0_b_lk1, l0_b_lq2, l0_b_lk2])
    y_b = _diff_attn(lam_p, qn, kn, vt, p0, jnp.tile(l0_b_subln, 2).reshape(1, LANES),
                     batch, seq, tq, lam_init)
    y_m = _mem_attn(p0, kv4, 0, jnp.tile(l0_m_qnorm, M_HEADS).reshape(1, mw),
                    3584 // mw, 3840 // mw, seq, tq)
    x1 = _out_proj(x0, y_a, y_b, y_m, l0_w_out.astype(BF16), tm)

    w = l1_w_in
    n_bg = 3 * D_HEADS
    w1p = jnp.concatenate([w[:, 0:2048], w[:, 2840:3352], w[:, 2048:2816], w[:, 3352:3864],
                           w[:, 2816:2840], jnp.zeros((d_model, LANES - n_bg), F32)], axis=1)
    p1 = _norm_matmul(x1, l1_norm, w1p.astype(BF16), tm)
    y_c = _conv_module(p1, l1_c_conv_w, l1_c_conv_b, l1_c_norm, batch, seq, 512)

    groups, rheads = D_KV_GROUPS, D_HEADS // D_KV_GROUPS
    d_width = D_HEADS * HEAD_DIM
    kv_blk = 2560 // LANES
    qexp = np.zeros((d_width, D_HEADS * LANES), np.float32)
    for h in range(D_HEADS):
        for dd in range(HEAD_DIM):
            qexp[h * HEAD_DIM + dd, h * LANES + (h // rheads) * HEAD_DIM + dd] = 1.0
    tab64 = _rope_tables(pos, HEAD_DIM)
    qe, ks, vst, kw, vwt, gt = _nsa_prep(
        p1, tab64, jnp.tile(l1_d_qnorm, D_HEADS).reshape(1, d_width),
        jnp.tile(l1_d_knorm, LANES // HEAD_DIM).reshape(1, LANES), jnp.asarray(qexp, BF16),
        batch, seq, tq, kv_blk, 3840 // LANES)

    ncp = seq // CMP_STRIDE
    span = CMP_STRIDE * HEAD_DIM

    def regroup(cols):
        t = p1[:, cols:cols + LANES].reshape(batch, seq, groups, HEAD_DIM)
        return t.transpose(0, 2, 1, 3).reshape(batch, groups, ncp, span)

    cmp_pos = jnp.arange(ncp, dtype=jnp.int32) * CMP_STRIDE + (CMP_BLOCK - 1)
    hidden = l1_d_cmp_w2_k.shape[0]
    w2k = jnp.stack([jnp.zeros((hidden, LANES), F32).at[:, g * HEAD_DIM:(g + 1) * HEAD_DIM]
                     .set(l1_d_cmp_w2_k) for g in range(groups)])
    w2v = jnp.zeros((hidden, LANES), F32).at[:, :HEAD_DIM].set(l1_d_cmp_w2_v)
    kc, vct = _compress(
        regroup(2560), regroup(2560 + LANES),
        l1_d_cmp_pos_k.reshape(2, span), l1_d_cmp_pos_v.reshape(2, span),
        l1_d_cmp_w1_k.astype(BF16), w2k.astype(BF16), l1_d_cmp_w1_v.astype(BF16), w2v.astype(BF16),
        jnp.tile(l1_d_knorm, LANES // HEAD_DIM).reshape(1, LANES), _rope_tables(cmp_pos, HEAD_DIM))

    nslc = seq // SLC_BLOCK
    y_d = _nsa_attn(qe, kc, vct, ks, vst, kw, vwt, gt, p1, _overlap_t(ncp, nslc),
                    batch, seq, tq, 2048 // (rheads * HEAD_DIM))
    y_m1 = _mem_attn(p1, kv4, 1, jnp.tile(l1_m_qnorm, M_HEADS).reshape(1, mw),
                     3328 // mw, 3584 // mw, seq, tq)
    out = _out_proj(x1, y_c, y_d, y_m1, l1_w_out.astype(BF16), tm)
    return out.reshape(batch, seq, d_model)
```

```python
import functools
import math

import numpy as np
import jax
import jax.numpy as jnp
from jax import lax
from jax.experimental import pallas as pl
from jax.experimental.pallas import tpu as pltpu

F32 = jnp.float32
BF16 = jnp.bfloat16

EPS = 1e-6
ROPE_THETA = 500000.0
HEAD_DIM = 64
A_CHUNK = 128
A_GROUPS = 4
B_HEADS = 8
B_QK_DIM = 32
C_KERNEL = 31
D_HEADS = 8
D_KV_GROUPS = 2
CMP_BLOCK = 32
CMP_STRIDE = 16
SLC_BLOCK = 64
SLC_TOPK = 16
WINDOW = 512
M_HEADS = 4

LANES = 128
SUBLANES = 8
MXU_DIM = 256
HALO = 32
NSLC_PAD = 128

NT_DIMS = (((1,), (1,)), ((), ()))
LOG2E = math.log2(math.e)


def _cparams(semantics, vmem_mb):
    return pltpu.CompilerParams(dimension_semantics=semantics,
                                vmem_limit_bytes=vmem_mb * 1024 * 1024)


def _split_bf16(x):
    hi = x.astype(BF16)
    lo = (x - hi.astype(F32)).astype(BF16)
    return hi, lo


def _group_mean(x2, gsize):
    t, c = x2.shape
    w = min(c, MXU_DIM)
    r = lax.broadcasted_iota(jnp.int32, (w, w), 0) // gsize
    cc = lax.broadcasted_iota(jnp.int32, (w, w), 1) // gsize
    ones = jnp.where(r == cc, 1.0, 0.0).astype(BF16)
    outs = []
    for s in range(c // w):
        hi, lo = _split_bf16(x2[:, s * w:(s + 1) * w])
        outs.append(jnp.dot(hi, ones, preferred_element_type=F32)
                    + jnp.dot(lo, ones, preferred_element_type=F32))
    out = outs[0] if len(outs) == 1 else jnp.concatenate(outs, axis=1)
    return out * (1.0 / gsize)


def _tile_lanes(t, width):
    rep = width // t.shape[1]
    return t if rep == 1 else jnp.concatenate([t] * rep, axis=1)


def _rope(x, cos_t, s1_t, s2_t, half):
    width = x.shape[1]
    up = pltpu.roll(x, width - half, 1)
    dn = pltpu.roll(x, half, 1)
    return (x * _tile_lanes(cos_t, width) + up * _tile_lanes(s1_t, width)
            + dn * _tile_lanes(s2_t, width))


def _scores_t(k, q, valid, st_ref, idx):
    st = lax.dot_general(k, q, NT_DIMS, preferred_element_type=F32)
    if valid is not None:
        st = jnp.where(valid, st, -jnp.inf)
    st_ref[idx] = st
    return jnp.max(st, axis=0, keepdims=True)


def _accumulate_t(st_ref, idx, cmax, m, l, acc_ref, aidx, vt, guard_empty):
    m_new = jnp.maximum(m, cmax)
    m_use = jnp.where(m_new == -jnp.inf, 0.0, m_new) if guard_empty else m_new
    alpha = jnp.exp2(m - m_use)
    pt = jnp.exp2(st_ref[idx] - m_use)
    l_new = alpha * l + jnp.sum(pt, axis=0, keepdims=True)
    acc_ref[aidx] = alpha * acc_ref[aidx] + jnp.dot(vt, pt.astype(BF16), preferred_element_type=F32)
    return m_new, l_new


def _pipelined_sweep(qi, score_fn, accum_fn):
    score_fn(qi, 0, True)

    def pair(i, c):
        score_fn(2 * i, 1, False)
        accum_fn(jnp.where(i == 0, qi, 2 * i - 1), 0)
        score_fn(2 * i + 1, 0, False)
        accum_fn(2 * i, 1)
        return c

    lax.fori_loop(0, qi // 2, pair, 0)

    @pl.when(qi % 2 == 1)
    def _():
        score_fn(qi - 1, 1, False)
        accum_fn(jnp.where(qi == 1, qi, qi - 2), 0)
        accum_fn(qi - 1, 1)

    @pl.when(qi % 2 == 0)
    def _():
        accum_fn(jnp.where(qi == 0, qi, qi - 1), 0)


def _norm_matmul_kernel(x_ref, g_ref, w_ref, o_ref):
    x = x_ref[...]
    ms = jnp.mean(x * x, axis=-1, keepdims=True)
    h = (x * lax.rsqrt(ms + EPS) * g_ref[...]).astype(BF16)
    o_ref[...] = jnp.dot(h, w_ref[...], preferred_element_type=F32)


def _norm_matmul(x, g, w, tm):
    n, d = x.shape
    c = w.shape[1]
    return pl.pallas_call(
        _norm_matmul_kernel,
        grid=(n // tm,),
        in_specs=[pl.BlockSpec((tm, d), lambda i: (i, 0)),
                  pl.BlockSpec((1, d), lambda i: (0, 0)),
                  pl.BlockSpec((d, c), lambda i: (0, 0))],
        out_specs=pl.BlockSpec((tm, c), lambda i: (i, 0)),
        out_shape=jax.ShapeDtypeStruct((n, c), F32),
        compiler_params=_cparams(("parallel",), 48),
        name="norm_in_proj",
    )(x, g.reshape(1, d), w)


def _out_proj_kernel(x_ref, ya_ref, yb_ref, ym_ref, w_ref, o_ref):
    wa = ya_ref.shape[1]
    wb = yb_ref.shape[1]
    acc = jnp.dot(ya_ref[...], w_ref[0:wa, :], preferred_element_type=F32)
    acc += jnp.dot(yb_ref[...], w_ref[wa:wa + wb, :], preferred_element_type=F32)
    acc += jnp.dot(ym_ref[...], w_ref[wa + wb:, :], preferred_element_type=F32)
    o_ref[...] = x_ref[...] + acc


def _out_proj(x, ya, yb, ym, w, tm):
    n, d = x.shape
    return pl.pallas_call(
        _out_proj_kernel,
        grid=(n // tm,),
        in_specs=[pl.BlockSpec((tm, d), lambda i: (i, 0)),
                  pl.BlockSpec((tm, ya.shape[1]), lambda i: (i, 0)),
                  pl.BlockSpec((tm, yb.shape[1]), lambda i: (i, 0)),
                  pl.BlockSpec((tm, ym.shape[1]), lambda i: (i, 0)),
                  pl.BlockSpec(w.shape, lambda i: (0, 0))],
        out_specs=pl.BlockSpec((tm, d), lambda i: (i, 0)),
        out_shape=jax.ShapeDtypeStruct((n, d), F32),
        compiler_params=_cparams(("parallel",), 32),
        name="out_proj",
    )(x, ya, yb, ym, w)


def _sgu_kernel(u_ref, v_ref, g_ref, ws_ref, bs_ref, vg_ref, o_ref):
    tm = u_ref.shape[0]
    gdim = u_ref.shape[1] // A_GROUPS
    row = lax.broadcasted_iota(jnp.int32, (A_CHUNK, A_CHUNK), 0)
    col = lax.broadcasted_iota(jnp.int32, (A_CHUNK, A_CHUNK), 1)
    causal = col <= row
    for g in range(A_GROUPS):
        cols = slice(g * gdim, (g + 1) * gdim)
        gv = jax.nn.gelu(v_ref[:, cols])
        ms = jnp.mean(gv * gv, axis=-1, keepdims=True)
        vn = (gv * lax.rsqrt(ms + EPS) * vg_ref[:, cols]).astype(BF16)
        w = jnp.where(causal, ws_ref[g], 0.0).astype(BF16)
        for c in range(tm // A_CHUNK):
            rows = slice(c * A_CHUNK, (c + 1) * A_CHUNK)
            z = jnp.dot(w, vn[rows, :], preferred_element_type=F32) + bs_ref[:, cols]
            y = jax.nn.gelu(u_ref[rows, cols]) * z * jax.nn.silu(g_ref[rows, cols])
            o_ref[rows, cols] = y.astype(BF16)


def _sgu(p0, a_ws, bs_exp, vgain, tm):
    n = p0.shape[0]
    width = vgain.shape[1]
    return pl.pallas_call(
        _sgu_kernel,
        grid=(n // tm,),
        in_specs=[pl.BlockSpec((tm, width), lambda i: (i, 0)),
                  pl.BlockSpec((tm, width), lambda i: (i, 1)),
                  pl.BlockSpec((tm, width), lambda i: (i, 2)),
                  pl.BlockSpec(a_ws.shape, lambda i: (0, 0, 0)),
                  pl.BlockSpec(bs_exp.shape, lambda i: (0, 0)),
                  pl.BlockSpec(vgain.shape, lambda i: (0, 0))],
        out_specs=pl.BlockSpec((tm, width), lambda i: (i, 0)),
        out_shape=jax.ShapeDtypeStruct((n, width), BF16),
        compiler_params=_cparams(("parallel",), 32),
        name="sgu",
    )(p0, p0, p0, a_ws, bs_exp, vgain)


def _diff_prep_kernel(q_ref, k_ref, v_ref, c_ref, s1_ref, s2_ref, qg_ref, kg_ref,
                      qo_ref, ko_ref, vto_ref):
    half = B_QK_DIM // 4 // 2
    fold = (B_QK_DIM ** -0.5 * LOG2E, 1.0)
    for src, gain, dst, mul in ((q_ref, qg_ref, qo_ref, fold[0]), (k_ref, kg_ref, ko_ref, fold[1])):
        x = src[...]
        xn = x * lax.rsqrt(_group_mean(x * x, B_QK_DIM) + EPS) * gain[...]
        xr = _rope(xn, c_ref[...], s1_ref[...], s2_ref[...], half)
        dst[...] = (xr * mul if mul != 1.0 else xr).astype(BF16)
    vto_ref[0, 0] = v_ref[...].T.astype(BF16)


def _diff_prep(p0, tables, qgain, kgain, batch, seq, tm):
    n = p0.shape[0]
    width = qgain.shape[1]
    tiles_per_seq = seq // tm
    tab_spec = pl.BlockSpec((tm, LANES), lambda i: (i % tiles_per_seq, 0))
    out = jax.ShapeDtypeStruct((n, width), BF16)
    out_t = jax.ShapeDtypeStruct((batch, tiles_per_seq, width, tm), BF16)
    return pl.pallas_call(
        _diff_prep_kernel,
        grid=(n // tm,),
        in_specs=[pl.BlockSpec((tm, width), lambda i: (i, 3)),
                  pl.BlockSpec((tm, width), lambda i: (i, 4)),
                  pl.BlockSpec((tm, width), lambda i: (i, 5)),
                  tab_spec, tab_spec, tab_spec,
                  pl.BlockSpec((1, width), lambda i: (0, 0)),
                  pl.BlockSpec((1, width), lambda i: (0, 0))],
        out_specs=[pl.BlockSpec((tm, width), lambda i: (i, 0)),
                   pl.BlockSpec((tm, width), lambda i: (i, 0)),
                   pl.BlockSpec((1, 1, width, tm),
                                lambda i: (i // tiles_per_seq, i % tiles_per_seq, 0, 0))],
        out_shape=[out, out, out_t],
        compiler_params=_cparams(("parallel",), 32),
        name="diff_prep",
    )(p0, p0, p0, *tables, qgain, kgain)


def _diff_attn_kernel(lam_ref, q_ref, k_ref, vt_ref, g_ref, sub_ref, o_ref, acc_ref, sta_ref, stb_ref,
                      stat_ref, *, tq, lam_init):
    qi = pl.program_id(2)
    dv = 2 * B_QK_DIM
    nmap = 4
    q = q_ref[...]
    lane = lax.broadcasted_iota(jnp.int32, (1, LANES), 1)
    zero = jnp.zeros_like(q)
    qm = [jnp.where(lane // B_QK_DIM == j, q, zero) for j in range(nmap)]
    bufs = (sta_ref, stb_ref)
    acc_ref[...] = jnp.zeros_like(acc_ref)
    stat_ref[2 * nmap:3 * nmap, :] = jnp.full((nmap, tq), -jnp.inf, F32)
    stat_ref[3 * nmap:4 * nmap, :] = jnp.zeros((nmap, tq), F32)
    krow = lax.broadcasted_iota(jnp.int32, (tq, tq), 0)
    qcol = lax.broadcasted_iota(jnp.int32, (tq, tq), 1)

    def score_fn(tile, buf, diag):
        k = k_ref[pl.ds(pl.multiple_of(tile * tq, tq), tq), :]
        for j in range(nmap):
            row = buf * nmap + j
            stat_ref[row:row + 1, :] = _scores_t(k, qm[j], (krow <= qcol) if diag else None,
                                                 bufs[buf], j)

    def accum_fn(tile, buf):
        vt = vt_ref[0, tile]
        for j in range(nmap):
            cm, mr, lr = buf * nmap + j, 2 * nmap + j, 3 * nmap + j
            m, l = _accumulate_t(bufs[buf], j, stat_ref[cm:cm + 1, :], stat_ref[mr:mr + 1, :],
                                 stat_ref[lr:lr + 1, :], acc_ref, j, vt, False)
            stat_ref[mr:mr + 1, :] = m
            stat_ref[lr:lr + 1, :] = l

    _pipelined_sweep(qi, score_fn, accum_fn)
    ls = [stat_ref[3 * nmap + j:3 * nmap + j + 1, :] for j in range(nmap)]

    lam_p = lam_ref[...]
    lam = (jnp.exp(jnp.sum(lam_p[0:1] * lam_p[1:2], axis=-1, keepdims=True))
           - jnp.exp(jnp.sum(lam_p[2:3] * lam_p[3:4], axis=-1, keepdims=True)) + lam_init)
    halves = []
    for h in range(2):
        rows = slice(h * dv, (h + 1) * dv)
        pd = acc_ref[2 * h, rows, :] / ls[2 * h] - lam * (acc_ref[2 * h + 1, rows, :] / ls[2 * h + 1])
        ms2 = jnp.mean(pd * pd, axis=0, keepdims=True)
        halves.append(pd * lax.rsqrt(ms2 + EPS))
    ob = jnp.concatenate(halves, axis=0).T * sub_ref[...] * (1.0 - lam_init)
    o_ref[...] = (ob * jax.nn.silu(g_ref[...])).astype(BF16)


def _diff_attn(lam_p, qn, kn, vt, p0, subln, batch, seq, tq, lam_init):
    n, width = qn.shape
    pairs = width // LANES
    qt = seq // tq
    g_off = 3072 // LANES
    kernel = functools.partial(_diff_attn_kernel, tq=tq, lam_init=lam_init)
    return pl.pallas_call(
        kernel,
        grid=(batch, pairs, qt),
        in_specs=[pl.BlockSpec(lam_p.shape, lambda b, p, i: (0, 0)),
                  pl.BlockSpec((tq, LANES), lambda b, p, i: (b * qt + i, p)),
                  pl.BlockSpec((seq, LANES), lambda b, p, i: (b, p)),
                  pl.BlockSpec((1, qt, LANES, tq), lambda b, p, i: (b, 0, p, 0)),
                  pl.BlockSpec((tq, LANES), lambda b, p, i: (b * qt + i, g_off + p)),
                  pl.BlockSpec((1, LANES), lambda b, p, i: (0, 0))],
        out_specs=pl.BlockSpec((tq, LANES), lambda b, p, i: (b * qt + i, p)),
        out_shape=jax.ShapeDtypeStruct((n, width), BF16),
        scratch_shapes=[pltpu.VMEM((4, LANES, tq), F32), pltpu.VMEM((4, tq, tq), F32),
                        pltpu.VMEM((4, tq, tq), F32), pltpu.VMEM((16, tq), F32)],
        compiler_params=_cparams(("parallel", "parallel", "parallel"), 32),
        name="diff_attn",
    )(lam_p, qn, kn, vt, p0, subln)


def _mem_kv_kernel(mem_ref, g_ref, w_ref, kg_ref, o_ref):
    x = mem_ref[0]
    ms = jnp.mean(x * x, axis=-1, keepdims=True)
    h = (x * lax.rsqrt(ms + EPS) * g_ref[...]).astype(BF16)
    kv = jnp.dot(h, w_ref[...], preferred_element_type=F32)
    mw = kv.shape[1] // 4
    for layer in range(2):
        k = kv[:, (2 * layer) * mw:(2 * layer + 1) * mw]
        kn = k * lax.rsqrt(_group_mean(k * k, HEAD_DIM) + EPS) * kg_ref[layer:layer + 1, :]
        o_ref[2 * layer, 0] = kn.astype(BF16)
        o_ref[2 * layer + 1, 0] = kv[:, (2 * layer + 1) * mw:(2 * layer + 2) * mw].astype(BF16)


def _mem_kv(mem, mem_norm, wkv_both, kgains):
    batch, mtok, d = mem.shape
    mw = wkv_both.shape[1] // 4
    return pl.pallas_call(
        _mem_kv_kernel,
        grid=(batch,),
        in_specs=[pl.BlockSpec((1, mtok, d), lambda b: (b, 0, 0)),
                  pl.BlockSpec((1, d), lambda b: (0, 0)),
                  pl.BlockSpec(wkv_both.shape, lambda b: (0, 0)),
                  pl.BlockSpec(kgains.shape, lambda b: (0, 0))],
        out_specs=pl.BlockSpec((4, 1, mtok, mw), lambda b: (0, b, 0, 0)),
        out_shape=jax.ShapeDtypeStruct((4, batch, mtok, mw), BF16),
        compiler_params=_cparams(("parallel",), 32),
        name="mem_kv",
    )(mem, mem_norm.reshape(1, d), wkv_both, kgains)


def _mem_attn_kernel(q_ref, g_ref, k_ref, v_ref, qg_ref, o_ref):
    scale = HEAD_DIM ** -0.5
    x = q_ref[...]
    width = x.shape[1]
    qn = (x * lax.rsqrt(_group_mean(x * x, HEAD_DIM) + EPS) * qg_ref[...]).astype(BF16)
    k = k_ref[0, 0]
    v = v_ref[0, 0]
    lane = lax.broadcasted_iota(jnp.int32, (1, width), 1)
    zero = jnp.zeros_like(qn)
    out = jnp.zeros(x.shape, F32)
    for h in range(width // HEAD_DIM):
        hmask = lane // HEAD_DIM == h
        s = lax.dot_general(jnp.where(hmask, qn, zero), k, NT_DIMS,
                            preferred_element_type=F32) * scale
        p = jnp.exp(s - jnp.max(s, axis=-1, keepdims=True))
        l = jnp.sum(p, axis=-1, keepdims=True)
        o = jnp.dot(p.astype(BF16), v, preferred_element_type=F32) / l
        out = jnp.where(hmask, o, out)
    o_ref[...] = (out * jax.nn.silu(g_ref[...])).astype(BF16)


def _mem_attn(p, kv4, layer, qgain, q_blk, g_blk, seq, tq):
    n = p.shape[0]
    _, batch, mtok, mw = kv4.shape
    qt = seq // tq
    return pl.pallas_call(
        _mem_attn_kernel,
        grid=(n // tq,),
        in_specs=[pl.BlockSpec((tq, mw), lambda i: (i, q_blk)),
                  pl.BlockSpec((tq, mw), lambda i: (i, g_blk)),
                  pl.BlockSpec((1, 1, mtok, mw), lambda i: (2 * layer, i // qt, 0, 0)),
                  pl.BlockSpec((1, 1, mtok, mw), lambda i: (2 * layer + 1, i // qt, 0, 0)),
                  pl.BlockSpec((1, mw), lambda i: (0, 0))],
        out_specs=pl.BlockSpec((tq, mw), lambda i: (i, 0)),
        out_shape=jax.ShapeDtypeStruct((n, mw), BF16),
        compiler_params=_cparams(("parallel",), 32),
        name="mem_attn",
    )(p, p, kv4, kv4, qgain)


def _conv_kernel(a_ref, b_ref, g_ref, w_ref, cb_ref, ng_ref, o_ref, hbuf_ref, shift_ref):
    t = pl.program_id(1)
    ts = a_ref.shape[0]
    span = ts + HALO

    @pl.when(t == 0)
    def _():
        hbuf_ref[0:HALO, :] = jnp.zeros((HALO, hbuf_ref.shape[1]), F32)

    hbuf_ref[HALO:HALO + ts, :] = a_ref[...] * jax.nn.sigmoid(b_ref[...])
    for ph in range(SUBLANES):
        shift_ref[ph, 0:span - ph, :] = hbuf_ref[ph:span, :]
    first = HALO - (C_KERNEL - 1)
    y = None
    for j in range(C_KERNEL):
        ph, base = (first + j) % SUBLANES, (first + j) // SUBLANES * SUBLANES
        term = shift_ref[ph, base:base + ts, :] * w_ref[j:j + 1, :]
        y = term if y is None else y + term
    y += cb_ref[...]
    ms = jnp.mean(y * y, axis=-1, keepdims=True)
    yn = y * lax.rsqrt(ms + EPS) * ng_ref[...]
    o_ref[...] = (jax.nn.silu(yn) * jax.nn.silu(g_ref[...])).astype(BF16)
    hbuf_ref[0:HALO, :] = hbuf_ref[ts:ts + HALO, :]


def _conv_module(p1, conv_w, conv_b, norm_g, batch, seq, ts):
    n = p1.shape[0]
    width = conv_w.shape[1]
    st = seq // ts
    return pl.pallas_call(
        _conv_kernel,
        grid=(batch, st),
        in_specs=[pl.BlockSpec((ts, width), lambda b, t: (b * st + t, 0)),
                  pl.BlockSpec((ts, width), lambda b, t: (b * st + t, 1)),
                  pl.BlockSpec((ts, width), lambda b, t: (b * st + t, 2)),
                  pl.BlockSpec(conv_w.shape, lambda b, t: (0, 0)),
                  pl.BlockSpec((1, width), lambda b, t: (0, 0)),
                  pl.BlockSpec((1, width), lambda b, t: (0, 0))],
        out_specs=pl.BlockSpec((ts, width), lambda b, t: (b * st + t, 0)),
        out_shape=jax.ShapeDtypeStruct((n, width), BF16),
        scratch_shapes=[pltpu.VMEM((ts + HALO, width), F32),
                        pltpu.VMEM((SUBLANES, ts + HALO, width), F32)],
        compiler_params=_cparams(("arbitrary", "arbitrary"), 32),
        name="conv_module",
    )(p1, p1, p1, conv_w, conv_b.reshape(1, width), norm_g.reshape(1, width))


def _nsa_prep_kernel(q_ref, ks_ref, vs_ref, kw_ref, vw_ref, bg_ref, c_ref, s1_ref, s2_ref,
                     qg_ref, kg_ref, qexp_ref, qo_ref, kso_ref, vsto_ref, kwo_ref, vwto_ref, gto_ref):
    half = HEAD_DIM // 4 // 2
    tabs = (c_ref[...], s1_ref[...], s2_ref[...])
    x = q_ref[...]
    xn = x * lax.rsqrt(_group_mean(x * x, HEAD_DIM) + EPS) * qg_ref[...]
    qr = (_rope(xn, *tabs, half) * (HEAD_DIM ** -0.5 * LOG2E)).astype(BF16)
    qo_ref[...] = jnp.dot(qr, qexp_ref[...], preferred_element_type=F32).astype(BF16)
    for src, dst in ((ks_ref, kso_ref), (kw_ref, kwo_ref)):
        k = src[...]
        kn = k * lax.rsqrt(_group_mean(k * k, HEAD_DIM) + EPS) * kg_ref[...]
        dst[...] = _rope(kn, *tabs, half).astype(BF16)
    for src, dst in ((vs_ref, vsto_ref), (vw_ref, vwto_ref)):
        dst[0, 0] = src[...].T.astype(BF16)
    gto_ref[0, 0] = jax.nn.sigmoid(bg_ref[...]).T


def _nsa_prep(p1, tables, qgain, kgain, qexp, batch, seq, tm, kv_blk, bg_blk):
    n = p1.shape[0]
    width = qgain.shape[1]
    tiles_per_seq = seq // tm
    tab_spec = pl.BlockSpec((tm, LANES), lambda i: (i % tiles_per_seq, 0))
    row_spec = pl.BlockSpec((tm, LANES), lambda i: (i, 0))
    t_spec = pl.BlockSpec((1, 1, LANES, tm), lambda i: (i // tiles_per_seq, i % tiles_per_seq, 0, 0))
    row_out = jax.ShapeDtypeStruct((n, LANES), BF16)
    t_out = jax.ShapeDtypeStruct((batch, tiles_per_seq, LANES, tm), BF16)

    def col(j):
        return pl.BlockSpec((tm, LANES), lambda i: (i, j))

    return pl.pallas_call(
        _nsa_prep_kernel,
        grid=(n // tm,),
        in_specs=[pl.BlockSpec((tm, width), lambda i: (i, 3)),
                  col(kv_blk + 2), col(kv_blk + 3), col(kv_blk + 4), col(kv_blk + 5), col(bg_blk),
                  tab_spec, tab_spec, tab_spec,
                  pl.BlockSpec((1, width), lambda i: (0, 0)),
                  pl.BlockSpec((1, LANES), lambda i: (0, 0)),
                  pl.BlockSpec(qexp.shape, lambda i: (0, 0))],
        out_specs=[pl.BlockSpec((tm, qexp.shape[1]), lambda i: (i, 0)),
                   row_spec, t_spec, row_spec, t_spec, t_spec],
        out_shape=[jax.ShapeDtypeStruct((n, qexp.shape[1]), BF16), row_out, t_out, row_out, t_out,
                   jax.ShapeDtypeStruct((batch, tiles_per_seq, LANES, tm), F32)],
        compiler_params=_cparams(("parallel",), 32),
        name="nsa_prep",
    )(p1, p1, p1, p1, p1, p1, *tables, qgain, kgain, qexp)


def _compress_kernel(rk_ref, rv_ref, pk_ref, pv_ref, w1k_ref, w2k_ref, w1v_ref, w2v_ref,
                     kg_ref, c_ref, s1_ref, s2_ref, kc_ref, vct_ref):
    half = HEAD_DIM // 4 // 2
    ncp = rk_ref.shape[2]
    span = rk_ref.shape[3]

    def mlp(r_ref, p_ref, w1_ref, w2):
        r = r_ref[0, 0]
        a = jnp.dot((r + p_ref[0:1, :]).astype(BF16), w1_ref[0:span, :],
                    preferred_element_type=F32)
        b = jnp.dot((r + p_ref[1:2, :]).astype(BF16), w1_ref[span:2 * span, :],
                    preferred_element_type=F32)
        h = a + pltpu.roll(b, ncp - 1, 0)
        return jnp.dot(jax.nn.silu(h).astype(BF16), w2, preferred_element_type=F32)

    kc = mlp(rk_ref, pk_ref, w1k_ref, w2k_ref[0])
    kn = kc * lax.rsqrt(_group_mean(kc * kc, HEAD_DIM) + EPS) * kg_ref[...]
    kc_ref[0, 0] = _rope(kn, c_ref[...], s1_ref[...], s2_ref[...], half).astype(BF16)
    vc = mlp(rv_ref, pv_ref, w1v_ref, w2v_ref[...])
    vct_ref[0, 0] = vc.T[0:HEAD_DIM, :].astype(BF16)


def _compress(rk, rv, pk, pv, w1k, w2k, w1v, w2v, kgain, tables):
    batch, groups, ncp, span = rk.shape
    r_spec = pl.BlockSpec((1, 1, ncp, span), lambda b, g: (b, g, 0, 0))

    def full(a):
        return pl.BlockSpec(a.shape, lambda b, g: (0,) * a.ndim)

    return pl.pallas_call(
        _compress_kernel,
        grid=(batch, groups),
        in_specs=[r_spec, r_spec, full(pk), full(pv), full(w1k),
                  pl.BlockSpec((1,) + w2k.shape[1:], lambda b, g: (g, 0, 0)),
                  full(w1v), full(w2v), full(kgain),
                  full(tables[0]), full(tables[1]), full(tables[2])],
        out_specs=[pl.BlockSpec((1, 1, ncp, LANES), lambda b, g: (b, g, 0, 0)),
                   pl.BlockSpec((1, 1, HEAD_DIM, ncp), lambda b, g: (b, g, 0, 0))],
        out_shape=[jax.ShapeDtypeStruct((batch, groups, ncp, LANES), BF16),
                   jax.ShapeDtypeStruct((batch, groups, HEAD_DIM, ncp), BF16)],
        compiler_params=_cparams(("parallel", "parallel"), 32),
        name="nsa_compress",
    )(rk, rv, pk, pv, w1k, w2k, w1v, w2v, kgain, *tables)


def _nsa_kernel(q_ref, kc_ref, vct_ref, ks_ref, vst_ref, kw_ref, vwt_ref, gt_ref, dg_ref,
                ovt_ref, o_ref, acc_ref, imp_ref, sel_ref, st_ref, stb_ref, stat_ref,
                *, tq, nslc, nsel):
    g = pl.program_id(1)
    qi = pl.program_id(2)
    rheads = D_HEADS // D_KV_GROUPS
    tiny = float(np.finfo(np.float32).tiny)
    q = [q_ref[:, r * LANES:(r + 1) * LANES] for r in range(rheads)]
    qcol = lax.broadcasted_iota(jnp.int32, (tq, tq), 1)
    krow = lax.broadcasted_iota(jnp.int32, (tq, tq), 0)

    kc = kc_ref[0, 0]
    vct = vct_ref[0, 0]
    ncp = kc.shape[0]
    cend = lax.broadcasted_iota(jnp.int32, (ncp, tq), 0) * CMP_STRIDE + (CMP_BLOCK - 1)
    cvalid = cend <= qi * tq + lax.broadcasted_iota(jnp.int32, (ncp, tq), 1)
    psum = jnp.zeros((ncp, tq), F32)
    for r in range(rheads):
        st = lax.dot_general(kc, q[r], NT_DIMS, preferred_element_type=F32)
        st = jnp.where(cvalid, st, -jnp.inf)
        m = jnp.max(st, axis=0, keepdims=True)
        m = jnp.where(jnp.isfinite(m), m, 0.0)
        e = jnp.where(cvalid, jnp.exp2(st - m), 0.0)
        p = e / jnp.maximum(jnp.sum(e, axis=0, keepdims=True), tiny)
        psum += p
        acc_ref[2 * rheads + r] = jnp.dot(vct, p.astype(BF16), preferred_element_type=F32)

    hi, lo = _split_bf16(psum)
    ovt = ovt_ref[...]
    imp = (jnp.dot(ovt, hi, preferred_element_type=F32)
           + jnp.dot(ovt, lo, preferred_element_type=F32))
    blk = lax.broadcasted_iota(jnp.int32, (NSLC_PAD, tq), 0)
    cur = (qi * tq + lax.broadcasted_iota(jnp.int32, (NSLC_PAD, tq), 1)) // SLC_BLOCK
    imp = jnp.where(blk > cur, -jnp.inf, imp)
    forced = (blk == 0) | (blk == cur) | (blk == cur - 1)
    imp = jnp.where(forced, jnp.inf, imp)
    imp_ref[...] = imp
    rank = jnp.zeros((NSLC_PAD, tq), F32)
    for jp in range(nslc):
        row = imp_ref[jp:jp + 1, :]
        beats = (row > imp) | ((row == imp) & (blk > jp))
        rank += jnp.where(beats, 1.0, 0.0)
    sel_ref[...] = jnp.where(rank < nsel, 1.0, 0.0)

    acc_ref[0:2 * rheads] = jnp.zeros((2 * rheads,) + acc_ref.shape[1:], F32)

    blocks_per_tile = tq // SLC_BLOCK
    bufs = (st_ref, stb_ref)
    stat_ref[2 * rheads:3 * rheads, :] = jnp.full((rheads, tq), -jnp.inf, F32)
    stat_ref[3 * rheads:4 * rheads, :] = jnp.zeros((rheads, tq), F32)

    def score_fn(tile, buf, diag):
        k = ks_ref[pl.ds(pl.multiple_of(tile * tq, tq), tq), :]
        chosen = jnp.concatenate(
            [jnp.broadcast_to(sel_ref[pl.ds(tile * blocks_per_tile + c, 1), :], (SLC_BLOCK, tq))
             for c in range(blocks_per_tile)], axis=0)
        valid = chosen > 0.5
        if diag:
            valid = valid & (krow <= qcol)
        for r in range(rheads):
            row = buf * rheads + r
            stat_ref[row:row + 1, :] = _scores_t(k, q[r], valid, bufs[buf], r)

    def accum_fn(tile, buf):
        vt = vst_ref[0, tile]
        for r in range(rheads):
            cm, mr, lr = buf * rheads + r, 2 * rheads + r, 3 * rheads + r
            m, l = _accumulate_t(bufs[buf], r, stat_ref[cm:cm + 1, :], stat_ref[mr:mr + 1, :],
                                 stat_ref[lr:lr + 1, :], acc_ref, r, vt, True)
            stat_ref[mr:mr + 1, :] = m
            stat_ref[lr:lr + 1, :] = l

    _pipelined_sweep(qi, score_fn, accum_fn)
    l_s = [stat_ref[3 * rheads + r:3 * rheads + r + 1, :] for r in range(rheads)]

    def init():
        return (tuple(jnp.full((1, tq), -jnp.inf, F32) for _ in range(rheads)),
                tuple(jnp.zeros((1, tq), F32) for _ in range(rheads)))

    def attend(k, vt, valid, carry, base):
        ms, ls = carry
        cmax = [_scores_t(k, q[r], valid, st_ref, r) for r in range(rheads)]
        out = [_accumulate_t(st_ref, r, cmax[r], ms[r], ls[r], acc_ref, base + r, vt, True)
               for r in range(rheads)]
        return tuple(o[0] for o in out), tuple(o[1] for o in out)

    def win_step(ki, carry):
        off = pl.multiple_of(ki * tq, tq)
        dpos = (qi - ki) * tq + qcol - krow
        valid = (dpos >= 0) & (dpos < WINDOW)
        return attend(kw_ref[pl.ds(off, tq), :], vwt_ref[0, ki], valid, carry, rheads)

    first = jnp.maximum(qi - (WINDOW + tq - 1) // tq, 0)
    _, l_w = lax.fori_loop(first, qi + 1, win_step, init())

    outs = []
    for r in range(rheads):
        row0 = (g * rheads + r) * 3
        o_c = acc_ref[2 * rheads + r]
        o_s = acc_ref[r] / jnp.maximum(l_s[r], tiny)
        o_w = acc_ref[rheads + r] / jnp.maximum(l_w[r], tiny)
        outs.append(gt_ref[0, 0, pl.ds(row0, 1), :] * o_c + gt_ref[0, 0, pl.ds(row0 + 1, 1), :] * o_s
                    + gt_ref[0, 0, pl.ds(row0 + 2, 1), :] * o_w)
    out = jnp.concatenate(outs, axis=0).T
    o_ref[...] = (out * jax.nn.silu(dg_ref[...])).astype(BF16)


def _nsa_attn(qe, kc, vct, ks, vst, kw, vwt, gt, p1, ovt, batch, seq, tq, dg_blk):
    n = qe.shape[0]
    groups = D_KV_GROUPS
    rheads = D_HEADS // groups
    width = rheads * HEAD_DIM
    qt = seq // tq
    ncp = kc.shape[2]
    nslc = seq // SLC_BLOCK
    kernel = functools.partial(_nsa_kernel, tq=tq, nslc=nslc, nsel=min(SLC_TOPK, nslc))
    k_spec = pl.BlockSpec((seq, LANES), lambda b, g, i: (b, 0))
    vt_spec = pl.BlockSpec((1, qt, HEAD_DIM, tq), lambda b, g, i: (b, 0, g, 0))
    return pl.pallas_call(
        kernel,
        grid=(batch, groups, qt),
        in_specs=[pl.BlockSpec((tq, rheads * LANES), lambda b, g, i: (b * qt + i, g)),
                  pl.BlockSpec((1, 1, ncp, LANES), lambda b, g, i: (b, g, 0, 0)),
                  pl.BlockSpec((1, 1, HEAD_DIM, ncp), lambda b, g, i: (b, g, 0, 0)),
                  k_spec, vt_spec, k_spec, vt_spec,
                  pl.BlockSpec((1, 1, LANES, tq), lambda b, g, i: (b, i, 0, 0)),
                  pl.BlockSpec((tq, width), lambda b, g, i: (b * qt + i, dg_blk + g)),
                  pl.BlockSpec(ovt.shape, lambda b, g, i: (0, 0))],
        out_specs=pl.BlockSpec((tq, width), lambda b, g, i: (b * qt + i, g)),
        out_shape=jax.ShapeDtypeStruct((n, groups * width), BF16),
        scratch_shapes=[pltpu.VMEM((3 * rheads, HEAD_DIM, tq), F32),
                        pltpu.VMEM((NSLC_PAD, tq), F32),
                        pltpu.VMEM((NSLC_PAD, tq), F32),
                        pltpu.VMEM((rheads, tq, tq), F32),
                        pltpu.VMEM((rheads, tq, tq), F32),
                        pltpu.VMEM((4 * rheads, tq), F32)],
        compiler_params=_cparams(("parallel", "parallel", "parallel"), 32),
        name="nsa_attn",
    )(qe, kc, vct, ks, vst, kw, vwt, gt, p1, ovt)


def _rope_tables(pos, d):
    rd = d // 4
    half = rd // 2
    inv = ROPE_THETA ** (-jnp.arange(half, dtype=F32) / half)
    ang = pos.astype(F32)[:, None] * inv[None, :]
    cos, sin = jnp.cos(ang), jnp.sin(ang)
    npos = pos.shape[0]
    zeros = jnp.zeros((npos, d), F32)
    c = jnp.concatenate([cos, cos, jnp.ones((npos, d - rd), F32)], axis=1)
    s1 = zeros.at[:, :half].set(-sin)
    s2 = zeros.at[:, half:rd].set(sin)
    return tuple(jnp.tile(t, (1, LANES // d)) for t in (c, s1, s2))


def _overlap_t(ncp, nslc):
    start = np.arange(ncp) * CMP_STRIDE
    s0 = np.arange(nslc) * SLC_BLOCK
    lo = np.maximum(start[:, None], s0[None, :])
    hi = np.minimum(start[:, None] + CMP_BLOCK, s0[None, :] + SLC_BLOCK)
    ov = np.clip(hi - lo, 0, None) / CMP_BLOCK
    ov[ncp - 1] = 0.0
    out = np.zeros((NSLC_PAD, ncp), np.float32)
    out[:nslc] = ov.T
    return jnp.asarray(out, BF16)


def kernel(x, mem, mem_norm, l0_norm, l0_w_in, l0_a_vnorm, l0_a_ws, l0_a_bs, l0_b_qnorm, l0_b_knorm, l0_b_lq1, l0_b_lk1, l0_b_lq2, l0_b_lk2, l0_b_subln, l0_m_wkv, l0_m_qnorm, l0_m_knorm, l0_w_out, l1_norm, l1_w_in, l1_c_conv_w, l1_c_conv_b, l1_c_norm, l1_d_qnorm, l1_d_knorm, l1_d_cmp_pos_k, l1_d_cmp_w1_k, l1_d_cmp_w2_k, l1_d_cmp_pos_v, l1_d_cmp_w1_v, l1_d_cmp_w2_v, l1_m_wkv, l1_m_qnorm, l1_m_knorm, l1_w_out):
    batch, seq, d_model = x.shape
    n = batch * seq
    tm = 256
    tq = 256
    mw = M_HEADS * HEAD_DIM
    pos = jnp.arange(seq, dtype=jnp.int32)
    x0 = x.reshape(n, d_model)

    wkv_both = jnp.concatenate([l0_m_wkv, l1_m_wkv], axis=1).astype(BF16)
    kgains = jnp.stack([jnp.tile(l0_m_knorm, M_HEADS), jnp.tile(l1_m_knorm, M_HEADS)])
    kv4 = _mem_kv(mem, mem_norm, wkv_both, kgains)

    p0 = _norm_matmul(x0, l0_norm, l0_w_in.astype(BF16), tm)
    a_width = l0_a_vnorm.shape[0]
    bs_exp = jnp.repeat(l0_a_bs.T, a_width // A_GROUPS, axis=1)
    y_a = _sgu(p0, l0_a_ws, bs_exp, l0_a_vnorm.reshape(1, a_width), 512)

    tab32 = _rope_tables(pos, B_QK_DIM)
    b_width = 2 * B_HEADS * B_QK_DIM
    qn, kn, vt = _diff_prep(p0, tab32,
                            jnp.tile(l0_b_qnorm, b_width // B_QK_DIM).reshape(1, b_width),
                            jnp.tile(l0_b_knorm, b_width // B_QK_DIM).reshape(1, b_width),
                            batch, seq, tq)
    lam_init = 0.8 - 0.6 * math.exp(-0.3 * 1)
    lam_p = jnp.stack([l0_b_lq1, l0_b_lk1, l0_b_lq2, l0_b_lk2])
    y_b = _diff_attn(lam_p, qn, kn, vt, p0, jnp.tile(l0_b_subln, 2).reshape(1, LANES),
                     batch, seq, tq, lam_init)
    y_m = _mem_attn(p0, kv4, 0, jnp.tile(l0_m_qnorm, M_HEADS).reshape(1, mw),
                    3584 // mw, 3840 // mw, seq, tq)
    x1 = _out_proj(x0, y_a, y_b, y_m, l0_w_out.astype(BF16), tm)

    w = l1_w_in
    n_bg = 3 * D_HEADS
    w1p = jnp.concatenate([w[:, 0:2048], w[:, 2840:3352], w[:, 2048:2816], w[:, 3352:3864],
                           w[:, 2816:2840], jnp.zeros((d_model, LANES - n_bg), F32)], axis=1)
    p1 = _norm_matmul(x1, l1_norm, w1p.astype(BF16), tm)
    y_c = _conv_module(p1, l1_c_conv_w, l1_c_conv_b, l1_c_norm, batch, seq, 512)

    groups, rheads = D_KV_GROUPS, D_HEADS // D_KV_GROUPS
    d_width = D_HEADS * HEAD_DIM
    kv_blk = 2560 // LANES
    qexp = np.zeros((d_width, D_HEADS * LANES), np.float32)
    for h in range(D_HEADS):
        for dd in range(HEAD_DIM):
            qexp[h * HEAD_DIM + dd, h * LANES + (h // rheads) * HEAD_DIM + dd] = 1.0
    tab64 = _rope_tables(pos, HEAD_DIM)
    qe, ks, vst, kw, vwt, gt = _nsa_prep(
        p1, tab64, jnp.tile(l1_d_qnorm, D_HEADS).reshape(1, d_width),
        jnp.tile(l1_d_knorm, LANES // HEAD_DIM).reshape(1, LANES), jnp.asarray(qexp, BF16),
        batch, seq, tq, kv_blk, 3840 // LANES)

    ncp = seq // CMP_STRIDE
    span = CMP_STRIDE * HEAD_DIM

    def regroup(cols):
        t = p1[:, cols:cols + LANES].reshape(batch, seq, groups, HEAD_DIM)
        return t.transpose(0, 2, 1, 3).reshape(batch, groups, ncp, span)

    cmp_pos = jnp.arange(ncp, dtype=jnp.int32) * CMP_STRIDE + (CMP_BLOCK - 1)
    hidden = l1_d_cmp_w2_k.shape[0]
    w2k = jnp.stack([jnp.zeros((hidden, LANES), F32).at[:, g * HEAD_DIM:(g + 1) * HEAD_DIM]
                     .set(l1_d_cmp_w2_k) for g in range(groups)])
    w2v = jnp.zeros((hidden, LANES), F32).at[:, :HEAD_DIM].set(l1_d_cmp_w2_v)
    kc, vct = _compress(
        regroup(2560), regroup(2560 + LANES),
        l1_d_cmp_pos_k.reshape(2, span), l1_d_cmp_pos_v.reshape(2, span),
        l1_d_cmp_w1_k.astype(BF16), w2k.astype(BF16), l1_d_cmp_w1_v.astype(BF16), w2v.astype(BF16),
        jnp.tile(l1_d_knorm, LANES // HEAD_DIM).reshape(1, LANES), _rope_tables(cmp_pos, HEAD_DIM))

    nslc = seq // SLC_BLOCK
    y_d = _nsa_attn(qe, kc, vct, ks, vst, kw, vwt, gt, p1, _overlap_t(ncp, nslc),
                    batch, seq, tq, 2048 // (rheads * HEAD_DIM))
    y_m1 = _mem_attn(p1, kv4, 1, jnp.tile(l1_m_qnorm, M_HEADS).reshape(1, mw),
                     3328 // mw, 3584 // mw, seq, tq)
    out = _out_proj(x1, y_c, y_d, y_m1, l1_w_out.astype(BF16), tm)
    return out.reshape(batch, seq, d_model)
```

```python
import functools
import math

import numpy as np
import jax
import jax.numpy as jnp
from jax import lax
from jax.experimental import pallas as pl
from jax.experimental.pallas import tpu as pltpu

F32 = jnp.float32
BF16 = jnp.bfloat16

EPS = 1e-6
ROPE_THETA = 500000.0
HEAD_DIM = 64
A_CHUNK = 128
A_GROUPS = 4
B_HEADS = 8
B_QK_DIM = 32
C_KERNEL = 31
D_HEADS = 8
D_KV_GROUPS = 2
CMP_BLOCK = 32
CMP_STRIDE = 16
SLC_BLOCK = 64
SLC_TOPK = 16
WINDOW = 512
M_HEADS = 4

LANES = 128
SUBLANES = 8
MXU_DIM = 256
HALO = 32
NSLC_PAD = 128
ONES_ROWS = 16

NT_DIMS = (((1,), (1,)), ((), ()))
LOG2E = math.log2(math.e)


def _cparams(semantics, vmem_mb):
    return pltpu.CompilerParams(dimension_semantics=semantics,
                                vmem_limit_bytes=vmem_mb * 1024 * 1024)


def _split_bf16(x):
    hi = x.astype(BF16)
    lo = (x - hi.astype(F32)).astype(BF16)
    return hi, lo


def _group_mean(x2, gsize):
    t, c = x2.shape
    w = min(c, MXU_DIM)
    r = lax.broadcasted_iota(jnp.int32, (w, w), 0) // gsize
    cc = lax.broadcasted_iota(jnp.int32, (w, w), 1) // gsize
    ones = jnp.where(r == cc, 1.0, 0.0).astype(BF16)
    outs = []
    for s in range(c // w):
        hi, lo = _split_bf16(x2[:, s * w:(s + 1) * w])
        outs.append(jnp.dot(hi, ones, preferred_element_type=F32)
                    + jnp.dot(lo, ones, preferred_element_type=F32))
    out = outs[0] if len(outs) == 1 else jnp.concatenate(outs, axis=1)
    return out * (1.0 / gsize)


def _tile_lanes(t, width):
    rep = width // t.shape[1]
    return t if rep == 1 else jnp.concatenate([t] * rep, axis=1)


def _rope(x, cos_t, s1_t, s2_t, half):
    width = x.shape[1]
    up = pltpu.roll(x, width - half, 1)
    dn = pltpu.roll(x, half, 1)
    return (x * _tile_lanes(cos_t, width) + up * _tile_lanes(s1_t, width)
            + dn * _tile_lanes(s2_t, width))


def _scores_t(k, q, valid, st_ref, idx):
    st = lax.dot_general(k, q, NT_DIMS, preferred_element_type=F32)
    if valid is not None:
        st = jnp.where(valid, st, -jnp.inf)
    st_ref[idx] = st
    return jnp.max(st, axis=0, keepdims=True)


def _accumulate_t(st_ref, idx, cmax, m, acc_ref, aidx, vt, guard_empty):
    m_new = jnp.maximum(m, cmax)
    m_use = jnp.where(m_new == -jnp.inf, 0.0, m_new) if guard_empty else m_new
    alpha = jnp.exp2(m - m_use)
    pt = jnp.exp2((st_ref[idx] - m_use).astype(BF16))
    acc_ref[aidx] = alpha * acc_ref[aidx] + jnp.dot(vt, pt, preferred_element_type=F32)
    return m_new


def _pipelined_sweep(n, score_fn, accum_fn, unroll=4):
    score_fn(0, 0)
    stages = n - 1

    def stage(e, parity):
        score_fn(e + 1, 1 - parity)
        accum_fn(e, parity)

    def body(i, c):
        for k in range(unroll):
            stage(unroll * i + k, k % 2)
        return c

    lax.fori_loop(0, stages // unroll, body, 0)
    base = stages // unroll * unroll
    left = stages - base
    for k in range(unroll - 1):
        pl.when(left > k)(functools.partial(stage, base + k, k % 2))
    for parity in range(2):
        pl.when(left % 2 == parity)(functools.partial(accum_fn, n - 1, parity))


def _norm_matmul_kernel(x_ref, g_ref, w_ref, o_ref):
    x = x_ref[...]
    ms = jnp.mean(x * x, axis=-1, keepdims=True)
    h = (x * lax.rsqrt(ms + EPS) * g_ref[...]).astype(BF16)
    o_ref[...] = jnp.dot(h, w_ref[...], preferred_element_type=F32)


def _norm_matmul(x, g, w, tm):
    n, d = x.shape
    c = w.shape[1]
    return pl.pallas_call(
        _norm_matmul_kernel,
        grid=(n // tm,),
        in_specs=[pl.BlockSpec((tm, d), lambda i: (i, 0)),
                  pl.BlockSpec((1, d), lambda i: (0, 0)),
                  pl.BlockSpec((d, c), lambda i: (0, 0))],
        out_specs=pl.BlockSpec((tm, c), lambda i: (i, 0)),
        out_shape=jax.ShapeDtypeStruct((n, c), F32),
        compiler_params=_cparams(("parallel",), 48),
        name="norm_in_proj",
    )(x, g.reshape(1, d), w)


def _out_proj_kernel(x_ref, ya_ref, yb_ref, ym_ref, w_ref, o_ref):
    wa = ya_ref.shape[1]
    wb = yb_ref.shape[1]
    acc = jnp.dot(ya_ref[...], w_ref[0:wa, :], preferred_element_type=F32)
    acc += jnp.dot(yb_ref[...], w_ref[wa:wa + wb, :], preferred_element_type=F32)
    acc += jnp.dot(ym_ref[...], w_ref[wa + wb:, :], preferred_element_type=F32)
    o_ref[...] = x_ref[...] + acc


def _out_proj(x, ya, yb, ym, w, tm):
    n, d = x.shape
    return pl.pallas_call(
        _out_proj_kernel,
        grid=(n // tm,),
        in_specs=[pl.BlockSpec((tm, d), lambda i: (i, 0)),
                  pl.BlockSpec((tm, ya.shape[1]), lambda i: (i, 0)),
                  pl.BlockSpec((tm, yb.shape[1]), lambda i: (i, 0)),
                  pl.BlockSpec((tm, ym.shape[1]), lambda i: (i, 0)),
                  pl.BlockSpec(w.shape, lambda i: (0, 0))],
        out_specs=pl.BlockSpec((tm, d), lambda i: (i, 0)),
        out_shape=jax.ShapeDtypeStruct((n, d), F32),
        compiler_params=_cparams(("parallel",), 32),
        name="out_proj",
    )(x, ya, yb, ym, w)


def _sgu_kernel(u_ref, v_ref, g_ref, ws_ref, bs_ref, vg_ref, o_ref):
    tm = u_ref.shape[0]
    gdim = u_ref.shape[1] // A_GROUPS
    row = lax.broadcasted_iota(jnp.int32, (A_CHUNK, A_CHUNK), 0)
    col = lax.broadcasted_iota(jnp.int32, (A_CHUNK, A_CHUNK), 1)
    causal = col <= row
    for g in range(A_GROUPS):
        cols = slice(g * gdim, (g + 1) * gdim)
        gv = jax.nn.gelu(v_ref[:, cols])
        ms = jnp.mean(gv * gv, axis=-1, keepdims=True)
        vn = (gv * lax.rsqrt(ms + EPS) * vg_ref[:, cols]).astype(BF16)
        w = jnp.where(causal, ws_ref[g], 0.0).astype(BF16)
        for c in range(tm // A_CHUNK):
            rows = slice(c * A_CHUNK, (c + 1) * A_CHUNK)
            z = jnp.dot(w, vn[rows, :], preferred_element_type=F32) + bs_ref[:, cols]
            y = jax.nn.gelu(u_ref[rows, cols]) * z * jax.nn.silu(g_ref[rows, cols])
            o_ref[rows, cols] = y.astype(BF16)


def _sgu(p0, a_ws, bs_exp, vgain, tm):
    n = p0.shape[0]
    width = vgain.shape[1]
    return pl.pallas_call(
        _sgu_kernel,
        grid=(n // tm,),
        in_specs=[pl.BlockSpec((tm, width), lambda i: (i, 0)),
                  pl.BlockSpec((tm, width), lambda i: (i, 1)),
                  pl.BlockSpec((tm, width), lambda i: (i, 2)),
                  pl.BlockSpec(a_ws.shape, lambda i: (0, 0, 0)),
                  pl.BlockSpec(bs_exp.shape, lambda i: (0, 0)),
                  pl.BlockSpec(vgain.shape, lambda i: (0, 0))],
        out_specs=pl.BlockSpec((tm, width), lambda i: (i, 0)),
        out_shape=jax.ShapeDtypeStruct((n, width), BF16),
        compiler_params=_cparams(("parallel",), 32),
        name="sgu",
    )(p0, p0, p0, a_ws, bs_exp, vgain)


def _diff_prep_kernel(q_ref, k_ref, v_ref, c_ref, s1_ref, s2_ref, qg_ref, kg_ref,
                      qo_ref, ko_ref, vto_ref):
    half = B_QK_DIM // 4 // 2
    fold = (B_QK_DIM ** -0.5 * LOG2E, 1.0)
    for src, gain, dst, mul in ((q_ref, qg_ref, qo_ref, fold[0]), (k_ref, kg_ref, ko_ref, fold[1])):
        x = src[...]
        xn = x * lax.rsqrt(_group_mean(x * x, B_QK_DIM) + EPS) * gain[...]
        xr = _rope(xn, c_ref[...], s1_ref[...], s2_ref[...], half)
        dst[...] = (xr * mul if mul != 1.0 else xr).astype(BF16)
    vt = v_ref[...].T
    dv = 2 * B_QK_DIM
    ones = jnp.ones((ONES_ROWS, vt.shape[1]), F32)
    parts = []
    for h in range(B_HEADS):
        parts += [vt[h * dv:(h + 1) * dv, :], ones]
    vto_ref[0, 0] = jnp.concatenate(parts, axis=0).astype(BF16)


def _diff_prep(p0, tables, qgain, kgain, batch, seq, tm):
    n = p0.shape[0]
    width = qgain.shape[1]
    tiles_per_seq = seq // tm
    tab_spec = pl.BlockSpec((tm, LANES), lambda i: (i % tiles_per_seq, 0))
    out = jax.ShapeDtypeStruct((n, width), BF16)
    vrows = B_HEADS * (2 * B_QK_DIM + ONES_ROWS)
    out_t = jax.ShapeDtypeStruct((batch, tiles_per_seq, vrows, tm), BF16)
    return pl.pallas_call(
        _diff_prep_kernel,
        grid=(n // tm,),
        in_specs=[pl.BlockSpec((tm, width), lambda i: (i, 3)),
                  pl.BlockSpec((tm, width), lambda i: (i, 4)),
                  pl.BlockSpec((tm, width), lambda i: (i, 5)),
                  tab_spec, tab_spec, tab_spec,
                  pl.BlockSpec((1, width), lambda i: (0, 0)),
                  pl.BlockSpec((1, width), lambda i: (0, 0))],
        out_specs=[pl.BlockSpec((tm, width), lambda i: (i, 0)),
                   pl.BlockSpec((tm, width), lambda i: (i, 0)),
                   pl.BlockSpec((1, 1, vrows, tm),
                                lambda i: (i // tiles_per_seq, i % tiles_per_seq, 0, 0))],
        out_shape=[out, out, out_t],
        compiler_params=_cparams(("parallel",), 32),
        name="diff_prep",
    )(p0, p0, p0, *tables, qgain, kgain)


def _diff_attn_kernel(lam_ref, q_ref, k_ref, vt_ref, g_ref, sub_ref, o_ref, acc_ref, sta_ref, stb_ref,
                      stat_ref, *, tq, lam_init):
    qi = pl.program_id(2)
    dv = 2 * B_QK_DIM
    vrows = dv + ONES_ROWS
    nmap = 4
    q = q_ref[...]
    lane = lax.broadcasted_iota(jnp.int32, (1, LANES), 1)
    zero = jnp.zeros_like(q)
    qm = [jnp.where(lane // B_QK_DIM == j, q, zero) for j in range(nmap)]
    bufs = (sta_ref, stb_ref)
    acc_ref[...] = jnp.zeros_like(acc_ref)
    stat_ref[2 * nmap:3 * nmap, :] = jnp.full((nmap, tq), -jnp.inf, F32)
    krow = lax.broadcasted_iota(jnp.int32, (tq, tq), 0)
    qcol = lax.broadcasted_iota(jnp.int32, (tq, tq), 1)

    def tile_of(entry):
        return jnp.where(entry == 0, qi, entry - 1)

    def score_fn(entry, buf):
        diag = isinstance(entry, int)
        tile = qi if diag else entry - 1
        k = k_ref[pl.ds(pl.multiple_of(tile * tq, tq), tq), :]
        for j in range(nmap):
            row = buf * nmap + j
            stat_ref[row:row + 1, :] = _scores_t(k, qm[j], (krow <= qcol) if diag else None,
                                                 bufs[buf], j)

    def accum_fn(entry, buf):
        tile = tile_of(entry)
        for j in range(nmap):
            vt = vt_ref[0, tile, (j // 2) * vrows:(j // 2 + 1) * vrows, :]
            cm, mr = buf * nmap + j, 2 * nmap + j
            stat_ref[mr:mr + 1, :] = _accumulate_t(bufs[buf], j, stat_ref[cm:cm + 1, :],
                                                   stat_ref[mr:mr + 1, :], acc_ref, j, vt, False)

    _pipelined_sweep(qi + 1, score_fn, accum_fn)

    lam_p = lam_ref[...]
    lam = (jnp.exp(jnp.sum(lam_p[0:1] * lam_p[1:2], axis=-1, keepdims=True))
           - jnp.exp(jnp.sum(lam_p[2:3] * lam_p[3:4], axis=-1, keepdims=True)) + lam_init)
    halves = []
    for h in range(2):
        o = [acc_ref[2 * h + mp, 0:dv, :] / acc_ref[2 * h + mp, dv:dv + 1, :] for mp in range(2)]
        pd = o[0] - lam * o[1]
        ms2 = jnp.mean(pd * pd, axis=0, keepdims=True)
        halves.append(pd * lax.rsqrt(ms2 + EPS))
    ob = jnp.concatenate(halves, axis=0).T * sub_ref[...] * (1.0 - lam_init)
    o_ref[...] = (ob * jax.nn.silu(g_ref[...])).astype(BF16)


def _diff_attn(lam_p, qn, kn, vt, p0, subln, batch, seq, tq, lam_init):
    n, width = qn.shape
    pairs = width // LANES
    qt = seq // tq
    g_off = 3072 // LANES
    vrows = 2 * B_QK_DIM + ONES_ROWS
    kernel = functools.partial(_diff_attn_kernel, tq=tq, lam_init=lam_init)
    return pl.pallas_call(
        kernel,
        grid=(batch, pairs, qt),
        in_specs=[pl.BlockSpec(lam_p.shape, lambda b, p, i: (0, 0)),
                  pl.BlockSpec((tq, LANES), lambda b, p, i: (b * qt + i, p)),
                  pl.BlockSpec((seq, LANES), lambda b, p, i: (b, p)),
                  pl.BlockSpec((1, qt, 2 * vrows, tq), lambda b, p, i: (b, 0, p, 0)),
                  pl.BlockSpec((tq, LANES), lambda b, p, i: (b * qt + i, g_off + p)),
                  pl.BlockSpec((1, LANES), lambda b, p, i: (0, 0))],
        out_specs=pl.BlockSpec((tq, LANES), lambda b, p, i: (b * qt + i, p)),
        out_shape=jax.ShapeDtypeStruct((n, width), BF16),
        scratch_shapes=[pltpu.VMEM((4, vrows, tq), F32), pltpu.VMEM((4, tq, tq), F32),
                        pltpu.VMEM((4, tq, tq), F32), pltpu.VMEM((16, tq), F32)],
        compiler_params=_cparams(("parallel", "parallel", "parallel"), 32),
        name="diff_attn",
    )(lam_p, qn, kn, vt, p0, subln)


def _mem_kv_kernel(mem_ref, g_ref, w_ref, kg_ref, o_ref):
    x = mem_ref[0]
    ms = jnp.mean(x * x, axis=-1, keepdims=True)
    h = (x * lax.rsqrt(ms + EPS) * g_ref[...]).astype(BF16)
    kv = jnp.dot(h, w_ref[...], preferred_element_type=F32)
    mw = kv.shape[1] // 4
    for layer in range(2):
        k = kv[:, (2 * layer) * mw:(2 * layer + 1) * mw]
        kn = k * lax.rsqrt(_group_mean(k * k, HEAD_DIM) + EPS) * kg_ref[layer:layer + 1, :]
        o_ref[2 * layer, 0] = kn.astype(BF16)
        o_ref[2 * layer + 1, 0] = kv[:, (2 * layer + 1) * mw:(2 * layer + 2) * mw].astype(BF16)


def _mem_kv(mem, mem_norm, wkv_both, kgains):
    batch, mtok, d = mem.shape
    mw = wkv_both.shape[1] // 4
    return pl.pallas_call(
        _mem_kv_kernel,
        grid=(batch,),
        in_specs=[pl.BlockSpec((1, mtok, d), lambda b: (b, 0, 0)),
                  pl.BlockSpec((1, d), lambda b: (0, 0)),
                  pl.BlockSpec(wkv_both.shape, lambda b: (0, 0)),
                  pl.BlockSpec(kgains.shape, lambda b: (0, 0))],
        out_specs=pl.BlockSpec((4, 1, mtok, mw), lambda b: (0, b, 0, 0)),
        out_shape=jax.ShapeDtypeStruct((4, batch, mtok, mw), BF16),
        compiler_params=_cparams(("parallel",), 32),
        name="mem_kv",
    )(mem, mem_norm.reshape(1, d), wkv_both, kgains)


def _mem_attn_kernel(q_ref, g_ref, k_ref, v_ref, qg_ref, o_ref):
    scale = HEAD_DIM ** -0.5
    x = q_ref[...]
    width = x.shape[1]
    qn = (x * lax.rsqrt(_group_mean(x * x, HEAD_DIM) + EPS) * qg_ref[...]).astype(BF16)
    k = k_ref[0, 0]
    v = v_ref[0, 0]
    lane = lax.broadcasted_iota(jnp.int32, (1, width), 1)
    zero = jnp.zeros_like(qn)
    out = jnp.zeros(x.shape, F32)
    for h in range(width // HEAD_DIM):
        hmask = lane // HEAD_DIM == h
        s = lax.dot_general(jnp.where(hmask, qn, zero), k, NT_DIMS,
                            preferred_element_type=F32) * scale
        p = jnp.exp(s - jnp.max(s, axis=-1, keepdims=True))
        l = jnp.sum(p, axis=-1, keepdims=True)
        o = jnp.dot(p.astype(BF16), v, preferred_element_type=F32) / l
        out = jnp.where(hmask, o, out)
    o_ref[...] = (out * jax.nn.silu(g_ref[...])).astype(BF16)


def _mem_attn(p, kv4, layer, qgain, q_blk, g_blk, seq, tq):
    n = p.shape[0]
    _, batch, mtok, mw = kv4.shape
    qt = seq // tq
    return pl.pallas_call(
        _mem_attn_kernel,
        grid=(n // tq,),
        in_specs=[pl.BlockSpec((tq, mw), lambda i: (i, q_blk)),
                  pl.BlockSpec((tq, mw), lambda i: (i, g_blk)),
                  pl.BlockSpec((1, 1, mtok, mw), lambda i: (2 * layer, i // qt, 0, 0)),
                  pl.BlockSpec((1, 1, mtok, mw), lambda i: (2 * layer + 1, i // qt, 0, 0)),
                  pl.BlockSpec((1, mw), lambda i: (0, 0))],
        out_specs=pl.BlockSpec((tq, mw), lambda i: (i, 0)),
        out_shape=jax.ShapeDtypeStruct((n, mw), BF16),
        compiler_params=_cparams(("parallel",), 32),
        name="mem_attn",
    )(p, p, kv4, kv4, qgain)


def _conv_kernel(a_ref, b_ref, g_ref, w_ref, cb_ref, ng_ref, o_ref, hbuf_ref, shift_ref):
    t = pl.program_id(1)
    ts = a_ref.shape[0]
    span = ts + HALO

    @pl.when(t == 0)
    def _():
        hbuf_ref[0:HALO, :] = jnp.zeros((HALO, hbuf_ref.shape[1]), F32)

    hbuf_ref[HALO:HALO + ts, :] = a_ref[...] * jax.nn.sigmoid(b_ref[...])
    for ph in range(SUBLANES):
        shift_ref[ph, 0:span - ph, :] = hbuf_ref[ph:span, :]
    first = HALO - (C_KERNEL - 1)
    y = None
    for j in range(C_KERNEL):
        ph, base = (first + j) % SUBLANES, (first + j) // SUBLANES * SUBLANES
        term = shift_ref[ph, base:base + ts, :] * w_ref[j:j + 1, :]
        y = term if y is None else y + term
    y += cb_ref[...]
    ms = jnp.mean(y * y, axis=-1, keepdims=True)
    yn = y * lax.rsqrt(ms + EPS) * ng_ref[...]
    o_ref[...] = (jax.nn.silu(yn) * jax.nn.silu(g_ref[...])).astype(BF16)
    hbuf_ref[0:HALO, :] = hbuf_ref[ts:ts + HALO, :]


def _conv_module(p1, conv_w, conv_b, norm_g, batch, seq, ts):
    n = p1.shape[0]
    width = conv_w.shape[1]
    st = seq // ts
    return pl.pallas_call(
        _conv_kernel,
        grid=(batch, st),
        in_specs=[pl.BlockSpec((ts, width), lambda b, t: (b * st + t, 0)),
                  pl.BlockSpec((ts, width), lambda b, t: (b * st + t, 1)),
                  pl.BlockSpec((ts, width), lambda b, t: (b * st + t, 2)),
                  pl.BlockSpec(conv_w.shape, lambda b, t: (0, 0)),
                  pl.BlockSpec((1, width), lambda b, t: (0, 0)),
                  pl.BlockSpec((1, width), lambda b, t: (0, 0))],
        out_specs=pl.BlockSpec((ts, width), lambda b, t: (b * st + t, 0)),
        out_shape=jax.ShapeDtypeStruct((n, width), BF16),
        scratch_shapes=[pltpu.VMEM((ts + HALO, width), F32),
                        pltpu.VMEM((SUBLANES, ts + HALO, width), F32)],
        compiler_params=_cparams(("arbitrary", "arbitrary"), 32),
        name="conv_module",
    )(p1, p1, p1, conv_w, conv_b.reshape(1, width), norm_g.reshape(1, width))


def _nsa_prep_kernel(q_ref, ks_ref, vs_ref, kw_ref, vw_ref, bg_ref, c_ref, s1_ref, s2_ref,
                     qg_ref, kg_ref, qexp_ref, qo_ref, kso_ref, vsto_ref, kwo_ref, vwto_ref, gto_ref):
    half = HEAD_DIM // 4 // 2
    tabs = (c_ref[...], s1_ref[...], s2_ref[...])
    x = q_ref[...]
    xn = x * lax.rsqrt(_group_mean(x * x, HEAD_DIM) + EPS) * qg_ref[...]
    qr = (_rope(xn, *tabs, half) * (HEAD_DIM ** -0.5 * LOG2E)).astype(BF16)
    qo_ref[...] = jnp.dot(qr, qexp_ref[...], preferred_element_type=F32).astype(BF16)
    for src, dst in ((ks_ref, kso_ref), (kw_ref, kwo_ref)):
        k = src[...]
        kn = k * lax.rsqrt(_group_mean(k * k, HEAD_DIM) + EPS) * kg_ref[...]
        dst[...] = _rope(kn, *tabs, half).astype(BF16)
    ones = jnp.ones((ONES_ROWS, q_ref.shape[0]), F32)
    for src, dst in ((vs_ref, vsto_ref), (vw_ref, vwto_ref)):
        vt = src[...].T
        parts = []
        for g in range(D_KV_GROUPS):
            parts += [vt[g * HEAD_DIM:(g + 1) * HEAD_DIM, :], ones]
        dst[0, 0] = jnp.concatenate(parts, axis=0).astype(BF16)
    gto_ref[0, 0] = jax.nn.sigmoid(bg_ref[...]).T


def _nsa_prep(p1, tables, qgain, kgain, qexp, batch, seq, tm, kv_blk, bg_blk):
    n = p1.shape[0]
    width = qgain.shape[1]
    tiles_per_seq = seq // tm
    tab_spec = pl.BlockSpec((tm, LANES), lambda i: (i % tiles_per_seq, 0))
    row_spec = pl.BlockSpec((tm, LANES), lambda i: (i, 0))
    t_spec = pl.BlockSpec((1, 1, LANES, tm), lambda i: (i // tiles_per_seq, i % tiles_per_seq, 0, 0))
    row_out = jax.ShapeDtypeStruct((n, LANES), BF16)
    vrows = D_KV_GROUPS * (HEAD_DIM + ONES_ROWS)
    vt_spec = pl.BlockSpec((1, 1, vrows, tm), lambda i: (i // tiles_per_seq, i % tiles_per_seq, 0, 0))
    vt_out = jax.ShapeDtypeStruct((batch, tiles_per_seq, vrows, tm), BF16)

    def col(j):
        return pl.BlockSpec((tm, LANES), lambda i: (i, j))

    return pl.pallas_call(
        _nsa_prep_kernel,
        grid=(n // tm,),
        in_specs=[pl.BlockSpec((tm, width), lambda i: (i, 3)),
                  col(kv_blk + 2), col(kv_blk + 3), col(kv_blk + 4), col(kv_blk + 5), col(bg_blk),
                  tab_spec, tab_spec, tab_spec,
                  pl.BlockSpec((1, width), lambda i: (0, 0)),
                  pl.BlockSpec((1, LANES), lambda i: (0, 0)),
                  pl.BlockSpec(qexp.shape, lambda i: (0, 0))],
        out_specs=[pl.BlockSpec((tm, qexp.shape[1]), lambda i: (i, 0)),
                   row_spec, vt_spec, row_spec, vt_spec, t_spec],
        out_shape=[jax.ShapeDtypeStruct((n, qexp.shape[1]), BF16), row_out, vt_out, row_out, vt_out,
                   jax.ShapeDtypeStruct((batch, tiles_per_seq, LANES, tm), F32)],
        compiler_params=_cparams(("parallel",), 32),
        name="nsa_prep",
    )(p1, p1, p1, p1, p1, p1, *tables, qgain, kgain, qexp)


def _compress_kernel(tk_ref, tv_ref, pk_ref, pv_ref, w1k_ref, w2k_ref, w1v_ref, w2v_ref,
                     kg_ref, c_ref, s1_ref, s2_ref, kc_ref, vct_ref):
    half = HEAD_DIM // 4 // 2
    ncp = kc_ref.shape[2]

    def mlp(t_ref, p_ref, w1_ref, w2):
        a = jnp.zeros((ncp, w1_ref.shape[3]), F32)
        b = jnp.zeros((ncp, w1_ref.shape[3]), F32)
        for l in range(CMP_STRIDE):
            x = t_ref[pl.ds(l, ncp, stride=CMP_STRIDE), :]
            a += jnp.dot((x + p_ref[l:l + 1, :]).astype(BF16), w1_ref[0, l],
                         preferred_element_type=F32)
            b += jnp.dot((x + p_ref[CMP_STRIDE + l:CMP_STRIDE + l + 1, :]).astype(BF16),
                         w1_ref[0, CMP_STRIDE + l], preferred_element_type=F32)
        h = a + pltpu.roll(b, ncp - 1, 0)
        return jnp.dot(jax.nn.silu(h).astype(BF16), w2, preferred_element_type=F32)

    kc = mlp(tk_ref, pk_ref, w1k_ref, w2k_ref[0])
    kn = kc * lax.rsqrt(_group_mean(kc * kc, HEAD_DIM) + EPS) * kg_ref[...]
    kc_ref[0, 0] = _rope(kn, c_ref[...], s1_ref[...], s2_ref[...], half).astype(BF16)
    vc = mlp(tv_ref, pv_ref, w1v_ref, w2v_ref[...])
    vct_ref[0, 0] = vc.T[0:HEAD_DIM, :].astype(BF16)


def _compress(p1, k_blk, pk, pv, w1k, w2k, w1v, w2v, kgain, tables, batch, seq):
    groups = w1k.shape[0]
    ncp = seq // CMP_STRIDE

    def full(a):
        return pl.BlockSpec(a.shape, lambda b, g: (0,) * a.ndim)

    def per_group(a):
        return pl.BlockSpec((1,) + a.shape[1:], lambda b, g: (g,) + (0,) * (a.ndim - 1))

    return pl.pallas_call(
        _compress_kernel,
        grid=(batch, groups),
        in_specs=[pl.BlockSpec((seq, LANES), lambda b, g: (b, k_blk)),
                  pl.BlockSpec((seq, LANES), lambda b, g: (b, k_blk + 1)),
                  full(pk), full(pv), per_group(w1k), per_group(w2k), per_group(w1v), full(w2v),
                  full(kgain), full(tables[0]), full(tables[1]), full(tables[2])],
        out_specs=[pl.BlockSpec((1, 1, ncp, LANES), lambda b, g: (b, g, 0, 0)),
                   pl.BlockSpec((1, 1, HEAD_DIM, ncp), lambda b, g: (b, g, 0, 0))],
        out_shape=[jax.ShapeDtypeStruct((batch, groups, ncp, LANES), BF16),
                   jax.ShapeDtypeStruct((batch, groups, HEAD_DIM, ncp), BF16)],
        compiler_params=_cparams(("parallel", "parallel"), 32),
        name="nsa_compress",
    )(p1, p1, pk, pv, w1k, w2k, w1v, w2v, kgain, *tables)


def _nsa_kernel(q_ref, kc_ref, vct_ref, ks_ref, vst_ref, kw_ref, vwt_ref, gt_ref, dg_ref,
                ovt_ref, o_ref, acc_ref, imp_ref, sel_ref, st_ref, stb_ref, stat_ref, cst_ref,
                *, tq, nslc, nsel):
    g = pl.program_id(1)
    qi = pl.program_id(2)
    rheads = D_HEADS // D_KV_GROUPS
    tiny = float(np.finfo(np.float32).tiny)
    q = [q_ref[:, r * LANES:(r + 1) * LANES] for r in range(rheads)]
    qcol = lax.broadcasted_iota(jnp.int32, (tq, tq), 1)
    krow = lax.broadcasted_iota(jnp.int32, (tq, tq), 0)

    kc = kc_ref[0, 0]
    vct = vct_ref[0, 0]
    ncp = kc.shape[0]
    cend = lax.broadcasted_iota(jnp.int32, (ncp, tq), 0) * CMP_STRIDE + (CMP_BLOCK - 1)
    cvalid = cend <= qi * tq + lax.broadcasted_iota(jnp.int32, (ncp, tq), 1)
    psum = jnp.zeros((ncp, tq), F32)
    cmax = [_scores_t(kc, q[r], cvalid, cst_ref, r) for r in range(rheads)]
    for r in range(rheads):
        m = jnp.where(jnp.isfinite(cmax[r]), cmax[r], 0.0)
        e = jnp.exp2(cst_ref[r] - m)
        p = e / jnp.maximum(jnp.sum(e, axis=0, keepdims=True), tiny)
        psum += p
        acc_ref[2 * rheads + r, 0:HEAD_DIM, :] = jnp.dot(vct, p.astype(BF16),
                                                         preferred_element_type=F32)

    hi, lo = _split_bf16(psum)
    ovt = ovt_ref[...]
    imp = (jnp.dot(ovt, hi, preferred_element_type=F32)
           + jnp.dot(ovt, lo, preferred_element_type=F32))
    blk = lax.broadcasted_iota(jnp.int32, (NSLC_PAD, tq), 0)
    cur = (qi * tq + lax.broadcasted_iota(jnp.int32, (NSLC_PAD, tq), 1)) // SLC_BLOCK
    imp = jnp.where(blk > cur, -jnp.inf, imp)
    forced = (blk == 0) | (blk == cur) | (blk == cur - 1)
    imp = jnp.where(forced, jnp.inf, imp)
    imp_ref[...] = imp
    nslab = -(-nslc // SUBLANES)
    slabs = [imp[s * SUBLANES:(s + 1) * SUBLANES, :] for s in range(nslab)]
    ranks = [jnp.zeros((SUBLANES, tq), F32) for _ in range(nslab)]
    sub = lax.broadcasted_iota(jnp.int32, (SUBLANES, tq), 0)
    for jp in range(nslc):
        row = imp_ref[jp:jp + 1, :]
        for s in range(nslab):
            if s * SUBLANES > jp:
                ahead = jnp.where(row >= slabs[s], 1.0, 0.0)
            elif (s + 1) * SUBLANES - 1 < jp:
                ahead = jnp.where(row > slabs[s], 1.0, 0.0)
            else:
                ahead = jnp.where(sub > jp - s * SUBLANES, jnp.where(row >= slabs[s], 1.0, 0.0),
                                  jnp.where(row > slabs[s], 1.0, 0.0))
            ranks[s] = ranks[s] + ahead
    for s in range(nslab):
        sel_ref[s * SUBLANES:(s + 1) * SUBLANES, :] = jnp.where(ranks[s] < nsel, 1.0, 0.0)

    acc_ref[0:2 * rheads] = jnp.zeros((2 * rheads,) + acc_ref.shape[1:], F32)

    blocks_per_tile = tq // SLC_BLOCK
    bufs = (st_ref, stb_ref)
    stat_ref[2 * rheads:3 * rheads, :] = jnp.full((rheads, tq), -jnp.inf, F32)

    def score_fn(entry, buf):
        diag = isinstance(entry, int)
        tile = qi if diag else entry - 1
        k = ks_ref[pl.ds(pl.multiple_of(tile * tq, tq), tq), :]
        chosen = jnp.concatenate(
            [jnp.broadcast_to(sel_ref[pl.ds(tile * blocks_per_tile + c, 1), :], (SLC_BLOCK, tq))
             for c in range(blocks_per_tile)], axis=0)
        valid = chosen > 0.5
        if diag:
            valid = valid & (krow <= qcol)
        for r in range(rheads):
            row = buf * rheads + r
            stat_ref[row:row + 1, :] = _scores_t(k, q[r], valid, bufs[buf], r)

    def accum_fn(entry, buf):
        vt = vst_ref[0, jnp.where(entry == 0, qi, entry - 1)]
        for r in range(rheads):
            cm, mr = buf * rheads + r, 2 * rheads + r
            stat_ref[mr:mr + 1, :] = _accumulate_t(bufs[buf], r, stat_ref[cm:cm + 1, :],
                                                   stat_ref[mr:mr + 1, :], acc_ref, r, vt, True)

    _pipelined_sweep(qi + 1, score_fn, accum_fn)

    first = jnp.maximum(qi - (WINDOW + tq - 1) // tq, 0)
    stat_ref[2 * rheads:3 * rheads, :] = jnp.full((rheads, tq), -jnp.inf, F32)

    def win_score_fn(entry, buf):
        tile = first + entry
        k = kw_ref[pl.ds(pl.multiple_of(tile * tq, tq), tq), :]
        dpos = (qi - tile) * tq + qcol - krow
        valid = (dpos >= 0) & (dpos < WINDOW)
        for r in range(rheads):
            row = buf * rheads + r
            stat_ref[row:row + 1, :] = _scores_t(k, q[r], valid, bufs[buf], r)

    def win_accum_fn(entry, buf):
        vt = vwt_ref[0, first + entry]
        for r in range(rheads):
            cm, mr = buf * rheads + r, 2 * rheads + r
            stat_ref[mr:mr + 1, :] = _accumulate_t(bufs[buf], r, stat_ref[cm:cm + 1, :],
                                                   stat_ref[mr:mr + 1, :], acc_ref, rheads + r, vt, True)

    _pipelined_sweep(qi + 1 - first, win_score_fn, win_accum_fn)

    outs = []
    for r in range(rheads):
        row0 = (g * rheads + r) * 3
        o_c = acc_ref[2 * rheads + r, 0:HEAD_DIM, :]
        den = [jnp.maximum(acc_ref[base + r, HEAD_DIM:HEAD_DIM + 1, :], tiny) for base in (0, rheads)]
        o_s = acc_ref[r, 0:HEAD_DIM, :] / den[0]
        o_w = acc_ref[rheads + r, 0:HEAD_DIM, :] / den[1]
        outs.append(gt_ref[0, 0, pl.ds(row0, 1), :] * o_c + gt_ref[0, 0, pl.ds(row0 + 1, 1), :] * o_s
                    + gt_ref[0, 0, pl.ds(row0 + 2, 1), :] * o_w)
    out = jnp.concatenate(outs, axis=0).T
    o_ref[...] = (out * jax.nn.silu(dg_ref[...])).astype(BF16)


def _nsa_attn(qe, kc, vct, ks, vst, kw, vwt, gt, p1, ovt, batch, seq, tq, dg_blk):
    n = qe.shape[0]
    groups = D_KV_GROUPS
    rheads = D_HEADS // groups
    width = rheads * HEAD_DIM
    qt = seq // tq
    ncp = kc.shape[2]
    nslc = seq // SLC_BLOCK
    kernel = functools.partial(_nsa_kernel, tq=tq, nslc=nslc, nsel=min(SLC_TOPK, nslc))
    k_spec = pl.BlockSpec((seq, LANES), lambda b, g, i: (b, 0))
    vt_spec = pl.BlockSpec((1, qt, HEAD_DIM + ONES_ROWS, tq), lambda b, g, i: (b, 0, g, 0))
    return pl.pallas_call(
        kernel,
        grid=(batch, groups, qt),
        in_specs=[pl.BlockSpec((tq, rheads * LANES), lambda b, g, i: (b * qt + i, g)),
                  pl.BlockSpec((1, 1, ncp, LANES), lambda b, g, i: (b, g, 0, 0)),
                  pl.BlockSpec((1, 1, HEAD_DIM, ncp), lambda b, g, i: (b, g, 0, 0)),
                  k_spec, vt_spec, k_spec, vt_spec,
                  pl.BlockSpec((1, 1, LANES, tq), lambda b, g, i: (b, i, 0, 0)),
                  pl.BlockSpec((tq, width), lambda b, g, i: (b * qt + i, dg_blk + g)),
                  pl.BlockSpec(ovt.shape, lambda b, g, i: (0, 0))],
        out_specs=pl.BlockSpec((tq, width), lambda b, g, i: (b * qt + i, g)),
        out_shape=jax.ShapeDtypeStruct((n, groups * width), BF16),
        scratch_shapes=[pltpu.VMEM((3 * rheads, HEAD_DIM + ONES_ROWS, tq), F32),
                        pltpu.VMEM((NSLC_PAD, tq), F32),
                        pltpu.VMEM((NSLC_PAD, tq), F32),
                        pltpu.VMEM((rheads, tq, tq), F32),
                        pltpu.VMEM((rheads, tq, tq), F32),
                        pltpu.VMEM((4 * rheads, tq), F32),
                        pltpu.VMEM((rheads, ncp, tq), F32)],
        compiler_params=_cparams(("parallel", "parallel", "parallel"), 32),
        name="nsa_attn",
    )(qe, kc, vct, ks, vst, kw, vwt, gt, p1, ovt)


def _rope_tables(pos, d):
    rd = d // 4
    half = rd // 2
    inv = ROPE_THETA ** (-jnp.arange(half, dtype=F32) / half)
    ang = pos.astype(F32)[:, None] * inv[None, :]
    cos, sin = jnp.cos(ang), jnp.sin(ang)
    npos = pos.shape[0]
    zeros = jnp.zeros((npos, d), F32)
    c = jnp.concatenate([cos, cos, jnp.ones((npos, d - rd), F32)], axis=1)
    s1 = zeros.at[:, :half].set(-sin)
    s2 = zeros.at[:, half:rd].set(sin)
    return tuple(jnp.tile(t, (1, LANES // d)) for t in (c, s1, s2))


def _overlap_t(ncp, nslc):
    start = np.arange(ncp) * CMP_STRIDE
    s0 = np.arange(nslc) * SLC_BLOCK
    lo = np.maximum(start[:, None], s0[None, :])
    hi = np.minimum(start[:, None] + CMP_BLOCK, s0[None, :] + SLC_BLOCK)
    ov = np.clip(hi - lo, 0, None) / CMP_BLOCK
    ov[ncp - 1] = 0.0
    out = np.zeros((NSLC_PAD, ncp), np.float32)
    out[:nslc] = ov.T
    return jnp.asarray(out, BF16)


def kernel(x, mem, mem_norm, l0_norm, l0_w_in, l0_a_vnorm, l0_a_ws, l0_a_bs, l0_b_qnorm, l0_b_knorm, l0_b_lq1, l0_b_lk1, l0_b_lq2, l0_b_lk2, l0_b_subln, l0_m_wkv, l0_m_qnorm, l0_m_knorm, l0_w_out, l1_norm, l1_w_in, l1_c_conv_w, l1_c_conv_b, l1_c_norm, l1_d_qnorm, l1_d_knorm, l1_d_cmp_pos_k, l1_d_cmp_w1_k, l1_d_cmp_w2_k, l1_d_cmp_pos_v, l1_d_cmp_w1_v, l1_d_cmp_w2_v, l1_m_wkv, l1_m_qnorm, l1_m_knorm, l1_w_out):
    batch, seq, d_model = x.shape
    n = batch * seq
    tm = 256
    tq = 256
    mw = M_HEADS * HEAD_DIM
    pos = jnp.arange(seq, dtype=jnp.int32)
    x0 = x.reshape(n, d_model)

    wkv_both = jnp.concatenate([l0_m_wkv, l1_m_wkv], axis=1).astype(BF16)
    kgains = jnp.stack([jnp.tile(l0_m_knorm, M_HEADS), jnp.tile(l1_m_knorm, M_HEADS)])
    kv4 = _mem_kv(mem, mem_norm, wkv_both, kgains)

    p0 = _norm_matmul(x0, l0_norm, l0_w_in.astype(BF16), tm)
    a_width = l0_a_vnorm.shape[0]
    bs_exp = jnp.repeat(l0_a_bs.T, a_width // A_GROUPS, axis=1)
    y_a = _sgu(p0, l0_a_ws, bs_exp, l0_a_vnorm.reshape(1, a_width), 512)

    tab32 = _rope_tables(pos, B_QK_DIM)
    b_width = 2 * B_HEADS * B_QK_DIM
    qn, kn, vt = _diff_prep(p0, tab32,
                            jnp.tile(l0_b_qnorm, b_width // B_QK_DIM).reshape(1, b_width),
                            jnp.tile(l0_b_knorm, b_width // B_QK_DIM).reshape(1, b_width),
                            batch, seq, tq)
    lam_init = 0.8 - 0.6 * math.exp(-0.3 * 1)
    lam_p = jnp.stack([l0_b_lq1, l0_b_lk1, l0_b_lq2, l0_b_lk2])
    y_b = _diff_attn(lam_p, qn, kn, vt, p0, jnp.tile(l0_b_subln, 2).reshape(1, LANES),
                     batch, seq, tq, lam_init)
    y_m = _mem_attn(p0, kv4, 0, jnp.tile(l0_m_qnorm, M_HEADS).reshape(1, mw),
                    3584 // mw, 3840 // mw, seq, tq)
    x1 = _out_proj(x0, y_a, y_b, y_m, l0_w_out.astype(BF16), tm)

    w = l1_w_in
    n_bg = 3 * D_HEADS
    w1p = jnp.concatenate([w[:, 0:2048], w[:, 2840:3352], w[:, 2048:2816], w[:, 3352:3864],
                           w[:, 2816:2840], jnp.zeros((d_model, LANES - n_bg), F32)], axis=1)
    p1 = _norm_matmul(x1, l1_norm, w1p.astype(BF16), tm)
    y_c = _conv_module(p1, l1_c_conv_w, l1_c_conv_b, l1_c_norm, batch, seq, 512)

    groups, rheads = D_KV_GROUPS, D_HEADS // D_KV_GROUPS
    d_width = D_HEADS * HEAD_DIM
    kv_blk = 2560 // LANES
    qexp = np.zeros((d_width, D_HEADS * LANES), np.float32)
    for h in range(D_HEADS):
        for dd in range(HEAD_DIM):
            qexp[h * HEAD_DIM + dd, h * LANES + (h // rheads) * HEAD_DIM + dd] = 1.0
    tab64 = _rope_tables(pos, HEAD_DIM)
    qe, ks, vst, kw, vwt, gt = _nsa_prep(
        p1, tab64, jnp.tile(l1_d_qnorm, D_HEADS).reshape(1, d_width),
        jnp.tile(l1_d_knorm, LANES // HEAD_DIM).reshape(1, LANES), jnp.asarray(qexp, BF16),
        batch, seq, tq, kv_blk, 3840 // LANES)

    ncp = seq // CMP_STRIDE
    cmp_pos = jnp.arange(ncp, dtype=jnp.int32) * CMP_STRIDE + (CMP_BLOCK - 1)
    hidden = l1_d_cmp_w2_k.shape[0]

    def group_w1(w1):
        w = w1.reshape(CMP_BLOCK, HEAD_DIM, hidden)
        return jnp.stack([jnp.zeros((CMP_BLOCK, LANES, hidden), F32)
                          .at[:, g * HEAD_DIM:(g + 1) * HEAD_DIM, :].set(w) for g in range(groups)])

    w2k = jnp.stack([jnp.zeros((hidden, LANES), F32).at[:, g * HEAD_DIM:(g + 1) * HEAD_DIM]
                     .set(l1_d_cmp_w2_k) for g in range(groups)])
    w2v = jnp.zeros((hidden, LANES), F32).at[:, :HEAD_DIM].set(l1_d_cmp_w2_v)
    kc, vct = _compress(
        p1, kv_blk, jnp.tile(l1_d_cmp_pos_k, (1, groups)), jnp.tile(l1_d_cmp_pos_v, (1, groups)),
        group_w1(l1_d_cmp_w1_k).astype(BF16), w2k.astype(BF16),
        group_w1(l1_d_cmp_w1_v).astype(BF16), w2v.astype(BF16),
        jnp.tile(l1_d_knorm, LANES // HEAD_DIM).reshape(1, LANES), _rope_tables(cmp_pos, HEAD_DIM),
        batch, seq)

    nslc = seq // SLC_BLOCK
    y_d = _nsa_attn(qe, kc, vct, ks, vst, kw, vwt, gt, p1, _overlap_t(ncp, nslc),
                    batch, seq, tq, 2048 // (rheads * HEAD_DIM))
    y_m1 = _mem_attn(p1, kv4, 1, jnp.tile(l1_m_qnorm, M_HEADS).reshape(1, mw),
                     3328 // mw, 3584 // mw, seq, tq)
    out = _out_proj(x1, y_c, y_d, y_m1, l1_w_out.astype(BF16), tm)
    return out.reshape(batch, seq, d_model)
```

```python
import functools
import math

import numpy as np
import jax
import jax.numpy as jnp
from jax import lax
from jax.experimental import pallas as pl
from jax.experimental.pallas import tpu as pltpu

F32 = jnp.float32
BF16 = jnp.bfloat16

EPS = 1e-6
ROPE_THETA = 500000.0
HEAD_DIM = 64
A_CHUNK = 128
A_GROUPS = 4
B_HEADS = 8
B_QK_DIM = 32
C_KERNEL = 31
D_HEADS = 8
D_KV_GROUPS = 2
CMP_BLOCK = 32
CMP_STRIDE = 16
SLC_BLOCK = 64
SLC_TOPK = 16
WINDOW = 512
M_HEADS = 4

LANES = 128
SUBLANES = 8
MXU_DIM = 256
HALO = 32
NSLC_PAD = 128
ONES_ROWS = 16

NT_DIMS = (((1,), (1,)), ((), ()))
LOG2E = math.log2(math.e)


def _cparams(semantics, vmem_mb):
    return pltpu.CompilerParams(dimension_semantics=semantics,
                                vmem_limit_bytes=vmem_mb * 1024 * 1024)


def _split_bf16(x):
    hi = x.astype(BF16)
    lo = (x - hi.astype(F32)).astype(BF16)
    return hi, lo


def _group_mean(x2, gsize):
    t, c = x2.shape
    w = min(c, MXU_DIM)
    r = lax.broadcasted_iota(jnp.int32, (w, w), 0) // gsize
    cc = lax.broadcasted_iota(jnp.int32, (w, w), 1) // gsize
    ones = jnp.where(r == cc, 1.0, 0.0).astype(BF16)
    outs = []
    for s in range(c // w):
        hi, lo = _split_bf16(x2[:, s * w:(s + 1) * w])
        outs.append(jnp.dot(hi, ones, preferred_element_type=F32)
                    + jnp.dot(lo, ones, preferred_element_type=F32))
    out = outs[0] if len(outs) == 1 else jnp.concatenate(outs, axis=1)
    return out * (1.0 / gsize)


def _tile_lanes(t, width):
    rep = width // t.shape[1]
    return t if rep == 1 else jnp.concatenate([t] * rep, axis=1)


def _rope(x, cos_t, s1_t, s2_t, half):
    width = x.shape[1]
    up = pltpu.roll(x, width - half, 1)
    dn = pltpu.roll(x, half, 1)
    return (x * _tile_lanes(cos_t, width) + up * _tile_lanes(s1_t, width)
            + dn * _tile_lanes(s2_t, width))


def _scores_t(k, q, valid, st_ref, idx):
    st = lax.dot_general(k, q, NT_DIMS, preferred_element_type=F32)
    if valid is not None:
        st = jnp.where(valid, st, -jnp.inf)
    st_ref[idx] = st
    return jnp.max(st, axis=0, keepdims=True)


def _accumulate_t(st_ref, idx, cmax, m, acc_ref, aidx, vt, guard_empty):
    m_new = jnp.maximum(m, cmax)
    m_use = jnp.where(m_new == -jnp.inf, 0.0, m_new) if guard_empty else m_new
    alpha = jnp.exp2(m - m_use)
    pt = jnp.exp2((st_ref[idx] - m_use).astype(BF16))
    acc_ref[aidx] = alpha * acc_ref[aidx] + jnp.dot(vt, pt, preferred_element_type=F32)
    return m_new


def _pipelined_sweep(n, score_fn, accum_fn, unroll=4):
    score_fn(0, 0)
    stages = n - 1

    def stage(e, parity):
        score_fn(e + 1, 1 - parity)
        accum_fn(e, parity)

    def body(i, c):
        for k in range(unroll):
            stage(unroll * i + k, k % 2)
        return c

    if isinstance(n, int):
        iters, left = divmod(stages, unroll)
        if iters:
            lax.fori_loop(0, iters, body, 0)
        for k in range(left):
            stage(iters * unroll + k, k % 2)
        accum_fn(n - 1, left % 2)
        return
    lax.fori_loop(0, stages // unroll, body, 0)
    base = stages // unroll * unroll
    left = stages - base
    for k in range(unroll - 1):
        pl.when(left > k)(functools.partial(stage, base + k, k % 2))
    for parity in range(2):
        pl.when(left % 2 == parity)(functools.partial(accum_fn, n - 1, parity))


def _norm_matmul_kernel(x_ref, g_ref, w_ref, o_ref):
    x = x_ref[...]
    ms = jnp.mean(x * x, axis=-1, keepdims=True)
    h = (x * lax.rsqrt(ms + EPS) * g_ref[...]).astype(BF16)
    o_ref[...] = jnp.dot(h, w_ref[...], preferred_element_type=F32)


def _norm_matmul(x, g, w, tm):
    n, d = x.shape
    c = w.shape[1]
    return pl.pallas_call(
        _norm_matmul_kernel,
        grid=(n // tm,),
        in_specs=[pl.BlockSpec((tm, d), lambda i: (i, 0)),
                  pl.BlockSpec((1, d), lambda i: (0, 0)),
                  pl.BlockSpec((d, c), lambda i: (0, 0))],
        out_specs=pl.BlockSpec((tm, c), lambda i: (i, 0)),
        out_shape=jax.ShapeDtypeStruct((n, c), F32),
        compiler_params=_cparams(("parallel",), 48),
        name="norm_in_proj",
    )(x, g.reshape(1, d), w)


def _out_proj_kernel(x_ref, ya_ref, yb_ref, ym_ref, w_ref, o_ref):
    wa = ya_ref.shape[1]
    wb = yb_ref.shape[1]
    acc = jnp.dot(ya_ref[...], w_ref[0:wa, :], preferred_element_type=F32)
    acc += jnp.dot(yb_ref[...], w_ref[wa:wa + wb, :], preferred_element_type=F32)
    acc += jnp.dot(ym_ref[...], w_ref[wa + wb:, :], preferred_element_type=F32)
    o_ref[...] = x_ref[...] + acc


def _out_proj(x, ya, yb, ym, w, tm):
    n, d = x.shape
    return pl.pallas_call(
        _out_proj_kernel,
        grid=(n // tm,),
        in_specs=[pl.BlockSpec((tm, d), lambda i: (i, 0)),
                  pl.BlockSpec((tm, ya.shape[1]), lambda i: (i, 0)),
                  pl.BlockSpec((tm, yb.shape[1]), lambda i: (i, 0)),
                  pl.BlockSpec((tm, ym.shape[1]), lambda i: (i, 0)),
                  pl.BlockSpec(w.shape, lambda i: (0, 0))],
        out_specs=pl.BlockSpec((tm, d), lambda i: (i, 0)),
        out_shape=jax.ShapeDtypeStruct((n, d), F32),
        compiler_params=_cparams(("parallel",), 32),
        name="out_proj",
    )(x, ya, yb, ym, w)


def _sgu_kernel(u_ref, v_ref, g_ref, ws_ref, bs_ref, vg_ref, o_ref):
    tm = u_ref.shape[0]
    gdim = u_ref.shape[1] // A_GROUPS
    row = lax.broadcasted_iota(jnp.int32, (A_CHUNK, A_CHUNK), 0)
    col = lax.broadcasted_iota(jnp.int32, (A_CHUNK, A_CHUNK), 1)
    causal = col <= row
    for g in range(A_GROUPS):
        cols = slice(g * gdim, (g + 1) * gdim)
        gv = jax.nn.gelu(v_ref[:, cols])
        ms = jnp.mean(gv * gv, axis=-1, keepdims=True)
        vn = (gv * lax.rsqrt(ms + EPS) * vg_ref[:, cols]).astype(BF16)
        w = jnp.where(causal, ws_ref[g], 0.0).astype(BF16)
        for c in range(tm // A_CHUNK):
            rows = slice(c * A_CHUNK, (c + 1) * A_CHUNK)
            z = jnp.dot(w, vn[rows, :], preferred_element_type=F32) + bs_ref[:, cols]
            y = jax.nn.gelu(u_ref[rows, cols]) * z * jax.nn.silu(g_ref[rows, cols])
            o_ref[rows, cols] = y.astype(BF16)


def _sgu(p0, a_ws, bs_exp, vgain, tm):
    n = p0.shape[0]
    width = vgain.shape[1]
    return pl.pallas_call(
        _sgu_kernel,
        grid=(n // tm,),
        in_specs=[pl.BlockSpec((tm, width), lambda i: (i, 0)),
                  pl.BlockSpec((tm, width), lambda i: (i, 1)),
                  pl.BlockSpec((tm, width), lambda i: (i, 2)),
                  pl.BlockSpec(a_ws.shape, lambda i: (0, 0, 0)),
                  pl.BlockSpec(bs_exp.shape, lambda i: (0, 0)),
                  pl.BlockSpec(vgain.shape, lambda i: (0, 0))],
        out_specs=pl.BlockSpec((tm, width), lambda i: (i, 0)),
        out_shape=jax.ShapeDtypeStruct((n, width), BF16),
        compiler_params=_cparams(("parallel",), 32),
        name="sgu",
    )(p0, p0, p0, a_ws, bs_exp, vgain)


def _diff_prep_kernel(q_ref, k_ref, v_ref, c_ref, s1_ref, s2_ref, qg_ref, kg_ref,
                      qo_ref, ko_ref, vto_ref):
    half = B_QK_DIM // 4 // 2
    fold = (B_QK_DIM ** -0.5 * LOG2E, 1.0)
    for src, gain, dst, mul in ((q_ref, qg_ref, qo_ref, fold[0]), (k_ref, kg_ref, ko_ref, fold[1])):
        x = src[...]
        xn = x * lax.rsqrt(_group_mean(x * x, B_QK_DIM) + EPS) * gain[...]
        xr = _rope(xn, c_ref[...], s1_ref[...], s2_ref[...], half)
        dst[...] = (xr * mul if mul != 1.0 else xr).astype(BF16)
    vt = v_ref[...].T
    dv = 2 * B_QK_DIM
    ones = jnp.ones((ONES_ROWS, vt.shape[1]), F32)
    parts = []
    for h in range(B_HEADS):
        parts += [vt[h * dv:(h + 1) * dv, :], ones]
    vto_ref[0, 0] = jnp.concatenate(parts, axis=0).astype(BF16)


def _diff_prep(p0, tables, qgain, kgain, batch, seq, tm):
    n = p0.shape[0]
    width = qgain.shape[1]
    tiles_per_seq = seq // tm
    tab_spec = pl.BlockSpec((tm, LANES), lambda i: (i % tiles_per_seq, 0))
    out = jax.ShapeDtypeStruct((n, width), BF16)
    vrows = B_HEADS * (2 * B_QK_DIM + ONES_ROWS)
    out_t = jax.ShapeDtypeStruct((batch, tiles_per_seq, vrows, tm), BF16)
    return pl.pallas_call(
        _diff_prep_kernel,
        grid=(n // tm,),
        in_specs=[pl.BlockSpec((tm, width), lambda i: (i, 3)),
                  pl.BlockSpec((tm, width), lambda i: (i, 4)),
                  pl.BlockSpec((tm, width), lambda i: (i, 5)),
                  tab_spec, tab_spec, tab_spec,
                  pl.BlockSpec((1, width), lambda i: (0, 0)),
                  pl.BlockSpec((1, width), lambda i: (0, 0))],
        out_specs=[pl.BlockSpec((tm, width), lambda i: (i, 0)),
                   pl.BlockSpec((tm, width), lambda i: (i, 0)),
                   pl.BlockSpec((1, 1, vrows, tm),
                                lambda i: (i // tiles_per_seq, i % tiles_per_seq, 0, 0))],
        out_shape=[out, out, out_t],
        compiler_params=_cparams(("parallel",), 32),
        name="diff_prep",
    )(p0, p0, p0, *tables, qgain, kgain)


def _diff_attn_kernel(qi_tab, tile_tab, lam_ref, q_ref, k_ref, vt_ref, g_ref, sub_ref, o_ref,
                      acc_ref, sta_ref, stb_ref, cmax_ref, m_ref, *, tq, qt, lam_init):
    dv = 2 * B_QK_DIM
    vrows = dv + ONES_ROWS
    nmap = 4
    lane = lax.broadcasted_iota(jnp.int32, (1, LANES), 1)
    bufs = (sta_ref, stb_ref)
    acc_ref[...] = jnp.zeros_like(acc_ref)
    m_ref[...] = jnp.full(m_ref.shape, -jnp.inf, F32)
    krow = lax.broadcasted_iota(jnp.int32, (tq, tq), 0)
    qcol = lax.broadcasted_iota(jnp.int32, (tq, tq), 1)
    lam_p = lam_ref[...]
    lam = (jnp.exp(jnp.sum(lam_p[0:1] * lam_p[1:2], axis=-1, keepdims=True))
           - jnp.exp(jnp.sum(lam_p[2:3] * lam_p[3:4], axis=-1, keepdims=True)) + lam_init)

    def rows(tile):
        return pl.ds(pl.multiple_of(tile * tq, tq), tq)

    def scores(qi, tile, buf, valid):
        q = q_ref[rows(qi), :]
        k = k_ref[rows(tile), :]
        zero = jnp.zeros_like(q)
        for j in range(nmap):
            qm = jnp.where(lane // B_QK_DIM == j, q, zero)
            row = buf * nmap + j
            cmax_ref[row:row + 1, :] = _scores_t(k, qm, valid, bufs[buf], j)

    def accumulate(qi, tile, buf):
        for j in range(nmap):
            vt = vt_ref[0, tile, (j // 2) * vrows:(j // 2 + 1) * vrows, :]
            cm, sr = buf * nmap + j, qi * nmap + j
            m_ref[pl.ds(sr, 1), :] = _accumulate_t(bufs[buf], j, cmax_ref[cm:cm + 1, :],
                                                   m_ref[pl.ds(sr, 1), :], acc_ref, sr, vt, False)

    def finish(qi):
        halves = []
        for h in range(2):
            o = [acc_ref[qi * nmap + 2 * h + mp, 0:dv, :] / acc_ref[qi * nmap + 2 * h + mp, dv:dv + 1, :]
                 for mp in range(2)]
            pd = o[0] - lam * o[1]
            ms2 = jnp.mean(pd * pd, axis=0, keepdims=True)
            halves.append(pd * lax.rsqrt(ms2 + EPS))
        ob = jnp.concatenate(halves, axis=0).T * sub_ref[...] * (1.0 - lam_init)
        o_ref[rows(qi), :] = (ob * jax.nn.silu(g_ref[rows(qi), :])).astype(BF16)

    def below_score(e, buf):
        scores(qi_tab[e], tile_tab[e], buf, None)

    def below_accum(e, buf):
        accumulate(qi_tab[e], tile_tab[e], buf)

    def diag_score(e, buf):
        scores(e, e, buf, krow <= qcol)

    def diag_accum(e, buf):
        accumulate(e, e, buf)
        finish(e)

    _pipelined_sweep(qt * (qt - 1) // 2, below_score, below_accum)
    _pipelined_sweep(qt, diag_score, diag_accum)


def _diff_attn(lam_p, qn, kn, vt, p0, subln, batch, seq, tq, lam_init):
    n, width = qn.shape
    pairs = width // LANES
    qt = seq // tq
    g_off = 3072 // LANES
    vrows = 2 * B_QK_DIM + ONES_ROWS
    below = [(qi, t) for qi in range(qt) for t in range(qi)]
    qi_tab = jnp.asarray([e[0] for e in below], jnp.int32)
    tile_tab = jnp.asarray([e[1] for e in below], jnp.int32)
    kernel = functools.partial(_diff_attn_kernel, tq=tq, qt=qt, lam_init=lam_init)
    seq_spec = pl.BlockSpec((seq, LANES), lambda b, p, *_: (b, p))
    return pl.pallas_call(
        kernel,
        grid_spec=pltpu.PrefetchScalarGridSpec(
            num_scalar_prefetch=2,
            grid=(batch, pairs),
            in_specs=[pl.BlockSpec(lam_p.shape, lambda b, p, *_: (0, 0)),
                      seq_spec, seq_spec,
                      pl.BlockSpec((1, qt, 2 * vrows, tq), lambda b, p, *_: (b, 0, p, 0)),
                      pl.BlockSpec((seq, LANES), lambda b, p, *_: (b, g_off + p)),
                      pl.BlockSpec((1, LANES), lambda b, p, *_: (0, 0))],
            out_specs=seq_spec,
            scratch_shapes=[pltpu.VMEM((qt * 4, vrows, tq), F32), pltpu.VMEM((4, tq, tq), F32),
                            pltpu.VMEM((4, tq, tq), F32), pltpu.VMEM((8, tq), F32),
                            pltpu.VMEM((qt * 4, tq), F32)]),
        out_shape=jax.ShapeDtypeStruct((n, width), BF16),
        compiler_params=_cparams(("parallel", "parallel"), 40),
        name="diff_attn",
    )(qi_tab, tile_tab, lam_p, qn, kn, vt, p0, subln)


def _mem_kv_kernel(mem_ref, g_ref, w_ref, kg_ref, o_ref):
    x = mem_ref[0]
    ms = jnp.mean(x * x, axis=-1, keepdims=True)
    h = (x * lax.rsqrt(ms + EPS) * g_ref[...]).astype(BF16)
    kv = jnp.dot(h, w_ref[...], preferred_element_type=F32)
    mw = kv.shape[1] // 4
    for layer in range(2):
        k = kv[:, (2 * layer) * mw:(2 * layer + 1) * mw]
        kn = k * lax.rsqrt(_group_mean(k * k, HEAD_DIM) + EPS) * kg_ref[layer:layer + 1, :]
        o_ref[2 * layer, 0] = kn.astype(BF16)
        o_ref[2 * layer + 1, 0] = kv[:, (2 * layer + 1) * mw:(2 * layer + 2) * mw].astype(BF16)


def _mem_kv(mem, mem_norm, wkv_both, kgains):
    batch, mtok, d = mem.shape
    mw = wkv_both.shape[1] // 4
    return pl.pallas_call(
        _mem_kv_kernel,
        grid=(batch,),
        in_specs=[pl.BlockSpec((1, mtok, d), lambda b: (b, 0, 0)),
                  pl.BlockSpec((1, d), lambda b: (0, 0)),
                  pl.BlockSpec(wkv_both.shape, lambda b: (0, 0)),
                  pl.BlockSpec(kgains.shape, lambda b: (0, 0))],
        out_specs=pl.BlockSpec((4, 1, mtok, mw), lambda b: (0, b, 0, 0)),
        out_shape=jax.ShapeDtypeStruct((4, batch, mtok, mw), BF16),
        compiler_params=_cparams(("parallel",), 32),
        name="mem_kv",
    )(mem, mem_norm.reshape(1, d), wkv_both, kgains)


def _mem_attn_kernel(q_ref, g_ref, k_ref, v_ref, qg_ref, o_ref):
    scale = HEAD_DIM ** -0.5
    x = q_ref[...]
    width = x.shape[1]
    qn = (x * lax.rsqrt(_group_mean(x * x, HEAD_DIM) + EPS) * qg_ref[...]).astype(BF16)
    k = k_ref[0, 0]
    v = v_ref[0, 0]
    lane = lax.broadcasted_iota(jnp.int32, (1, width), 1)
    zero = jnp.zeros_like(qn)
    out = jnp.zeros(x.shape, F32)
    for h in range(width // HEAD_DIM):
        hmask = lane // HEAD_DIM == h
        s = lax.dot_general(jnp.where(hmask, qn, zero), k, NT_DIMS,
                            preferred_element_type=F32) * scale
        p = jnp.exp(s - jnp.max(s, axis=-1, keepdims=True))
        l = jnp.sum(p, axis=-1, keepdims=True)
        o = jnp.dot(p.astype(BF16), v, preferred_element_type=F32) / l
        out = jnp.where(hmask, o, out)
    o_ref[...] = (out * jax.nn.silu(g_ref[...])).astype(BF16)


def _mem_attn(p, kv4, layer, qgain, q_blk, g_blk, seq, tq):
    n = p.shape[0]
    _, batch, mtok, mw = kv4.shape
    qt = seq // tq
    return pl.pallas_call(
        _mem_attn_kernel,
        grid=(n // tq,),
        in_specs=[pl.BlockSpec((tq, mw), lambda i: (i, q_blk)),
                  pl.BlockSpec((tq, mw), lambda i: (i, g_blk)),
                  pl.BlockSpec((1, 1, mtok, mw), lambda i: (2 * layer, i // qt, 0, 0)),
                  pl.BlockSpec((1, 1, mtok, mw), lambda i: (2 * layer + 1, i // qt, 0, 0)),
                  pl.BlockSpec((1, mw), lambda i: (0, 0))],
        out_specs=pl.BlockSpec((tq, mw), lambda i: (i, 0)),
        out_shape=jax.ShapeDtypeStruct((n, mw), BF16),
        compiler_params=_cparams(("parallel",), 32),
        name="mem_attn",
    )(p, p, kv4, kv4, qgain)


def _conv_kernel(a_ref, b_ref, g_ref, w_ref, cb_ref, ng_ref, o_ref, hbuf_ref, shift_ref):
    t = pl.program_id(1)
    ts = a_ref.shape[0]
    span = ts + HALO

    @pl.when(t == 0)
    def _():
        hbuf_ref[0:HALO, :] = jnp.zeros((HALO, hbuf_ref.shape[1]), F32)

    hbuf_ref[HALO:HALO + ts, :] = a_ref[...] * jax.nn.sigmoid(b_ref[...])
    for ph in range(SUBLANES):
        shift_ref[ph, 0:span - ph, :] = hbuf_ref[ph:span, :]
    first = HALO - (C_KERNEL - 1)
    y = None
    for j in range(C_KERNEL):
        ph, base = (first + j) % SUBLANES, (first + j) // SUBLANES * SUBLANES
        term = shift_ref[ph, base:base + ts, :] * w_ref[j:j + 1, :]
        y = term if y is None else y + term
    y += cb_ref[...]
    ms = jnp.mean(y * y, axis=-1, keepdims=True)
    yn = y * lax.rsqrt(ms + EPS) * ng_ref[...]
    o_ref[...] = (jax.nn.silu(yn) * jax.nn.silu(g_ref[...])).astype(BF16)
    hbuf_ref[0:HALO, :] = hbuf_ref[ts:ts + HALO, :]


def _conv_module(p1, conv_w, conv_b, norm_g, batch, seq, ts):
    n = p1.shape[0]
    width = conv_w.shape[1]
    st = seq // ts
    return pl.pallas_call(
        _conv_kernel,
        grid=(batch, st),
        in_specs=[pl.BlockSpec((ts, width), lambda b, t: (b * st + t, 0)),
                  pl.BlockSpec((ts, width), lambda b, t: (b * st + t, 1)),
                  pl.BlockSpec((ts, width), lambda b, t: (b * st + t, 2)),
                  pl.BlockSpec(conv_w.shape, lambda b, t: (0, 0)),
                  pl.BlockSpec((1, width), lambda b, t: (0, 0)),
                  pl.BlockSpec((1, width), lambda b, t: (0, 0))],
        out_specs=pl.BlockSpec((ts, width), lambda b, t: (b * st + t, 0)),
        out_shape=jax.ShapeDtypeStruct((n, width), BF16),
        scratch_shapes=[pltpu.VMEM((ts + HALO, width), F32),
                        pltpu.VMEM((SUBLANES, ts + HALO, width), F32)],
        compiler_params=_cparams(("arbitrary", "arbitrary"), 32),
        name="conv_module",
    )(p1, p1, p1, conv_w, conv_b.reshape(1, width), norm_g.reshape(1, width))


def _nsa_prep_kernel(q_ref, ks_ref, vs_ref, kw_ref, vw_ref, bg_ref, c_ref, s1_ref, s2_ref,
                     qg_ref, kg_ref, qexp_ref, qo_ref, kso_ref, vsto_ref, kwo_ref, vwto_ref, gto_ref):
    half = HEAD_DIM // 4 // 2
    tabs = (c_ref[...], s1_ref[...], s2_ref[...])
    x = q_ref[...]
    xn = x * lax.rsqrt(_group_mean(x * x, HEAD_DIM) + EPS) * qg_ref[...]
    qr = (_rope(xn, *tabs, half) * (HEAD_DIM ** -0.5 * LOG2E)).astype(BF16)
    qo_ref[...] = jnp.dot(qr, qexp_ref[...], preferred_element_type=F32).astype(BF16)
    for src, dst in ((ks_ref, kso_ref), (kw_ref, kwo_ref)):
        k = src[...]
        kn = k * lax.rsqrt(_group_mean(k * k, HEAD_DIM) + EPS) * kg_ref[...]
        dst[...] = _rope(kn, *tabs, half).astype(BF16)
    ones = jnp.ones((ONES_ROWS, q_ref.shape[0]), F32)
    for src, dst in ((vs_ref, vsto_ref), (vw_ref, vwto_ref)):
        vt = src[...].T
        parts = []
        for g in range(D_KV_GROUPS):
            parts += [vt[g * HEAD_DIM:(g + 1) * HEAD_DIM, :], ones]
        dst[0, 0] = jnp.concatenate(parts, axis=0).astype(BF16)
    gto_ref[0, 0] = jax.nn.sigmoid(bg_ref[...]).T


def _nsa_prep(p1, tables, qgain, kgain, qexp, batch, seq, tm, kv_blk, bg_blk):
    n = p1.shape[0]
    width = qgain.shape[1]
    tiles_per_seq = seq // tm
    tab_spec = pl.BlockSpec((tm, LANES), lambda i: (i % tiles_per_seq, 0))
    row_spec = pl.BlockSpec((tm, LANES), lambda i: (i, 0))
    t_spec = pl.BlockSpec((1, 1, LANES, tm), lambda i: (i // tiles_per_seq, i % tiles_per_seq, 0, 0))
    row_out = jax.ShapeDtypeStruct((n, LANES), BF16)
    vrows = D_KV_GROUPS * (HEAD_DIM + ONES_ROWS)
    vt_spec = pl.BlockSpec((1, 1, vrows, tm), lambda i: (i // tiles_per_seq, i % tiles_per_seq, 0, 0))
    vt_out = jax.ShapeDtypeStruct((batch, tiles_per_seq, vrows, tm), BF16)

    def col(j):
        return pl.BlockSpec((tm, LANES), lambda i: (i, j))

    return pl.pallas_call(
        _nsa_prep_kernel,
        grid=(n // tm,),
        in_specs=[pl.BlockSpec((tm, width), lambda i: (i, 3)),
                  col(kv_blk + 2), col(kv_blk + 3), col(kv_blk + 4), col(kv_blk + 5), col(bg_blk),
                  tab_spec, tab_spec, tab_spec,
                  pl.BlockSpec((1, width), lambda i: (0, 0)),
                  pl.BlockSpec((1, LANES), lambda i: (0, 0)),
                  pl.BlockSpec(qexp.shape, lambda i: (0, 0))],
        out_specs=[pl.BlockSpec((tm, qexp.shape[1]), lambda i: (i, 0)),
                   row_spec, vt_spec, row_spec, vt_spec, t_spec],
        out_shape=[jax.ShapeDtypeStruct((n, qexp.shape[1]), BF16), row_out, vt_out, row_out, vt_out,
                   jax.ShapeDtypeStruct((batch, tiles_per_seq, LANES, tm), F32)],
        compiler_params=_cparams(("parallel",), 32),
        name="nsa_prep",
    )(p1, p1, p1, p1, p1, p1, *tables, qgain, kgain, qexp)


def _compress_kernel(tk_ref, tv_ref, pk_ref, pv_ref, w1k_ref, w2k_ref, w1v_ref, w2v_ref,
                     kg_ref, c_ref, s1_ref, s2_ref, kc_ref, vct_ref):
    half = HEAD_DIM // 4 // 2
    ncp = kc_ref.shape[2]

    def mlp(t_ref, p_ref, w1_ref, w2):
        a = jnp.zeros((ncp, w1_ref.shape[3]), F32)
        b = jnp.zeros((ncp, w1_ref.shape[3]), F32)
        for l in range(CMP_STRIDE):
            x = t_ref[pl.ds(l, ncp, stride=CMP_STRIDE), :]
            a += jnp.dot((x + p_ref[l:l + 1, :]).astype(BF16), w1_ref[0, l],
                         preferred_element_type=F32)
            b += jnp.dot((x + p_ref[CMP_STRIDE + l:CMP_STRIDE + l + 1, :]).astype(BF16),
                         w1_ref[0, CMP_STRIDE + l], preferred_element_type=F32)
        h = a + pltpu.roll(b, ncp - 1, 0)
        return jnp.dot(jax.nn.silu(h).astype(BF16), w2, preferred_element_type=F32)

    kc = mlp(tk_ref, pk_ref, w1k_ref, w2k_ref[0])
    kn = kc * lax.rsqrt(_group_mean(kc * kc, HEAD_DIM) + EPS) * kg_ref[...]
    kc_ref[0, 0] = _rope(kn, c_ref[...], s1_ref[...], s2_ref[...], half).astype(BF16)
    vc = mlp(tv_ref, pv_ref, w1v_ref, w2v_ref[...])
    vct_ref[0, 0] = vc.T[0:HEAD_DIM, :].astype(BF16)


def _compress(p1, k_blk, pk, pv, w1k, w2k, w1v, w2v, kgain, tables, batch, seq):
    groups = w1k.shape[0]
    ncp = seq // CMP_STRIDE

    def full(a):
        return pl.BlockSpec(a.shape, lambda b, g: (0,) * a.ndim)

    def per_group(a):
        return pl.BlockSpec((1,) + a.shape[1:], lambda b, g: (g,) + (0,) * (a.ndim - 1))

    return pl.pallas_call(
        _compress_kernel,
        grid=(batch, groups),
        in_specs=[pl.BlockSpec((seq, LANES), lambda b, g: (b, k_blk)),
                  pl.BlockSpec((seq, LANES), lambda b, g: (b, k_blk + 1)),
                  full(pk), full(pv), per_group(w1k), per_group(w2k), per_group(w1v), full(w2v),
                  full(kgain), full(tables[0]), full(tables[1]), full(tables[2])],
        out_specs=[pl.BlockSpec((1, 1, ncp, LANES), lambda b, g: (b, g, 0, 0)),
                   pl.BlockSpec((1, 1, HEAD_DIM, ncp), lambda b, g: (b, g, 0, 0))],
        out_shape=[jax.ShapeDtypeStruct((batch, groups, ncp, LANES), BF16),
                   jax.ShapeDtypeStruct((batch, groups, HEAD_DIM, ncp), BF16)],
        compiler_params=_cparams(("parallel", "parallel"), 32),
        name="nsa_compress",
    )(p1, p1, pk, pv, w1k, w2k, w1v, w2v, kgain, *tables)


def _nsa_kernel(q_ref, kc_ref, vct_ref, ks_ref, vst_ref, kw_ref, vwt_ref, gt_ref, dg_ref,
                ovt_ref, o_ref, acc_ref, imp_ref, sel_ref, st_ref, stb_ref, stat_ref, cst_ref,
                *, tq, nslc, nsel):
    g = pl.program_id(1)
    qi = pl.program_id(2)
    rheads = D_HEADS // D_KV_GROUPS
    tiny = float(np.finfo(np.float32).tiny)
    q = [q_ref[:, r * LANES:(r + 1) * LANES] for r in range(rheads)]
    qcol = lax.broadcasted_iota(jnp.int32, (tq, tq), 1)
    krow = lax.broadcasted_iota(jnp.int32, (tq, tq), 0)

    kc = kc_ref[0, 0]
    vct = vct_ref[0, 0]
    ncp = kc.shape[0]
    cend = lax.broadcasted_iota(jnp.int32, (ncp, tq), 0) * CMP_STRIDE + (CMP_BLOCK - 1)
    cvalid = cend <= qi * tq + lax.broadcasted_iota(jnp.int32, (ncp, tq), 1)
    psum = jnp.zeros((ncp, tq), F32)
    cmax = [_scores_t(kc, q[r], cvalid, cst_ref, r) for r in range(rheads)]
    for r in range(rheads):
        m = jnp.where(jnp.isfinite(cmax[r]), cmax[r], 0.0)
        e = jnp.exp2(cst_ref[r] - m)
        p = e / jnp.maximum(jnp.sum(e, axis=0, keepdims=True), tiny)
        psum += p
        acc_ref[2 * rheads + r, 0:HEAD_DIM, :] = jnp.dot(vct, p.astype(BF16),
                                                         preferred_element_type=F32)

    hi, lo = _split_bf16(psum)
    ovt = ovt_ref[...]
    imp = (jnp.dot(ovt, hi, preferred_element_type=F32)
           + jnp.dot(ovt, lo, preferred_element_type=F32))
    blk = lax.broadcasted_iota(jnp.int32, (NSLC_PAD, tq), 0)
    cur = (qi * tq + lax.broadcasted_iota(jnp.int32, (NSLC_PAD, tq), 1)) // SLC_BLOCK
    imp = jnp.where(blk > cur, -jnp.inf, imp)
    forced = (blk == 0) | (blk == cur) | (blk == cur - 1)
    imp = jnp.where(forced, jnp.inf, imp)
    imp_ref[...] = imp
    nslab = -(-nslc // SUBLANES)
    slabs = [imp[s * SUBLANES:(s + 1) * SUBLANES, :] for s in range(nslab)]
    ranks = [jnp.zeros((SUBLANES, tq), F32) for _ in range(nslab)]
    sub = lax.broadcasted_iota(jnp.int32, (SUBLANES, tq), 0)
    for jp in range(nslc):
        row = imp_ref[jp:jp + 1, :]
        for s in range(nslab):
            if s * SUBLANES > jp:
                ahead = jnp.where(row >= slabs[s], 1.0, 0.0)
            elif (s + 1) * SUBLANES - 1 < jp:
                ahead = jnp.where(row > slabs[s], 1.0, 0.0)
            else:
                ahead = jnp.where(sub > jp - s * SUBLANES, jnp.where(row >= slabs[s], 1.0, 0.0),
                                  jnp.where(row > slabs[s], 1.0, 0.0))
            ranks[s] = ranks[s] + ahead
    for s in range(nslab):
        sel_ref[s * SUBLANES:(s + 1) * SUBLANES, :] = jnp.where(ranks[s] < nsel, 1.0, 0.0)

    acc_ref[0:2 * rheads] = jnp.zeros((2 * rheads,) + acc_ref.shape[1:], F32)

    blocks_per_tile = tq // SLC_BLOCK
    bufs = (st_ref, stb_ref)
    stat_ref[2 * rheads:3 * rheads, :] = jnp.full((rheads, tq), -jnp.inf, F32)

    def score_fn(entry, buf):
        diag = isinstance(entry, int)
        tile = qi if diag else entry - 1
        k = ks_ref[pl.ds(pl.multiple_of(tile * tq, tq), tq), :]
        chosen = jnp.concatenate(
            [jnp.broadcast_to(sel_ref[pl.ds(tile * blocks_per_tile + c, 1), :], (SLC_BLOCK, tq))
             for c in range(blocks_per_tile)], axis=0)
        valid = chosen > 0.5
        if diag:
            valid = valid & (krow <= qcol)
        for r in range(rheads):
            row = buf * rheads + r
            stat_ref[row:row + 1, :] = _scores_t(k, q[r], valid, bufs[buf], r)

    def accum_fn(entry, buf):
        vt = vst_ref[0, jnp.where(entry == 0, qi, entry - 1)]
        for r in range(rheads):
            cm, mr = buf * rheads + r, 2 * rheads + r
            stat_ref[mr:mr + 1, :] = _accumulate_t(bufs[buf], r, stat_ref[cm:cm + 1, :],
                                                   stat_ref[mr:mr + 1, :], acc_ref, r, vt, True)

    _pipelined_sweep(qi + 1, score_fn, accum_fn)

    first = jnp.maximum(qi - (WINDOW + tq - 1) // tq, 0)
    stat_ref[2 * rheads:3 * rheads, :] = jnp.full((rheads, tq), -jnp.inf, F32)

    def win_score_fn(entry, buf):
        tile = first + entry
        k = kw_ref[pl.ds(pl.multiple_of(tile * tq, tq), tq), :]
        dpos = (qi - tile) * tq + qcol - krow
        valid = (dpos >= 0) & (dpos < WINDOW)
        for r in range(rheads):
            row = buf * rheads + r
            stat_ref[row:row + 1, :] = _scores_t(k, q[r], valid, bufs[buf], r)

    def win_accum_fn(entry, buf):
        vt = vwt_ref[0, first + entry]
        for r in range(rheads):
            cm, mr = buf * rheads + r, 2 * rheads + r
            stat_ref[mr:mr + 1, :] = _accumulate_t(bufs[buf], r, stat_ref[cm:cm + 1, :],
                                                   stat_ref[mr:mr + 1, :], acc_ref, rheads + r, vt, True)

    _pipelined_sweep(qi + 1 - first, win_score_fn, win_accum_fn)

    outs = []
    for r in range(rheads):
        row0 = (g * rheads + r) * 3
        o_c = acc_ref[2 * rheads + r, 0:HEAD_DIM, :]
        den = [jnp.maximum(acc_ref[base + r, HEAD_DIM:HEAD_DIM + 1, :], tiny) for base in (0, rheads)]
        o_s = acc_ref[r, 0:HEAD_DIM, :] / den[0]
        o_w = acc_ref[rheads + r, 0:HEAD_DIM, :] / den[1]
        outs.append(gt_ref[0, 0, pl.ds(row0, 1), :] * o_c + gt_ref[0, 0, pl.ds(row0 + 1, 1), :] * o_s
                    + gt_ref[0, 0, pl.ds(row0 + 2, 1), :] * o_w)
    out = jnp.concatenate(outs, axis=0).T
    o_ref[...] = (out * jax.nn.silu(dg_ref[...])).astype(BF16)


def _nsa_attn(qe, kc, vct, ks, vst, kw, vwt, gt, p1, ovt, batch, seq, tq, dg_blk):
    n = qe.shape[0]
    groups = D_KV_GROUPS
    rheads = D_HEADS // groups
    width = rheads * HEAD_DIM
    qt = seq // tq
    ncp = kc.shape[2]
    nslc = seq // SLC_BLOCK
    kernel = functools.partial(_nsa_kernel, tq=tq, nslc=nslc, nsel=min(SLC_TOPK, nslc))
    k_spec = pl.BlockSpec((seq, LANES), lambda b, g, i: (b, 0))
    vt_spec = pl.BlockSpec((1, qt, HEAD_DIM + ONES_ROWS, tq), lambda b, g, i: (b, 0, g, 0))
    return pl.pallas_call(
        kernel,
        grid=(batch, groups, qt),
        in_specs=[pl.BlockSpec((tq, rheads * LANES), lambda b, g, i: (b * qt + i, g)),
                  pl.BlockSpec((1, 1, ncp, LANES), lambda b, g, i: (b, g, 0, 0)),
                  pl.BlockSpec((1, 1, HEAD_DIM, ncp), lambda b, g, i: (b, g, 0, 0)),
                  k_spec, vt_spec, k_spec, vt_spec,
                  pl.BlockSpec((1, 1, LANES, tq), lambda b, g, i: (b, i, 0, 0)),
                  pl.BlockSpec((tq, width), lambda b, g, i: (b * qt + i, dg_blk + g)),
                  pl.BlockSpec(ovt.shape, lambda b, g, i: (0, 0))],
        out_specs=pl.BlockSpec((tq, width), lambda b, g, i: (b * qt + i, g)),
        out_shape=jax.ShapeDtypeStruct((n, groups * width), BF16),
        scratch_shapes=[pltpu.VMEM((3 * rheads, HEAD_DIM + ONES_ROWS, tq), F32),
                        pltpu.VMEM((NSLC_PAD, tq), F32),
                        pltpu.VMEM((NSLC_PAD, tq), F32),
                        pltpu.VMEM((rheads, tq, tq), F32),
                        pltpu.VMEM((rheads, tq, tq), F32),
                        pltpu.VMEM((4 * rheads, tq), F32),
                        pltpu.VMEM((rheads, ncp, tq), F32)],
        compiler_params=_cparams(("parallel", "parallel", "parallel"), 32),
        name="nsa_attn",
    )(qe, kc, vct, ks, vst, kw, vwt, gt, p1, ovt)


def _rope_tables(pos, d):
    rd = d // 4
    half = rd // 2
    inv = ROPE_THETA ** (-jnp.arange(half, dtype=F32) / half)
    ang = pos.astype(F32)[:, None] * inv[None, :]
    cos, sin = jnp.cos(ang), jnp.sin(ang)
    npos = pos.shape[0]
    zeros = jnp.zeros((npos, d), F32)
    c = jnp.concatenate([cos, cos, jnp.ones((npos, d - rd), F32)], axis=1)
    s1 = zeros.at[:, :half].set(-sin)
    s2 = zeros.at[:, half:rd].set(sin)
    return tuple(jnp.tile(t, (1, LANES // d)) for t in (c, s1, s2))


def _overlap_t(ncp, nslc):
    start = np.arange(ncp) * CMP_STRIDE
    s0 = np.arange(nslc) * SLC_BLOCK
    lo = np.maximum(start[:, None], s0[None, :])
    hi = np.minimum(start[:, None] + CMP_BLOCK, s0[None, :] + SLC_BLOCK)
    ov = np.clip(hi - lo, 0, None) / CMP_BLOCK
    ov[ncp - 1] = 0.0
    out = np.zeros((NSLC_PAD, ncp), np.float32)
    out[:nslc] = ov.T
    return jnp.asarray(out, BF16)


def kernel(x, mem, mem_norm, l0_norm, l0_w_in, l0_a_vnorm, l0_a_ws, l0_a_bs, l0_b_qnorm, l0_b_knorm, l0_b_lq1, l0_b_lk1, l0_b_lq2, l0_b_lk2, l0_b_subln, l0_m_wkv, l0_m_qnorm, l0_m_knorm, l0_w_out, l1_norm, l1_w_in, l1_c_conv_w, l1_c_conv_b, l1_c_norm, l1_d_qnorm, l1_d_knorm, l1_d_cmp_pos_k, l1_d_cmp_w1_k, l1_d_cmp_w2_k, l1_d_cmp_pos_v, l1_d_cmp_w1_v, l1_d_cmp_w2_v, l1_m_wkv, l1_m_qnorm, l1_m_knorm, l1_w_out):
    batch, seq, d_model = x.shape
    n = batch * seq
    tm = 256
    tq = 256
    mw = M_HEADS * HEAD_DIM
    pos = jnp.arange(seq, dtype=jnp.int32)
    x0 = x.reshape(n, d_model)

    wkv_both = jnp.concatenate([l0_m_wkv, l1_m_wkv], axis=1).astype(BF16)
    kgains = jnp.stack([jnp.tile(l0_m_knorm, M_HEADS), jnp.tile(l1_m_knorm, M_HEADS)])
    kv4 = _mem_kv(mem, mem_norm, wkv_both, kgains)

    p0 = _norm_matmul(x0, l0_norm, l0_w_in.astype(BF16), tm)
    a_width = l0_a_vnorm.shape[0]
    bs_exp = jnp.repeat(l0_a_bs.T, a_width // A_GROUPS, axis=1)
    y_a = _sgu(p0, l0_a_ws, bs_exp, l0_a_vnorm.reshape(1, a_width), 512)

    tab32 = _rope_tables(pos, B_QK_DIM)
    b_width = 2 * B_HEADS * B_QK_DIM
    qn, kn, vt = _diff_prep(p0, tab32,
                            jnp.tile(l0_b_qnorm, b_width // B_QK_DIM).reshape(1, b_width),
                            jnp.tile(l0_b_knorm, b_width // B_QK_DIM).reshape(1, b_width),
                            batch, seq, tq)
    lam_init = 0.8 - 0.6 * math.exp(-0.3 * 1)
    lam_p = jnp.stack([l0_b_lq1, l0_b_lk1, l0_b_lq2, l0_b_lk2])
    y_b = _diff_attn(lam_p, qn, kn, vt, p0, jnp.tile(l0_b_subln, 2).reshape(1, LANES),
                     batch, seq, tq, lam_init)
    y_m = _mem_attn(p0, kv4, 0, jnp.tile(l0_m_qnorm, M_HEADS).reshape(1, mw),
                    3584 // mw, 3840 // mw, seq, tq)
    x1 = _out_proj(x0, y_a, y_b, y_m, l0_w_out.astype(BF16), tm)

    w = l1_w_in
    n_bg = 3 * D_HEADS
    w1p = jnp.concatenate([w[:, 0:2048], w[:, 2840:3352], w[:, 2048:2816], w[:, 3352:3864],
                           w[:, 2816:2840], jnp.zeros((d_model, LANES - n_bg), F32)], axis=1)
    p1 = _norm_matmul(x1, l1_norm, w1p.astype(BF16), tm)
    y_c = _conv_module(p1, l1_c_conv_w, l1_c_conv_b, l1_c_norm, batch, seq, 512)

    groups, rheads = D_KV_GROUPS, D_HEADS // D_KV_GROUPS
    d_width = D_HEADS * HEAD_DIM
    kv_blk = 2560 // LANES
    qexp = np.zeros((d_width, D_HEADS * LANES), np.float32)
    for h in range(D_HEADS):
        for dd in range(HEAD_DIM):
            qexp[h * HEAD_DIM + dd, h * LANES + (h // rheads) * HEAD_DIM + dd] = 1.0
    tab64 = _rope_tables(pos, HEAD_DIM)
    qe, ks, vst, kw, vwt, gt = _nsa_prep(
        p1, tab64, jnp.tile(l1_d_qnorm, D_HEADS).reshape(1, d_width),
        jnp.tile(l1_d_knorm, LANES // HEAD_DIM).reshape(1, LANES), jnp.asarray(qexp, BF16),
        batch, seq, tq, kv_blk, 3840 // LANES)

    ncp = seq // CMP_STRIDE
    cmp_pos = jnp.arange(ncp, dtype=jnp.int32) * CMP_STRIDE + (CMP_BLOCK - 1)
    hidden = l1_d_cmp_w2_k.shape[0]

    def group_w1(w1):
        w = w1.reshape(CMP_BLOCK, HEAD_DIM, hidden)
        return jnp.stack([jnp.zeros((CMP_BLOCK, LANES, hidden), F32)
                          .at[:, g * HEAD_DIM:(g + 1) * HEAD_DIM, :].set(w) for g in range(groups)])

    w2k = jnp.stack([jnp.zeros((hidden, LANES), F32).at[:, g * HEAD_DIM:(g + 1) * HEAD_DIM]
                     .set(l1_d_cmp_w2_k) for g in range(groups)])
    w2v = jnp.zeros((hidden, LANES), F32).at[:, :HEAD_DIM].set(l1_d_cmp_w2_v)
    kc, vct = _compress(
        p1, kv_blk, jnp.tile(l1_d_cmp_pos_k, (1, groups)), jnp.tile(l1_d_cmp_pos_v, (1, groups)),
        group_w1(l1_d_cmp_w1_k).astype(BF16), w2k.astype(BF16),
        group_w1(l1_d_cmp_w1_v).astype(BF16), w2v.astype(BF16),
        jnp.tile(l1_d_knorm, LANES // HEAD_DIM).reshape(1, LANES), _rope_tables(cmp_pos, HEAD_DIM),
        batch, seq)

    nslc = seq // SLC_BLOCK
    y_d = _nsa_attn(qe, kc, vct, ks, vst, kw, vwt, gt, p1, _overlap_t(ncp, nslc),
                    batch, seq, tq, 2048 // (rheads * HEAD_DIM))
    y_m1 = _mem_attn(p1, kv4, 1, jnp.tile(l1_m_qnorm, M_HEADS).reshape(1, mw),
                     3328 // mw, 3584 // mw, seq, tq)
    out = _out_proj(x1, y_c, y_d, y_m1, l1_w_out.astype(BF16), tm)
    return out.reshape(batch, seq, d_model)
```

```python
import functools
import math

import numpy as np
import jax
import jax.numpy as jnp
from jax import lax
from jax.experimental import pallas as pl
from jax.experimental.pallas import tpu as pltpu

F32 = jnp.float32
BF16 = jnp.bfloat16

EPS = 1e-6
ROPE_THETA = 500000.0
HEAD_DIM = 64
A_CHUNK = 128
A_GROUPS = 4
B_HEADS = 8
B_QK_DIM = 32
C_KERNEL = 31
D_HEADS = 8
D_KV_GROUPS = 2
CMP_BLOCK = 32
CMP_STRIDE = 16
SLC_BLOCK = 64
SLC_TOPK = 16
WINDOW = 512
M_HEADS = 4

LANES = 128
SUBLANES = 8
MXU_DIM = 256
HALO = 32
NSLC_PAD = 128
ONES_ROWS = 16

NT_DIMS = (((1,), (1,)), ((), ()))
LOG2E = math.log2(math.e)


def _cparams(semantics, vmem_mb):
    return pltpu.CompilerParams(dimension_semantics=semantics,
                                vmem_limit_bytes=vmem_mb * 1024 * 1024)


def _split_bf16(x):
    hi = x.astype(BF16)
    lo = (x - hi.astype(F32)).astype(BF16)
    return hi, lo


def _group_mean(x2, gsize):
    t, c = x2.shape
    w = min(c, MXU_DIM)
    r = lax.broadcasted_iota(jnp.int32, (w, w), 0) // gsize
    cc = lax.broadcasted_iota(jnp.int32, (w, w), 1) // gsize
    ones = jnp.where(r == cc, 1.0, 0.0).astype(BF16)
    outs = []
    for s in range(c // w):
        hi, lo = _split_bf16(x2[:, s * w:(s + 1) * w])
        outs.append(jnp.dot(hi, ones, preferred_element_type=F32)
                    + jnp.dot(lo, ones, preferred_element_type=F32))
    out = outs[0] if len(outs) == 1 else jnp.concatenate(outs, axis=1)
    return out * (1.0 / gsize)


def _tile_lanes(t, width):
    rep = width // t.shape[1]
    return t if rep == 1 else jnp.concatenate([t] * rep, axis=1)


def _rope(x, cos_t, s1_t, s2_t, half):
    width = x.shape[1]
    up = pltpu.roll(x, width - half, 1)
    dn = pltpu.roll(x, half, 1)
    return (x * _tile_lanes(cos_t, width) + up * _tile_lanes(s1_t, width)
            + dn * _tile_lanes(s2_t, width))


def _scores_t(k, q, valid, st_ref, idx):
    st = lax.dot_general(k, q, NT_DIMS, preferred_element_type=F32)
    if valid is not None:
        st = jnp.where(valid, st, -jnp.inf)
    st_ref[idx] = st
    return jnp.max(st, axis=0, keepdims=True)


def _accumulate_t(st_ref, idx, cmax, m, acc_ref, aidx, vt, guard_empty):
    m_new = jnp.maximum(m, cmax)
    m_use = jnp.where(m_new == -jnp.inf, 0.0, m_new) if guard_empty else m_new
    alpha = jnp.exp2(m - m_use)
    pt = jnp.exp2((st_ref[idx] - m_use).astype(BF16))
    acc_ref[aidx] = alpha * acc_ref[aidx] + jnp.dot(vt, pt, preferred_element_type=F32)
    return m_new


def _pipelined_sweep(n, score_fn, accum_fn, unroll=4):
    score_fn(0, 0)
    stages = n - 1

    def stage(e, parity):
        score_fn(e + 1, 1 - parity)
        accum_fn(e, parity)

    def body(i, c):
        for k in range(unroll):
            stage(unroll * i + k, k % 2)
        return c

    if isinstance(n, int):
        iters, left = divmod(stages, unroll)
        if iters:
            lax.fori_loop(0, iters, body, 0)
        for k in range(left):
            stage(iters * unroll + k, k % 2)
        accum_fn(n - 1, left % 2)
        return
    lax.fori_loop(0, stages // unroll, body, 0)
    base = stages // unroll * unroll
    left = stages - base
    for k in range(unroll - 1):
        pl.when(left > k)(functools.partial(stage, base + k, k % 2))
    for parity in range(2):
        pl.when(left % 2 == parity)(functools.partial(accum_fn, n - 1, parity))


def _norm_matmul_kernel(x_ref, g_ref, w_ref, o_ref):
    x = x_ref[...]
    ms = jnp.mean(x * x, axis=-1, keepdims=True)
    h = (x * lax.rsqrt(ms + EPS) * g_ref[...]).astype(BF16)
    o_ref[...] = jnp.dot(h, w_ref[...], preferred_element_type=F32)


def _norm_matmul(x, g, w, tm):
    n, d = x.shape
    c = w.shape[1]
    return pl.pallas_call(
        _norm_matmul_kernel,
        grid=(n // tm,),
        in_specs=[pl.BlockSpec((tm, d), lambda i: (i, 0)),
                  pl.BlockSpec((1, d), lambda i: (0, 0)),
                  pl.BlockSpec((d, c), lambda i: (0, 0))],
        out_specs=pl.BlockSpec((tm, c), lambda i: (i, 0)),
        out_shape=jax.ShapeDtypeStruct((n, c), F32),
        compiler_params=_cparams(("parallel",), 48),
        name="norm_in_proj",
    )(x, g.reshape(1, d), w)


def _out_proj_kernel(x_ref, ya_ref, yb_ref, ym_ref, w_ref, o_ref):
    wa = ya_ref.shape[1]
    wb = yb_ref.shape[1]
    acc = jnp.dot(ya_ref[...], w_ref[0:wa, :], preferred_element_type=F32)
    acc += jnp.dot(yb_ref[...], w_ref[wa:wa + wb, :], preferred_element_type=F32)
    acc += jnp.dot(ym_ref[...], w_ref[wa + wb:, :], preferred_element_type=F32)
    o_ref[...] = x_ref[...] + acc


def _out_proj(x, ya, yb, ym, w, tm):
    n, d = x.shape
    return pl.pallas_call(
        _out_proj_kernel,
        grid=(n // tm,),
        in_specs=[pl.BlockSpec((tm, d), lambda i: (i, 0)),
                  pl.BlockSpec((tm, ya.shape[1]), lambda i: (i, 0)),
                  pl.BlockSpec((tm, yb.shape[1]), lambda i: (i, 0)),
                  pl.BlockSpec((tm, ym.shape[1]), lambda i: (i, 0)),
                  pl.BlockSpec(w.shape, lambda i: (0, 0))],
        out_specs=pl.BlockSpec((tm, d), lambda i: (i, 0)),
        out_shape=jax.ShapeDtypeStruct((n, d), F32),
        compiler_params=_cparams(("parallel",), 32),
        name="out_proj",
    )(x, ya, yb, ym, w)


def _sgu_kernel(u_ref, v_ref, g_ref, ws_ref, bs_ref, vg_ref, o_ref):
    tm = u_ref.shape[0]
    gdim = u_ref.shape[1] // A_GROUPS
    row = lax.broadcasted_iota(jnp.int32, (A_CHUNK, A_CHUNK), 0)
    col = lax.broadcasted_iota(jnp.int32, (A_CHUNK, A_CHUNK), 1)
    causal = col <= row
    for g in range(A_GROUPS):
        cols = slice(g * gdim, (g + 1) * gdim)
        gv = jax.nn.gelu(v_ref[:, cols])
        ms = jnp.mean(gv * gv, axis=-1, keepdims=True)
        vn = (gv * lax.rsqrt(ms + EPS) * vg_ref[:, cols]).astype(BF16)
        w = jnp.where(causal, ws_ref[g], 0.0).astype(BF16)
        for c in range(tm // A_CHUNK):
            rows = slice(c * A_CHUNK, (c + 1) * A_CHUNK)
            z = jnp.dot(w, vn[rows, :], preferred_element_type=F32) + bs_ref[:, cols]
            y = jax.nn.gelu(u_ref[rows, cols]) * z * jax.nn.silu(g_ref[rows, cols])
            o_ref[rows, cols] = y.astype(BF16)


def _sgu(p0, a_ws, bs_exp, vgain, tm):
    n = p0.shape[0]
    width = vgain.shape[1]
    return pl.pallas_call(
        _sgu_kernel,
        grid=(n // tm,),
        in_specs=[pl.BlockSpec((tm, width), lambda i: (i, 0)),
                  pl.BlockSpec((tm, width), lambda i: (i, 1)),
                  pl.BlockSpec((tm, width), lambda i: (i, 2)),
                  pl.BlockSpec(a_ws.shape, lambda i: (0, 0, 0)),
                  pl.BlockSpec(bs_exp.shape, lambda i: (0, 0)),
                  pl.BlockSpec(vgain.shape, lambda i: (0, 0))],
        out_specs=pl.BlockSpec((tm, width), lambda i: (i, 0)),
        out_shape=jax.ShapeDtypeStruct((n, width), BF16),
        compiler_params=_cparams(("parallel",), 32),
        name="sgu",
    )(p0, p0, p0, a_ws, bs_exp, vgain)


def _diff_prep_kernel(q_ref, k_ref, v_ref, c_ref, s1_ref, s2_ref, qg_ref, kg_ref,
                      qo_ref, ko_ref, vto_ref):
    half = B_QK_DIM // 4 // 2
    fold = (B_QK_DIM ** -0.5 * LOG2E, 1.0)
    for src, gain, dst, mul in ((q_ref, qg_ref, qo_ref, fold[0]), (k_ref, kg_ref, ko_ref, fold[1])):
        x = src[...]
        xn = x * lax.rsqrt(_group_mean(x * x, B_QK_DIM) + EPS) * gain[...]
        xr = _rope(xn, c_ref[...], s1_ref[...], s2_ref[...], half)
        dst[...] = (xr * mul if mul != 1.0 else xr).astype(BF16)
    vt = v_ref[...].T
    dv = 2 * B_QK_DIM
    ones = jnp.ones((ONES_ROWS, vt.shape[1]), F32)
    parts = []
    for h in range(B_HEADS):
        parts += [vt[h * dv:(h + 1) * dv, :], ones]
    vto_ref[0, 0] = jnp.concatenate(parts, axis=0).astype(BF16)


def _diff_prep(p0, tables, qgain, kgain, batch, seq, tm):
    n = p0.shape[0]
    width = qgain.shape[1]
    tiles_per_seq = seq // tm
    tab_spec = pl.BlockSpec((tm, LANES), lambda i: (i % tiles_per_seq, 0))
    out = jax.ShapeDtypeStruct((n, width), BF16)
    vrows = B_HEADS * (2 * B_QK_DIM + ONES_ROWS)
    out_t = jax.ShapeDtypeStruct((batch, tiles_per_seq, vrows, tm), BF16)
    return pl.pallas_call(
        _diff_prep_kernel,
        grid=(n // tm,),
        in_specs=[pl.BlockSpec((tm, width), lambda i: (i, 3)),
                  pl.BlockSpec((tm, width), lambda i: (i, 4)),
                  pl.BlockSpec((tm, width), lambda i: (i, 5)),
                  tab_spec, tab_spec, tab_spec,
                  pl.BlockSpec((1, width), lambda i: (0, 0)),
                  pl.BlockSpec((1, width), lambda i: (0, 0))],
        out_specs=[pl.BlockSpec((tm, width), lambda i: (i, 0)),
                   pl.BlockSpec((tm, width), lambda i: (i, 0)),
                   pl.BlockSpec((1, 1, vrows, tm),
                                lambda i: (i // tiles_per_seq, i % tiles_per_seq, 0, 0))],
        out_shape=[out, out, out_t],
        compiler_params=_cparams(("parallel",), 32),
        name="diff_prep",
    )(p0, p0, p0, *tables, qgain, kgain)


def _diff_attn_kernel(qi_tab, tile_tab, lam_ref, q_ref, k_ref, vt_ref, g_ref, sub_ref, o_ref,
                      acc_ref, sta_ref, stb_ref, cmax_ref, m_ref, *, tq, qt, lam_init):
    dv = 2 * B_QK_DIM
    vrows = dv + ONES_ROWS
    nmap = 4
    lane = lax.broadcasted_iota(jnp.int32, (1, LANES), 1)
    bufs = (sta_ref, stb_ref)
    acc_ref[...] = jnp.zeros_like(acc_ref)
    m_ref[...] = jnp.full(m_ref.shape, -jnp.inf, F32)
    krow = lax.broadcasted_iota(jnp.int32, (tq, tq), 0)
    qcol = lax.broadcasted_iota(jnp.int32, (tq, tq), 1)
    lam_p = lam_ref[...]
    lam = (jnp.exp(jnp.sum(lam_p[0:1] * lam_p[1:2], axis=-1, keepdims=True))
           - jnp.exp(jnp.sum(lam_p[2:3] * lam_p[3:4], axis=-1, keepdims=True)) + lam_init)

    def rows(tile):
        return pl.ds(pl.multiple_of(tile * tq, tq), tq)

    def scores(qi, tile, buf, valid):
        q = q_ref[rows(qi), :]
        k = k_ref[rows(tile), :]
        zero = jnp.zeros_like(q)
        for j in range(nmap):
            qm = jnp.where(lane // B_QK_DIM == j, q, zero)
            row = buf * nmap + j
            cmax_ref[row:row + 1, :] = _scores_t(k, qm, valid, bufs[buf], j)

    def accumulate(qi, tile, buf):
        for j in range(nmap):
            vt = vt_ref[0, tile, (j // 2) * vrows:(j // 2 + 1) * vrows, :]
            cm, sr = buf * nmap + j, qi * nmap + j
            m_ref[pl.ds(sr, 1), :] = _accumulate_t(bufs[buf], j, cmax_ref[cm:cm + 1, :],
                                                   m_ref[pl.ds(sr, 1), :], acc_ref, sr, vt, False)

    def finish(qi):
        halves = []
        for h in range(2):
            o = [acc_ref[qi * nmap + 2 * h + mp, 0:dv, :] / acc_ref[qi * nmap + 2 * h + mp, dv:dv + 1, :]
                 for mp in range(2)]
            pd = o[0] - lam * o[1]
            ms2 = jnp.mean(pd * pd, axis=0, keepdims=True)
            halves.append(pd * lax.rsqrt(ms2 + EPS))
        ob = jnp.concatenate(halves, axis=0).T * sub_ref[...] * (1.0 - lam_init)
        o_ref[rows(qi), :] = (ob * jax.nn.silu(g_ref[rows(qi), :])).astype(BF16)

    def below_score(e, buf):
        scores(qi_tab[e], tile_tab[e], buf, None)

    def below_accum(e, buf):
        accumulate(qi_tab[e], tile_tab[e], buf)

    def diag_score(e, buf):
        scores(e, e, buf, krow <= qcol)

    def diag_accum(e, buf):
        accumulate(e, e, buf)
        finish(e)

    _pipelined_sweep(qt * (qt - 1) // 2, below_score, below_accum)
    _pipelined_sweep(qt, diag_score, diag_accum)


def _diff_attn(lam_p, qn, kn, vt, p0, subln, batch, seq, tq, lam_init):
    n, width = qn.shape
    pairs = width // LANES
    qt = seq // tq
    g_off = 3072 // LANES
    vrows = 2 * B_QK_DIM + ONES_ROWS
    below = [(qi, t) for qi in range(qt) for t in range(qi)]
    qi_tab = jnp.asarray([e[0] for e in below], jnp.int32)
    tile_tab = jnp.asarray([e[1] for e in below], jnp.int32)
    kernel = functools.partial(_diff_attn_kernel, tq=tq, qt=qt, lam_init=lam_init)
    seq_spec = pl.BlockSpec((seq, LANES), lambda b, p, *_: (b, p))
    return pl.pallas_call(
        kernel,
        grid_spec=pltpu.PrefetchScalarGridSpec(
            num_scalar_prefetch=2,
            grid=(batch, pairs),
            in_specs=[pl.BlockSpec(lam_p.shape, lambda b, p, *_: (0, 0)),
                      seq_spec, seq_spec,
                      pl.BlockSpec((1, qt, 2 * vrows, tq), lambda b, p, *_: (b, 0, p, 0)),
                      pl.BlockSpec((seq, LANES), lambda b, p, *_: (b, g_off + p)),
                      pl.BlockSpec((1, LANES), lambda b, p, *_: (0, 0))],
            out_specs=seq_spec,
            scratch_shapes=[pltpu.VMEM((qt * 4, vrows, tq), F32), pltpu.VMEM((4, tq, tq), F32),
                            pltpu.VMEM((4, tq, tq), F32), pltpu.VMEM((8, tq), F32),
                            pltpu.VMEM((qt * 4, tq), F32)]),
        out_shape=jax.ShapeDtypeStruct((n, width), BF16),
        compiler_params=_cparams(("parallel", "parallel"), 40),
        name="diff_attn",
    )(qi_tab, tile_tab, lam_p, qn, kn, vt, p0, subln)


def _mem_kv_kernel(mem_ref, g_ref, w_ref, kg_ref, o_ref):
    x = mem_ref[0]
    ms = jnp.mean(x * x, axis=-1, keepdims=True)
    h = (x * lax.rsqrt(ms + EPS) * g_ref[...]).astype(BF16)
    kv = jnp.dot(h, w_ref[...], preferred_element_type=F32)
    mw = kv.shape[1] // 4
    for layer in range(2):
        k = kv[:, (2 * layer) * mw:(2 * layer + 1) * mw]
        kn = k * lax.rsqrt(_group_mean(k * k, HEAD_DIM) + EPS) * kg_ref[layer:layer + 1, :]
        o_ref[2 * layer, 0] = kn.astype(BF16)
        o_ref[2 * layer + 1, 0] = kv[:, (2 * layer + 1) * mw:(2 * layer + 2) * mw].astype(BF16)


def _mem_kv(mem, mem_norm, wkv_both, kgains):
    batch, mtok, d = mem.shape
    mw = wkv_both.shape[1] // 4
    return pl.pallas_call(
        _mem_kv_kernel,
        grid=(batch,),
        in_specs=[pl.BlockSpec((1, mtok, d), lambda b: (b, 0, 0)),
                  pl.BlockSpec((1, d), lambda b: (0, 0)),
                  pl.BlockSpec(wkv_both.shape, lambda b: (0, 0)),
                  pl.BlockSpec(kgains.shape, lambda b: (0, 0))],
        out_specs=pl.BlockSpec((4, 1, mtok, mw), lambda b: (0, b, 0, 0)),
        out_shape=jax.ShapeDtypeStruct((4, batch, mtok, mw), BF16),
        compiler_params=_cparams(("parallel",), 32),
        name="mem_kv",
    )(mem, mem_norm.reshape(1, d), wkv_both, kgains)


def _mem_attn_kernel(q_ref, g_ref, k_ref, v_ref, qg_ref, o_ref):
    scale = HEAD_DIM ** -0.5
    x = q_ref[...]
    width = x.shape[1]
    qn = (x * lax.rsqrt(_group_mean(x * x, HEAD_DIM) + EPS) * qg_ref[...]).astype(BF16)
    k = k_ref[0, 0]
    v = v_ref[0, 0]
    lane = lax.broadcasted_iota(jnp.int32, (1, width), 1)
    zero = jnp.zeros_like(qn)
    out = jnp.zeros(x.shape, F32)
    for h in range(width // HEAD_DIM):
        hmask = lane // HEAD_DIM == h
        s = lax.dot_general(jnp.where(hmask, qn, zero), k, NT_DIMS,
                            preferred_element_type=F32) * scale
        p = jnp.exp(s - jnp.max(s, axis=-1, keepdims=True))
        l = jnp.sum(p, axis=-1, keepdims=True)
        o = jnp.dot(p.astype(BF16), v, preferred_element_type=F32) / l
        out = jnp.where(hmask, o, out)
    o_ref[...] = (out * jax.nn.silu(g_ref[...])).astype(BF16)


def _mem_attn(p, kv4, layer, qgain, q_blk, g_blk, seq, tq):
    n = p.shape[0]
    _, batch, mtok, mw = kv4.shape
    qt = seq // tq
    return pl.pallas_call(
        _mem_attn_kernel,
        grid=(n // tq,),
        in_specs=[pl.BlockSpec((tq, mw), lambda i: (i, q_blk)),
                  pl.BlockSpec((tq, mw), lambda i: (i, g_blk)),
                  pl.BlockSpec((1, 1, mtok, mw), lambda i: (2 * layer, i // qt, 0, 0)),
                  pl.BlockSpec((1, 1, mtok, mw), lambda i: (2 * layer + 1, i // qt, 0, 0)),
                  pl.BlockSpec((1, mw), lambda i: (0, 0))],
        out_specs=pl.BlockSpec((tq, mw), lambda i: (i, 0)),
        out_shape=jax.ShapeDtypeStruct((n, mw), BF16),
        compiler_params=_cparams(("parallel",), 32),
        name="mem_attn",
    )(p, p, kv4, kv4, qgain)


def _conv_kernel(a_ref, b_ref, g_ref, w_ref, cb_ref, ng_ref, o_ref, hbuf_ref, shift_ref):
    t = pl.program_id(1)
    ts = a_ref.shape[0]
    span = ts + HALO

    @pl.when(t == 0)
    def _():
        hbuf_ref[0:HALO, :] = jnp.zeros((HALO, hbuf_ref.shape[1]), F32)

    hbuf_ref[HALO:HALO + ts, :] = a_ref[...] * jax.nn.sigmoid(b_ref[...])
    for ph in range(SUBLANES):
        shift_ref[ph, 0:span - ph, :] = hbuf_ref[ph:span, :]
    first = HALO - (C_KERNEL - 1)
    y = None
    for j in range(C_KERNEL):
        ph, base = (first + j) % SUBLANES, (first + j) // SUBLANES * SUBLANES
        term = shift_ref[ph, base:base + ts, :] * w_ref[j:j + 1, :]
        y = term if y is None else y + term
    y += cb_ref[...]
    ms = jnp.mean(y * y, axis=-1, keepdims=True)
    yn = y * lax.rsqrt(ms + EPS) * ng_ref[...]
    o_ref[...] = (jax.nn.silu(yn) * jax.nn.silu(g_ref[...])).astype(BF16)
    hbuf_ref[0:HALO, :] = hbuf_ref[ts:ts + HALO, :]


def _conv_module(p1, conv_w, conv_b, norm_g, batch, seq, ts):
    n = p1.shape[0]
    width = conv_w.shape[1]
    st = seq // ts
    return pl.pallas_call(
        _conv_kernel,
        grid=(batch, st),
        in_specs=[pl.BlockSpec((ts, width), lambda b, t: (b * st + t, 0)),
                  pl.BlockSpec((ts, width), lambda b, t: (b * st + t, 1)),
                  pl.BlockSpec((ts, width), lambda b, t: (b * st + t, 2)),
                  pl.BlockSpec(conv_w.shape, lambda b, t: (0, 0)),
                  pl.BlockSpec((1, width), lambda b, t: (0, 0)),
                  pl.BlockSpec((1, width), lambda b, t: (0, 0))],
        out_specs=pl.BlockSpec((ts, width), lambda b, t: (b * st + t, 0)),
        out_shape=jax.ShapeDtypeStruct((n, width), BF16),
        scratch_shapes=[pltpu.VMEM((ts + HALO, width), F32),
                        pltpu.VMEM((SUBLANES, ts + HALO, width), F32)],
        compiler_params=_cparams(("arbitrary", "arbitrary"), 32),
        name="conv_module",
    )(p1, p1, p1, conv_w, conv_b.reshape(1, width), norm_g.reshape(1, width))


def _nsa_prep_kernel(q_ref, ks_ref, vs_ref, kw_ref, vw_ref, bg_ref, c_ref, s1_ref, s2_ref,
                     qg_ref, kg_ref, qexp_ref, qo_ref, ko_ref, vto_ref, gto_ref):
    half = HEAD_DIM // 4 // 2
    tabs = (c_ref[...], s1_ref[...], s2_ref[...])
    x = q_ref[...]
    xn = x * lax.rsqrt(_group_mean(x * x, HEAD_DIM) + EPS) * qg_ref[...]
    qr = (_rope(xn, *tabs, half) * (HEAD_DIM ** -0.5 * LOG2E)).astype(BF16)
    qo_ref[...] = jnp.dot(qr, qexp_ref[...], preferred_element_type=F32).astype(BF16)
    for br, src in enumerate((ks_ref, kw_ref)):
        k = src[...]
        kn = k * lax.rsqrt(_group_mean(k * k, HEAD_DIM) + EPS) * kg_ref[...]
        ko_ref[br] = _rope(kn, *tabs, half).astype(BF16)
    ones = jnp.ones((ONES_ROWS, q_ref.shape[0]), F32)
    for br, src in enumerate((vs_ref, vw_ref)):
        vt = src[...].T
        parts = []
        for g in range(D_KV_GROUPS):
            parts += [vt[g * HEAD_DIM:(g + 1) * HEAD_DIM, :], ones]
        vto_ref[0, br, 0] = jnp.concatenate(parts, axis=0).astype(BF16)
    gto_ref[0, 0] = jax.nn.sigmoid(bg_ref[...]).T


def _nsa_prep(p1, tables, qgain, kgain, qexp, batch, seq, tm, kv_blk, bg_blk):
    n = p1.shape[0]
    width = qgain.shape[1]
    tiles_per_seq = seq // tm
    tab_spec = pl.BlockSpec((tm, LANES), lambda i: (i % tiles_per_seq, 0))
    t_spec = pl.BlockSpec((1, 1, LANES, tm), lambda i: (i // tiles_per_seq, i % tiles_per_seq, 0, 0))
    k_spec = pl.BlockSpec((2, tm, LANES), lambda i: (0, i, 0))
    k_out = jax.ShapeDtypeStruct((2, n, LANES), BF16)
    vrows = D_KV_GROUPS * (HEAD_DIM + ONES_ROWS)
    vt_spec = pl.BlockSpec((1, 2, 1, vrows, tm),
                           lambda i: (i // tiles_per_seq, 0, i % tiles_per_seq, 0, 0))
    vt_out = jax.ShapeDtypeStruct((batch, 2, tiles_per_seq, vrows, tm), BF16)

    def col(j):
        return pl.BlockSpec((tm, LANES), lambda i: (i, j))

    return pl.pallas_call(
        _nsa_prep_kernel,
        grid=(n // tm,),
        in_specs=[pl.BlockSpec((tm, width), lambda i: (i, 3)),
                  col(kv_blk + 2), col(kv_blk + 3), col(kv_blk + 4), col(kv_blk + 5), col(bg_blk),
                  tab_spec, tab_spec, tab_spec,
                  pl.BlockSpec((1, width), lambda i: (0, 0)),
                  pl.BlockSpec((1, LANES), lambda i: (0, 0)),
                  pl.BlockSpec(qexp.shape, lambda i: (0, 0))],
        out_specs=[pl.BlockSpec((tm, qexp.shape[1]), lambda i: (i, 0)),
                   k_spec, vt_spec, t_spec],
        out_shape=[jax.ShapeDtypeStruct((n, qexp.shape[1]), BF16), k_out, vt_out,
                   jax.ShapeDtypeStruct((batch, tiles_per_seq, LANES, tm), F32)],
        compiler_params=_cparams(("parallel",), 32),
        name="nsa_prep",
    )(p1, p1, p1, p1, p1, p1, *tables, qgain, kgain, qexp)


def _compress_kernel(tk_ref, tv_ref, pk_ref, pv_ref, w1k_ref, w2k_ref, w1v_ref, w2v_ref,
                     kg_ref, c_ref, s1_ref, s2_ref, kc_ref, vct_ref):
    half = HEAD_DIM // 4 // 2
    ncp = kc_ref.shape[2]

    def mlp(t_ref, p_ref, w1_ref, w2):
        a = jnp.zeros((ncp, w1_ref.shape[3]), F32)
        b = jnp.zeros((ncp, w1_ref.shape[3]), F32)
        for l in range(CMP_STRIDE):
            x = t_ref[pl.ds(l, ncp, stride=CMP_STRIDE), :]
            a += jnp.dot((x + p_ref[l:l + 1, :]).astype(BF16), w1_ref[0, l],
                         preferred_element_type=F32)
            b += jnp.dot((x + p_ref[CMP_STRIDE + l:CMP_STRIDE + l + 1, :]).astype(BF16),
                         w1_ref[0, CMP_STRIDE + l], preferred_element_type=F32)
        h = a + pltpu.roll(b, ncp - 1, 0)
        return jnp.dot(jax.nn.silu(h).astype(BF16), w2, preferred_element_type=F32)

    kc = mlp(tk_ref, pk_ref, w1k_ref, w2k_ref[0])
    kn = kc * lax.rsqrt(_group_mean(kc * kc, HEAD_DIM) + EPS) * kg_ref[...]
    kc_ref[0, 0] = _rope(kn, c_ref[...], s1_ref[...], s2_ref[...], half).astype(BF16)
    vc = mlp(tv_ref, pv_ref, w1v_ref, w2v_ref[...])
    vct_ref[0, 0] = vc.T[0:HEAD_DIM, :].astype(BF16)


def _compress(p1, k_blk, pk, pv, w1k, w2k, w1v, w2v, kgain, tables, batch, seq):
    groups = w1k.shape[0]
    ncp = seq // CMP_STRIDE

    def full(a):
        return pl.BlockSpec(a.shape, lambda b, g: (0,) * a.ndim)

    def per_group(a):
        return pl.BlockSpec((1,) + a.shape[1:], lambda b, g: (g,) + (0,) * (a.ndim - 1))

    return pl.pallas_call(
        _compress_kernel,
        grid=(batch, groups),
        in_specs=[pl.BlockSpec((seq, LANES), lambda b, g: (b, k_blk)),
                  pl.BlockSpec((seq, LANES), lambda b, g: (b, k_blk + 1)),
                  full(pk), full(pv), per_group(w1k), per_group(w2k), per_group(w1v), full(w2v),
                  full(kgain), full(tables[0]), full(tables[1]), full(tables[2])],
        out_specs=[pl.BlockSpec((1, 1, ncp, LANES), lambda b, g: (b, g, 0, 0)),
                   pl.BlockSpec((1, 1, HEAD_DIM, ncp), lambda b, g: (b, g, 0, 0))],
        out_shape=[jax.ShapeDtypeStruct((batch, groups, ncp, LANES), BF16),
                   jax.ShapeDtypeStruct((batch, groups, HEAD_DIM, ncp), BF16)],
        compiler_params=_cparams(("parallel", "parallel"), 32),
        name="nsa_compress",
    )(p1, p1, pk, pv, w1k, w2k, w1v, w2v, kgain, *tables)


def _nsa_kernel(qi_tab, tile_tab, br_tab, q_ref, kc_ref, vct_ref, k_ref, vt_ref, gt_ref,
                dg_ref, ovt_ref, o_ref, acc_ref, oc_ref, sel_ref, imp_ref, sta_ref, stb_ref, cst_ref,
                cmax_ref, m_ref, *, tq, qt, nslc, nsel, nentries):
    g = pl.program_id(1)
    rheads = D_HEADS // D_KV_GROUPS
    seq = qt * tq
    tiny = float(np.finfo(np.float32).tiny)
    qcol = lax.broadcasted_iota(jnp.int32, (tq, tq), 1)
    krow = lax.broadcasted_iota(jnp.int32, (tq, tq), 0)
    bufs = (sta_ref, stb_ref)
    blocks_per_tile = tq // SLC_BLOCK
    nslab = -(-nslc // SUBLANES)
    sel_rows = nslab * SUBLANES

    def rows(tile):
        return pl.ds(pl.multiple_of(tile * tq, tq), tq)

    def heads_q(qi):
        return [q_ref[rows(qi), r * LANES:(r + 1) * LANES] for r in range(rheads)]

    kc = kc_ref[0, 0]
    vct = vct_ref[0, 0]
    ncp = kc.shape[0]
    ovt = ovt_ref[...]

    def select_blocks(qi, c):
        q = heads_q(qi)
        cend = lax.broadcasted_iota(jnp.int32, (ncp, tq), 0) * CMP_STRIDE + (CMP_BLOCK - 1)
        cvalid = cend <= qi * tq + lax.broadcasted_iota(jnp.int32, (ncp, tq), 1)
        psum = jnp.zeros((ncp, tq), F32)
        cmax = [_scores_t(kc, q[r], cvalid, cst_ref, r) for r in range(rheads)]
        for r in range(rheads):
            m = jnp.where(jnp.isfinite(cmax[r]), cmax[r], 0.0)
            e = jnp.exp2(cst_ref[r] - m)
            p = e / jnp.maximum(jnp.sum(e, axis=0, keepdims=True), tiny)
            psum += p
            oc_ref[qi * rheads + r] = jnp.dot(vct, p.astype(BF16), preferred_element_type=F32)
        hi, lo = _split_bf16(psum)
        imp = (jnp.dot(ovt, hi, preferred_element_type=F32)
               + jnp.dot(ovt, lo, preferred_element_type=F32))
        blk = lax.broadcasted_iota(jnp.int32, (NSLC_PAD, tq), 0)
        cur = (qi * tq + lax.broadcasted_iota(jnp.int32, (NSLC_PAD, tq), 1)) // SLC_BLOCK
        imp = jnp.where(blk > cur, -jnp.inf, imp)
        forced = (blk == 0) | (blk == cur) | (blk == cur - 1)
        imp = jnp.where(forced, jnp.inf, imp)
        imp_ref[...] = imp
        slabs = [imp[s * SUBLANES:(s + 1) * SUBLANES, :] for s in range(nslab)]
        ranks = [jnp.zeros((SUBLANES, tq), F32) for _ in range(nslab)]
        sub = lax.broadcasted_iota(jnp.int32, (SUBLANES, tq), 0)
        for jp in range(nslc):
            row = imp_ref[jp:jp + 1, :]
            for s in range(nslab):
                if s * SUBLANES > jp:
                    ahead = jnp.where(row >= slabs[s], 1.0, 0.0)
                elif (s + 1) * SUBLANES - 1 < jp:
                    ahead = jnp.where(row > slabs[s], 1.0, 0.0)
                else:
                    ahead = jnp.where(sub > jp - s * SUBLANES, jnp.where(row >= slabs[s], 1.0, 0.0),
                                      jnp.where(row > slabs[s], 1.0, 0.0))
                ranks[s] = ranks[s] + ahead
        for s in range(nslab):
            start = pl.multiple_of(qi * sel_rows + s * SUBLANES, SUBLANES)
            sel_ref[pl.ds(start, SUBLANES), :] = jnp.where(ranks[s] < nsel, 1.0, 0.0)
        return c

    lax.fori_loop(0, qt, select_blocks, 0)
    acc_ref[...] = jnp.zeros_like(acc_ref)
    m_ref[...] = jnp.full(m_ref.shape, -jnp.inf, F32)

    def score_fn(e, buf):
        qi, tile, br = qi_tab[e], tile_tab[e], br_tab[e]
        q = heads_q(qi)
        k = k_ref[br, rows(tile), :]
        base = qi * sel_rows + tile * blocks_per_tile
        chosen = jnp.concatenate(
            [jnp.broadcast_to(sel_ref[pl.ds(base + c, 1), :], (SLC_BLOCK, tq))
             for c in range(blocks_per_tile)], axis=0)
        dpos = (qi - tile) * tq + (qcol - krow)
        reach = jnp.where(br == 0, seq, WINDOW).astype(jnp.uint32)
        kept = jnp.where(jnp.maximum(chosen, br.astype(F32)) > 0.5, dpos, -1)
        valid = kept.astype(jnp.uint32) < reach
        for r in range(rheads):
            row = buf * rheads + r
            cmax_ref[row:row + 1, :] = _scores_t(k, q[r], valid, bufs[buf], r)

    def accum_fn(e, buf):
        qi, tile, br = qi_tab[e], tile_tab[e], br_tab[e]
        vt = vt_ref[0, br, tile]
        for r in range(rheads):
            cm, sr = buf * rheads + r, (qi * 2 + br) * rheads + r
            m_ref[pl.ds(sr, 1), :] = _accumulate_t(bufs[buf], r, cmax_ref[cm:cm + 1, :],
                                                   m_ref[pl.ds(sr, 1), :], acc_ref, sr, vt, True)

    _pipelined_sweep(nentries, score_fn, accum_fn)

    def finish(qi, c):
        outs = []
        for r in range(rheads):
            row0 = (g * rheads + r) * 3
            o = [oc_ref[qi * rheads + r]]
            for br in range(2):
                sr = (qi * 2 + br) * rheads + r
                o.append(acc_ref[sr, 0:HEAD_DIM, :]
                         / jnp.maximum(acc_ref[sr, HEAD_DIM:HEAD_DIM + 1, :], tiny))
            outs.append(sum(gt_ref[0, qi, pl.ds(row0 + j, 1), :] * o[j] for j in range(3)))
        out = jnp.concatenate(outs, axis=0).T
        o_ref[rows(qi), :] = (out * jax.nn.silu(dg_ref[rows(qi), :])).astype(BF16)
        return c

    lax.fori_loop(0, qt, finish, 0)


def _nsa_attn(qe, kc, vct, kboth, vtboth, gt, p1, ovt, batch, seq, tq, dg_blk):
    n = qe.shape[0]
    groups = D_KV_GROUPS
    rheads = D_HEADS // groups
    width = rheads * HEAD_DIM
    qt = seq // tq
    ncp = kc.shape[2]
    nslc = seq // SLC_BLOCK
    sel_rows = -(-nslc // SUBLANES) * SUBLANES
    wtiles = (WINDOW + tq - 1) // tq
    entries = []
    for qi in range(qt):
        entries += [(qi, t, 0) for t in range(qi + 1)]
        entries += [(qi, t, 1) for t in range(max(qi - wtiles, 0), qi + 1)]
    tabs = [jnp.asarray([e[c] for e in entries], jnp.int32) for c in range(3)]
    kernel = functools.partial(_nsa_kernel, tq=tq, qt=qt, nslc=nslc, nsel=min(SLC_TOPK, nslc),
                               nentries=len(entries))
    return pl.pallas_call(
        kernel,
        grid_spec=pltpu.PrefetchScalarGridSpec(
            num_scalar_prefetch=3,
            grid=(batch, groups),
            in_specs=[pl.BlockSpec((seq, rheads * LANES), lambda b, g, *_: (b, g)),
                      pl.BlockSpec((1, 1, ncp, LANES), lambda b, g, *_: (b, g, 0, 0)),
                      pl.BlockSpec((1, 1, HEAD_DIM, ncp), lambda b, g, *_: (b, g, 0, 0)),
                      pl.BlockSpec((2, seq, LANES), lambda b, g, *_: (0, b, 0)),
                      pl.BlockSpec((1, 2, qt, HEAD_DIM + ONES_ROWS, tq), lambda b, g, *_: (b, 0, 0, g, 0)),
                      pl.BlockSpec((1, qt, LANES, tq), lambda b, g, *_: (b, 0, 0, 0)),
                      pl.BlockSpec((seq, width), lambda b, g, *_: (b, dg_blk + g)),
                      pl.BlockSpec(ovt.shape, lambda b, g, *_: (0, 0))],
            out_specs=pl.BlockSpec((seq, width), lambda b, g, *_: (b, g)),
            scratch_shapes=[pltpu.VMEM((qt * 2 * rheads, HEAD_DIM + ONES_ROWS, tq), F32),
                            pltpu.VMEM((qt * rheads, HEAD_DIM, tq), F32),
                            pltpu.VMEM((qt * sel_rows, tq), F32),
                            pltpu.VMEM((NSLC_PAD, tq), F32),
                            pltpu.VMEM((rheads, tq, tq), F32),
                            pltpu.VMEM((rheads, tq, tq), F32),
                            pltpu.VMEM((rheads, ncp, tq), F32),
                            pltpu.VMEM((2 * rheads, tq), F32),
                            pltpu.VMEM((qt * 2 * rheads, tq), F32)]),
        out_shape=jax.ShapeDtypeStruct((n, groups * width), BF16),
        compiler_params=_cparams(("parallel", "parallel"), 56),
        name="nsa_attn",
    )(*tabs, qe, kc, vct, kboth, vtboth, gt, p1, ovt)


def _rope_tables(pos, d):
    rd = d // 4
    half = rd // 2
    inv = ROPE_THETA ** (-jnp.arange(half, dtype=F32) / half)
    ang = pos.astype(F32)[:, None] * inv[None, :]
    cos, sin = jnp.cos(ang), jnp.sin(ang)
    npos = pos.shape[0]
    zeros = jnp.zeros((npos, d), F32)
    c = jnp.concatenate([cos, cos, jnp.ones((npos, d - rd), F32)], axis=1)
    s1 = zeros.at[:, :half].set(-sin)
    s2 = zeros.at[:, half:rd].set(sin)
    return tuple(jnp.tile(t, (1, LANES // d)) for t in (c, s1, s2))


def _overlap_t(ncp, nslc):
    start = np.arange(ncp) * CMP_STRIDE
    s0 = np.arange(nslc) * SLC_BLOCK
    lo = np.maximum(start[:, None], s0[None, :])
    hi = np.minimum(start[:, None] + CMP_BLOCK, s0[None, :] + SLC_BLOCK)
    ov = np.clip(hi - lo, 0, None) / CMP_BLOCK
    ov[ncp - 1] = 0.0
    out = np.zeros((NSLC_PAD, ncp), np.float32)
    out[:nslc] = ov.T
    return jnp.asarray(out, BF16)


def kernel(x, mem, mem_norm, l0_norm, l0_w_in, l0_a_vnorm, l0_a_ws, l0_a_bs, l0_b_qnorm, l0_b_knorm, l0_b_lq1, l0_b_lk1, l0_b_lq2, l0_b_lk2, l0_b_subln, l0_m_wkv, l0_m_qnorm, l0_m_knorm, l0_w_out, l1_norm, l1_w_in, l1_c_conv_w, l1_c_conv_b, l1_c_norm, l1_d_qnorm, l1_d_knorm, l1_d_cmp_pos_k, l1_d_cmp_w1_k, l1_d_cmp_w2_k, l1_d_cmp_pos_v, l1_d_cmp_w1_v, l1_d_cmp_w2_v, l1_m_wkv, l1_m_qnorm, l1_m_knorm, l1_w_out):
    batch, seq, d_model = x.shape
    n = batch * seq
    tm = 256
    tq = 256
    mw = M_HEADS * HEAD_DIM
    pos = jnp.arange(seq, dtype=jnp.int32)
    x0 = x.reshape(n, d_model)

    wkv_both = jnp.concatenate([l0_m_wkv, l1_m_wkv], axis=1).astype(BF16)
    kgains = jnp.stack([jnp.tile(l0_m_knorm, M_HEADS), jnp.tile(l1_m_knorm, M_HEADS)])
    kv4 = _mem_kv(mem, mem_norm, wkv_both, kgains)

    p0 = _norm_matmul(x0, l0_norm, l0_w_in.astype(BF16), tm)
    a_width = l0_a_vnorm.shape[0]
    bs_exp = jnp.repeat(l0_a_bs.T, a_width // A_GROUPS, axis=1)
    y_a = _sgu(p0, l0_a_ws, bs_exp, l0_a_vnorm.reshape(1, a_width), 512)

    tab32 = _rope_tables(pos, B_QK_DIM)
    b_width = 2 * B_HEADS * B_QK_DIM
    qn, kn, vt = _diff_prep(p0, tab32,
                            jnp.tile(l0_b_qnorm, b_width // B_QK_DIM).reshape(1, b_width),
                            jnp.tile(l0_b_knorm, b_width // B_QK_DIM).reshape(1, b_width),
                            batch, seq, tq)
    lam_init = 0.8 - 0.6 * math.exp(-0.3 * 1)
    lam_p = jnp.stack([l0_b_lq1, l0_b_lk1, l0_b_lq2, l0_b_lk2])
    y_b = _diff_attn(lam_p, qn, kn, vt, p0, jnp.tile(l0_b_subln, 2).reshape(1, LANES),
                     batch, seq, tq, lam_init)
    y_m = _mem_attn(p0, kv4, 0, jnp.tile(l0_m_qnorm, M_HEADS).reshape(1, mw),
                    3584 // mw, 3840 // mw, seq, tq)
    x1 = _out_proj(x0, y_a, y_b, y_m, l0_w_out.astype(BF16), tm)

    w = l1_w_in
    n_bg = 3 * D_HEADS
    w1p = jnp.concatenate([w[:, 0:2048], w[:, 2840:3352], w[:, 2048:2816], w[:, 3352:3864],
                           w[:, 2816:2840], jnp.zeros((d_model, LANES - n_bg), F32)], axis=1)
    p1 = _norm_matmul(x1, l1_norm, w1p.astype(BF16), tm)
    y_c = _conv_module(p1, l1_c_conv_w, l1_c_conv_b, l1_c_norm, batch, seq, 512)

    groups, rheads = D_KV_GROUPS, D_HEADS // D_KV_GROUPS
    d_width = D_HEADS * HEAD_DIM
    kv_blk = 2560 // LANES
    qexp = np.zeros((d_width, D_HEADS * LANES), np.float32)
    for h in range(D_HEADS):
        for dd in range(HEAD_DIM):
            qexp[h * HEAD_DIM + dd, h * LANES + (h // rheads) * HEAD_DIM + dd] = 1.0
    tab64 = _rope_tables(pos, HEAD_DIM)
    qe, kboth, vtboth, gt = _nsa_prep(
        p1, tab64, jnp.tile(l1_d_qnorm, D_HEADS).reshape(1, d_width),
        jnp.tile(l1_d_knorm, LANES // HEAD_DIM).reshape(1, LANES), jnp.asarray(qexp, BF16),
        batch, seq, tq, kv_blk, 3840 // LANES)

    ncp = seq // CMP_STRIDE
    cmp_pos = jnp.arange(ncp, dtype=jnp.int32) * CMP_STRIDE + (CMP_BLOCK - 1)
    hidden = l1_d_cmp_w2_k.shape[0]

    def group_w1(w1):
        w = w1.reshape(CMP_BLOCK, HEAD_DIM, hidden)
        return jnp.stack([jnp.zeros((CMP_BLOCK, LANES, hidden), F32)
                          .at[:, g * HEAD_DIM:(g + 1) * HEAD_DIM, :].set(w) for g in range(groups)])

    w2k = jnp.stack([jnp.zeros((hidden, LANES), F32).at[:, g * HEAD_DIM:(g + 1) * HEAD_DIM]
                     .set(l1_d_cmp_w2_k) for g in range(groups)])
    w2v = jnp.zeros((hidden, LANES), F32).at[:, :HEAD_DIM].set(l1_d_cmp_w2_v)
    kc, vct = _compress(
        p1, kv_blk, jnp.tile(l1_d_cmp_pos_k, (1, groups)), jnp.tile(l1_d_cmp_pos_v, (1, groups)),
        group_w1(l1_d_cmp_w1_k).astype(BF16), w2k.astype(BF16),
        group_w1(l1_d_cmp_w1_v).astype(BF16), w2v.astype(BF16),
        jnp.tile(l1_d_knorm, LANES // HEAD_DIM).reshape(1, LANES), _rope_tables(cmp_pos, HEAD_DIM),
        batch, seq)

    nslc = seq // SLC_BLOCK
    y_d = _nsa_attn(qe, kc, vct, kboth, vtboth, gt, p1, _overlap_t(ncp, nslc),
                    batch, seq, tq, 2048 // (rheads * HEAD_DIM))
    y_m1 = _mem_attn(p1, kv4, 1, jnp.tile(l1_m_qnorm, M_HEADS).reshape(1, mw),
                     3328 // mw, 3584 // mw, seq, tq)
    out = _out_proj(x1, y_c, y_d, y_m1, l1_w_out.astype(BF16), tm)
    return out.reshape(batch, seq, d_model)
```

```python
import functools
import math

import numpy as np
import jax
import jax.numpy as jnp
from jax import lax
from jax.experimental import pallas as pl
from jax.experimental.pallas import tpu as pltpu

F32 = jnp.float32
BF16 = jnp.bfloat16

EPS = 1e-6
ROPE_THETA = 500000.0
HEAD_DIM = 64
A_CHUNK = 128
A_GROUPS = 4
B_HEADS = 8
B_QK_DIM = 32
C_KERNEL = 31
D_HEADS = 8
D_KV_GROUPS = 2
CMP_BLOCK = 32
CMP_STRIDE = 16
SLC_BLOCK = 64
SLC_TOPK = 16
WINDOW = 512
M_HEADS = 4

LANES = 128
SUBLANES = 8
MXU_DIM = 256
HALO = 32
NSLC_PAD = 128
ONES_ROWS = 16

NT_DIMS = (((1,), (1,)), ((), ()))
LOG2E = math.log2(math.e)


def _cparams(semantics, vmem_mb):
    return pltpu.CompilerParams(dimension_semantics=semantics,
                                vmem_limit_bytes=vmem_mb * 1024 * 1024)


def _split_bf16(x):
    hi = x.astype(BF16)
    lo = (x - hi.astype(F32)).astype(BF16)
    return hi, lo


def _group_mean(x2, gsize):
    t, c = x2.shape
    w = min(c, MXU_DIM)
    r = lax.broadcasted_iota(jnp.int32, (w, w), 0) // gsize
    cc = lax.broadcasted_iota(jnp.int32, (w, w), 1) // gsize
    ones = jnp.where(r == cc, 1.0, 0.0).astype(BF16)
    outs = []
    for s in range(c // w):
        hi, lo = _split_bf16(x2[:, s * w:(s + 1) * w])
        outs.append(jnp.dot(hi, ones, preferred_element_type=F32)
                    + jnp.dot(lo, ones, preferred_element_type=F32))
    out = outs[0] if len(outs) == 1 else jnp.concatenate(outs, axis=1)
    return out * (1.0 / gsize)


def _tile_lanes(t, width):
    rep = width // t.shape[1]
    return t if rep == 1 else jnp.concatenate([t] * rep, axis=1)


def _rope(x, cos_t, s1_t, s2_t, half):
    width = x.shape[1]
    up = pltpu.roll(x, width - half, 1)
    dn = pltpu.roll(x, half, 1)
    return (x * _tile_lanes(cos_t, width) + up * _tile_lanes(s1_t, width)
            + dn * _tile_lanes(s2_t, width))


def _scores_t(k, q, valid, st_ref, idx):
    st = lax.dot_general(k, q, NT_DIMS, preferred_element_type=F32)
    if valid is not None:
        st = jnp.where(valid, st, -jnp.inf)
    st_ref[idx] = st
    return jnp.max(st, axis=0, keepdims=True)


def _accumulate_t(st_ref, idx, cmax, m, acc_ref, aidx, vt, guard_empty):
    m_new = jnp.maximum(m, cmax)
    m_use = jnp.where(m_new == -jnp.inf, 0.0, m_new) if guard_empty else m_new
    alpha = jnp.exp2(m - m_use)
    pt = jnp.exp2((st_ref[idx] - m_use).astype(BF16))
    acc_ref[aidx] = alpha * acc_ref[aidx] + jnp.dot(vt, pt, preferred_element_type=F32)
    return m_new


def _pipelined_sweep(n, score_fn, accum_fn, unroll=4):
    score_fn(0, 0)
    stages = n - 1

    def stage(e, parity):
        score_fn(e + 1, 1 - parity)
        accum_fn(e, parity)

    def body(i, c):
        for k in range(unroll):
            stage(unroll * i + k, k % 2)
        return c

    if isinstance(n, int):
        iters, left = divmod(stages, unroll)
        if iters:
            lax.fori_loop(0, iters, body, 0)
        for k in range(left):
            stage(iters * unroll + k, k % 2)
        accum_fn(n - 1, left % 2)
        return
    lax.fori_loop(0, stages // unroll, body, 0)
    base = stages // unroll * unroll
    left = stages - base
    for k in range(unroll - 1):
        pl.when(left > k)(functools.partial(stage, base + k, k % 2))
    for parity in range(2):
        pl.when(left % 2 == parity)(functools.partial(accum_fn, n - 1, parity))


def _norm_matmul_kernel(x_ref, g_ref, w_ref, o_ref):
    x = x_ref[...]
    ms = jnp.mean(x * x, axis=-1, keepdims=True)
    h = (x * lax.rsqrt(ms + EPS) * g_ref[...]).astype(BF16)
    o_ref[...] = jnp.dot(h, w_ref[...], preferred_element_type=F32)


def _norm_matmul(x, g, w, tm):
    n, d = x.shape
    c = w.shape[1]
    return pl.pallas_call(
        _norm_matmul_kernel,
        grid=(n // tm,),
        in_specs=[pl.BlockSpec((tm, d), lambda i: (i, 0)),
                  pl.BlockSpec((1, d), lambda i: (0, 0)),
                  pl.BlockSpec((d, c), lambda i: (0, 0))],
        out_specs=pl.BlockSpec((tm, c), lambda i: (i, 0)),
        out_shape=jax.ShapeDtypeStruct((n, c), F32),
        compiler_params=_cparams(("parallel",), 48),
        name="norm_in_proj",
    )(x, g.reshape(1, d), w)


def _out_proj_kernel(x_ref, ya_ref, yb_ref, ym_ref, w_ref, o_ref):
    wa = ya_ref.shape[1]
    wb = yb_ref.shape[1]
    acc = jnp.dot(ya_ref[...], w_ref[0:wa, :], preferred_element_type=F32)
    acc += jnp.dot(yb_ref[...], w_ref[wa:wa + wb, :], preferred_element_type=F32)
    acc += jnp.dot(ym_ref[...], w_ref[wa + wb:, :], preferred_element_type=F32)
    o_ref[...] = x_ref[...] + acc


def _out_proj(x, ya, yb, ym, w, tm):
    n, d = x.shape
    return pl.pallas_call(
        _out_proj_kernel,
        grid=(n // tm,),
        in_specs=[pl.BlockSpec((tm, d), lambda i: (i, 0)),
                  pl.BlockSpec((tm, ya.shape[1]), lambda i: (i, 0)),
                  pl.BlockSpec((tm, yb.shape[1]), lambda i: (i, 0)),
                  pl.BlockSpec((tm, ym.shape[1]), lambda i: (i, 0)),
                  pl.BlockSpec(w.shape, lambda i: (0, 0))],
        out_specs=pl.BlockSpec((tm, d), lambda i: (i, 0)),
        out_shape=jax.ShapeDtypeStruct((n, d), F32),
        compiler_params=_cparams(("parallel",), 32),
        name="out_proj",
    )(x, ya, yb, ym, w)


def _sgu_kernel(u_ref, v_ref, g_ref, ws_ref, bs_ref, vg_ref, o_ref):
    tm = u_ref.shape[0]
    gdim = u_ref.shape[1] // A_GROUPS
    row = lax.broadcasted_iota(jnp.int32, (A_CHUNK, A_CHUNK), 0)
    col = lax.broadcasted_iota(jnp.int32, (A_CHUNK, A_CHUNK), 1)
    causal = col <= row
    for g in range(A_GROUPS):
        cols = slice(g * gdim, (g + 1) * gdim)
        gv = jax.nn.gelu(v_ref[:, cols])
        ms = jnp.mean(gv * gv, axis=-1, keepdims=True)
        vn = (gv * lax.rsqrt(ms + EPS) * vg_ref[:, cols]).astype(BF16)
        w = jnp.where(causal, ws_ref[g], 0.0).astype(BF16)
        for c in range(tm // A_CHUNK):
            rows = slice(c * A_CHUNK, (c + 1) * A_CHUNK)
            z = jnp.dot(w, vn[rows, :], preferred_element_type=F32) + bs_ref[:, cols]
            y = jax.nn.gelu(u_ref[rows, cols]) * z * jax.nn.silu(g_ref[rows, cols])
            o_ref[rows, cols] = y.astype(BF16)


def _sgu(p0, a_ws, bs_exp, vgain, tm):
    n = p0.shape[0]
    width = vgain.shape[1]
    return pl.pallas_call(
        _sgu_kernel,
        grid=(n // tm,),
        in_specs=[pl.BlockSpec((tm, width), lambda i: (i, 0)),
                  pl.BlockSpec((tm, width), lambda i: (i, 1)),
                  pl.BlockSpec((tm, width), lambda i: (i, 2)),
                  pl.BlockSpec(a_ws.shape, lambda i: (0, 0, 0)),
                  pl.BlockSpec(bs_exp.shape, lambda i: (0, 0)),
                  pl.BlockSpec(vgain.shape, lambda i: (0, 0))],
        out_specs=pl.BlockSpec((tm, width), lambda i: (i, 0)),
        out_shape=jax.ShapeDtypeStruct((n, width), BF16),
        compiler_params=_cparams(("parallel",), 32),
        name="sgu",
    )(p0, p0, p0, a_ws, bs_exp, vgain)


def _diff_prep_kernel(q_ref, k_ref, v_ref, c_ref, s1_ref, s2_ref, qg_ref, kg_ref,
                      qo_ref, ko_ref, vto_ref):
    half = B_QK_DIM // 4 // 2
    fold = (B_QK_DIM ** -0.5 * LOG2E, 1.0)
    for src, gain, dst, mul in ((q_ref, qg_ref, qo_ref, fold[0]), (k_ref, kg_ref, ko_ref, fold[1])):
        x = src[...]
        xn = x * lax.rsqrt(_group_mean(x * x, B_QK_DIM) + EPS) * gain[...]
        xr = _rope(xn, c_ref[...], s1_ref[...], s2_ref[...], half)
        dst[...] = (xr * mul if mul != 1.0 else xr).astype(BF16)
    vt = v_ref[...].T
    dv = 2 * B_QK_DIM
    ones = jnp.ones((ONES_ROWS, vt.shape[1]), F32)
    parts = []
    for h in range(B_HEADS):
        parts += [vt[h * dv:(h + 1) * dv, :], ones]
    vto_ref[0, 0] = jnp.concatenate(parts, axis=0).astype(BF16)


def _diff_prep(p0, tables, qgain, kgain, batch, seq, tm):
    n = p0.shape[0]
    width = qgain.shape[1]
    tiles_per_seq = seq // tm
    tab_spec = pl.BlockSpec((tm, LANES), lambda i: (i % tiles_per_seq, 0))
    out = jax.ShapeDtypeStruct((n, width), BF16)
    vrows = B_HEADS * (2 * B_QK_DIM + ONES_ROWS)
    out_t = jax.ShapeDtypeStruct((batch, tiles_per_seq, vrows, tm), BF16)
    return pl.pallas_call(
        _diff_prep_kernel,
        grid=(n // tm,),
        in_specs=[pl.BlockSpec((tm, width), lambda i: (i, 3)),
                  pl.BlockSpec((tm, width), lambda i: (i, 4)),
                  pl.BlockSpec((tm, width), lambda i: (i, 5)),
                  tab_spec, tab_spec, tab_spec,
                  pl.BlockSpec((1, width), lambda i: (0, 0)),
                  pl.BlockSpec((1, width), lambda i: (0, 0))],
        out_specs=[pl.BlockSpec((tm, width), lambda i: (i, 0)),
                   pl.BlockSpec((tm, width), lambda i: (i, 0)),
                   pl.BlockSpec((1, 1, vrows, tm),
                                lambda i: (i // tiles_per_seq, i % tiles_per_seq, 0, 0))],
        out_shape=[out, out, out_t],
        compiler_params=_cparams(("parallel",), 32),
        name="diff_prep",
    )(p0, p0, p0, *tables, qgain, kgain)


def _diff_attn_kernel(qi_tab, tile_tab, lam_ref, q_ref, k_ref, vt_ref, g_ref, sub_ref, o_ref,
                      acc_ref, sta_ref, stb_ref, cmax_ref, m_ref, *, tq, qt, lam_init):
    dv = 2 * B_QK_DIM
    vrows = dv + ONES_ROWS
    nmap = 4
    lane = lax.broadcasted_iota(jnp.int32, (1, LANES), 1)
    bufs = (sta_ref, stb_ref)
    acc_ref[...] = jnp.zeros_like(acc_ref)
    m_ref[...] = jnp.full(m_ref.shape, -jnp.inf, F32)
    krow = lax.broadcasted_iota(jnp.int32, (tq, tq), 0)
    qcol = lax.broadcasted_iota(jnp.int32, (tq, tq), 1)
    lam_p = lam_ref[...]
    lam = (jnp.exp(jnp.sum(lam_p[0:1] * lam_p[1:2], axis=-1, keepdims=True))
           - jnp.exp(jnp.sum(lam_p[2:3] * lam_p[3:4], axis=-1, keepdims=True)) + lam_init)

    def rows(tile):
        return pl.ds(pl.multiple_of(tile * tq, tq), tq)

    def scores(qi, tile, buf, valid):
        q = q_ref[rows(qi), :]
        k = k_ref[rows(tile), :]
        zero = jnp.zeros_like(q)
        for j in range(nmap):
            qm = jnp.where(lane // B_QK_DIM == j, q, zero)
            row = buf * nmap + j
            cmax_ref[row:row + 1, :] = _scores_t(k, qm, valid, bufs[buf], j)

    def accumulate(qi, tile, buf):
        for j in range(nmap):
            vt = vt_ref[0, tile, (j // 2) * vrows:(j // 2 + 1) * vrows, :]
            cm, sr = buf * nmap + j, qi * nmap + j
            m_ref[pl.ds(sr, 1), :] = _accumulate_t(bufs[buf], j, cmax_ref[cm:cm + 1, :],
                                                   m_ref[pl.ds(sr, 1), :], acc_ref, sr, vt, False)

    def finish(qi):
        halves = []
        for h in range(2):
            o = [acc_ref[qi * nmap + 2 * h + mp, 0:dv, :] / acc_ref[qi * nmap + 2 * h + mp, dv:dv + 1, :]
                 for mp in range(2)]
            pd = o[0] - lam * o[1]
            ms2 = jnp.mean(pd * pd, axis=0, keepdims=True)
            halves.append(pd * lax.rsqrt(ms2 + EPS))
        ob = jnp.concatenate(halves, axis=0).T * sub_ref[...] * (1.0 - lam_init)
        o_ref[rows(qi), :] = (ob * jax.nn.silu(g_ref[rows(qi), :])).astype(BF16)

    def below_score(e, buf):
        scores(qi_tab[e], tile_tab[e], buf, None)

    def below_accum(e, buf):
        accumulate(qi_tab[e], tile_tab[e], buf)

    def diag_score(e, buf):
        scores(e, e, buf, krow <= qcol)

    def diag_accum(e, buf):
        accumulate(e, e, buf)
        finish(e)

    _pipelined_sweep(qt * (qt - 1) // 2, below_score, below_accum, unroll=8)
    _pipelined_sweep(qt, diag_score, diag_accum)


def _diff_attn(lam_p, qn, kn, vt, p0, subln, batch, seq, tq, lam_init):
    n, width = qn.shape
    pairs = width // LANES
    qt = seq // tq
    g_off = 3072 // LANES
    vrows = 2 * B_QK_DIM + ONES_ROWS
    below = [(qi, t) for qi in range(qt) for t in range(qi)]
    qi_tab = jnp.asarray([e[0] for e in below], jnp.int32)
    tile_tab = jnp.asarray([e[1] for e in below], jnp.int32)
    kernel = functools.partial(_diff_attn_kernel, tq=tq, qt=qt, lam_init=lam_init)
    seq_spec = pl.BlockSpec((seq, LANES), lambda b, p, *_: (b, p))
    return pl.pallas_call(
        kernel,
        grid_spec=pltpu.PrefetchScalarGridSpec(
            num_scalar_prefetch=2,
            grid=(batch, pairs),
            in_specs=[pl.BlockSpec(lam_p.shape, lambda b, p, *_: (0, 0)),
                      seq_spec, seq_spec,
                      pl.BlockSpec((1, qt, 2 * vrows, tq), lambda b, p, *_: (b, 0, p, 0)),
                      pl.BlockSpec((seq, LANES), lambda b, p, *_: (b, g_off + p)),
                      pl.BlockSpec((1, LANES), lambda b, p, *_: (0, 0))],
            out_specs=seq_spec,
            scratch_shapes=[pltpu.VMEM((qt * 4, vrows, tq), F32), pltpu.VMEM((4, tq, tq), F32),
                            pltpu.VMEM((4, tq, tq), F32), pltpu.VMEM((8, tq), F32),
                            pltpu.VMEM((qt * 4, tq), F32)]),
        out_shape=jax.ShapeDtypeStruct((n, width), BF16),
        compiler_params=_cparams(("parallel", "parallel"), 40),
        name="diff_attn",
    )(qi_tab, tile_tab, lam_p, qn, kn, vt, p0, subln)


def _mem_kv_kernel(mem_ref, g_ref, w_ref, kg_ref, kh_ref, vt_ref):
    x = mem_ref[0]
    ms = jnp.mean(x * x, axis=-1, keepdims=True)
    h = (x * lax.rsqrt(ms + EPS) * g_ref[...]).astype(BF16)
    kv = jnp.dot(h, w_ref[...], preferred_element_type=F32)
    mw = kv.shape[1] // 4
    lane = lax.broadcasted_iota(jnp.int32, (1, mw), 1)
    ones = jnp.ones((ONES_ROWS, kv.shape[0]), F32)
    for layer in range(2):
        k = kv[:, (2 * layer) * mw:(2 * layer + 1) * mw]
        kn = k * lax.rsqrt(_group_mean(k * k, HEAD_DIM) + EPS) * kg_ref[layer:layer + 1, :]
        vt = kv[:, (2 * layer + 1) * mw:(2 * layer + 2) * mw].T
        parts = []
        for hd in range(M_HEADS):
            kh_ref[layer, 0, hd] = jnp.where(lane // HEAD_DIM == hd, kn, 0.0).astype(BF16)
            parts += [vt[hd * HEAD_DIM:(hd + 1) * HEAD_DIM, :], ones]
        vt_ref[layer, 0] = jnp.concatenate(parts, axis=0).astype(BF16)


def _mem_kv(mem, mem_norm, wkv_both, kgains):
    batch, mtok, d = mem.shape
    mw = wkv_both.shape[1] // 4
    vrows = M_HEADS * (HEAD_DIM + ONES_ROWS)
    return pl.pallas_call(
        _mem_kv_kernel,
        grid=(batch,),
        in_specs=[pl.BlockSpec((1, mtok, d), lambda b: (b, 0, 0)),
                  pl.BlockSpec((1, d), lambda b: (0, 0)),
                  pl.BlockSpec(wkv_both.shape, lambda b: (0, 0)),
                  pl.BlockSpec(kgains.shape, lambda b: (0, 0))],
        out_specs=[pl.BlockSpec((2, 1, M_HEADS, mtok, mw), lambda b: (0, b, 0, 0, 0)),
                   pl.BlockSpec((2, 1, vrows, mtok), lambda b: (0, b, 0, 0))],
        out_shape=[jax.ShapeDtypeStruct((2, batch, M_HEADS, mtok, mw), BF16),
                   jax.ShapeDtypeStruct((2, batch, vrows, mtok), BF16)],
        compiler_params=_cparams(("parallel",), 32),
        name="mem_kv",
    )(mem, mem_norm.reshape(1, d), wkv_both, kgains)


def _mem_attn_kernel(q_ref, g_ref, kh_ref, vt_ref, qg_ref, o_ref, st_ref):
    vrows = HEAD_DIM + ONES_ROWS
    x = q_ref[...]
    qn = (x * lax.rsqrt(_group_mean(x * x, HEAD_DIM) + EPS) * qg_ref[...]
          * (HEAD_DIM ** -0.5 * LOG2E)).astype(BF16)
    cmax = [_scores_t(kh_ref[0, 0, h], qn, None, st_ref, h) for h in range(M_HEADS)]
    outs = []
    for h in range(M_HEADS):
        pt = jnp.exp2((st_ref[h] - cmax[h]).astype(BF16))
        acc = jnp.dot(vt_ref[0, 0, h * vrows:(h + 1) * vrows, :], pt, preferred_element_type=F32)
        outs.append(acc[0:HEAD_DIM, :] / acc[HEAD_DIM:HEAD_DIM + 1, :])
    out = jnp.concatenate(outs, axis=0).T
    o_ref[...] = (out * jax.nn.silu(g_ref[...])).astype(BF16)


def _mem_attn(p, kh, vt, layer, qgain, q_blk, g_blk, seq, tq):
    n = p.shape[0]
    _, batch, heads, mtok, mw = kh.shape
    qt = seq // tq
    return pl.pallas_call(
        _mem_attn_kernel,
        grid=(n // tq,),
        in_specs=[pl.BlockSpec((tq, mw), lambda i: (i, q_blk)),
                  pl.BlockSpec((tq, mw), lambda i: (i, g_blk)),
                  pl.BlockSpec((1, 1, heads, mtok, mw), lambda i: (layer, i // qt, 0, 0, 0)),
                  pl.BlockSpec((1, 1, vt.shape[2], mtok), lambda i: (layer, i // qt, 0, 0)),
                  pl.BlockSpec((1, mw), lambda i: (0, 0))],
        out_specs=pl.BlockSpec((tq, mw), lambda i: (i, 0)),
        out_shape=jax.ShapeDtypeStruct((n, mw), BF16),
        scratch_shapes=[pltpu.VMEM((heads, mtok, tq), F32)],
        compiler_params=_cparams(("parallel",), 32),
        name="mem_attn",
    )(p, p, kh, vt, qgain)


def _conv_kernel(a_ref, b_ref, g_ref, w_ref, cb_ref, ng_ref, o_ref, hbuf_ref, shift_ref):
    t = pl.program_id(1)
    ts = a_ref.shape[0]
    span = ts + HALO

    @pl.when(t == 0)
    def _():
        hbuf_ref[0:HALO, :] = jnp.zeros((HALO, hbuf_ref.shape[1]), F32)

    hbuf_ref[HALO:HALO + ts, :] = a_ref[...] * jax.nn.sigmoid(b_ref[...])
    for ph in range(SUBLANES):
        shift_ref[ph, 0:span - ph, :] = hbuf_ref[ph:span, :]
    first = HALO - (C_KERNEL - 1)
    y = None
    for j in range(C_KERNEL):
        ph, base = (first + j) % SUBLANES, (first + j) // SUBLANES * SUBLANES
        term = shift_ref[ph, base:base + ts, :] * w_ref[j:j + 1, :]
        y = term if y is None else y + term
    y += cb_ref[...]
    ms = jnp.mean(y * y, axis=-1, keepdims=True)
    yn = y * lax.rsqrt(ms + EPS) * ng_ref[...]
    o_ref[...] = (jax.nn.silu(yn) * jax.nn.silu(g_ref[...])).astype(BF16)
    hbuf_ref[0:HALO, :] = hbuf_ref[ts:ts + HALO, :]


def _conv_module(p1, conv_w, conv_b, norm_g, batch, seq, ts):
    n = p1.shape[0]
    width = conv_w.shape[1]
    st = seq // ts
    return pl.pallas_call(
        _conv_kernel,
        grid=(batch, st),
        in_specs=[pl.BlockSpec((ts, width), lambda b, t: (b * st + t, 0)),
                  pl.BlockSpec((ts, width), lambda b, t: (b * st + t, 1)),
                  pl.BlockSpec((ts, width), lambda b, t: (b * st + t, 2)),
                  pl.BlockSpec(conv_w.shape, lambda b, t: (0, 0)),
                  pl.BlockSpec((1, width), lambda b, t: (0, 0)),
                  pl.BlockSpec((1, width), lambda b, t: (0, 0))],
        out_specs=pl.BlockSpec((ts, width), lambda b, t: (b * st + t, 0)),
        out_shape=jax.ShapeDtypeStruct((n, width), BF16),
        scratch_shapes=[pltpu.VMEM((ts + HALO, width), F32),
                        pltpu.VMEM((SUBLANES, ts + HALO, width), F32)],
        compiler_params=_cparams(("arbitrary", "arbitrary"), 32),
        name="conv_module",
    )(p1, p1, p1, conv_w, conv_b.reshape(1, width), norm_g.reshape(1, width))


def _nsa_prep_kernel(q_ref, ks_ref, vs_ref, kw_ref, vw_ref, bg_ref, c_ref, s1_ref, s2_ref,
                     qg_ref, kg_ref, qexp_ref, qo_ref, ko_ref, vto_ref, gto_ref):
    half = HEAD_DIM // 4 // 2
    tabs = (c_ref[...], s1_ref[...], s2_ref[...])
    x = q_ref[...]
    xn = x * lax.rsqrt(_group_mean(x * x, HEAD_DIM) + EPS) * qg_ref[...]
    qr = (_rope(xn, *tabs, half) * (HEAD_DIM ** -0.5 * LOG2E)).astype(BF16)
    qo_ref[...] = jnp.dot(qr, qexp_ref[...], preferred_element_type=F32).astype(BF16)
    for br, src in enumerate((ks_ref, kw_ref)):
        k = src[...]
        kn = k * lax.rsqrt(_group_mean(k * k, HEAD_DIM) + EPS) * kg_ref[...]
        ko_ref[br] = _rope(kn, *tabs, half).astype(BF16)
    ones = jnp.ones((ONES_ROWS, q_ref.shape[0]), F32)
    for br, src in enumerate((vs_ref, vw_ref)):
        vt = src[...].T
        parts = []
        for g in range(D_KV_GROUPS):
            parts += [vt[g * HEAD_DIM:(g + 1) * HEAD_DIM, :], ones]
        vto_ref[0, br, 0] = jnp.concatenate(parts, axis=0).astype(BF16)
    gto_ref[0, 0] = jax.nn.sigmoid(bg_ref[...]).T


def _nsa_prep(p1, tables, qgain, kgain, qexp, batch, seq, tm, kv_blk, bg_blk):
    n = p1.shape[0]
    width = qgain.shape[1]
    tiles_per_seq = seq // tm
    tab_spec = pl.BlockSpec((tm, LANES), lambda i: (i % tiles_per_seq, 0))
    t_spec = pl.BlockSpec((1, 1, LANES, tm), lambda i: (i // tiles_per_seq, i % tiles_per_seq, 0, 0))
    k_spec = pl.BlockSpec((2, tm, LANES), lambda i: (0, i, 0))
    k_out = jax.ShapeDtypeStruct((2, n, LANES), BF16)
    vrows = D_KV_GROUPS * (HEAD_DIM + ONES_ROWS)
    vt_spec = pl.BlockSpec((1, 2, 1, vrows, tm),
                           lambda i: (i // tiles_per_seq, 0, i % tiles_per_seq, 0, 0))
    vt_out = jax.ShapeDtypeStruct((batch, 2, tiles_per_seq, vrows, tm), BF16)

    def col(j):
        return pl.BlockSpec((tm, LANES), lambda i: (i, j))

    return pl.pallas_call(
        _nsa_prep_kernel,
        grid=(n // tm,),
        in_specs=[pl.BlockSpec((tm, width), lambda i: (i, 3)),
                  col(kv_blk + 2), col(kv_blk + 3), col(kv_blk + 4), col(kv_blk + 5), col(bg_blk),
                  tab_spec, tab_spec, tab_spec,
                  pl.BlockSpec((1, width), lambda i: (0, 0)),
                  pl.BlockSpec((1, LANES), lambda i: (0, 0)),
                  pl.BlockSpec(qexp.shape, lambda i: (0, 0))],
        out_specs=[pl.BlockSpec((tm, qexp.shape[1]), lambda i: (i, 0)),
                   k_spec, vt_spec, t_spec],
        out_shape=[jax.ShapeDtypeStruct((n, qexp.shape[1]), BF16), k_out, vt_out,
                   jax.ShapeDtypeStruct((batch, tiles_per_seq, LANES, tm), F32)],
        compiler_params=_cparams(("parallel",), 32),
        name="nsa_prep",
    )(p1, p1, p1, p1, p1, p1, *tables, qgain, kgain, qexp)


def _compress_kernel(tk_ref, tv_ref, pk_ref, pv_ref, w1k_ref, w2k_ref, w1v_ref, w2v_ref,
                     kg_ref, c_ref, s1_ref, s2_ref, kc_ref, vct_ref):
    half = HEAD_DIM // 4 // 2
    ncp = kc_ref.shape[2]

    def mlp(t_ref, p_ref, w1_ref, w2):
        a = jnp.zeros((ncp, w1_ref.shape[3]), F32)
        b = jnp.zeros((ncp, w1_ref.shape[3]), F32)
        for l in range(CMP_STRIDE):
            x = t_ref[pl.ds(l, ncp, stride=CMP_STRIDE), :]
            a += jnp.dot((x + p_ref[l:l + 1, :]).astype(BF16), w1_ref[0, l],
                         preferred_element_type=F32)
            b += jnp.dot((x + p_ref[CMP_STRIDE + l:CMP_STRIDE + l + 1, :]).astype(BF16),
                         w1_ref[0, CMP_STRIDE + l], preferred_element_type=F32)
        h = a + pltpu.roll(b, ncp - 1, 0)
        return jnp.dot(jax.nn.silu(h).astype(BF16), w2, preferred_element_type=F32)

    kc = mlp(tk_ref, pk_ref, w1k_ref, w2k_ref[0])
    kn = kc * lax.rsqrt(_group_mean(kc * kc, HEAD_DIM) + EPS) * kg_ref[...]
    kc_ref[0, 0] = _rope(kn, c_ref[...], s1_ref[...], s2_ref[...], half).astype(BF16)
    vc = mlp(tv_ref, pv_ref, w1v_ref, w2v_ref[...])
    vct_ref[0, 0] = vc.T[0:HEAD_DIM, :].astype(BF16)


def _compress(p1, k_blk, pk, pv, w1k, w2k, w1v, w2v, kgain, tables, batch, seq):
    groups = w1k.shape[0]
    ncp = seq // CMP_STRIDE

    def full(a):
        return pl.BlockSpec(a.shape, lambda b, g: (0,) * a.ndim)

    def per_group(a):
        return pl.BlockSpec((1,) + a.shape[1:], lambda b, g: (g,) + (0,) * (a.ndim - 1))

    return pl.pallas_call(
        _compress_kernel,
        grid=(batch, groups),
        in_specs=[pl.BlockSpec((seq, LANES), lambda b, g: (b, k_blk)),
                  pl.BlockSpec((seq, LANES), lambda b, g: (b, k_blk + 1)),
                  full(pk), full(pv), per_group(w1k), per_group(w2k), per_group(w1v), full(w2v),
                  full(kgain), full(tables[0]), full(tables[1]), full(tables[2])],
        out_specs=[pl.BlockSpec((1, 1, ncp, LANES), lambda b, g: (b, g, 0, 0)),
                   pl.BlockSpec((1, 1, HEAD_DIM, ncp), lambda b, g: (b, g, 0, 0))],
        out_shape=[jax.ShapeDtypeStruct((batch, groups, ncp, LANES), BF16),
                   jax.ShapeDtypeStruct((batch, groups, HEAD_DIM, ncp), BF16)],
        compiler_params=_cparams(("parallel", "parallel"), 32),
        name="nsa_compress",
    )(p1, p1, pk, pv, w1k, w2k, w1v, w2v, kgain, *tables)


def _nsa_kernel(qi_tab, tile_tab, br_tab, q_ref, kc_ref, vct_ref, k_ref, vt_ref, gt_ref,
                dg_ref, ovt_ref, o_ref, acc_ref, oc_ref, sel_ref, imp_ref, sta_ref, stb_ref, cst_ref,
                cmax_ref, m_ref, *, tq, qt, nslc, nsel, nentries):
    g = pl.program_id(1)
    rheads = D_HEADS // D_KV_GROUPS
    seq = qt * tq
    tiny = float(np.finfo(np.float32).tiny)
    qcol = lax.broadcasted_iota(jnp.int32, (tq, tq), 1)
    krow = lax.broadcasted_iota(jnp.int32, (tq, tq), 0)
    bufs = (sta_ref, stb_ref)
    blocks_per_tile = tq // SLC_BLOCK
    nslab = -(-nslc // SUBLANES)
    sel_rows = nslab * SUBLANES

    def rows(tile):
        return pl.ds(pl.multiple_of(tile * tq, tq), tq)

    def heads_q(qi):
        return [q_ref[rows(qi), r * LANES:(r + 1) * LANES] for r in range(rheads)]

    kc = kc_ref[0, 0]
    vct = vct_ref[0, 0]
    ncp = kc.shape[0]
    ovt = ovt_ref[...]

    def select_blocks(qi, c):
        q = heads_q(qi)
        cend = lax.broadcasted_iota(jnp.int32, (ncp, tq), 0) * CMP_STRIDE + (CMP_BLOCK - 1)
        cvalid = cend <= qi * tq + lax.broadcasted_iota(jnp.int32, (ncp, tq), 1)
        psum = jnp.zeros((ncp, tq), F32)
        cmax = [_scores_t(kc, q[r], cvalid, cst_ref, r) for r in range(rheads)]
        for r in range(rheads):
            m = jnp.where(jnp.isfinite(cmax[r]), cmax[r], 0.0)
            e = jnp.exp2(cst_ref[r] - m)
            p = e / jnp.maximum(jnp.sum(e, axis=0, keepdims=True), tiny)
            psum += p
            oc_ref[qi * rheads + r] = jnp.dot(vct, p.astype(BF16), preferred_element_type=F32)
        hi, lo = _split_bf16(psum)
        imp = (jnp.dot(ovt, hi, preferred_element_type=F32)
               + jnp.dot(ovt, lo, preferred_element_type=F32))
        blk = lax.broadcasted_iota(jnp.int32, (NSLC_PAD, tq), 0)
        cur = (qi * tq + lax.broadcasted_iota(jnp.int32, (NSLC_PAD, tq), 1)) // SLC_BLOCK
        imp = jnp.where(blk > cur, -jnp.inf, imp)
        forced = (blk == 0) | (blk == cur) | (blk == cur - 1)
        imp = jnp.where(forced, jnp.inf, imp)
        imp_ref[...] = imp
        slabs = [imp[s * SUBLANES:(s + 1) * SUBLANES, :] for s in range(nslab)]
        ranks = [jnp.zeros((SUBLANES, tq), F32) for _ in range(nslab)]
        sub = lax.broadcasted_iota(jnp.int32, (SUBLANES, tq), 0)
        for jp in range(nslc):
            row = imp_ref[jp:jp + 1, :]
            for s in range(nslab):
                if s * SUBLANES > jp:
                    ahead = jnp.where(row >= slabs[s], 1.0, 0.0)
                elif (s + 1) * SUBLANES - 1 < jp:
                    ahead = jnp.where(row > slabs[s], 1.0, 0.0)
                else:
                    ahead = jnp.where(sub > jp - s * SUBLANES, jnp.where(row >= slabs[s], 1.0, 0.0),
                                      jnp.where(row > slabs[s], 1.0, 0.0))
                ranks[s] = ranks[s] + ahead
        for s in range(nslab):
            start = pl.multiple_of(qi * sel_rows + s * SUBLANES, SUBLANES)
            sel_ref[pl.ds(start, SUBLANES), :] = jnp.where(ranks[s] < nsel, 1.0, 0.0)
        return c

    lax.fori_loop(0, qt, select_blocks, 0)
    acc_ref[...] = jnp.zeros_like(acc_ref)
    m_ref[...] = jnp.full(m_ref.shape, -jnp.inf, F32)

    def score_fn(e, buf):
        qi, tile, br = qi_tab[e], tile_tab[e], br_tab[e]
        q = heads_q(qi)
        k = k_ref[br, rows(tile), :]
        base = qi * sel_rows + tile * blocks_per_tile
        chosen = jnp.concatenate(
            [jnp.broadcast_to(sel_ref[pl.ds(base + c, 1), :], (SLC_BLOCK, tq))
             for c in range(blocks_per_tile)], axis=0)
        dpos = (qi - tile) * tq + (qcol - krow)
        reach = jnp.where(br == 0, seq, WINDOW).astype(jnp.uint32)
        kept = jnp.where(jnp.maximum(chosen, br.astype(F32)) > 0.5, dpos, -1)
        valid = kept.astype(jnp.uint32) < reach
        for r in range(rheads):
            row = buf * rheads + r
            cmax_ref[row:row + 1, :] = _scores_t(k, q[r], valid, bufs[buf], r)

    def accum_fn(e, buf):
        qi, tile, br = qi_tab[e], tile_tab[e], br_tab[e]
        vt = vt_ref[0, br, tile]
        for r in range(rheads):
            cm, sr = buf * rheads + r, (qi * 2 + br) * rheads + r
            m_ref[pl.ds(sr, 1), :] = _accumulate_t(bufs[buf], r, cmax_ref[cm:cm + 1, :],
                                                   m_ref[pl.ds(sr, 1), :], acc_ref, sr, vt, True)

    _pipelined_sweep(nentries, score_fn, accum_fn)

    def finish(qi, c):
        outs = []
        for r in range(rheads):
            row0 = (g * rheads + r) * 3
            o = [oc_ref[qi * rheads + r]]
            for br in range(2):
                sr = (qi * 2 + br) * rheads + r
                o.append(acc_ref[sr, 0:HEAD_DIM, :]
                         / jnp.maximum(acc_ref[sr, HEAD_DIM:HEAD_DIM + 1, :], tiny))
            outs.append(sum(gt_ref[0, qi, pl.ds(row0 + j, 1), :] * o[j] for j in range(3)))
        out = jnp.concatenate(outs, axis=0).T
        o_ref[rows(qi), :] = (out * jax.nn.silu(dg_ref[rows(qi), :])).astype(BF16)
        return c

    lax.fori_loop(0, qt, finish, 0)


def _nsa_attn(qe, kc, vct, kboth, vtboth, gt, p1, ovt, batch, seq, tq, dg_blk):
    n = qe.shape[0]
    groups = D_KV_GROUPS
    rheads = D_HEADS // groups
    width = rheads * HEAD_DIM
    qt = seq // tq
    ncp = kc.shape[2]
    nslc = seq // SLC_BLOCK
    sel_rows = -(-nslc // SUBLANES) * SUBLANES
    wtiles = (WINDOW + tq - 1) // tq
    entries = []
    for qi in range(qt):
        entries += [(qi, t, 0) for t in range(qi + 1)]
        entries += [(qi, t, 1) for t in range(max(qi - wtiles, 0), qi + 1)]
    tabs = [jnp.asarray([e[c] for e in entries], jnp.int32) for c in range(3)]
    kernel = functools.partial(_nsa_kernel, tq=tq, qt=qt, nslc=nslc, nsel=min(SLC_TOPK, nslc),
                               nentries=len(entries))
    return pl.pallas_call(
        kernel,
        grid_spec=pltpu.PrefetchScalarGridSpec(
            num_scalar_prefetch=3,
            grid=(batch, groups),
            in_specs=[pl.BlockSpec((seq, rheads * LANES), lambda b, g, *_: (b, g)),
                      pl.BlockSpec((1, 1, ncp, LANES), lambda b, g, *_: (b, g, 0, 0)),
                      pl.BlockSpec((1, 1, HEAD_DIM, ncp), lambda b, g, *_: (b, g, 0, 0)),
                      pl.BlockSpec((2, seq, LANES), lambda b, g, *_: (0, b, 0)),
                      pl.BlockSpec((1, 2, qt, HEAD_DIM + ONES_ROWS, tq), lambda b, g, *_: (b, 0, 0, g, 0)),
                      pl.BlockSpec((1, qt, LANES, tq), lambda b, g, *_: (b, 0, 0, 0)),
                      pl.BlockSpec((seq, width), lambda b, g, *_: (b, dg_blk + g)),
                      pl.BlockSpec(ovt.shape, lambda b, g, *_: (0, 0))],
            out_specs=pl.BlockSpec((seq, width), lambda b, g, *_: (b, g)),
            scratch_shapes=[pltpu.VMEM((qt * 2 * rheads, HEAD_DIM + ONES_ROWS, tq), F32),
                            pltpu.VMEM((qt * rheads, HEAD_DIM, tq), F32),
                            pltpu.VMEM((qt * sel_rows, tq), F32),
                            pltpu.VMEM((NSLC_PAD, tq), F32),
                            pltpu.VMEM((rheads, tq, tq), F32),
                            pltpu.VMEM((rheads, tq, tq), F32),
                            pltpu.VMEM((rheads, ncp, tq), F32),
                            pltpu.VMEM((2 * rheads, tq), F32),
                            pltpu.VMEM((qt * 2 * rheads, tq), F32)]),
        out_shape=jax.ShapeDtypeStruct((n, groups * width), BF16),
        compiler_params=_cparams(("parallel", "parallel"), 56),
        name="nsa_attn",
    )(*tabs, qe, kc, vct, kboth, vtboth, gt, p1, ovt)


def _rope_tables(pos, d):
    rd = d // 4
    half = rd // 2
    inv = ROPE_THETA ** (-jnp.arange(half, dtype=F32) / half)
    ang = pos.astype(F32)[:, None] * inv[None, :]
    cos, sin = jnp.cos(ang), jnp.sin(ang)
    npos = pos.shape[0]
    zeros = jnp.zeros((npos, d), F32)
    c = jnp.concatenate([cos, cos, jnp.ones((npos, d - rd), F32)], axis=1)
    s1 = zeros.at[:, :half].set(-sin)
    s2 = zeros.at[:, half:rd].set(sin)
    return tuple(jnp.tile(t, (1, LANES // d)) for t in (c, s1, s2))


def _overlap_t(ncp, nslc):
    start = np.arange(ncp) * CMP_STRIDE
    s0 = np.arange(nslc) * SLC_BLOCK
    lo = np.maximum(start[:, None], s0[None, :])
    hi = np.minimum(start[:, None] + CMP_BLOCK, s0[None, :] + SLC_BLOCK)
    ov = np.clip(hi - lo, 0, None) / CMP_BLOCK
    ov[ncp - 1] = 0.0
    out = np.zeros((NSLC_PAD, ncp), np.float32)
    out[:nslc] = ov.T
    return jnp.asarray(out, BF16)


def kernel(x, mem, mem_norm, l0_norm, l0_w_in, l0_a_vnorm, l0_a_ws, l0_a_bs, l0_b_qnorm, l0_b_knorm, l0_b_lq1, l0_b_lk1, l0_b_lq2, l0_b_lk2, l0_b_subln, l0_m_wkv, l0_m_qnorm, l0_m_knorm, l0_w_out, l1_norm, l1_w_in, l1_c_conv_w, l1_c_conv_b, l1_c_norm, l1_d_qnorm, l1_d_knorm, l1_d_cmp_pos_k, l1_d_cmp_w1_k, l1_d_cmp_w2_k, l1_d_cmp_pos_v, l1_d_cmp_w1_v, l1_d_cmp_w2_v, l1_m_wkv, l1_m_qnorm, l1_m_knorm, l1_w_out):
    batch, seq, d_model = x.shape
    n = batch * seq
    tm = 256
    tq = 256
    mw = M_HEADS * HEAD_DIM
    pos = jnp.arange(seq, dtype=jnp.int32)
    x0 = x.reshape(n, d_model)

    wkv_both = jnp.concatenate([l0_m_wkv, l1_m_wkv], axis=1).astype(BF16)
    kgains = jnp.stack([jnp.tile(l0_m_knorm, M_HEADS), jnp.tile(l1_m_knorm, M_HEADS)])
    mem_kh, mem_vt = _mem_kv(mem, mem_norm, wkv_both, kgains)

    p0 = _norm_matmul(x0, l0_norm, l0_w_in.astype(BF16), tm)
    a_width = l0_a_vnorm.shape[0]
    bs_exp = jnp.repeat(l0_a_bs.T, a_width // A_GROUPS, axis=1)
    y_a = _sgu(p0, l0_a_ws, bs_exp, l0_a_vnorm.reshape(1, a_width), 512)

    tab32 = _rope_tables(pos, B_QK_DIM)
    b_width = 2 * B_HEADS * B_QK_DIM
    qn, kn, vt = _diff_prep(p0, tab32,
                            jnp.tile(l0_b_qnorm, b_width // B_QK_DIM).reshape(1, b_width),
                            jnp.tile(l0_b_knorm, b_width // B_QK_DIM).reshape(1, b_width),
                            batch, seq, tq)
    lam_init = 0.8 - 0.6 * math.exp(-0.3 * 1)
    lam_p = jnp.stack([l0_b_lq1, l0_b_lk1, l0_b_lq2, l0_b_lk2])
    y_b = _diff_attn(lam_p, qn, kn, vt, p0, jnp.tile(l0_b_subln, 2).reshape(1, LANES),
                     batch, seq, tq, lam_init)
    y_m = _mem_attn(p0, mem_kh, mem_vt, 0, jnp.tile(l0_m_qnorm, M_HEADS).reshape(1, mw),
                    3584 // mw, 3840 // mw, seq, 2 * tq)
    x1 = _out_proj(x0, y_a, y_b, y_m, l0_w_out.astype(BF16), tm)

    w = l1_w_in
    n_bg = 3 * D_HEADS
    w1p = jnp.concatenate([w[:, 0:2048], w[:, 2840:3352], w[:, 2048:2816], w[:, 3352:3864],
                           w[:, 2816:2840], jnp.zeros((d_model, LANES - n_bg), F32)], axis=1)
    p1 = _norm_matmul(x1, l1_norm, w1p.astype(BF16), tm)
    y_c = _conv_module(p1, l1_c_conv_w, l1_c_conv_b, l1_c_norm, batch, seq, 512)

    groups, rheads = D_KV_GROUPS, D_HEADS // D_KV_GROUPS
    d_width = D_HEADS * HEAD_DIM
    kv_blk = 2560 // LANES
    qexp = np.zeros((d_width, D_HEADS * LANES), np.float32)
    for h in range(D_HEADS):
        for dd in range(HEAD_DIM):
            qexp[h * HEAD_DIM + dd, h * LANES + (h // rheads) * HEAD_DIM + dd] = 1.0
    tab64 = _rope_tables(pos, HEAD_DIM)
    qe, kboth, vtboth, gt = _nsa_prep(
        p1, tab64, jnp.tile(l1_d_qnorm, D_HEADS).reshape(1, d_width),
        jnp.tile(l1_d_knorm, LANES // HEAD_DIM).reshape(1, LANES), jnp.asarray(qexp, BF16),
        batch, seq, tq, kv_blk, 3840 // LANES)

    ncp = seq // CMP_STRIDE
    cmp_pos = jnp.arange(ncp, dtype=jnp.int32) * CMP_STRIDE + (CMP_BLOCK - 1)
    hidden = l1_d_cmp_w2_k.shape[0]

    def group_w1(w1):
        w = w1.reshape(CMP_BLOCK, HEAD_DIM, hidden)
        return jnp.stack([jnp.zeros((CMP_BLOCK, LANES, hidden), F32)
                          .at[:, g * HEAD_DIM:(g + 1) * HEAD_DIM, :].set(w) for g in range(groups)])

    w2k = jnp.stack([jnp.zeros((hidden, LANES), F32).at[:, g * HEAD_DIM:(g + 1) * HEAD_DIM]
                     .set(l1_d_cmp_w2_k) for g in range(groups)])
    w2v = jnp.zeros((hidden, LANES), F32).at[:, :HEAD_DIM].set(l1_d_cmp_w2_v)
    kc, vct = _compress(
        p1, kv_blk, jnp.tile(l1_d_cmp_pos_k, (1, groups)), jnp.tile(l1_d_cmp_pos_v, (1, groups)),
        group_w1(l1_d_cmp_w1_k).astype(BF16), w2k.astype(BF16),
        group_w1(l1_d_cmp_w1_v).astype(BF16), w2v.astype(BF16),
        jnp.tile(l1_d_knorm, LANES // HEAD_DIM).reshape(1, LANES), _rope_tables(cmp_pos, HEAD_DIM),
        batch, seq)

    nslc = seq // SLC_BLOCK
    y_d = _nsa_attn(qe, kc, vct, kboth, vtboth, gt, p1, _overlap_t(ncp, nslc),
                    batch, seq, tq, 2048 // (rheads * HEAD_DIM))
    y_m1 = _mem_attn(p1, mem_kh, mem_vt, 1, jnp.tile(l1_m_qnorm, M_HEADS).reshape(1, mw),
                     3328 // mw, 3584 // mw, seq, 2 * tq)
    out = _out_proj(x1, y_c, y_d, y_m1, l1_w_out.astype(BF16), tm)
    return out.reshape(batch, seq, d_model)
```

```python
import functools
import math

import numpy as np
import jax
import jax.numpy as jnp
from jax import lax
from jax.experimental import pallas as pl
from jax.experimental.pallas import tpu as pltpu

F32 = jnp.float32
BF16 = jnp.bfloat16

EPS = 1e-6
ROPE_THETA = 500000.0
HEAD_DIM = 64
A_CHUNK = 128
A_GROUPS = 4
B_HEADS = 8
B_QK_DIM = 32
C_KERNEL = 31
D_HEADS = 8
D_KV_GROUPS = 2
CMP_BLOCK = 32
CMP_STRIDE = 16
SLC_BLOCK = 64
SLC_TOPK = 16
WINDOW = 512
M_HEADS = 4

LANES = 128
SUBLANES = 8
MXU_DIM = 256
HALO = 32
NSLC_PAD = 128
ONES_ROWS = 16

NT_DIMS = (((1,), (1,)), ((), ()))
LOG2E = math.log2(math.e)


def _cparams(semantics, vmem_mb):
    return pltpu.CompilerParams(dimension_semantics=semantics,
                                vmem_limit_bytes=vmem_mb * 1024 * 1024)


def _split_bf16(x):
    hi = x.astype(BF16)
    lo = (x - hi.astype(F32)).astype(BF16)
    return hi, lo


def _group_mean(x2, gsize):
    t, c = x2.shape
    w = min(c, MXU_DIM)
    r = lax.broadcasted_iota(jnp.int32, (w, w), 0) // gsize
    cc = lax.broadcasted_iota(jnp.int32, (w, w), 1) // gsize
    ones = jnp.where(r == cc, 1.0, 0.0).astype(BF16)
    outs = []
    for s in range(c // w):
        hi, lo = _split_bf16(x2[:, s * w:(s + 1) * w])
        outs.append(jnp.dot(hi, ones, preferred_element_type=F32)
                    + jnp.dot(lo, ones, preferred_element_type=F32))
    out = outs[0] if len(outs) == 1 else jnp.concatenate(outs, axis=1)
    return out * (1.0 / gsize)


def _tile_lanes(t, width):
    rep = width // t.shape[1]
    return t if rep == 1 else jnp.concatenate([t] * rep, axis=1)


def _rope(x, cos_t, s1_t, s2_t, half):
    width = x.shape[1]
    up = pltpu.roll(x, width - half, 1)
    dn = pltpu.roll(x, half, 1)
    return (x * _tile_lanes(cos_t, width) + up * _tile_lanes(s1_t, width)
            + dn * _tile_lanes(s2_t, width))


def _project_chunk(h, w_ref, p_ref, nchunks, c):
    tiles = p_ref.shape[1] // LANES
    lo, hi = (c * tiles // nchunks) * LANES, ((c + 1) * tiles // nchunks) * LANES
    p_ref[:, lo:hi] = jnp.dot(h, w_ref[:, lo:hi], preferred_element_type=F32)


def _scores_t(k, q, valid, st_ref, idx):
    st = lax.dot_general(k, q, NT_DIMS, preferred_element_type=F32)
    if valid is not None:
        st = jnp.where(valid, st, -jnp.inf)
    st_ref[idx] = st
    return jnp.max(st, axis=0, keepdims=True)


def _accumulate_t(st_ref, idx, cmax, m, acc_ref, aidx, vt, guard_empty):
    m_new = jnp.maximum(m, cmax)
    m_use = jnp.where(m_new == -jnp.inf, 0.0, m_new) if guard_empty else m_new
    alpha = jnp.exp2(m - m_use)
    pt = jnp.exp2((st_ref[idx] - m_use).astype(BF16))
    acc_ref[aidx] = alpha * acc_ref[aidx] + jnp.dot(vt, pt, preferred_element_type=F32)
    return m_new


def _pipelined_sweep(n, score_fn, accum_fn, unroll=4):
    score_fn(0, 0)
    stages = n - 1

    def stage(e, parity):
        score_fn(e + 1, 1 - parity)
        accum_fn(e, parity)

    def body(i, c):
        for k in range(unroll):
            stage(unroll * i + k, k % 2)
        return c

    if isinstance(n, int):
        iters, left = divmod(stages, unroll)
        if iters:
            lax.fori_loop(0, iters, body, 0)
        for k in range(left):
            stage(iters * unroll + k, k % 2)
        accum_fn(n - 1, left % 2)
        return
    lax.fori_loop(0, stages // unroll, body, 0)
    base = stages // unroll * unroll
    left = stages - base
    for k in range(unroll - 1):
        pl.when(left > k)(functools.partial(stage, base + k, k % 2))
    for parity in range(2):
        pl.when(left % 2 == parity)(functools.partial(accum_fn, n - 1, parity))


def _out_proj_kernel(x_ref, ya_ref, yb_ref, ym_ref, w_ref, o_ref):
    wa = ya_ref.shape[1]
    wb = yb_ref.shape[1]
    acc = jnp.dot(ya_ref[...], w_ref[0:wa, :], preferred_element_type=F32)
    acc += jnp.dot(yb_ref[...], w_ref[wa:wa + wb, :], preferred_element_type=F32)
    acc += jnp.dot(ym_ref[...], w_ref[wa + wb:, :], preferred_element_type=F32)
    o_ref[...] = x_ref[...] + acc


def _out_proj(x, ya, yb, ym, w, tm):
    n, d = x.shape
    return pl.pallas_call(
        _out_proj_kernel,
        grid=(n // tm,),
        in_specs=[pl.BlockSpec((tm, d), lambda i: (i, 0)),
                  pl.BlockSpec((tm, ya.shape[1]), lambda i: (i, 0)),
                  pl.BlockSpec((tm, yb.shape[1]), lambda i: (i, 0)),
                  pl.BlockSpec((tm, ym.shape[1]), lambda i: (i, 0)),
                  pl.BlockSpec(w.shape, lambda i: (0, 0))],
        out_specs=pl.BlockSpec((tm, d), lambda i: (i, 0)),
        out_shape=jax.ShapeDtypeStruct((n, d), F32),
        compiler_params=_cparams(("parallel",), 32),
        name="out_proj",
    )(x, ya, yb, ym, w)


def _sgu_kernel(u_ref, v_ref, g_ref, ws_ref, bs_ref, vg_ref, o_ref):
    tm = u_ref.shape[0]
    gdim = u_ref.shape[1] // A_GROUPS
    row = lax.broadcasted_iota(jnp.int32, (A_CHUNK, A_CHUNK), 0)
    col = lax.broadcasted_iota(jnp.int32, (A_CHUNK, A_CHUNK), 1)
    causal = col <= row
    for g in range(A_GROUPS):
        cols = slice(g * gdim, (g + 1) * gdim)
        gv = jax.nn.gelu(v_ref[:, cols])
        ms = jnp.mean(gv * gv, axis=-1, keepdims=True)
        vn = (gv * lax.rsqrt(ms + EPS) * vg_ref[:, cols]).astype(BF16)
        w = jnp.where(causal, ws_ref[g], 0.0).astype(BF16)
        for c in range(tm // A_CHUNK):
            rows = slice(c * A_CHUNK, (c + 1) * A_CHUNK)
            z = jnp.dot(w, vn[rows, :], preferred_element_type=F32) + bs_ref[:, cols]
            y = jax.nn.gelu(u_ref[rows, cols]) * z * jax.nn.silu(g_ref[rows, cols])
            o_ref[rows, cols] = y.astype(BF16)


def _sgu(p0, a_ws, bs_exp, vgain, tm):
    n = p0.shape[0]
    width = vgain.shape[1]
    return pl.pallas_call(
        _sgu_kernel,
        grid=(n // tm,),
        in_specs=[pl.BlockSpec((tm, width), lambda i: (i, 0)),
                  pl.BlockSpec((tm, width), lambda i: (i, 1)),
                  pl.BlockSpec((tm, width), lambda i: (i, 2)),
                  pl.BlockSpec(a_ws.shape, lambda i: (0, 0, 0)),
                  pl.BlockSpec(bs_exp.shape, lambda i: (0, 0)),
                  pl.BlockSpec(vgain.shape, lambda i: (0, 0))],
        out_specs=pl.BlockSpec((tm, width), lambda i: (i, 0)),
        out_shape=jax.ShapeDtypeStruct((n, width), BF16),
        compiler_params=_cparams(("parallel",), 32),
        name="sgu",
    )(p0, p0, p0, a_ws, bs_exp, vgain)


def _proj_diff_prep_kernel(x_ref, ng_ref, w_ref, wqkv_ref, c_ref, s1_ref, s2_ref, qg_ref, kg_ref,
                           p_ref, qo_ref, ko_ref, vto_ref):
    x = x_ref[...]
    ms = jnp.mean(x * x, axis=-1, keepdims=True)
    h = (x * lax.rsqrt(ms + EPS) * ng_ref[...]).astype(BF16)
    qkv = jnp.dot(h, wqkv_ref[...], preferred_element_type=F32)
    width = qg_ref.shape[1]
    half = B_QK_DIM // 4 // 2
    project = functools.partial(_project_chunk, h, w_ref, p_ref, 3)
    for idx, gain, dst, mul in ((0, qg_ref, qo_ref, B_QK_DIM ** -0.5 * LOG2E), (1, kg_ref, ko_ref, 1.0)):
        project(idx)
        t = qkv[:, idx * width:(idx + 1) * width]
        tn = t * lax.rsqrt(_group_mean(t * t, B_QK_DIM) + EPS) * gain[...]
        tr = _rope(tn, c_ref[...], s1_ref[...], s2_ref[...], half)
        dst[...] = (tr * mul if mul != 1.0 else tr).astype(BF16)
    project(2)
    vt = qkv[:, 2 * width:3 * width].T
    dv = 2 * B_QK_DIM
    ones = jnp.ones((ONES_ROWS, vt.shape[1]), F32)
    parts = []
    for hd in range(B_HEADS):
        parts += [vt[hd * dv:(hd + 1) * dv, :], ones]
    vto_ref[0, 0] = jnp.concatenate(parts, axis=0).astype(BF16)


def _proj_diff_prep(x, norm_g, w_rest, w_qkv, tables, qgain, kgain, batch, seq, tm):
    n, d = x.shape
    width = qgain.shape[1]
    tiles_per_seq = seq // tm
    tab_spec = pl.BlockSpec((tm, LANES), lambda i: (i % tiles_per_seq, 0))
    out = jax.ShapeDtypeStruct((n, width), BF16)
    vrows = B_HEADS * (2 * B_QK_DIM + ONES_ROWS)

    def full(a):
        return pl.BlockSpec(a.shape, lambda i: (0,) * a.ndim)

    return pl.pallas_call(
        _proj_diff_prep_kernel,
        grid=(n // tm,),
        in_specs=[pl.BlockSpec((tm, d), lambda i: (i, 0)), pl.BlockSpec((1, d), lambda i: (0, 0)),
                  full(w_rest), full(w_qkv), tab_spec, tab_spec, tab_spec, full(qgain), full(kgain)],
        out_specs=[pl.BlockSpec((tm, w_rest.shape[1]), lambda i: (i, 0)),
                   pl.BlockSpec((tm, width), lambda i: (i, 0)),
                   pl.BlockSpec((tm, width), lambda i: (i, 0)),
                   pl.BlockSpec((1, 1, vrows, tm),
                                lambda i: (i // tiles_per_seq, i % tiles_per_seq, 0, 0))],
        out_shape=[jax.ShapeDtypeStruct((n, w_rest.shape[1]), F32), out, out,
                   jax.ShapeDtypeStruct((batch, tiles_per_seq, vrows, tm), BF16)],
        compiler_params=_cparams(("parallel",), 48),
        name="proj_diff_prep",
    )(x, norm_g.reshape(1, d), w_rest, w_qkv, *tables, qgain, kgain)


def _diff_attn_kernel(qi_tab, tile_tab, lam_ref, q_ref, k_ref, vt_ref, g_ref, sub_ref, o_ref,
                      acc_ref, sta_ref, stb_ref, cmax_ref, m_ref, *, tq, qt, lam_init):
    dv = 2 * B_QK_DIM
    vrows = dv + ONES_ROWS
    nmap = 4
    lane = lax.broadcasted_iota(jnp.int32, (1, LANES), 1)
    bufs = (sta_ref, stb_ref)
    acc_ref[...] = jnp.zeros_like(acc_ref)
    m_ref[...] = jnp.full(m_ref.shape, -jnp.inf, F32)
    krow = lax.broadcasted_iota(jnp.int32, (tq, tq), 0)
    qcol = lax.broadcasted_iota(jnp.int32, (tq, tq), 1)
    lam_p = lam_ref[...]
    lam = (jnp.exp(jnp.sum(lam_p[0:1] * lam_p[1:2], axis=-1, keepdims=True))
           - jnp.exp(jnp.sum(lam_p[2:3] * lam_p[3:4], axis=-1, keepdims=True)) + lam_init)

    def rows(tile):
        return pl.ds(pl.multiple_of(tile * tq, tq), tq)

    def scores(qi, tile, buf, valid):
        q = q_ref[rows(qi), :]
        k = k_ref[rows(tile), :]
        zero = jnp.zeros_like(q)
        for j in range(nmap):
            qm = jnp.where(lane // B_QK_DIM == j, q, zero)
            row = buf * nmap + j
            cmax_ref[row:row + 1, :] = _scores_t(k, qm, valid, bufs[buf], j)

    def accumulate(qi, tile, buf):
        for j in range(nmap):
            vt = vt_ref[0, tile, (j // 2) * vrows:(j // 2 + 1) * vrows, :]
            cm, sr = buf * nmap + j, qi * nmap + j
            m_ref[pl.ds(sr, 1), :] = _accumulate_t(bufs[buf], j, cmax_ref[cm:cm + 1, :],
                                                   m_ref[pl.ds(sr, 1), :], acc_ref, sr, vt, False)

    def finish(qi):
        halves = []
        for h in range(2):
            o = [acc_ref[qi * nmap + 2 * h + mp, 0:dv, :] / acc_ref[qi * nmap + 2 * h + mp, dv:dv + 1, :]
                 for mp in range(2)]
            pd = o[0] - lam * o[1]
            ms2 = jnp.mean(pd * pd, axis=0, keepdims=True)
            halves.append(pd * lax.rsqrt(ms2 + EPS))
        ob = jnp.concatenate(halves, axis=0).T * sub_ref[...] * (1.0 - lam_init)
        o_ref[rows(qi), :] = (ob * jax.nn.silu(g_ref[rows(qi), :])).astype(BF16)

    def below_score(e, buf):
        scores(qi_tab[e], tile_tab[e], buf, None)

    def below_accum(e, buf):
        accumulate(qi_tab[e], tile_tab[e], buf)

    def diag_score(e, buf):
        scores(e, e, buf, krow <= qcol)

    def diag_accum(e, buf):
        accumulate(e, e, buf)
        finish(e)

    _pipelined_sweep(qt * (qt - 1) // 2, below_score, below_accum, unroll=8)
    _pipelined_sweep(qt, diag_score, diag_accum)


def _diff_attn(lam_p, qn, kn, vt, p0, subln, batch, seq, tq, lam_init, g_off):
    n, width = qn.shape
    pairs = width // LANES
    qt = seq // tq
    vrows = 2 * B_QK_DIM + ONES_ROWS
    below = [(qi, t) for qi in range(qt) for t in range(qi)]
    qi_tab = jnp.asarray([e[0] for e in below], jnp.int32)
    tile_tab = jnp.asarray([e[1] for e in below], jnp.int32)
    kernel = functools.partial(_diff_attn_kernel, tq=tq, qt=qt, lam_init=lam_init)
    seq_spec = pl.BlockSpec((seq, LANES), lambda b, p, *_: (b, p))
    return pl.pallas_call(
        kernel,
        grid_spec=pltpu.PrefetchScalarGridSpec(
            num_scalar_prefetch=2,
            grid=(batch, pairs),
            in_specs=[pl.BlockSpec(lam_p.shape, lambda b, p, *_: (0, 0)),
                      seq_spec, seq_spec,
                      pl.BlockSpec((1, qt, 2 * vrows, tq), lambda b, p, *_: (b, 0, p, 0)),
                      pl.BlockSpec((seq, LANES), lambda b, p, *_: (b, g_off + p)),
                      pl.BlockSpec((1, LANES), lambda b, p, *_: (0, 0))],
            out_specs=seq_spec,
            scratch_shapes=[pltpu.VMEM((qt * 4, vrows, tq), F32), pltpu.VMEM((4, tq, tq), F32),
                            pltpu.VMEM((4, tq, tq), F32), pltpu.VMEM((8, tq), F32),
                            pltpu.VMEM((qt * 4, tq), F32)]),
        out_shape=jax.ShapeDtypeStruct((n, width), BF16),
        compiler_params=_cparams(("parallel", "parallel"), 40),
        name="diff_attn",
    )(qi_tab, tile_tab, lam_p, qn, kn, vt, p0, subln)


def _mem_kv_kernel(mem_ref, g_ref, w_ref, kg_ref, kh_ref, vt_ref):
    x = mem_ref[0]
    ms = jnp.mean(x * x, axis=-1, keepdims=True)
    h = (x * lax.rsqrt(ms + EPS) * g_ref[...]).astype(BF16)
    kv = jnp.dot(h, w_ref[...], preferred_element_type=F32)
    mw = kv.shape[1] // 4
    lane = lax.broadcasted_iota(jnp.int32, (1, mw), 1)
    ones = jnp.ones((ONES_ROWS, kv.shape[0]), F32)
    for layer in range(2):
        k = kv[:, (2 * layer) * mw:(2 * layer + 1) * mw]
        kn = k * lax.rsqrt(_group_mean(k * k, HEAD_DIM) + EPS) * kg_ref[layer:layer + 1, :]
        vt = kv[:, (2 * layer + 1) * mw:(2 * layer + 2) * mw].T
        parts = []
        for hd in range(M_HEADS):
            kh_ref[layer, 0, hd] = jnp.where(lane // HEAD_DIM == hd, kn, 0.0).astype(BF16)
            parts += [vt[hd * HEAD_DIM:(hd + 1) * HEAD_DIM, :], ones]
        vt_ref[layer, 0] = jnp.concatenate(parts, axis=0).astype(BF16)


def _mem_kv(mem, mem_norm, wkv_both, kgains):
    batch, mtok, d = mem.shape
    mw = wkv_both.shape[1] // 4
    vrows = M_HEADS * (HEAD_DIM + ONES_ROWS)
    return pl.pallas_call(
        _mem_kv_kernel,
        grid=(batch,),
        in_specs=[pl.BlockSpec((1, mtok, d), lambda b: (b, 0, 0)),
                  pl.BlockSpec((1, d), lambda b: (0, 0)),
                  pl.BlockSpec(wkv_both.shape, lambda b: (0, 0)),
                  pl.BlockSpec(kgains.shape, lambda b: (0, 0))],
        out_specs=[pl.BlockSpec((2, 1, M_HEADS, mtok, mw), lambda b: (0, b, 0, 0, 0)),
                   pl.BlockSpec((2, 1, vrows, mtok), lambda b: (0, b, 0, 0))],
        out_shape=[jax.ShapeDtypeStruct((2, batch, M_HEADS, mtok, mw), BF16),
                   jax.ShapeDtypeStruct((2, batch, vrows, mtok), BF16)],
        compiler_params=_cparams(("parallel",), 32),
        name="mem_kv",
    )(mem, mem_norm.reshape(1, d), wkv_both, kgains)


def _mem_attn_kernel(q_ref, g_ref, kh_ref, vt_ref, qg_ref, o_ref, st_ref):
    vrows = HEAD_DIM + ONES_ROWS
    x = q_ref[...]
    qn = (x * lax.rsqrt(_group_mean(x * x, HEAD_DIM) + EPS) * qg_ref[...]
          * (HEAD_DIM ** -0.5 * LOG2E)).astype(BF16)
    cmax = [_scores_t(kh_ref[0, 0, h], qn, None, st_ref, h) for h in range(M_HEADS)]
    outs = []
    for h in range(M_HEADS):
        pt = jnp.exp2((st_ref[h] - cmax[h]).astype(BF16))
        acc = jnp.dot(vt_ref[0, 0, h * vrows:(h + 1) * vrows, :], pt, preferred_element_type=F32)
        outs.append(acc[0:HEAD_DIM, :] / acc[HEAD_DIM:HEAD_DIM + 1, :])
    out = jnp.concatenate(outs, axis=0).T
    o_ref[...] = (out * jax.nn.silu(g_ref[...])).astype(BF16)


def _mem_attn(p, kh, vt, layer, qgain, q_blk, g_blk, seq, tq):
    n = p.shape[0]
    _, batch, heads, mtok, mw = kh.shape
    qt = seq // tq
    return pl.pallas_call(
        _mem_attn_kernel,
        grid=(n // tq,),
        in_specs=[pl.BlockSpec((tq, mw), lambda i: (i, q_blk)),
                  pl.BlockSpec((tq, mw), lambda i: (i, g_blk)),
                  pl.BlockSpec((1, 1, heads, mtok, mw), lambda i: (layer, i // qt, 0, 0, 0)),
                  pl.BlockSpec((1, 1, vt.shape[2], mtok), lambda i: (layer, i // qt, 0, 0)),
                  pl.BlockSpec((1, mw), lambda i: (0, 0))],
        out_specs=pl.BlockSpec((tq, mw), lambda i: (i, 0)),
        out_shape=jax.ShapeDtypeStruct((n, mw), BF16),
        scratch_shapes=[pltpu.VMEM((heads, mtok, tq), F32)],
        compiler_params=_cparams(("parallel",), 32),
        name="mem_attn",
    )(p, p, kh, vt, qgain)


def _conv_kernel(a_ref, b_ref, g_ref, w_ref, cb_ref, ng_ref, o_ref, hbuf_ref, shift_ref):
    t = pl.program_id(1)
    ts = a_ref.shape[0]
    span = ts + HALO

    @pl.when(t == 0)
    def _():
        hbuf_ref[0:HALO, :] = jnp.zeros((HALO, hbuf_ref.shape[1]), F32)

    hbuf_ref[HALO:HALO + ts, :] = a_ref[...] * jax.nn.sigmoid(b_ref[...])
    for ph in range(SUBLANES):
        shift_ref[ph, 0:span - ph, :] = hbuf_ref[ph:span, :]
    first = HALO - (C_KERNEL - 1)
    y = None
    for j in range(C_KERNEL):
        ph, base = (first + j) % SUBLANES, (first + j) // SUBLANES * SUBLANES
        term = shift_ref[ph, base:base + ts, :] * w_ref[j:j + 1, :]
        y = term if y is None else y + term
    y += cb_ref[...]
    ms = jnp.mean(y * y, axis=-1, keepdims=True)
    yn = y * lax.rsqrt(ms + EPS) * ng_ref[...]
    o_ref[...] = (jax.nn.silu(yn) * jax.nn.silu(g_ref[...])).astype(BF16)
    hbuf_ref[0:HALO, :] = hbuf_ref[ts:ts + HALO, :]


def _conv_module(p1, conv_w, conv_b, norm_g, batch, seq, ts):
    n = p1.shape[0]
    width = conv_w.shape[1]
    st = seq // ts
    return pl.pallas_call(
        _conv_kernel,
        grid=(batch, st),
        in_specs=[pl.BlockSpec((ts, width), lambda b, t: (b * st + t, 0)),
                  pl.BlockSpec((ts, width), lambda b, t: (b * st + t, 1)),
                  pl.BlockSpec((ts, width), lambda b, t: (b * st + t, 2)),
                  pl.BlockSpec(conv_w.shape, lambda b, t: (0, 0)),
                  pl.BlockSpec((1, width), lambda b, t: (0, 0)),
                  pl.BlockSpec((1, width), lambda b, t: (0, 0))],
        out_specs=pl.BlockSpec((ts, width), lambda b, t: (b * st + t, 0)),
        out_shape=jax.ShapeDtypeStruct((n, width), BF16),
        scratch_shapes=[pltpu.VMEM((ts + HALO, width), F32),
                        pltpu.VMEM((SUBLANES, ts + HALO, width), F32)],
        compiler_params=_cparams(("arbitrary", "arbitrary"), 32),
        name="conv_module",
    )(p1, p1, p1, conv_w, conv_b.reshape(1, width), norm_g.reshape(1, width))


def _proj_nsa_prep_kernel(x_ref, ng_ref, w_ref, wd_ref, c_ref, s1_ref, s2_ref, qg_ref, kg_ref,
                          p_ref, qo_ref, ko_ref, vto_ref, gto_ref):
    x = x_ref[...]
    ms = jnp.mean(x * x, axis=-1, keepdims=True)
    h = (x * lax.rsqrt(ms + EPS) * ng_ref[...]).astype(BF16)
    d = jnp.dot(h, wd_ref[...], preferred_element_type=F32)
    project = functools.partial(_project_chunk, h, w_ref, p_ref, 3)
    qw = qg_ref.shape[1]
    half = HEAD_DIM // 4 // 2
    tabs = (c_ref[...], s1_ref[...], s2_ref[...])
    project(0)
    t = d[:, 0:qw]
    tn = t * lax.rsqrt(_group_mean(t * t, HEAD_DIM) + EPS) * qg_ref[...]
    qo_ref[...] = (_rope(tn, *tabs, half) * (HEAD_DIM ** -0.5 * LOG2E)).astype(BF16)
    ones = jnp.ones((ONES_ROWS, x.shape[0]), F32)
    lane = lax.broadcasted_iota(jnp.int32, (1, LANES), 1)
    for br in range(2):
        project(1 + br)
        k = d[:, qw + 2 * br * LANES:qw + (2 * br + 1) * LANES]
        kn = k * lax.rsqrt(_group_mean(k * k, HEAD_DIM) + EPS) * kg_ref[...]
        kr = _rope(kn, *tabs, half)
        swapped = pltpu.roll(kr, HEAD_DIM, 1)
        for g in range(D_KV_GROUPS):
            own = (lane // HEAD_DIM) == g
            ko_ref[br, :, g * LANES:(g + 1) * LANES] = jnp.where(own, kr, swapped).astype(BF16)
        vt = d[:, qw + (2 * br + 1) * LANES:qw + (2 * br + 2) * LANES].T
        parts = []
        for g in range(D_KV_GROUPS):
            parts += [vt[g * HEAD_DIM:(g + 1) * HEAD_DIM, :], ones]
        vto_ref[0, br, 0] = jnp.concatenate(parts, axis=0).astype(BF16)
    gto_ref[0, 0] = jax.nn.sigmoid(d[:, qw + 4 * LANES:qw + 5 * LANES]).T


def _proj_nsa_prep(x, norm_g, w_rest, w_d, tables, qgain, kgain, batch, seq, tm):
    n, dm = x.shape
    tiles_per_seq = seq // tm
    tab_spec = pl.BlockSpec((tm, LANES), lambda i: (i % tiles_per_seq, 0))
    vrows = D_KV_GROUPS * (HEAD_DIM + ONES_ROWS)

    def full(a):
        return pl.BlockSpec(a.shape, lambda i: (0,) * a.ndim)

    return pl.pallas_call(
        _proj_nsa_prep_kernel,
        grid=(n // tm,),
        in_specs=[pl.BlockSpec((tm, dm), lambda i: (i, 0)), pl.BlockSpec((1, dm), lambda i: (0, 0)),
                  full(w_rest), full(w_d), tab_spec, tab_spec, tab_spec, full(qgain), full(kgain)],
        out_specs=[pl.BlockSpec((tm, w_rest.shape[1]), lambda i: (i, 0)),
                   pl.BlockSpec((tm, qgain.shape[1]), lambda i: (i, 0)),
                   pl.BlockSpec((2, tm, D_KV_GROUPS * LANES), lambda i: (0, i, 0)),
                   pl.BlockSpec((1, 2, 1, vrows, tm),
                                lambda i: (i // tiles_per_seq, 0, i % tiles_per_seq, 0, 0)),
                   pl.BlockSpec((1, 1, LANES, tm),
                                lambda i: (i // tiles_per_seq, i % tiles_per_seq, 0, 0))],
        out_shape=[jax.ShapeDtypeStruct((n, w_rest.shape[1]), F32),
                   jax.ShapeDtypeStruct((n, qgain.shape[1]), BF16),
                   jax.ShapeDtypeStruct((2, n, D_KV_GROUPS * LANES), BF16),
                   jax.ShapeDtypeStruct((batch, 2, tiles_per_seq, vrows, tm), BF16),
                   jax.ShapeDtypeStruct((batch, tiles_per_seq, LANES, tm), F32)],
        compiler_params=_cparams(("parallel",), 48),
        name="proj_nsa_prep",
    )(x, norm_g.reshape(1, dm), w_rest, w_d, *tables, qgain, kgain)


def _compress_kernel(tk_ref, tv_ref, pk_ref, pv_ref, w1k_ref, w2k_ref, w1v_ref, w2v_ref,
                     kg_ref, c_ref, s1_ref, s2_ref, kc_ref, vct_ref):
    half = HEAD_DIM // 4 // 2
    ncp = kc_ref.shape[2]

    def mlp(t_ref, p_ref, w1_ref, w2):
        a = jnp.zeros((ncp, w1_ref.shape[3]), F32)
        b = jnp.zeros((ncp, w1_ref.shape[3]), F32)
        for l in range(CMP_STRIDE):
            x = t_ref[pl.ds(l, ncp, stride=CMP_STRIDE), :]
            a += jnp.dot((x + p_ref[l:l + 1, :]).astype(BF16), w1_ref[0, l],
                         preferred_element_type=F32)
            b += jnp.dot((x + p_ref[CMP_STRIDE + l:CMP_STRIDE + l + 1, :]).astype(BF16),
                         w1_ref[0, CMP_STRIDE + l], preferred_element_type=F32)
        h = a + pltpu.roll(b, ncp - 1, 0)
        return jnp.dot(jax.nn.silu(h).astype(BF16), w2, preferred_element_type=F32)

    kc = mlp(tk_ref, pk_ref, w1k_ref, w2k_ref[...])
    kn = kc * lax.rsqrt(_group_mean(kc * kc, HEAD_DIM) + EPS) * kg_ref[...]
    kc_ref[0, 0] = _rope(kn, c_ref[...], s1_ref[...], s2_ref[...], half).astype(BF16)
    vc = mlp(tv_ref, pv_ref, w1v_ref, w2v_ref[...])
    vct_ref[0, 0] = vc.T[0:HEAD_DIM, :].astype(BF16)


def _compress(p1, k_blk, pk, pv, w1k, w2k, w1v, w2v, kgain, tables, batch, seq):
    groups = w1k.shape[0]
    ncp = seq // CMP_STRIDE

    def full(a):
        return pl.BlockSpec(a.shape, lambda b, g: (0,) * a.ndim)

    def per_group(a):
        return pl.BlockSpec((1,) + a.shape[1:], lambda b, g: (g,) + (0,) * (a.ndim - 1))

    return pl.pallas_call(
        _compress_kernel,
        grid=(batch, groups),
        in_specs=[pl.BlockSpec((seq, LANES), lambda b, g: (b, k_blk)),
                  pl.BlockSpec((seq, LANES), lambda b, g: (b, k_blk + 1)),
                  full(pk), full(pv), per_group(w1k), full(w2k), per_group(w1v), full(w2v),
                  full(kgain), full(tables[0]), full(tables[1]), full(tables[2])],
        out_specs=[pl.BlockSpec((1, 1, ncp, LANES), lambda b, g: (b, g, 0, 0)),
                   pl.BlockSpec((1, 1, HEAD_DIM, ncp), lambda b, g: (b, g, 0, 0))],
        out_shape=[jax.ShapeDtypeStruct((batch, groups, ncp, LANES), BF16),
                   jax.ShapeDtypeStruct((batch, groups, HEAD_DIM, ncp), BF16)],
        compiler_params=_cparams(("parallel", "parallel"), 32),
        name="nsa_compress",
    )(p1, p1, pk, pv, w1k, w2k, w1v, w2v, kgain, *tables)


def _nsa_kernel(qi_tab, tile_tab, br_tab, q_ref, kc_ref, vct_ref, k_ref, vt_ref, gt_ref,
                dg_ref, ovt_ref, o_ref, acc_ref, oc_ref, sel_ref, imp_ref, sta_ref, stb_ref, cst_ref,
                cmax_ref, m_ref, *, tq, qt, nslc, nsel, nentries):
    g = pl.program_id(1)
    rheads = D_HEADS // D_KV_GROUPS
    seq = qt * tq
    tiny = float(np.finfo(np.float32).tiny)
    qcol = lax.broadcasted_iota(jnp.int32, (tq, tq), 1)
    krow = lax.broadcasted_iota(jnp.int32, (tq, tq), 0)
    bufs = (sta_ref, stb_ref)
    blocks_per_tile = tq // SLC_BLOCK
    nslab = -(-nslc // SUBLANES)
    sel_rows = nslab * SUBLANES

    def rows(tile):
        return pl.ds(pl.multiple_of(tile * tq, tq), tq)

    half_lane = lax.broadcasted_iota(jnp.int32, (1, LANES), 1) // HEAD_DIM

    def heads_q(qi):
        out = []
        for r in range(rheads):
            blk = q_ref[rows(qi), (r // 2) * LANES:(r // 2 + 1) * LANES]
            out.append(jnp.where(half_lane == r % 2, blk, jnp.zeros_like(blk)))
        return out

    kc = kc_ref[0, 0]
    vct = vct_ref[0, 0]
    ncp = kc.shape[0]
    ovt = ovt_ref[...]

    def select_blocks(qi, c):
        q = heads_q(qi)
        cend = lax.broadcasted_iota(jnp.int32, (ncp, tq), 0) * CMP_STRIDE + (CMP_BLOCK - 1)
        cvalid = cend <= qi * tq + lax.broadcasted_iota(jnp.int32, (ncp, tq), 1)
        psum = jnp.zeros((ncp, tq), F32)
        cmax = [_scores_t(kc, q[r], cvalid, cst_ref, r) for r in range(rheads)]
        for r in range(rheads):
            m = jnp.where(jnp.isfinite(cmax[r]), cmax[r], 0.0)
            e = jnp.exp2(cst_ref[r] - m)
            p = e / jnp.maximum(jnp.sum(e, axis=0, keepdims=True), tiny)
            psum += p
            oc_ref[qi * rheads + r] = jnp.dot(vct, p.astype(BF16), preferred_element_type=F32)
        hi, lo = _split_bf16(psum)
        imp = (jnp.dot(ovt, hi, preferred_element_type=F32)
               + jnp.dot(ovt, lo, preferred_element_type=F32))
        blk = lax.broadcasted_iota(jnp.int32, (NSLC_PAD, tq), 0)
        cur = (qi * tq + lax.broadcasted_iota(jnp.int32, (NSLC_PAD, tq), 1)) // SLC_BLOCK
        imp = jnp.where(blk > cur, -jnp.inf, imp)
        forced = (blk == 0) | (blk == cur) | (blk == cur - 1)
        imp = jnp.where(forced, jnp.inf, imp)
        imp_ref[...] = imp
        slabs = [imp[s * SUBLANES:(s + 1) * SUBLANES, :] for s in range(nslab)]
        ranks = [jnp.zeros((SUBLANES, tq), F32) for _ in range(nslab)]
        sub = lax.broadcasted_iota(jnp.int32, (SUBLANES, tq), 0)
        for jp in range(nslc):
            row = imp_ref[jp:jp + 1, :]
            for s in range(nslab):
                if s * SUBLANES > jp:
                    ahead = jnp.where(row >= slabs[s], 1.0, 0.0)
                elif (s + 1) * SUBLANES - 1 < jp:
                    ahead = jnp.where(row > slabs[s], 1.0, 0.0)
                else:
                    ahead = jnp.where(sub > jp - s * SUBLANES, jnp.where(row >= slabs[s], 1.0, 0.0),
                                      jnp.where(row > slabs[s], 1.0, 0.0))
                ranks[s] = ranks[s] + ahead
        for s in range(nslab):
            start = pl.multiple_of(qi * sel_rows + s * SUBLANES, SUBLANES)
            sel_ref[pl.ds(start, SUBLANES), :] = jnp.where(ranks[s] < nsel, 1.0, 0.0)
        return c

    lax.fori_loop(0, qt, select_blocks, 0)
    acc_ref[...] = jnp.zeros_like(acc_ref)
    m_ref[...] = jnp.full(m_ref.shape, -jnp.inf, F32)

    def score_fn(e, buf):
        qi, tile, br = qi_tab[e], tile_tab[e], br_tab[e]
        q = heads_q(qi)
        k = k_ref[br, rows(tile), :]
        base = qi * sel_rows + tile * blocks_per_tile
        chosen = jnp.concatenate(
            [jnp.broadcast_to(sel_ref[pl.ds(base + c, 1), :], (SLC_BLOCK, tq))
             for c in range(blocks_per_tile)], axis=0)
        dpos = (qi - tile) * tq + (qcol - krow)
        reach = jnp.where(br == 0, seq, WINDOW).astype(jnp.uint32)
        kept = jnp.where(jnp.maximum(chosen, br.astype(F32)) > 0.5, dpos, -1)
        valid = kept.astype(jnp.uint32) < reach
        for r in range(rheads):
            row = buf * rheads + r
            cmax_ref[row:row + 1, :] = _scores_t(k, q[r], valid, bufs[buf], r)

    def accum_fn(e, buf):
        qi, tile, br = qi_tab[e], tile_tab[e], br_tab[e]
        vt = vt_ref[0, br, tile]
        for r in range(rheads):
            cm, sr = buf * rheads + r, (qi * 2 + br) * rheads + r
            m_ref[pl.ds(sr, 1), :] = _accumulate_t(bufs[buf], r, cmax_ref[cm:cm + 1, :],
                                                   m_ref[pl.ds(sr, 1), :], acc_ref, sr, vt, True)

    _pipelined_sweep(nentries, score_fn, accum_fn)

    def finish(qi, c):
        outs = []
        for r in range(rheads):
            row0 = (g * rheads + r) * 3
            o = [oc_ref[qi * rheads + r]]
            for br in range(2):
                sr = (qi * 2 + br) * rheads + r
                o.append(acc_ref[sr, 0:HEAD_DIM, :]
                         / jnp.maximum(acc_ref[sr, HEAD_DIM:HEAD_DIM + 1, :], tiny))
            outs.append(sum(gt_ref[0, qi, pl.ds(row0 + j, 1), :] * o[j] for j in range(3)))
        out = jnp.concatenate(outs, axis=0).T
        o_ref[rows(qi), :] = (out * jax.nn.silu(dg_ref[rows(qi), :])).astype(BF16)
        return c

    lax.fori_loop(0, qt, finish, 0)


def _nsa_attn(qe, kc, vct, kboth, vtboth, gt, p1, ovt, batch, seq, tq, dg_blk):
    n = qe.shape[0]
    groups = D_KV_GROUPS
    rheads = D_HEADS // groups
    width = rheads * HEAD_DIM
    qt = seq // tq
    ncp = kc.shape[2]
    nslc = seq // SLC_BLOCK
    sel_rows = -(-nslc // SUBLANES) * SUBLANES
    wtiles = (WINDOW + tq - 1) // tq
    entries = []
    for qi in range(qt):
        entries += [(qi, t, 0) for t in range(qi + 1)]
        entries += [(qi, t, 1) for t in range(max(qi - wtiles, 0), qi + 1)]
    tabs = [jnp.asarray([e[c] for e in entries], jnp.int32) for c in range(3)]
    kernel = functools.partial(_nsa_kernel, tq=tq, qt=qt, nslc=nslc, nsel=min(SLC_TOPK, nslc),
                               nentries=len(entries))
    return pl.pallas_call(
        kernel,
        grid_spec=pltpu.PrefetchScalarGridSpec(
            num_scalar_prefetch=3,
            grid=(batch, groups),
            in_specs=[pl.BlockSpec((seq, width), lambda b, g, *_: (b, g)),
                      pl.BlockSpec((1, 1, ncp, LANES), lambda b, g, *_: (b, g, 0, 0)),
                      pl.BlockSpec((1, 1, HEAD_DIM, ncp), lambda b, g, *_: (b, g, 0, 0)),
                      pl.BlockSpec((2, seq, LANES), lambda b, g, *_: (0, b, g)),
                      pl.BlockSpec((1, 2, qt, HEAD_DIM + ONES_ROWS, tq), lambda b, g, *_: (b, 0, 0, g, 0)),
                      pl.BlockSpec((1, qt, LANES, tq), lambda b, g, *_: (b, 0, 0, 0)),
                      pl.BlockSpec((seq, width), lambda b, g, *_: (b, dg_blk + g)),
                      pl.BlockSpec(ovt.shape, lambda b, g, *_: (0, 0))],
            out_specs=pl.BlockSpec((seq, width), lambda b, g, *_: (b, g)),
            scratch_shapes=[pltpu.VMEM((qt * 2 * rheads, HEAD_DIM + ONES_ROWS, tq), F32),
                            pltpu.VMEM((qt * rheads, HEAD_DIM, tq), F32),
                            pltpu.VMEM((qt * sel_rows, tq), F32),
                            pltpu.VMEM((NSLC_PAD, tq), F32),
                            pltpu.VMEM((rheads, tq, tq), F32),
                            pltpu.VMEM((rheads, tq, tq), F32),
                            pltpu.VMEM((rheads, ncp, tq), F32),
                            pltpu.VMEM((2 * rheads, tq), F32),
                            pltpu.VMEM((qt * 2 * rheads, tq), F32)]),
        out_shape=jax.ShapeDtypeStruct((n, groups * width), BF16),
        compiler_params=_cparams(("parallel", "parallel"), 56),
        name="nsa_attn",
    )(*tabs, qe, kc, vct, kboth, vtboth, gt, p1, ovt)


def _rope_tables(pos, d):
    rd = d // 4
    half = rd // 2
    inv = ROPE_THETA ** (-jnp.arange(half, dtype=F32) / half)
    ang = pos.astype(F32)[:, None] * inv[None, :]
    cos, sin = jnp.cos(ang), jnp.sin(ang)
    npos = pos.shape[0]
    zeros = jnp.zeros((npos, d), F32)
    c = jnp.concatenate([cos, cos, jnp.ones((npos, d - rd), F32)], axis=1)
    s1 = zeros.at[:, :half].set(-sin)
    s2 = zeros.at[:, half:rd].set(sin)
    return tuple(jnp.tile(t, (1, LANES // d)) for t in (c, s1, s2))


def _overlap_t(ncp, nslc):
    start = np.arange(ncp) * CMP_STRIDE
    s0 = np.arange(nslc) * SLC_BLOCK
    lo = np.maximum(start[:, None], s0[None, :])
    hi = np.minimum(start[:, None] + CMP_BLOCK, s0[None, :] + SLC_BLOCK)
    ov = np.clip(hi - lo, 0, None) / CMP_BLOCK
    ov[ncp - 1] = 0.0
    out = np.zeros((NSLC_PAD, ncp), np.float32)
    out[:nslc] = ov.T
    return jnp.asarray(out, BF16)


def kernel(x, mem, mem_norm, l0_norm, l0_w_in, l0_a_vnorm, l0_a_ws, l0_a_bs, l0_b_qnorm, l0_b_knorm, l0_b_lq1, l0_b_lk1, l0_b_lq2, l0_b_lk2, l0_b_subln, l0_m_wkv, l0_m_qnorm, l0_m_knorm, l0_w_out, l1_norm, l1_w_in, l1_c_conv_w, l1_c_conv_b, l1_c_norm, l1_d_qnorm, l1_d_knorm, l1_d_cmp_pos_k, l1_d_cmp_w1_k, l1_d_cmp_w2_k, l1_d_cmp_pos_v, l1_d_cmp_w1_v, l1_d_cmp_w2_v, l1_m_wkv, l1_m_qnorm, l1_m_knorm, l1_w_out):
    batch, seq, d_model = x.shape
    n = batch * seq
    tm = 256
    tq = 256
    mw = M_HEADS * HEAD_DIM
    pos = jnp.arange(seq, dtype=jnp.int32)
    x0 = x.reshape(n, d_model)

    wkv_both = jnp.concatenate([l0_m_wkv, l1_m_wkv], axis=1).astype(BF16)
    kgains = jnp.stack([jnp.tile(l0_m_knorm, M_HEADS), jnp.tile(l1_m_knorm, M_HEADS)])
    mem_kh, mem_vt = _mem_kv(mem, mem_norm, wkv_both, kgains)

    w = l0_w_in
    tab32 = _rope_tables(pos, B_QK_DIM)
    b_width = 2 * B_HEADS * B_QK_DIM
    p0, qn, kn, vt = _proj_diff_prep(
        x0, l0_norm, jnp.concatenate([w[:, 0:1536], w[:, 3072:4096]], axis=1).astype(BF16),
        w[:, 1536:3072].astype(BF16), tab32,
        jnp.tile(l0_b_qnorm, b_width // B_QK_DIM).reshape(1, b_width),
        jnp.tile(l0_b_knorm, b_width // B_QK_DIM).reshape(1, b_width), batch, seq, tq)
    a_width = l0_a_vnorm.shape[0]
    bs_exp = jnp.repeat(l0_a_bs.T, a_width // A_GROUPS, axis=1)
    y_a = _sgu(p0, l0_a_ws, bs_exp, l0_a_vnorm.reshape(1, a_width), 512)
    lam_init = 0.8 - 0.6 * math.exp(-0.3 * 1)
    lam_p = jnp.stack([l0_b_lq1, l0_b_lk1, l0_b_lq2, l0_b_lk2])
    y_b = _diff_attn(lam_p, qn, kn, vt, p0, jnp.tile(l0_b_subln, 2).reshape(1, LANES),
                     batch, seq, tq, lam_init, 1536 // LANES)
    y_m = _mem_attn(p0, mem_kh, mem_vt, 0, jnp.tile(l0_m_qnorm, M_HEADS).reshape(1, mw),
                    2048 // mw, 2304 // mw, seq, 2 * tq)
    x1 = _out_proj(x0, y_a, y_b, y_m, l0_w_out.astype(BF16), tm)

    w = l1_w_in
    n_bg = 3 * D_HEADS
    w_rest = jnp.concatenate([w[:, 0:1536], w[:, 2840:3352], w[:, 2048:2304], w[:, 3352:3864]], axis=1)
    w_d = jnp.concatenate([w[:, 1536:2048], w[:, 2304:2816], w[:, 2816:2840],
                           jnp.zeros((d_model, LANES - n_bg), F32)], axis=1)
    groups, rheads = D_KV_GROUPS, D_HEADS // D_KV_GROUPS
    d_width = D_HEADS * HEAD_DIM
    tab64 = _rope_tables(pos, HEAD_DIM)
    p1, qe, kboth, vtboth, gt = _proj_nsa_prep(
        x1, l1_norm, w_rest.astype(BF16), w_d.astype(BF16), tab64,
        jnp.tile(l1_d_qnorm, D_HEADS).reshape(1, d_width),
        jnp.tile(l1_d_knorm, LANES // HEAD_DIM).reshape(1, LANES), batch, seq, tq)
    y_c = _conv_module(p1, l1_c_conv_w, l1_c_conv_b, l1_c_norm, batch, seq, 512)
    kv_blk = 2048 // LANES

    ncp = seq // CMP_STRIDE
    cmp_pos = jnp.arange(ncp, dtype=jnp.int32) * CMP_STRIDE + (CMP_BLOCK - 1)
    hidden = l1_d_cmp_w2_k.shape[0]

    def group_w1(w1):
        w = w1.reshape(CMP_BLOCK, HEAD_DIM, hidden)
        return jnp.stack([jnp.zeros((CMP_BLOCK, LANES, hidden), F32)
                          .at[:, g * HEAD_DIM:(g + 1) * HEAD_DIM, :].set(w) for g in range(groups)])

    w2k = jnp.tile(l1_d_cmp_w2_k, (1, LANES // HEAD_DIM))
    w2v = jnp.zeros((hidden, LANES), F32).at[:, :HEAD_DIM].set(l1_d_cmp_w2_v)
    kc, vct = _compress(
        p1, kv_blk, jnp.tile(l1_d_cmp_pos_k, (1, groups)), jnp.tile(l1_d_cmp_pos_v, (1, groups)),
        group_w1(l1_d_cmp_w1_k).astype(BF16), w2k.astype(BF16),
        group_w1(l1_d_cmp_w1_v).astype(BF16), w2v.astype(BF16),
        jnp.tile(l1_d_knorm, LANES // HEAD_DIM).reshape(1, LANES), _rope_tables(cmp_pos, HEAD_DIM),
        batch, seq)

    nslc = seq // SLC_BLOCK
    y_d = _nsa_attn(qe, kc, vct, kboth, vtboth, gt, p1, _overlap_t(ncp, nslc),
                    batch, seq, tq, 1536 // (rheads * HEAD_DIM))
    y_m1 = _mem_attn(p1, mem_kh, mem_vt, 1, jnp.tile(l1_m_qnorm, M_HEADS).reshape(1, mw),
                     2304 // mw, 2560 // mw, seq, 2 * tq)
    out = _out_proj(x1, y_c, y_d, y_m1, l1_w_out.astype(BF16), tm)
    return out.reshape(batch, seq, d_model)
```

```python
import functools
import math

import numpy as np
import jax
import jax.numpy as jnp
from jax import lax
from jax.experimental import pallas as pl
from jax.experimental.pallas import tpu as pltpu

F32 = jnp.float32
BF16 = jnp.bfloat16

EPS = 1e-6
ROPE_THETA = 500000.0
HEAD_DIM = 64
A_CHUNK = 128
A_GROUPS = 4
B_HEADS = 8
B_QK_DIM = 32
C_KERNEL = 31
D_HEADS = 8
D_KV_GROUPS = 2
CMP_BLOCK = 32
CMP_STRIDE = 16
SLC_BLOCK = 64
SLC_TOPK = 16
WINDOW = 512
M_HEADS = 4

LANES = 128
SUBLANES = 8
MXU_DIM = 256
HALO = 32
NSLC_PAD = 128
ONES_ROWS = 16

NT_DIMS = (((1,), (1,)), ((), ()))
LOG2E = math.log2(math.e)


def _cparams(semantics, vmem_mb):
    return pltpu.CompilerParams(dimension_semantics=semantics,
                                vmem_limit_bytes=vmem_mb * 1024 * 1024)


def _split_bf16(x):
    hi = x.astype(BF16)
    lo = (x - hi.astype(F32)).astype(BF16)
    return hi, lo


def _group_mean(x2, gsize):
    t, c = x2.shape
    w = min(c, MXU_DIM)
    r = lax.broadcasted_iota(jnp.int32, (w, w), 0) // gsize
    cc = lax.broadcasted_iota(jnp.int32, (w, w), 1) // gsize
    ones = jnp.where(r == cc, 1.0, 0.0).astype(BF16)
    outs = []
    for s in range(c // w):
        hi, lo = _split_bf16(x2[:, s * w:(s + 1) * w])
        outs.append(jnp.dot(hi, ones, preferred_element_type=F32)
                    + jnp.dot(lo, ones, preferred_element_type=F32))
    out = outs[0] if len(outs) == 1 else jnp.concatenate(outs, axis=1)
    return out * (1.0 / gsize)


def _tile_lanes(t, width):
    rep = width // t.shape[1]
    return t if rep == 1 else jnp.concatenate([t] * rep, axis=1)


def _rope(x, cos_t, s1_t, s2_t, half):
    width = x.shape[1]
    up = pltpu.roll(x, width - half, 1)
    dn = pltpu.roll(x, half, 1)
    return (x * _tile_lanes(cos_t, width) + up * _tile_lanes(s1_t, width)
            + dn * _tile_lanes(s2_t, width))


def _project_chunk(h, w_ref, p_ref, nchunks, c):
    tiles = p_ref.shape[1] // LANES
    lo, hi = (c * tiles // nchunks) * LANES, ((c + 1) * tiles // nchunks) * LANES
    p_ref[:, lo:hi] = jnp.dot(h, w_ref[:, lo:hi], preferred_element_type=F32)


def _scores_t(k, q, valid, st_ref, idx):
    st = lax.dot_general(k, q, NT_DIMS, preferred_element_type=F32)
    if valid is not None:
        st = jnp.where(valid, st, -jnp.inf)
    st_ref[idx] = st
    return jnp.max(st, axis=0, keepdims=True)


def _accumulate_t(st_ref, idx, cmax, m, acc_ref, aidx, vt, guard_empty):
    m_new = jnp.maximum(m, cmax)
    m_use = jnp.where(m_new == -jnp.inf, 0.0, m_new) if guard_empty else m_new
    alpha = jnp.exp2(m - m_use)
    pt = jnp.exp2((st_ref[idx] - m_use).astype(BF16))
    acc_ref[aidx] = alpha * acc_ref[aidx] + jnp.dot(vt, pt, preferred_element_type=F32)
    return m_new


def _pipelined_sweep(n, score_fn, accum_fn, unroll=4):
    score_fn(0, 0)
    stages = n - 1

    def stage(e, parity):
        score_fn(e + 1, 1 - parity)
        accum_fn(e, parity)

    def body(i, c):
        for k in range(unroll):
            stage(unroll * i + k, k % 2)
        return c

    if isinstance(n, int):
        iters, left = divmod(stages, unroll)
        if iters:
            lax.fori_loop(0, iters, body, 0)
        for k in range(left):
            stage(iters * unroll + k, k % 2)
        accum_fn(n - 1, left % 2)
        return
    lax.fori_loop(0, stages // unroll, body, 0)
    base = stages // unroll * unroll
    left = stages - base
    for k in range(unroll - 1):
        pl.when(left > k)(functools.partial(stage, base + k, k % 2))
    for parity in range(2):
        pl.when(left % 2 == parity)(functools.partial(accum_fn, n - 1, parity))


def _out_proj_kernel(x_ref, ya_ref, yb_ref, ym_ref, w_ref, o_ref):
    wa = ya_ref.shape[1]
    wb = yb_ref.shape[1]
    acc = jnp.dot(ya_ref[...], w_ref[0:wa, :], preferred_element_type=F32)
    acc += jnp.dot(yb_ref[...], w_ref[wa:wa + wb, :], preferred_element_type=F32)
    acc += jnp.dot(ym_ref[...], w_ref[wa + wb:, :], preferred_element_type=F32)
    o_ref[...] = x_ref[...] + acc


def _out_proj(x, ya, yb, ym, w, tm):
    n, d = x.shape
    return pl.pallas_call(
        _out_proj_kernel,
        grid=(n // tm,),
        in_specs=[pl.BlockSpec((tm, d), lambda i: (i, 0)),
                  pl.BlockSpec((tm, ya.shape[1]), lambda i: (i, 0)),
                  pl.BlockSpec((tm, yb.shape[1]), lambda i: (i, 0)),
                  pl.BlockSpec((tm, ym.shape[1]), lambda i: (i, 0)),
                  pl.BlockSpec(w.shape, lambda i: (0, 0))],
        out_specs=pl.BlockSpec((tm, d), lambda i: (i, 0)),
        out_shape=jax.ShapeDtypeStruct((n, d), F32),
        compiler_params=_cparams(("parallel",), 32),
        name="out_proj",
    )(x, ya, yb, ym, w)


def _sgu_kernel(u_ref, v_ref, g_ref, ws_ref, bs_ref, vg_ref, o_ref):
    tm = u_ref.shape[0]
    gdim = u_ref.shape[1] // A_GROUPS
    row = lax.broadcasted_iota(jnp.int32, (A_CHUNK, A_CHUNK), 0)
    col = lax.broadcasted_iota(jnp.int32, (A_CHUNK, A_CHUNK), 1)
    causal = col <= row
    for g in range(A_GROUPS):
        cols = slice(g * gdim, (g + 1) * gdim)
        gv = jax.nn.gelu(v_ref[:, cols])
        ms = jnp.mean(gv * gv, axis=-1, keepdims=True)
        vn = (gv * lax.rsqrt(ms + EPS) * vg_ref[:, cols]).astype(BF16)
        w = jnp.where(causal, ws_ref[g], 0.0).astype(BF16)
        for c in range(tm // A_CHUNK):
            rows = slice(c * A_CHUNK, (c + 1) * A_CHUNK)
            z = jnp.dot(w, vn[rows, :], preferred_element_type=F32) + bs_ref[:, cols]
            y = jax.nn.gelu(u_ref[rows, cols]) * z * jax.nn.silu(g_ref[rows, cols])
            o_ref[rows, cols] = y.astype(BF16)


def _sgu(p0, a_ws, bs_exp, vgain, tm):
    n = p0.shape[0]
    width = vgain.shape[1]
    return pl.pallas_call(
        _sgu_kernel,
        grid=(n // tm,),
        in_specs=[pl.BlockSpec((tm, width), lambda i: (i, 0)),
                  pl.BlockSpec((tm, width), lambda i: (i, 1)),
                  pl.BlockSpec((tm, width), lambda i: (i, 2)),
                  pl.BlockSpec(a_ws.shape, lambda i: (0, 0, 0)),
                  pl.BlockSpec(bs_exp.shape, lambda i: (0, 0)),
                  pl.BlockSpec(vgain.shape, lambda i: (0, 0))],
        out_specs=pl.BlockSpec((tm, width), lambda i: (i, 0)),
        out_shape=jax.ShapeDtypeStruct((n, width), BF16),
        compiler_params=_cparams(("parallel",), 32),
        name="sgu",
    )(p0, p0, p0, a_ws, bs_exp, vgain)


def _proj_diff_prep_kernel(x_ref, ng_ref, w_ref, wqkv_ref, c_ref, s1_ref, s2_ref, qg_ref, kg_ref,
                           p_ref, qo_ref, ko_ref, vto_ref):
    x = x_ref[...]
    ms = jnp.mean(x * x, axis=-1, keepdims=True)
    h = (x * lax.rsqrt(ms + EPS) * ng_ref[...]).astype(BF16)
    qkv = jnp.dot(h, wqkv_ref[...], preferred_element_type=F32)
    width = qg_ref.shape[1]
    half = B_QK_DIM // 4 // 2
    project = functools.partial(_project_chunk, h, w_ref, p_ref, 3)
    for idx, gain, dst, mul in ((0, qg_ref, qo_ref, B_QK_DIM ** -0.5 * LOG2E), (1, kg_ref, ko_ref, 1.0)):
        project(idx)
        t = qkv[:, idx * width:(idx + 1) * width]
        tn = t * lax.rsqrt(_group_mean(t * t, B_QK_DIM) + EPS) * gain[...]
        tr = _rope(tn, c_ref[...], s1_ref[...], s2_ref[...], half)
        dst[...] = (tr * mul if mul != 1.0 else tr).astype(BF16)
    project(2)
    vt = qkv[:, 2 * width:3 * width].T
    dv = 2 * B_QK_DIM
    ones = jnp.ones((ONES_ROWS, vt.shape[1]), F32)
    parts = []
    for hd in range(B_HEADS):
        parts += [vt[hd * dv:(hd + 1) * dv, :], ones]
    vto_ref[0, 0] = jnp.concatenate(parts, axis=0).astype(BF16)


def _proj_diff_prep(x, norm_g, w_rest, w_qkv, tables, qgain, kgain, batch, seq, tm):
    n, d = x.shape
    width = qgain.shape[1]
    tiles_per_seq = seq // tm
    tab_spec = pl.BlockSpec((tm, LANES), lambda i: (i % tiles_per_seq, 0))
    out = jax.ShapeDtypeStruct((n, width), BF16)
    vrows = B_HEADS * (2 * B_QK_DIM + ONES_ROWS)

    def full(a):
        return pl.BlockSpec(a.shape, lambda i: (0,) * a.ndim)

    return pl.pallas_call(
        _proj_diff_prep_kernel,
        grid=(n // tm,),
        in_specs=[pl.BlockSpec((tm, d), lambda i: (i, 0)), pl.BlockSpec((1, d), lambda i: (0, 0)),
                  full(w_rest), full(w_qkv), tab_spec, tab_spec, tab_spec, full(qgain), full(kgain)],
        out_specs=[pl.BlockSpec((tm, w_rest.shape[1]), lambda i: (i, 0)),
                   pl.BlockSpec((tm, width), lambda i: (i, 0)),
                   pl.BlockSpec((tm, width), lambda i: (i, 0)),
                   pl.BlockSpec((1, 1, vrows, tm),
                                lambda i: (i // tiles_per_seq, i % tiles_per_seq, 0, 0))],
        out_shape=[jax.ShapeDtypeStruct((n, w_rest.shape[1]), F32), out, out,
                   jax.ShapeDtypeStruct((batch, tiles_per_seq, vrows, tm), BF16)],
        compiler_params=_cparams(("parallel",), 48),
        name="proj_diff_prep",
    )(x, norm_g.reshape(1, d), w_rest, w_qkv, *tables, qgain, kgain)


def _diff_attn_kernel(qi_tab, tile_tab, lam_ref, q_ref, k_ref, vt_ref, g_ref, sub_ref, o_ref,
                      acc_ref, sta_ref, stb_ref, cmax_ref, m_ref, *, tq, qt, lam_init):
    dv = 2 * B_QK_DIM
    vrows = dv + ONES_ROWS
    nmap = 4
    lane = lax.broadcasted_iota(jnp.int32, (1, LANES), 1)
    bufs = (sta_ref, stb_ref)
    acc_ref[...] = jnp.zeros_like(acc_ref)
    m_ref[...] = jnp.full(m_ref.shape, -jnp.inf, F32)
    krow = lax.broadcasted_iota(jnp.int32, (tq, tq), 0)
    qcol = lax.broadcasted_iota(jnp.int32, (tq, tq), 1)
    lam_p = lam_ref[...]
    lam = (jnp.exp(jnp.sum(lam_p[0:1] * lam_p[1:2], axis=-1, keepdims=True))
           - jnp.exp(jnp.sum(lam_p[2:3] * lam_p[3:4], axis=-1, keepdims=True)) + lam_init)

    def rows(tile):
        return pl.ds(pl.multiple_of(tile * tq, tq), tq)

    def scores(qi, tile, buf, valid):
        q = q_ref[rows(qi), :]
        k = k_ref[rows(tile), :]
        zero = jnp.zeros_like(q)
        for j in range(nmap):
            qm = jnp.where(lane // B_QK_DIM == j, q, zero)
            row = buf * nmap + j
            cmax_ref[row:row + 1, :] = _scores_t(k, qm, valid, bufs[buf], j)

    def accumulate(qi, tile, buf):
        for j in range(nmap):
            vt = vt_ref[0, tile, (j // 2) * vrows:(j // 2 + 1) * vrows, :]
            cm, sr = buf * nmap + j, qi * nmap + j
            m_ref[pl.ds(sr, 1), :] = _accumulate_t(bufs[buf], j, cmax_ref[cm:cm + 1, :],
                                                   m_ref[pl.ds(sr, 1), :], acc_ref, sr, vt, False)

    def finish(qi):
        halves = []
        for h in range(2):
            o = [acc_ref[qi * nmap + 2 * h + mp, 0:dv, :] / acc_ref[qi * nmap + 2 * h + mp, dv:dv + 1, :]
                 for mp in range(2)]
            pd = o[0] - lam * o[1]
            ms2 = jnp.mean(pd * pd, axis=0, keepdims=True)
            halves.append(pd * lax.rsqrt(ms2 + EPS))
        ob = jnp.concatenate(halves, axis=0).T * sub_ref[...] * (1.0 - lam_init)
        o_ref[rows(qi), :] = (ob * jax.nn.silu(g_ref[rows(qi), :])).astype(BF16)

    def below_score(e, buf):
        scores(qi_tab[e], tile_tab[e], buf, None)

    def below_accum(e, buf):
        accumulate(qi_tab[e], tile_tab[e], buf)

    def diag_score(e, buf):
        scores(e, e, buf, krow <= qcol)

    def diag_accum(e, buf):
        accumulate(e, e, buf)
        finish(e)

    _pipelined_sweep(qt * (qt - 1) // 2, below_score, below_accum, unroll=8)
    _pipelined_sweep(qt, diag_score, diag_accum)


def _diff_attn(lam_p, qn, kn, vt, p0, subln, batch, seq, tq, lam_init, g_off):
    n, width = qn.shape
    pairs = width // LANES
    qt = seq // tq
    vrows = 2 * B_QK_DIM + ONES_ROWS
    below = [(qi, t) for qi in range(qt) for t in range(qi)]
    qi_tab = jnp.asarray([e[0] for e in below], jnp.int32)
    tile_tab = jnp.asarray([e[1] for e in below], jnp.int32)
    kernel = functools.partial(_diff_attn_kernel, tq=tq, qt=qt, lam_init=lam_init)
    seq_spec = pl.BlockSpec((seq, LANES), lambda b, p, *_: (b, p))
    return pl.pallas_call(
        kernel,
        grid_spec=pltpu.PrefetchScalarGridSpec(
            num_scalar_prefetch=2,
            grid=(batch, pairs),
            in_specs=[pl.BlockSpec(lam_p.shape, lambda b, p, *_: (0, 0)),
                      seq_spec, seq_spec,
                      pl.BlockSpec((1, qt, 2 * vrows, tq), lambda b, p, *_: (b, 0, p, 0)),
                      pl.BlockSpec((seq, LANES), lambda b, p, *_: (b, g_off + p)),
                      pl.BlockSpec((1, LANES), lambda b, p, *_: (0, 0))],
            out_specs=seq_spec,
            scratch_shapes=[pltpu.VMEM((qt * 4, vrows, tq), F32), pltpu.VMEM((4, tq, tq), F32),
                            pltpu.VMEM((4, tq, tq), F32), pltpu.VMEM((8, tq), F32),
                            pltpu.VMEM((qt * 4, tq), F32)]),
        out_shape=jax.ShapeDtypeStruct((n, width), BF16),
        compiler_params=_cparams(("parallel", "parallel"), 40),
        name="diff_attn",
    )(qi_tab, tile_tab, lam_p, qn, kn, vt, p0, subln)


def _mem_kv_kernel(mem_ref, g_ref, w_ref, kg_ref, kh_ref, vt_ref):
    x = mem_ref[0]
    ms = jnp.mean(x * x, axis=-1, keepdims=True)
    h = (x * lax.rsqrt(ms + EPS) * g_ref[...]).astype(BF16)
    kv = jnp.dot(h, w_ref[...], preferred_element_type=F32)
    mw = kv.shape[1] // 4
    lane = lax.broadcasted_iota(jnp.int32, (1, mw), 1)
    ones = jnp.ones((ONES_ROWS, kv.shape[0]), F32)
    for layer in range(2):
        k = kv[:, (2 * layer) * mw:(2 * layer + 1) * mw]
        kn = k * lax.rsqrt(_group_mean(k * k, HEAD_DIM) + EPS) * kg_ref[layer:layer + 1, :]
        vt = kv[:, (2 * layer + 1) * mw:(2 * layer + 2) * mw].T
        parts = []
        for hd in range(M_HEADS):
            kh_ref[layer, 0, hd] = jnp.where(lane // HEAD_DIM == hd, kn, 0.0).astype(BF16)
            parts += [vt[hd * HEAD_DIM:(hd + 1) * HEAD_DIM, :], ones]
        vt_ref[layer, 0] = jnp.concatenate(parts, axis=0).astype(BF16)


def _mem_kv(mem, mem_norm, wkv_both, kgains):
    batch, mtok, d = mem.shape
    mw = wkv_both.shape[1] // 4
    vrows = M_HEADS * (HEAD_DIM + ONES_ROWS)
    return pl.pallas_call(
        _mem_kv_kernel,
        grid=(batch,),
        in_specs=[pl.BlockSpec((1, mtok, d), lambda b: (b, 0, 0)),
                  pl.BlockSpec((1, d), lambda b: (0, 0)),
                  pl.BlockSpec(wkv_both.shape, lambda b: (0, 0)),
                  pl.BlockSpec(kgains.shape, lambda b: (0, 0))],
        out_specs=[pl.BlockSpec((2, 1, M_HEADS, mtok, mw), lambda b: (0, b, 0, 0, 0)),
                   pl.BlockSpec((2, 1, vrows, mtok), lambda b: (0, b, 0, 0))],
        out_shape=[jax.ShapeDtypeStruct((2, batch, M_HEADS, mtok, mw), BF16),
                   jax.ShapeDtypeStruct((2, batch, vrows, mtok), BF16)],
        compiler_params=_cparams(("parallel",), 32),
        name="mem_kv",
    )(mem, mem_norm.reshape(1, d), wkv_both, kgains)


def _mem_attn_kernel(q_ref, g_ref, kh_ref, vt_ref, qg_ref, o_ref, st_ref):
    vrows = HEAD_DIM + ONES_ROWS
    x = q_ref[...]
    qn = (x * lax.rsqrt(_group_mean(x * x, HEAD_DIM) + EPS) * qg_ref[...]
          * (HEAD_DIM ** -0.5 * LOG2E)).astype(BF16)
    cmax = [_scores_t(kh_ref[0, 0, h], qn, None, st_ref, h) for h in range(M_HEADS)]
    outs = []
    for h in range(M_HEADS):
        pt = jnp.exp2((st_ref[h] - cmax[h]).astype(BF16))
        acc = jnp.dot(vt_ref[0, 0, h * vrows:(h + 1) * vrows, :], pt, preferred_element_type=F32)
        outs.append(acc[0:HEAD_DIM, :] / acc[HEAD_DIM:HEAD_DIM + 1, :])
    out = jnp.concatenate(outs, axis=0).T
    o_ref[...] = (out * jax.nn.silu(g_ref[...])).astype(BF16)


def _mem_attn(p, kh, vt, layer, qgain, q_blk, g_blk, seq, tq):
    n = p.shape[0]
    _, batch, heads, mtok, mw = kh.shape
    qt = seq // tq
    return pl.pallas_call(
        _mem_attn_kernel,
        grid=(n // tq,),
        in_specs=[pl.BlockSpec((tq, mw), lambda i: (i, q_blk)),
                  pl.BlockSpec((tq, mw), lambda i: (i, g_blk)),
                  pl.BlockSpec((1, 1, heads, mtok, mw), lambda i: (layer, i // qt, 0, 0, 0)),
                  pl.BlockSpec((1, 1, vt.shape[2], mtok), lambda i: (layer, i // qt, 0, 0)),
                  pl.BlockSpec((1, mw), lambda i: (0, 0))],
        out_specs=pl.BlockSpec((tq, mw), lambda i: (i, 0)),
        out_shape=jax.ShapeDtypeStruct((n, mw), BF16),
        scratch_shapes=[pltpu.VMEM((heads, mtok, tq), F32)],
        compiler_params=_cparams(("parallel",), 32),
        name="mem_attn",
    )(p, p, kh, vt, qgain)


def _conv_kernel(a_ref, b_ref, g_ref, w_ref, cb_ref, ng_ref, o_ref, hbuf_ref, shift_ref):
    t = pl.program_id(1)
    ts = a_ref.shape[0]
    span = ts + HALO

    @pl.when(t == 0)
    def _():
        hbuf_ref[0:HALO, :] = jnp.zeros((HALO, hbuf_ref.shape[1]), F32)

    hbuf_ref[HALO:HALO + ts, :] = a_ref[...] * jax.nn.sigmoid(b_ref[...])
    for ph in range(SUBLANES):
        shift_ref[ph, 0:span - ph, :] = hbuf_ref[ph:span, :]
    first = HALO - (C_KERNEL - 1)
    y = None
    for j in range(C_KERNEL):
        ph, base = (first + j) % SUBLANES, (first + j) // SUBLANES * SUBLANES
        term = shift_ref[ph, base:base + ts, :] * w_ref[j:j + 1, :]
        y = term if y is None else y + term
    y += cb_ref[...]
    ms = jnp.mean(y * y, axis=-1, keepdims=True)
    yn = y * lax.rsqrt(ms + EPS) * ng_ref[...]
    o_ref[...] = (jax.nn.silu(yn) * jax.nn.silu(g_ref[...])).astype(BF16)
    hbuf_ref[0:HALO, :] = hbuf_ref[ts:ts + HALO, :]


def _conv_module(p1, conv_w, conv_b, norm_g, batch, seq, ts):
    n = p1.shape[0]
    width = conv_w.shape[1]
    st = seq // ts
    return pl.pallas_call(
        _conv_kernel,
        grid=(batch, st),
        in_specs=[pl.BlockSpec((ts, width), lambda b, t: (b * st + t, 0)),
                  pl.BlockSpec((ts, width), lambda b, t: (b * st + t, 1)),
                  pl.BlockSpec((ts, width), lambda b, t: (b * st + t, 2)),
                  pl.BlockSpec(conv_w.shape, lambda b, t: (0, 0)),
                  pl.BlockSpec((1, width), lambda b, t: (0, 0)),
                  pl.BlockSpec((1, width), lambda b, t: (0, 0))],
        out_specs=pl.BlockSpec((ts, width), lambda b, t: (b * st + t, 0)),
        out_shape=jax.ShapeDtypeStruct((n, width), BF16),
        scratch_shapes=[pltpu.VMEM((ts + HALO, width), F32),
                        pltpu.VMEM((SUBLANES, ts + HALO, width), F32)],
        compiler_params=_cparams(("arbitrary", "arbitrary"), 32),
        name="conv_module",
    )(p1, p1, p1, conv_w, conv_b.reshape(1, width), norm_g.reshape(1, width))


def _proj_nsa_prep_kernel(x_ref, ng_ref, w_ref, wd_ref, c_ref, s1_ref, s2_ref, qg_ref, kg_ref,
                          p_ref, qo_ref, ko_ref, vto_ref, gto_ref):
    x = x_ref[...]
    ms = jnp.mean(x * x, axis=-1, keepdims=True)
    h = (x * lax.rsqrt(ms + EPS) * ng_ref[...]).astype(BF16)
    d = jnp.dot(h, wd_ref[...], preferred_element_type=F32)
    project = functools.partial(_project_chunk, h, w_ref, p_ref, 3)
    qw = qg_ref.shape[1]
    half = HEAD_DIM // 4 // 2
    tabs = (c_ref[...], s1_ref[...], s2_ref[...])
    project(0)
    t = d[:, 0:qw]
    tn = t * lax.rsqrt(_group_mean(t * t, HEAD_DIM) + EPS) * qg_ref[...]
    qo_ref[...] = (_rope(tn, *tabs, half) * (HEAD_DIM ** -0.5 * LOG2E)).astype(BF16)
    ones = jnp.ones((ONES_ROWS, x.shape[0]), F32)
    lane = lax.broadcasted_iota(jnp.int32, (1, LANES), 1)
    for br in range(2):
        project(1 + br)
        k = d[:, qw + 2 * br * LANES:qw + (2 * br + 1) * LANES]
        kn = k * lax.rsqrt(_group_mean(k * k, HEAD_DIM) + EPS) * kg_ref[...]
        kr = _rope(kn, *tabs, half)
        swapped = pltpu.roll(kr, HEAD_DIM, 1)
        for g in range(D_KV_GROUPS):
            own = (lane // HEAD_DIM) == g
            ko_ref[br, :, g * LANES:(g + 1) * LANES] = jnp.where(own, kr, swapped).astype(BF16)
        vt = d[:, qw + (2 * br + 1) * LANES:qw + (2 * br + 2) * LANES].T
        parts = []
        for g in range(D_KV_GROUPS):
            parts += [vt[g * HEAD_DIM:(g + 1) * HEAD_DIM, :], ones]
        vto_ref[0, br, 0] = jnp.concatenate(parts, axis=0).astype(BF16)
    gto_ref[0, 0] = jax.nn.sigmoid(d[:, qw + 4 * LANES:qw + 5 * LANES]).T


def _proj_nsa_prep(x, norm_g, w_rest, w_d, tables, qgain, kgain, batch, seq, tm):
    n, dm = x.shape
    tiles_per_seq = seq // tm
    tab_spec = pl.BlockSpec((tm, LANES), lambda i: (i % tiles_per_seq, 0))
    vrows = D_KV_GROUPS * (HEAD_DIM + ONES_ROWS)

    def full(a):
        return pl.BlockSpec(a.shape, lambda i: (0,) * a.ndim)

    return pl.pallas_call(
        _proj_nsa_prep_kernel,
        grid=(n // tm,),
        in_specs=[pl.BlockSpec((tm, dm), lambda i: (i, 0)), pl.BlockSpec((1, dm), lambda i: (0, 0)),
                  full(w_rest), full(w_d), tab_spec, tab_spec, tab_spec, full(qgain), full(kgain)],
        out_specs=[pl.BlockSpec((tm, w_rest.shape[1]), lambda i: (i, 0)),
                   pl.BlockSpec((tm, qgain.shape[1]), lambda i: (i, 0)),
                   pl.BlockSpec((2, tm, D_KV_GROUPS * LANES), lambda i: (0, i, 0)),
                   pl.BlockSpec((1, 2, 1, vrows, tm),
                                lambda i: (i // tiles_per_seq, 0, i % tiles_per_seq, 0, 0)),
                   pl.BlockSpec((1, 1, LANES, tm),
                                lambda i: (i // tiles_per_seq, i % tiles_per_seq, 0, 0))],
        out_shape=[jax.ShapeDtypeStruct((n, w_rest.shape[1]), F32),
                   jax.ShapeDtypeStruct((n, qgain.shape[1]), BF16),
                   jax.ShapeDtypeStruct((2, n, D_KV_GROUPS * LANES), BF16),
                   jax.ShapeDtypeStruct((batch, 2, tiles_per_seq, vrows, tm), BF16),
                   jax.ShapeDtypeStruct((batch, tiles_per_seq, LANES, tm), F32)],
        compiler_params=_cparams(("parallel",), 48),
        name="proj_nsa_prep",
    )(x, norm_g.reshape(1, dm), w_rest, w_d, *tables, qgain, kgain)


def _compress_kernel(tk_ref, tv_ref, pk_ref, pv_ref, w1k_ref, w2k_ref, w1v_ref, w2v_ref,
                     kg_ref, c_ref, s1_ref, s2_ref, kc_ref, vct_ref):
    half = HEAD_DIM // 4 // 2
    ncp = kc_ref.shape[2]

    def mlp(t_ref, p_ref, w1_ref, w2):
        a = jnp.zeros((ncp, w1_ref.shape[3]), F32)
        b = jnp.zeros((ncp, w1_ref.shape[3]), F32)
        for l in range(CMP_STRIDE):
            x = t_ref[pl.ds(l, ncp, stride=CMP_STRIDE), :]
            a += jnp.dot((x + p_ref[l:l + 1, :]).astype(BF16), w1_ref[0, l],
                         preferred_element_type=F32)
            b += jnp.dot((x + p_ref[CMP_STRIDE + l:CMP_STRIDE + l + 1, :]).astype(BF16),
                         w1_ref[0, CMP_STRIDE + l], preferred_element_type=F32)
        h = a + pltpu.roll(b, ncp - 1, 0)
        return jnp.dot(jax.nn.silu(h).astype(BF16), w2, preferred_element_type=F32)

    kc = mlp(tk_ref, pk_ref, w1k_ref, w2k_ref[...])
    kn = kc * lax.rsqrt(_group_mean(kc * kc, HEAD_DIM) + EPS) * kg_ref[...]
    kc_ref[0, 0] = _rope(kn, c_ref[...], s1_ref[...], s2_ref[...], half).astype(BF16)
    vc = mlp(tv_ref, pv_ref, w1v_ref, w2v_ref[...])
    vct_ref[0, 0] = vc.T[0:HEAD_DIM, :].astype(BF16)


def _compress(p1, k_blk, pk, pv, w1k, w2k, w1v, w2v, kgain, tables, batch, seq):
    groups = w1k.shape[0]
    ncp = seq // CMP_STRIDE

    def full(a):
        return pl.BlockSpec(a.shape, lambda b, g: (0,) * a.ndim)

    def per_group(a):
        return pl.BlockSpec((1,) + a.shape[1:], lambda b, g: (g,) + (0,) * (a.ndim - 1))

    return pl.pallas_call(
        _compress_kernel,
        grid=(batch, groups),
        in_specs=[pl.BlockSpec((seq, LANES), lambda b, g: (b, k_blk)),
                  pl.BlockSpec((seq, LANES), lambda b, g: (b, k_blk + 1)),
                  full(pk), full(pv), per_group(w1k), full(w2k), per_group(w1v), full(w2v),
                  full(kgain), full(tables[0]), full(tables[1]), full(tables[2])],
        out_specs=[pl.BlockSpec((1, 1, ncp, LANES), lambda b, g: (b, g, 0, 0)),
                   pl.BlockSpec((1, 1, HEAD_DIM, ncp), lambda b, g: (b, g, 0, 0))],
        out_shape=[jax.ShapeDtypeStruct((batch, groups, ncp, LANES), BF16),
                   jax.ShapeDtypeStruct((batch, groups, HEAD_DIM, ncp), BF16)],
        compiler_params=_cparams(("parallel", "parallel"), 32),
        name="nsa_compress",
    )(p1, p1, pk, pv, w1k, w2k, w1v, w2v, kgain, *tables)


def _nsa_kernel(qi_tab, tile_tab, br_tab, q_ref, kc_ref, vct_ref, k_ref, vt_ref, gt_ref,
                dg_ref, ovt_ref, o_ref, acc_ref, oc_ref, sel_ref, imp_ref, sta_ref, stb_ref, cst_ref,
                cmax_ref, m_ref, qh_ref, *, tq, qt, nslc, nsel, nentries):
    g = pl.program_id(1)
    rheads = D_HEADS // D_KV_GROUPS
    seq = qt * tq
    tiny = float(np.finfo(np.float32).tiny)
    qcol = lax.broadcasted_iota(jnp.int32, (tq, tq), 1)
    krow = lax.broadcasted_iota(jnp.int32, (tq, tq), 0)
    bufs = (sta_ref, stb_ref)
    blocks_per_tile = tq // SLC_BLOCK
    nslab = -(-nslc // SUBLANES)
    sel_rows = nslab * SUBLANES

    def rows(tile):
        return pl.ds(pl.multiple_of(tile * tq, tq), tq)

    half_lane = lax.broadcasted_iota(jnp.int32, (1, LANES), 1) // HEAD_DIM

    def split_heads(t, c):
        for r in range(rheads):
            blk = q_ref[rows(t), (r // 2) * LANES:(r // 2 + 1) * LANES]
            qh_ref[rows(t), r * LANES:(r + 1) * LANES] = jnp.where(half_lane == r % 2, blk,
                                                                  jnp.zeros_like(blk))
        return c

    lax.fori_loop(0, qt, split_heads, 0)

    def heads_q(qi):
        return [qh_ref[rows(qi), r * LANES:(r + 1) * LANES] for r in range(rheads)]

    kc = kc_ref[0, 0]
    vct = vct_ref[0, 0]
    ncp = kc.shape[0]
    ovt = ovt_ref[...]

    def select_blocks(qi, c):
        q = heads_q(qi)
        cend = lax.broadcasted_iota(jnp.int32, (ncp, tq), 0) * CMP_STRIDE + (CMP_BLOCK - 1)
        cvalid = cend <= qi * tq + lax.broadcasted_iota(jnp.int32, (ncp, tq), 1)
        psum = jnp.zeros((ncp, tq), F32)
        cmax = [_scores_t(kc, q[r], cvalid, cst_ref, r) for r in range(rheads)]
        for r in range(rheads):
            m = jnp.where(jnp.isfinite(cmax[r]), cmax[r], 0.0)
            e = jnp.exp2(cst_ref[r] - m)
            p = e / jnp.maximum(jnp.sum(e, axis=0, keepdims=True), tiny)
            psum += p
            oc_ref[qi * rheads + r] = jnp.dot(vct, p.astype(BF16), preferred_element_type=F32)
        hi, lo = _split_bf16(psum)
        imp = (jnp.dot(ovt, hi, preferred_element_type=F32)
               + jnp.dot(ovt, lo, preferred_element_type=F32))
        blk = lax.broadcasted_iota(jnp.int32, (NSLC_PAD, tq), 0)
        cur = (qi * tq + lax.broadcasted_iota(jnp.int32, (NSLC_PAD, tq), 1)) // SLC_BLOCK
        imp = jnp.where(blk > cur, -jnp.inf, imp)
        forced = (blk == 0) | (blk == cur) | (blk == cur - 1)
        imp = jnp.where(forced, jnp.inf, imp)
        imp_ref[...] = imp
        slabs = [imp[s * SUBLANES:(s + 1) * SUBLANES, :] for s in range(nslab)]
        ranks = [jnp.zeros((SUBLANES, tq), F32) for _ in range(nslab)]
        sub = lax.broadcasted_iota(jnp.int32, (SUBLANES, tq), 0)
        for jp in range(nslc):
            row = imp_ref[jp:jp + 1, :]
            for s in range(nslab):
                if s * SUBLANES > jp:
                    ahead = jnp.where(row >= slabs[s], 1.0, 0.0)
                elif (s + 1) * SUBLANES - 1 < jp:
                    ahead = jnp.where(row > slabs[s], 1.0, 0.0)
                else:
                    ahead = jnp.where(sub > jp - s * SUBLANES, jnp.where(row >= slabs[s], 1.0, 0.0),
                                      jnp.where(row > slabs[s], 1.0, 0.0))
                ranks[s] = ranks[s] + ahead
        for s in range(nslab):
            start = pl.multiple_of(qi * sel_rows + s * SUBLANES, SUBLANES)
            sel_ref[pl.ds(start, SUBLANES), :] = jnp.where(ranks[s] < nsel, 1.0, 0.0)
        return c

    lax.fori_loop(0, qt, select_blocks, 0)
    acc_ref[...] = jnp.zeros_like(acc_ref)
    m_ref[...] = jnp.full(m_ref.shape, -jnp.inf, F32)

    def score_fn(e, buf):
        qi, tile, br = qi_tab[e], tile_tab[e], br_tab[e]
        q = heads_q(qi)
        k = k_ref[br, rows(tile), :]
        base = qi * sel_rows + tile * blocks_per_tile
        chosen = jnp.concatenate(
            [jnp.broadcast_to(sel_ref[pl.ds(base + c, 1), :], (SLC_BLOCK, tq))
             for c in range(blocks_per_tile)], axis=0)
        dpos = (qi - tile) * tq + (qcol - krow)
        reach = jnp.where(br == 0, seq, WINDOW).astype(jnp.uint32)
        kept = jnp.where(jnp.maximum(chosen, br.astype(F32)) > 0.5, dpos, -1)
        valid = kept.astype(jnp.uint32) < reach
        for r in range(rheads):
            row = buf * rheads + r
            cmax_ref[row:row + 1, :] = _scores_t(k, q[r], valid, bufs[buf], r)

    def accum_fn(e, buf):
        qi, tile, br = qi_tab[e], tile_tab[e], br_tab[e]
        vt = vt_ref[0, br, tile]
        for r in range(rheads):
            cm, sr = buf * rheads + r, (qi * 2 + br) * rheads + r
            m_ref[pl.ds(sr, 1), :] = _accumulate_t(bufs[buf], r, cmax_ref[cm:cm + 1, :],
                                                   m_ref[pl.ds(sr, 1), :], acc_ref, sr, vt, True)

    _pipelined_sweep(nentries, score_fn, accum_fn, unroll=8)

    def finish(qi, c):
        outs = []
        for r in range(rheads):
            row0 = (g * rheads + r) * 3
            o = [oc_ref[qi * rheads + r]]
            for br in range(2):
                sr = (qi * 2 + br) * rheads + r
                o.append(acc_ref[sr, 0:HEAD_DIM, :]
                         / jnp.maximum(acc_ref[sr, HEAD_DIM:HEAD_DIM + 1, :], tiny))
            outs.append(sum(gt_ref[0, qi, pl.ds(row0 + j, 1), :] * o[j] for j in range(3)))
        out = jnp.concatenate(outs, axis=0).T
        o_ref[rows(qi), :] = (out * jax.nn.silu(dg_ref[rows(qi), :])).astype(BF16)
        return c

    lax.fori_loop(0, qt, finish, 0)


def _nsa_attn(qe, kc, vct, kboth, vtboth, gt, p1, ovt, batch, seq, tq, dg_blk):
    n = qe.shape[0]
    groups = D_KV_GROUPS
    rheads = D_HEADS // groups
    width = rheads * HEAD_DIM
    qt = seq // tq
    ncp = kc.shape[2]
    nslc = seq // SLC_BLOCK
    sel_rows = -(-nslc // SUBLANES) * SUBLANES
    wtiles = (WINDOW + tq - 1) // tq
    entries = []
    for qi in range(qt):
        entries += [(qi, t, 0) for t in range(qi + 1)]
        entries += [(qi, t, 1) for t in range(max(qi - wtiles, 0), qi + 1)]
    tabs = [jnp.asarray([e[c] for e in entries], jnp.int32) for c in range(3)]
    kernel = functools.partial(_nsa_kernel, tq=tq, qt=qt, nslc=nslc, nsel=min(SLC_TOPK, nslc),
                               nentries=len(entries))
    return pl.pallas_call(
        kernel,
        grid_spec=pltpu.PrefetchScalarGridSpec(
            num_scalar_prefetch=3,
            grid=(batch, groups),
            in_specs=[pl.BlockSpec((seq, width), lambda b, g, *_: (b, g)),
                      pl.BlockSpec((1, 1, ncp, LANES), lambda b, g, *_: (b, g, 0, 0)),
                      pl.BlockSpec((1, 1, HEAD_DIM, ncp), lambda b, g, *_: (b, g, 0, 0)),
                      pl.BlockSpec((2, seq, LANES), lambda b, g, *_: (0, b, g)),
                      pl.BlockSpec((1, 2, qt, HEAD_DIM + ONES_ROWS, tq), lambda b, g, *_: (b, 0, 0, g, 0)),
                      pl.BlockSpec((1, qt, LANES, tq), lambda b, g, *_: (b, 0, 0, 0)),
                      pl.BlockSpec((seq, width), lambda b, g, *_: (b, dg_blk + g)),
                      pl.BlockSpec(ovt.shape, lambda b, g, *_: (0, 0))],
            out_specs=pl.BlockSpec((seq, width), lambda b, g, *_: (b, g)),
            scratch_shapes=[pltpu.VMEM((qt * 2 * rheads, HEAD_DIM + ONES_ROWS, tq), F32),
                            pltpu.VMEM((qt * rheads, HEAD_DIM, tq), F32),
                            pltpu.VMEM((qt * sel_rows, tq), F32),
                            pltpu.VMEM((NSLC_PAD, tq), F32),
                            pltpu.VMEM((rheads, tq, tq), F32),
                            pltpu.VMEM((rheads, tq, tq), F32),
                            pltpu.VMEM((rheads, ncp, tq), F32),
                            pltpu.VMEM((2 * rheads, tq), F32),
                            pltpu.VMEM((qt * 2 * rheads, tq), F32),
                            pltpu.VMEM((seq, rheads * LANES), BF16)]),
        out_shape=jax.ShapeDtypeStruct((n, groups * width), BF16),
        compiler_params=_cparams(("parallel", "parallel"), 56),
        name="nsa_attn",
    )(*tabs, qe, kc, vct, kboth, vtboth, gt, p1, ovt)


def _rope_tables(pos, d):
    rd = d // 4
    half = rd // 2
    inv = ROPE_THETA ** (-jnp.arange(half, dtype=F32) / half)
    ang = pos.astype(F32)[:, None] * inv[None, :]
    cos, sin = jnp.cos(ang), jnp.sin(ang)
    npos = pos.shape[0]
    zeros = jnp.zeros((npos, d), F32)
    c = jnp.concatenate([cos, cos, jnp.ones((npos, d - rd), F32)], axis=1)
    s1 = zeros.at[:, :half].set(-sin)
    s2 = zeros.at[:, half:rd].set(sin)
    return tuple(jnp.tile(t, (1, LANES // d)) for t in (c, s1, s2))


def _overlap_t(ncp, nslc):
    start = np.arange(ncp) * CMP_STRIDE
    s0 = np.arange(nslc) * SLC_BLOCK
    lo = np.maximum(start[:, None], s0[None, :])
    hi = np.minimum(start[:, None] + CMP_BLOCK, s0[None, :] + SLC_BLOCK)
    ov = np.clip(hi - lo, 0, None) / CMP_BLOCK
    ov[ncp - 1] = 0.0
    out = np.zeros((NSLC_PAD, ncp), np.float32)
    out[:nslc] = ov.T
    return jnp.asarray(out, BF16)


def kernel(x, mem, mem_norm, l0_norm, l0_w_in, l0_a_vnorm, l0_a_ws, l0_a_bs, l0_b_qnorm, l0_b_knorm, l0_b_lq1, l0_b_lk1, l0_b_lq2, l0_b_lk2, l0_b_subln, l0_m_wkv, l0_m_qnorm, l0_m_knorm, l0_w_out, l1_norm, l1_w_in, l1_c_conv_w, l1_c_conv_b, l1_c_norm, l1_d_qnorm, l1_d_knorm, l1_d_cmp_pos_k, l1_d_cmp_w1_k, l1_d_cmp_w2_k, l1_d_cmp_pos_v, l1_d_cmp_w1_v, l1_d_cmp_w2_v, l1_m_wkv, l1_m_qnorm, l1_m_knorm, l1_w_out):
    batch, seq, d_model = x.shape
    n = batch * seq
    tm = 256
    tq = 256
    mw = M_HEADS * HEAD_DIM
    pos = jnp.arange(seq, dtype=jnp.int32)
    x0 = x.reshape(n, d_model)

    wkv_both = jnp.concatenate([l0_m_wkv, l1_m_wkv], axis=1).astype(BF16)
    kgains = jnp.stack([jnp.tile(l0_m_knorm, M_HEADS), jnp.tile(l1_m_knorm, M_HEADS)])
    mem_kh, mem_vt = _mem_kv(mem, mem_norm, wkv_both, kgains)

    w = l0_w_in
    tab32 = _rope_tables(pos, B_QK_DIM)
    b_width = 2 * B_HEADS * B_QK_DIM
    p0, qn, kn, vt = _proj_diff_prep(
        x0, l0_norm, jnp.concatenate([w[:, 0:1536], w[:, 3072:4096]], axis=1).astype(BF16),
        w[:, 1536:3072].astype(BF16), tab32,
        jnp.tile(l0_b_qnorm, b_width // B_QK_DIM).reshape(1, b_width),
        jnp.tile(l0_b_knorm, b_width // B_QK_DIM).reshape(1, b_width), batch, seq, tq)
    a_width = l0_a_vnorm.shape[0]
    bs_exp = jnp.repeat(l0_a_bs.T, a_width // A_GROUPS, axis=1)
    y_a = _sgu(p0, l0_a_ws, bs_exp, l0_a_vnorm.reshape(1, a_width), 512)
    lam_init = 0.8 - 0.6 * math.exp(-0.3 * 1)
    lam_p = jnp.stack([l0_b_lq1, l0_b_lk1, l0_b_lq2, l0_b_lk2])
    y_b = _diff_attn(lam_p, qn, kn, vt, p0, jnp.tile(l0_b_subln, 2).reshape(1, LANES),
                     batch, seq, tq, lam_init, 1536 // LANES)
    y_m = _mem_attn(p0, mem_kh, mem_vt, 0, jnp.tile(l0_m_qnorm, M_HEADS).reshape(1, mw),
                    2048 // mw, 2304 // mw, seq, 2 * tq)
    x1 = _out_proj(x0, y_a, y_b, y_m, l0_w_out.astype(BF16), tm)

    w = l1_w_in
    n_bg = 3 * D_HEADS
    w_rest = jnp.concatenate([w[:, 0:1536], w[:, 2840:3352], w[:, 2048:2304], w[:, 3352:3864]], axis=1)
    w_d = jnp.concatenate([w[:, 1536:2048], w[:, 2304:2816], w[:, 2816:2840],
                           jnp.zeros((d_model, LANES - n_bg), F32)], axis=1)
    groups, rheads = D_KV_GROUPS, D_HEADS // D_KV_GROUPS
    d_width = D_HEADS * HEAD_DIM
    tab64 = _rope_tables(pos, HEAD_DIM)
    p1, qe, kboth, vtboth, gt = _proj_nsa_prep(
        x1, l1_norm, w_rest.astype(BF16), w_d.astype(BF16), tab64,
        jnp.tile(l1_d_qnorm, D_HEADS).reshape(1, d_width),
        jnp.tile(l1_d_knorm, LANES // HEAD_DIM).reshape(1, LANES), batch, seq, tq)
    y_c = _conv_module(p1, l1_c_conv_w, l1_c_conv_b, l1_c_norm, batch, seq, 512)
    kv_blk = 2048 // LANES

    ncp = seq // CMP_STRIDE
    cmp_pos = jnp.arange(ncp, dtype=jnp.int32) * CMP_STRIDE + (CMP_BLOCK - 1)
    hidden = l1_d_cmp_w2_k.shape[0]

    def group_w1(w1):
        w = w1.reshape(CMP_BLOCK, HEAD_DIM, hidden)
        return jnp.stack([jnp.zeros((CMP_BLOCK, LANES, hidden), F32)
                          .at[:, g * HEAD_DIM:(g + 1) * HEAD_DIM, :].set(w) for g in range(groups)])

    w2k = jnp.tile(l1_d_cmp_w2_k, (1, LANES // HEAD_DIM))
    w2v = jnp.zeros((hidden, LANES), F32).at[:, :HEAD_DIM].set(l1_d_cmp_w2_v)
    kc, vct = _compress(
        p1, kv_blk, jnp.tile(l1_d_cmp_pos_k, (1, groups)), jnp.tile(l1_d_cmp_pos_v, (1, groups)),
        group_w1(l1_d_cmp_w1_k).astype(BF16), w2k.astype(BF16),
        group_w1(l1_d_cmp_w1_v).astype(BF16), w2v.astype(BF16),
        jnp.tile(l1_d_knorm, LANES // HEAD_DIM).reshape(1, LANES), _rope_tables(cmp_pos, HEAD_DIM),
        batch, seq)

    nslc = seq // SLC_BLOCK
    y_d = _nsa_attn(qe, kc, vct, kboth, vtboth, gt, p1, _overlap_t(ncp, nslc),
                    batch, seq, tq, 1536 // (rheads * HEAD_DIM))
    y_m1 = _mem_attn(p1, mem_kh, mem_vt, 1, jnp.tile(l1_m_qnorm, M_HEADS).reshape(1, mw),
                     2304 // mw, 2560 // mw, seq, 2 * tq)
    out = _out_proj(x1, y_c, y_d, y_m1, l1_w_out.astype(BF16), tm)
    return out.reshape(batch, seq, d_model)
```

```python
import functools
import math

import numpy as np
import jax
import jax.numpy as jnp
from jax import lax
from jax.experimental import pallas as pl
from jax.experimental.pallas import tpu as pltpu

F32 = jnp.float32
BF16 = jnp.bfloat16

EPS = 1e-6
ROPE_THETA = 500000.0
HEAD_DIM = 64
A_CHUNK = 128
A_GROUPS = 4
B_HEADS = 8
B_QK_DIM = 32
C_KERNEL = 31
D_HEADS = 8
D_KV_GROUPS = 2
CMP_BLOCK = 32
CMP_STRIDE = 16
SLC_BLOCK = 64
SLC_TOPK = 16
WINDOW = 512
M_HEADS = 4

LANES = 128
SUBLANES = 8
MXU_DIM = 256
HALO = 32
NSLC_PAD = 128
ONES_ROWS = 16

NT_DIMS = (((1,), (1,)), ((), ()))
LOG2E = math.log2(math.e)


def _cparams(semantics, vmem_mb):
    return pltpu.CompilerParams(dimension_semantics=semantics,
                                vmem_limit_bytes=vmem_mb * 1024 * 1024)


def _split_bf16(x):
    hi = x.astype(BF16)
    lo = (x - hi.astype(F32)).astype(BF16)
    return hi, lo


def _group_mean(x2, gsize):
    t, c = x2.shape
    w = min(c, MXU_DIM)
    r = lax.broadcasted_iota(jnp.int32, (w, w), 0) // gsize
    cc = lax.broadcasted_iota(jnp.int32, (w, w), 1) // gsize
    ones = jnp.where(r == cc, 1.0, 0.0).astype(BF16)
    outs = []
    for s in range(c // w):
        hi, lo = _split_bf16(x2[:, s * w:(s + 1) * w])
        outs.append(jnp.dot(hi, ones, preferred_element_type=F32)
                    + jnp.dot(lo, ones, preferred_element_type=F32))
    out = outs[0] if len(outs) == 1 else jnp.concatenate(outs, axis=1)
    return out * (1.0 / gsize)


def _tile_lanes(t, width):
    rep = width // t.shape[1]
    return t if rep == 1 else jnp.concatenate([t] * rep, axis=1)


def _rope(x, cos_t, s1_t, s2_t, half):
    width = x.shape[1]
    up = pltpu.roll(x, width - half, 1)
    dn = pltpu.roll(x, half, 1)
    return (x * _tile_lanes(cos_t, width) + up * _tile_lanes(s1_t, width)
            + dn * _tile_lanes(s2_t, width))


def _matmul_cols(h, w_ref, lo, hi, out_ref=None):
    r = jnp.dot(h, w_ref[:, lo:hi], preferred_element_type=F32)
    if out_ref is None:
        return r
    out_ref[:, lo:hi] = r


def _rms_rows(x_ref, ng_ref):
    x = x_ref[...]
    ms = jnp.mean(x * x, axis=-1, keepdims=True)
    return (x * lax.rsqrt(ms + EPS) * ng_ref[...]).astype(BF16)


def _scores_t(k, q, valid, st_ref, idx):
    st = lax.dot_general(k, q, NT_DIMS, preferred_element_type=F32)
    if valid is not None:
        st = jnp.where(valid, st, -jnp.inf)
    st_ref[idx] = st
    return jnp.max(st, axis=0, keepdims=True)


def _accumulate_t(st_ref, idx, cmax, m, acc_ref, aidx, vt, guard_empty):
    m_new = jnp.maximum(m, cmax)
    m_use = jnp.where(m_new == -jnp.inf, 0.0, m_new) if guard_empty else m_new
    alpha = jnp.exp2(m - m_use)
    pt = jnp.exp2((st_ref[idx] - m_use).astype(BF16))
    acc_ref[aidx] = alpha * acc_ref[aidx] + jnp.dot(vt, pt, preferred_element_type=F32)
    return m_new


def _pipelined_sweep(n, score_fn, accum_fn, unroll=4):
    score_fn(0, 0)
    stages = n - 1

    def stage(e, parity):
        score_fn(e + 1, 1 - parity)
        accum_fn(e, parity)

    def body(i, c):
        for k in range(unroll):
            stage(unroll * i + k, k % 2)
        return c

    if isinstance(n, int):
        iters, left = divmod(stages, unroll)
        if iters:
            lax.fori_loop(0, iters, body, 0)
        for k in range(left):
            stage(iters * unroll + k, k % 2)
        accum_fn(n - 1, left % 2)
        return
    lax.fori_loop(0, stages // unroll, body, 0)
    base = stages // unroll * unroll
    left = stages - base
    for k in range(unroll - 1):
        pl.when(left > k)(functools.partial(stage, base + k, k % 2))
    for parity in range(2):
        pl.when(left % 2 == parity)(functools.partial(accum_fn, n - 1, parity))


def _out_proj_kernel(x_ref, ya_ref, yb_ref, ym_ref, w_ref, o_ref):
    wa = ya_ref.shape[1]
    wb = yb_ref.shape[1]
    acc = jnp.dot(ya_ref[...], w_ref[0:wa, :], preferred_element_type=F32)
    acc += jnp.dot(yb_ref[...], w_ref[wa:wa + wb, :], preferred_element_type=F32)
    acc += jnp.dot(ym_ref[...], w_ref[wa + wb:, :], preferred_element_type=F32)
    o_ref[...] = x_ref[...] + acc


def _out_proj(x, ya, yb, ym, w, tm):
    n, d = x.shape
    return pl.pallas_call(
        _out_proj_kernel,
        grid=(n // tm,),
        in_specs=[pl.BlockSpec((tm, d), lambda i: (i, 0)),
                  pl.BlockSpec((tm, ya.shape[1]), lambda i: (i, 0)),
                  pl.BlockSpec((tm, yb.shape[1]), lambda i: (i, 0)),
                  pl.BlockSpec((tm, ym.shape[1]), lambda i: (i, 0)),
                  pl.BlockSpec(w.shape, lambda i: (0, 0))],
        out_specs=pl.BlockSpec((tm, d), lambda i: (i, 0)),
        out_shape=jax.ShapeDtypeStruct((n, d), F32),
        compiler_params=_cparams(("parallel",), 32),
        name="out_proj",
    )(x, ya, yb, ym, w)


def _sgu_group(u, v, gate, w_ref, bs, vgain, o_ref, cols):
    tm = u.shape[0]
    row = lax.broadcasted_iota(jnp.int32, (A_CHUNK, A_CHUNK), 0)
    col = lax.broadcasted_iota(jnp.int32, (A_CHUNK, A_CHUNK), 1)
    w = jnp.where(col <= row, w_ref, 0.0).astype(BF16)
    gv = jax.nn.gelu(v)
    ms = jnp.mean(gv * gv, axis=-1, keepdims=True)
    vn = (gv * lax.rsqrt(ms + EPS) * vgain).astype(BF16)
    gu = jax.nn.gelu(u) * jax.nn.silu(gate)
    for c in range(tm // A_CHUNK):
        rows = slice(c * A_CHUNK, (c + 1) * A_CHUNK)
        z = jnp.dot(w, vn[rows, :], preferred_element_type=F32) + bs
        o_ref[rows, cols] = (gu[rows, :] * z).astype(BF16)


def _layer0_proj_kernel(x_ref, ng_ref, wa_ref, wqkv_ref, w_ref, c_ref, s1_ref, s2_ref, qg_ref, kg_ref,
                        ws_ref, bs_ref, vg_ref, p_ref, ya_ref, qo_ref, ko_ref, vto_ref):
    h = _rms_rows(x_ref, ng_ref)
    aw = vg_ref.shape[1]
    gdim = aw // A_GROUPS
    width = qg_ref.shape[1]
    half = B_QK_DIM // 4 // 2
    pw = p_ref.shape[1]

    a = _matmul_cols(h, wa_ref, 0, 3 * aw)

    def sgu(g):
        cols = slice(g * gdim, (g + 1) * gdim)
        _sgu_group(a[:, cols], a[:, aw + g * gdim:aw + (g + 1) * gdim],
                   a[:, 2 * aw + g * gdim:2 * aw + (g + 1) * gdim],
                   ws_ref[g], bs_ref[:, cols], vg_ref[:, cols], ya_ref, cols)

    def prep(t, gain, dst, mul):
        tn = t * lax.rsqrt(_group_mean(t * t, B_QK_DIM) + EPS) * gain[...]
        tr = _rope(tn, c_ref[...], s1_ref[...], s2_ref[...], half)
        dst[...] = (tr * mul if mul != 1.0 else tr).astype(BF16)

    q = _matmul_cols(h, wqkv_ref, 0, width)
    sgu(0)
    k = _matmul_cols(h, wqkv_ref, width, 2 * width)
    sgu(1)
    v = _matmul_cols(h, wqkv_ref, 2 * width, 3 * width)
    sgu(2)
    _matmul_cols(h, w_ref, 0, pw // 2, p_ref)
    sgu(3)
    prep(q, qg_ref, qo_ref, B_QK_DIM ** -0.5 * LOG2E)
    _matmul_cols(h, w_ref, pw // 2, pw, p_ref)
    prep(k, kg_ref, ko_ref, 1.0)
    vt = v.T
    dv = 2 * B_QK_DIM
    ones = jnp.ones((ONES_ROWS, vt.shape[1]), F32)
    parts = []
    for hd in range(B_HEADS):
        parts += [vt[hd * dv:(hd + 1) * dv, :], ones]
    vto_ref[0, 0] = jnp.concatenate(parts, axis=0).astype(BF16)


def _layer0_proj(x, norm_g, w_a, w_qkv, w_rest, tables, qgain, kgain, a_ws, bs_exp, vgain, batch, seq, tm):
    n, d = x.shape
    width = qgain.shape[1]
    aw = vgain.shape[1]
    tiles_per_seq = seq // tm
    tab_spec = pl.BlockSpec((tm, LANES), lambda i: (i % tiles_per_seq, 0))
    vrows = B_HEADS * (2 * B_QK_DIM + ONES_ROWS)

    def full(a):
        return pl.BlockSpec(a.shape, lambda i: (0,) * a.ndim)

    def rows(w):
        return pl.BlockSpec((tm, w), lambda i: (i, 0))

    return pl.pallas_call(
        _layer0_proj_kernel,
        grid=(n // tm,),
        in_specs=[rows(d), pl.BlockSpec((1, d), lambda i: (0, 0)), full(w_a), full(w_qkv), full(w_rest),
                  tab_spec, tab_spec, tab_spec, full(qgain), full(kgain),
                  full(a_ws), full(bs_exp), full(vgain)],
        out_specs=[rows(w_rest.shape[1]), rows(aw), rows(width), rows(width),
                   pl.BlockSpec((1, 1, vrows, tm),
                                lambda i: (i // tiles_per_seq, i % tiles_per_seq, 0, 0))],
        out_shape=[jax.ShapeDtypeStruct((n, w_rest.shape[1]), F32),
                   jax.ShapeDtypeStruct((n, aw), BF16),
                   jax.ShapeDtypeStruct((n, width), BF16), jax.ShapeDtypeStruct((n, width), BF16),
                   jax.ShapeDtypeStruct((batch, tiles_per_seq, vrows, tm), BF16)],
        compiler_params=_cparams(("parallel",), 48),
        name="layer0_proj",
    )(x, norm_g.reshape(1, d), w_a, w_qkv, w_rest, *tables, qgain, kgain, a_ws, bs_exp, vgain)


def _diff_attn_kernel(qi_tab, tile_tab, lam_ref, q_ref, k_ref, vt_ref, g_ref, sub_ref, o_ref,
                      acc_ref, sta_ref, stb_ref, cmax_ref, m_ref, *, tq, qt, lam_init):
    dv = 2 * B_QK_DIM
    vrows = dv + ONES_ROWS
    nmap = 4
    lane = lax.broadcasted_iota(jnp.int32, (1, LANES), 1)
    bufs = (sta_ref, stb_ref)
    acc_ref[...] = jnp.zeros_like(acc_ref)
    m_ref[...] = jnp.full(m_ref.shape, -jnp.inf, F32)
    krow = lax.broadcasted_iota(jnp.int32, (tq, tq), 0)
    qcol = lax.broadcasted_iota(jnp.int32, (tq, tq), 1)
    lam_p = lam_ref[...]
    lam = (jnp.exp(jnp.sum(lam_p[0:1] * lam_p[1:2], axis=-1, keepdims=True))
           - jnp.exp(jnp.sum(lam_p[2:3] * lam_p[3:4], axis=-1, keepdims=True)) + lam_init)

    def rows(tile):
        return pl.ds(pl.multiple_of(tile * tq, tq), tq)

    def scores(qi, tile, buf, valid):
        q = q_ref[rows(qi), :]
        k = k_ref[rows(tile), :]
        zero = jnp.zeros_like(q)
        for j in range(nmap):
            qm = jnp.where(lane // B_QK_DIM == j, q, zero)
            row = buf * nmap + j
            cmax_ref[row:row + 1, :] = _scores_t(k, qm, valid, bufs[buf], j)

    def accumulate(qi, tile, buf):
        for j in range(nmap):
            vt = vt_ref[0, tile, (j // 2) * vrows:(j // 2 + 1) * vrows, :]
            cm, sr = buf * nmap + j, qi * nmap + j
            m_ref[pl.ds(sr, 1), :] = _accumulate_t(bufs[buf], j, cmax_ref[cm:cm + 1, :],
                                                   m_ref[pl.ds(sr, 1), :], acc_ref, sr, vt, False)

    def finish(qi):
        halves = []
        for h in range(2):
            o = [acc_ref[qi * nmap + 2 * h + mp, 0:dv, :] / acc_ref[qi * nmap + 2 * h + mp, dv:dv + 1, :]
                 for mp in range(2)]
            pd = o[0] - lam * o[1]
            ms2 = jnp.mean(pd * pd, axis=0, keepdims=True)
            halves.append(pd * lax.rsqrt(ms2 + EPS))
        ob = jnp.concatenate(halves, axis=0).T * sub_ref[...] * (1.0 - lam_init)
        o_ref[rows(qi), :] = (ob * jax.nn.silu(g_ref[rows(qi), :])).astype(BF16)

    def below_score(e, buf):
        scores(qi_tab[e], tile_tab[e], buf, None)

    def below_accum(e, buf):
        accumulate(qi_tab[e], tile_tab[e], buf)

    def diag_score(e, buf):
        scores(e, e, buf, krow <= qcol)

    def diag_accum(e, buf):
        accumulate(e, e, buf)
        finish(e)

    _pipelined_sweep(qt * (qt - 1) // 2, below_score, below_accum, unroll=8)
    _pipelined_sweep(qt, diag_score, diag_accum)


def _diff_attn(lam_p, qn, kn, vt, p0, subln, batch, seq, tq, lam_init, g_off):
    n, width = qn.shape
    pairs = width // LANES
    qt = seq // tq
    vrows = 2 * B_QK_DIM + ONES_ROWS
    below = [(qi, t) for qi in range(qt) for t in range(qi)]
    qi_tab = jnp.asarray([e[0] for e in below], jnp.int32)
    tile_tab = jnp.asarray([e[1] for e in below], jnp.int32)
    kernel = functools.partial(_diff_attn_kernel, tq=tq, qt=qt, lam_init=lam_init)
    seq_spec = pl.BlockSpec((seq, LANES), lambda b, p, *_: (b, p))
    return pl.pallas_call(
        kernel,
        grid_spec=pltpu.PrefetchScalarGridSpec(
            num_scalar_prefetch=2,
            grid=(batch, pairs),
            in_specs=[pl.BlockSpec(lam_p.shape, lambda b, p, *_: (0, 0)),
                      seq_spec, seq_spec,
                      pl.BlockSpec((1, qt, 2 * vrows, tq), lambda b, p, *_: (b, 0, p, 0)),
                      pl.BlockSpec((seq, LANES), lambda b, p, *_: (b, g_off + p)),
                      pl.BlockSpec((1, LANES), lambda b, p, *_: (0, 0))],
            out_specs=seq_spec,
            scratch_shapes=[pltpu.VMEM((qt * 4, vrows, tq), F32), pltpu.VMEM((4, tq, tq), F32),
                            pltpu.VMEM((4, tq, tq), F32), pltpu.VMEM((8, tq), F32),
                            pltpu.VMEM((qt * 4, tq), F32)]),
        out_shape=jax.ShapeDtypeStruct((n, width), BF16),
        compiler_params=_cparams(("parallel", "parallel"), 40),
        name="diff_attn",
    )(qi_tab, tile_tab, lam_p, qn, kn, vt, p0, subln)


def _mem_kv_kernel(mem_ref, g_ref, w_ref, kg_ref, kh_ref, vt_ref):
    x = mem_ref[0]
    ms = jnp.mean(x * x, axis=-1, keepdims=True)
    h = (x * lax.rsqrt(ms + EPS) * g_ref[...]).astype(BF16)
    kv = jnp.dot(h, w_ref[...], preferred_element_type=F32)
    mw = kv.shape[1] // 4
    lane = lax.broadcasted_iota(jnp.int32, (1, mw), 1)
    ones = jnp.ones((ONES_ROWS, kv.shape[0]), F32)
    for layer in range(2):
        k = kv[:, (2 * layer) * mw:(2 * layer + 1) * mw]
        kn = k * lax.rsqrt(_group_mean(k * k, HEAD_DIM) + EPS) * kg_ref[layer:layer + 1, :]
        vt = kv[:, (2 * layer + 1) * mw:(2 * layer + 2) * mw].T
        parts = []
        for hd in range(M_HEADS):
            kh_ref[layer, 0, hd] = jnp.where(lane // HEAD_DIM == hd, kn, 0.0).astype(BF16)
            parts += [vt[hd * HEAD_DIM:(hd + 1) * HEAD_DIM, :], ones]
        vt_ref[layer, 0] = jnp.concatenate(parts, axis=0).astype(BF16)


def _mem_kv(mem, mem_norm, wkv_both, kgains):
    batch, mtok, d = mem.shape
    mw = wkv_both.shape[1] // 4
    vrows = M_HEADS * (HEAD_DIM + ONES_ROWS)
    return pl.pallas_call(
        _mem_kv_kernel,
        grid=(batch,),
        in_specs=[pl.BlockSpec((1, mtok, d), lambda b: (b, 0, 0)),
                  pl.BlockSpec((1, d), lambda b: (0, 0)),
                  pl.BlockSpec(wkv_both.shape, lambda b: (0, 0)),
                  pl.BlockSpec(kgains.shape, lambda b: (0, 0))],
        out_specs=[pl.BlockSpec((2, 1, M_HEADS, mtok, mw), lambda b: (0, b, 0, 0, 0)),
                   pl.BlockSpec((2, 1, vrows, mtok), lambda b: (0, b, 0, 0))],
        out_shape=[jax.ShapeDtypeStruct((2, batch, M_HEADS, mtok, mw), BF16),
                   jax.ShapeDtypeStruct((2, batch, vrows, mtok), BF16)],
        compiler_params=_cparams(("parallel",), 32),
        name="mem_kv",
    )(mem, mem_norm.reshape(1, d), wkv_both, kgains)


def _mem_attn_kernel(q_ref, g_ref, kh_ref, vt_ref, qg_ref, o_ref, st_ref):
    vrows = HEAD_DIM + ONES_ROWS
    x = q_ref[...]
    qn = (x * lax.rsqrt(_group_mean(x * x, HEAD_DIM) + EPS) * qg_ref[...]
          * (HEAD_DIM ** -0.5 * LOG2E)).astype(BF16)
    cmax = [_scores_t(kh_ref[0, 0, h], qn, None, st_ref, h) for h in range(M_HEADS)]
    outs = []
    for h in range(M_HEADS):
        pt = jnp.exp2((st_ref[h] - cmax[h]).astype(BF16))
        acc = jnp.dot(vt_ref[0, 0, h * vrows:(h + 1) * vrows, :], pt, preferred_element_type=F32)
        outs.append(acc[0:HEAD_DIM, :] / acc[HEAD_DIM:HEAD_DIM + 1, :])
    out = jnp.concatenate(outs, axis=0).T
    o_ref[...] = (out * jax.nn.silu(g_ref[...])).astype(BF16)


def _mem_attn(p, kh, vt, layer, qgain, q_blk, g_blk, seq, tq):
    n = p.shape[0]
    _, batch, heads, mtok, mw = kh.shape
    qt = seq // tq
    return pl.pallas_call(
        _mem_attn_kernel,
        grid=(n // tq,),
        in_specs=[pl.BlockSpec((tq, mw), lambda i: (i, q_blk)),
                  pl.BlockSpec((tq, mw), lambda i: (i, g_blk)),
                  pl.BlockSpec((1, 1, heads, mtok, mw), lambda i: (layer, i // qt, 0, 0, 0)),
                  pl.BlockSpec((1, 1, vt.shape[2], mtok), lambda i: (layer, i // qt, 0, 0)),
                  pl.BlockSpec((1, mw), lambda i: (0, 0))],
        out_specs=pl.BlockSpec((tq, mw), lambda i: (i, 0)),
        out_shape=jax.ShapeDtypeStruct((n, mw), BF16),
        scratch_shapes=[pltpu.VMEM((heads, mtok, tq), F32)],
        compiler_params=_cparams(("parallel",), 32),
        name="mem_attn",
    )(p, p, kh, vt, qgain)


def _layer1_proj_kernel(x_ref, ng_ref, wc_ref, wd_ref, w_ref, c_ref, s1_ref, s2_ref, qg_ref, kg_ref,
                        cw_ref, cb_ref, cn_ref, p_ref, yc_ref, qo_ref, ko_ref, vto_ref, gto_ref,
                        hbuf_ref, shift_ref, *, tiles_per_seq):
    i = pl.program_id(0)
    ts = x_ref.shape[0]
    span = ts + HALO
    cwid = cn_ref.shape[1]
    qw = qg_ref.shape[1]
    pw = p_ref.shape[1]
    half = HEAD_DIM // 4 // 2
    h = _rms_rows(x_ref, ng_ref)

    @pl.when(i % tiles_per_seq == 0)
    def _():
        hbuf_ref[0:HALO, :] = jnp.zeros((HALO, cwid), F32)

    ca = _matmul_cols(h, wc_ref, 0, cwid)
    cbv = _matmul_cols(h, wc_ref, cwid, 2 * cwid)
    q = _matmul_cols(h, wd_ref, 0, qw)
    hbuf_ref[HALO:HALO + ts, :] = ca * jax.nn.sigmoid(cbv)
    for ph in range(SUBLANES):
        shift_ref[ph, 0:span - ph, :] = hbuf_ref[ph:span, :]
    cg = _matmul_cols(h, wc_ref, 2 * cwid, 3 * cwid)
    d = _matmul_cols(h, wd_ref, qw, wd_ref.shape[1])
    first = HALO - (C_KERNEL - 1)
    y = cb_ref[...]
    for j in range(C_KERNEL):
        ph, base = (first + j) % SUBLANES, (first + j) // SUBLANES * SUBLANES
        y = y + shift_ref[ph, base:base + ts, :] * cw_ref[j:j + 1, :]
    _matmul_cols(h, w_ref, 0, pw // 2, p_ref)
    ms = jnp.mean(y * y, axis=-1, keepdims=True)
    yn = y * lax.rsqrt(ms + EPS) * cn_ref[...]
    yc_ref[...] = (jax.nn.silu(yn) * jax.nn.silu(cg)).astype(BF16)
    hbuf_ref[0:HALO, :] = hbuf_ref[ts:ts + HALO, :]
    _matmul_cols(h, w_ref, pw // 2, pw, p_ref)

    tabs = (c_ref[...], s1_ref[...], s2_ref[...])
    tn = q * lax.rsqrt(_group_mean(q * q, HEAD_DIM) + EPS) * qg_ref[...]
    qo_ref[...] = (_rope(tn, *tabs, half) * (HEAD_DIM ** -0.5 * LOG2E)).astype(BF16)
    ones = jnp.ones((ONES_ROWS, ts), F32)
    lane = lax.broadcasted_iota(jnp.int32, (1, LANES), 1)
    for br in range(2):
        k = d[:, 2 * br * LANES:(2 * br + 1) * LANES]
        kn = k * lax.rsqrt(_group_mean(k * k, HEAD_DIM) + EPS) * kg_ref[...]
        kr = _rope(kn, *tabs, half)
        swapped = pltpu.roll(kr, HEAD_DIM, 1)
        for g in range(D_KV_GROUPS):
            own = (lane // HEAD_DIM) == g
            ko_ref[br, :, g * LANES:(g + 1) * LANES] = jnp.where(own, kr, swapped).astype(BF16)
        vt = d[:, (2 * br + 1) * LANES:(2 * br + 2) * LANES].T
        parts = []
        for g in range(D_KV_GROUPS):
            parts += [vt[g * HEAD_DIM:(g + 1) * HEAD_DIM, :], ones]
        vto_ref[0, br, 0] = jnp.concatenate(parts, axis=0).astype(BF16)
    gto_ref[0, 0] = jax.nn.sigmoid(d[:, 4 * LANES:5 * LANES]).T


def _layer1_proj(x, norm_g, w_c, w_d, w_rest, tables, qgain, kgain, conv_w, conv_b, conv_norm,
                 batch, seq, tm):
    n, dm = x.shape
    cwid = conv_w.shape[1]
    tiles_per_seq = seq // tm
    tab_spec = pl.BlockSpec((tm, LANES), lambda i: (i % tiles_per_seq, 0))
    vrows = D_KV_GROUPS * (HEAD_DIM + ONES_ROWS)

    def full(a):
        return pl.BlockSpec(a.shape, lambda i: (0,) * a.ndim)

    def rows(w):
        return pl.BlockSpec((tm, w), lambda i: (i, 0))

    kernel = functools.partial(_layer1_proj_kernel, tiles_per_seq=tiles_per_seq)
    return pl.pallas_call(
        kernel,
        grid=(n // tm,),
        in_specs=[rows(dm), pl.BlockSpec((1, dm), lambda i: (0, 0)), full(w_c), full(w_d), full(w_rest),
                  tab_spec, tab_spec, tab_spec, full(qgain), full(kgain),
                  full(conv_w), pl.BlockSpec((1, cwid), lambda i: (0, 0)),
                  pl.BlockSpec((1, cwid), lambda i: (0, 0))],
        out_specs=[rows(w_rest.shape[1]), rows(cwid), rows(qgain.shape[1]),
                   pl.BlockSpec((2, tm, D_KV_GROUPS * LANES), lambda i: (0, i, 0)),
                   pl.BlockSpec((1, 2, 1, vrows, tm),
                                lambda i: (i // tiles_per_seq, 0, i % tiles_per_seq, 0, 0)),
                   pl.BlockSpec((1, 1, LANES, tm),
                                lambda i: (i // tiles_per_seq, i % tiles_per_seq, 0, 0))],
        out_shape=[jax.ShapeDtypeStruct((n, w_rest.shape[1]), F32),
                   jax.ShapeDtypeStruct((n, cwid), BF16),
                   jax.ShapeDtypeStruct((n, qgain.shape[1]), BF16),
                   jax.ShapeDtypeStruct((2, n, D_KV_GROUPS * LANES), BF16),
                   jax.ShapeDtypeStruct((batch, 2, tiles_per_seq, vrows, tm), BF16),
                   jax.ShapeDtypeStruct((batch, tiles_per_seq, LANES, tm), F32)],
        scratch_shapes=[pltpu.VMEM((tm + HALO, cwid), F32),
                        pltpu.VMEM((SUBLANES, tm + HALO, cwid), F32)],
        compiler_params=_cparams(("arbitrary",), 48),
        name="layer1_proj",
    )(x, norm_g.reshape(1, dm), w_c, w_d, w_rest, *tables, qgain, kgain, conv_w,
      conv_b.reshape(1, cwid), conv_norm.reshape(1, cwid))


def _compress_kernel(tk_ref, tv_ref, pk_ref, pv_ref, w1k_ref, w2k_ref, w1v_ref, w2v_ref,
                     kg_ref, c_ref, s1_ref, s2_ref, kc_ref, vct_ref):
    half = HEAD_DIM // 4 // 2
    ncp = kc_ref.shape[2]

    def mlp(t_ref, p_ref, w1_ref, w2):
        a = jnp.zeros((ncp, w1_ref.shape[3]), F32)
        b = jnp.zeros((ncp, w1_ref.shape[3]), F32)
        for l in range(CMP_STRIDE):
            x = t_ref[pl.ds(l, ncp, stride=CMP_STRIDE), :]
            a += jnp.dot((x + p_ref[l:l + 1, :]).astype(BF16), w1_ref[0, l],
                         preferred_element_type=F32)
            b += jnp.dot((x + p_ref[CMP_STRIDE + l:CMP_STRIDE + l + 1, :]).astype(BF16),
                         w1_ref[0, CMP_STRIDE + l], preferred_element_type=F32)
        h = a + pltpu.roll(b, ncp - 1, 0)
        return jnp.dot(jax.nn.silu(h).astype(BF16), w2, preferred_element_type=F32)

    kc = mlp(tk_ref, pk_ref, w1k_ref, w2k_ref[...])
    kn = kc * lax.rsqrt(_group_mean(kc * kc, HEAD_DIM) + EPS) * kg_ref[...]
    kc_ref[0, 0] = _rope(kn, c_ref[...], s1_ref[...], s2_ref[...], half).astype(BF16)
    vc = mlp(tv_ref, pv_ref, w1v_ref, w2v_ref[...])
    vct_ref[0, 0] = vc.T[0:HEAD_DIM, :].astype(BF16)


def _compress(p1, k_blk, pk, pv, w1k, w2k, w1v, w2v, kgain, tables, batch, seq):
    groups = w1k.shape[0]
    ncp = seq // CMP_STRIDE

    def full(a):
        return pl.BlockSpec(a.shape, lambda b, g: (0,) * a.ndim)

    def per_group(a):
        return pl.BlockSpec((1,) + a.shape[1:], lambda b, g: (g,) + (0,) * (a.ndim - 1))

    return pl.pallas_call(
        _compress_kernel,
        grid=(batch, groups),
        in_specs=[pl.BlockSpec((seq, LANES), lambda b, g: (b, k_blk)),
                  pl.BlockSpec((seq, LANES), lambda b, g: (b, k_blk + 1)),
                  full(pk), full(pv), per_group(w1k), full(w2k), per_group(w1v), full(w2v),
                  full(kgain), full(tables[0]), full(tables[1]), full(tables[2])],
        out_specs=[pl.BlockSpec((1, 1, ncp, LANES), lambda b, g: (b, g, 0, 0)),
                   pl.BlockSpec((1, 1, HEAD_DIM, ncp), lambda b, g: (b, g, 0, 0))],
        out_shape=[jax.ShapeDtypeStruct((batch, groups, ncp, LANES), BF16),
                   jax.ShapeDtypeStruct((batch, groups, HEAD_DIM, ncp), BF16)],
        compiler_params=_cparams(("parallel", "parallel"), 32),
        name="nsa_compress",
    )(p1, p1, pk, pv, w1k, w2k, w1v, w2v, kgain, *tables)


def _nsa_kernel(qi_tab, tile_tab, br_tab, q_ref, kc_ref, vct_ref, k_ref, vt_ref, gt_ref,
                dg_ref, ovt_ref, o_ref, acc_ref, oc_ref, sel_ref, imp_ref, sta_ref, stb_ref, cst_ref,
                cmax_ref, m_ref, qh_ref, *, tq, qt, nslc, nsel, nentries):
    g = pl.program_id(1)
    rheads = D_HEADS // D_KV_GROUPS
    seq = qt * tq
    tiny = float(np.finfo(np.float32).tiny)
    qcol = lax.broadcasted_iota(jnp.int32, (tq, tq), 1)
    krow = lax.broadcasted_iota(jnp.int32, (tq, tq), 0)
    bufs = (sta_ref, stb_ref)
    blocks_per_tile = tq // SLC_BLOCK
    nslab = -(-nslc // SUBLANES)
    sel_rows = nslab * SUBLANES

    def rows(tile):
        return pl.ds(pl.multiple_of(tile * tq, tq), tq)

    half_lane = lax.broadcasted_iota(jnp.int32, (1, LANES), 1) // HEAD_DIM

    def split_heads(t, c):
        for r in range(rheads):
            blk = q_ref[rows(t), (r // 2) * LANES:(r // 2 + 1) * LANES]
            qh_ref[rows(t), r * LANES:(r + 1) * LANES] = jnp.where(half_lane == r % 2, blk,
                                                                  jnp.zeros_like(blk))
        return c

    lax.fori_loop(0, qt, split_heads, 0)

    def heads_q(qi):
        return [qh_ref[rows(qi), r * LANES:(r + 1) * LANES] for r in range(rheads)]

    kc = kc_ref[0, 0]
    vct = vct_ref[0, 0]
    ncp = kc.shape[0]
    ovt = ovt_ref[...]

    def select_blocks(qi, c):
        q = heads_q(qi)
        cend = lax.broadcasted_iota(jnp.int32, (ncp, tq), 0) * CMP_STRIDE + (CMP_BLOCK - 1)
        cvalid = cend <= qi * tq + lax.broadcasted_iota(jnp.int32, (ncp, tq), 1)
        psum = jnp.zeros((ncp, tq), F32)
        cmax = [_scores_t(kc, q[r], cvalid, cst_ref, r) for r in range(rheads)]
        for r in range(rheads):
            m = jnp.where(jnp.isfinite(cmax[r]), cmax[r], 0.0)
            e = jnp.exp2(cst_ref[r] - m)
            p = e / jnp.maximum(jnp.sum(e, axis=0, keepdims=True), tiny)
            psum += p
            oc_ref[qi * rheads + r] = jnp.dot(vct, p.astype(BF16), preferred_element_type=F32)
        hi, lo = _split_bf16(psum)
        imp = (jnp.dot(ovt, hi, preferred_element_type=F32)
               + jnp.dot(ovt, lo, preferred_element_type=F32))
        blk = lax.broadcasted_iota(jnp.int32, (NSLC_PAD, tq), 0)
        cur = (qi * tq + lax.broadcasted_iota(jnp.int32, (NSLC_PAD, tq), 1)) // SLC_BLOCK
        imp = jnp.where(blk > cur, -jnp.inf, imp)
        forced = (blk == 0) | (blk == cur) | (blk == cur - 1)
        imp = jnp.where(forced, jnp.inf, imp)
        imp_ref[...] = imp
        slabs = [imp[s * SUBLANES:(s + 1) * SUBLANES, :] for s in range(nslab)]
        ranks = [jnp.zeros((SUBLANES, tq), F32) for _ in range(nslab)]
        sub = lax.broadcasted_iota(jnp.int32, (SUBLANES, tq), 0)
        for jp in range(nslc):
            row = imp_ref[jp:jp + 1, :]
            for s in range(nslab):
                if s * SUBLANES > jp:
                    ahead = jnp.where(row >= slabs[s], 1.0, 0.0)
                elif (s + 1) * SUBLANES - 1 < jp:
                    ahead = jnp.where(row > slabs[s], 1.0, 0.0)
                else:
                    ahead = jnp.where(sub > jp - s * SUBLANES, jnp.where(row >= slabs[s], 1.0, 0.0),
                                      jnp.where(row > slabs[s], 1.0, 0.0))
                ranks[s] = ranks[s] + ahead
        for s in range(nslab):
            start = pl.multiple_of(qi * sel_rows + s * SUBLANES, SUBLANES)
            sel_ref[pl.ds(start, SUBLANES), :] = jnp.where(ranks[s] < nsel, 1.0, 0.0)
        return c

    lax.fori_loop(0, qt, select_blocks, 0)
    acc_ref[...] = jnp.zeros_like(acc_ref)
    m_ref[...] = jnp.full(m_ref.shape, -jnp.inf, F32)

    def score_fn(e, buf):
        qi, tile, br = qi_tab[e], tile_tab[e], br_tab[e]
        q = heads_q(qi)
        k = k_ref[br, rows(tile), :]
        base = qi * sel_rows + tile * blocks_per_tile
        chosen = jnp.concatenate(
            [jnp.broadcast_to(sel_ref[pl.ds(base + c, 1), :], (SLC_BLOCK, tq))
             for c in range(blocks_per_tile)], axis=0)
        dpos = (qi - tile) * tq + (qcol - krow)
        reach = jnp.where(br == 0, seq, WINDOW).astype(jnp.uint32)
        kept = jnp.where(jnp.maximum(chosen, br.astype(F32)) > 0.5, dpos, -1)
        valid = kept.astype(jnp.uint32) < reach
        for r in range(rheads):
            row = buf * rheads + r
            cmax_ref[row:row + 1, :] = _scores_t(k, q[r], valid, bufs[buf], r)

    def accum_fn(e, buf):
        qi, tile, br = qi_tab[e], tile_tab[e], br_tab[e]
        vt = vt_ref[0, br, tile]
        for r in range(rheads):
            cm, sr = buf * rheads + r, (qi * 2 + br) * rheads + r
            m_ref[pl.ds(sr, 1), :] = _accumulate_t(bufs[buf], r, cmax_ref[cm:cm + 1, :],
                                                   m_ref[pl.ds(sr, 1), :], acc_ref, sr, vt, True)

    _pipelined_sweep(nentries, score_fn, accum_fn, unroll=8)

    def finish(qi, c):
        outs = []
        for r in range(rheads):
            row0 = (g * rheads + r) * 3
            o = [oc_ref[qi * rheads + r]]
            for br in range(2):
                sr = (qi * 2 + br) * rheads + r
                o.append(acc_ref[sr, 0:HEAD_DIM, :]
                         / jnp.maximum(acc_ref[sr, HEAD_DIM:HEAD_DIM + 1, :], tiny))
            outs.append(sum(gt_ref[0, qi, pl.ds(row0 + j, 1), :] * o[j] for j in range(3)))
        out = jnp.concatenate(outs, axis=0).T
        o_ref[rows(qi), :] = (out * jax.nn.silu(dg_ref[rows(qi), :])).astype(BF16)
        return c

    lax.fori_loop(0, qt, finish, 0)


def _nsa_attn(qe, kc, vct, kboth, vtboth, gt, p1, ovt, batch, seq, tq, dg_blk):
    n = qe.shape[0]
    groups = D_KV_GROUPS
    rheads = D_HEADS // groups
    width = rheads * HEAD_DIM
    qt = seq // tq
    ncp = kc.shape[2]
    nslc = seq // SLC_BLOCK
    sel_rows = -(-nslc // SUBLANES) * SUBLANES
    wtiles = (WINDOW + tq - 1) // tq
    entries = []
    for qi in range(qt):
        entries += [(qi, t, 0) for t in range(qi + 1)]
        entries += [(qi, t, 1) for t in range(max(qi - wtiles, 0), qi + 1)]
    tabs = [jnp.asarray([e[c] for e in entries], jnp.int32) for c in range(3)]
    kernel = functools.partial(_nsa_kernel, tq=tq, qt=qt, nslc=nslc, nsel=min(SLC_TOPK, nslc),
                               nentries=len(entries))
    return pl.pallas_call(
        kernel,
        grid_spec=pltpu.PrefetchScalarGridSpec(
            num_scalar_prefetch=3,
            grid=(batch, groups),
            in_specs=[pl.BlockSpec((seq, width), lambda b, g, *_: (b, g)),
                      pl.BlockSpec((1, 1, ncp, LANES), lambda b, g, *_: (b, g, 0, 0)),
                      pl.BlockSpec((1, 1, HEAD_DIM, ncp), lambda b, g, *_: (b, g, 0, 0)),
                      pl.BlockSpec((2, seq, LANES), lambda b, g, *_: (0, b, g)),
                      pl.BlockSpec((1, 2, qt, HEAD_DIM + ONES_ROWS, tq), lambda b, g, *_: (b, 0, 0, g, 0)),
                      pl.BlockSpec((1, qt, LANES, tq), lambda b, g, *_: (b, 0, 0, 0)),
                      pl.BlockSpec((seq, width), lambda b, g, *_: (b, dg_blk + g)),
                      pl.BlockSpec(ovt.shape, lambda b, g, *_: (0, 0))],
            out_specs=pl.BlockSpec((seq, width), lambda b, g, *_: (b, g)),
            scratch_shapes=[pltpu.VMEM((qt * 2 * rheads, HEAD_DIM + ONES_ROWS, tq), F32),
                            pltpu.VMEM((qt * rheads, HEAD_DIM, tq), F32),
                            pltpu.VMEM((qt * sel_rows, tq), F32),
                            pltpu.VMEM((NSLC_PAD, tq), F32),
                            pltpu.VMEM((rheads, tq, tq), F32),
                            pltpu.VMEM((rheads, tq, tq), F32),
                            pltpu.VMEM((rheads, ncp, tq), F32),
                            pltpu.VMEM((2 * rheads, tq), F32),
                            pltpu.VMEM((qt * 2 * rheads, tq), F32),
                            pltpu.VMEM((seq, rheads * LANES), BF16)]),
        out_shape=jax.ShapeDtypeStruct((n, groups * width), BF16),
        compiler_params=_cparams(("parallel", "parallel"), 56),
        name="nsa_attn",
    )(*tabs, qe, kc, vct, kboth, vtboth, gt, p1, ovt)


def _rope_tables(pos, d):
    rd = d // 4
    half = rd // 2
    inv = ROPE_THETA ** (-jnp.arange(half, dtype=F32) / half)
    ang = pos.astype(F32)[:, None] * inv[None, :]
    cos, sin = jnp.cos(ang), jnp.sin(ang)
    npos = pos.shape[0]
    zeros = jnp.zeros((npos, d), F32)
    c = jnp.concatenate([cos, cos, jnp.ones((npos, d - rd), F32)], axis=1)
    s1 = zeros.at[:, :half].set(-sin)
    s2 = zeros.at[:, half:rd].set(sin)
    return tuple(jnp.tile(t, (1, LANES // d)) for t in (c, s1, s2))


def _overlap_t(ncp, nslc):
    start = np.arange(ncp) * CMP_STRIDE
    s0 = np.arange(nslc) * SLC_BLOCK
    lo = np.maximum(start[:, None], s0[None, :])
    hi = np.minimum(start[:, None] + CMP_BLOCK, s0[None, :] + SLC_BLOCK)
    ov = np.clip(hi - lo, 0, None) / CMP_BLOCK
    ov[ncp - 1] = 0.0
    out = np.zeros((NSLC_PAD, ncp), np.float32)
    out[:nslc] = ov.T
    return jnp.asarray(out, BF16)


def kernel(x, mem, mem_norm, l0_norm, l0_w_in, l0_a_vnorm, l0_a_ws, l0_a_bs, l0_b_qnorm, l0_b_knorm, l0_b_lq1, l0_b_lk1, l0_b_lq2, l0_b_lk2, l0_b_subln, l0_m_wkv, l0_m_qnorm, l0_m_knorm, l0_w_out, l1_norm, l1_w_in, l1_c_conv_w, l1_c_conv_b, l1_c_norm, l1_d_qnorm, l1_d_knorm, l1_d_cmp_pos_k, l1_d_cmp_w1_k, l1_d_cmp_w2_k, l1_d_cmp_pos_v, l1_d_cmp_w1_v, l1_d_cmp_w2_v, l1_m_wkv, l1_m_qnorm, l1_m_knorm, l1_w_out):
    batch, seq, d_model = x.shape
    n = batch * seq
    tm = 256
    tq = 256
    mw = M_HEADS * HEAD_DIM
    pos = jnp.arange(seq, dtype=jnp.int32)
    x0 = x.reshape(n, d_model)

    wkv_both = jnp.concatenate([l0_m_wkv, l1_m_wkv], axis=1).astype(BF16)
    kgains = jnp.stack([jnp.tile(l0_m_knorm, M_HEADS), jnp.tile(l1_m_knorm, M_HEADS)])
    mem_kh, mem_vt = _mem_kv(mem, mem_norm, wkv_both, kgains)

    w = l0_w_in
    tab32 = _rope_tables(pos, B_QK_DIM)
    b_width = 2 * B_HEADS * B_QK_DIM
    a_width = l0_a_vnorm.shape[0]
    bs_exp = jnp.repeat(l0_a_bs.T, a_width // A_GROUPS, axis=1)
    p0, y_a, qn, kn, vt = _layer0_proj(
        x0, l0_norm, w[:, 0:1536].astype(BF16), w[:, 1536:3072].astype(BF16), w[:, 3072:4096].astype(BF16),
        tab32, jnp.tile(l0_b_qnorm, b_width // B_QK_DIM).reshape(1, b_width),
        jnp.tile(l0_b_knorm, b_width // B_QK_DIM).reshape(1, b_width),
        l0_a_ws, bs_exp, l0_a_vnorm.reshape(1, a_width), batch, seq, tq)
    lam_init = 0.8 - 0.6 * math.exp(-0.3 * 1)
    lam_p = jnp.stack([l0_b_lq1, l0_b_lk1, l0_b_lq2, l0_b_lk2])
    y_b = _diff_attn(lam_p, qn, kn, vt, p0, jnp.tile(l0_b_subln, 2).reshape(1, LANES),
                     batch, seq, tq, lam_init, 0)
    y_m = _mem_attn(p0, mem_kh, mem_vt, 0, jnp.tile(l0_m_qnorm, M_HEADS).reshape(1, mw),
                    512 // mw, 768 // mw, seq, 2 * tq)
    x1 = _out_proj(x0, y_a, y_b, y_m, l0_w_out.astype(BF16), tm)

    w = l1_w_in
    n_bg = 3 * D_HEADS
    w_rest = jnp.concatenate([w[:, 2840:3352], w[:, 2048:2304], w[:, 3352:3864]], axis=1)
    w_d = jnp.concatenate([w[:, 1536:2048], w[:, 2304:2816], w[:, 2816:2840],
                           jnp.zeros((d_model, LANES - n_bg), F32)], axis=1)
    groups, rheads = D_KV_GROUPS, D_HEADS // D_KV_GROUPS
    d_width = D_HEADS * HEAD_DIM
    tab64 = _rope_tables(pos, HEAD_DIM)
    p1, y_c, qe, kboth, vtboth, gt = _layer1_proj(
        x1, l1_norm, w[:, 0:1536].astype(BF16), w_d.astype(BF16), w_rest.astype(BF16), tab64,
        jnp.tile(l1_d_qnorm, D_HEADS).reshape(1, d_width),
        jnp.tile(l1_d_knorm, LANES // HEAD_DIM).reshape(1, LANES),
        l1_c_conv_w, l1_c_conv_b, l1_c_norm, batch, seq, tq)
    kv_blk = 512 // LANES

    ncp = seq // CMP_STRIDE
    cmp_pos = jnp.arange(ncp, dtype=jnp.int32) * CMP_STRIDE + (CMP_BLOCK - 1)
    hidden = l1_d_cmp_w2_k.shape[0]

    def group_w1(w1):
        w = w1.reshape(CMP_BLOCK, HEAD_DIM, hidden)
        return jnp.stack([jnp.zeros((CMP_BLOCK, LANES, hidden), F32)
                          .at[:, g * HEAD_DIM:(g + 1) * HEAD_DIM, :].set(w) for g in range(groups)])

    w2k = jnp.tile(l1_d_cmp_w2_k, (1, LANES // HEAD_DIM))
    w2v = jnp.zeros((hidden, LANES), F32).at[:, :HEAD_DIM].set(l1_d_cmp_w2_v)
    kc, vct = _compress(
        p1, kv_blk, jnp.tile(l1_d_cmp_pos_k, (1, groups)), jnp.tile(l1_d_cmp_pos_v, (1, groups)),
        group_w1(l1_d_cmp_w1_k).astype(BF16), w2k.astype(BF16),
        group_w1(l1_d_cmp_w1_v).astype(BF16), w2v.astype(BF16),
        jnp.tile(l1_d_knorm, LANES // HEAD_DIM).reshape(1, LANES), _rope_tables(cmp_pos, HEAD_DIM),
        batch, seq)

    nslc = seq // SLC_BLOCK
    y_d = _nsa_attn(qe, kc, vct, kboth, vtboth, gt, p1, _overlap_t(ncp, nslc),
                    batch, seq, tq, 0)
    y_m1 = _mem_attn(p1, mem_kh, mem_vt, 1, jnp.tile(l1_m_qnorm, M_HEADS).reshape(1, mw),
                     768 // mw, 1024 // mw, seq, 2 * tq)
    out = _out_proj(x1, y_c, y_d, y_m1, l1_w_out.astype(BF16), tm)
    return out.reshape(batch, seq, d_model)
```

```python
import functools
import math

import numpy as np
import jax
import jax.numpy as jnp
from jax import lax
from jax.experimental import pallas as pl
from jax.experimental.pallas import tpu as pltpu

F32 = jnp.float32
BF16 = jnp.bfloat16

EPS = 1e-6
ROPE_THETA = 500000.0
HEAD_DIM = 64
A_CHUNK = 128
A_GROUPS = 4
B_HEADS = 8
B_QK_DIM = 32
C_KERNEL = 31
D_HEADS = 8
D_KV_GROUPS = 2
CMP_BLOCK = 32
CMP_STRIDE = 16
SLC_BLOCK = 64
SLC_TOPK = 16
WINDOW = 512
M_HEADS = 4

LANES = 128
SUBLANES = 8
MXU_DIM = 256
HALO = 32
NSLC_PAD = 128
ONES_ROWS = 16

NT_DIMS = (((1,), (1,)), ((), ()))
LOG2E = math.log2(math.e)


def _cparams(semantics, vmem_mb):
    return pltpu.CompilerParams(dimension_semantics=semantics,
                                vmem_limit_bytes=vmem_mb * 1024 * 1024)


def _split_bf16(x):
    hi = x.astype(BF16)
    lo = (x - hi.astype(F32)).astype(BF16)
    return hi, lo


def _group_mean(x2, gsize):
    t, c = x2.shape
    w = min(c, MXU_DIM)
    r = lax.broadcasted_iota(jnp.int32, (w, w), 0) // gsize
    cc = lax.broadcasted_iota(jnp.int32, (w, w), 1) // gsize
    ones = jnp.where(r == cc, 1.0, 0.0).astype(BF16)
    outs = []
    for s in range(c // w):
        hi, lo = _split_bf16(x2[:, s * w:(s + 1) * w])
        outs.append(jnp.dot(hi, ones, preferred_element_type=F32)
                    + jnp.dot(lo, ones, preferred_element_type=F32))
    out = outs[0] if len(outs) == 1 else jnp.concatenate(outs, axis=1)
    return out * (1.0 / gsize)


def _tile_lanes(t, width):
    rep = width // t.shape[1]
    return t if rep == 1 else jnp.concatenate([t] * rep, axis=1)


def _rope(x, cos_t, s1_t, s2_t, half):
    width = x.shape[1]
    up = pltpu.roll(x, width - half, 1)
    dn = pltpu.roll(x, half, 1)
    return (x * _tile_lanes(cos_t, width) + up * _tile_lanes(s1_t, width)
            + dn * _tile_lanes(s2_t, width))


def _matmul_cols(h, w_ref, lo, hi, out_ref=None):
    r = jnp.dot(h, w_ref[:, lo:hi], preferred_element_type=F32)
    if out_ref is None:
        return r
    out_ref[:, lo:hi] = r


def _rms_rows(x, ng_ref):
    ms = jnp.mean(x * x, axis=-1, keepdims=True)
    return (x * lax.rsqrt(ms + EPS) * ng_ref[...]).astype(BF16)


def _scores_t(k, q, valid, st_ref, idx):
    st = lax.dot_general(k, q, NT_DIMS, preferred_element_type=F32)
    if valid is not None:
        st = jnp.where(valid, st, -jnp.inf)
    st_ref[idx] = st
    return jnp.max(st, axis=0, keepdims=True)


def _accumulate_t(st_ref, idx, cmax, m, acc_ref, aidx, vt, guard_empty):
    m_new = jnp.maximum(m, cmax)
    m_use = jnp.where(m_new == -jnp.inf, 0.0, m_new) if guard_empty else m_new
    alpha = jnp.exp2(m - m_use)
    pt = jnp.exp2((st_ref[idx] - m_use).astype(BF16))
    acc_ref[aidx] = alpha * acc_ref[aidx] + jnp.dot(vt, pt, preferred_element_type=F32)
    return m_new


def _pipelined_sweep(n, score_fn, accum_fn, unroll=4):
    score_fn(0, 0)
    stages = n - 1

    def stage(e, parity):
        score_fn(e + 1, 1 - parity)
        accum_fn(e, parity)

    def body(i, c):
        for k in range(unroll):
            stage(unroll * i + k, k % 2)
        return c

    if isinstance(n, int):
        iters, left = divmod(stages, unroll)
        if iters:
            lax.fori_loop(0, iters, body, 0)
        for k in range(left):
            stage(iters * unroll + k, k % 2)
        accum_fn(n - 1, left % 2)
        return
    lax.fori_loop(0, stages // unroll, body, 0)
    base = stages // unroll * unroll
    left = stages - base
    for k in range(unroll - 1):
        pl.when(left > k)(functools.partial(stage, base + k, k % 2))
    for parity in range(2):
        pl.when(left % 2 == parity)(functools.partial(accum_fn, n - 1, parity))


def _mix_out(ya_ref, yb_ref, ym_ref, w_ref):
    wa = ya_ref.shape[1]
    wb = yb_ref.shape[1]
    acc = jnp.dot(ya_ref[...], w_ref[0:wa, :], preferred_element_type=F32)
    acc += jnp.dot(yb_ref[...], w_ref[wa:wa + wb, :], preferred_element_type=F32)
    acc += jnp.dot(ym_ref[...], w_ref[wa + wb:, :], preferred_element_type=F32)
    return acc


def _out_proj_kernel(x_ref, ya_ref, yb_ref, ym_ref, w_ref, o_ref):
    o_ref[...] = x_ref[...] + _mix_out(ya_ref, yb_ref, ym_ref, w_ref)


def _out_proj(x, ya, yb, ym, w, tm):
    n, d = x.shape
    return pl.pallas_call(
        _out_proj_kernel,
        grid=(n // tm,),
        in_specs=[pl.BlockSpec((tm, d), lambda i: (i, 0)),
                  pl.BlockSpec((tm, ya.shape[1]), lambda i: (i, 0)),
                  pl.BlockSpec((tm, yb.shape[1]), lambda i: (i, 0)),
                  pl.BlockSpec((tm, ym.shape[1]), lambda i: (i, 0)),
                  pl.BlockSpec(w.shape, lambda i: (0, 0))],
        out_specs=pl.BlockSpec((tm, d), lambda i: (i, 0)),
        out_shape=jax.ShapeDtypeStruct((n, d), F32),
        compiler_params=_cparams(("parallel",), 32),
        name="out_proj",
    )(x, ya, yb, ym, w)


def _sgu_group(u, v, gate, w_ref, bs, vgain, o_ref, cols):
    tm = u.shape[0]
    row = lax.broadcasted_iota(jnp.int32, (A_CHUNK, A_CHUNK), 0)
    col = lax.broadcasted_iota(jnp.int32, (A_CHUNK, A_CHUNK), 1)
    w = jnp.where(col <= row, w_ref, 0.0).astype(BF16)
    gv = jax.nn.gelu(v)
    ms = jnp.mean(gv * gv, axis=-1, keepdims=True)
    vn = (gv * lax.rsqrt(ms + EPS) * vgain).astype(BF16)
    gu = jax.nn.gelu(u) * jax.nn.silu(gate)
    for c in range(tm // A_CHUNK):
        rows = slice(c * A_CHUNK, (c + 1) * A_CHUNK)
        z = jnp.dot(w, vn[rows, :], preferred_element_type=F32) + bs
        o_ref[rows, cols] = (gu[rows, :] * z).astype(BF16)


def _layer0_proj_kernel(x_ref, ng_ref, wa_ref, wqkv_ref, w_ref, c_ref, s1_ref, s2_ref, qg_ref, kg_ref,
                        ws_ref, bs_ref, vg_ref, p_ref, ya_ref, qo_ref, ko_ref, vto_ref):
    h = _rms_rows(x_ref[...], ng_ref)
    aw = vg_ref.shape[1]
    gdim = aw // A_GROUPS
    width = qg_ref.shape[1]
    half = B_QK_DIM // 4 // 2
    pw = p_ref.shape[1]

    a = _matmul_cols(h, wa_ref, 0, 3 * aw)

    def sgu(g):
        cols = slice(g * gdim, (g + 1) * gdim)
        _sgu_group(a[:, cols], a[:, aw + g * gdim:aw + (g + 1) * gdim],
                   a[:, 2 * aw + g * gdim:2 * aw + (g + 1) * gdim],
                   ws_ref[g], bs_ref[:, cols], vg_ref[:, cols], ya_ref, cols)

    def prep(t, gain, dst, mul):
        tn = t * lax.rsqrt(_group_mean(t * t, B_QK_DIM) + EPS) * gain[...]
        tr = _rope(tn, c_ref[...], s1_ref[...], s2_ref[...], half)
        dst[...] = (tr * mul if mul != 1.0 else tr).astype(BF16)

    q = _matmul_cols(h, wqkv_ref, 0, width)
    sgu(0)
    k = _matmul_cols(h, wqkv_ref, width, 2 * width)
    sgu(1)
    v = _matmul_cols(h, wqkv_ref, 2 * width, 3 * width)
    sgu(2)
    _matmul_cols(h, w_ref, 0, pw // 2, p_ref)
    sgu(3)
    prep(q, qg_ref, qo_ref, B_QK_DIM ** -0.5 * LOG2E)
    _matmul_cols(h, w_ref, pw // 2, pw, p_ref)
    prep(k, kg_ref, ko_ref, 1.0)
    vt = v.T
    dv = 2 * B_QK_DIM
    ones = jnp.ones((ONES_ROWS, vt.shape[1]), F32)
    parts = []
    for hd in range(B_HEADS):
        parts += [vt[hd * dv:(hd + 1) * dv, :], ones]
    vto_ref[0, 0] = jnp.concatenate(parts, axis=0).astype(BF16)


def _layer0_proj(x, norm_g, w_a, w_qkv, w_rest, tables, qgain, kgain, a_ws, bs_exp, vgain, batch, seq, tm):
    n, d = x.shape
    width = qgain.shape[1]
    aw = vgain.shape[1]
    tiles_per_seq = seq // tm
    tab_spec = pl.BlockSpec((tm, LANES), lambda i: (i % tiles_per_seq, 0))
    vrows = B_HEADS * (2 * B_QK_DIM + ONES_ROWS)

    def full(a):
        return pl.BlockSpec(a.shape, lambda i: (0,) * a.ndim)

    def rows(w):
        return pl.BlockSpec((tm, w), lambda i: (i, 0))

    return pl.pallas_call(
        _layer0_proj_kernel,
        grid=(n // tm,),
        in_specs=[rows(d), pl.BlockSpec((1, d), lambda i: (0, 0)), full(w_a), full(w_qkv), full(w_rest),
                  tab_spec, tab_spec, tab_spec, full(qgain), full(kgain),
                  full(a_ws), full(bs_exp), full(vgain)],
        out_specs=[rows(w_rest.shape[1]), rows(aw), rows(width), rows(width),
                   pl.BlockSpec((1, 1, vrows, tm),
                                lambda i: (i // tiles_per_seq, i % tiles_per_seq, 0, 0))],
        out_shape=[jax.ShapeDtypeStruct((n, w_rest.shape[1]), F32),
                   jax.ShapeDtypeStruct((n, aw), BF16),
                   jax.ShapeDtypeStruct((n, width), BF16), jax.ShapeDtypeStruct((n, width), BF16),
                   jax.ShapeDtypeStruct((batch, tiles_per_seq, vrows, tm), BF16)],
        compiler_params=_cparams(("parallel",), 48),
        name="layer0_proj",
    )(x, norm_g.reshape(1, d), w_a, w_qkv, w_rest, *tables, qgain, kgain, a_ws, bs_exp, vgain)


def _diff_attn_kernel(qi_tab, tile_tab, lam_ref, q_ref, k_ref, vt_ref, g_ref, sub_ref, o_ref,
                      acc_ref, sta_ref, stb_ref, cmax_ref, m_ref, *, tq, qt, lam_init):
    dv = 2 * B_QK_DIM
    vrows = dv + ONES_ROWS
    nmap = 4
    lane = lax.broadcasted_iota(jnp.int32, (1, LANES), 1)
    bufs = (sta_ref, stb_ref)
    acc_ref[...] = jnp.zeros_like(acc_ref)
    m_ref[...] = jnp.full(m_ref.shape, -jnp.inf, F32)
    krow = lax.broadcasted_iota(jnp.int32, (tq, tq), 0)
    qcol = lax.broadcasted_iota(jnp.int32, (tq, tq), 1)
    lam_p = lam_ref[...]
    lam = (jnp.exp(jnp.sum(lam_p[0:1] * lam_p[1:2], axis=-1, keepdims=True))
           - jnp.exp(jnp.sum(lam_p[2:3] * lam_p[3:4], axis=-1, keepdims=True)) + lam_init)

    def rows(tile):
        return pl.ds(pl.multiple_of(tile * tq, tq), tq)

    def scores(qi, tile, buf, valid):
        q = q_ref[rows(qi), :]
        k = k_ref[rows(tile), :]
        zero = jnp.zeros_like(q)
        for j in range(nmap):
            qm = jnp.where(lane // B_QK_DIM == j, q, zero)
            row = buf * nmap + j
            cmax_ref[row:row + 1, :] = _scores_t(k, qm, valid, bufs[buf], j)

    def accumulate(qi, tile, buf):
        for j in range(nmap):
            vt = vt_ref[0, tile, (j // 2) * vrows:(j // 2 + 1) * vrows, :]
            cm, sr = buf * nmap + j, qi * nmap + j
            m_ref[pl.ds(sr, 1), :] = _accumulate_t(bufs[buf], j, cmax_ref[cm:cm + 1, :],
                                                   m_ref[pl.ds(sr, 1), :], acc_ref, sr, vt, False)

    def finish(qi):
        halves = []
        for h in range(2):
            o = [acc_ref[qi * nmap + 2 * h + mp, 0:dv, :] / acc_ref[qi * nmap + 2 * h + mp, dv:dv + 1, :]
                 for mp in range(2)]
            pd = o[0] - lam * o[1]
            ms2 = jnp.mean(pd * pd, axis=0, keepdims=True)
            halves.append(pd * lax.rsqrt(ms2 + EPS))
        ob = jnp.concatenate(halves, axis=0).T * sub_ref[...] * (1.0 - lam_init)
        o_ref[rows(qi), :] = (ob * jax.nn.silu(g_ref[rows(qi), :])).astype(BF16)

    def below_score(e, buf):
        scores(qi_tab[e], tile_tab[e], buf, None)

    def below_accum(e, buf):
        accumulate(qi_tab[e], tile_tab[e], buf)

    def diag_score(e, buf):
        scores(e, e, buf, krow <= qcol)

    def diag_accum(e, buf):
        accumulate(e, e, buf)
        finish(e)

    _pipelined_sweep(qt * (qt - 1) // 2, below_score, below_accum, unroll=8)
    _pipelined_sweep(qt, diag_score, diag_accum)


def _diff_attn(lam_p, qn, kn, vt, p0, subln, batch, seq, tq, lam_init, g_off):
    n, width = qn.shape
    pairs = width // LANES
    qt = seq // tq
    vrows = 2 * B_QK_DIM + ONES_ROWS
    below = [(qi, t) for qi in range(qt) for t in range(qi)]
    qi_tab = jnp.asarray([e[0] for e in below], jnp.int32)
    tile_tab = jnp.asarray([e[1] for e in below], jnp.int32)
    kernel = functools.partial(_diff_attn_kernel, tq=tq, qt=qt, lam_init=lam_init)
    seq_spec = pl.BlockSpec((seq, LANES), lambda b, p, *_: (b, p))
    return pl.pallas_call(
        kernel,
        grid_spec=pltpu.PrefetchScalarGridSpec(
            num_scalar_prefetch=2,
            grid=(batch, pairs),
            in_specs=[pl.BlockSpec(lam_p.shape, lambda b, p, *_: (0, 0)),
                      seq_spec, seq_spec,
                      pl.BlockSpec((1, qt, 2 * vrows, tq), lambda b, p, *_: (b, 0, p, 0)),
                      pl.BlockSpec((seq, LANES), lambda b, p, *_: (b, g_off + p)),
                      pl.BlockSpec((1, LANES), lambda b, p, *_: (0, 0))],
            out_specs=seq_spec,
            scratch_shapes=[pltpu.VMEM((qt * 4, vrows, tq), F32), pltpu.VMEM((4, tq, tq), F32),
                            pltpu.VMEM((4, tq, tq), F32), pltpu.VMEM((8, tq), F32),
                            pltpu.VMEM((qt * 4, tq), F32)]),
        out_shape=jax.ShapeDtypeStruct((n, width), BF16),
        compiler_params=_cparams(("parallel", "parallel"), 40),
        name="diff_attn",
    )(qi_tab, tile_tab, lam_p, qn, kn, vt, p0, subln)


def _mem_kv_kernel(mem_ref, g_ref, w_ref, kg_ref, kh_ref, vt_ref):
    x = mem_ref[0]
    ms = jnp.mean(x * x, axis=-1, keepdims=True)
    h = (x * lax.rsqrt(ms + EPS) * g_ref[...]).astype(BF16)
    kv = jnp.dot(h, w_ref[...], preferred_element_type=F32)
    mw = kv.shape[1] // 4
    lane = lax.broadcasted_iota(jnp.int32, (1, mw), 1)
    ones = jnp.ones((ONES_ROWS, kv.shape[0]), F32)
    for layer in range(2):
        k = kv[:, (2 * layer) * mw:(2 * layer + 1) * mw]
        kn = k * lax.rsqrt(_group_mean(k * k, HEAD_DIM) + EPS) * kg_ref[layer:layer + 1, :]
        vt = kv[:, (2 * layer + 1) * mw:(2 * layer + 2) * mw].T
        parts = []
        for hd in range(M_HEADS):
            kh_ref[layer, 0, hd] = jnp.where(lane // HEAD_DIM == hd, kn, 0.0).astype(BF16)
            parts += [vt[hd * HEAD_DIM:(hd + 1) * HEAD_DIM, :], ones]
        vt_ref[layer, 0] = jnp.concatenate(parts, axis=0).astype(BF16)


def _mem_kv(mem, mem_norm, wkv_both, kgains):
    batch, mtok, d = mem.shape
    mw = wkv_both.shape[1] // 4
    vrows = M_HEADS * (HEAD_DIM + ONES_ROWS)
    return pl.pallas_call(
        _mem_kv_kernel,
        grid=(batch,),
        in_specs=[pl.BlockSpec((1, mtok, d), lambda b: (b, 0, 0)),
                  pl.BlockSpec((1, d), lambda b: (0, 0)),
                  pl.BlockSpec(wkv_both.shape, lambda b: (0, 0)),
                  pl.BlockSpec(kgains.shape, lambda b: (0, 0))],
        out_specs=[pl.BlockSpec((2, 1, M_HEADS, mtok, mw), lambda b: (0, b, 0, 0, 0)),
                   pl.BlockSpec((2, 1, vrows, mtok), lambda b: (0, b, 0, 0))],
        out_shape=[jax.ShapeDtypeStruct((2, batch, M_HEADS, mtok, mw), BF16),
                   jax.ShapeDtypeStruct((2, batch, vrows, mtok), BF16)],
        compiler_params=_cparams(("parallel",), 32),
        name="mem_kv",
    )(mem, mem_norm.reshape(1, d), wkv_both, kgains)


def _mem_attn_kernel(q_ref, g_ref, kh_ref, vt_ref, qg_ref, o_ref, st_ref):
    vrows = HEAD_DIM + ONES_ROWS
    x = q_ref[...]
    qn = (x * lax.rsqrt(_group_mean(x * x, HEAD_DIM) + EPS) * qg_ref[...]
          * (HEAD_DIM ** -0.5 * LOG2E)).astype(BF16)
    cmax = [_scores_t(kh_ref[0, 0, h], qn, None, st_ref, h) for h in range(M_HEADS)]
    outs = []
    for h in range(M_HEADS):
        pt = jnp.exp2((st_ref[h] - cmax[h]).astype(BF16))
        acc = jnp.dot(vt_ref[0, 0, h * vrows:(h + 1) * vrows, :], pt, preferred_element_type=F32)
        outs.append(acc[0:HEAD_DIM, :] / acc[HEAD_DIM:HEAD_DIM + 1, :])
    out = jnp.concatenate(outs, axis=0).T
    o_ref[...] = (out * jax.nn.silu(g_ref[...])).astype(BF16)


def _mem_attn(p, kh, vt, layer, qgain, q_blk, g_blk, seq, tq):
    n = p.shape[0]
    _, batch, heads, mtok, mw = kh.shape
    qt = seq // tq
    return pl.pallas_call(
        _mem_attn_kernel,
        grid=(n // tq,),
        in_specs=[pl.BlockSpec((tq, mw), lambda i: (i, q_blk)),
                  pl.BlockSpec((tq, mw), lambda i: (i, g_blk)),
                  pl.BlockSpec((1, 1, heads, mtok, mw), lambda i: (layer, i // qt, 0, 0, 0)),
                  pl.BlockSpec((1, 1, vt.shape[2], mtok), lambda i: (layer, i // qt, 0, 0)),
                  pl.BlockSpec((1, mw), lambda i: (0, 0))],
        out_specs=pl.BlockSpec((tq, mw), lambda i: (i, 0)),
        out_shape=jax.ShapeDtypeStruct((n, mw), BF16),
        scratch_shapes=[pltpu.VMEM((heads, mtok, tq), F32)],
        compiler_params=_cparams(("parallel",), 32),
        name="mem_attn",
    )(p, p, kh, vt, qgain)


def _layer1_proj_kernel(x_ref, ya_ref, yb_ref, ym_ref, wo_ref, ng_ref, wc_ref, wd_ref, w_ref,
                        c_ref, s1_ref, s2_ref, qg_ref, kg_ref, cw_ref, cb_ref, cn_ref,
                        x1_ref, p_ref, yc_ref, qo_ref, ko_ref, vto_ref, gto_ref,
                        hbuf_ref, shift_ref, *, tiles_per_seq):
    i = pl.program_id(0)
    ts = x_ref.shape[0]
    span = ts + HALO
    cwid = cn_ref.shape[1]
    qw = qg_ref.shape[1]
    pw = p_ref.shape[1]
    half = HEAD_DIM // 4 // 2
    x1 = x_ref[...] + _mix_out(ya_ref, yb_ref, ym_ref, wo_ref)
    x1_ref[...] = x1
    h = _rms_rows(x1, ng_ref)

    @pl.when(i % tiles_per_seq == 0)
    def _():
        hbuf_ref[0:HALO, :] = jnp.zeros((HALO, cwid), F32)

    ca = _matmul_cols(h, wc_ref, 0, cwid)
    cbv = _matmul_cols(h, wc_ref, cwid, 2 * cwid)
    q = _matmul_cols(h, wd_ref, 0, qw)
    hbuf_ref[HALO:HALO + ts, :] = ca * jax.nn.sigmoid(cbv)
    for ph in range(SUBLANES):
        shift_ref[ph, 0:span - ph, :] = hbuf_ref[ph:span, :]
    cg = _matmul_cols(h, wc_ref, 2 * cwid, 3 * cwid)
    d = _matmul_cols(h, wd_ref, qw, wd_ref.shape[1])
    first = HALO - (C_KERNEL - 1)
    y = cb_ref[...]
    for j in range(C_KERNEL):
        ph, base = (first + j) % SUBLANES, (first + j) // SUBLANES * SUBLANES
        y = y + shift_ref[ph, base:base + ts, :] * cw_ref[j:j + 1, :]
    _matmul_cols(h, w_ref, 0, pw // 2, p_ref)
    ms = jnp.mean(y * y, axis=-1, keepdims=True)
    yn = y * lax.rsqrt(ms + EPS) * cn_ref[...]
    yc_ref[...] = (jax.nn.silu(yn) * jax.nn.silu(cg)).astype(BF16)
    hbuf_ref[0:HALO, :] = hbuf_ref[ts:ts + HALO, :]
    _matmul_cols(h, w_ref, pw // 2, pw, p_ref)

    tabs = (c_ref[...], s1_ref[...], s2_ref[...])
    tn = q * lax.rsqrt(_group_mean(q * q, HEAD_DIM) + EPS) * qg_ref[...]
    qo_ref[...] = (_rope(tn, *tabs, half) * (HEAD_DIM ** -0.5 * LOG2E)).astype(BF16)
    ones = jnp.ones((ONES_ROWS, ts), F32)
    lane = lax.broadcasted_iota(jnp.int32, (1, LANES), 1)
    for br in range(2):
        k = d[:, 2 * br * LANES:(2 * br + 1) * LANES]
        kn = k * lax.rsqrt(_group_mean(k * k, HEAD_DIM) + EPS) * kg_ref[...]
        kr = _rope(kn, *tabs, half)
        swapped = pltpu.roll(kr, HEAD_DIM, 1)
        for g in range(D_KV_GROUPS):
            own = (lane // HEAD_DIM) == g
            ko_ref[br, :, g * LANES:(g + 1) * LANES] = jnp.where(own, kr, swapped).astype(BF16)
        vt = d[:, (2 * br + 1) * LANES:(2 * br + 2) * LANES].T
        parts = []
        for g in range(D_KV_GROUPS):
            parts += [vt[g * HEAD_DIM:(g + 1) * HEAD_DIM, :], ones]
        vto_ref[0, br, 0] = jnp.concatenate(parts, axis=0).astype(BF16)
    gto_ref[0, 0] = jax.nn.sigmoid(d[:, 4 * LANES:5 * LANES]).T


def _layer1_proj(x, ya, yb, ym, w_out, norm_g, w_c, w_d, w_rest, tables, qgain, kgain,
                 conv_w, conv_b, conv_norm, batch, seq, tm):
    n, dm = x.shape
    cwid = conv_w.shape[1]
    tiles_per_seq = seq // tm
    tab_spec = pl.BlockSpec((tm, LANES), lambda i: (i % tiles_per_seq, 0))
    vrows = D_KV_GROUPS * (HEAD_DIM + ONES_ROWS)

    def full(a):
        return pl.BlockSpec(a.shape, lambda i: (0,) * a.ndim)

    def rows(w):
        return pl.BlockSpec((tm, w), lambda i: (i, 0))

    kernel = functools.partial(_layer1_proj_kernel, tiles_per_seq=tiles_per_seq)
    return pl.pallas_call(
        kernel,
        grid=(n // tm,),
        in_specs=[rows(dm), rows(ya.shape[1]), rows(yb.shape[1]), rows(ym.shape[1]), full(w_out),
                  pl.BlockSpec((1, dm), lambda i: (0, 0)), full(w_c), full(w_d), full(w_rest),
                  tab_spec, tab_spec, tab_spec, full(qgain), full(kgain),
                  full(conv_w), pl.BlockSpec((1, cwid), lambda i: (0, 0)),
                  pl.BlockSpec((1, cwid), lambda i: (0, 0))],
        out_specs=[rows(dm), rows(w_rest.shape[1]), rows(cwid), rows(qgain.shape[1]),
                   pl.BlockSpec((2, tm, D_KV_GROUPS * LANES), lambda i: (0, i, 0)),
                   pl.BlockSpec((1, 2, 1, vrows, tm),
                                lambda i: (i // tiles_per_seq, 0, i % tiles_per_seq, 0, 0)),
                   pl.BlockSpec((1, 1, LANES, tm),
                                lambda i: (i // tiles_per_seq, i % tiles_per_seq, 0, 0))],
        out_shape=[jax.ShapeDtypeStruct((n, dm), F32),
                   jax.ShapeDtypeStruct((n, w_rest.shape[1]), F32),
                   jax.ShapeDtypeStruct((n, cwid), BF16),
                   jax.ShapeDtypeStruct((n, qgain.shape[1]), BF16),
                   jax.ShapeDtypeStruct((2, n, D_KV_GROUPS * LANES), BF16),
                   jax.ShapeDtypeStruct((batch, 2, tiles_per_seq, vrows, tm), BF16),
                   jax.ShapeDtypeStruct((batch, tiles_per_seq, LANES, tm), F32)],
        scratch_shapes=[pltpu.VMEM((tm + HALO, cwid), F32),
                        pltpu.VMEM((SUBLANES, tm + HALO, cwid), F32)],
        compiler_params=_cparams(("arbitrary",), 52),
        name="layer1_proj",
    )(x, ya, yb, ym, w_out, norm_g.reshape(1, dm), w_c, w_d, w_rest, *tables, qgain, kgain, conv_w,
      conv_b.reshape(1, cwid), conv_norm.reshape(1, cwid))


def _compress_kernel(tk_ref, tv_ref, pk_ref, pv_ref, w1k_ref, w2k_ref, w1v_ref, w2v_ref,
                     kg_ref, c_ref, s1_ref, s2_ref, kc_ref, vct_ref):
    half = HEAD_DIM // 4 // 2
    ncp = kc_ref.shape[2]

    def mlp(t_ref, p_ref, w1_ref, w2):
        a = jnp.zeros((ncp, w1_ref.shape[3]), F32)
        b = jnp.zeros((ncp, w1_ref.shape[3]), F32)
        for l in range(CMP_STRIDE):
            x = t_ref[pl.ds(l, ncp, stride=CMP_STRIDE), :]
            a += jnp.dot((x + p_ref[l:l + 1, :]).astype(BF16), w1_ref[0, l],
                         preferred_element_type=F32)
            b += jnp.dot((x + p_ref[CMP_STRIDE + l:CMP_STRIDE + l + 1, :]).astype(BF16),
                         w1_ref[0, CMP_STRIDE + l], preferred_element_type=F32)
        h = a + pltpu.roll(b, ncp - 1, 0)
        return jnp.dot(jax.nn.silu(h).astype(BF16), w2, preferred_element_type=F32)

    kc = mlp(tk_ref, pk_ref, w1k_ref, w2k_ref[...])
    kn = kc * lax.rsqrt(_group_mean(kc * kc, HEAD_DIM) + EPS) * kg_ref[...]
    kc_ref[0, 0] = _rope(kn, c_ref[...], s1_ref[...], s2_ref[...], half).astype(BF16)
    vc = mlp(tv_ref, pv_ref, w1v_ref, w2v_ref[...])
    vct_ref[0, 0] = vc.T[0:HEAD_DIM, :].astype(BF16)


def _compress(p1, k_blk, pk, pv, w1k, w2k, w1v, w2v, kgain, tables, batch, seq):
    groups = w1k.shape[0]
    ncp = seq // CMP_STRIDE

    def full(a):
        return pl.BlockSpec(a.shape, lambda b, g: (0,) * a.ndim)

    def per_group(a):
        return pl.BlockSpec((1,) + a.shape[1:], lambda b, g: (g,) + (0,) * (a.ndim - 1))

    return pl.pallas_call(
        _compress_kernel,
        grid=(batch, groups),
        in_specs=[pl.BlockSpec((seq, LANES), lambda b, g: (b, k_blk)),
                  pl.BlockSpec((seq, LANES), lambda b, g: (b, k_blk + 1)),
                  full(pk), full(pv), per_group(w1k), full(w2k), per_group(w1v), full(w2v),
                  full(kgain), full(tables[0]), full(tables[1]), full(tables[2])],
        out_specs=[pl.BlockSpec((1, 1, ncp, LANES), lambda b, g: (b, g, 0, 0)),
                   pl.BlockSpec((1, 1, HEAD_DIM, ncp), lambda b, g: (b, g, 0, 0))],
        out_shape=[jax.ShapeDtypeStruct((batch, groups, ncp, LANES), BF16),
                   jax.ShapeDtypeStruct((batch, groups, HEAD_DIM, ncp), BF16)],
        compiler_params=_cparams(("parallel", "parallel"), 32),
        name="nsa_compress",
    )(p1, p1, pk, pv, w1k, w2k, w1v, w2v, kgain, *tables)


def _nsa_kernel(qi_tab, tile_tab, br_tab, q_ref, kc_ref, vct_ref, k_ref, vt_ref, gt_ref,
                dg_ref, ovt_ref, o_ref, acc_ref, oc_ref, sel_ref, imp_ref, sta_ref, stb_ref, cst_ref,
                cmax_ref, m_ref, qh_ref, *, tq, qt, nslc, nsel, nentries):
    g = pl.program_id(1)
    rheads = D_HEADS // D_KV_GROUPS
    seq = qt * tq
    tiny = float(np.finfo(np.float32).tiny)
    qcol = lax.broadcasted_iota(jnp.int32, (tq, tq), 1)
    krow = lax.broadcasted_iota(jnp.int32, (tq, tq), 0)
    bufs = (sta_ref, stb_ref)
    blocks_per_tile = tq // SLC_BLOCK
    nslab = -(-nslc // SUBLANES)
    sel_rows = nslab * SUBLANES

    def rows(tile):
        return pl.ds(pl.multiple_of(tile * tq, tq), tq)

    half_lane = lax.broadcasted_iota(jnp.int32, (1, LANES), 1) // HEAD_DIM

    def split_heads(t, c):
        for r in range(rheads):
            blk = q_ref[rows(t), (r // 2) * LANES:(r // 2 + 1) * LANES]
            qh_ref[rows(t), r * LANES:(r + 1) * LANES] = jnp.where(half_lane == r % 2, blk,
                                                                  jnp.zeros_like(blk))
        return c

    lax.fori_loop(0, qt, split_heads, 0)

    def heads_q(qi):
        return [qh_ref[rows(qi), r * LANES:(r + 1) * LANES] for r in range(rheads)]

    kc = kc_ref[0, 0]
    vct = vct_ref[0, 0]
    ncp = kc.shape[0]
    ovt = ovt_ref[...]

    def select_blocks(qi, c):
        q = heads_q(qi)
        cend = lax.broadcasted_iota(jnp.int32, (ncp, tq), 0) * CMP_STRIDE + (CMP_BLOCK - 1)
        cvalid = cend <= qi * tq + lax.broadcasted_iota(jnp.int32, (ncp, tq), 1)
        psum = jnp.zeros((ncp, tq), F32)
        cmax = [_scores_t(kc, q[r], cvalid, cst_ref, r) for r in range(rheads)]
        for r in range(rheads):
            m = jnp.where(jnp.isfinite(cmax[r]), cmax[r], 0.0)
            e = jnp.exp2(cst_ref[r] - m)
            p = e / jnp.maximum(jnp.sum(e, axis=0, keepdims=True), tiny)
            psum += p
            oc_ref[qi * rheads + r] = jnp.dot(vct, p.astype(BF16), preferred_element_type=F32)
        hi, lo = _split_bf16(psum)
        imp = (jnp.dot(ovt, hi, preferred_element_type=F32)
               + jnp.dot(ovt, lo, preferred_element_type=F32))
        blk = lax.broadcasted_iota(jnp.int32, (NSLC_PAD, tq), 0)
        cur = (qi * tq + lax.broadcasted_iota(jnp.int32, (NSLC_PAD, tq), 1)) // SLC_BLOCK
        imp = jnp.where(blk > cur, -jnp.inf, imp)
        forced = (blk == 0) | (blk == cur) | (blk == cur - 1)
        imp = jnp.where(forced, jnp.inf, imp)
        imp_ref[...] = imp
        slabs = [imp[s * SUBLANES:(s + 1) * SUBLANES, :] for s in range(nslab)]
        ranks = [jnp.zeros((SUBLANES, tq), F32) for _ in range(nslab)]
        sub = lax.broadcasted_iota(jnp.int32, (SUBLANES, tq), 0)
        for jp in range(nslc):
            row = imp_ref[jp:jp + 1, :]
            for s in range(nslab):
                if s * SUBLANES > jp:
                    ahead = jnp.where(row >= slabs[s], 1.0, 0.0)
                elif (s + 1) * SUBLANES - 1 < jp:
                    ahead = jnp.where(row > slabs[s], 1.0, 0.0)
                else:
                    ahead = jnp.where(sub > jp - s * SUBLANES, jnp.where(row >= slabs[s], 1.0, 0.0),
                                      jnp.where(row > slabs[s], 1.0, 0.0))
                ranks[s] = ranks[s] + ahead
        for s in range(nslab):
            start = pl.multiple_of(qi * sel_rows + s * SUBLANES, SUBLANES)
            sel_ref[pl.ds(start, SUBLANES), :] = jnp.where(ranks[s] < nsel, 1.0, 0.0)
        return c

    lax.fori_loop(0, qt, select_blocks, 0)
    acc_ref[...] = jnp.zeros_like(acc_ref)
    m_ref[...] = jnp.full(m_ref.shape, -jnp.inf, F32)

    def score_fn(e, buf):
        qi, tile, br = qi_tab[e], tile_tab[e], br_tab[e]
        q = heads_q(qi)
        k = k_ref[br, rows(tile), :]
        base = qi * sel_rows + tile * blocks_per_tile
        chosen = jnp.concatenate(
            [jnp.broadcast_to(sel_ref[pl.ds(base + c, 1), :], (SLC_BLOCK, tq))
             for c in range(blocks_per_tile)], axis=0)
        dpos = (qi - tile) * tq + (qcol - krow)
        reach = jnp.where(br == 0, seq, WINDOW).astype(jnp.uint32)
        kept = jnp.where(jnp.maximum(chosen, br.astype(F32)) > 0.5, dpos, -1)
        valid = kept.astype(jnp.uint32) < reach
        for r in range(rheads):
            row = buf * rheads + r
            cmax_ref[row:row + 1, :] = _scores_t(k, q[r], valid, bufs[buf], r)

    def accum_fn(e, buf):
        qi, tile, br = qi_tab[e], tile_tab[e], br_tab[e]
        vt = vt_ref[0, br, tile]
        for r in range(rheads):
            cm, sr = buf * rheads + r, (qi * 2 + br) * rheads + r
            m_ref[pl.ds(sr, 1), :] = _accumulate_t(bufs[buf], r, cmax_ref[cm:cm + 1, :],
                                                   m_ref[pl.ds(sr, 1), :], acc_ref, sr, vt, True)

    _pipelined_sweep(nentries, score_fn, accum_fn, unroll=8)

    def finish(qi, c):
        outs = []
        for r in range(rheads):
            row0 = (g * rheads + r) * 3
            o = [oc_ref[qi * rheads + r]]
            for br in range(2):
                sr = (qi * 2 + br) * rheads + r
                o.append(acc_ref[sr, 0:HEAD_DIM, :]
                         / jnp.maximum(acc_ref[sr, HEAD_DIM:HEAD_DIM + 1, :], tiny))
            outs.append(sum(gt_ref[0, qi, pl.ds(row0 + j, 1), :] * o[j] for j in range(3)))
        out = jnp.concatenate(outs, axis=0).T
        o_ref[rows(qi), :] = (out * jax.nn.silu(dg_ref[rows(qi), :])).astype(BF16)
        return c

    lax.fori_loop(0, qt, finish, 0)


def _nsa_attn(qe, kc, vct, kboth, vtboth, gt, p1, ovt, batch, seq, tq, dg_blk):
    n = qe.shape[0]
    groups = D_KV_GROUPS
    rheads = D_HEADS // groups
    width = rheads * HEAD_DIM
    qt = seq // tq
    ncp = kc.shape[2]
    nslc = seq // SLC_BLOCK
    sel_rows = -(-nslc // SUBLANES) * SUBLANES
    wtiles = (WINDOW + tq - 1) // tq
    entries = []
    for qi in range(qt):
        entries += [(qi, t, 0) for t in range(qi + 1)]
        entries += [(qi, t, 1) for t in range(max(qi - wtiles, 0), qi + 1)]
    tabs = [jnp.asarray([e[c] for e in entries], jnp.int32) for c in range(3)]
    kernel = functools.partial(_nsa_kernel, tq=tq, qt=qt, nslc=nslc, nsel=min(SLC_TOPK, nslc),
                               nentries=len(entries))
    return pl.pallas_call(
        kernel,
        grid_spec=pltpu.PrefetchScalarGridSpec(
            num_scalar_prefetch=3,
            grid=(batch, groups),
            in_specs=[pl.BlockSpec((seq, width), lambda b, g, *_: (b, g)),
                      pl.BlockSpec((1, 1, ncp, LANES), lambda b, g, *_: (b, g, 0, 0)),
                      pl.BlockSpec((1, 1, HEAD_DIM, ncp), lambda b, g, *_: (b, g, 0, 0)),
                      pl.BlockSpec((2, seq, LANES), lambda b, g, *_: (0, b, g)),
                      pl.BlockSpec((1, 2, qt, HEAD_DIM + ONES_ROWS, tq), lambda b, g, *_: (b, 0, 0, g, 0)),
                      pl.BlockSpec((1, qt, LANES, tq), lambda b, g, *_: (b, 0, 0, 0)),
                      pl.BlockSpec((seq, width), lambda b, g, *_: (b, dg_blk + g)),
                      pl.BlockSpec(ovt.shape, lambda b, g, *_: (0, 0))],
            out_specs=pl.BlockSpec((seq, width), lambda b, g, *_: (b, g)),
            scratch_shapes=[pltpu.VMEM((qt * 2 * rheads, HEAD_DIM + ONES_ROWS, tq), F32),
                            pltpu.VMEM((qt * rheads, HEAD_DIM, tq), F32),
                            pltpu.VMEM((qt * sel_rows, tq), F32),
                            pltpu.VMEM((NSLC_PAD, tq), F32),
                            pltpu.VMEM((rheads, tq, tq), F32),
                            pltpu.VMEM((rheads, tq, tq), F32),
                            pltpu.VMEM((rheads, ncp, tq), F32),
                            pltpu.VMEM((2 * rheads, tq), F32),
                            pltpu.VMEM((qt * 2 * rheads, tq), F32),
                            pltpu.VMEM((seq, rheads * LANES), BF16)]),
        out_shape=jax.ShapeDtypeStruct((n, groups * width), BF16),
        compiler_params=_cparams(("parallel", "parallel"), 56),
        name="nsa_attn",
    )(*tabs, qe, kc, vct, kboth, vtboth, gt, p1, ovt)


def _rope_tables(pos, d):
    rd = d // 4
    half = rd // 2
    inv = ROPE_THETA ** (-jnp.arange(half, dtype=F32) / half)
    ang = pos.astype(F32)[:, None] * inv[None, :]
    cos, sin = jnp.cos(ang), jnp.sin(ang)
    npos = pos.shape[0]
    zeros = jnp.zeros((npos, d), F32)
    c = jnp.concatenate([cos, cos, jnp.ones((npos, d - rd), F32)], axis=1)
    s1 = zeros.at[:, :half].set(-sin)
    s2 = zeros.at[:, half:rd].set(sin)
    return tuple(jnp.tile(t, (1, LANES // d)) for t in (c, s1, s2))


def _overlap_t(ncp, nslc):
    start = np.arange(ncp) * CMP_STRIDE
    s0 = np.arange(nslc) * SLC_BLOCK
    lo = np.maximum(start[:, None], s0[None, :])
    hi = np.minimum(start[:, None] + CMP_BLOCK, s0[None, :] + SLC_BLOCK)
    ov = np.clip(hi - lo, 0, None) / CMP_BLOCK
    ov[ncp - 1] = 0.0
    out = np.zeros((NSLC_PAD, ncp), np.float32)
    out[:nslc] = ov.T
    return jnp.asarray(out, BF16)


def kernel(x, mem, mem_norm, l0_norm, l0_w_in, l0_a_vnorm, l0_a_ws, l0_a_bs, l0_b_qnorm, l0_b_knorm, l0_b_lq1, l0_b_lk1, l0_b_lq2, l0_b_lk2, l0_b_subln, l0_m_wkv, l0_m_qnorm, l0_m_knorm, l0_w_out, l1_norm, l1_w_in, l1_c_conv_w, l1_c_conv_b, l1_c_norm, l1_d_qnorm, l1_d_knorm, l1_d_cmp_pos_k, l1_d_cmp_w1_k, l1_d_cmp_w2_k, l1_d_cmp_pos_v, l1_d_cmp_w1_v, l1_d_cmp_w2_v, l1_m_wkv, l1_m_qnorm, l1_m_knorm, l1_w_out):
    batch, seq, d_model = x.shape
    n = batch * seq
    tm = 256
    tq = 256
    mw = M_HEADS * HEAD_DIM
    pos = jnp.arange(seq, dtype=jnp.int32)
    x0 = x.reshape(n, d_model)

    wkv_both = jnp.concatenate([l0_m_wkv, l1_m_wkv], axis=1).astype(BF16)
    kgains = jnp.stack([jnp.tile(l0_m_knorm, M_HEADS), jnp.tile(l1_m_knorm, M_HEADS)])
    mem_kh, mem_vt = _mem_kv(mem, mem_norm, wkv_both, kgains)

    w = l0_w_in
    tab32 = _rope_tables(pos, B_QK_DIM)
    b_width = 2 * B_HEADS * B_QK_DIM
    a_width = l0_a_vnorm.shape[0]
    bs_exp = jnp.repeat(l0_a_bs.T, a_width // A_GROUPS, axis=1)
    p0, y_a, qn, kn, vt = _layer0_proj(
        x0, l0_norm, w[:, 0:1536].astype(BF16), w[:, 1536:3072].astype(BF16), w[:, 3072:4096].astype(BF16),
        tab32, jnp.tile(l0_b_qnorm, b_width // B_QK_DIM).reshape(1, b_width),
        jnp.tile(l0_b_knorm, b_width // B_QK_DIM).reshape(1, b_width),
        l0_a_ws, bs_exp, l0_a_vnorm.reshape(1, a_width), batch, seq, tq)
    lam_init = 0.8 - 0.6 * math.exp(-0.3 * 1)
    lam_p = jnp.stack([l0_b_lq1, l0_b_lk1, l0_b_lq2, l0_b_lk2])
    y_b = _diff_attn(lam_p, qn, kn, vt, p0, jnp.tile(l0_b_subln, 2).reshape(1, LANES),
                     batch, seq, tq, lam_init, 0)
    y_m = _mem_attn(p0, mem_kh, mem_vt, 0, jnp.tile(l0_m_qnorm, M_HEADS).reshape(1, mw),
                    512 // mw, 768 // mw, seq, 2 * tq)

    w = l1_w_in
    n_bg = 3 * D_HEADS
    w_rest = jnp.concatenate([w[:, 2840:3352], w[:, 2048:2304], w[:, 3352:3864]], axis=1)
    w_d = jnp.concatenate([w[:, 1536:2048], w[:, 2304:2816], w[:, 2816:2840],
                           jnp.zeros((d_model, LANES - n_bg), F32)], axis=1)
    groups, rheads = D_KV_GROUPS, D_HEADS // D_KV_GROUPS
    d_width = D_HEADS * HEAD_DIM
    tab64 = _rope_tables(pos, HEAD_DIM)
    x1, p1, y_c, qe, kboth, vtboth, gt = _layer1_proj(
        x0, y_a, y_b, y_m, l0_w_out.astype(BF16),
        l1_norm, w[:, 0:1536].astype(BF16), w_d.astype(BF16), w_rest.astype(BF16), tab64,
        jnp.tile(l1_d_qnorm, D_HEADS).reshape(1, d_width),
        jnp.tile(l1_d_knorm, LANES // HEAD_DIM).reshape(1, LANES),
        l1_c_conv_w, l1_c_conv_b, l1_c_norm, batch, seq, tq)
    kv_blk = 512 // LANES

    ncp = seq // CMP_STRIDE
    cmp_pos = jnp.arange(ncp, dtype=jnp.int32) * CMP_STRIDE + (CMP_BLOCK - 1)
    hidden = l1_d_cmp_w2_k.shape[0]

    def group_w1(w1):
        w = w1.reshape(CMP_BLOCK, HEAD_DIM, hidden)
        return jnp.stack([jnp.zeros((CMP_BLOCK, LANES, hidden), F32)
                          .at[:, g * HEAD_DIM:(g + 1) * HEAD_DIM, :].set(w) for g in range(groups)])

    w2k = jnp.tile(l1_d_cmp_w2_k, (1, LANES // HEAD_DIM))
    w2v = jnp.zeros((hidden, LANES), F32).at[:, :HEAD_DIM].set(l1_d_cmp_w2_v)
    kc, vct = _compress(
        p1, kv_blk, jnp.tile(l1_d_cmp_pos_k, (1, groups)), jnp.tile(l1_d_cmp_pos_v, (1, groups)),
        group_w1(l1_d_cmp_w1_k).astype(BF16), w2k.astype(BF16),
        group_w1(l1_d_cmp_w1_v).astype(BF16), w2v.astype(BF16),
        jnp.tile(l1_d_knorm, LANES // HEAD_DIM).reshape(1, LANES), _rope_tables(cmp_pos, HEAD_DIM),
        batch, seq)

    nslc = seq // SLC_BLOCK
    y_d = _nsa_attn(qe, kc, vct, kboth, vtboth, gt, p1, _overlap_t(ncp, nslc),
                    batch, seq, tq, 0)
    y_m1 = _mem_attn(p1, mem_kh, mem_vt, 1, jnp.tile(l1_m_qnorm, M_HEADS).reshape(1, mw),
                     768 // mw, 1024 // mw, seq, 2 * tq)
    out = _out_proj(x1, y_c, y_d, y_m1, l1_w_out.astype(BF16), tm)
    return out.reshape(batch, seq, d_model)
```

```python
import functools
import math

import numpy as np
import jax
import jax.numpy as jnp
from jax import lax
from jax.experimental import pallas as pl
from jax.experimental.pallas import tpu as pltpu

F32 = jnp.float32
BF16 = jnp.bfloat16

EPS = 1e-6
ROPE_THETA = 500000.0
HEAD_DIM = 64
A_CHUNK = 128
A_GROUPS = 4
B_HEADS = 8
B_QK_DIM = 32
C_KERNEL = 31
D_HEADS = 8
D_KV_GROUPS = 2
CMP_BLOCK = 32
CMP_STRIDE = 16
SLC_BLOCK = 64
SLC_TOPK = 16
WINDOW = 512
M_HEADS = 4

LANES = 128
SUBLANES = 8
MXU_DIM = 256
HALO = 32
NSLC_PAD = 128
ONES_ROWS = 16

NT_DIMS = (((1,), (1,)), ((), ()))
LOG2E = math.log2(math.e)


def _cparams(semantics, vmem_mb):
    return pltpu.CompilerParams(dimension_semantics=semantics,
                                vmem_limit_bytes=vmem_mb * 1024 * 1024)


def _split_bf16(x):
    hi = x.astype(BF16)
    lo = (x - hi.astype(F32)).astype(BF16)
    return hi, lo


def _group_mean(x2, gsize):
    t, c = x2.shape
    w = min(c, MXU_DIM)
    r = lax.broadcasted_iota(jnp.int32, (w, w), 0) // gsize
    cc = lax.broadcasted_iota(jnp.int32, (w, w), 1) // gsize
    ones = jnp.where(r == cc, 1.0, 0.0).astype(BF16)
    outs = []
    for s in range(c // w):
        hi, lo = _split_bf16(x2[:, s * w:(s + 1) * w])
        outs.append(jnp.dot(hi, ones, preferred_element_type=F32)
                    + jnp.dot(lo, ones, preferred_element_type=F32))
    out = outs[0] if len(outs) == 1 else jnp.concatenate(outs, axis=1)
    return out * (1.0 / gsize)


def _tile_lanes(t, width):
    rep = width // t.shape[1]
    return t if rep == 1 else jnp.concatenate([t] * rep, axis=1)


def _rope(x, cos_t, s1_t, s2_t, half):
    width = x.shape[1]
    up = pltpu.roll(x, width - half, 1)
    dn = pltpu.roll(x, half, 1)
    return (x * _tile_lanes(cos_t, width) + up * _tile_lanes(s1_t, width)
            + dn * _tile_lanes(s2_t, width))


def _matmul_cols(h, w_ref, lo, hi, out_ref=None):
    r = jnp.dot(h, w_ref[:, lo:hi], preferred_element_type=F32)
    if out_ref is None:
        return r
    out_ref[:, lo:hi] = r


def _rms_rows(x, ng_ref):
    ms = jnp.mean(x * x, axis=-1, keepdims=True)
    return (x * lax.rsqrt(ms + EPS) * ng_ref[...]).astype(BF16)


def _scores_t(k, q, valid, st_ref, idx):
    st = lax.dot_general(k, q, NT_DIMS, preferred_element_type=F32)
    if valid is not None:
        st = jnp.where(valid, st, -jnp.inf)
    st_ref[idx] = st
    return jnp.max(st, axis=0, keepdims=True)


def _accumulate_t(st_ref, idx, cmax, m, acc_ref, aidx, vt, guard_empty):
    m_new = jnp.maximum(m, cmax)
    m_use = jnp.where(m_new == -jnp.inf, 0.0, m_new) if guard_empty else m_new
    alpha = jnp.exp2(m - m_use)
    pt = jnp.exp2((st_ref[idx] - m_use).astype(BF16))
    acc_ref[aidx] = alpha * acc_ref[aidx] + jnp.dot(vt, pt, preferred_element_type=F32)
    return m_new


def _pipelined_sweep(n, score_fn, accum_fn, unroll=4):
    score_fn(0, 0)
    stages = n - 1

    def stage(e, parity):
        score_fn(e + 1, 1 - parity)
        accum_fn(e, parity)

    def body(i, c):
        for k in range(unroll):
            stage(unroll * i + k, k % 2)
        return c

    if isinstance(n, int):
        iters, left = divmod(stages, unroll)
        if iters:
            lax.fori_loop(0, iters, body, 0)
        for k in range(left):
            stage(iters * unroll + k, k % 2)
        accum_fn(n - 1, left % 2)
        return
    lax.fori_loop(0, stages // unroll, body, 0)
    base = stages // unroll * unroll
    left = stages - base
    for k in range(unroll - 1):
        pl.when(left > k)(functools.partial(stage, base + k, k % 2))
    for parity in range(2):
        pl.when(left % 2 == parity)(functools.partial(accum_fn, n - 1, parity))


def _mix_out(ya_ref, yb_ref, ym_ref, w_ref):
    wa = ya_ref.shape[1]
    wb = yb_ref.shape[1]
    acc = jnp.dot(ya_ref[...], w_ref[0:wa, :], preferred_element_type=F32)
    acc += jnp.dot(yb_ref[...], w_ref[wa:wa + wb, :], preferred_element_type=F32)
    acc += jnp.dot(ym_ref[...], w_ref[wa + wb:, :], preferred_element_type=F32)
    return acc


def _sgu_group(u, v, gate, w_ref, bs, vgain, o_ref, cols):
    tm = u.shape[0]
    row = lax.broadcasted_iota(jnp.int32, (A_CHUNK, A_CHUNK), 0)
    col = lax.broadcasted_iota(jnp.int32, (A_CHUNK, A_CHUNK), 1)
    w = jnp.where(col <= row, w_ref, 0.0).astype(BF16)
    gv = jax.nn.gelu(v)
    ms = jnp.mean(gv * gv, axis=-1, keepdims=True)
    vn = (gv * lax.rsqrt(ms + EPS) * vgain).astype(BF16)
    gu = jax.nn.gelu(u) * jax.nn.silu(gate)
    for c in range(tm // A_CHUNK):
        rows = slice(c * A_CHUNK, (c + 1) * A_CHUNK)
        z = jnp.dot(w, vn[rows, :], preferred_element_type=F32) + bs
        o_ref[rows, cols] = (gu[rows, :] * z).astype(BF16)


def _layer0_proj_kernel(x_ref, ng_ref, wa_ref, wqkv_ref, w_ref, c_ref, s1_ref, s2_ref, qg_ref, kg_ref,
                        ws_ref, bs_ref, vg_ref, p_ref, ya_ref, qo_ref, ko_ref, vto_ref):
    h = _rms_rows(x_ref[...], ng_ref)
    aw = vg_ref.shape[1]
    gdim = aw // A_GROUPS
    width = qg_ref.shape[1]
    half = B_QK_DIM // 4 // 2
    pw = p_ref.shape[1]

    a = _matmul_cols(h, wa_ref, 0, 3 * aw)

    def sgu(g):
        cols = slice(g * gdim, (g + 1) * gdim)
        _sgu_group(a[:, cols], a[:, aw + g * gdim:aw + (g + 1) * gdim],
                   a[:, 2 * aw + g * gdim:2 * aw + (g + 1) * gdim],
                   ws_ref[g], bs_ref[:, cols], vg_ref[:, cols], ya_ref, cols)

    def prep(t, gain, dst, mul):
        tn = t * lax.rsqrt(_group_mean(t * t, B_QK_DIM) + EPS) * gain[...]
        tr = _rope(tn, c_ref[...], s1_ref[...], s2_ref[...], half)
        dst[...] = (tr * mul if mul != 1.0 else tr).astype(BF16)

    q = _matmul_cols(h, wqkv_ref, 0, width)
    sgu(0)
    k = _matmul_cols(h, wqkv_ref, width, 2 * width)
    sgu(1)
    v = _matmul_cols(h, wqkv_ref, 2 * width, 3 * width)
    sgu(2)
    _matmul_cols(h, w_ref, 0, pw // 2, p_ref)
    sgu(3)
    prep(q, qg_ref, qo_ref, B_QK_DIM ** -0.5 * LOG2E)
    _matmul_cols(h, w_ref, pw // 2, pw, p_ref)
    prep(k, kg_ref, ko_ref, 1.0)
    vt = v.T
    dv = 2 * B_QK_DIM
    ones = jnp.ones((ONES_ROWS, vt.shape[1]), F32)
    parts = []
    for hd in range(B_HEADS):
        parts += [vt[hd * dv:(hd + 1) * dv, :], ones]
    vto_ref[0, 0] = jnp.concatenate(parts, axis=0).astype(BF16)


def _layer0_proj(x, norm_g, w_a, w_qkv, w_rest, tables, qgain, kgain, a_ws, bs_exp, vgain, batch, seq, tm):
    n, d = x.shape
    width = qgain.shape[1]
    aw = vgain.shape[1]
    tiles_per_seq = seq // tm
    tab_spec = pl.BlockSpec((tm, LANES), lambda i: (i % tiles_per_seq, 0))
    vrows = B_HEADS * (2 * B_QK_DIM + ONES_ROWS)

    def full(a):
        return pl.BlockSpec(a.shape, lambda i: (0,) * a.ndim)

    def rows(w):
        return pl.BlockSpec((tm, w), lambda i: (i, 0))

    return pl.pallas_call(
        _layer0_proj_kernel,
        grid=(n // tm,),
        in_specs=[rows(d), pl.BlockSpec((1, d), lambda i: (0, 0)), full(w_a), full(w_qkv), full(w_rest),
                  tab_spec, tab_spec, tab_spec, full(qgain), full(kgain),
                  full(a_ws), full(bs_exp), full(vgain)],
        out_specs=[rows(w_rest.shape[1]), rows(aw), rows(width), rows(width),
                   pl.BlockSpec((1, 1, vrows, tm),
                                lambda i: (i // tiles_per_seq, i % tiles_per_seq, 0, 0))],
        out_shape=[jax.ShapeDtypeStruct((n, w_rest.shape[1]), F32),
                   jax.ShapeDtypeStruct((n, aw), BF16),
                   jax.ShapeDtypeStruct((n, width), BF16), jax.ShapeDtypeStruct((n, width), BF16),
                   jax.ShapeDtypeStruct((batch, tiles_per_seq, vrows, tm), BF16)],
        compiler_params=_cparams(("parallel",), 48),
        name="layer0_proj",
    )(x, norm_g.reshape(1, d), w_a, w_qkv, w_rest, *tables, qgain, kgain, a_ws, bs_exp, vgain)


def _diff_attn_kernel(qi_tab, tile_tab, lam_ref, q_ref, k_ref, vt_ref, g_ref, sub_ref, o_ref,
                      acc_ref, sta_ref, stb_ref, cmax_ref, m_ref, *, tq, qt, lam_init):
    dv = 2 * B_QK_DIM
    vrows = dv + ONES_ROWS
    nmap = 4
    lane = lax.broadcasted_iota(jnp.int32, (1, LANES), 1)
    bufs = (sta_ref, stb_ref)
    acc_ref[...] = jnp.zeros_like(acc_ref)
    m_ref[...] = jnp.full(m_ref.shape, -jnp.inf, F32)
    krow = lax.broadcasted_iota(jnp.int32, (tq, tq), 0)
    qcol = lax.broadcasted_iota(jnp.int32, (tq, tq), 1)
    lam_p = lam_ref[...]
    lam = (jnp.exp(jnp.sum(lam_p[0:1] * lam_p[1:2], axis=-1, keepdims=True))
           - jnp.exp(jnp.sum(lam_p[2:3] * lam_p[3:4], axis=-1, keepdims=True)) + lam_init)

    def rows(tile):
        return pl.ds(pl.multiple_of(tile * tq, tq), tq)

    def scores(qi, tile, buf, valid):
        q = q_ref[rows(qi), :]
        k = k_ref[rows(tile), :]
        zero = jnp.zeros_like(q)
        for j in range(nmap):
            qm = jnp.where(lane // B_QK_DIM == j, q, zero)
            row = buf * nmap + j
            cmax_ref[row:row + 1, :] = _scores_t(k, qm, valid, bufs[buf], j)

    def accumulate(qi, tile, buf):
        for j in range(nmap):
            vt = vt_ref[0, tile, (j // 2) * vrows:(j // 2 + 1) * vrows, :]
            cm, sr = buf * nmap + j, qi * nmap + j
            m_ref[pl.ds(sr, 1), :] = _accumulate_t(bufs[buf], j, cmax_ref[cm:cm + 1, :],
                                                   m_ref[pl.ds(sr, 1), :], acc_ref, sr, vt, False)

    def finish(qi):
        halves = []
        for h in range(2):
            o = [acc_ref[qi * nmap + 2 * h + mp, 0:dv, :] / acc_ref[qi * nmap + 2 * h + mp, dv:dv + 1, :]
                 for mp in range(2)]
            pd = o[0] - lam * o[1]
            ms2 = jnp.mean(pd * pd, axis=0, keepdims=True)
            halves.append(pd * lax.rsqrt(ms2 + EPS))
        ob = jnp.concatenate(halves, axis=0).T * sub_ref[...] * (1.0 - lam_init)
        o_ref[rows(qi), :] = (ob * jax.nn.silu(g_ref[rows(qi), :])).astype(BF16)

    def below_score(e, buf):
        scores(qi_tab[e], tile_tab[e], buf, None)

    def below_accum(e, buf):
        accumulate(qi_tab[e], tile_tab[e], buf)

    def diag_score(e, buf):
        scores(e, e, buf, krow <= qcol)

    def diag_accum(e, buf):
        accumulate(e, e, buf)
        finish(e)

    _pipelined_sweep(qt * (qt - 1) // 2, below_score, below_accum, unroll=8)
    _pipelined_sweep(qt, diag_score, diag_accum)


def _diff_attn(lam_p, qn, kn, vt, p0, subln, batch, seq, tq, lam_init, g_off):
    n, width = qn.shape
    pairs = width // LANES
    qt = seq // tq
    vrows = 2 * B_QK_DIM + ONES_ROWS
    below = [(qi, t) for qi in range(qt) for t in range(qi)]
    qi_tab = jnp.asarray([e[0] for e in below], jnp.int32)
    tile_tab = jnp.asarray([e[1] for e in below], jnp.int32)
    kernel = functools.partial(_diff_attn_kernel, tq=tq, qt=qt, lam_init=lam_init)
    seq_spec = pl.BlockSpec((seq, LANES), lambda b, p, *_: (b, p))
    return pl.pallas_call(
        kernel,
        grid_spec=pltpu.PrefetchScalarGridSpec(
            num_scalar_prefetch=2,
            grid=(batch, pairs),
            in_specs=[pl.BlockSpec(lam_p.shape, lambda b, p, *_: (0, 0)),
                      seq_spec, seq_spec,
                      pl.BlockSpec((1, qt, 2 * vrows, tq), lambda b, p, *_: (b, 0, p, 0)),
                      pl.BlockSpec((seq, LANES), lambda b, p, *_: (b, g_off + p)),
                      pl.BlockSpec((1, LANES), lambda b, p, *_: (0, 0))],
            out_specs=seq_spec,
            scratch_shapes=[pltpu.VMEM((qt * 4, vrows, tq), F32), pltpu.VMEM((4, tq, tq), F32),
                            pltpu.VMEM((4, tq, tq), F32), pltpu.VMEM((8, tq), F32),
                            pltpu.VMEM((qt * 4, tq), F32)]),
        out_shape=jax.ShapeDtypeStruct((n, width), BF16),
        compiler_params=_cparams(("parallel", "parallel"), 40),
        name="diff_attn",
    )(qi_tab, tile_tab, lam_p, qn, kn, vt, p0, subln)


def _mem_kv_kernel(mem_ref, g_ref, w_ref, kg_ref, kh_ref, vt_ref):
    x = mem_ref[0]
    ms = jnp.mean(x * x, axis=-1, keepdims=True)
    h = (x * lax.rsqrt(ms + EPS) * g_ref[...]).astype(BF16)
    kv = jnp.dot(h, w_ref[...], preferred_element_type=F32)
    mw = kv.shape[1] // 4
    lane = lax.broadcasted_iota(jnp.int32, (1, mw), 1)
    ones = jnp.ones((ONES_ROWS, kv.shape[0]), F32)
    for layer in range(2):
        k = kv[:, (2 * layer) * mw:(2 * layer + 1) * mw]
        kn = k * lax.rsqrt(_group_mean(k * k, HEAD_DIM) + EPS) * kg_ref[layer:layer + 1, :]
        vt = kv[:, (2 * layer + 1) * mw:(2 * layer + 2) * mw].T
        parts = []
        for hd in range(M_HEADS):
            kh_ref[layer, 0, hd] = jnp.where(lane // HEAD_DIM == hd, kn, 0.0).astype(BF16)
            parts += [vt[hd * HEAD_DIM:(hd + 1) * HEAD_DIM, :], ones]
        vt_ref[layer, 0] = jnp.concatenate(parts, axis=0).astype(BF16)


def _mem_kv(mem, mem_norm, wkv_both, kgains):
    batch, mtok, d = mem.shape
    mw = wkv_both.shape[1] // 4
    vrows = M_HEADS * (HEAD_DIM + ONES_ROWS)
    return pl.pallas_call(
        _mem_kv_kernel,
        grid=(batch,),
        in_specs=[pl.BlockSpec((1, mtok, d), lambda b: (b, 0, 0)),
                  pl.BlockSpec((1, d), lambda b: (0, 0)),
                  pl.BlockSpec(wkv_both.shape, lambda b: (0, 0)),
                  pl.BlockSpec(kgains.shape, lambda b: (0, 0))],
        out_specs=[pl.BlockSpec((2, 1, M_HEADS, mtok, mw), lambda b: (0, b, 0, 0, 0)),
                   pl.BlockSpec((2, 1, vrows, mtok), lambda b: (0, b, 0, 0))],
        out_shape=[jax.ShapeDtypeStruct((2, batch, M_HEADS, mtok, mw), BF16),
                   jax.ShapeDtypeStruct((2, batch, vrows, mtok), BF16)],
        compiler_params=_cparams(("parallel",), 32),
        name="mem_kv",
    )(mem, mem_norm.reshape(1, d), wkv_both, kgains)


def _mem_attn_tile(q_ref, g_ref, kh_ref, vt_ref, qg_ref, st_ref):
    vrows = HEAD_DIM + ONES_ROWS
    x = q_ref[...]
    qn = (x * lax.rsqrt(_group_mean(x * x, HEAD_DIM) + EPS) * qg_ref[...]
          * (HEAD_DIM ** -0.5 * LOG2E)).astype(BF16)
    cmax = [_scores_t(kh_ref[0, 0, h], qn, None, st_ref, h) for h in range(M_HEADS)]
    outs = []
    for h in range(M_HEADS):
        pt = jnp.exp2((st_ref[h] - cmax[h]).astype(BF16))
        acc = jnp.dot(vt_ref[0, 0, h * vrows:(h + 1) * vrows, :], pt, preferred_element_type=F32)
        outs.append(acc[0:HEAD_DIM, :] / acc[HEAD_DIM:HEAD_DIM + 1, :])
    out = jnp.concatenate(outs, axis=0).T
    return (out * jax.nn.silu(g_ref[...])).astype(BF16)


def _mem_attn_kernel(q_ref, g_ref, kh_ref, vt_ref, qg_ref, o_ref, st_ref):
    o_ref[...] = _mem_attn_tile(q_ref, g_ref, kh_ref, vt_ref, qg_ref, st_ref)


def _mem_out_kernel(x_ref, ya_ref, yb_ref, q_ref, g_ref, kh_ref, vt_ref, qg_ref, w_ref, o_ref, st_ref):
    ym = _mem_attn_tile(q_ref, g_ref, kh_ref, vt_ref, qg_ref, st_ref)
    wa = ya_ref.shape[1]
    wb = yb_ref.shape[1]
    acc = jnp.dot(ya_ref[...], w_ref[0:wa, :], preferred_element_type=F32)
    acc += jnp.dot(yb_ref[...], w_ref[wa:wa + wb, :], preferred_element_type=F32)
    acc += jnp.dot(ym, w_ref[wa + wb:, :], preferred_element_type=F32)
    o_ref[...] = x_ref[...] + acc


def _mem_out(x, ya, yb, p, kh, vt, layer, qgain, q_blk, g_blk, w, seq, tq):
    n, d = x.shape
    _, batch, heads, mtok, mw = kh.shape
    qt = seq // tq

    def rows(width, blk=0):
        return pl.BlockSpec((tq, width), lambda i: (i, blk))

    return pl.pallas_call(
        _mem_out_kernel,
        grid=(n // tq,),
        in_specs=[rows(d), rows(ya.shape[1]), rows(yb.shape[1]), rows(mw, q_blk), rows(mw, g_blk),
                  pl.BlockSpec((1, 1, heads, mtok, mw), lambda i: (layer, i // qt, 0, 0, 0)),
                  pl.BlockSpec((1, 1, vt.shape[2], mtok), lambda i: (layer, i // qt, 0, 0)),
                  pl.BlockSpec((1, mw), lambda i: (0, 0)),
                  pl.BlockSpec(w.shape, lambda i: (0, 0))],
        out_specs=rows(d),
        out_shape=jax.ShapeDtypeStruct((n, d), F32),
        scratch_shapes=[pltpu.VMEM((heads, mtok, tq), F32)],
        compiler_params=_cparams(("parallel",), 40),
        name="mem_out_proj",
    )(x, ya, yb, p, p, kh, vt, qgain, w)


def _mem_attn(p, kh, vt, layer, qgain, q_blk, g_blk, seq, tq):
    n = p.shape[0]
    _, batch, heads, mtok, mw = kh.shape
    qt = seq // tq
    return pl.pallas_call(
        _mem_attn_kernel,
        grid=(n // tq,),
        in_specs=[pl.BlockSpec((tq, mw), lambda i: (i, q_blk)),
                  pl.BlockSpec((tq, mw), lambda i: (i, g_blk)),
                  pl.BlockSpec((1, 1, heads, mtok, mw), lambda i: (layer, i // qt, 0, 0, 0)),
                  pl.BlockSpec((1, 1, vt.shape[2], mtok), lambda i: (layer, i // qt, 0, 0)),
                  pl.BlockSpec((1, mw), lambda i: (0, 0))],
        out_specs=pl.BlockSpec((tq, mw), lambda i: (i, 0)),
        out_shape=jax.ShapeDtypeStruct((n, mw), BF16),
        scratch_shapes=[pltpu.VMEM((heads, mtok, tq), F32)],
        compiler_params=_cparams(("parallel",), 32),
        name="mem_attn",
    )(p, p, kh, vt, qgain)


def _layer1_proj_kernel(x_ref, ya_ref, yb_ref, ym_ref, wo_ref, ng_ref, wc_ref, wd_ref, w_ref,
                        c_ref, s1_ref, s2_ref, qg_ref, kg_ref, cw_ref, cb_ref, cn_ref,
                        x1_ref, p_ref, yc_ref, qo_ref, ko_ref, vto_ref, gto_ref,
                        hbuf_ref, shift_ref, *, tiles_per_seq):
    i = pl.program_id(0)
    ts = x_ref.shape[0]
    span = ts + HALO
    cwid = cn_ref.shape[1]
    qw = qg_ref.shape[1]
    pw = p_ref.shape[1]
    half = HEAD_DIM // 4 // 2
    x1 = x_ref[...] + _mix_out(ya_ref, yb_ref, ym_ref, wo_ref)
    x1_ref[...] = x1
    h = _rms_rows(x1, ng_ref)

    @pl.when(i % tiles_per_seq == 0)
    def _():
        hbuf_ref[0:HALO, :] = jnp.zeros((HALO, cwid), F32)

    ca = _matmul_cols(h, wc_ref, 0, cwid)
    cbv = _matmul_cols(h, wc_ref, cwid, 2 * cwid)
    q = _matmul_cols(h, wd_ref, 0, qw)
    hbuf_ref[HALO:HALO + ts, :] = ca * jax.nn.sigmoid(cbv)
    for ph in range(SUBLANES):
        shift_ref[ph, 0:span - ph, :] = hbuf_ref[ph:span, :]
    cg = _matmul_cols(h, wc_ref, 2 * cwid, 3 * cwid)
    d = _matmul_cols(h, wd_ref, qw, wd_ref.shape[1])
    first = HALO - (C_KERNEL - 1)
    y = cb_ref[...]
    for j in range(C_KERNEL):
        ph, base = (first + j) % SUBLANES, (first + j) // SUBLANES * SUBLANES
        y = y + shift_ref[ph, base:base + ts, :] * cw_ref[j:j + 1, :]
    _matmul_cols(h, w_ref, 0, pw // 2, p_ref)
    ms = jnp.mean(y * y, axis=-1, keepdims=True)
    yn = y * lax.rsqrt(ms + EPS) * cn_ref[...]
    yc_ref[...] = (jax.nn.silu(yn) * jax.nn.silu(cg)).astype(BF16)
    hbuf_ref[0:HALO, :] = hbuf_ref[ts:ts + HALO, :]
    _matmul_cols(h, w_ref, pw // 2, pw, p_ref)

    tabs = (c_ref[...], s1_ref[...], s2_ref[...])
    tn = q * lax.rsqrt(_group_mean(q * q, HEAD_DIM) + EPS) * qg_ref[...]
    qo_ref[...] = (_rope(tn, *tabs, half) * (HEAD_DIM ** -0.5 * LOG2E)).astype(BF16)
    ones = jnp.ones((ONES_ROWS, ts), F32)
    lane = lax.broadcasted_iota(jnp.int32, (1, LANES), 1)
    for br in range(2):
        k = d[:, 2 * br * LANES:(2 * br + 1) * LANES]
        kn = k * lax.rsqrt(_group_mean(k * k, HEAD_DIM) + EPS) * kg_ref[...]
        kr = _rope(kn, *tabs, half)
        swapped = pltpu.roll(kr, HEAD_DIM, 1)
        for g in range(D_KV_GROUPS):
            own = (lane // HEAD_DIM) == g
            ko_ref[br, :, g * LANES:(g + 1) * LANES] = jnp.where(own, kr, swapped).astype(BF16)
        vt = d[:, (2 * br + 1) * LANES:(2 * br + 2) * LANES].T
        parts = []
        for g in range(D_KV_GROUPS):
            parts += [vt[g * HEAD_DIM:(g + 1) * HEAD_DIM, :], ones]
        vto_ref[0, br, 0] = jnp.concatenate(parts, axis=0).astype(BF16)
    gto_ref[0, 0] = jax.nn.sigmoid(d[:, 4 * LANES:5 * LANES]).T


def _layer1_proj(x, ya, yb, ym, w_out, norm_g, w_c, w_d, w_rest, tables, qgain, kgain,
                 conv_w, conv_b, conv_norm, batch, seq, tm):
    n, dm = x.shape
    cwid = conv_w.shape[1]
    tiles_per_seq = seq // tm
    tab_spec = pl.BlockSpec((tm, LANES), lambda i: (i % tiles_per_seq, 0))
    vrows = D_KV_GROUPS * (HEAD_DIM + ONES_ROWS)

    def full(a):
        return pl.BlockSpec(a.shape, lambda i: (0,) * a.ndim)

    def rows(w):
        return pl.BlockSpec((tm, w), lambda i: (i, 0))

    kernel = functools.partial(_layer1_proj_kernel, tiles_per_seq=tiles_per_seq)
    return pl.pallas_call(
        kernel,
        grid=(n // tm,),
        in_specs=[rows(dm), rows(ya.shape[1]), rows(yb.shape[1]), rows(ym.shape[1]), full(w_out),
                  pl.BlockSpec((1, dm), lambda i: (0, 0)), full(w_c), full(w_d), full(w_rest),
                  tab_spec, tab_spec, tab_spec, full(qgain), full(kgain),
                  full(conv_w), pl.BlockSpec((1, cwid), lambda i: (0, 0)),
                  pl.BlockSpec((1, cwid), lambda i: (0, 0))],
        out_specs=[rows(dm), rows(w_rest.shape[1]), rows(cwid), rows(qgain.shape[1]),
                   pl.BlockSpec((2, tm, D_KV_GROUPS * LANES), lambda i: (0, i, 0)),
                   pl.BlockSpec((1, 2, 1, vrows, tm),
                                lambda i: (i // tiles_per_seq, 0, i % tiles_per_seq, 0, 0)),
                   pl.BlockSpec((1, 1, LANES, tm),
                                lambda i: (i // tiles_per_seq, i % tiles_per_seq, 0, 0))],
        out_shape=[jax.ShapeDtypeStruct((n, dm), F32),
                   jax.ShapeDtypeStruct((n, w_rest.shape[1]), F32),
                   jax.ShapeDtypeStruct((n, cwid), BF16),
                   jax.ShapeDtypeStruct((n, qgain.shape[1]), BF16),
                   jax.ShapeDtypeStruct((2, n, D_KV_GROUPS * LANES), BF16),
                   jax.ShapeDtypeStruct((batch, 2, tiles_per_seq, vrows, tm), BF16),
                   jax.ShapeDtypeStruct((batch, tiles_per_seq, LANES, tm), F32)],
        scratch_shapes=[pltpu.VMEM((tm + HALO, cwid), F32),
                        pltpu.VMEM((SUBLANES, tm + HALO, cwid), F32)],
        compiler_params=_cparams(("arbitrary",), 52),
        name="layer1_proj",
    )(x, ya, yb, ym, w_out, norm_g.reshape(1, dm), w_c, w_d, w_rest, *tables, qgain, kgain, conv_w,
      conv_b.reshape(1, cwid), conv_norm.reshape(1, cwid))


def _compress_kernel(tk_ref, tv_ref, pk_ref, pv_ref, w1k_ref, w2k_ref, w1v_ref, w2v_ref,
                     kg_ref, c_ref, s1_ref, s2_ref, kc_ref, vct_ref):
    half = HEAD_DIM // 4 // 2
    ncp = kc_ref.shape[2]

    def mlp(t_ref, p_ref, w1_ref, w2):
        a = jnp.zeros((ncp, w1_ref.shape[3]), F32)
        b = jnp.zeros((ncp, w1_ref.shape[3]), F32)
        for l in range(CMP_STRIDE):
            x = t_ref[pl.ds(l, ncp, stride=CMP_STRIDE), :]
            a += jnp.dot((x + p_ref[l:l + 1, :]).astype(BF16), w1_ref[0, l],
                         preferred_element_type=F32)
            b += jnp.dot((x + p_ref[CMP_STRIDE + l:CMP_STRIDE + l + 1, :]).astype(BF16),
                         w1_ref[0, CMP_STRIDE + l], preferred_element_type=F32)
        h = a + pltpu.roll(b, ncp - 1, 0)
        return jnp.dot(jax.nn.silu(h).astype(BF16), w2, preferred_element_type=F32)

    kc = mlp(tk_ref, pk_ref, w1k_ref, w2k_ref[...])
    kn = kc * lax.rsqrt(_group_mean(kc * kc, HEAD_DIM) + EPS) * kg_ref[...]
    kc_ref[0, 0] = _rope(kn, c_ref[...], s1_ref[...], s2_ref[...], half).astype(BF16)
    vc = mlp(tv_ref, pv_ref, w1v_ref, w2v_ref[...])
    vct_ref[0, 0] = vc.T[0:HEAD_DIM, :].astype(BF16)


def _compress(p1, k_blk, pk, pv, w1k, w2k, w1v, w2v, kgain, tables, batch, seq):
    groups = w1k.shape[0]
    ncp = seq // CMP_STRIDE

    def full(a):
        return pl.BlockSpec(a.shape, lambda b, g: (0,) * a.ndim)

    def per_group(a):
        return pl.BlockSpec((1,) + a.shape[1:], lambda b, g: (g,) + (0,) * (a.ndim - 1))

    return pl.pallas_call(
        _compress_kernel,
        grid=(batch, groups),
        in_specs=[pl.BlockSpec((seq, LANES), lambda b, g: (b, k_blk)),
                  pl.BlockSpec((seq, LANES), lambda b, g: (b, k_blk + 1)),
                  full(pk), full(pv), per_group(w1k), full(w2k), per_group(w1v), full(w2v),
                  full(kgain), full(tables[0]), full(tables[1]), full(tables[2])],
        out_specs=[pl.BlockSpec((1, 1, ncp, LANES), lambda b, g: (b, g, 0, 0)),
                   pl.BlockSpec((1, 1, HEAD_DIM, ncp), lambda b, g: (b, g, 0, 0))],
        out_shape=[jax.ShapeDtypeStruct((batch, groups, ncp, LANES), BF16),
                   jax.ShapeDtypeStruct((batch, groups, HEAD_DIM, ncp), BF16)],
        compiler_params=_cparams(("parallel", "parallel"), 32),
        name="nsa_compress",
    )(p1, p1, pk, pv, w1k, w2k, w1v, w2v, kgain, *tables)


def _nsa_kernel(qi_tab, tile_tab, br_tab, q_ref, kc_ref, vct_ref, k_ref, vt_ref, gt_ref,
                dg_ref, ovt_ref, o_ref, acc_ref, oc_ref, sel_ref, imp_ref, sta_ref, stb_ref, cst_ref,
                cmax_ref, m_ref, qh_ref, *, tq, qt, nslc, nsel, nentries):
    g = pl.program_id(1)
    rheads = D_HEADS // D_KV_GROUPS
    seq = qt * tq
    tiny = float(np.finfo(np.float32).tiny)
    qcol = lax.broadcasted_iota(jnp.int32, (tq, tq), 1)
    krow = lax.broadcasted_iota(jnp.int32, (tq, tq), 0)
    bufs = (sta_ref, stb_ref)
    blocks_per_tile = tq // SLC_BLOCK
    nslab = -(-nslc // SUBLANES)
    sel_rows = nslab * SUBLANES

    def rows(tile):
        return pl.ds(pl.multiple_of(tile * tq, tq), tq)

    half_lane = lax.broadcasted_iota(jnp.int32, (1, LANES), 1) // HEAD_DIM

    def split_heads(t, c):
        for r in range(rheads):
            blk = q_ref[rows(t), (r // 2) * LANES:(r // 2 + 1) * LANES]
            qh_ref[rows(t), r * LANES:(r + 1) * LANES] = jnp.where(half_lane == r % 2, blk,
                                                                  jnp.zeros_like(blk))
        return c

    lax.fori_loop(0, qt, split_heads, 0)

    def heads_q(qi):
        return [qh_ref[rows(qi), r * LANES:(r + 1) * LANES] for r in range(rheads)]

    kc = kc_ref[0, 0]
    vct = vct_ref[0, 0]
    ncp = kc.shape[0]
    ovt = ovt_ref[...]

    def select_blocks(qi, c):
        q = heads_q(qi)
        cend = lax.broadcasted_iota(jnp.int32, (ncp, tq), 0) * CMP_STRIDE + (CMP_BLOCK - 1)
        cvalid = cend <= qi * tq + lax.broadcasted_iota(jnp.int32, (ncp, tq), 1)
        psum = jnp.zeros((ncp, tq), F32)
        cmax = [_scores_t(kc, q[r], cvalid, cst_ref, r) for r in range(rheads)]
        for r in range(rheads):
            m = jnp.where(jnp.isfinite(cmax[r]), cmax[r], 0.0)
            e = jnp.exp2(cst_ref[r] - m)
            p = e / jnp.maximum(jnp.sum(e, axis=0, keepdims=True), tiny)
            psum += p
            oc_ref[qi * rheads + r] = jnp.dot(vct, p.astype(BF16), preferred_element_type=F32)
        hi, lo = _split_bf16(psum)
        imp = (jnp.dot(ovt, hi, preferred_element_type=F32)
               + jnp.dot(ovt, lo, preferred_element_type=F32))
        blk = lax.broadcasted_iota(jnp.int32, (NSLC_PAD, tq), 0)
        cur = (qi * tq + lax.broadcasted_iota(jnp.int32, (NSLC_PAD, tq), 1)) // SLC_BLOCK
        imp = jnp.where(blk > cur, -jnp.inf, imp)
        forced = (blk == 0) | (blk == cur) | (blk == cur - 1)
        imp = jnp.where(forced, jnp.inf, imp)
        imp_ref[...] = imp
        slabs = [imp[s * SUBLANES:(s + 1) * SUBLANES, :] for s in range(nslab)]
        ranks = [jnp.zeros((SUBLANES, tq), F32) for _ in range(nslab)]
        sub = lax.broadcasted_iota(jnp.int32, (SUBLANES, tq), 0)
        for jp in range(nslc):
            row = imp_ref[jp:jp + 1, :]
            for s in range(nslab):
                if s * SUBLANES > jp:
                    ahead = jnp.where(row >= slabs[s], 1.0, 0.0)
                elif (s + 1) * SUBLANES - 1 < jp:
                    ahead = jnp.where(row > slabs[s], 1.0, 0.0)
                else:
                    ahead = jnp.where(sub > jp - s * SUBLANES, jnp.where(row >= slabs[s], 1.0, 0.0),
                                      jnp.where(row > slabs[s], 1.0, 0.0))
                ranks[s] = ranks[s] + ahead
        for s in range(nslab):
            start = pl.multiple_of(qi * sel_rows + s * SUBLANES, SUBLANES)
            sel_ref[pl.ds(start, SUBLANES), :] = jnp.where(ranks[s] < nsel, 1.0, 0.0)
        return c

    lax.fori_loop(0, qt, select_blocks, 0)
    acc_ref[...] = jnp.zeros_like(acc_ref)
    m_ref[...] = jnp.full(m_ref.shape, -jnp.inf, F32)

    def score_fn(e, buf):
        qi, tile, br = qi_tab[e], tile_tab[e], br_tab[e]
        q = heads_q(qi)
        k = k_ref[br, rows(tile), :]
        base = qi * sel_rows + tile * blocks_per_tile
        chosen = jnp.concatenate(
            [jnp.broadcast_to(sel_ref[pl.ds(base + c, 1), :], (SLC_BLOCK, tq))
             for c in range(blocks_per_tile)], axis=0)
        dpos = (qi - tile) * tq + (qcol - krow)
        reach = jnp.where(br == 0, seq, WINDOW).astype(jnp.uint32)
        kept = jnp.where(jnp.maximum(chosen, br.astype(F32)) > 0.5, dpos, -1)
        valid = kept.astype(jnp.uint32) < reach
        for r in range(rheads):
            row = buf * rheads + r
            cmax_ref[row:row + 1, :] = _scores_t(k, q[r], valid, bufs[buf], r)

    def accum_fn(e, buf):
        qi, tile, br = qi_tab[e], tile_tab[e], br_tab[e]
        vt = vt_ref[0, br, tile]
        for r in range(rheads):
            cm, sr = buf * rheads + r, (qi * 2 + br) * rheads + r
            m_ref[pl.ds(sr, 1), :] = _accumulate_t(bufs[buf], r, cmax_ref[cm:cm + 1, :],
                                                   m_ref[pl.ds(sr, 1), :], acc_ref, sr, vt, True)

    _pipelined_sweep(nentries, score_fn, accum_fn, unroll=8)

    def finish(qi, c):
        outs = []
        for r in range(rheads):
            row0 = (g * rheads + r) * 3
            o = [oc_ref[qi * rheads + r]]
            for br in range(2):
                sr = (qi * 2 + br) * rheads + r
                o.append(acc_ref[sr, 0:HEAD_DIM, :]
                         / jnp.maximum(acc_ref[sr, HEAD_DIM:HEAD_DIM + 1, :], tiny))
            outs.append(sum(gt_ref[0, qi, pl.ds(row0 + j, 1), :] * o[j] for j in range(3)))
        out = jnp.concatenate(outs, axis=0).T
        o_ref[rows(qi), :] = (out * jax.nn.silu(dg_ref[rows(qi), :])).astype(BF16)
        return c

    lax.fori_loop(0, qt, finish, 0)


def _nsa_attn(qe, kc, vct, kboth, vtboth, gt, p1, ovt, batch, seq, tq, dg_blk):
    n = qe.shape[0]
    groups = D_KV_GROUPS
    rheads = D_HEADS // groups
    width = rheads * HEAD_DIM
    qt = seq // tq
    ncp = kc.shape[2]
    nslc = seq // SLC_BLOCK
    sel_rows = -(-nslc // SUBLANES) * SUBLANES
    wtiles = (WINDOW + tq - 1) // tq
    entries = []
    for qi in range(qt):
        entries += [(qi, t, 0) for t in range(qi + 1)]
        entries += [(qi, t, 1) for t in range(max(qi - wtiles, 0), qi + 1)]
    tabs = [jnp.asarray([e[c] for e in entries], jnp.int32) for c in range(3)]
    kernel = functools.partial(_nsa_kernel, tq=tq, qt=qt, nslc=nslc, nsel=min(SLC_TOPK, nslc),
                               nentries=len(entries))
    return pl.pallas_call(
        kernel,
        grid_spec=pltpu.PrefetchScalarGridSpec(
            num_scalar_prefetch=3,
            grid=(batch, groups),
            in_specs=[pl.BlockSpec((seq, width), lambda b, g, *_: (b, g)),
                      pl.BlockSpec((1, 1, ncp, LANES), lambda b, g, *_: (b, g, 0, 0)),
                      pl.BlockSpec((1, 1, HEAD_DIM, ncp), lambda b, g, *_: (b, g, 0, 0)),
                      pl.BlockSpec((2, seq, LANES), lambda b, g, *_: (0, b, g)),
                      pl.BlockSpec((1, 2, qt, HEAD_DIM + ONES_ROWS, tq), lambda b, g, *_: (b, 0, 0, g, 0)),
                      pl.BlockSpec((1, qt, LANES, tq), lambda b, g, *_: (b, 0, 0, 0)),
                      pl.BlockSpec((seq, width), lambda b, g, *_: (b, dg_blk + g)),
                      pl.BlockSpec(ovt.shape, lambda b, g, *_: (0, 0))],
            out_specs=pl.BlockSpec((seq, width), lambda b, g, *_: (b, g)),
            scratch_shapes=[pltpu.VMEM((qt * 2 * rheads, HEAD_DIM + ONES_ROWS, tq), F32),
                            pltpu.VMEM((qt * rheads, HEAD_DIM, tq), F32),
                            pltpu.VMEM((qt * sel_rows, tq), F32),
                            pltpu.VMEM((NSLC_PAD, tq), F32),
                            pltpu.VMEM((rheads, tq, tq), F32),
                            pltpu.VMEM((rheads, tq, tq), F32),
                            pltpu.VMEM((rheads, ncp, tq), F32),
                            pltpu.VMEM((2 * rheads, tq), F32),
                            pltpu.VMEM((qt * 2 * rheads, tq), F32),
                            pltpu.VMEM((seq, rheads * LANES), BF16)]),
        out_shape=jax.ShapeDtypeStruct((n, groups * width), BF16),
        compiler_params=_cparams(("parallel", "parallel"), 56),
        name="nsa_attn",
    )(*tabs, qe, kc, vct, kboth, vtboth, gt, p1, ovt)


def _rope_tables(pos, d):
    rd = d // 4
    half = rd // 2
    inv = ROPE_THETA ** (-jnp.arange(half, dtype=F32) / half)
    ang = pos.astype(F32)[:, None] * inv[None, :]
    cos, sin = jnp.cos(ang), jnp.sin(ang)
    npos = pos.shape[0]
    zeros = jnp.zeros((npos, d), F32)
    c = jnp.concatenate([cos, cos, jnp.ones((npos, d - rd), F32)], axis=1)
    s1 = zeros.at[:, :half].set(-sin)
    s2 = zeros.at[:, half:rd].set(sin)
    return tuple(jnp.tile(t, (1, LANES // d)) for t in (c, s1, s2))


def _overlap_t(ncp, nslc):
    start = np.arange(ncp) * CMP_STRIDE
    s0 = np.arange(nslc) * SLC_BLOCK
    lo = np.maximum(start[:, None], s0[None, :])
    hi = np.minimum(start[:, None] + CMP_BLOCK, s0[None, :] + SLC_BLOCK)
    ov = np.clip(hi - lo, 0, None) / CMP_BLOCK
    ov[ncp - 1] = 0.0
    out = np.zeros((NSLC_PAD, ncp), np.float32)
    out[:nslc] = ov.T
    return jnp.asarray(out, BF16)


def kernel(x, mem, mem_norm, l0_norm, l0_w_in, l0_a_vnorm, l0_a_ws, l0_a_bs, l0_b_qnorm, l0_b_knorm, l0_b_lq1, l0_b_lk1, l0_b_lq2, l0_b_lk2, l0_b_subln, l0_m_wkv, l0_m_qnorm, l0_m_knorm, l0_w_out, l1_norm, l1_w_in, l1_c_conv_w, l1_c_conv_b, l1_c_norm, l1_d_qnorm, l1_d_knorm, l1_d_cmp_pos_k, l1_d_cmp_w1_k, l1_d_cmp_w2_k, l1_d_cmp_pos_v, l1_d_cmp_w1_v, l1_d_cmp_w2_v, l1_m_wkv, l1_m_qnorm, l1_m_knorm, l1_w_out):
    batch, seq, d_model = x.shape
    n = batch * seq
    tm = 256
    tq = 256
    mw = M_HEADS * HEAD_DIM
    pos = jnp.arange(seq, dtype=jnp.int32)
    x0 = x.reshape(n, d_model)

    wkv_both = jnp.concatenate([l0_m_wkv, l1_m_wkv], axis=1).astype(BF16)
    kgains = jnp.stack([jnp.tile(l0_m_knorm, M_HEADS), jnp.tile(l1_m_knorm, M_HEADS)])
    mem_kh, mem_vt = _mem_kv(mem, mem_norm, wkv_both, kgains)

    w = l0_w_in
    tab32 = _rope_tables(pos, B_QK_DIM)
    b_width = 2 * B_HEADS * B_QK_DIM
    a_width = l0_a_vnorm.shape[0]
    bs_exp = jnp.repeat(l0_a_bs.T, a_width // A_GROUPS, axis=1)
    p0, y_a, qn, kn, vt = _layer0_proj(
        x0, l0_norm, w[:, 0:1536].astype(BF16), w[:, 1536:3072].astype(BF16), w[:, 3072:4096].astype(BF16),
        tab32, jnp.tile(l0_b_qnorm, b_width // B_QK_DIM).reshape(1, b_width),
        jnp.tile(l0_b_knorm, b_width // B_QK_DIM).reshape(1, b_width),
        l0_a_ws, bs_exp, l0_a_vnorm.reshape(1, a_width), batch, seq, tq)
    lam_init = 0.8 - 0.6 * math.exp(-0.3 * 1)
    lam_p = jnp.stack([l0_b_lq1, l0_b_lk1, l0_b_lq2, l0_b_lk2])
    y_b = _diff_attn(lam_p, qn, kn, vt, p0, jnp.tile(l0_b_subln, 2).reshape(1, LANES),
                     batch, seq, tq, lam_init, 0)
    y_m = _mem_attn(p0, mem_kh, mem_vt, 0, jnp.tile(l0_m_qnorm, M_HEADS).reshape(1, mw),
                    512 // mw, 768 // mw, seq, 2 * tq)

    w = l1_w_in
    n_bg = 3 * D_HEADS
    w_rest = jnp.concatenate([w[:, 2840:3352], w[:, 2048:2304], w[:, 3352:3864]], axis=1)
    w_d = jnp.concatenate([w[:, 1536:2048], w[:, 2304:2816], w[:, 2816:2840],
                           jnp.zeros((d_model, LANES - n_bg), F32)], axis=1)
    groups, rheads = D_KV_GROUPS, D_HEADS // D_KV_GROUPS
    d_width = D_HEADS * HEAD_DIM
    tab64 = _rope_tables(pos, HEAD_DIM)
    x1, p1, y_c, qe, kboth, vtboth, gt = _layer1_proj(
        x0, y_a, y_b, y_m, l0_w_out.astype(BF16),
        l1_norm, w[:, 0:1536].astype(BF16), w_d.astype(BF16), w_rest.astype(BF16), tab64,
        jnp.tile(l1_d_qnorm, D_HEADS).reshape(1, d_width),
        jnp.tile(l1_d_knorm, LANES // HEAD_DIM).reshape(1, LANES),
        l1_c_conv_w, l1_c_conv_b, l1_c_norm, batch, seq, tq)
    kv_blk = 512 // LANES

    ncp = seq // CMP_STRIDE
    cmp_pos = jnp.arange(ncp, dtype=jnp.int32) * CMP_STRIDE + (CMP_BLOCK - 1)
    hidden = l1_d_cmp_w2_k.shape[0]

    def group_w1(w1):
        w = w1.reshape(CMP_BLOCK, HEAD_DIM, hidden)
        return jnp.stack([jnp.zeros((CMP_BLOCK, LANES, hidden), F32)
                          .at[:, g * HEAD_DIM:(g + 1) * HEAD_DIM, :].set(w) for g in range(groups)])

    w2k = jnp.tile(l1_d_cmp_w2_k, (1, LANES // HEAD_DIM))
    w2v = jnp.zeros((hidden, LANES), F32).at[:, :HEAD_DIM].set(l1_d_cmp_w2_v)
    kc, vct = _compress(
        p1, kv_blk, jnp.tile(l1_d_cmp_pos_k, (1, groups)), jnp.tile(l1_d_cmp_pos_v, (1, groups)),
        group_w1(l1_d_cmp_w1_k).astype(BF16), w2k.astype(BF16),
        group_w1(l1_d_cmp_w1_v).astype(BF16), w2v.astype(BF16),
        jnp.tile(l1_d_knorm, LANES // HEAD_DIM).reshape(1, LANES), _rope_tables(cmp_pos, HEAD_DIM),
        batch, seq)

    nslc = seq // SLC_BLOCK
    y_d = _nsa_attn(qe, kc, vct, kboth, vtboth, gt, p1, _overlap_t(ncp, nslc),
                    batch, seq, tq, 0)
    out = _mem_out(x1, y_c, y_d, p1, mem_kh, mem_vt, 1, jnp.tile(l1_m_qnorm, M_HEADS).reshape(1, mw),
                   768 // mw, 1024 // mw, l1_w_out.astype(BF16), seq, 2 * tq)
    return out.reshape(batch, seq, d_model)
```

```python
import functools
import math

import numpy as np
import jax
import jax.numpy as jnp
from jax import lax
from jax.experimental import pallas as pl
from jax.experimental.pallas import tpu as pltpu

F32 = jnp.float32
BF16 = jnp.bfloat16

EPS = 1e-6
ROPE_THETA = 500000.0
HEAD_DIM = 64
A_CHUNK = 128
A_GROUPS = 4
B_HEADS = 8
B_QK_DIM = 32
C_KERNEL = 31
D_HEADS = 8
D_KV_GROUPS = 2
CMP_BLOCK = 32
CMP_STRIDE = 16
SLC_BLOCK = 64
SLC_TOPK = 16
WINDOW = 512
M_HEADS = 4

LANES = 128
SUBLANES = 8
MXU_DIM = 256
HALO = 32
NSLC_PAD = 128
ONES_ROWS = 16

NT_DIMS = (((1,), (1,)), ((), ()))
LOG2E = math.log2(math.e)


def _cparams(semantics, vmem_mb):
    return pltpu.CompilerParams(dimension_semantics=semantics,
                                vmem_limit_bytes=vmem_mb * 1024 * 1024)


def _split_bf16(x):
    hi = x.astype(BF16)
    lo = (x - hi.astype(F32)).astype(BF16)
    return hi, lo


def _group_mean(x2, gsize):
    t, c = x2.shape
    w = min(c, MXU_DIM)
    r = lax.broadcasted_iota(jnp.int32, (w, w), 0) // gsize
    cc = lax.broadcasted_iota(jnp.int32, (w, w), 1) // gsize
    ones = jnp.where(r == cc, 1.0, 0.0).astype(BF16)
    outs = []
    for s in range(c // w):
        hi, lo = _split_bf16(x2[:, s * w:(s + 1) * w])
        outs.append(jnp.dot(hi, ones, preferred_element_type=F32)
                    + jnp.dot(lo, ones, preferred_element_type=F32))
    out = outs[0] if len(outs) == 1 else jnp.concatenate(outs, axis=1)
    return out * (1.0 / gsize)


def _tile_lanes(t, width):
    rep = width // t.shape[1]
    return t if rep == 1 else jnp.concatenate([t] * rep, axis=1)


def _rope(x, cos_t, s1_t, s2_t, half):
    width = x.shape[1]
    up = pltpu.roll(x, width - half, 1)
    dn = pltpu.roll(x, half, 1)
    return (x * _tile_lanes(cos_t, width) + up * _tile_lanes(s1_t, width)
            + dn * _tile_lanes(s2_t, width))


def _matmul_cols(h, w_ref, lo, hi, out_ref=None):
    r = jnp.dot(h, w_ref[:, lo:hi], preferred_element_type=F32)
    if out_ref is None:
        return r
    out_ref[:, lo:hi] = r


def _rms_rows(x, ng_ref):
    ms = jnp.mean(x * x, axis=-1, keepdims=True)
    return (x * lax.rsqrt(ms + EPS) * ng_ref[...]).astype(BF16)


def _scores_t(k, q, valid, st_ref, idx):
    st = lax.dot_general(k, q, NT_DIMS, preferred_element_type=F32)
    if valid is not None:
        st = jnp.where(valid, st, -jnp.inf)
    st_ref[idx] = st
    return jnp.max(st, axis=0, keepdims=True)


def _accumulate_t(st_ref, idx, cmax, m, acc_ref, aidx, vt, guard_empty):
    m_new = jnp.maximum(m, cmax)
    m_use = jnp.where(m_new == -jnp.inf, 0.0, m_new) if guard_empty else m_new
    alpha = jnp.exp2(m - m_use)
    pt = jnp.exp2((st_ref[idx] - m_use).astype(BF16))
    acc_ref[aidx] = alpha * acc_ref[aidx] + jnp.dot(vt, pt, preferred_element_type=F32)
    return m_new


def _pipelined_sweep(n, score_fn, accum_fn, unroll=4):
    score_fn(0, 0)
    stages = n - 1

    def stage(e, parity):
        score_fn(e + 1, 1 - parity)
        accum_fn(e, parity)

    def body(i, c):
        for k in range(unroll):
            stage(unroll * i + k, k % 2)
        return c

    if isinstance(n, int):
        iters, left = divmod(stages, unroll)
        if iters:
            lax.fori_loop(0, iters, body, 0)
        for k in range(left):
            stage(iters * unroll + k, k % 2)
        accum_fn(n - 1, left % 2)
        return
    lax.fori_loop(0, stages // unroll, body, 0)
    base = stages // unroll * unroll
    left = stages - base
    for k in range(unroll - 1):
        pl.when(left > k)(functools.partial(stage, base + k, k % 2))
    for parity in range(2):
        pl.when(left % 2 == parity)(functools.partial(accum_fn, n - 1, parity))


def _mix_out(ya_ref, yb_ref, ym_ref, w_ref):
    wa = ya_ref.shape[1]
    wb = yb_ref.shape[1]
    acc = jnp.dot(ya_ref[...], w_ref[0:wa, :], preferred_element_type=F32)
    acc += jnp.dot(yb_ref[...], w_ref[wa:wa + wb, :], preferred_element_type=F32)
    acc += jnp.dot(ym_ref[...], w_ref[wa + wb:, :], preferred_element_type=F32)
    return acc


def _sgu_group(u, v, gate, w_ref, bs, vgain, o_ref, cols):
    tm = u.shape[0]
    row = lax.broadcasted_iota(jnp.int32, (A_CHUNK, A_CHUNK), 0)
    col = lax.broadcasted_iota(jnp.int32, (A_CHUNK, A_CHUNK), 1)
    w = jnp.where(col <= row, w_ref, 0.0).astype(BF16)
    gv = jax.nn.gelu(v)
    ms = jnp.mean(gv * gv, axis=-1, keepdims=True)
    vn = (gv * lax.rsqrt(ms + EPS) * vgain).astype(BF16)
    gu = jax.nn.gelu(u) * jax.nn.silu(gate)
    for c in range(tm // A_CHUNK):
        rows = slice(c * A_CHUNK, (c + 1) * A_CHUNK)
        z = jnp.dot(w, vn[rows, :], preferred_element_type=F32) + bs
        o_ref[rows, cols] = (gu[rows, :] * z).astype(BF16)


def _layer0_proj_kernel(x_ref, ng_ref, wa_ref, wqkv_ref, w_ref, c_ref, s1_ref, s2_ref, qg_ref, kg_ref,
                        ws_ref, bs_ref, vg_ref, p_ref, ya_ref, qo_ref, ko_ref, vto_ref):
    h = _rms_rows(x_ref[...], ng_ref)
    aw = vg_ref.shape[1]
    gdim = aw // A_GROUPS
    width = qg_ref.shape[1]
    half = B_QK_DIM // 4 // 2
    pw = p_ref.shape[1]

    a = _matmul_cols(h, wa_ref, 0, 3 * aw)

    def sgu(g):
        cols = slice(g * gdim, (g + 1) * gdim)
        _sgu_group(a[:, cols], a[:, aw + g * gdim:aw + (g + 1) * gdim],
                   a[:, 2 * aw + g * gdim:2 * aw + (g + 1) * gdim],
                   ws_ref[g], bs_ref[:, cols], vg_ref[:, cols], ya_ref, cols)

    def prep(t, gain, dst, mul):
        tn = t * lax.rsqrt(_group_mean(t * t, B_QK_DIM) + EPS) * gain[...]
        tr = _rope(tn, c_ref[...], s1_ref[...], s2_ref[...], half)
        dst[...] = (tr * mul if mul != 1.0 else tr).astype(BF16)

    q = _matmul_cols(h, wqkv_ref, 0, width)
    sgu(0)
    k = _matmul_cols(h, wqkv_ref, width, 2 * width)
    sgu(1)
    v = _matmul_cols(h, wqkv_ref, 2 * width, 3 * width)
    sgu(2)
    _matmul_cols(h, w_ref, 0, pw // 2, p_ref)
    sgu(3)
    prep(q, qg_ref, qo_ref, B_QK_DIM ** -0.5 * LOG2E)
    _matmul_cols(h, w_ref, pw // 2, pw, p_ref)
    prep(k, kg_ref, ko_ref, 1.0)
    vt = v.T
    dv = 2 * B_QK_DIM
    ones = jnp.ones((ONES_ROWS, vt.shape[1]), F32)
    parts = []
    for hd in range(B_HEADS):
        parts += [vt[hd * dv:(hd + 1) * dv, :], ones]
    vto_ref[0, 0] = jnp.concatenate(parts, axis=0).astype(BF16)


def _layer0_proj(x, norm_g, w_a, w_qkv, w_rest, tables, qgain, kgain, a_ws, bs_exp, vgain, batch, seq, tm):
    n, d = x.shape
    width = qgain.shape[1]
    aw = vgain.shape[1]
    tiles_per_seq = seq // tm
    tab_spec = pl.BlockSpec((tm, LANES), lambda i: (i % tiles_per_seq, 0))
    vrows = B_HEADS * (2 * B_QK_DIM + ONES_ROWS)

    def full(a):
        return pl.BlockSpec(a.shape, lambda i: (0,) * a.ndim)

    def rows(w):
        return pl.BlockSpec((tm, w), lambda i: (i, 0))

    return pl.pallas_call(
        _layer0_proj_kernel,
        grid=(n // tm,),
        in_specs=[rows(d), pl.BlockSpec((1, d), lambda i: (0, 0)), full(w_a), full(w_qkv), full(w_rest),
                  tab_spec, tab_spec, tab_spec, full(qgain), full(kgain),
                  full(a_ws), full(bs_exp), full(vgain)],
        out_specs=[rows(w_rest.shape[1]), rows(aw), rows(width), rows(width),
                   pl.BlockSpec((1, 1, vrows, tm),
                                lambda i: (i // tiles_per_seq, i % tiles_per_seq, 0, 0))],
        out_shape=[jax.ShapeDtypeStruct((n, w_rest.shape[1]), F32),
                   jax.ShapeDtypeStruct((n, aw), BF16),
                   jax.ShapeDtypeStruct((n, width), BF16), jax.ShapeDtypeStruct((n, width), BF16),
                   jax.ShapeDtypeStruct((batch, tiles_per_seq, vrows, tm), BF16)],
        compiler_params=_cparams(("parallel",), 48),
        name="layer0_proj",
    )(x, norm_g.reshape(1, d), w_a, w_qkv, w_rest, *tables, qgain, kgain, a_ws, bs_exp, vgain)


def _diff_attn_kernel(qi_tab, tile_tab, lam_ref, q_ref, k_ref, vt_ref, g_ref, sub_ref, o_ref,
                      acc_ref, sta_ref, stb_ref, cmax_ref, m_ref, *, tq, qt, lam_init):
    dv = 2 * B_QK_DIM
    vrows = dv + ONES_ROWS
    nmap = 4
    lane = lax.broadcasted_iota(jnp.int32, (1, LANES), 1)
    bufs = (sta_ref, stb_ref)
    acc_ref[...] = jnp.zeros_like(acc_ref)
    m_ref[...] = jnp.full(m_ref.shape, -jnp.inf, F32)
    krow = lax.broadcasted_iota(jnp.int32, (tq, tq), 0)
    qcol = lax.broadcasted_iota(jnp.int32, (tq, tq), 1)
    lam_p = lam_ref[...]
    lam = (jnp.exp(jnp.sum(lam_p[0:1] * lam_p[1:2], axis=-1, keepdims=True))
           - jnp.exp(jnp.sum(lam_p[2:3] * lam_p[3:4], axis=-1, keepdims=True)) + lam_init)

    def rows(tile):
        return pl.ds(pl.multiple_of(tile * tq, tq), tq)

    def scores(qi, tile, buf, valid):
        q = q_ref[rows(qi), :]
        k = k_ref[rows(tile), :]
        zero = jnp.zeros_like(q)
        for j in range(nmap):
            qm = jnp.where(lane // B_QK_DIM == j, q, zero)
            row = buf * nmap + j
            cmax_ref[row:row + 1, :] = _scores_t(k, qm, valid, bufs[buf], j)

    def accumulate(qi, tile, buf):
        for j in range(nmap):
            vt = vt_ref[0, tile, (j // 2) * vrows:(j // 2 + 1) * vrows, :]
            cm, sr = buf * nmap + j, qi * nmap + j
            m_ref[pl.ds(sr, 1), :] = _accumulate_t(bufs[buf], j, cmax_ref[cm:cm + 1, :],
                                                   m_ref[pl.ds(sr, 1), :], acc_ref, sr, vt, False)

    def finish(qi):
        halves = []
        for h in range(2):
            o = [acc_ref[qi * nmap + 2 * h + mp, 0:dv, :] / acc_ref[qi * nmap + 2 * h + mp, dv:dv + 1, :]
                 for mp in range(2)]
            pd = o[0] - lam * o[1]
            ms2 = jnp.mean(pd * pd, axis=0, keepdims=True)
            halves.append(pd * lax.rsqrt(ms2 + EPS))
        ob = jnp.concatenate(halves, axis=0).T * sub_ref[...] * (1.0 - lam_init)
        o_ref[rows(qi), :] = (ob * jax.nn.silu(g_ref[rows(qi), :])).astype(BF16)

    def below_score(e, buf):
        scores(qi_tab[e], tile_tab[e], buf, None)

    def below_accum(e, buf):
        accumulate(qi_tab[e], tile_tab[e], buf)

    def diag_score(e, buf):
        scores(e, e, buf, krow <= qcol)

    def diag_accum(e, buf):
        accumulate(e, e, buf)
        finish(e)

    _pipelined_sweep(qt * (qt - 1) // 2, below_score, below_accum, unroll=8)
    _pipelined_sweep(qt, diag_score, diag_accum)


def _diff_attn(lam_p, qn, kn, vt, p0, subln, batch, seq, tq, lam_init, g_off):
    n, width = qn.shape
    pairs = width // LANES
    qt = seq // tq
    vrows = 2 * B_QK_DIM + ONES_ROWS
    below = [(qi, t) for qi in range(qt) for t in range(qi)]
    qi_tab = jnp.asarray([e[0] for e in below], jnp.int32)
    tile_tab = jnp.asarray([e[1] for e in below], jnp.int32)
    kernel = functools.partial(_diff_attn_kernel, tq=tq, qt=qt, lam_init=lam_init)
    seq_spec = pl.BlockSpec((seq, LANES), lambda b, p, *_: (b, p))
    return pl.pallas_call(
        kernel,
        grid_spec=pltpu.PrefetchScalarGridSpec(
            num_scalar_prefetch=2,
            grid=(batch, pairs),
            in_specs=[pl.BlockSpec(lam_p.shape, lambda b, p, *_: (0, 0)),
                      seq_spec, seq_spec,
                      pl.BlockSpec((1, qt, 2 * vrows, tq), lambda b, p, *_: (b, 0, p, 0)),
                      pl.BlockSpec((seq, LANES), lambda b, p, *_: (b, g_off + p)),
                      pl.BlockSpec((1, LANES), lambda b, p, *_: (0, 0))],
            out_specs=seq_spec,
            scratch_shapes=[pltpu.VMEM((qt * 4, vrows, tq), F32), pltpu.VMEM((4, tq, tq), F32),
                            pltpu.VMEM((4, tq, tq), F32), pltpu.VMEM((8, tq), F32),
                            pltpu.VMEM((qt * 4, tq), F32)]),
        out_shape=jax.ShapeDtypeStruct((n, width), BF16),
        compiler_params=_cparams(("parallel", "parallel"), 40),
        name="diff_attn",
    )(qi_tab, tile_tab, lam_p, qn, kn, vt, p0, subln)


def _mem_kv_kernel(mem_ref, g_ref, w_ref, kg_ref, kh_ref, vt_ref):
    x = mem_ref[0]
    ms = jnp.mean(x * x, axis=-1, keepdims=True)
    h = (x * lax.rsqrt(ms + EPS) * g_ref[...]).astype(BF16)
    kv = jnp.dot(h, w_ref[...], preferred_element_type=F32)
    mw = kv.shape[1] // 4
    lane = lax.broadcasted_iota(jnp.int32, (1, mw), 1)
    ones = jnp.ones((ONES_ROWS, kv.shape[0]), F32)
    for layer in range(2):
        k = kv[:, (2 * layer) * mw:(2 * layer + 1) * mw]
        kn = k * lax.rsqrt(_group_mean(k * k, HEAD_DIM) + EPS) * kg_ref[layer:layer + 1, :]
        vt = kv[:, (2 * layer + 1) * mw:(2 * layer + 2) * mw].T
        parts = []
        for hd in range(M_HEADS):
            kh_ref[layer, 0, hd] = jnp.where(lane // HEAD_DIM == hd, kn, 0.0).astype(BF16)
            parts += [vt[hd * HEAD_DIM:(hd + 1) * HEAD_DIM, :], ones]
        vt_ref[layer, 0] = jnp.concatenate(parts, axis=0).astype(BF16)


def _mem_kv(mem, mem_norm, wkv_both, kgains):
    batch, mtok, d = mem.shape
    mw = wkv_both.shape[1] // 4
    vrows = M_HEADS * (HEAD_DIM + ONES_ROWS)
    return pl.pallas_call(
        _mem_kv_kernel,
        grid=(batch,),
        in_specs=[pl.BlockSpec((1, mtok, d), lambda b: (b, 0, 0)),
                  pl.BlockSpec((1, d), lambda b: (0, 0)),
                  pl.BlockSpec(wkv_both.shape, lambda b: (0, 0)),
                  pl.BlockSpec(kgains.shape, lambda b: (0, 0))],
        out_specs=[pl.BlockSpec((2, 1, M_HEADS, mtok, mw), lambda b: (0, b, 0, 0, 0)),
                   pl.BlockSpec((2, 1, vrows, mtok), lambda b: (0, b, 0, 0))],
        out_shape=[jax.ShapeDtypeStruct((2, batch, M_HEADS, mtok, mw), BF16),
                   jax.ShapeDtypeStruct((2, batch, vrows, mtok), BF16)],
        compiler_params=_cparams(("parallel",), 32),
        name="mem_kv",
    )(mem, mem_norm.reshape(1, d), wkv_both, kgains)


def _mem_attn_tile(q_ref, g_ref, kh_ref, vt_ref, qg_ref, st_ref):
    vrows = HEAD_DIM + ONES_ROWS
    x = q_ref[...]
    qn = (x * lax.rsqrt(_group_mean(x * x, HEAD_DIM) + EPS) * qg_ref[...]
          * (HEAD_DIM ** -0.5 * LOG2E)).astype(BF16)
    cmax = [_scores_t(kh_ref[0, 0, h], qn, None, st_ref, h) for h in range(M_HEADS)]
    outs = []
    for h in range(M_HEADS):
        pt = jnp.exp2((st_ref[h] - cmax[h]).astype(BF16))
        acc = jnp.dot(vt_ref[0, 0, h * vrows:(h + 1) * vrows, :], pt, preferred_element_type=F32)
        outs.append(acc[0:HEAD_DIM, :] / acc[HEAD_DIM:HEAD_DIM + 1, :])
    out = jnp.concatenate(outs, axis=0).T
    return (out * jax.nn.silu(g_ref[...])).astype(BF16)


def _mem_attn_kernel(q_ref, g_ref, kh_ref, vt_ref, qg_ref, o_ref, st_ref):
    o_ref[...] = _mem_attn_tile(q_ref, g_ref, kh_ref, vt_ref, qg_ref, st_ref)


def _mem_out_kernel(x_ref, ya_ref, yb_ref, q_ref, g_ref, kh_ref, vt_ref, qg_ref, w_ref, o_ref, st_ref):
    ym = _mem_attn_tile(q_ref, g_ref, kh_ref, vt_ref, qg_ref, st_ref)
    wa = ya_ref.shape[1]
    wb = yb_ref.shape[1]
    acc = jnp.dot(ya_ref[...], w_ref[0:wa, :], preferred_element_type=F32)
    acc += jnp.dot(yb_ref[...], w_ref[wa:wa + wb, :], preferred_element_type=F32)
    acc += jnp.dot(ym, w_ref[wa + wb:, :], preferred_element_type=F32)
    o_ref[...] = x_ref[...] + acc


def _mem_out(x, ya, yb, p, kh, vt, layer, qgain, q_blk, g_blk, w, seq, tq):
    n, d = x.shape
    _, batch, heads, mtok, mw = kh.shape
    qt = seq // tq

    def rows(width, blk=0):
        return pl.BlockSpec((tq, width), lambda i: (i, blk))

    return pl.pallas_call(
        _mem_out_kernel,
        grid=(n // tq,),
        in_specs=[rows(d), rows(ya.shape[1]), rows(yb.shape[1]), rows(mw, q_blk), rows(mw, g_blk),
                  pl.BlockSpec((1, 1, heads, mtok, mw), lambda i: (layer, i // qt, 0, 0, 0)),
                  pl.BlockSpec((1, 1, vt.shape[2], mtok), lambda i: (layer, i // qt, 0, 0)),
                  pl.BlockSpec((1, mw), lambda i: (0, 0)),
                  pl.BlockSpec(w.shape, lambda i: (0, 0))],
        out_specs=rows(d),
        out_shape=jax.ShapeDtypeStruct((n, d), F32),
        scratch_shapes=[pltpu.VMEM((heads, mtok, tq), F32)],
        compiler_params=_cparams(("parallel",), 40),
        name="mem_out_proj",
    )(x, ya, yb, p, p, kh, vt, qgain, w)


def _mem_attn(p, kh, vt, layer, qgain, q_blk, g_blk, seq, tq):
    n = p.shape[0]
    _, batch, heads, mtok, mw = kh.shape
    qt = seq // tq
    return pl.pallas_call(
        _mem_attn_kernel,
        grid=(n // tq,),
        in_specs=[pl.BlockSpec((tq, mw), lambda i: (i, q_blk)),
                  pl.BlockSpec((tq, mw), lambda i: (i, g_blk)),
                  pl.BlockSpec((1, 1, heads, mtok, mw), lambda i: (layer, i // qt, 0, 0, 0)),
                  pl.BlockSpec((1, 1, vt.shape[2], mtok), lambda i: (layer, i // qt, 0, 0)),
                  pl.BlockSpec((1, mw), lambda i: (0, 0))],
        out_specs=pl.BlockSpec((tq, mw), lambda i: (i, 0)),
        out_shape=jax.ShapeDtypeStruct((n, mw), BF16),
        scratch_shapes=[pltpu.VMEM((heads, mtok, tq), F32)],
        compiler_params=_cparams(("parallel",), 32),
        name="mem_attn",
    )(p, p, kh, vt, qgain)


def _layer1_proj_kernel(x_ref, ya_ref, yb_ref, ym_ref, wo_ref, ng_ref, wc_ref, wd_ref, w_ref,
                        c_ref, s1_ref, s2_ref, qg_ref, kg_ref, cw_ref, cb_ref, cn_ref,
                        x1_ref, p_ref, yc_ref, qo_ref, ko_ref, vto_ref, gto_ref,
                        hbuf_ref, shift_ref, *, tiles_per_seq):
    i = pl.program_id(0)
    ts = x_ref.shape[0]
    span = ts + HALO
    cwid = cn_ref.shape[1]
    qw = qg_ref.shape[1]
    pw = p_ref.shape[1]
    half = HEAD_DIM // 4 // 2
    x1 = x_ref[...] + _mix_out(ya_ref, yb_ref, ym_ref, wo_ref)
    x1_ref[...] = x1
    h = _rms_rows(x1, ng_ref)

    @pl.when(i % tiles_per_seq == 0)
    def _():
        hbuf_ref[0:HALO, :] = jnp.zeros((HALO, cwid), F32)

    ca = _matmul_cols(h, wc_ref, 0, cwid)
    cbv = _matmul_cols(h, wc_ref, cwid, 2 * cwid)
    q = _matmul_cols(h, wd_ref, 0, qw)
    hbuf_ref[HALO:HALO + ts, :] = ca * jax.nn.sigmoid(cbv)
    for ph in range(SUBLANES):
        shift_ref[ph, 0:span - ph, :] = hbuf_ref[ph:span, :]
    cg = _matmul_cols(h, wc_ref, 2 * cwid, 3 * cwid)
    d = _matmul_cols(h, wd_ref, qw, wd_ref.shape[1])
    first = HALO - (C_KERNEL - 1)
    y = cb_ref[...]
    for j in range(C_KERNEL):
        ph, base = (first + j) % SUBLANES, (first + j) // SUBLANES * SUBLANES
        y = y + shift_ref[ph, base:base + ts, :] * cw_ref[j:j + 1, :]
    _matmul_cols(h, w_ref, 0, pw // 2, p_ref)
    ms = jnp.mean(y * y, axis=-1, keepdims=True)
    yn = y * lax.rsqrt(ms + EPS) * cn_ref[...]
    yc_ref[...] = (jax.nn.silu(yn) * jax.nn.silu(cg)).astype(BF16)
    hbuf_ref[0:HALO, :] = hbuf_ref[ts:ts + HALO, :]
    _matmul_cols(h, w_ref, pw // 2, pw, p_ref)

    tabs = (c_ref[...], s1_ref[...], s2_ref[...])
    tn = q * lax.rsqrt(_group_mean(q * q, HEAD_DIM) + EPS) * qg_ref[...]
    qo_ref[...] = (_rope(tn, *tabs, half) * (HEAD_DIM ** -0.5 * LOG2E)).astype(BF16)
    ones = jnp.ones((ONES_ROWS, ts), F32)
    lane = lax.broadcasted_iota(jnp.int32, (1, LANES), 1)
    for br in range(2):
        k = d[:, 2 * br * LANES:(2 * br + 1) * LANES]
        kn = k * lax.rsqrt(_group_mean(k * k, HEAD_DIM) + EPS) * kg_ref[...]
        kr = _rope(kn, *tabs, half)
        swapped = pltpu.roll(kr, HEAD_DIM, 1)
        for g in range(D_KV_GROUPS):
            own = (lane // HEAD_DIM) == g
            ko_ref[br, :, g * LANES:(g + 1) * LANES] = jnp.where(own, kr, swapped).astype(BF16)
        vt = d[:, (2 * br + 1) * LANES:(2 * br + 2) * LANES].T
        parts = []
        for g in range(D_KV_GROUPS):
            parts += [vt[g * HEAD_DIM:(g + 1) * HEAD_DIM, :], ones]
        vto_ref[0, br, 0] = jnp.concatenate(parts, axis=0).astype(BF16)
    gto_ref[0, 0] = jax.nn.sigmoid(d[:, 4 * LANES:5 * LANES]).T


def _layer1_proj(x, ya, yb, ym, w_out, norm_g, w_c, w_d, w_rest, tables, qgain, kgain,
                 conv_w, conv_b, conv_norm, batch, seq, tm):
    n, dm = x.shape
    cwid = conv_w.shape[1]
    tiles_per_seq = seq // tm
    tab_spec = pl.BlockSpec((tm, LANES), lambda i: (i % tiles_per_seq, 0))
    vrows = D_KV_GROUPS * (HEAD_DIM + ONES_ROWS)

    def full(a):
        return pl.BlockSpec(a.shape, lambda i: (0,) * a.ndim)

    def rows(w):
        return pl.BlockSpec((tm, w), lambda i: (i, 0))

    kernel = functools.partial(_layer1_proj_kernel, tiles_per_seq=tiles_per_seq)
    return pl.pallas_call(
        kernel,
        grid=(n // tm,),
        in_specs=[rows(dm), rows(ya.shape[1]), rows(yb.shape[1]), rows(ym.shape[1]), full(w_out),
                  pl.BlockSpec((1, dm), lambda i: (0, 0)), full(w_c), full(w_d), full(w_rest),
                  tab_spec, tab_spec, tab_spec, full(qgain), full(kgain),
                  full(conv_w), pl.BlockSpec((1, cwid), lambda i: (0, 0)),
                  pl.BlockSpec((1, cwid), lambda i: (0, 0))],
        out_specs=[rows(dm), rows(w_rest.shape[1]), rows(cwid), rows(qgain.shape[1]),
                   pl.BlockSpec((2, tm, D_KV_GROUPS * LANES), lambda i: (0, i, 0)),
                   pl.BlockSpec((1, 2, 1, vrows, tm),
                                lambda i: (i // tiles_per_seq, 0, i % tiles_per_seq, 0, 0)),
                   pl.BlockSpec((1, 1, LANES, tm),
                                lambda i: (i // tiles_per_seq, i % tiles_per_seq, 0, 0))],
        out_shape=[jax.ShapeDtypeStruct((n, dm), F32),
                   jax.ShapeDtypeStruct((n, w_rest.shape[1]), F32),
                   jax.ShapeDtypeStruct((n, cwid), BF16),
                   jax.ShapeDtypeStruct((n, qgain.shape[1]), BF16),
                   jax.ShapeDtypeStruct((2, n, D_KV_GROUPS * LANES), BF16),
                   jax.ShapeDtypeStruct((batch, 2, tiles_per_seq, vrows, tm), BF16),
                   jax.ShapeDtypeStruct((batch, tiles_per_seq, LANES, tm), F32)],
        scratch_shapes=[pltpu.VMEM((tm + HALO, cwid), F32),
                        pltpu.VMEM((SUBLANES, tm + HALO, cwid), F32)],
        compiler_params=_cparams(("arbitrary",), 52),
        name="layer1_proj",
    )(x, ya, yb, ym, w_out, norm_g.reshape(1, dm), w_c, w_d, w_rest, *tables, qgain, kgain, conv_w,
      conv_b.reshape(1, cwid), conv_norm.reshape(1, cwid))


def _compress_kernel(tk_ref, tv_ref, pk_ref, pv_ref, w1k_ref, w2k_ref, w1v_ref, w2v_ref,
                     kg_ref, c_ref, s1_ref, s2_ref, kc_ref, vct_ref):
    half = HEAD_DIM // 4 // 2
    ncp = kc_ref.shape[2]

    def mlp(t_ref, p_ref, w1_ref, w2):
        a = jnp.zeros((ncp, w1_ref.shape[3]), F32)
        b = jnp.zeros((ncp, w1_ref.shape[3]), F32)
        for l in range(CMP_STRIDE):
            x = t_ref[pl.ds(l, ncp, stride=CMP_STRIDE), :]
            a += jnp.dot((x + p_ref[l:l + 1, :]).astype(BF16), w1_ref[0, l],
                         preferred_element_type=F32)
            b += jnp.dot((x + p_ref[CMP_STRIDE + l:CMP_STRIDE + l + 1, :]).astype(BF16),
                         w1_ref[0, CMP_STRIDE + l], preferred_element_type=F32)
        h = a + pltpu.roll(b, ncp - 1, 0)
        return jnp.dot(jax.nn.silu(h).astype(BF16), w2, preferred_element_type=F32)

    kc = mlp(tk_ref, pk_ref, w1k_ref, w2k_ref[...])
    kn = kc * lax.rsqrt(_group_mean(kc * kc, HEAD_DIM) + EPS) * kg_ref[...]
    kc_ref[0, 0] = _rope(kn, c_ref[...], s1_ref[...], s2_ref[...], half).astype(BF16)
    vc = mlp(tv_ref, pv_ref, w1v_ref, w2v_ref[...])
    vct_ref[0, 0] = vc.T[0:HEAD_DIM, :].astype(BF16)


def _compress(p1, k_blk, pk, pv, w1k, w2k, w1v, w2v, kgain, tables, batch, seq):
    groups = w1k.shape[0]
    ncp = seq // CMP_STRIDE

    def full(a):
        return pl.BlockSpec(a.shape, lambda b, g: (0,) * a.ndim)

    def per_group(a):
        return pl.BlockSpec((1,) + a.shape[1:], lambda b, g: (g,) + (0,) * (a.ndim - 1))

    return pl.pallas_call(
        _compress_kernel,
        grid=(batch, groups),
        in_specs=[pl.BlockSpec((seq, LANES), lambda b, g: (b, k_blk)),
                  pl.BlockSpec((seq, LANES), lambda b, g: (b, k_blk + 1)),
                  full(pk), full(pv), per_group(w1k), full(w2k), per_group(w1v), full(w2v),
                  full(kgain), full(tables[0]), full(tables[1]), full(tables[2])],
        out_specs=[pl.BlockSpec((1, 1, ncp, LANES), lambda b, g: (b, g, 0, 0)),
                   pl.BlockSpec((1, 1, HEAD_DIM, ncp), lambda b, g: (b, g, 0, 0))],
        out_shape=[jax.ShapeDtypeStruct((batch, groups, ncp, LANES), BF16),
                   jax.ShapeDtypeStruct((batch, groups, HEAD_DIM, ncp), BF16)],
        compiler_params=_cparams(("parallel", "parallel"), 32),
        name="nsa_compress",
    )(p1, p1, pk, pv, w1k, w2k, w1v, w2v, kgain, *tables)


def _nsa_kernel(qi_tab, tile_tab, br_tab, q_ref, kc_ref, vct_ref, k_ref, vt_ref, gt_ref,
                dg_ref, ovt_ref, o_ref, acc_ref, oc_ref, sel_ref, imp_ref, sta_ref, stb_ref, cst_ref,
                cmax_ref, m_ref, qh_ref, *, tq, qt, nslc, nsel, nentries):
    g = pl.program_id(1)
    rheads = D_HEADS // D_KV_GROUPS
    seq = qt * tq
    tiny = float(np.finfo(np.float32).tiny)
    qcol = lax.broadcasted_iota(jnp.int32, (tq, tq), 1)
    krow = lax.broadcasted_iota(jnp.int32, (tq, tq), 0)
    bufs = (sta_ref, stb_ref)
    blocks_per_tile = tq // SLC_BLOCK
    nslab = -(-nslc // SUBLANES)
    sel_rows = nslab * SUBLANES

    def rows(tile):
        return pl.ds(tile * tq if isinstance(tile, int) else pl.multiple_of(tile * tq, tq), tq)

    half_lane = lax.broadcasted_iota(jnp.int32, (1, LANES), 1) // HEAD_DIM

    def split_heads(t, c):
        for r in range(rheads):
            blk = q_ref[rows(t), (r // 2) * LANES:(r // 2 + 1) * LANES]
            qh_ref[rows(t), r * LANES:(r + 1) * LANES] = jnp.where(half_lane == r % 2, blk,
                                                                  jnp.zeros_like(blk))
        return c

    lax.fori_loop(0, qt, split_heads, 0)

    def heads_q(qi):
        return [qh_ref[rows(qi), r * LANES:(r + 1) * LANES] for r in range(rheads)]

    kc = kc_ref[0, 0]
    vct = vct_ref[0, 0]
    ncp = kc.shape[0]
    ovt = ovt_ref[...]

    def select_blocks(qi):
        q = heads_q(qi)
        ncq = min(ncp, -(-(((qi + 1) * tq - CMP_BLOCK) // CMP_STRIDE + 1) // LANES) * LANES)
        cend = lax.broadcasted_iota(jnp.int32, (ncq, tq), 0) * CMP_STRIDE + (CMP_BLOCK - 1)
        cvalid = cend <= qi * tq + lax.broadcasted_iota(jnp.int32, (ncq, tq), 1)
        psum = jnp.zeros((ncq, tq), F32)
        cmax = []
        for r in range(rheads):
            st = lax.dot_general(kc[0:ncq, :], q[r], NT_DIMS, preferred_element_type=F32)
            st = jnp.where(cvalid, st, -jnp.inf)
            cst_ref[r, 0:ncq, :] = st
            cmax.append(jnp.max(st, axis=0, keepdims=True))
        for r in range(rheads):
            m = jnp.where(jnp.isfinite(cmax[r]), cmax[r], 0.0)
            e = jnp.exp2(cst_ref[r, 0:ncq, :] - m)
            p = e / jnp.maximum(jnp.sum(e, axis=0, keepdims=True), tiny)
            psum += p
            oc_ref[qi * rheads + r] = jnp.dot(vct[:, 0:ncq], p.astype(BF16), preferred_element_type=F32)
        hi, lo = _split_bf16(psum)
        imp = (jnp.dot(ovt[:, 0:ncq], hi, preferred_element_type=F32)
               + jnp.dot(ovt[:, 0:ncq], lo, preferred_element_type=F32))
        blk = lax.broadcasted_iota(jnp.int32, (NSLC_PAD, tq), 0)
        cur = (qi * tq + lax.broadcasted_iota(jnp.int32, (NSLC_PAD, tq), 1)) // SLC_BLOCK
        imp = jnp.where(blk > cur, -jnp.inf, imp)
        forced = (blk == 0) | (blk == cur) | (blk == cur - 1)
        imp = jnp.where(forced, jnp.inf, imp)
        imp_ref[...] = imp
        nvis = min(nslc, (qi + 1) * blocks_per_tile)
        vslab = -(-nvis // SUBLANES)
        slabs = [imp[s * SUBLANES:(s + 1) * SUBLANES, :] for s in range(vslab)]
        ranks = [jnp.zeros((SUBLANES, tq), F32) for _ in range(vslab)]
        sub = lax.broadcasted_iota(jnp.int32, (SUBLANES, tq), 0)
        for jp in range(nvis):
            row = imp_ref[jp:jp + 1, :]
            for s in range(vslab):
                if s * SUBLANES > jp:
                    ahead = jnp.where(row >= slabs[s], 1.0, 0.0)
                elif (s + 1) * SUBLANES - 1 < jp:
                    ahead = jnp.where(row > slabs[s], 1.0, 0.0)
                else:
                    ahead = jnp.where(sub > jp - s * SUBLANES, jnp.where(row >= slabs[s], 1.0, 0.0),
                                      jnp.where(row > slabs[s], 1.0, 0.0))
                ranks[s] = ranks[s] + ahead
        for s in range(vslab):
            start = qi * sel_rows + s * SUBLANES
            sel_ref[start:start + SUBLANES, :] = jnp.where(ranks[s] < nsel, 1.0, 0.0)

    for qi in range(qt):
        select_blocks(qi)
    acc_ref[...] = jnp.zeros_like(acc_ref)
    m_ref[...] = jnp.full(m_ref.shape, -jnp.inf, F32)

    def score_fn(e, buf):
        qi, tile, br = qi_tab[e], tile_tab[e], br_tab[e]
        q = heads_q(qi)
        k = k_ref[br, rows(tile), :]
        base = qi * sel_rows + tile * blocks_per_tile
        chosen = jnp.concatenate(
            [jnp.broadcast_to(sel_ref[pl.ds(base + c, 1), :], (SLC_BLOCK, tq))
             for c in range(blocks_per_tile)], axis=0)
        dpos = (qi - tile) * tq + (qcol - krow)
        reach = jnp.where(br == 0, seq, WINDOW).astype(jnp.uint32)
        kept = jnp.where(jnp.maximum(chosen, br.astype(F32)) > 0.5, dpos, -1)
        valid = kept.astype(jnp.uint32) < reach
        for r in range(rheads):
            row = buf * rheads + r
            cmax_ref[row:row + 1, :] = _scores_t(k, q[r], valid, bufs[buf], r)

    def accum_fn(e, buf):
        qi, tile, br = qi_tab[e], tile_tab[e], br_tab[e]
        vt = vt_ref[0, br, tile]
        for r in range(rheads):
            cm, sr = buf * rheads + r, (qi * 2 + br) * rheads + r
            m_ref[pl.ds(sr, 1), :] = _accumulate_t(bufs[buf], r, cmax_ref[cm:cm + 1, :],
                                                   m_ref[pl.ds(sr, 1), :], acc_ref, sr, vt, True)

    _pipelined_sweep(nentries, score_fn, accum_fn, unroll=8)

    def finish(qi, c):
        outs = []
        for r in range(rheads):
            row0 = (g * rheads + r) * 3
            o = [oc_ref[qi * rheads + r]]
            for br in range(2):
                sr = (qi * 2 + br) * rheads + r
                o.append(acc_ref[sr, 0:HEAD_DIM, :]
                         / jnp.maximum(acc_ref[sr, HEAD_DIM:HEAD_DIM + 1, :], tiny))
            outs.append(sum(gt_ref[0, qi, pl.ds(row0 + j, 1), :] * o[j] for j in range(3)))
        out = jnp.concatenate(outs, axis=0).T
        o_ref[rows(qi), :] = (out * jax.nn.silu(dg_ref[rows(qi), :])).astype(BF16)
        return c

    lax.fori_loop(0, qt, finish, 0)


def _nsa_attn(qe, kc, vct, kboth, vtboth, gt, p1, ovt, batch, seq, tq, dg_blk):
    n = qe.shape[0]
    groups = D_KV_GROUPS
    rheads = D_HEADS // groups
    width = rheads * HEAD_DIM
    qt = seq // tq
    ncp = kc.shape[2]
    nslc = seq // SLC_BLOCK
    sel_rows = -(-nslc // SUBLANES) * SUBLANES
    wtiles = (WINDOW + tq - 1) // tq
    entries = []
    for qi in range(qt):
        entries += [(qi, t, 0) for t in range(qi + 1)]
        entries += [(qi, t, 1) for t in range(max(qi - wtiles, 0), qi + 1)]
    tabs = [jnp.asarray([e[c] for e in entries], jnp.int32) for c in range(3)]
    kernel = functools.partial(_nsa_kernel, tq=tq, qt=qt, nslc=nslc, nsel=min(SLC_TOPK, nslc),
                               nentries=len(entries))
    return pl.pallas_call(
        kernel,
        grid_spec=pltpu.PrefetchScalarGridSpec(
            num_scalar_prefetch=3,
            grid=(batch, groups),
            in_specs=[pl.BlockSpec((seq, width), lambda b, g, *_: (b, g)),
                      pl.BlockSpec((1, 1, ncp, LANES), lambda b, g, *_: (b, g, 0, 0)),
                      pl.BlockSpec((1, 1, HEAD_DIM, ncp), lambda b, g, *_: (b, g, 0, 0)),
                      pl.BlockSpec((2, seq, LANES), lambda b, g, *_: (0, b, g)),
                      pl.BlockSpec((1, 2, qt, HEAD_DIM + ONES_ROWS, tq), lambda b, g, *_: (b, 0, 0, g, 0)),
                      pl.BlockSpec((1, qt, LANES, tq), lambda b, g, *_: (b, 0, 0, 0)),
                      pl.BlockSpec((seq, width), lambda b, g, *_: (b, dg_blk + g)),
                      pl.BlockSpec(ovt.shape, lambda b, g, *_: (0, 0))],
            out_specs=pl.BlockSpec((seq, width), lambda b, g, *_: (b, g)),
            scratch_shapes=[pltpu.VMEM((qt * 2 * rheads, HEAD_DIM + ONES_ROWS, tq), F32),
                            pltpu.VMEM((qt * rheads, HEAD_DIM, tq), F32),
                            pltpu.VMEM((qt * sel_rows, tq), F32),
                            pltpu.VMEM((NSLC_PAD, tq), F32),
                            pltpu.VMEM((rheads, tq, tq), F32),
                            pltpu.VMEM((rheads, tq, tq), F32),
                            pltpu.VMEM((rheads, ncp, tq), F32),
                            pltpu.VMEM((2 * rheads, tq), F32),
                            pltpu.VMEM((qt * 2 * rheads, tq), F32),
                            pltpu.VMEM((seq, rheads * LANES), BF16)]),
        out_shape=jax.ShapeDtypeStruct((n, groups * width), BF16),
        compiler_params=_cparams(("parallel", "parallel"), 56),
        name="nsa_attn",
    )(*tabs, qe, kc, vct, kboth, vtboth, gt, p1, ovt)


def _rope_tables(pos, d):
    rd = d // 4
    half = rd // 2
    inv = ROPE_THETA ** (-jnp.arange(half, dtype=F32) / half)
    ang = pos.astype(F32)[:, None] * inv[None, :]
    cos, sin = jnp.cos(ang), jnp.sin(ang)
    npos = pos.shape[0]
    zeros = jnp.zeros((npos, d), F32)
    c = jnp.concatenate([cos, cos, jnp.ones((npos, d - rd), F32)], axis=1)
    s1 = zeros.at[:, :half].set(-sin)
    s2 = zeros.at[:, half:rd].set(sin)
    return tuple(jnp.tile(t, (1, LANES // d)) for t in (c, s1, s2))


def _overlap_t(ncp, nslc):
    start = np.arange(ncp) * CMP_STRIDE
    s0 = np.arange(nslc) * SLC_BLOCK
    lo = np.maximum(start[:, None], s0[None, :])
    hi = np.minimum(start[:, None] + CMP_BLOCK, s0[None, :] + SLC_BLOCK)
    ov = np.clip(hi - lo, 0, None) / CMP_BLOCK
    ov[ncp - 1] = 0.0
    out = np.zeros((NSLC_PAD, ncp), np.float32)
    out[:nslc] = ov.T
    return jnp.asarray(out, BF16)


def kernel(x, mem, mem_norm, l0_norm, l0_w_in, l0_a_vnorm, l0_a_ws, l0_a_bs, l0_b_qnorm, l0_b_knorm, l0_b_lq1, l0_b_lk1, l0_b_lq2, l0_b_lk2, l0_b_subln, l0_m_wkv, l0_m_qnorm, l0_m_knorm, l0_w_out, l1_norm, l1_w_in, l1_c_conv_w, l1_c_conv_b, l1_c_norm, l1_d_qnorm, l1_d_knorm, l1_d_cmp_pos_k, l1_d_cmp_w1_k, l1_d_cmp_w2_k, l1_d_cmp_pos_v, l1_d_cmp_w1_v, l1_d_cmp_w2_v, l1_m_wkv, l1_m_qnorm, l1_m_knorm, l1_w_out):
    batch, seq, d_model = x.shape
    n = batch * seq
    tm = 256
    tq = 256
    mw = M_HEADS * HEAD_DIM
    pos = jnp.arange(seq, dtype=jnp.int32)
    x0 = x.reshape(n, d_model)

    wkv_both = jnp.concatenate([l0_m_wkv, l1_m_wkv], axis=1).astype(BF16)
    kgains = jnp.stack([jnp.tile(l0_m_knorm, M_HEADS), jnp.tile(l1_m_knorm, M_HEADS)])
    mem_kh, mem_vt = _mem_kv(mem, mem_norm, wkv_both, kgains)

    w = l0_w_in
    tab32 = _rope_tables(pos, B_QK_DIM)
    b_width = 2 * B_HEADS * B_QK_DIM
    a_width = l0_a_vnorm.shape[0]
    bs_exp = jnp.repeat(l0_a_bs.T, a_width // A_GROUPS, axis=1)
    p0, y_a, qn, kn, vt = _layer0_proj(
        x0, l0_norm, w[:, 0:1536].astype(BF16), w[:, 1536:3072].astype(BF16), w[:, 3072:4096].astype(BF16),
        tab32, jnp.tile(l0_b_qnorm, b_width // B_QK_DIM).reshape(1, b_width),
        jnp.tile(l0_b_knorm, b_width // B_QK_DIM).reshape(1, b_width),
        l0_a_ws, bs_exp, l0_a_vnorm.reshape(1, a_width), batch, seq, tq)
    lam_init = 0.8 - 0.6 * math.exp(-0.3 * 1)
    lam_p = jnp.stack([l0_b_lq1, l0_b_lk1, l0_b_lq2, l0_b_lk2])
    y_b = _diff_attn(lam_p, qn, kn, vt, p0, jnp.tile(l0_b_subln, 2).reshape(1, LANES),
                     batch, seq, tq, lam_init, 0)
    y_m = _mem_attn(p0, mem_kh, mem_vt, 0, jnp.tile(l0_m_qnorm, M_HEADS).reshape(1, mw),
                    512 // mw, 768 // mw, seq, 2 * tq)

    w = l1_w_in
    n_bg = 3 * D_HEADS
    w_rest = jnp.concatenate([w[:, 2840:3352], w[:, 2048:2304], w[:, 3352:3864]], axis=1)
    w_d = jnp.concatenate([w[:, 1536:2048], w[:, 2304:2816], w[:, 2816:2840],
                           jnp.zeros((d_model, LANES - n_bg), F32)], axis=1)
    groups, rheads = D_KV_GROUPS, D_HEADS // D_KV_GROUPS
    d_width = D_HEADS * HEAD_DIM
    tab64 = _rope_tables(pos, HEAD_DIM)
    x1, p1, y_c, qe, kboth, vtboth, gt = _layer1_proj(
        x0, y_a, y_b, y_m, l0_w_out.astype(BF16),
        l1_norm, w[:, 0:1536].astype(BF16), w_d.astype(BF16), w_rest.astype(BF16), tab64,
        jnp.tile(l1_d_qnorm, D_HEADS).reshape(1, d_width),
        jnp.tile(l1_d_knorm, LANES // HEAD_DIM).reshape(1, LANES),
        l1_c_conv_w, l1_c_conv_b, l1_c_norm, batch, seq, tq)
    kv_blk = 512 // LANES

    ncp = seq // CMP_STRIDE
    cmp_pos = jnp.arange(ncp, dtype=jnp.int32) * CMP_STRIDE + (CMP_BLOCK - 1)
    hidden = l1_d_cmp_w2_k.shape[0]

    def group_w1(w1):
        w = w1.reshape(CMP_BLOCK, HEAD_DIM, hidden)
        return jnp.stack([jnp.zeros((CMP_BLOCK, LANES, hidden), F32)
                          .at[:, g * HEAD_DIM:(g + 1) * HEAD_DIM, :].set(w) for g in range(groups)])

    w2k = jnp.tile(l1_d_cmp_w2_k, (1, LANES // HEAD_DIM))
    w2v = jnp.zeros((hidden, LANES), F32).at[:, :HEAD_DIM].set(l1_d_cmp_w2_v)
    kc, vct = _compress(
        p1, kv_blk, jnp.tile(l1_d_cmp_pos_k, (1, groups)), jnp.tile(l1_d_cmp_pos_v, (1, groups)),
        group_w1(l1_d_cmp_w1_k).astype(BF16), w2k.astype(BF16),
        group_w1(l1_d_cmp_w1_v).astype(BF16), w2v.astype(BF16),
        jnp.tile(l1_d_knorm, LANES // HEAD_DIM).reshape(1, LANES), _rope_tables(cmp_pos, HEAD_DIM),
        batch, seq)

    nslc = seq // SLC_BLOCK
    y_d = _nsa_attn(qe, kc, vct, kboth, vtboth, gt, p1, _overlap_t(ncp, nslc),
                    batch, seq, tq, 0)
    out = _mem_out(x1, y_c, y_d, p1, mem_kh, mem_vt, 1, jnp.tile(l1_m_qnorm, M_HEADS).reshape(1, mw),
                   768 // mw, 1024 // mw, l1_w_out.astype(BF16), seq, 2 * tq)
    return out.reshape(batch, seq, d_model)
```

```python
import functools
import math

import numpy as np
import jax
import jax.numpy as jnp
from jax import lax
from jax.experimental import pallas as pl
from jax.experimental.pallas import tpu as pltpu

F32 = jnp.float32
BF16 = jnp.bfloat16

EPS = 1e-6
ROPE_THETA = 500000.0
HEAD_DIM = 64
A_CHUNK = 128
A_GROUPS = 4
B_HEADS = 8
B_QK_DIM = 32
C_KERNEL = 31
D_HEADS = 8
D_KV_GROUPS = 2
CMP_BLOCK = 32
CMP_STRIDE = 16
SLC_BLOCK = 64
SLC_TOPK = 16
WINDOW = 512
M_HEADS = 4

LANES = 128
SUBLANES = 8
MXU_DIM = 256
HALO = 32
NSLC_PAD = 128
ONES_ROWS = 16

NT_DIMS = (((1,), (1,)), ((), ()))
LOG2E = math.log2(math.e)


def _cparams(semantics, vmem_mb):
    return pltpu.CompilerParams(dimension_semantics=semantics,
                                vmem_limit_bytes=vmem_mb * 1024 * 1024)


def _split_bf16(x):
    hi = x.astype(BF16)
    lo = (x - hi.astype(F32)).astype(BF16)
    return hi, lo


def _group_mean(x2, gsize):
    t, c = x2.shape
    w = min(c, MXU_DIM)
    r = lax.broadcasted_iota(jnp.int32, (w, w), 0) // gsize
    cc = lax.broadcasted_iota(jnp.int32, (w, w), 1) // gsize
    ones = jnp.where(r == cc, 1.0, 0.0).astype(BF16)
    outs = []
    for s in range(c // w):
        hi, lo = _split_bf16(x2[:, s * w:(s + 1) * w])
        outs.append(jnp.dot(hi, ones, preferred_element_type=F32)
                    + jnp.dot(lo, ones, preferred_element_type=F32))
    out = outs[0] if len(outs) == 1 else jnp.concatenate(outs, axis=1)
    return out * (1.0 / gsize)


def _tile_lanes(t, width):
    rep = width // t.shape[1]
    return t if rep == 1 else jnp.concatenate([t] * rep, axis=1)


def _rope(x, cos_t, s1_t, s2_t, half):
    width = x.shape[1]
    up = pltpu.roll(x, width - half, 1)
    dn = pltpu.roll(x, half, 1)
    return (x * _tile_lanes(cos_t, width) + up * _tile_lanes(s1_t, width)
            + dn * _tile_lanes(s2_t, width))


def _matmul_cols(h, w_ref, lo, hi, out_ref=None):
    r = jnp.dot(h, w_ref[:, lo:hi], preferred_element_type=F32)
    if out_ref is None:
        return r
    out_ref[:, lo:hi] = r


def _rms_rows(x, ng_ref):
    ms = jnp.mean(x * x, axis=-1, keepdims=True)
    return (x * lax.rsqrt(ms + EPS) * ng_ref[...]).astype(BF16)


def _scores_t(k, q, valid, st_ref, idx):
    st = lax.dot_general(k, q, NT_DIMS, preferred_element_type=F32)
    if isinstance(valid, (list, tuple)):
        blk = st.shape[0] // len(valid)
        st = jnp.concatenate([jnp.where(v, st[c * blk:(c + 1) * blk, :], -jnp.inf)
                              for c, v in enumerate(valid)], axis=0)
    elif valid is not None:
        st = jnp.where(valid, st, -jnp.inf)
    st_ref[idx] = st
    return jnp.max(st, axis=0, keepdims=True)


def _accumulate_t(st_ref, idx, cmax, m, acc_ref, aidx, vt, guard_empty):
    m_new = jnp.maximum(m, cmax)
    m_use = jnp.where(m_new == -jnp.inf, 0.0, m_new) if guard_empty else m_new
    alpha = jnp.exp2(m - m_use)
    pt = jnp.exp2((st_ref[idx] - m_use).astype(BF16))
    acc_ref[aidx] = alpha * acc_ref[aidx] + jnp.dot(vt, pt, preferred_element_type=F32)
    return m_new


def _pipelined_sweep(n, score_fn, accum_fn, unroll=4):
    score_fn(0, 0)
    stages = n - 1

    def stage(e, parity):
        score_fn(e + 1, 1 - parity)
        accum_fn(e, parity)

    def body(i, c):
        for k in range(unroll):
            stage(unroll * i + k, k % 2)
        return c

    if isinstance(n, int):
        iters, left = divmod(stages, unroll)
        if iters:
            lax.fori_loop(0, iters, body, 0)
        for k in range(left):
            stage(iters * unroll + k, k % 2)
        accum_fn(n - 1, left % 2)
        return
    lax.fori_loop(0, stages // unroll, body, 0)
    base = stages // unroll * unroll
    left = stages - base
    for k in range(unroll - 1):
        pl.when(left > k)(functools.partial(stage, base + k, k % 2))
    for parity in range(2):
        pl.when(left % 2 == parity)(functools.partial(accum_fn, n - 1, parity))


def _mix_out(ya_ref, yb_ref, ym_ref, w_ref):
    wa = ya_ref.shape[1]
    wb = yb_ref.shape[1]
    acc = jnp.dot(ya_ref[...], w_ref[0:wa, :], preferred_element_type=F32)
    acc += jnp.dot(yb_ref[...], w_ref[wa:wa + wb, :], preferred_element_type=F32)
    acc += jnp.dot(ym_ref[...], w_ref[wa + wb:, :], preferred_element_type=F32)
    return acc


def _sgu_group(u, v, gate, w_ref, bs, vgain, o_ref, cols):
    tm = u.shape[0]
    row = lax.broadcasted_iota(jnp.int32, (A_CHUNK, A_CHUNK), 0)
    col = lax.broadcasted_iota(jnp.int32, (A_CHUNK, A_CHUNK), 1)
    w = jnp.where(col <= row, w_ref, 0.0).astype(BF16)
    gv = jax.nn.gelu(v)
    ms = jnp.mean(gv * gv, axis=-1, keepdims=True)
    vn = (gv * lax.rsqrt(ms + EPS) * vgain).astype(BF16)
    gu = jax.nn.gelu(u) * jax.nn.silu(gate)
    for c in range(tm // A_CHUNK):
        rows = slice(c * A_CHUNK, (c + 1) * A_CHUNK)
        z = jnp.dot(w, vn[rows, :], preferred_element_type=F32) + bs
        o_ref[rows, cols] = (gu[rows, :] * z).astype(BF16)


def _layer0_proj_kernel(x_ref, ng_ref, wa_ref, wqkv_ref, w_ref, c_ref, s1_ref, s2_ref, qg_ref, kg_ref,
                        ws_ref, bs_ref, vg_ref, p_ref, ya_ref, qo_ref, ko_ref, vto_ref):
    h = _rms_rows(x_ref[...], ng_ref)
    aw = vg_ref.shape[1]
    gdim = aw // A_GROUPS
    width = qg_ref.shape[1]
    half = B_QK_DIM // 4 // 2
    pw = p_ref.shape[1]

    a = _matmul_cols(h, wa_ref, 0, 3 * aw)

    def sgu(g):
        cols = slice(g * gdim, (g + 1) * gdim)
        _sgu_group(a[:, cols], a[:, aw + g * gdim:aw + (g + 1) * gdim],
                   a[:, 2 * aw + g * gdim:2 * aw + (g + 1) * gdim],
                   ws_ref[g], bs_ref[:, cols], vg_ref[:, cols], ya_ref, cols)

    def prep(t, gain, dst, mul):
        tn = t * lax.rsqrt(_group_mean(t * t, B_QK_DIM) + EPS) * gain[...]
        tr = _rope(tn, c_ref[...], s1_ref[...], s2_ref[...], half)
        dst[...] = (tr * mul if mul != 1.0 else tr).astype(BF16)

    q = _matmul_cols(h, wqkv_ref, 0, width)
    sgu(0)
    k = _matmul_cols(h, wqkv_ref, width, 2 * width)
    sgu(1)
    v = _matmul_cols(h, wqkv_ref, 2 * width, 3 * width)
    sgu(2)
    _matmul_cols(h, w_ref, 0, pw // 2, p_ref)
    sgu(3)
    prep(q, qg_ref, qo_ref, B_QK_DIM ** -0.5 * LOG2E)
    _matmul_cols(h, w_ref, pw // 2, pw, p_ref)
    prep(k, kg_ref, ko_ref, 1.0)
    vt = v.T
    dv = 2 * B_QK_DIM
    ones = jnp.ones((ONES_ROWS, vt.shape[1]), F32)
    parts = []
    for hd in range(B_HEADS):
        parts += [vt[hd * dv:(hd + 1) * dv, :], ones]
    vto_ref[0, 0] = jnp.concatenate(parts, axis=0).astype(BF16)


def _layer0_proj(x, norm_g, w_a, w_qkv, w_rest, tables, qgain, kgain, a_ws, bs_exp, vgain, batch, seq, tm):
    n, d = x.shape
    width = qgain.shape[1]
    aw = vgain.shape[1]
    tiles_per_seq = seq // tm
    tab_spec = pl.BlockSpec((tm, LANES), lambda i: (i % tiles_per_seq, 0))
    vrows = B_HEADS * (2 * B_QK_DIM + ONES_ROWS)

    def full(a):
        return pl.BlockSpec(a.shape, lambda i: (0,) * a.ndim)

    def rows(w):
        return pl.BlockSpec((tm, w), lambda i: (i, 0))

    return pl.pallas_call(
        _layer0_proj_kernel,
        grid=(n // tm,),
        in_specs=[rows(d), pl.BlockSpec((1, d), lambda i: (0, 0)), full(w_a), full(w_qkv), full(w_rest),
                  tab_spec, tab_spec, tab_spec, full(qgain), full(kgain),
                  full(a_ws), full(bs_exp), full(vgain)],
        out_specs=[rows(w_rest.shape[1]), rows(aw), rows(width), rows(width),
                   pl.BlockSpec((1, 1, vrows, tm),
                                lambda i: (i // tiles_per_seq, i % tiles_per_seq, 0, 0))],
        out_shape=[jax.ShapeDtypeStruct((n, w_rest.shape[1]), F32),
                   jax.ShapeDtypeStruct((n, aw), BF16),
                   jax.ShapeDtypeStruct((n, width), BF16), jax.ShapeDtypeStruct((n, width), BF16),
                   jax.ShapeDtypeStruct((batch, tiles_per_seq, vrows, tm), BF16)],
        compiler_params=_cparams(("parallel",), 48),
        name="layer0_proj",
    )(x, norm_g.reshape(1, d), w_a, w_qkv, w_rest, *tables, qgain, kgain, a_ws, bs_exp, vgain)


def _diff_attn_kernel(qi_tab, tile_tab, lam_ref, q_ref, k_ref, vt_ref, g_ref, sub_ref, o_ref,
                      acc_ref, sta_ref, stb_ref, cmax_ref, m_ref, *, tq, qt, lam_init):
    dv = 2 * B_QK_DIM
    vrows = dv + ONES_ROWS
    nmap = 4
    lane = lax.broadcasted_iota(jnp.int32, (1, LANES), 1)
    bufs = (sta_ref, stb_ref)
    acc_ref[...] = jnp.zeros_like(acc_ref)
    m_ref[...] = jnp.full(m_ref.shape, -jnp.inf, F32)
    krow = lax.broadcasted_iota(jnp.int32, (tq, tq), 0)
    qcol = lax.broadcasted_iota(jnp.int32, (tq, tq), 1)
    lam_p = lam_ref[...]
    lam = (jnp.exp(jnp.sum(lam_p[0:1] * lam_p[1:2], axis=-1, keepdims=True))
           - jnp.exp(jnp.sum(lam_p[2:3] * lam_p[3:4], axis=-1, keepdims=True)) + lam_init)

    def rows(tile):
        return pl.ds(pl.multiple_of(tile * tq, tq), tq)

    def scores(qi, tile, buf, valid):
        q = q_ref[rows(qi), :]
        k = k_ref[rows(tile), :]
        zero = jnp.zeros_like(q)
        for j in range(nmap):
            qm = jnp.where(lane // B_QK_DIM == j, q, zero)
            row = buf * nmap + j
            cmax_ref[row:row + 1, :] = _scores_t(k, qm, valid, bufs[buf], j)

    def accumulate(qi, tile, buf):
        for j in range(nmap):
            vt = vt_ref[0, tile, (j // 2) * vrows:(j // 2 + 1) * vrows, :]
            cm, sr = buf * nmap + j, qi * nmap + j
            m_ref[pl.ds(sr, 1), :] = _accumulate_t(bufs[buf], j, cmax_ref[cm:cm + 1, :],
                                                   m_ref[pl.ds(sr, 1), :], acc_ref, sr, vt, False)

    def finish(qi):
        halves = []
        for h in range(2):
            o = [acc_ref[qi * nmap + 2 * h + mp, 0:dv, :] / acc_ref[qi * nmap + 2 * h + mp, dv:dv + 1, :]
                 for mp in range(2)]
            pd = o[0] - lam * o[1]
            ms2 = jnp.mean(pd * pd, axis=0, keepdims=True)
            halves.append(pd * lax.rsqrt(ms2 + EPS))
        ob = jnp.concatenate(halves, axis=0).T * sub_ref[...] * (1.0 - lam_init)
        o_ref[rows(qi), :] = (ob * jax.nn.silu(g_ref[rows(qi), :])).astype(BF16)

    def below_score(e, buf):
        scores(qi_tab[e], tile_tab[e], buf, None)

    def below_accum(e, buf):
        accumulate(qi_tab[e], tile_tab[e], buf)

    def diag_score(e, buf):
        scores(e, e, buf, krow <= qcol)

    def diag_accum(e, buf):
        accumulate(e, e, buf)
        finish(e)

    _pipelined_sweep(qt * (qt - 1) // 2, below_score, below_accum, unroll=8)
    _pipelined_sweep(qt, diag_score, diag_accum)


def _diff_attn(lam_p, qn, kn, vt, p0, subln, batch, seq, tq, lam_init, g_off):
    n, width = qn.shape
    pairs = width // LANES
    qt = seq // tq
    vrows = 2 * B_QK_DIM + ONES_ROWS
    below = [(qi, t) for qi in range(qt) for t in range(qi)]
    qi_tab = jnp.asarray([e[0] for e in below], jnp.int32)
    tile_tab = jnp.asarray([e[1] for e in below], jnp.int32)
    kernel = functools.partial(_diff_attn_kernel, tq=tq, qt=qt, lam_init=lam_init)
    seq_spec = pl.BlockSpec((seq, LANES), lambda b, p, *_: (b, p))
    return pl.pallas_call(
        kernel,
        grid_spec=pltpu.PrefetchScalarGridSpec(
            num_scalar_prefetch=2,
            grid=(batch, pairs),
            in_specs=[pl.BlockSpec(lam_p.shape, lambda b, p, *_: (0, 0)),
                      seq_spec, seq_spec,
                      pl.BlockSpec((1, qt, 2 * vrows, tq), lambda b, p, *_: (b, 0, p, 0)),
                      pl.BlockSpec((seq, LANES), lambda b, p, *_: (b, g_off + p)),
                      pl.BlockSpec((1, LANES), lambda b, p, *_: (0, 0))],
            out_specs=seq_spec,
            scratch_shapes=[pltpu.VMEM((qt * 4, vrows, tq), F32), pltpu.VMEM((4, tq, tq), F32),
                            pltpu.VMEM((4, tq, tq), F32), pltpu.VMEM((8, tq), F32),
                            pltpu.VMEM((qt * 4, tq), F32)]),
        out_shape=jax.ShapeDtypeStruct((n, width), BF16),
        compiler_params=_cparams(("parallel", "parallel"), 40),
        name="diff_attn",
    )(qi_tab, tile_tab, lam_p, qn, kn, vt, p0, subln)


def _mem_kv_kernel(mem_ref, g_ref, w_ref, kg_ref, kh_ref, vt_ref):
    x = mem_ref[0]
    ms = jnp.mean(x * x, axis=-1, keepdims=True)
    h = (x * lax.rsqrt(ms + EPS) * g_ref[...]).astype(BF16)
    kv = jnp.dot(h, w_ref[...], preferred_element_type=F32)
    mw = kv.shape[1] // 4
    lane = lax.broadcasted_iota(jnp.int32, (1, mw), 1)
    ones = jnp.ones((ONES_ROWS, kv.shape[0]), F32)
    for layer in range(2):
        k = kv[:, (2 * layer) * mw:(2 * layer + 1) * mw]
        kn = k * lax.rsqrt(_group_mean(k * k, HEAD_DIM) + EPS) * kg_ref[layer:layer + 1, :]
        vt = kv[:, (2 * layer + 1) * mw:(2 * layer + 2) * mw].T
        parts = []
        for hd in range(M_HEADS):
            kh_ref[layer, 0, hd] = jnp.where(lane // HEAD_DIM == hd, kn, 0.0).astype(BF16)
            parts += [vt[hd * HEAD_DIM:(hd + 1) * HEAD_DIM, :], ones]
        vt_ref[layer, 0] = jnp.concatenate(parts, axis=0).astype(BF16)


def _mem_kv(mem, mem_norm, wkv_both, kgains):
    batch, mtok, d = mem.shape
    mw = wkv_both.shape[1] // 4
    vrows = M_HEADS * (HEAD_DIM + ONES_ROWS)
    return pl.pallas_call(
        _mem_kv_kernel,
        grid=(batch,),
        in_specs=[pl.BlockSpec((1, mtok, d), lambda b: (b, 0, 0)),
                  pl.BlockSpec((1, d), lambda b: (0, 0)),
                  pl.BlockSpec(wkv_both.shape, lambda b: (0, 0)),
                  pl.BlockSpec(kgains.shape, lambda b: (0, 0))],
        out_specs=[pl.BlockSpec((2, 1, M_HEADS, mtok, mw), lambda b: (0, b, 0, 0, 0)),
                   pl.BlockSpec((2, 1, vrows, mtok), lambda b: (0, b, 0, 0))],
        out_shape=[jax.ShapeDtypeStruct((2, batch, M_HEADS, mtok, mw), BF16),
                   jax.ShapeDtypeStruct((2, batch, vrows, mtok), BF16)],
        compiler_params=_cparams(("parallel",), 32),
        name="mem_kv",
    )(mem, mem_norm.reshape(1, d), wkv_both, kgains)


def _mem_attn_tile(q_ref, g_ref, kh_ref, vt_ref, qg_ref, st_ref):
    vrows = HEAD_DIM + ONES_ROWS
    x = q_ref[...]
    qn = (x * lax.rsqrt(_group_mean(x * x, HEAD_DIM) + EPS) * qg_ref[...]
          * (HEAD_DIM ** -0.5 * LOG2E)).astype(BF16)
    cmax = [_scores_t(kh_ref[0, 0, h], qn, None, st_ref, h) for h in range(M_HEADS)]
    outs = []
    for h in range(M_HEADS):
        pt = jnp.exp2((st_ref[h] - cmax[h]).astype(BF16))
        acc = jnp.dot(vt_ref[0, 0, h * vrows:(h + 1) * vrows, :], pt, preferred_element_type=F32)
        outs.append(acc[0:HEAD_DIM, :] / acc[HEAD_DIM:HEAD_DIM + 1, :])
    out = jnp.concatenate(outs, axis=0).T
    return (out * jax.nn.silu(g_ref[...])).astype(BF16)


def _mem_attn_kernel(q_ref, g_ref, kh_ref, vt_ref, qg_ref, o_ref, st_ref):
    o_ref[...] = _mem_attn_tile(q_ref, g_ref, kh_ref, vt_ref, qg_ref, st_ref)


def _mem_out_kernel(x_ref, ya_ref, yb_ref, q_ref, g_ref, kh_ref, vt_ref, qg_ref, w_ref, o_ref, st_ref):
    ym = _mem_attn_tile(q_ref, g_ref, kh_ref, vt_ref, qg_ref, st_ref)
    wa = ya_ref.shape[1]
    wb = yb_ref.shape[1]
    acc = jnp.dot(ya_ref[...], w_ref[0:wa, :], preferred_element_type=F32)
    acc += jnp.dot(yb_ref[...], w_ref[wa:wa + wb, :], preferred_element_type=F32)
    acc += jnp.dot(ym, w_ref[wa + wb:, :], preferred_element_type=F32)
    o_ref[...] = x_ref[...] + acc


def _mem_out(x, ya, yb, p, kh, vt, layer, qgain, q_blk, g_blk, w, seq, tq):
    n, d = x.shape
    _, batch, heads, mtok, mw = kh.shape
    qt = seq // tq

    def rows(width, blk=0):
        return pl.BlockSpec((tq, width), lambda i: (i, blk))

    return pl.pallas_call(
        _mem_out_kernel,
        grid=(n // tq,),
        in_specs=[rows(d), rows(ya.shape[1]), rows(yb.shape[1]), rows(mw, q_blk), rows(mw, g_blk),
                  pl.BlockSpec((1, 1, heads, mtok, mw), lambda i: (layer, i // qt, 0, 0, 0)),
                  pl.BlockSpec((1, 1, vt.shape[2], mtok), lambda i: (layer, i // qt, 0, 0)),
                  pl.BlockSpec((1, mw), lambda i: (0, 0)),
                  pl.BlockSpec(w.shape, lambda i: (0, 0))],
        out_specs=rows(d),
        out_shape=jax.ShapeDtypeStruct((n, d), F32),
        scratch_shapes=[pltpu.VMEM((heads, mtok, tq), F32)],
        compiler_params=_cparams(("parallel",), 40),
        name="mem_out_proj",
    )(x, ya, yb, p, p, kh, vt, qgain, w)


def _mem_attn(p, kh, vt, layer, qgain, q_blk, g_blk, seq, tq):
    n = p.shape[0]
    _, batch, heads, mtok, mw = kh.shape
    qt = seq // tq
    return pl.pallas_call(
        _mem_attn_kernel,
        grid=(n // tq,),
        in_specs=[pl.BlockSpec((tq, mw), lambda i: (i, q_blk)),
                  pl.BlockSpec((tq, mw), lambda i: (i, g_blk)),
                  pl.BlockSpec((1, 1, heads, mtok, mw), lambda i: (layer, i // qt, 0, 0, 0)),
                  pl.BlockSpec((1, 1, vt.shape[2], mtok), lambda i: (layer, i // qt, 0, 0)),
                  pl.BlockSpec((1, mw), lambda i: (0, 0))],
        out_specs=pl.BlockSpec((tq, mw), lambda i: (i, 0)),
        out_shape=jax.ShapeDtypeStruct((n, mw), BF16),
        scratch_shapes=[pltpu.VMEM((heads, mtok, tq), F32)],
        compiler_params=_cparams(("parallel",), 32),
        name="mem_attn",
    )(p, p, kh, vt, qgain)


def _layer1_proj_kernel(x_ref, ya_ref, yb_ref, ym_ref, wo_ref, ng_ref, wc_ref, wd_ref, w_ref,
                        c_ref, s1_ref, s2_ref, qg_ref, kg_ref, cw_ref, cb_ref, cn_ref,
                        x1_ref, p_ref, yc_ref, qo_ref, ko_ref, vto_ref, gto_ref,
                        hbuf_ref, shift_ref, *, tiles_per_seq):
    i = pl.program_id(0)
    ts = x_ref.shape[0]
    span = ts + HALO
    cwid = cn_ref.shape[1]
    qw = qg_ref.shape[1]
    pw = p_ref.shape[1]
    half = HEAD_DIM // 4 // 2
    x1 = x_ref[...] + _mix_out(ya_ref, yb_ref, ym_ref, wo_ref)
    x1_ref[...] = x1
    h = _rms_rows(x1, ng_ref)

    @pl.when(i % tiles_per_seq == 0)
    def _():
        hbuf_ref[0:HALO, :] = jnp.zeros((HALO, cwid), F32)

    ca = _matmul_cols(h, wc_ref, 0, cwid)
    cbv = _matmul_cols(h, wc_ref, cwid, 2 * cwid)
    q = _matmul_cols(h, wd_ref, 0, qw)
    hbuf_ref[HALO:HALO + ts, :] = ca * jax.nn.sigmoid(cbv)
    for ph in range(SUBLANES):
        shift_ref[ph, 0:span - ph, :] = hbuf_ref[ph:span, :]
    cg = _matmul_cols(h, wc_ref, 2 * cwid, 3 * cwid)
    d = _matmul_cols(h, wd_ref, qw, wd_ref.shape[1])
    first = HALO - (C_KERNEL - 1)
    y = cb_ref[...]
    for j in range(C_KERNEL):
        ph, base = (first + j) % SUBLANES, (first + j) // SUBLANES * SUBLANES
        y = y + shift_ref[ph, base:base + ts, :] * cw_ref[j:j + 1, :]
    _matmul_cols(h, w_ref, 0, pw // 2, p_ref)
    ms = jnp.mean(y * y, axis=-1, keepdims=True)
    yn = y * lax.rsqrt(ms + EPS) * cn_ref[...]
    yc_ref[...] = (jax.nn.silu(yn) * jax.nn.silu(cg)).astype(BF16)
    hbuf_ref[0:HALO, :] = hbuf_ref[ts:ts + HALO, :]
    _matmul_cols(h, w_ref, pw // 2, pw, p_ref)

    tabs = (c_ref[...], s1_ref[...], s2_ref[...])
    tn = q * lax.rsqrt(_group_mean(q * q, HEAD_DIM) + EPS) * qg_ref[...]
    qo_ref[...] = (_rope(tn, *tabs, half) * (HEAD_DIM ** -0.5 * LOG2E)).astype(BF16)
    ones = jnp.ones((ONES_ROWS, ts), F32)
    lane = lax.broadcasted_iota(jnp.int32, (1, LANES), 1)
    for br in range(2):
        k = d[:, 2 * br * LANES:(2 * br + 1) * LANES]
        kn = k * lax.rsqrt(_group_mean(k * k, HEAD_DIM) + EPS) * kg_ref[...]
        kr = _rope(kn, *tabs, half)
        swapped = pltpu.roll(kr, HEAD_DIM, 1)
        for g in range(D_KV_GROUPS):
            own = (lane // HEAD_DIM) == g
            ko_ref[br, :, g * LANES:(g + 1) * LANES] = jnp.where(own, kr, swapped).astype(BF16)
        vt = d[:, (2 * br + 1) * LANES:(2 * br + 2) * LANES].T
        parts = []
        for g in range(D_KV_GROUPS):
            parts += [vt[g * HEAD_DIM:(g + 1) * HEAD_DIM, :], ones]
        vto_ref[0, br, 0] = jnp.concatenate(parts, axis=0).astype(BF16)
    gto_ref[0, 0] = jax.nn.sigmoid(d[:, 4 * LANES:5 * LANES]).T


def _layer1_proj(x, ya, yb, ym, w_out, norm_g, w_c, w_d, w_rest, tables, qgain, kgain,
                 conv_w, conv_b, conv_norm, batch, seq, tm):
    n, dm = x.shape
    cwid = conv_w.shape[1]
    tiles_per_seq = seq // tm
    tab_spec = pl.BlockSpec((tm, LANES), lambda i: (i % tiles_per_seq, 0))
    vrows = D_KV_GROUPS * (HEAD_DIM + ONES_ROWS)

    def full(a):
        return pl.BlockSpec(a.shape, lambda i: (0,) * a.ndim)

    def rows(w):
        return pl.BlockSpec((tm, w), lambda i: (i, 0))

    kernel = functools.partial(_layer1_proj_kernel, tiles_per_seq=tiles_per_seq)
    return pl.pallas_call(
        kernel,
        grid=(n // tm,),
        in_specs=[rows(dm), rows(ya.shape[1]), rows(yb.shape[1]), rows(ym.shape[1]), full(w_out),
                  pl.BlockSpec((1, dm), lambda i: (0, 0)), full(w_c), full(w_d), full(w_rest),
                  tab_spec, tab_spec, tab_spec, full(qgain), full(kgain),
                  full(conv_w), pl.BlockSpec((1, cwid), lambda i: (0, 0)),
                  pl.BlockSpec((1, cwid), lambda i: (0, 0))],
        out_specs=[rows(dm), rows(w_rest.shape[1]), rows(cwid), rows(qgain.shape[1]),
                   pl.BlockSpec((2, tm, D_KV_GROUPS * LANES), lambda i: (0, i, 0)),
                   pl.BlockSpec((1, 2, 1, vrows, tm),
                                lambda i: (i // tiles_per_seq, 0, i % tiles_per_seq, 0, 0)),
                   pl.BlockSpec((1, 1, LANES, tm),
                                lambda i: (i // tiles_per_seq, i % tiles_per_seq, 0, 0))],
        out_shape=[jax.ShapeDtypeStruct((n, dm), F32),
                   jax.ShapeDtypeStruct((n, w_rest.shape[1]), F32),
                   jax.ShapeDtypeStruct((n, cwid), BF16),
                   jax.ShapeDtypeStruct((n, qgain.shape[1]), BF16),
                   jax.ShapeDtypeStruct((2, n, D_KV_GROUPS * LANES), BF16),
                   jax.ShapeDtypeStruct((batch, 2, tiles_per_seq, vrows, tm), BF16),
                   jax.ShapeDtypeStruct((batch, tiles_per_seq, LANES, tm), F32)],
        scratch_shapes=[pltpu.VMEM((tm + HALO, cwid), F32),
                        pltpu.VMEM((SUBLANES, tm + HALO, cwid), F32)],
        compiler_params=_cparams(("arbitrary",), 52),
        name="layer1_proj",
    )(x, ya, yb, ym, w_out, norm_g.reshape(1, dm), w_c, w_d, w_rest, *tables, qgain, kgain, conv_w,
      conv_b.reshape(1, cwid), conv_norm.reshape(1, cwid))


def _compress_kernel(tk_ref, tv_ref, pk_ref, pv_ref, w1k_ref, w2k_ref, w1v_ref, w2v_ref,
                     kg_ref, c_ref, s1_ref, s2_ref, kc_ref, vct_ref):
    half = HEAD_DIM // 4 // 2
    ncp = kc_ref.shape[2]

    def mlp(t_ref, p_ref, w1_ref, w2):
        a = jnp.zeros((ncp, w1_ref.shape[3]), F32)
        b = jnp.zeros((ncp, w1_ref.shape[3]), F32)
        for l in range(CMP_STRIDE):
            x = t_ref[pl.ds(l, ncp, stride=CMP_STRIDE), :]
            a += jnp.dot((x + p_ref[l:l + 1, :]).astype(BF16), w1_ref[0, l],
                         preferred_element_type=F32)
            b += jnp.dot((x + p_ref[CMP_STRIDE + l:CMP_STRIDE + l + 1, :]).astype(BF16),
                         w1_ref[0, CMP_STRIDE + l], preferred_element_type=F32)
        h = a + pltpu.roll(b, ncp - 1, 0)
        return jnp.dot(jax.nn.silu(h).astype(BF16), w2, preferred_element_type=F32)

    kc = mlp(tk_ref, pk_ref, w1k_ref, w2k_ref[...])
    kn = kc * lax.rsqrt(_group_mean(kc * kc, HEAD_DIM) + EPS) * kg_ref[...]
    kc_ref[0, 0] = _rope(kn, c_ref[...], s1_ref[...], s2_ref[...], half).astype(BF16)
    vc = mlp(tv_ref, pv_ref, w1v_ref, w2v_ref[...])
    vct_ref[0, 0] = vc.T[0:HEAD_DIM, :].astype(BF16)


def _compress(p1, k_blk, pk, pv, w1k, w2k, w1v, w2v, kgain, tables, batch, seq):
    groups = w1k.shape[0]
    ncp = seq // CMP_STRIDE

    def full(a):
        return pl.BlockSpec(a.shape, lambda b, g: (0,) * a.ndim)

    def per_group(a):
        return pl.BlockSpec((1,) + a.shape[1:], lambda b, g: (g,) + (0,) * (a.ndim - 1))

    return pl.pallas_call(
        _compress_kernel,
        grid=(batch, groups),
        in_specs=[pl.BlockSpec((seq, LANES), lambda b, g: (b, k_blk)),
                  pl.BlockSpec((seq, LANES), lambda b, g: (b, k_blk + 1)),
                  full(pk), full(pv), per_group(w1k), full(w2k), per_group(w1v), full(w2v),
                  full(kgain), full(tables[0]), full(tables[1]), full(tables[2])],
        out_specs=[pl.BlockSpec((1, 1, ncp, LANES), lambda b, g: (b, g, 0, 0)),
                   pl.BlockSpec((1, 1, HEAD_DIM, ncp), lambda b, g: (b, g, 0, 0))],
        out_shape=[jax.ShapeDtypeStruct((batch, groups, ncp, LANES), BF16),
                   jax.ShapeDtypeStruct((batch, groups, HEAD_DIM, ncp), BF16)],
        compiler_params=_cparams(("parallel", "parallel"), 32),
        name="nsa_compress",
    )(p1, p1, pk, pv, w1k, w2k, w1v, w2v, kgain, *tables)


def _nsa_kernel(qi_tab, tile_tab, br_tab, q_ref, kc_ref, vct_ref, k_ref, vt_ref, gt_ref,
                dg_ref, ovt_ref, o_ref, acc_ref, oc_ref, sel_ref, imp_ref, sta_ref, stb_ref, cst_ref,
                cmax_ref, m_ref, qh_ref, *, tq, qt, nslc, nsel, nbelow, nentries):
    g = pl.program_id(1)
    rheads = D_HEADS // D_KV_GROUPS
    seq = qt * tq
    tiny = float(np.finfo(np.float32).tiny)
    qcol = lax.broadcasted_iota(jnp.int32, (tq, tq), 1)
    krow = lax.broadcasted_iota(jnp.int32, (tq, tq), 0)
    bufs = (sta_ref, stb_ref)
    blocks_per_tile = tq // SLC_BLOCK
    nslab = -(-nslc // SUBLANES)
    sel_rows = nslab * SUBLANES

    def rows(tile):
        return pl.ds(tile * tq if isinstance(tile, int) else pl.multiple_of(tile * tq, tq), tq)

    half_lane = lax.broadcasted_iota(jnp.int32, (1, LANES), 1) // HEAD_DIM

    def split_heads(t, c):
        for r in range(rheads):
            blk = q_ref[rows(t), (r // 2) * LANES:(r // 2 + 1) * LANES]
            qh_ref[rows(t), r * LANES:(r + 1) * LANES] = jnp.where(half_lane == r % 2, blk,
                                                                  jnp.zeros_like(blk))
        return c

    lax.fori_loop(0, qt, split_heads, 0)

    def heads_q(qi):
        return [qh_ref[rows(qi), r * LANES:(r + 1) * LANES] for r in range(rheads)]

    kc = kc_ref[0, 0]
    vct = vct_ref[0, 0]
    ncp = kc.shape[0]
    ovt = ovt_ref[...]

    def select_blocks(qi):
        q = heads_q(qi)
        ncq = min(ncp, -(-(((qi + 1) * tq - CMP_BLOCK) // CMP_STRIDE + 1) // LANES) * LANES)
        cend = lax.broadcasted_iota(jnp.int32, (ncq, tq), 0) * CMP_STRIDE + (CMP_BLOCK - 1)
        cvalid = cend <= qi * tq + lax.broadcasted_iota(jnp.int32, (ncq, tq), 1)
        psum = jnp.zeros((ncq, tq), F32)
        cmax = []
        for r in range(rheads):
            st = lax.dot_general(kc[0:ncq, :], q[r], NT_DIMS, preferred_element_type=F32)
            st = jnp.where(cvalid, st, -jnp.inf)
            cst_ref[r, 0:ncq, :] = st
            cmax.append(jnp.max(st, axis=0, keepdims=True))
        for r in range(rheads):
            m = jnp.where(jnp.isfinite(cmax[r]), cmax[r], 0.0)
            e = jnp.exp2(cst_ref[r, 0:ncq, :] - m)
            p = e / jnp.maximum(jnp.sum(e, axis=0, keepdims=True), tiny)
            psum += p
            oc_ref[qi * rheads + r] = jnp.dot(vct[:, 0:ncq], p.astype(BF16), preferred_element_type=F32)
        hi, lo = _split_bf16(psum)
        imp = (jnp.dot(ovt[:, 0:ncq], hi, preferred_element_type=F32)
               + jnp.dot(ovt[:, 0:ncq], lo, preferred_element_type=F32))
        blk = lax.broadcasted_iota(jnp.int32, (NSLC_PAD, tq), 0)
        cur = (qi * tq + lax.broadcasted_iota(jnp.int32, (NSLC_PAD, tq), 1)) // SLC_BLOCK
        imp = jnp.where(blk > cur, -jnp.inf, imp)
        forced = (blk == 0) | (blk == cur) | (blk == cur - 1)
        imp = jnp.where(forced, jnp.inf, imp)
        imp_ref[...] = imp
        nvis = min(nslc, (qi + 1) * blocks_per_tile)
        vslab = -(-nvis // SUBLANES)
        slabs = [imp[s * SUBLANES:(s + 1) * SUBLANES, :] for s in range(vslab)]
        ranks = [jnp.zeros((SUBLANES, tq), F32) for _ in range(vslab)]
        sub = lax.broadcasted_iota(jnp.int32, (SUBLANES, tq), 0)
        for jp in range(nvis):
            row = imp_ref[jp:jp + 1, :]
            for s in range(vslab):
                if s * SUBLANES > jp:
                    ahead = jnp.where(row >= slabs[s], 1.0, 0.0)
                elif (s + 1) * SUBLANES - 1 < jp:
                    ahead = jnp.where(row > slabs[s], 1.0, 0.0)
                else:
                    ahead = jnp.where(sub > jp - s * SUBLANES, jnp.where(row >= slabs[s], 1.0, 0.0),
                                      jnp.where(row > slabs[s], 1.0, 0.0))
                ranks[s] = ranks[s] + ahead
        for s in range(vslab):
            start = qi * sel_rows + s * SUBLANES
            sel_ref[start:start + SUBLANES, :] = jnp.where(ranks[s] < nsel, 1.0, 0.0)

    for qi in range(qt):
        select_blocks(qi)
    acc_ref[...] = jnp.zeros_like(acc_ref)
    m_ref[...] = jnp.full(m_ref.shape, -jnp.inf, F32)

    def chosen_rows(qi, tile):
        base = qi * sel_rows + tile * blocks_per_tile
        return [sel_ref[pl.ds(base + c, 1), :] for c in range(blocks_per_tile)]

    def scores(qi, tile, br, buf, valid):
        q = heads_q(qi)
        k = k_ref[br, rows(tile), :]
        for r in range(rheads):
            row = buf * rheads + r
            cmax_ref[row:row + 1, :] = _scores_t(k, q[r], valid, bufs[buf], r)

    def below_score_fn(e, buf):
        qi, tile = qi_tab[e], tile_tab[e]
        scores(qi, tile, 0, buf, [c > 0.5 for c in chosen_rows(qi, tile)])

    def edge_score_fn(e, buf):
        qi, tile, br = qi_tab[e], tile_tab[e], br_tab[e]
        chosen = jnp.concatenate([jnp.broadcast_to(c, (SLC_BLOCK, tq)) for c in chosen_rows(qi, tile)],
                                 axis=0)
        dpos = (qi - tile) * tq + (qcol - krow)
        reach = jnp.where(br == 0, seq, WINDOW).astype(jnp.uint32)
        kept = jnp.where(jnp.maximum(chosen, br.astype(F32)) > 0.5, dpos, -1)
        scores(qi, tile, br, buf, kept.astype(jnp.uint32) < reach)

    def accum_fn(e, buf):
        qi, tile, br = qi_tab[e], tile_tab[e], br_tab[e]
        vt = vt_ref[0, br, tile]
        for r in range(rheads):
            cm, sr = buf * rheads + r, (qi * 2 + br) * rheads + r
            m_ref[pl.ds(sr, 1), :] = _accumulate_t(bufs[buf], r, cmax_ref[cm:cm + 1, :],
                                                   m_ref[pl.ds(sr, 1), :], acc_ref, sr, vt, True)

    _pipelined_sweep(nbelow, below_score_fn, accum_fn, unroll=8)
    _pipelined_sweep(nentries - nbelow, lambda e, buf: edge_score_fn(e + nbelow, buf),
                     lambda e, buf: accum_fn(e + nbelow, buf), unroll=8)

    def finish(qi, c):
        outs = []
        for r in range(rheads):
            row0 = (g * rheads + r) * 3
            o = [oc_ref[qi * rheads + r]]
            for br in range(2):
                sr = (qi * 2 + br) * rheads + r
                o.append(acc_ref[sr, 0:HEAD_DIM, :]
                         / jnp.maximum(acc_ref[sr, HEAD_DIM:HEAD_DIM + 1, :], tiny))
            outs.append(sum(gt_ref[0, qi, pl.ds(row0 + j, 1), :] * o[j] for j in range(3)))
        out = jnp.concatenate(outs, axis=0).T
        o_ref[rows(qi), :] = (out * jax.nn.silu(dg_ref[rows(qi), :])).astype(BF16)
        return c

    lax.fori_loop(0, qt, finish, 0)


def _nsa_attn(qe, kc, vct, kboth, vtboth, gt, p1, ovt, batch, seq, tq, dg_blk):
    n = qe.shape[0]
    groups = D_KV_GROUPS
    rheads = D_HEADS // groups
    width = rheads * HEAD_DIM
    qt = seq // tq
    ncp = kc.shape[2]
    nslc = seq // SLC_BLOCK
    sel_rows = -(-nslc // SUBLANES) * SUBLANES
    wtiles = (WINDOW + tq - 1) // tq
    below = [(qi, t, 0) for qi in range(qt) for t in range(qi)]
    edge = []
    for qi in range(qt):
        edge += [(qi, qi, 0)] + [(qi, t, 1) for t in range(max(qi - wtiles, 0), qi + 1)]
    entries = below + edge
    tabs = [jnp.asarray([e[c] for e in entries], jnp.int32) for c in range(3)]
    kernel = functools.partial(_nsa_kernel, tq=tq, qt=qt, nslc=nslc, nsel=min(SLC_TOPK, nslc),
                               nbelow=len(below), nentries=len(entries))
    return pl.pallas_call(
        kernel,
        grid_spec=pltpu.PrefetchScalarGridSpec(
            num_scalar_prefetch=3,
            grid=(batch, groups),
            in_specs=[pl.BlockSpec((seq, width), lambda b, g, *_: (b, g)),
                      pl.BlockSpec((1, 1, ncp, LANES), lambda b, g, *_: (b, g, 0, 0)),
                      pl.BlockSpec((1, 1, HEAD_DIM, ncp), lambda b, g, *_: (b, g, 0, 0)),
                      pl.BlockSpec((2, seq, LANES), lambda b, g, *_: (0, b, g)),
                      pl.BlockSpec((1, 2, qt, HEAD_DIM + ONES_ROWS, tq), lambda b, g, *_: (b, 0, 0, g, 0)),
                      pl.BlockSpec((1, qt, LANES, tq), lambda b, g, *_: (b, 0, 0, 0)),
                      pl.BlockSpec((seq, width), lambda b, g, *_: (b, dg_blk + g)),
                      pl.BlockSpec(ovt.shape, lambda b, g, *_: (0, 0))],
            out_specs=pl.BlockSpec((seq, width), lambda b, g, *_: (b, g)),
            scratch_shapes=[pltpu.VMEM((qt * 2 * rheads, HEAD_DIM + ONES_ROWS, tq), F32),
                            pltpu.VMEM((qt * rheads, HEAD_DIM, tq), F32),
                            pltpu.VMEM((qt * sel_rows, tq), F32),
                            pltpu.VMEM((NSLC_PAD, tq), F32),
                            pltpu.VMEM((rheads, tq, tq), F32),
                            pltpu.VMEM((rheads, tq, tq), F32),
                            pltpu.VMEM((rheads, ncp, tq), F32),
                            pltpu.VMEM((2 * rheads, tq), F32),
                            pltpu.VMEM((qt * 2 * rheads, tq), F32),
                            pltpu.VMEM((seq, rheads * LANES), BF16)]),
        out_shape=jax.ShapeDtypeStruct((n, groups * width), BF16),
        compiler_params=_cparams(("parallel", "parallel"), 56),
        name="nsa_attn",
    )(*tabs, qe, kc, vct, kboth, vtboth, gt, p1, ovt)


def _rope_tables(pos, d):
    rd = d // 4
    half = rd // 2
    inv = ROPE_THETA ** (-jnp.arange(half, dtype=F32) / half)
    ang = pos.astype(F32)[:, None] * inv[None, :]
    cos, sin = jnp.cos(ang), jnp.sin(ang)
    npos = pos.shape[0]
    zeros = jnp.zeros((npos, d), F32)
    c = jnp.concatenate([cos, cos, jnp.ones((npos, d - rd), F32)], axis=1)
    s1 = zeros.at[:, :half].set(-sin)
    s2 = zeros.at[:, half:rd].set(sin)
    return tuple(jnp.tile(t, (1, LANES // d)) for t in (c, s1, s2))


def _overlap_t(ncp, nslc):
    start = np.arange(ncp) * CMP_STRIDE
    s0 = np.arange(nslc) * SLC_BLOCK
    lo = np.maximum(start[:, None], s0[None, :])
    hi = np.minimum(start[:, None] + CMP_BLOCK, s0[None, :] + SLC_BLOCK)
    ov = np.clip(hi - lo, 0, None) / CMP_BLOCK
    ov[ncp - 1] = 0.0
    out = np.zeros((NSLC_PAD, ncp), np.float32)
    out[:nslc] = ov.T
    return jnp.asarray(out, BF16)


def kernel(x, mem, mem_norm, l0_norm, l0_w_in, l0_a_vnorm, l0_a_ws, l0_a_bs, l0_b_qnorm, l0_b_knorm, l0_b_lq1, l0_b_lk1, l0_b_lq2, l0_b_lk2, l0_b_subln, l0_m_wkv, l0_m_qnorm, l0_m_knorm, l0_w_out, l1_norm, l1_w_in, l1_c_conv_w, l1_c_conv_b, l1_c_norm, l1_d_qnorm, l1_d_knorm, l1_d_cmp_pos_k, l1_d_cmp_w1_k, l1_d_cmp_w2_k, l1_d_cmp_pos_v, l1_d_cmp_w1_v, l1_d_cmp_w2_v, l1_m_wkv, l1_m_qnorm, l1_m_knorm, l1_w_out):
    batch, seq, d_model = x.shape
    n = batch * seq
    tm = 256
    tq = 256
    mw = M_HEADS * HEAD_DIM
    pos = jnp.arange(seq, dtype=jnp.int32)
    x0 = x.reshape(n, d_model)

    wkv_both = jnp.concatenate([l0_m_wkv, l1_m_wkv], axis=1).astype(BF16)
    kgains = jnp.stack([jnp.tile(l0_m_knorm, M_HEADS), jnp.tile(l1_m_knorm, M_HEADS)])
    mem_kh, mem_vt = _mem_kv(mem, mem_norm, wkv_both, kgains)

    w = l0_w_in
    tab32 = _rope_tables(pos, B_QK_DIM)
    b_width = 2 * B_HEADS * B_QK_DIM
    a_width = l0_a_vnorm.shape[0]
    bs_exp = jnp.repeat(l0_a_bs.T, a_width // A_GROUPS, axis=1)
    p0, y_a, qn, kn, vt = _layer0_proj(
        x0, l0_norm, w[:, 0:1536].astype(BF16), w[:, 1536:3072].astype(BF16), w[:, 3072:4096].astype(BF16),
        tab32, jnp.tile(l0_b_qnorm, b_width // B_QK_DIM).reshape(1, b_width),
        jnp.tile(l0_b_knorm, b_width // B_QK_DIM).reshape(1, b_width),
        l0_a_ws, bs_exp, l0_a_vnorm.reshape(1, a_width), batch, seq, tq)
    lam_init = 0.8 - 0.6 * math.exp(-0.3 * 1)
    lam_p = jnp.stack([l0_b_lq1, l0_b_lk1, l0_b_lq2, l0_b_lk2])
    y_b = _diff_attn(lam_p, qn, kn, vt, p0, jnp.tile(l0_b_subln, 2).reshape(1, LANES),
                     batch, seq, tq, lam_init, 0)
    y_m = _mem_attn(p0, mem_kh, mem_vt, 0, jnp.tile(l0_m_qnorm, M_HEADS).reshape(1, mw),
                    512 // mw, 768 // mw, seq, 2 * tq)

    w = l1_w_in
    n_bg = 3 * D_HEADS
    w_rest = jnp.concatenate([w[:, 2840:3352], w[:, 2048:2304], w[:, 3352:3864]], axis=1)
    w_d = jnp.concatenate([w[:, 1536:2048], w[:, 2304:2816], w[:, 2816:2840],
                           jnp.zeros((d_model, LANES - n_bg), F32)], axis=1)
    groups, rheads = D_KV_GROUPS, D_HEADS // D_KV_GROUPS
    d_width = D_HEADS * HEAD_DIM
    tab64 = _rope_tables(pos, HEAD_DIM)
    x1, p1, y_c, qe, kboth, vtboth, gt = _layer1_proj(
        x0, y_a, y_b, y_m, l0_w_out.astype(BF16),
        l1_norm, w[:, 0:1536].astype(BF16), w_d.astype(BF16), w_rest.astype(BF16), tab64,
        jnp.tile(l1_d_qnorm, D_HEADS).reshape(1, d_width),
        jnp.tile(l1_d_knorm, LANES // HEAD_DIM).reshape(1, LANES),
        l1_c_conv_w, l1_c_conv_b, l1_c_norm, batch, seq, tq)
    kv_blk = 512 // LANES

    ncp = seq // CMP_STRIDE
    cmp_pos = jnp.arange(ncp, dtype=jnp.int32) * CMP_STRIDE + (CMP_BLOCK - 1)
    hidden = l1_d_cmp_w2_k.shape[0]

    def group_w1(w1):
        w = w1.reshape(CMP_BLOCK, HEAD_DIM, hidden)
        return jnp.stack([jnp.zeros((CMP_BLOCK, LANES, hidden), F32)
                          .at[:, g * HEAD_DIM:(g + 1) * HEAD_DIM, :].set(w) for g in range(groups)])

    w2k = jnp.tile(l1_d_cmp_w2_k, (1, LANES // HEAD_DIM))
    w2v = jnp.zeros((hidden, LANES), F32).at[:, :HEAD_DIM].set(l1_d_cmp_w2_v)
    kc, vct = _compress(
        p1, kv_blk, jnp.tile(l1_d_cmp_pos_k, (1, groups)), jnp.tile(l1_d_cmp_pos_v, (1, groups)),
        group_w1(l1_d_cmp_w1_k).astype(BF16), w2k.astype(BF16),
        group_w1(l1_d_cmp_w1_v).astype(BF16), w2v.astype(BF16),
        jnp.tile(l1_d_knorm, LANES // HEAD_DIM).reshape(1, LANES), _rope_tables(cmp_pos, HEAD_DIM),
        batch, seq)

    nslc = seq // SLC_BLOCK
    y_d = _nsa_attn(qe, kc, vct, kboth, vtboth, gt, p1, _overlap_t(ncp, nslc),
                    batch, seq, tq, 0)
    out = _mem_out(x1, y_c, y_d, p1, mem_kh, mem_vt, 1, jnp.tile(l1_m_qnorm, M_HEADS).reshape(1, mw),
                   768 // mw, 1024 // mw, l1_w_out.astype(BF16), seq, 2 * tq)
    return out.reshape(batch, seq, d_model)
```

```python
import functools
import math

import numpy as np
import jax
import jax.numpy as jnp
from jax import lax
from jax.experimental import pallas as pl
from jax.experimental.pallas import tpu as pltpu

F32 = jnp.float32
BF16 = jnp.bfloat16

EPS = 1e-6
ROPE_THETA = 500000.0
HEAD_DIM = 64
A_CHUNK = 128
A_GROUPS = 4
B_HEADS = 8
B_QK_DIM = 32
C_KERNEL = 31
D_HEADS = 8
D_KV_GROUPS = 2
CMP_BLOCK = 32
CMP_STRIDE = 16
SLC_BLOCK = 64
SLC_TOPK = 16
WINDOW = 512
M_HEADS = 4

LANES = 128
SUBLANES = 8
MXU_DIM = 256
HALO = 32
NSLC_PAD = 128
ONES_ROWS = 16

NT_DIMS = (((1,), (1,)), ((), ()))
LOG2E = math.log2(math.e)


def _cparams(semantics, vmem_mb):
    return pltpu.CompilerParams(dimension_semantics=semantics,
                                vmem_limit_bytes=vmem_mb * 1024 * 1024)


def _split_bf16(x):
    hi = x.astype(BF16)
    lo = (x - hi.astype(F32)).astype(BF16)
    return hi, lo


def _group_mean(x2, gsize):
    t, c = x2.shape
    w = min(c, MXU_DIM)
    r = lax.broadcasted_iota(jnp.int32, (w, w), 0) // gsize
    cc = lax.broadcasted_iota(jnp.int32, (w, w), 1) // gsize
    ones = jnp.where(r == cc, 1.0, 0.0).astype(BF16)
    outs = []
    for s in range(c // w):
        hi, lo = _split_bf16(x2[:, s * w:(s + 1) * w])
        outs.append(jnp.dot(hi, ones, preferred_element_type=F32)
                    + jnp.dot(lo, ones, preferred_element_type=F32))
    out = outs[0] if len(outs) == 1 else jnp.concatenate(outs, axis=1)
    return out * (1.0 / gsize)


def _tile_lanes(t, width):
    rep = width // t.shape[1]
    return t if rep == 1 else jnp.concatenate([t] * rep, axis=1)


def _rope(x, cos_t, s1_t, s2_t, half):
    width = x.shape[1]
    up = pltpu.roll(x, width - half, 1)
    dn = pltpu.roll(x, half, 1)
    return (x * _tile_lanes(cos_t, width) + up * _tile_lanes(s1_t, width)
            + dn * _tile_lanes(s2_t, width))


def _matmul_cols(h, w_ref, lo, hi, out_ref=None):
    r = jnp.dot(h, w_ref[:, lo:hi], preferred_element_type=F32)
    if out_ref is None:
        return r
    out_ref[:, lo:hi] = r


def _rms_rows(x, ng_ref):
    ms = jnp.mean(x * x, axis=-1, keepdims=True)
    return (x * lax.rsqrt(ms + EPS) * ng_ref[...]).astype(BF16)


def _scores_t(k, q, valid, st_ref, idx):
    st = lax.dot_general(k, q, NT_DIMS, preferred_element_type=F32)
    if isinstance(valid, (list, tuple)):
        blk = st.shape[0] // len(valid)
        st = jnp.concatenate([jnp.where(v, st[c * blk:(c + 1) * blk, :], -jnp.inf)
                              for c, v in enumerate(valid)], axis=0)
    elif valid is not None:
        st = jnp.where(valid, st, -jnp.inf)
    st_ref[idx] = st
    return jnp.max(st, axis=0, keepdims=True)


def _accumulate_t(st_ref, idx, cmax, m, acc_ref, aidx, vt, guard_empty):
    m_new = jnp.maximum(m, cmax)
    m_use = jnp.where(m_new == -jnp.inf, 0.0, m_new) if guard_empty else m_new
    alpha = jnp.exp2(m - m_use)
    pt = jnp.exp2((st_ref[idx] - m_use).astype(BF16))
    acc_ref[aidx] = alpha * acc_ref[aidx] + jnp.dot(vt, pt, preferred_element_type=F32)
    return m_new


def _pipelined_sweep(n, score_fn, accum_fn, unroll=4):
    score_fn(0, 0)
    stages = n - 1

    def stage(e, parity):
        score_fn(e + 1, 1 - parity)
        accum_fn(e, parity)

    def body(i, c):
        for k in range(unroll):
            stage(unroll * i + k, k % 2)
        return c

    if isinstance(n, int):
        iters, left = divmod(stages, unroll)
        if iters:
            lax.fori_loop(0, iters, body, 0)
        for k in range(left):
            stage(iters * unroll + k, k % 2)
        accum_fn(n - 1, left % 2)
        return
    lax.fori_loop(0, stages // unroll, body, 0)
    base = stages // unroll * unroll
    left = stages - base
    for k in range(unroll - 1):
        pl.when(left > k)(functools.partial(stage, base + k, k % 2))
    for parity in range(2):
        pl.when(left % 2 == parity)(functools.partial(accum_fn, n - 1, parity))


def _mix_out(ya_ref, yb_ref, ym_ref, w_ref):
    wa = ya_ref.shape[1]
    wb = yb_ref.shape[1]
    acc = jnp.dot(ya_ref[...], w_ref[0:wa, :], preferred_element_type=F32)
    acc += jnp.dot(yb_ref[...], w_ref[wa:wa + wb, :], preferred_element_type=F32)
    acc += jnp.dot(ym_ref[...], w_ref[wa + wb:, :], preferred_element_type=F32)
    return acc


def _sgu_group(u, v, gate, w_ref, bs, vgain, o_ref, cols):
    tm = u.shape[0]
    row = lax.broadcasted_iota(jnp.int32, (A_CHUNK, A_CHUNK), 0)
    col = lax.broadcasted_iota(jnp.int32, (A_CHUNK, A_CHUNK), 1)
    w = jnp.where(col <= row, w_ref, 0.0).astype(BF16)
    gv = jax.nn.gelu(v)
    ms = jnp.mean(gv * gv, axis=-1, keepdims=True)
    vn = (gv * lax.rsqrt(ms + EPS) * vgain).astype(BF16)
    gu = jax.nn.gelu(u) * jax.nn.silu(gate)
    for c in range(tm // A_CHUNK):
        rows = slice(c * A_CHUNK, (c + 1) * A_CHUNK)
        z = jnp.dot(w, vn[rows, :], preferred_element_type=F32) + bs
        o_ref[rows, cols] = (gu[rows, :] * z).astype(BF16)


def _layer0_proj_kernel(x_ref, ng_ref, wa_ref, wqkv_ref, w_ref, c_ref, s1_ref, s2_ref, qg_ref, kg_ref,
                        ws_ref, bs_ref, vg_ref, p_ref, ya_ref, qo_ref, ko_ref, vto_ref):
    h = _rms_rows(x_ref[...], ng_ref)
    aw = vg_ref.shape[1]
    gdim = aw // A_GROUPS
    width = qg_ref.shape[1]
    half = B_QK_DIM // 4 // 2
    pw = p_ref.shape[1]

    a = _matmul_cols(h, wa_ref, 0, 3 * aw)

    def sgu(g):
        cols = slice(g * gdim, (g + 1) * gdim)
        _sgu_group(a[:, cols], a[:, aw + g * gdim:aw + (g + 1) * gdim],
                   a[:, 2 * aw + g * gdim:2 * aw + (g + 1) * gdim],
                   ws_ref[g], bs_ref[:, cols], vg_ref[:, cols], ya_ref, cols)

    def prep(t, gain, dst, mul):
        tn = t * lax.rsqrt(_group_mean(t * t, B_QK_DIM) + EPS) * gain[...]
        tr = _rope(tn, c_ref[...], s1_ref[...], s2_ref[...], half)
        dst[...] = (tr * mul if mul != 1.0 else tr).astype(BF16)

    q = _matmul_cols(h, wqkv_ref, 0, width)
    sgu(0)
    k = _matmul_cols(h, wqkv_ref, width, 2 * width)
    sgu(1)
    v = _matmul_cols(h, wqkv_ref, 2 * width, 3 * width)
    sgu(2)
    _matmul_cols(h, w_ref, 0, pw // 2, p_ref)
    sgu(3)
    prep(q, qg_ref, qo_ref, B_QK_DIM ** -0.5 * LOG2E)
    _matmul_cols(h, w_ref, pw // 2, pw, p_ref)
    prep(k, kg_ref, ko_ref, 1.0)
    vt = v.T
    dv = 2 * B_QK_DIM
    ones = jnp.ones((ONES_ROWS, vt.shape[1]), F32)
    parts = []
    for hd in range(B_HEADS):
        parts += [vt[hd * dv:(hd + 1) * dv, :], ones]
    vto_ref[0, 0] = jnp.concatenate(parts, axis=0).astype(BF16)


def _layer0_proj(x, norm_g, w_a, w_qkv, w_rest, tables, qgain, kgain, a_ws, bs_exp, vgain, batch, seq, tm):
    n, d = x.shape
    width = qgain.shape[1]
    aw = vgain.shape[1]
    tiles_per_seq = seq // tm
    tab_spec = pl.BlockSpec((tm, LANES), lambda i: (i % tiles_per_seq, 0))
    vrows = B_HEADS * (2 * B_QK_DIM + ONES_ROWS)

    def full(a):
        return pl.BlockSpec(a.shape, lambda i: (0,) * a.ndim)

    def rows(w):
        return pl.BlockSpec((tm, w), lambda i: (i, 0))

    return pl.pallas_call(
        _layer0_proj_kernel,
        grid=(n // tm,),
        in_specs=[rows(d), pl.BlockSpec((1, d), lambda i: (0, 0)), full(w_a), full(w_qkv), full(w_rest),
                  tab_spec, tab_spec, tab_spec, full(qgain), full(kgain),
                  full(a_ws), full(bs_exp), full(vgain)],
        out_specs=[rows(w_rest.shape[1]), rows(aw), rows(width), rows(width),
                   pl.BlockSpec((1, 1, vrows, tm),
                                lambda i: (i // tiles_per_seq, i % tiles_per_seq, 0, 0))],
        out_shape=[jax.ShapeDtypeStruct((n, w_rest.shape[1]), F32),
                   jax.ShapeDtypeStruct((n, aw), BF16),
                   jax.ShapeDtypeStruct((n, width), BF16), jax.ShapeDtypeStruct((n, width), BF16),
                   jax.ShapeDtypeStruct((batch, tiles_per_seq, vrows, tm), BF16)],
        compiler_params=_cparams(("parallel",), 48),
        name="layer0_proj",
    )(x, norm_g.reshape(1, d), w_a, w_qkv, w_rest, *tables, qgain, kgain, a_ws, bs_exp, vgain)


def _diff_attn_kernel(qi_tab, tile_tab, lam_ref, q_ref, k_ref, vt_ref, g_ref, sub_ref, o_ref,
                      acc_ref, sta_ref, stb_ref, cmax_ref, m_ref, *, tq, qt, lam_init):
    dv = 2 * B_QK_DIM
    vrows = dv + ONES_ROWS
    nmap = 4
    lane = lax.broadcasted_iota(jnp.int32, (1, LANES), 1)
    bufs = (sta_ref, stb_ref)
    acc_ref[...] = jnp.zeros_like(acc_ref)
    m_ref[...] = jnp.full(m_ref.shape, -jnp.inf, F32)
    krow = lax.broadcasted_iota(jnp.int32, (tq, tq), 0)
    qcol = lax.broadcasted_iota(jnp.int32, (tq, tq), 1)
    lam_p = lam_ref[...]
    lam = (jnp.exp(jnp.sum(lam_p[0:1] * lam_p[1:2], axis=-1, keepdims=True))
           - jnp.exp(jnp.sum(lam_p[2:3] * lam_p[3:4], axis=-1, keepdims=True)) + lam_init)

    def rows(tile):
        return pl.ds(pl.multiple_of(tile * tq, tq), tq)

    def scores(qi, tile, buf, valid):
        q = q_ref[rows(qi), :]
        k = k_ref[rows(tile), :]
        zero = jnp.zeros_like(q)
        for j in range(nmap):
            qm = jnp.where(lane // B_QK_DIM == j, q, zero)
            row = buf * nmap + j
            cmax_ref[row:row + 1, :] = _scores_t(k, qm, valid, bufs[buf], j)

    def accumulate(qi, tile, buf):
        for j in range(nmap):
            vt = vt_ref[0, tile, (j // 2) * vrows:(j // 2 + 1) * vrows, :]
            cm, sr = buf * nmap + j, qi * nmap + j
            m_ref[pl.ds(sr, 1), :] = _accumulate_t(bufs[buf], j, cmax_ref[cm:cm + 1, :],
                                                   m_ref[pl.ds(sr, 1), :], acc_ref, sr, vt, False)

    def finish(qi):
        halves = []
        for h in range(2):
            o = [acc_ref[qi * nmap + 2 * h + mp, 0:dv, :] / acc_ref[qi * nmap + 2 * h + mp, dv:dv + 1, :]
                 for mp in range(2)]
            pd = o[0] - lam * o[1]
            ms2 = jnp.mean(pd * pd, axis=0, keepdims=True)
            halves.append(pd * lax.rsqrt(ms2 + EPS))
        ob = jnp.concatenate(halves, axis=0).T * sub_ref[...] * (1.0 - lam_init)
        o_ref[rows(qi), :] = (ob * jax.nn.silu(g_ref[rows(qi), :])).astype(BF16)

    def below_score(e, buf):
        scores(qi_tab[e], tile_tab[e], buf, None)

    def below_accum(e, buf):
        accumulate(qi_tab[e], tile_tab[e], buf)

    def diag_score(e, buf):
        scores(e, e, buf, krow <= qcol)

    def diag_accum(e, buf):
        accumulate(e, e, buf)
        finish(e)

    _pipelined_sweep(qt * (qt - 1) // 2, below_score, below_accum, unroll=8)
    _pipelined_sweep(qt, diag_score, diag_accum)


def _diff_attn(lam_p, qn, kn, vt, p0, subln, batch, seq, tq, lam_init, g_off):
    n, width = qn.shape
    pairs = width // LANES
    qt = seq // tq
    vrows = 2 * B_QK_DIM + ONES_ROWS
    below = [(qi, t) for qi in range(qt) for t in range(qi)]
    qi_tab = jnp.asarray([e[0] for e in below], jnp.int32)
    tile_tab = jnp.asarray([e[1] for e in below], jnp.int32)
    kernel = functools.partial(_diff_attn_kernel, tq=tq, qt=qt, lam_init=lam_init)
    seq_spec = pl.BlockSpec((seq, LANES), lambda b, p, *_: (b, p))
    return pl.pallas_call(
        kernel,
        grid_spec=pltpu.PrefetchScalarGridSpec(
            num_scalar_prefetch=2,
            grid=(batch, pairs),
            in_specs=[pl.BlockSpec(lam_p.shape, lambda b, p, *_: (0, 0)),
                      seq_spec, seq_spec,
                      pl.BlockSpec((1, qt, 2 * vrows, tq), lambda b, p, *_: (b, 0, p, 0)),
                      pl.BlockSpec((seq, LANES), lambda b, p, *_: (b, g_off + p)),
                      pl.BlockSpec((1, LANES), lambda b, p, *_: (0, 0))],
            out_specs=seq_spec,
            scratch_shapes=[pltpu.VMEM((qt * 4, vrows, tq), F32), pltpu.VMEM((4, tq, tq), F32),
                            pltpu.VMEM((4, tq, tq), F32), pltpu.VMEM((8, tq), F32),
                            pltpu.VMEM((qt * 4, tq), F32)]),
        out_shape=jax.ShapeDtypeStruct((n, width), BF16),
        compiler_params=_cparams(("parallel", "parallel"), 40),
        name="diff_attn",
    )(qi_tab, tile_tab, lam_p, qn, kn, vt, p0, subln)


def _mem_kv_kernel(mem_ref, g_ref, w_ref, kg_ref, kh_ref, vt_ref):
    x = mem_ref[0]
    ms = jnp.mean(x * x, axis=-1, keepdims=True)
    h = (x * lax.rsqrt(ms + EPS) * g_ref[...]).astype(BF16)
    kv = jnp.dot(h, w_ref[...], preferred_element_type=F32)
    mw = kv.shape[1] // 4
    lane = lax.broadcasted_iota(jnp.int32, (1, mw), 1)
    ones = jnp.ones((ONES_ROWS, kv.shape[0]), F32)
    for layer in range(2):
        k = kv[:, (2 * layer) * mw:(2 * layer + 1) * mw]
        kn = k * lax.rsqrt(_group_mean(k * k, HEAD_DIM) + EPS) * kg_ref[layer:layer + 1, :]
        vt = kv[:, (2 * layer + 1) * mw:(2 * layer + 2) * mw].T
        parts = []
        for hd in range(M_HEADS):
            kh_ref[layer, 0, hd] = jnp.where(lane // HEAD_DIM == hd, kn, 0.0).astype(BF16)
            parts += [vt[hd * HEAD_DIM:(hd + 1) * HEAD_DIM, :], ones]
        vt_ref[layer, 0] = jnp.concatenate(parts, axis=0).astype(BF16)


def _mem_kv(mem, mem_norm, wkv_both, kgains):
    batch, mtok, d = mem.shape
    mw = wkv_both.shape[1] // 4
    vrows = M_HEADS * (HEAD_DIM + ONES_ROWS)
    return pl.pallas_call(
        _mem_kv_kernel,
        grid=(batch,),
        in_specs=[pl.BlockSpec((1, mtok, d), lambda b: (b, 0, 0)),
                  pl.BlockSpec((1, d), lambda b: (0, 0)),
                  pl.BlockSpec(wkv_both.shape, lambda b: (0, 0)),
                  pl.BlockSpec(kgains.shape, lambda b: (0, 0))],
        out_specs=[pl.BlockSpec((2, 1, M_HEADS, mtok, mw), lambda b: (0, b, 0, 0, 0)),
                   pl.BlockSpec((2, 1, vrows, mtok), lambda b: (0, b, 0, 0))],
        out_shape=[jax.ShapeDtypeStruct((2, batch, M_HEADS, mtok, mw), BF16),
                   jax.ShapeDtypeStruct((2, batch, vrows, mtok), BF16)],
        compiler_params=_cparams(("parallel",), 32),
        name="mem_kv",
    )(mem, mem_norm.reshape(1, d), wkv_both, kgains)


def _mem_attn_tile(q_ref, g_ref, kh_ref, vt_ref, qg_ref, st_ref):
    vrows = HEAD_DIM + ONES_ROWS
    x = q_ref[...]
    qn = (x * lax.rsqrt(_group_mean(x * x, HEAD_DIM) + EPS) * qg_ref[...]
          * (HEAD_DIM ** -0.5 * LOG2E)).astype(BF16)
    cmax = [_scores_t(kh_ref[0, 0, h], qn, None, st_ref, h) for h in range(M_HEADS)]
    outs = []
    for h in range(M_HEADS):
        pt = jnp.exp2((st_ref[h] - cmax[h]).astype(BF16))
        acc = jnp.dot(vt_ref[0, 0, h * vrows:(h + 1) * vrows, :], pt, preferred_element_type=F32)
        outs.append(acc[0:HEAD_DIM, :] / acc[HEAD_DIM:HEAD_DIM + 1, :])
    out = jnp.concatenate(outs, axis=0).T
    return (out * jax.nn.silu(g_ref[...])).astype(BF16)


def _mem_attn_kernel(q_ref, g_ref, kh_ref, vt_ref, qg_ref, o_ref, st_ref):
    o_ref[...] = _mem_attn_tile(q_ref, g_ref, kh_ref, vt_ref, qg_ref, st_ref)


def _mem_out_kernel(x_ref, ya_ref, yb_ref, q_ref, g_ref, kh_ref, vt_ref, qg_ref, w_ref, o_ref, st_ref):
    ym = _mem_attn_tile(q_ref, g_ref, kh_ref, vt_ref, qg_ref, st_ref)
    wa = ya_ref.shape[1]
    wb = yb_ref.shape[1]
    acc = jnp.dot(ya_ref[...], w_ref[0:wa, :], preferred_element_type=F32)
    acc += jnp.dot(yb_ref[...], w_ref[wa:wa + wb, :], preferred_element_type=F32)
    acc += jnp.dot(ym, w_ref[wa + wb:, :], preferred_element_type=F32)
    o_ref[...] = x_ref[...] + acc


def _mem_out(x, ya, yb, p, kh, vt, layer, qgain, q_blk, g_blk, w, seq, tq):
    n, d = x.shape
    _, batch, heads, mtok, mw = kh.shape
    qt = seq // tq

    def rows(width, blk=0):
        return pl.BlockSpec((tq, width), lambda i: (i, blk))

    return pl.pallas_call(
        _mem_out_kernel,
        grid=(n // tq,),
        in_specs=[rows(d), rows(ya.shape[1]), rows(yb.shape[1]), rows(mw, q_blk), rows(mw, g_blk),
                  pl.BlockSpec((1, 1, heads, mtok, mw), lambda i: (layer, i // qt, 0, 0, 0)),
                  pl.BlockSpec((1, 1, vt.shape[2], mtok), lambda i: (layer, i // qt, 0, 0)),
                  pl.BlockSpec((1, mw), lambda i: (0, 0)),
                  pl.BlockSpec(w.shape, lambda i: (0, 0))],
        out_specs=rows(d),
        out_shape=jax.ShapeDtypeStruct((n, d), F32),
        scratch_shapes=[pltpu.VMEM((heads, mtok, tq), F32)],
        compiler_params=_cparams(("parallel",), 40),
        name="mem_out_proj",
    )(x, ya, yb, p, p, kh, vt, qgain, w)


def _mem_attn(p, kh, vt, layer, qgain, q_blk, g_blk, seq, tq):
    n = p.shape[0]
    _, batch, heads, mtok, mw = kh.shape
    qt = seq // tq
    return pl.pallas_call(
        _mem_attn_kernel,
        grid=(n // tq,),
        in_specs=[pl.BlockSpec((tq, mw), lambda i: (i, q_blk)),
                  pl.BlockSpec((tq, mw), lambda i: (i, g_blk)),
                  pl.BlockSpec((1, 1, heads, mtok, mw), lambda i: (layer, i // qt, 0, 0, 0)),
                  pl.BlockSpec((1, 1, vt.shape[2], mtok), lambda i: (layer, i // qt, 0, 0)),
                  pl.BlockSpec((1, mw), lambda i: (0, 0))],
        out_specs=pl.BlockSpec((tq, mw), lambda i: (i, 0)),
        out_shape=jax.ShapeDtypeStruct((n, mw), BF16),
        scratch_shapes=[pltpu.VMEM((heads, mtok, tq), F32)],
        compiler_params=_cparams(("parallel",), 32),
        name="mem_attn",
    )(p, p, kh, vt, qgain)


def _layer1_proj_kernel(x_ref, ya_ref, yb_ref, ym_ref, wo_ref, ng_ref, wc_ref, wd_ref, w_ref,
                        c_ref, s1_ref, s2_ref, qg_ref, kg_ref, cw_ref, cb_ref, cn_ref,
                        x1_ref, p_ref, yc_ref, qo_ref, ko_ref, vto_ref, gto_ref,
                        hbuf_ref, shift_ref, *, tiles_per_seq):
    i = pl.program_id(0)
    ts = x_ref.shape[0]
    span = ts + HALO
    cwid = cn_ref.shape[1]
    qw = qg_ref.shape[1]
    pw = p_ref.shape[1]
    half = HEAD_DIM // 4 // 2
    x1 = x_ref[...] + _mix_out(ya_ref, yb_ref, ym_ref, wo_ref)
    x1_ref[...] = x1
    h = _rms_rows(x1, ng_ref)

    @pl.when(i % tiles_per_seq == 0)
    def _():
        hbuf_ref[0:HALO, :] = jnp.zeros((HALO, cwid), F32)

    ca = _matmul_cols(h, wc_ref, 0, cwid)
    cbv = _matmul_cols(h, wc_ref, cwid, 2 * cwid)
    q = _matmul_cols(h, wd_ref, 0, qw)
    hbuf_ref[HALO:HALO + ts, :] = ca * jax.nn.sigmoid(cbv)
    for ph in range(SUBLANES):
        shift_ref[ph, 0:span - ph, :] = hbuf_ref[ph:span, :]
    cg = _matmul_cols(h, wc_ref, 2 * cwid, 3 * cwid)
    d = _matmul_cols(h, wd_ref, qw, wd_ref.shape[1])
    first = HALO - (C_KERNEL - 1)
    y = cb_ref[...]
    for j in range(C_KERNEL):
        ph, base = (first + j) % SUBLANES, (first + j) // SUBLANES * SUBLANES
        y = y + shift_ref[ph, base:base + ts, :] * cw_ref[j:j + 1, :]
    psplit = -(-pw // (2 * MXU_DIM)) * MXU_DIM
    _matmul_cols(h, w_ref, 0, psplit, p_ref)
    ms = jnp.mean(y * y, axis=-1, keepdims=True)
    yn = y * lax.rsqrt(ms + EPS) * cn_ref[...]
    yc_ref[...] = (jax.nn.silu(yn) * jax.nn.silu(cg)).astype(BF16)
    hbuf_ref[0:HALO, :] = hbuf_ref[ts:ts + HALO, :]
    _matmul_cols(h, w_ref, psplit, pw, p_ref)

    tabs = (c_ref[...], s1_ref[...], s2_ref[...])
    tn = q * lax.rsqrt(_group_mean(q * q, HEAD_DIM) + EPS) * qg_ref[...]
    qo_ref[...] = (_rope(tn, *tabs, half) * (HEAD_DIM ** -0.5 * LOG2E)).astype(BF16)
    ones = jnp.ones((ONES_ROWS, ts), F32)
    lane = lax.broadcasted_iota(jnp.int32, (1, LANES), 1)
    for br in range(2):
        k = d[:, 2 * br * LANES:(2 * br + 1) * LANES]
        kn = k * lax.rsqrt(_group_mean(k * k, HEAD_DIM) + EPS) * kg_ref[...]
        kr = _rope(kn, *tabs, half)
        swapped = pltpu.roll(kr, HEAD_DIM, 1)
        for g in range(D_KV_GROUPS):
            own = (lane // HEAD_DIM) == g
            ko_ref[br, :, g * LANES:(g + 1) * LANES] = jnp.where(own, kr, swapped).astype(BF16)
        vt = d[:, (2 * br + 1) * LANES:(2 * br + 2) * LANES].T
        parts = []
        for g in range(D_KV_GROUPS):
            parts += [vt[g * HEAD_DIM:(g + 1) * HEAD_DIM, :], ones]
        vto_ref[0, br, 0] = jnp.concatenate(parts, axis=0).astype(BF16)
    gto_ref[0, 0] = jax.nn.sigmoid(d[:, 4 * LANES:5 * LANES]).T


def _layer1_proj(x, ya, yb, ym, w_out, norm_g, w_c, w_d, w_rest, tables, qgain, kgain,
                 conv_w, conv_b, conv_norm, batch, seq, tm):
    n, dm = x.shape
    cwid = conv_w.shape[1]
    tiles_per_seq = seq // tm
    tab_spec = pl.BlockSpec((tm, LANES), lambda i: (i % tiles_per_seq, 0))
    vrows = D_KV_GROUPS * (HEAD_DIM + ONES_ROWS)

    def full(a):
        return pl.BlockSpec(a.shape, lambda i: (0,) * a.ndim)

    def rows(w):
        return pl.BlockSpec((tm, w), lambda i: (i, 0))

    kernel = functools.partial(_layer1_proj_kernel, tiles_per_seq=tiles_per_seq)
    return pl.pallas_call(
        kernel,
        grid=(n // tm,),
        in_specs=[rows(dm), rows(ya.shape[1]), rows(yb.shape[1]), rows(ym.shape[1]), full(w_out),
                  pl.BlockSpec((1, dm), lambda i: (0, 0)), full(w_c), full(w_d), full(w_rest),
                  tab_spec, tab_spec, tab_spec, full(qgain), full(kgain),
                  full(conv_w), pl.BlockSpec((1, cwid), lambda i: (0, 0)),
                  pl.BlockSpec((1, cwid), lambda i: (0, 0))],
        out_specs=[rows(dm), rows(w_rest.shape[1]), rows(cwid), rows(qgain.shape[1]),
                   pl.BlockSpec((2, tm, D_KV_GROUPS * LANES), lambda i: (0, i, 0)),
                   pl.BlockSpec((1, 2, 1, vrows, tm),
                                lambda i: (i // tiles_per_seq, 0, i % tiles_per_seq, 0, 0)),
                   pl.BlockSpec((1, 1, LANES, tm),
                                lambda i: (i // tiles_per_seq, i % tiles_per_seq, 0, 0))],
        out_shape=[jax.ShapeDtypeStruct((n, dm), F32),
                   jax.ShapeDtypeStruct((n, w_rest.shape[1]), F32),
                   jax.ShapeDtypeStruct((n, cwid), BF16),
                   jax.ShapeDtypeStruct((n, qgain.shape[1]), BF16),
                   jax.ShapeDtypeStruct((2, n, D_KV_GROUPS * LANES), BF16),
                   jax.ShapeDtypeStruct((batch, 2, tiles_per_seq, vrows, tm), BF16),
                   jax.ShapeDtypeStruct((batch, tiles_per_seq, LANES, tm), F32)],
        scratch_shapes=[pltpu.VMEM((tm + HALO, cwid), F32),
                        pltpu.VMEM((SUBLANES, tm + HALO, cwid), F32)],
        compiler_params=_cparams(("arbitrary",), 52),
        name="layer1_proj",
    )(x, ya, yb, ym, w_out, norm_g.reshape(1, dm), w_c, w_d, w_rest, *tables, qgain, kgain, conv_w,
      conv_b.reshape(1, cwid), conv_norm.reshape(1, cwid))


def _compress_kernel(tk_ref, tv_ref, pk_ref, pv_ref, w1k_ref, w2k_ref, w1v_ref, w2v_ref,
                     kg_ref, c_ref, s1_ref, s2_ref, kc_ref, vct_ref):
    half = HEAD_DIM // 4 // 2
    ncp = kc_ref.shape[2]

    def mlp(t_ref, p_ref, w1_ref, w2):
        a = jnp.zeros((ncp, w1_ref.shape[3]), F32)
        b = jnp.zeros((ncp, w1_ref.shape[3]), F32)
        for l in range(CMP_STRIDE):
            x = t_ref[pl.ds(l, ncp, stride=CMP_STRIDE), :]
            a += jnp.dot((x + p_ref[l:l + 1, :]).astype(BF16), w1_ref[0, l],
                         preferred_element_type=F32)
            b += jnp.dot((x + p_ref[CMP_STRIDE + l:CMP_STRIDE + l + 1, :]).astype(BF16),
                         w1_ref[0, CMP_STRIDE + l], preferred_element_type=F32)
        h = a + pltpu.roll(b, ncp - 1, 0)
        return jnp.dot(jax.nn.silu(h).astype(BF16), w2, preferred_element_type=F32)

    kc = mlp(tk_ref, pk_ref, w1k_ref, w2k_ref[...])
    kn = kc * lax.rsqrt(_group_mean(kc * kc, HEAD_DIM) + EPS) * kg_ref[...]
    kc_ref[0, 0] = _rope(kn, c_ref[...], s1_ref[...], s2_ref[...], half).astype(BF16)
    vc = mlp(tv_ref, pv_ref, w1v_ref, w2v_ref[...])
    vct_ref[0, 0] = vc.T[0:HEAD_DIM, :].astype(BF16)


def _compress(p1, k_blk, pk, pv, w1k, w2k, w1v, w2v, kgain, tables, batch, seq):
    groups = w1k.shape[0]
    ncp = seq // CMP_STRIDE

    def full(a):
        return pl.BlockSpec(a.shape, lambda b, g: (0,) * a.ndim)

    def per_group(a):
        return pl.BlockSpec((1,) + a.shape[1:], lambda b, g: (g,) + (0,) * (a.ndim - 1))

    return pl.pallas_call(
        _compress_kernel,
        grid=(batch, groups),
        in_specs=[pl.BlockSpec((seq, LANES), lambda b, g: (b, k_blk)),
                  pl.BlockSpec((seq, LANES), lambda b, g: (b, k_blk + 1)),
                  full(pk), full(pv), per_group(w1k), full(w2k), per_group(w1v), full(w2v),
                  full(kgain), full(tables[0]), full(tables[1]), full(tables[2])],
        out_specs=[pl.BlockSpec((1, 1, ncp, LANES), lambda b, g: (b, g, 0, 0)),
                   pl.BlockSpec((1, 1, HEAD_DIM, ncp), lambda b, g: (b, g, 0, 0))],
        out_shape=[jax.ShapeDtypeStruct((batch, groups, ncp, LANES), BF16),
                   jax.ShapeDtypeStruct((batch, groups, HEAD_DIM, ncp), BF16)],
        compiler_params=_cparams(("parallel", "parallel"), 32),
        name="nsa_compress",
    )(p1, p1, pk, pv, w1k, w2k, w1v, w2v, kgain, *tables)


def _nsa_kernel(qi_tab, tile_tab, br_tab, q_ref, kc_ref, vct_ref, k_ref, vt_ref, gt_ref,
                dg_ref, ovt_ref, o_ref, acc_ref, oc_ref, sel_ref, imp_ref, sta_ref, stb_ref, cst_ref,
                cmax_ref, m_ref, qh_ref, *, tq, qt, nslc, nsel, nbelow, nentries):
    g = pl.program_id(1)
    rheads = D_HEADS // D_KV_GROUPS
    seq = qt * tq
    tiny = float(np.finfo(np.float32).tiny)
    qcol = lax.broadcasted_iota(jnp.int32, (tq, tq), 1)
    krow = lax.broadcasted_iota(jnp.int32, (tq, tq), 0)
    bufs = (sta_ref, stb_ref)
    blocks_per_tile = tq // SLC_BLOCK
    nslab = -(-nslc // SUBLANES)
    sel_rows = nslab * SUBLANES

    def rows(tile):
        return pl.ds(tile * tq if isinstance(tile, int) else pl.multiple_of(tile * tq, tq), tq)

    half_lane = lax.broadcasted_iota(jnp.int32, (1, LANES), 1) // HEAD_DIM

    def split_heads(t, c):
        for r in range(rheads):
            blk = q_ref[rows(t), (r // 2) * LANES:(r // 2 + 1) * LANES]
            qh_ref[rows(t), r * LANES:(r + 1) * LANES] = jnp.where(half_lane == r % 2, blk,
                                                                  jnp.zeros_like(blk))
        return c

    lax.fori_loop(0, qt, split_heads, 0)

    def heads_q(qi):
        return [qh_ref[rows(qi), r * LANES:(r + 1) * LANES] for r in range(rheads)]

    kc = kc_ref[0, 0]
    vct = vct_ref[0, 0]
    ncp = kc.shape[0]
    ovt = ovt_ref[...]

    def select_blocks(qi):
        q = heads_q(qi)
        ncq = min(ncp, -(-(((qi + 1) * tq - CMP_BLOCK) // CMP_STRIDE + 1) // LANES) * LANES)
        cend = lax.broadcasted_iota(jnp.int32, (ncq, tq), 0) * CMP_STRIDE + (CMP_BLOCK - 1)
        cvalid = cend <= qi * tq + lax.broadcasted_iota(jnp.int32, (ncq, tq), 1)
        psum = jnp.zeros((ncq, tq), F32)
        cmax = []
        for r in range(rheads):
            st = lax.dot_general(kc[0:ncq, :], q[r], NT_DIMS, preferred_element_type=F32)
            st = jnp.where(cvalid, st, -jnp.inf)
            cst_ref[r, 0:ncq, :] = st
            cmax.append(jnp.max(st, axis=0, keepdims=True))
        for r in range(rheads):
            m = jnp.where(jnp.isfinite(cmax[r]), cmax[r], 0.0)
            e = jnp.exp2(cst_ref[r, 0:ncq, :] - m)
            p = e / jnp.maximum(jnp.sum(e, axis=0, keepdims=True), tiny)
            psum += p
            oc_ref[qi * rheads + r] = jnp.dot(vct[:, 0:ncq], p.astype(BF16), preferred_element_type=F32)
        hi, lo = _split_bf16(psum)
        imp = (jnp.dot(ovt[:, 0:ncq], hi, preferred_element_type=F32)
               + jnp.dot(ovt[:, 0:ncq], lo, preferred_element_type=F32))
        blk = lax.broadcasted_iota(jnp.int32, (NSLC_PAD, tq), 0)
        cur = (qi * tq + lax.broadcasted_iota(jnp.int32, (NSLC_PAD, tq), 1)) // SLC_BLOCK
        imp = jnp.where(blk > cur, -jnp.inf, imp)
        forced = (blk == 0) | (blk == cur) | (blk == cur - 1)
        imp = jnp.where(forced, jnp.inf, imp)
        imp_ref[...] = imp
        nvis = min(nslc, (qi + 1) * blocks_per_tile)
        vslab = -(-nvis // SUBLANES)
        slabs = [imp[s * SUBLANES:(s + 1) * SUBLANES, :] for s in range(vslab)]
        ranks = [jnp.zeros((SUBLANES, tq), F32) for _ in range(vslab)]
        sub = lax.broadcasted_iota(jnp.int32, (SUBLANES, tq), 0)
        for jp in range(nvis):
            row = imp_ref[jp:jp + 1, :]
            for s in range(vslab):
                if s * SUBLANES > jp:
                    ahead = jnp.where(row >= slabs[s], 1.0, 0.0)
                elif (s + 1) * SUBLANES - 1 < jp:
                    ahead = jnp.where(row > slabs[s], 1.0, 0.0)
                else:
                    ahead = jnp.where(sub > jp - s * SUBLANES, jnp.where(row >= slabs[s], 1.0, 0.0),
                                      jnp.where(row > slabs[s], 1.0, 0.0))
                ranks[s] = ranks[s] + ahead
        for s in range(vslab):
            start = qi * sel_rows + s * SUBLANES
            sel_ref[start:start + SUBLANES, :] = jnp.where(ranks[s] < nsel, 1.0, 0.0)

    for qi in range(qt):
        select_blocks(qi)
    acc_ref[...] = jnp.zeros_like(acc_ref)
    m_ref[...] = jnp.full(m_ref.shape, -jnp.inf, F32)

    def chosen_rows(qi, tile):
        base = qi * sel_rows + tile * blocks_per_tile
        return [sel_ref[pl.ds(base + c, 1), :] for c in range(blocks_per_tile)]

    def scores(qi, tile, br, buf, valid):
        q = heads_q(qi)
        k = k_ref[br, rows(tile), :]
        for r in range(rheads):
            row = buf * rheads + r
            cmax_ref[row:row + 1, :] = _scores_t(k, q[r], valid, bufs[buf], r)

    def below_score_fn(e, buf):
        qi, tile = qi_tab[e], tile_tab[e]
        scores(qi, tile, 0, buf, [c > 0.5 for c in chosen_rows(qi, tile)])

    def edge_score_fn(e, buf):
        qi, tile, br = qi_tab[e], tile_tab[e], br_tab[e]
        chosen = jnp.concatenate([jnp.broadcast_to(c, (SLC_BLOCK, tq)) for c in chosen_rows(qi, tile)],
                                 axis=0)
        dpos = (qi - tile) * tq + (qcol - krow)
        reach = jnp.where(br == 0, seq, WINDOW).astype(jnp.uint32)
        kept = jnp.where(jnp.maximum(chosen, br.astype(F32)) > 0.5, dpos, -1)
        scores(qi, tile, br, buf, kept.astype(jnp.uint32) < reach)

    def accum_fn(e, buf):
        qi, tile, br = qi_tab[e], tile_tab[e], br_tab[e]
        vt = vt_ref[0, br, tile]
        for r in range(rheads):
            cm, sr = buf * rheads + r, (qi * 2 + br) * rheads + r
            m_ref[pl.ds(sr, 1), :] = _accumulate_t(bufs[buf], r, cmax_ref[cm:cm + 1, :],
                                                   m_ref[pl.ds(sr, 1), :], acc_ref, sr, vt, True)

    _pipelined_sweep(nbelow, below_score_fn, accum_fn, unroll=8)
    _pipelined_sweep(nentries - nbelow, lambda e, buf: edge_score_fn(e + nbelow, buf),
                     lambda e, buf: accum_fn(e + nbelow, buf), unroll=8)

    def finish(qi, c):
        outs = []
        for r in range(rheads):
            row0 = (g * rheads + r) * 3
            o = [oc_ref[qi * rheads + r]]
            for br in range(2):
                sr = (qi * 2 + br) * rheads + r
                o.append(acc_ref[sr, 0:HEAD_DIM, :]
                         / jnp.maximum(acc_ref[sr, HEAD_DIM:HEAD_DIM + 1, :], tiny))
            outs.append(sum(gt_ref[0, qi, pl.ds(row0 + j, 1), :] * o[j] for j in range(3)))
        out = jnp.concatenate(outs, axis=0).T
        o_ref[rows(qi), :] = (out * jax.nn.silu(dg_ref[rows(qi), :])).astype(BF16)
        return c

    lax.fori_loop(0, qt, finish, 0)


def _nsa_attn(qe, kc, vct, kboth, vtboth, gt, p1, ovt, batch, seq, tq, dg_blk):
    n = qe.shape[0]
    groups = D_KV_GROUPS
    rheads = D_HEADS // groups
    width = rheads * HEAD_DIM
    qt = seq // tq
    ncp = kc.shape[2]
    nslc = seq // SLC_BLOCK
    sel_rows = -(-nslc // SUBLANES) * SUBLANES
    wtiles = (WINDOW + tq - 1) // tq
    below = [(qi, t, 0) for qi in range(qt) for t in range(qi)]
    edge = []
    for qi in range(qt):
        edge += [(qi, qi, 0)] + [(qi, t, 1) for t in range(max(qi - wtiles, 0), qi + 1)]
    entries = below + edge
    tabs = [jnp.asarray([e[c] for e in entries], jnp.int32) for c in range(3)]
    kernel = functools.partial(_nsa_kernel, tq=tq, qt=qt, nslc=nslc, nsel=min(SLC_TOPK, nslc),
                               nbelow=len(below), nentries=len(entries))
    return pl.pallas_call(
        kernel,
        grid_spec=pltpu.PrefetchScalarGridSpec(
            num_scalar_prefetch=3,
            grid=(batch, groups),
            in_specs=[pl.BlockSpec((seq, width), lambda b, g, *_: (b, g)),
                      pl.BlockSpec((1, 1, ncp, LANES), lambda b, g, *_: (b, g, 0, 0)),
                      pl.BlockSpec((1, 1, HEAD_DIM, ncp), lambda b, g, *_: (b, g, 0, 0)),
                      pl.BlockSpec((2, seq, LANES), lambda b, g, *_: (0, b, g)),
                      pl.BlockSpec((1, 2, qt, HEAD_DIM + ONES_ROWS, tq), lambda b, g, *_: (b, 0, 0, g, 0)),
                      pl.BlockSpec((1, qt, LANES, tq), lambda b, g, *_: (b, 0, 0, 0)),
                      pl.BlockSpec((seq, width), lambda b, g, *_: (b, dg_blk + g)),
                      pl.BlockSpec(ovt.shape, lambda b, g, *_: (0, 0))],
            out_specs=pl.BlockSpec((seq, width), lambda b, g, *_: (b, g)),
            scratch_shapes=[pltpu.VMEM((qt * 2 * rheads, HEAD_DIM + ONES_ROWS, tq), F32),
                            pltpu.VMEM((qt * rheads, HEAD_DIM, tq), F32),
                            pltpu.VMEM((qt * sel_rows, tq), F32),
                            pltpu.VMEM((NSLC_PAD, tq), F32),
                            pltpu.VMEM((rheads, tq, tq), F32),
                            pltpu.VMEM((rheads, tq, tq), F32),
                            pltpu.VMEM((rheads, ncp, tq), F32),
                            pltpu.VMEM((2 * rheads, tq), F32),
                            pltpu.VMEM((qt * 2 * rheads, tq), F32),
                            pltpu.VMEM((seq, rheads * LANES), BF16)]),
        out_shape=jax.ShapeDtypeStruct((n, groups * width), BF16),
        compiler_params=_cparams(("parallel", "parallel"), 56),
        name="nsa_attn",
    )(*tabs, qe, kc, vct, kboth, vtboth, gt, p1, ovt)


def _rope_tables(pos, d):
    rd = d // 4
    half = rd // 2
    inv = ROPE_THETA ** (-jnp.arange(half, dtype=F32) / half)
    ang = pos.astype(F32)[:, None] * inv[None, :]
    cos, sin = jnp.cos(ang), jnp.sin(ang)
    npos = pos.shape[0]
    zeros = jnp.zeros((npos, d), F32)
    c = jnp.concatenate([cos, cos, jnp.ones((npos, d - rd), F32)], axis=1)
    s1 = zeros.at[:, :half].set(-sin)
    s2 = zeros.at[:, half:rd].set(sin)
    return tuple(jnp.tile(t, (1, LANES // d)) for t in (c, s1, s2))


def _overlap_t(ncp, nslc):
    start = np.arange(ncp) * CMP_STRIDE
    s0 = np.arange(nslc) * SLC_BLOCK
    lo = np.maximum(start[:, None], s0[None, :])
    hi = np.minimum(start[:, None] + CMP_BLOCK, s0[None, :] + SLC_BLOCK)
    ov = np.clip(hi - lo, 0, None) / CMP_BLOCK
    ov[ncp - 1] = 0.0
    out = np.zeros((NSLC_PAD, ncp), np.float32)
    out[:nslc] = ov.T
    return jnp.asarray(out, BF16)


def kernel(x, mem, mem_norm, l0_norm, l0_w_in, l0_a_vnorm, l0_a_ws, l0_a_bs, l0_b_qnorm, l0_b_knorm, l0_b_lq1, l0_b_lk1, l0_b_lq2, l0_b_lk2, l0_b_subln, l0_m_wkv, l0_m_qnorm, l0_m_knorm, l0_w_out, l1_norm, l1_w_in, l1_c_conv_w, l1_c_conv_b, l1_c_norm, l1_d_qnorm, l1_d_knorm, l1_d_cmp_pos_k, l1_d_cmp_w1_k, l1_d_cmp_w2_k, l1_d_cmp_pos_v, l1_d_cmp_w1_v, l1_d_cmp_w2_v, l1_m_wkv, l1_m_qnorm, l1_m_knorm, l1_w_out):
    batch, seq, d_model = x.shape
    n = batch * seq
    tq = 256
    mw = M_HEADS * HEAD_DIM
    pos = jnp.arange(seq, dtype=jnp.int32)
    x0 = x.reshape(n, d_model)

    wkv_both = jnp.concatenate([l0_m_wkv, l1_m_wkv], axis=1).astype(BF16)
    kgains = jnp.stack([jnp.tile(l0_m_knorm, M_HEADS), jnp.tile(l1_m_knorm, M_HEADS)])
    mem_kh, mem_vt = _mem_kv(mem, mem_norm, wkv_both, kgains)

    w = l0_w_in
    tab32 = _rope_tables(pos, B_QK_DIM)
    b_width = 2 * B_HEADS * B_QK_DIM
    a_width = l0_a_vnorm.shape[0]
    bs_exp = jnp.repeat(l0_a_bs.T, a_width // A_GROUPS, axis=1)
    p0, y_a, qn, kn, vt = _layer0_proj(
        x0, l0_norm, w[:, 0:1536].astype(BF16), w[:, 1536:3072].astype(BF16), w[:, 3072:4096].astype(BF16),
        tab32, jnp.tile(l0_b_qnorm, b_width // B_QK_DIM).reshape(1, b_width),
        jnp.tile(l0_b_knorm, b_width // B_QK_DIM).reshape(1, b_width),
        l0_a_ws, bs_exp, l0_a_vnorm.reshape(1, a_width), batch, seq, tq)
    lam_init = 0.8 - 0.6 * math.exp(-0.3 * 1)
    lam_p = jnp.stack([l0_b_lq1, l0_b_lk1, l0_b_lq2, l0_b_lk2])
    y_b = _diff_attn(lam_p, qn, kn, vt, p0, jnp.tile(l0_b_subln, 2).reshape(1, LANES),
                     batch, seq, tq, lam_init, 0)
    y_m = _mem_attn(p0, mem_kh, mem_vt, 0, jnp.tile(l0_m_qnorm, M_HEADS).reshape(1, mw),
                    512 // mw, 768 // mw, seq, 2 * tq)

    w = l1_w_in
    n_bg = 3 * D_HEADS
    w_rest = jnp.concatenate([w[:, 2840:3352], w[:, 2048:2304], w[:, 3352:3864]], axis=1)
    w_d = jnp.concatenate([w[:, 1536:2048], w[:, 2304:2816], w[:, 2816:2840],
                           jnp.zeros((d_model, LANES - n_bg), F32)], axis=1)
    groups, rheads = D_KV_GROUPS, D_HEADS // D_KV_GROUPS
    d_width = D_HEADS * HEAD_DIM
    tab64 = _rope_tables(pos, HEAD_DIM)
    x1, p1, y_c, qe, kboth, vtboth, gt = _layer1_proj(
        x0, y_a, y_b, y_m, l0_w_out.astype(BF16),
        l1_norm, w[:, 0:1536].astype(BF16), w_d.astype(BF16), w_rest.astype(BF16), tab64,
        jnp.tile(l1_d_qnorm, D_HEADS).reshape(1, d_width),
        jnp.tile(l1_d_knorm, LANES // HEAD_DIM).reshape(1, LANES),
        l1_c_conv_w, l1_c_conv_b, l1_c_norm, batch, seq, tq)
    kv_blk = 512 // LANES

    ncp = seq // CMP_STRIDE
    cmp_pos = jnp.arange(ncp, dtype=jnp.int32) * CMP_STRIDE + (CMP_BLOCK - 1)
    hidden = l1_d_cmp_w2_k.shape[0]

    def group_w1(w1):
        w = w1.reshape(CMP_BLOCK, HEAD_DIM, hidden)
        return jnp.stack([jnp.zeros((CMP_BLOCK, LANES, hidden), F32)
                          .at[:, g * HEAD_DIM:(g + 1) * HEAD_DIM, :].set(w) for g in range(groups)])

    w2k = jnp.tile(l1_d_cmp_w2_k, (1, LANES // HEAD_DIM))
    w2v = jnp.zeros((hidden, LANES), F32).at[:, :HEAD_DIM].set(l1_d_cmp_w2_v)
    kc, vct = _compress(
        p1, kv_blk, jnp.tile(l1_d_cmp_pos_k, (1, groups)), jnp.tile(l1_d_cmp_pos_v, (1, groups)),
        group_w1(l1_d_cmp_w1_k).astype(BF16), w2k.astype(BF16),
        group_w1(l1_d_cmp_w1_v).astype(BF16), w2v.astype(BF16),
        jnp.tile(l1_d_knorm, LANES // HEAD_DIM).reshape(1, LANES), _rope_tables(cmp_pos, HEAD_DIM),
        batch, seq)

    nslc = seq // SLC_BLOCK
    y_d = _nsa_attn(qe, kc, vct, kboth, vtboth, gt, p1, _overlap_t(ncp, nslc),
                    batch, seq, tq, 0)
    out = _mem_out(x1, y_c, y_d, p1, mem_kh, mem_vt, 1, jnp.tile(l1_m_qnorm, M_HEADS).reshape(1, mw),
                   768 // mw, 1024 // mw, l1_w_out.astype(BF16), seq, 2 * tq)
    return out.reshape(batch, seq, d_model)
```

```python
import functools
import math

import numpy as np
import jax
import jax.numpy as jnp
from jax import lax
from jax.experimental import pallas as pl
from jax.experimental.pallas import tpu as pltpu

F32 = jnp.float32
BF16 = jnp.bfloat16

EPS = 1e-6
ROPE_THETA = 500000.0
HEAD_DIM = 64
A_CHUNK = 128
A_GROUPS = 4
B_HEADS = 8
B_QK_DIM = 32
C_KERNEL = 31
D_HEADS = 8
D_KV_GROUPS = 2
CMP_BLOCK = 32
CMP_STRIDE = 16
SLC_BLOCK = 64
SLC_TOPK = 16
WINDOW = 512
M_HEADS = 4

LANES = 128
SUBLANES = 8
MXU_DIM = 256
HALO = 32
NSLC_PAD = 128
ONES_ROWS = 16

NT_DIMS = (((1,), (1,)), ((), ()))
LOG2E = math.log2(math.e)


def _cparams(semantics, vmem_mb):
    return pltpu.CompilerParams(dimension_semantics=semantics,
                                vmem_limit_bytes=vmem_mb * 1024 * 1024)


def _split_bf16(x):
    hi = x.astype(BF16)
    lo = (x - hi.astype(F32)).astype(BF16)
    return hi, lo


def _group_mean(x2, gsize):
    t, c = x2.shape
    w = min(c, MXU_DIM)
    r = lax.broadcasted_iota(jnp.int32, (w, w), 0) // gsize
    cc = lax.broadcasted_iota(jnp.int32, (w, w), 1) // gsize
    ones = jnp.where(r == cc, 1.0, 0.0).astype(BF16)
    outs = []
    for s in range(c // w):
        hi, lo = _split_bf16(x2[:, s * w:(s + 1) * w])
        outs.append(jnp.dot(hi, ones, preferred_element_type=F32)
                    + jnp.dot(lo, ones, preferred_element_type=F32))
    out = outs[0] if len(outs) == 1 else jnp.concatenate(outs, axis=1)
    return out * (1.0 / gsize)


def _tile_lanes(t, width):
    rep = width // t.shape[1]
    return t if rep == 1 else jnp.concatenate([t] * rep, axis=1)


def _rope(x, cos_t, s1_t, s2_t, half):
    width = x.shape[1]
    up = pltpu.roll(x, width - half, 1)
    dn = pltpu.roll(x, half, 1)
    return (x * _tile_lanes(cos_t, width) + up * _tile_lanes(s1_t, width)
            + dn * _tile_lanes(s2_t, width))


def _matmul_cols(h, w_ref, lo, hi, out_ref=None):
    r = jnp.dot(h, w_ref[:, lo:hi], preferred_element_type=F32)
    if out_ref is None:
        return r
    out_ref[:, lo:hi] = r


def _rms_rows(x, ng_ref):
    ms = jnp.mean(x * x, axis=-1, keepdims=True)
    return (x * lax.rsqrt(ms + EPS) * ng_ref[...]).astype(BF16)


def _scores_t(k, q, valid, st_ref, idx):
    st = lax.dot_general(k, q, NT_DIMS, preferred_element_type=F32)
    if isinstance(valid, (list, tuple)):
        blk = st.shape[0] // len(valid)
        st = jnp.concatenate([jnp.where(v, st[c * blk:(c + 1) * blk, :], -jnp.inf)
                              for c, v in enumerate(valid)], axis=0)
    elif valid is not None:
        st = jnp.where(valid, st, -jnp.inf)
    st_ref[idx] = st
    return jnp.max(st, axis=0, keepdims=True)


def _accumulate_t(st_ref, idx, cmax, m, acc_ref, aidx, vt, guard_empty):
    m_new = jnp.maximum(m, cmax)
    m_use = jnp.where(m_new == -jnp.inf, 0.0, m_new) if guard_empty else m_new
    alpha = jnp.exp2(m - m_use)
    pt = jnp.exp2((st_ref[idx] - m_use).astype(BF16))
    acc_ref[aidx] = alpha * acc_ref[aidx] + jnp.dot(vt, pt, preferred_element_type=F32)
    return m_new


def _pipelined_sweep(n, score_fn, accum_fn, unroll=4):
    score_fn(0, 0)
    stages = n - 1

    def stage(e, parity):
        score_fn(e + 1, 1 - parity)
        accum_fn(e, parity)

    def body(i, c):
        for k in range(unroll):
            stage(unroll * i + k, k % 2)
        return c

    if isinstance(n, int):
        iters, left = divmod(stages, unroll)
        if iters:
            lax.fori_loop(0, iters, body, 0)
        for k in range(left):
            stage(iters * unroll + k, k % 2)
        accum_fn(n - 1, left % 2)
        return
    lax.fori_loop(0, stages // unroll, body, 0)
    base = stages // unroll * unroll
    left = stages - base
    for k in range(unroll - 1):
        pl.when(left > k)(functools.partial(stage, base + k, k % 2))
    for parity in range(2):
        pl.when(left % 2 == parity)(functools.partial(accum_fn, n - 1, parity))


def _mix_out(ya_ref, yb_ref, ym_ref, w_ref):
    wa = ya_ref.shape[1]
    wb = yb_ref.shape[1]
    acc = jnp.dot(ya_ref[...], w_ref[0:wa, :], preferred_element_type=F32)
    acc += jnp.dot(yb_ref[...], w_ref[wa:wa + wb, :], preferred_element_type=F32)
    acc += jnp.dot(ym_ref[...], w_ref[wa + wb:, :], preferred_element_type=F32)
    return acc


def _sgu_group(u, v, gate, w_ref, bs, vgain, o_ref, cols):
    tm = u.shape[0]
    row = lax.broadcasted_iota(jnp.int32, (A_CHUNK, A_CHUNK), 0)
    col = lax.broadcasted_iota(jnp.int32, (A_CHUNK, A_CHUNK), 1)
    w = jnp.where(col <= row, w_ref, 0.0).astype(BF16)
    gv = jax.nn.gelu(v)
    ms = jnp.mean(gv * gv, axis=-1, keepdims=True)
    vn = (gv * lax.rsqrt(ms + EPS) * vgain).astype(BF16)
    gu = jax.nn.gelu(u) * jax.nn.silu(gate)
    for c in range(tm // A_CHUNK):
        rows = slice(c * A_CHUNK, (c + 1) * A_CHUNK)
        z = jnp.dot(w, vn[rows, :], preferred_element_type=F32) + bs
        o_ref[rows, cols] = (gu[rows, :] * z).astype(BF16)


def _layer0_proj_kernel(x_ref, ng_ref, wa_ref, wqkv_ref, w_ref, c_ref, s1_ref, s2_ref, qg_ref, kg_ref,
                        ws_ref, bs_ref, vg_ref, p_ref, ya_ref, qo_ref, ko_ref, vto_ref):
    h = _rms_rows(x_ref[...], ng_ref)
    aw = vg_ref.shape[1]
    gdim = aw // A_GROUPS
    width = qg_ref.shape[1]
    half = B_QK_DIM // 4 // 2
    pw = p_ref.shape[1]

    a = _matmul_cols(h, wa_ref, 0, 3 * aw)

    def sgu(g):
        cols = slice(g * gdim, (g + 1) * gdim)
        _sgu_group(a[:, cols], a[:, aw + g * gdim:aw + (g + 1) * gdim],
                   a[:, 2 * aw + g * gdim:2 * aw + (g + 1) * gdim],
                   ws_ref[g], bs_ref[:, cols], vg_ref[:, cols], ya_ref, cols)

    def prep(t, gain, dst, mul):
        tn = t * lax.rsqrt(_group_mean(t * t, B_QK_DIM) + EPS) * gain[...]
        tr = _rope(tn, c_ref[...], s1_ref[...], s2_ref[...], half)
        dst[...] = (tr * mul if mul != 1.0 else tr).astype(BF16)

    q = _matmul_cols(h, wqkv_ref, 0, width)
    sgu(0)
    k = _matmul_cols(h, wqkv_ref, width, 2 * width)
    sgu(1)
    v = _matmul_cols(h, wqkv_ref, 2 * width, 3 * width)
    sgu(2)
    _matmul_cols(h, w_ref, 0, pw // 2, p_ref)
    sgu(3)
    prep(q, qg_ref, qo_ref, B_QK_DIM ** -0.5 * LOG2E)
    _matmul_cols(h, w_ref, pw // 2, pw, p_ref)
    prep(k, kg_ref, ko_ref, 1.0)
    vt = v.T
    dv = 2 * B_QK_DIM
    ones = jnp.ones((ONES_ROWS, vt.shape[1]), F32)
    parts = []
    for hd in range(B_HEADS):
        parts += [vt[hd * dv:(hd + 1) * dv, :], ones]
    vto_ref[0, 0] = jnp.concatenate(parts, axis=0).astype(BF16)


def _layer0_proj(x, norm_g, w_a, w_qkv, w_rest, tables, qgain, kgain, a_ws, bs_exp, vgain, batch, seq, tm):
    n, d = x.shape
    width = qgain.shape[1]
    aw = vgain.shape[1]
    tiles_per_seq = seq // tm
    tab_spec = pl.BlockSpec((tm, LANES), lambda i: (i % tiles_per_seq, 0))
    vrows = B_HEADS * (2 * B_QK_DIM + ONES_ROWS)

    def full(a):
        return pl.BlockSpec(a.shape, lambda i: (0,) * a.ndim)

    def rows(w):
        return pl.BlockSpec((tm, w), lambda i: (i, 0))

    return pl.pallas_call(
        _layer0_proj_kernel,
        grid=(n // tm,),
        in_specs=[rows(d), pl.BlockSpec((1, d), lambda i: (0, 0)), full(w_a), full(w_qkv), full(w_rest),
                  tab_spec, tab_spec, tab_spec, full(qgain), full(kgain),
                  full(a_ws), full(bs_exp), full(vgain)],
        out_specs=[rows(w_rest.shape[1]), rows(aw), rows(width), rows(width),
                   pl.BlockSpec((1, 1, vrows, tm),
                                lambda i: (i // tiles_per_seq, i % tiles_per_seq, 0, 0))],
        out_shape=[jax.ShapeDtypeStruct((n, w_rest.shape[1]), F32),
                   jax.ShapeDtypeStruct((n, aw), BF16),
                   jax.ShapeDtypeStruct((n, width), BF16), jax.ShapeDtypeStruct((n, width), BF16),
                   jax.ShapeDtypeStruct((batch, tiles_per_seq, vrows, tm), BF16)],
        compiler_params=_cparams(("parallel",), 48),
        name="layer0_proj",
    )(x, norm_g.reshape(1, d), w_a, w_qkv, w_rest, *tables, qgain, kgain, a_ws, bs_exp, vgain)


def _diff_attn_kernel(qi_tab, tile_tab, lam_ref, q_ref, k_ref, vt_ref, g_ref, sub_ref, o_ref,
                      acc_ref, sta_ref, stb_ref, cmax_ref, m_ref, *, tq, qt, lam_init):
    dv = 2 * B_QK_DIM
    vrows = dv + ONES_ROWS
    nmap = 4
    lane = lax.broadcasted_iota(jnp.int32, (1, LANES), 1)
    bufs = (sta_ref, stb_ref)
    acc_ref[...] = jnp.zeros_like(acc_ref)
    m_ref[...] = jnp.full(m_ref.shape, -jnp.inf, F32)
    krow = lax.broadcasted_iota(jnp.int32, (tq, tq), 0)
    qcol = lax.broadcasted_iota(jnp.int32, (tq, tq), 1)
    lam_p = lam_ref[...]
    lam = (jnp.exp(jnp.sum(lam_p[0:1] * lam_p[1:2], axis=-1, keepdims=True))
           - jnp.exp(jnp.sum(lam_p[2:3] * lam_p[3:4], axis=-1, keepdims=True)) + lam_init)

    def rows(tile):
        return pl.ds(pl.multiple_of(tile * tq, tq), tq)

    def scores(qi, tile, buf, valid):
        q = q_ref[rows(qi), :]
        k = k_ref[rows(tile), :]
        zero = jnp.zeros_like(q)
        for j in range(nmap):
            qm = jnp.where(lane // B_QK_DIM == j, q, zero)
            row = buf * nmap + j
            cmax_ref[row:row + 1, :] = _scores_t(k, qm, valid, bufs[buf], j)

    def accumulate(qi, tile, buf):
        for j in range(nmap):
            vt = vt_ref[0, tile, (j // 2) * vrows:(j // 2 + 1) * vrows, :]
            cm, sr = buf * nmap + j, qi * nmap + j
            m_ref[pl.ds(sr, 1), :] = _accumulate_t(bufs[buf], j, cmax_ref[cm:cm + 1, :],
                                                   m_ref[pl.ds(sr, 1), :], acc_ref, sr, vt, False)

    def finish(qi):
        halves = []
        for h in range(2):
            o = [acc_ref[qi * nmap + 2 * h + mp, 0:dv, :] / acc_ref[qi * nmap + 2 * h + mp, dv:dv + 1, :]
                 for mp in range(2)]
            pd = o[0] - lam * o[1]
            ms2 = jnp.mean(pd * pd, axis=0, keepdims=True)
            halves.append(pd * lax.rsqrt(ms2 + EPS))
        ob = jnp.concatenate(halves, axis=0).T * sub_ref[...] * (1.0 - lam_init)
        o_ref[rows(qi), :] = (ob * jax.nn.silu(g_ref[rows(qi), :])).astype(BF16)

    def below_score(e, buf):
        scores(qi_tab[e], tile_tab[e], buf, None)

    def below_accum(e, buf):
        accumulate(qi_tab[e], tile_tab[e], buf)

    def diag_score(e, buf):
        scores(e, e, buf, krow <= qcol)

    def diag_accum(e, buf):
        accumulate(e, e, buf)
        finish(e)

    _pipelined_sweep(qt * (qt - 1) // 2, below_score, below_accum, unroll=16)
    _pipelined_sweep(qt, diag_score, diag_accum)


def _diff_attn(lam_p, qn, kn, vt, p0, subln, batch, seq, tq, lam_init, g_off):
    n, width = qn.shape
    pairs = width // LANES
    qt = seq // tq
    vrows = 2 * B_QK_DIM + ONES_ROWS
    below = [(qi, t) for qi in range(qt) for t in range(qi)]
    qi_tab = jnp.asarray([e[0] for e in below], jnp.int32)
    tile_tab = jnp.asarray([e[1] for e in below], jnp.int32)
    kernel = functools.partial(_diff_attn_kernel, tq=tq, qt=qt, lam_init=lam_init)
    seq_spec = pl.BlockSpec((seq, LANES), lambda b, p, *_: (b, p))
    return pl.pallas_call(
        kernel,
        grid_spec=pltpu.PrefetchScalarGridSpec(
            num_scalar_prefetch=2,
            grid=(batch, pairs),
            in_specs=[pl.BlockSpec(lam_p.shape, lambda b, p, *_: (0, 0)),
                      seq_spec, seq_spec,
                      pl.BlockSpec((1, qt, 2 * vrows, tq), lambda b, p, *_: (b, 0, p, 0)),
                      pl.BlockSpec((seq, LANES), lambda b, p, *_: (b, g_off + p)),
                      pl.BlockSpec((1, LANES), lambda b, p, *_: (0, 0))],
            out_specs=seq_spec,
            scratch_shapes=[pltpu.VMEM((qt * 4, vrows, tq), F32), pltpu.VMEM((4, tq, tq), F32),
                            pltpu.VMEM((4, tq, tq), F32), pltpu.VMEM((8, tq), F32),
                            pltpu.VMEM((qt * 4, tq), F32)]),
        out_shape=jax.ShapeDtypeStruct((n, width), BF16),
        compiler_params=_cparams(("parallel", "parallel"), 40),
        name="diff_attn",
    )(qi_tab, tile_tab, lam_p, qn, kn, vt, p0, subln)


def _mem_kv_kernel(mem_ref, g_ref, w_ref, kg_ref, kh_ref, vt_ref):
    x = mem_ref[0]
    ms = jnp.mean(x * x, axis=-1, keepdims=True)
    h = (x * lax.rsqrt(ms + EPS) * g_ref[...]).astype(BF16)
    kv = jnp.dot(h, w_ref[...], preferred_element_type=F32)
    mw = kv.shape[1] // 4
    lane = lax.broadcasted_iota(jnp.int32, (1, mw), 1)
    ones = jnp.ones((ONES_ROWS, kv.shape[0]), F32)
    for layer in range(2):
        k = kv[:, (2 * layer) * mw:(2 * layer + 1) * mw]
        kn = k * lax.rsqrt(_group_mean(k * k, HEAD_DIM) + EPS) * kg_ref[layer:layer + 1, :]
        vt = kv[:, (2 * layer + 1) * mw:(2 * layer + 2) * mw].T
        parts = []
        for hd in range(M_HEADS):
            kh_ref[layer, 0, hd] = jnp.where(lane // HEAD_DIM == hd, kn, 0.0).astype(BF16)
            parts += [vt[hd * HEAD_DIM:(hd + 1) * HEAD_DIM, :], ones]
        vt_ref[layer, 0] = jnp.concatenate(parts, axis=0).astype(BF16)


def _mem_kv(mem, mem_norm, wkv_both, kgains):
    batch, mtok, d = mem.shape
    mw = wkv_both.shape[1] // 4
    vrows = M_HEADS * (HEAD_DIM + ONES_ROWS)
    return pl.pallas_call(
        _mem_kv_kernel,
        grid=(batch,),
        in_specs=[pl.BlockSpec((1, mtok, d), lambda b: (b, 0, 0)),
                  pl.BlockSpec((1, d), lambda b: (0, 0)),
                  pl.BlockSpec(wkv_both.shape, lambda b: (0, 0)),
                  pl.BlockSpec(kgains.shape, lambda b: (0, 0))],
        out_specs=[pl.BlockSpec((2, 1, M_HEADS, mtok, mw), lambda b: (0, b, 0, 0, 0)),
                   pl.BlockSpec((2, 1, vrows, mtok), lambda b: (0, b, 0, 0))],
        out_shape=[jax.ShapeDtypeStruct((2, batch, M_HEADS, mtok, mw), BF16),
                   jax.ShapeDtypeStruct((2, batch, vrows, mtok), BF16)],
        compiler_params=_cparams(("parallel",), 32),
        name="mem_kv",
    )(mem, mem_norm.reshape(1, d), wkv_both, kgains)


def _mem_attn_tile(q_ref, g_ref, kh_ref, vt_ref, qg_ref, st_ref):
    vrows = HEAD_DIM + ONES_ROWS
    x = q_ref[...]
    qn = (x * lax.rsqrt(_group_mean(x * x, HEAD_DIM) + EPS) * qg_ref[...]
          * (HEAD_DIM ** -0.5 * LOG2E)).astype(BF16)
    cmax = [_scores_t(kh_ref[0, 0, h], qn, None, st_ref, h) for h in range(M_HEADS)]
    outs = []
    for h in range(M_HEADS):
        pt = jnp.exp2((st_ref[h] - cmax[h]).astype(BF16))
        acc = jnp.dot(vt_ref[0, 0, h * vrows:(h + 1) * vrows, :], pt, preferred_element_type=F32)
        outs.append(acc[0:HEAD_DIM, :] / acc[HEAD_DIM:HEAD_DIM + 1, :])
    out = jnp.concatenate(outs, axis=0).T
    return (out * jax.nn.silu(g_ref[...])).astype(BF16)


def _mem_attn_kernel(q_ref, g_ref, kh_ref, vt_ref, qg_ref, o_ref, st_ref):
    o_ref[...] = _mem_attn_tile(q_ref, g_ref, kh_ref, vt_ref, qg_ref, st_ref)


def _mem_out_kernel(x_ref, ya_ref, yb_ref, q_ref, g_ref, kh_ref, vt_ref, qg_ref, w_ref, o_ref, st_ref):
    ym = _mem_attn_tile(q_ref, g_ref, kh_ref, vt_ref, qg_ref, st_ref)
    wa = ya_ref.shape[1]
    wb = yb_ref.shape[1]
    acc = jnp.dot(ya_ref[...], w_ref[0:wa, :], preferred_element_type=F32)
    acc += jnp.dot(yb_ref[...], w_ref[wa:wa + wb, :], preferred_element_type=F32)
    acc += jnp.dot(ym, w_ref[wa + wb:, :], preferred_element_type=F32)
    o_ref[...] = x_ref[...] + acc


def _mem_out(x, ya, yb, p, kh, vt, layer, qgain, q_blk, g_blk, w, seq, tq):
    n, d = x.shape
    _, batch, heads, mtok, mw = kh.shape
    qt = seq // tq

    def rows(width, blk=0):
        return pl.BlockSpec((tq, width), lambda i: (i, blk))

    return pl.pallas_call(
        _mem_out_kernel,
        grid=(n // tq,),
        in_specs=[rows(d), rows(ya.shape[1]), rows(yb.shape[1]), rows(mw, q_blk), rows(mw, g_blk),
                  pl.BlockSpec((1, 1, heads, mtok, mw), lambda i: (layer, i // qt, 0, 0, 0)),
                  pl.BlockSpec((1, 1, vt.shape[2], mtok), lambda i: (layer, i // qt, 0, 0)),
                  pl.BlockSpec((1, mw), lambda i: (0, 0)),
                  pl.BlockSpec(w.shape, lambda i: (0, 0))],
        out_specs=rows(d),
        out_shape=jax.ShapeDtypeStruct((n, d), F32),
        scratch_shapes=[pltpu.VMEM((heads, mtok, tq), F32)],
        compiler_params=_cparams(("parallel",), 40),
        name="mem_out_proj",
    )(x, ya, yb, p, p, kh, vt, qgain, w)


def _mem_attn(p, kh, vt, layer, qgain, q_blk, g_blk, seq, tq):
    n = p.shape[0]
    _, batch, heads, mtok, mw = kh.shape
    qt = seq // tq
    return pl.pallas_call(
        _mem_attn_kernel,
        grid=(n // tq,),
        in_specs=[pl.BlockSpec((tq, mw), lambda i: (i, q_blk)),
                  pl.BlockSpec((tq, mw), lambda i: (i, g_blk)),
                  pl.BlockSpec((1, 1, heads, mtok, mw), lambda i: (layer, i // qt, 0, 0, 0)),
                  pl.BlockSpec((1, 1, vt.shape[2], mtok), lambda i: (layer, i // qt, 0, 0)),
                  pl.BlockSpec((1, mw), lambda i: (0, 0))],
        out_specs=pl.BlockSpec((tq, mw), lambda i: (i, 0)),
        out_shape=jax.ShapeDtypeStruct((n, mw), BF16),
        scratch_shapes=[pltpu.VMEM((heads, mtok, tq), F32)],
        compiler_params=_cparams(("parallel",), 32),
        name="mem_attn",
    )(p, p, kh, vt, qgain)


def _layer1_proj_kernel(x_ref, ya_ref, yb_ref, ym_ref, wo_ref, ng_ref, wc_ref, wd_ref, w_ref,
                        c_ref, s1_ref, s2_ref, qg_ref, kg_ref, cw_ref, cb_ref, cn_ref,
                        x1_ref, p_ref, yc_ref, qo_ref, ko_ref, vto_ref, gto_ref,
                        hbuf_ref, shift_ref, *, tiles_per_seq):
    i = pl.program_id(0)
    ts = x_ref.shape[0]
    span = ts + HALO
    cwid = cn_ref.shape[1]
    qw = qg_ref.shape[1]
    pw = p_ref.shape[1]
    half = HEAD_DIM // 4 // 2
    x1 = x_ref[...] + _mix_out(ya_ref, yb_ref, ym_ref, wo_ref)
    x1_ref[...] = x1
    h = _rms_rows(x1, ng_ref)

    @pl.when(i % tiles_per_seq == 0)
    def _():
        hbuf_ref[0:HALO, :] = jnp.zeros((HALO, cwid), F32)

    ca = _matmul_cols(h, wc_ref, 0, cwid)
    cbv = _matmul_cols(h, wc_ref, cwid, 2 * cwid)
    q = _matmul_cols(h, wd_ref, 0, qw)
    hbuf_ref[HALO:HALO + ts, :] = ca * jax.nn.sigmoid(cbv)
    for ph in range(SUBLANES):
        shift_ref[ph, 0:span - ph, :] = hbuf_ref[ph:span, :]
    cg = _matmul_cols(h, wc_ref, 2 * cwid, 3 * cwid)
    d = _matmul_cols(h, wd_ref, qw, wd_ref.shape[1])
    first = HALO - (C_KERNEL - 1)
    y = cb_ref[...]
    for j in range(C_KERNEL):
        ph, base = (first + j) % SUBLANES, (first + j) // SUBLANES * SUBLANES
        y = y + shift_ref[ph, base:base + ts, :] * cw_ref[j:j + 1, :]
    psplit = -(-pw // (2 * MXU_DIM)) * MXU_DIM
    _matmul_cols(h, w_ref, 0, psplit, p_ref)
    ms = jnp.mean(y * y, axis=-1, keepdims=True)
    yn = y * lax.rsqrt(ms + EPS) * cn_ref[...]
    yc_ref[...] = (jax.nn.silu(yn) * jax.nn.silu(cg)).astype(BF16)
    hbuf_ref[0:HALO, :] = hbuf_ref[ts:ts + HALO, :]
    _matmul_cols(h, w_ref, psplit, pw, p_ref)

    tabs = (c_ref[...], s1_ref[...], s2_ref[...])
    tn = q * lax.rsqrt(_group_mean(q * q, HEAD_DIM) + EPS) * qg_ref[...]
    qo_ref[...] = (_rope(tn, *tabs, half) * (HEAD_DIM ** -0.5 * LOG2E)).astype(BF16)
    ones = jnp.ones((ONES_ROWS, ts), F32)
    lane = lax.broadcasted_iota(jnp.int32, (1, LANES), 1)
    for br in range(2):
        k = d[:, 2 * br * LANES:(2 * br + 1) * LANES]
        kn = k * lax.rsqrt(_group_mean(k * k, HEAD_DIM) + EPS) * kg_ref[...]
        kr = _rope(kn, *tabs, half)
        swapped = pltpu.roll(kr, HEAD_DIM, 1)
        for g in range(D_KV_GROUPS):
            own = (lane // HEAD_DIM) == g
            ko_ref[br, :, g * LANES:(g + 1) * LANES] = jnp.where(own, kr, swapped).astype(BF16)
        vt = d[:, (2 * br + 1) * LANES:(2 * br + 2) * LANES].T
        parts = []
        for g in range(D_KV_GROUPS):
            parts += [vt[g * HEAD_DIM:(g + 1) * HEAD_DIM, :], ones]
        vto_ref[0, br, 0] = jnp.concatenate(parts, axis=0).astype(BF16)
    gto_ref[0, 0] = jax.nn.sigmoid(d[:, 4 * LANES:5 * LANES]).T


def _layer1_proj(x, ya, yb, ym, w_out, norm_g, w_c, w_d, w_rest, tables, qgain, kgain,
                 conv_w, conv_b, conv_norm, batch, seq, tm):
    n, dm = x.shape
    cwid = conv_w.shape[1]
    tiles_per_seq = seq // tm
    tab_spec = pl.BlockSpec((tm, LANES), lambda i: (i % tiles_per_seq, 0))
    vrows = D_KV_GROUPS * (HEAD_DIM + ONES_ROWS)

    def full(a):
        return pl.BlockSpec(a.shape, lambda i: (0,) * a.ndim)

    def rows(w):
        return pl.BlockSpec((tm, w), lambda i: (i, 0))

    kernel = functools.partial(_layer1_proj_kernel, tiles_per_seq=tiles_per_seq)
    return pl.pallas_call(
        kernel,
        grid=(n // tm,),
        in_specs=[rows(dm), rows(ya.shape[1]), rows(yb.shape[1]), rows(ym.shape[1]), full(w_out),
                  pl.BlockSpec((1, dm), lambda i: (0, 0)), full(w_c), full(w_d), full(w_rest),
                  tab_spec, tab_spec, tab_spec, full(qgain), full(kgain),
                  full(conv_w), pl.BlockSpec((1, cwid), lambda i: (0, 0)),
                  pl.BlockSpec((1, cwid), lambda i: (0, 0))],
        out_specs=[rows(dm), rows(w_rest.shape[1]), rows(cwid), rows(qgain.shape[1]),
                   pl.BlockSpec((2, tm, D_KV_GROUPS * LANES), lambda i: (0, i, 0)),
                   pl.BlockSpec((1, 2, 1, vrows, tm),
                                lambda i: (i // tiles_per_seq, 0, i % tiles_per_seq, 0, 0)),
                   pl.BlockSpec((1, 1, LANES, tm),
                                lambda i: (i // tiles_per_seq, i % tiles_per_seq, 0, 0))],
        out_shape=[jax.ShapeDtypeStruct((n, dm), F32),
                   jax.ShapeDtypeStruct((n, w_rest.shape[1]), F32),
                   jax.ShapeDtypeStruct((n, cwid), BF16),
                   jax.ShapeDtypeStruct((n, qgain.shape[1]), BF16),
                   jax.ShapeDtypeStruct((2, n, D_KV_GROUPS * LANES), BF16),
                   jax.ShapeDtypeStruct((batch, 2, tiles_per_seq, vrows, tm), BF16),
                   jax.ShapeDtypeStruct((batch, tiles_per_seq, LANES, tm), F32)],
        scratch_shapes=[pltpu.VMEM((tm + HALO, cwid), F32),
                        pltpu.VMEM((SUBLANES, tm + HALO, cwid), F32)],
        compiler_params=_cparams(("arbitrary",), 52),
        name="layer1_proj",
    )(x, ya, yb, ym, w_out, norm_g.reshape(1, dm), w_c, w_d, w_rest, *tables, qgain, kgain, conv_w,
      conv_b.reshape(1, cwid), conv_norm.reshape(1, cwid))


def _compress_kernel(tk_ref, tv_ref, pk_ref, pv_ref, w1k_ref, w2k_ref, w1v_ref, w2v_ref,
                     kg_ref, c_ref, s1_ref, s2_ref, kc_ref, vct_ref):
    half = HEAD_DIM // 4 // 2
    ncp = kc_ref.shape[2]

    def mlp(t_ref, p_ref, w1_ref, w2):
        a = jnp.zeros((ncp, w1_ref.shape[3]), F32)
        b = jnp.zeros((ncp, w1_ref.shape[3]), F32)
        for l in range(CMP_STRIDE):
            x = t_ref[pl.ds(l, ncp, stride=CMP_STRIDE), :]
            a += jnp.dot((x + p_ref[l:l + 1, :]).astype(BF16), w1_ref[0, l],
                         preferred_element_type=F32)
            b += jnp.dot((x + p_ref[CMP_STRIDE + l:CMP_STRIDE + l + 1, :]).astype(BF16),
                         w1_ref[0, CMP_STRIDE + l], preferred_element_type=F32)
        h = a + pltpu.roll(b, ncp - 1, 0)
        return jnp.dot(jax.nn.silu(h).astype(BF16), w2, preferred_element_type=F32)

    kc = mlp(tk_ref, pk_ref, w1k_ref, w2k_ref[...])
    kn = kc * lax.rsqrt(_group_mean(kc * kc, HEAD_DIM) + EPS) * kg_ref[...]
    kc_ref[0, 0] = _rope(kn, c_ref[...], s1_ref[...], s2_ref[...], half).astype(BF16)
    vc = mlp(tv_ref, pv_ref, w1v_ref, w2v_ref[...])
    vct_ref[0, 0] = vc.T[0:HEAD_DIM, :].astype(BF16)


def _compress(p1, k_blk, pk, pv, w1k, w2k, w1v, w2v, kgain, tables, batch, seq):
    groups = w1k.shape[0]
    ncp = seq // CMP_STRIDE

    def full(a):
        return pl.BlockSpec(a.shape, lambda b, g: (0,) * a.ndim)

    def per_group(a):
        return pl.BlockSpec((1,) + a.shape[1:], lambda b, g: (g,) + (0,) * (a.ndim - 1))

    return pl.pallas_call(
        _compress_kernel,
        grid=(batch, groups),
        in_specs=[pl.BlockSpec((seq, LANES), lambda b, g: (b, k_blk)),
                  pl.BlockSpec((seq, LANES), lambda b, g: (b, k_blk + 1)),
                  full(pk), full(pv), per_group(w1k), full(w2k), per_group(w1v), full(w2v),
                  full(kgain), full(tables[0]), full(tables[1]), full(tables[2])],
        out_specs=[pl.BlockSpec((1, 1, ncp, LANES), lambda b, g: (b, g, 0, 0)),
                   pl.BlockSpec((1, 1, HEAD_DIM, ncp), lambda b, g: (b, g, 0, 0))],
        out_shape=[jax.ShapeDtypeStruct((batch, groups, ncp, LANES), BF16),
                   jax.ShapeDtypeStruct((batch, groups, HEAD_DIM, ncp), BF16)],
        compiler_params=_cparams(("parallel", "parallel"), 32),
        name="nsa_compress",
    )(p1, p1, pk, pv, w1k, w2k, w1v, w2v, kgain, *tables)


def _nsa_kernel(qi_tab, tile_tab, br_tab, q_ref, kc_ref, vct_ref, k_ref, vt_ref, gt_ref,
                dg_ref, ovt_ref, o_ref, acc_ref, oc_ref, sel_ref, imp_ref, sta_ref, stb_ref, cst_ref,
                cmax_ref, m_ref, qh_ref, *, tq, qt, nslc, nsel, nbelow, nentries):
    g = pl.program_id(1)
    rheads = D_HEADS // D_KV_GROUPS
    seq = qt * tq
    tiny = float(np.finfo(np.float32).tiny)
    qcol = lax.broadcasted_iota(jnp.int32, (tq, tq), 1)
    krow = lax.broadcasted_iota(jnp.int32, (tq, tq), 0)
    bufs = (sta_ref, stb_ref)
    blocks_per_tile = tq // SLC_BLOCK
    nslab = -(-nslc // SUBLANES)
    sel_rows = nslab * SUBLANES

    def rows(tile):
        return pl.ds(tile * tq if isinstance(tile, int) else pl.multiple_of(tile * tq, tq), tq)

    half_lane = lax.broadcasted_iota(jnp.int32, (1, LANES), 1) // HEAD_DIM

    def split_heads(t, c):
        for r in range(rheads):
            blk = q_ref[rows(t), (r // 2) * LANES:(r // 2 + 1) * LANES]
            qh_ref[rows(t), r * LANES:(r + 1) * LANES] = jnp.where(half_lane == r % 2, blk,
                                                                  jnp.zeros_like(blk))
        return c

    lax.fori_loop(0, qt, split_heads, 0)

    def heads_q(qi):
        return [qh_ref[rows(qi), r * LANES:(r + 1) * LANES] for r in range(rheads)]

    kc = kc_ref[0, 0]
    vct = vct_ref[0, 0]
    ncp = kc.shape[0]
    ovt = ovt_ref[...]

    def select_blocks(qi):
        q = heads_q(qi)
        ncq = min(ncp, -(-(((qi + 1) * tq - CMP_BLOCK) // CMP_STRIDE + 1) // LANES) * LANES)
        cend = lax.broadcasted_iota(jnp.int32, (ncq, tq), 0) * CMP_STRIDE + (CMP_BLOCK - 1)
        cvalid = cend <= qi * tq + lax.broadcasted_iota(jnp.int32, (ncq, tq), 1)
        psum = jnp.zeros((ncq, tq), F32)
        cmax = []
        for r in range(rheads):
            st = lax.dot_general(kc[0:ncq, :], q[r], NT_DIMS, preferred_element_type=F32)
            st = jnp.where(cvalid, st, -jnp.inf)
            cst_ref[r, 0:ncq, :] = st
            cmax.append(jnp.max(st, axis=0, keepdims=True))
        for r in range(rheads):
            m = jnp.where(jnp.isfinite(cmax[r]), cmax[r], 0.0)
            e = jnp.exp2(cst_ref[r, 0:ncq, :] - m)
            p = e / jnp.maximum(jnp.sum(e, axis=0, keepdims=True), tiny)
            psum += p
            oc_ref[qi * rheads + r] = jnp.dot(vct[:, 0:ncq], p.astype(BF16), preferred_element_type=F32)
        hi, lo = _split_bf16(psum)
        imp = (jnp.dot(ovt[:, 0:ncq], hi, preferred_element_type=F32)
               + jnp.dot(ovt[:, 0:ncq], lo, preferred_element_type=F32))
        blk = lax.broadcasted_iota(jnp.int32, (NSLC_PAD, tq), 0)
        cur = (qi * tq + lax.broadcasted_iota(jnp.int32, (NSLC_PAD, tq), 1)) // SLC_BLOCK
        imp = jnp.where(blk > cur, -jnp.inf, imp)
        forced = (blk == 0) | (blk == cur) | (blk == cur - 1)
        imp = jnp.where(forced, jnp.inf, imp)
        imp_ref[...] = imp
        nvis = min(nslc, (qi + 1) * blocks_per_tile)
        vslab = -(-nvis // SUBLANES)
        slabs = [imp[s * SUBLANES:(s + 1) * SUBLANES, :] for s in range(vslab)]
        ranks = [jnp.zeros((SUBLANES, tq), F32) for _ in range(vslab)]
        sub = lax.broadcasted_iota(jnp.int32, (SUBLANES, tq), 0)
        for jp in range(nvis):
            row = imp_ref[jp:jp + 1, :]
            for s in range(vslab):
                if s * SUBLANES > jp:
                    ahead = jnp.where(row >= slabs[s], 1.0, 0.0)
                elif (s + 1) * SUBLANES - 1 < jp:
                    ahead = jnp.where(row > slabs[s], 1.0, 0.0)
                else:
                    ahead = jnp.where(sub > jp - s * SUBLANES, jnp.where(row >= slabs[s], 1.0, 0.0),
                                      jnp.where(row > slabs[s], 1.0, 0.0))
                ranks[s] = ranks[s] + ahead
        for s in range(vslab):
            start = qi * sel_rows + s * SUBLANES
            sel_ref[start:start + SUBLANES, :] = jnp.where(ranks[s] < nsel, 1.0, 0.0)

    for qi in range(qt):
        select_blocks(qi)
    acc_ref[...] = jnp.zeros_like(acc_ref)
    m_ref[...] = jnp.full(m_ref.shape, -jnp.inf, F32)

    def chosen_rows(qi, tile):
        base = qi * sel_rows + tile * blocks_per_tile
        return [sel_ref[pl.ds(base + c, 1), :] for c in range(blocks_per_tile)]

    def scores(qi, tile, br, buf, valid):
        q = heads_q(qi)
        k = k_ref[br, rows(tile), :]
        for r in range(rheads):
            row = buf * rheads + r
            cmax_ref[row:row + 1, :] = _scores_t(k, q[r], valid, bufs[buf], r)

    def below_score_fn(e, buf):
        qi, tile = qi_tab[e], tile_tab[e]
        scores(qi, tile, 0, buf, [c > 0.5 for c in chosen_rows(qi, tile)])

    def edge_score_fn(e, buf):
        qi, tile, br = qi_tab[e], tile_tab[e], br_tab[e]
        chosen = jnp.concatenate([jnp.broadcast_to(c, (SLC_BLOCK, tq)) for c in chosen_rows(qi, tile)],
                                 axis=0)
        dpos = (qi - tile) * tq + (qcol - krow)
        reach = jnp.where(br == 0, seq, WINDOW).astype(jnp.uint32)
        kept = jnp.where(jnp.maximum(chosen, br.astype(F32)) > 0.5, dpos, -1)
        scores(qi, tile, br, buf, kept.astype(jnp.uint32) < reach)

    def accum_fn(e, buf):
        qi, tile, br = qi_tab[e], tile_tab[e], br_tab[e]
        vt = vt_ref[0, br, tile]
        for r in range(rheads):
            cm, sr = buf * rheads + r, (qi * 2 + br) * rheads + r
            m_ref[pl.ds(sr, 1), :] = _accumulate_t(bufs[buf], r, cmax_ref[cm:cm + 1, :],
                                                   m_ref[pl.ds(sr, 1), :], acc_ref, sr, vt, True)

    _pipelined_sweep(nbelow, below_score_fn, accum_fn, unroll=16)
    _pipelined_sweep(nentries - nbelow, lambda e, buf: edge_score_fn(e + nbelow, buf),
                     lambda e, buf: accum_fn(e + nbelow, buf), unroll=8)

    def finish(qi, c):
        outs = []
        for r in range(rheads):
            row0 = (g * rheads + r) * 3
            o = [oc_ref[qi * rheads + r]]
            for br in range(2):
                sr = (qi * 2 + br) * rheads + r
                o.append(acc_ref[sr, 0:HEAD_DIM, :]
                         / jnp.maximum(acc_ref[sr, HEAD_DIM:HEAD_DIM + 1, :], tiny))
            outs.append(sum(gt_ref[0, qi, pl.ds(row0 + j, 1), :] * o[j] for j in range(3)))
        out = jnp.concatenate(outs, axis=0).T
        o_ref[rows(qi), :] = (out * jax.nn.silu(dg_ref[rows(qi), :])).astype(BF16)
        return c

    lax.fori_loop(0, qt, finish, 0)


def _nsa_attn(qe, kc, vct, kboth, vtboth, gt, p1, ovt, batch, seq, tq, dg_blk):
    n = qe.shape[0]
    groups = D_KV_GROUPS
    rheads = D_HEADS // groups
    width = rheads * HEAD_DIM
    qt = seq // tq
    ncp = kc.shape[2]
    nslc = seq // SLC_BLOCK
    sel_rows = -(-nslc // SUBLANES) * SUBLANES
    wtiles = (WINDOW + tq - 1) // tq
    below = [(qi, t, 0) for qi in range(qt) for t in range(qi)]
    edge = []
    for qi in range(qt):
        edge += [(qi, qi, 0)] + [(qi, t, 1) for t in range(max(qi - wtiles, 0), qi + 1)]
    entries = below + edge
    tabs = [jnp.asarray([e[c] for e in entries], jnp.int32) for c in range(3)]
    kernel = functools.partial(_nsa_kernel, tq=tq, qt=qt, nslc=nslc, nsel=min(SLC_TOPK, nslc),
                               nbelow=len(below), nentries=len(entries))
    return pl.pallas_call(
        kernel,
        grid_spec=pltpu.PrefetchScalarGridSpec(
            num_scalar_prefetch=3,
            grid=(batch, groups),
            in_specs=[pl.BlockSpec((seq, width), lambda b, g, *_: (b, g)),
                      pl.BlockSpec((1, 1, ncp, LANES), lambda b, g, *_: (b, g, 0, 0)),
                      pl.BlockSpec((1, 1, HEAD_DIM, ncp), lambda b, g, *_: (b, g, 0, 0)),
                      pl.BlockSpec((2, seq, LANES), lambda b, g, *_: (0, b, g)),
                      pl.BlockSpec((1, 2, qt, HEAD_DIM + ONES_ROWS, tq), lambda b, g, *_: (b, 0, 0, g, 0)),
                      pl.BlockSpec((1, qt, LANES, tq), lambda b, g, *_: (b, 0, 0, 0)),
                      pl.BlockSpec((seq, width), lambda b, g, *_: (b, dg_blk + g)),
                      pl.BlockSpec(ovt.shape, lambda b, g, *_: (0, 0))],
            out_specs=pl.BlockSpec((seq, width), lambda b, g, *_: (b, g)),
            scratch_shapes=[pltpu.VMEM((qt * 2 * rheads, HEAD_DIM + ONES_ROWS, tq), F32),
                            pltpu.VMEM((qt * rheads, HEAD_DIM, tq), F32),
                            pltpu.VMEM((qt * sel_rows, tq), F32),
                            pltpu.VMEM((NSLC_PAD, tq), F32),
                            pltpu.VMEM((rheads, tq, tq), F32),
                            pltpu.VMEM((rheads, tq, tq), F32),
                            pltpu.VMEM((rheads, ncp, tq), F32),
                            pltpu.VMEM((2 * rheads, tq), F32),
                            pltpu.VMEM((qt * 2 * rheads, tq), F32),
                            pltpu.VMEM((seq, rheads * LANES), BF16)]),
        out_shape=jax.ShapeDtypeStruct((n, groups * width), BF16),
        compiler_params=_cparams(("parallel", "parallel"), 56),
        name="nsa_attn",
    )(*tabs, qe, kc, vct, kboth, vtboth, gt, p1, ovt)


def _rope_tables(pos, d):
    rd = d // 4
    half = rd // 2
    inv = ROPE_THETA ** (-jnp.arange(half, dtype=F32) / half)
    ang = pos.astype(F32)[:, None] * inv[None, :]
    cos, sin = jnp.cos(ang), jnp.sin(ang)
    npos = pos.shape[0]
    zeros = jnp.zeros((npos, d), F32)
    c = jnp.concatenate([cos, cos, jnp.ones((npos, d - rd), F32)], axis=1)
    s1 = zeros.at[:, :half].set(-sin)
    s2 = zeros.at[:, half:rd].set(sin)
    return tuple(jnp.tile(t, (1, LANES // d)) for t in (c, s1, s2))


def _overlap_t(ncp, nslc):
    start = np.arange(ncp) * CMP_STRIDE
    s0 = np.arange(nslc) * SLC_BLOCK
    lo = np.maximum(start[:, None], s0[None, :])
    hi = np.minimum(start[:, None] + CMP_BLOCK, s0[None, :] + SLC_BLOCK)
    ov = np.clip(hi - lo, 0, None) / CMP_BLOCK
    ov[ncp - 1] = 0.0
    out = np.zeros((NSLC_PAD, ncp), np.float32)
    out[:nslc] = ov.T
    return jnp.asarray(out, BF16)


def kernel(x, mem, mem_norm, l0_norm, l0_w_in, l0_a_vnorm, l0_a_ws, l0_a_bs, l0_b_qnorm, l0_b_knorm, l0_b_lq1, l0_b_lk1, l0_b_lq2, l0_b_lk2, l0_b_subln, l0_m_wkv, l0_m_qnorm, l0_m_knorm, l0_w_out, l1_norm, l1_w_in, l1_c_conv_w, l1_c_conv_b, l1_c_norm, l1_d_qnorm, l1_d_knorm, l1_d_cmp_pos_k, l1_d_cmp_w1_k, l1_d_cmp_w2_k, l1_d_cmp_pos_v, l1_d_cmp_w1_v, l1_d_cmp_w2_v, l1_m_wkv, l1_m_qnorm, l1_m_knorm, l1_w_out):
    batch, seq, d_model = x.shape
    n = batch * seq
    tq = 256
    mw = M_HEADS * HEAD_DIM
    pos = jnp.arange(seq, dtype=jnp.int32)
    x0 = x.reshape(n, d_model)

    wkv_both = jnp.concatenate([l0_m_wkv, l1_m_wkv], axis=1).astype(BF16)
    kgains = jnp.stack([jnp.tile(l0_m_knorm, M_HEADS), jnp.tile(l1_m_knorm, M_HEADS)])
    mem_kh, mem_vt = _mem_kv(mem, mem_norm, wkv_both, kgains)

    w = l0_w_in
    tab32 = _rope_tables(pos, B_QK_DIM)
    b_width = 2 * B_HEADS * B_QK_DIM
    a_width = l0_a_vnorm.shape[0]
    bs_exp = jnp.repeat(l0_a_bs.T, a_width // A_GROUPS, axis=1)
    p0, y_a, qn, kn, vt = _layer0_proj(
        x0, l0_norm, w[:, 0:1536].astype(BF16), w[:, 1536:3072].astype(BF16), w[:, 3072:4096].astype(BF16),
        tab32, jnp.tile(l0_b_qnorm, b_width // B_QK_DIM).reshape(1, b_width),
        jnp.tile(l0_b_knorm, b_width // B_QK_DIM).reshape(1, b_width),
        l0_a_ws, bs_exp, l0_a_vnorm.reshape(1, a_width), batch, seq, tq)
    lam_init = 0.8 - 0.6 * math.exp(-0.3 * 1)
    lam_p = jnp.stack([l0_b_lq1, l0_b_lk1, l0_b_lq2, l0_b_lk2])
    y_b = _diff_attn(lam_p, qn, kn, vt, p0, jnp.tile(l0_b_subln, 2).reshape(1, LANES),
                     batch, seq, tq, lam_init, 0)
    y_m = _mem_attn(p0, mem_kh, mem_vt, 0, jnp.tile(l0_m_qnorm, M_HEADS).reshape(1, mw),
                    512 // mw, 768 // mw, seq, 2 * tq)

    w = l1_w_in
    n_bg = 3 * D_HEADS
    w_rest = jnp.concatenate([w[:, 2840:3352], w[:, 2048:2304], w[:, 3352:3864]], axis=1)
    w_d = jnp.concatenate([w[:, 1536:2048], w[:, 2304:2816], w[:, 2816:2840],
                           jnp.zeros((d_model, LANES - n_bg), F32)], axis=1)
    groups, rheads = D_KV_GROUPS, D_HEADS // D_KV_GROUPS
    d_width = D_HEADS * HEAD_DIM
    tab64 = _rope_tables(pos, HEAD_DIM)
    x1, p1, y_c, qe, kboth, vtboth, gt = _layer1_proj(
        x0, y_a, y_b, y_m, l0_w_out.astype(BF16),
        l1_norm, w[:, 0:1536].astype(BF16), w_d.astype(BF16), w_rest.astype(BF16), tab64,
        jnp.tile(l1_d_qnorm, D_HEADS).reshape(1, d_width),
        jnp.tile(l1_d_knorm, LANES // HEAD_DIM).reshape(1, LANES),
        l1_c_conv_w, l1_c_conv_b, l1_c_norm, batch, seq, tq)
    kv_blk = 512 // LANES

    ncp = seq // CMP_STRIDE
    cmp_pos = jnp.arange(ncp, dtype=jnp.int32) * CMP_STRIDE + (CMP_BLOCK - 1)
    hidden = l1_d_cmp_w2_k.shape[0]

    def group_w1(w1):
        w = w1.reshape(CMP_BLOCK, HEAD_DIM, hidden)
        return jnp.stack([jnp.zeros((CMP_BLOCK, LANES, hidden), F32)
                          .at[:, g * HEAD_DIM:(g + 1) * HEAD_DIM, :].set(w) for g in range(groups)])

    w2k = jnp.tile(l1_d_cmp_w2_k, (1, LANES // HEAD_DIM))
    w2v = jnp.zeros((hidden, LANES), F32).at[:, :HEAD_DIM].set(l1_d_cmp_w2_v)
    kc, vct = _compress(
        p1, kv_blk, jnp.tile(l1_d_cmp_pos_k, (1, groups)), jnp.tile(l1_d_cmp_pos_v, (1, groups)),
        group_w1(l1_d_cmp_w1_k).astype(BF16), w2k.astype(BF16),
        group_w1(l1_d_cmp_w1_v).astype(BF16), w2v.astype(BF16),
        jnp.tile(l1_d_knorm, LANES // HEAD_DIM).reshape(1, LANES), _rope_tables(cmp_pos, HEAD_DIM),
        batch, seq)

    nslc = seq // SLC_BLOCK
    y_d = _nsa_attn(qe, kc, vct, kboth, vtboth, gt, p1, _overlap_t(ncp, nslc),
                    batch, seq, tq, 0)
    out = _mem_out(x1, y_c, y_d, p1, mem_kh, mem_vt, 1, jnp.tile(l1_m_qnorm, M_HEADS).reshape(1, mw),
                   768 // mw, 1024 // mw, l1_w_out.astype(BF16), seq, 2 * tq)
    return out.reshape(batch, seq, d_model)
```

```python
import functools
import math

import numpy as np
import jax
import jax.numpy as jnp
from jax import lax
from jax.experimental import pallas as pl
from jax.experimental.pallas import tpu as pltpu

F32 = jnp.float32
BF16 = jnp.bfloat16

EPS = 1e-6
ROPE_THETA = 500000.0
HEAD_DIM = 64
A_CHUNK = 128
A_GROUPS = 4
B_HEADS = 8
B_QK_DIM = 32
C_KERNEL = 31
D_HEADS = 8
D_KV_GROUPS = 2
CMP_BLOCK = 32
CMP_STRIDE = 16
SLC_BLOCK = 64
SLC_TOPK = 16
WINDOW = 512
M_HEADS = 4

LANES = 128
SUBLANES = 8
MXU_DIM = 256
HALO = 32
NSLC_PAD = 128
ONES_ROWS = 16

NT_DIMS = (((1,), (1,)), ((), ()))
LOG2E = math.log2(math.e)


def _cparams(semantics, vmem_mb):
    return pltpu.CompilerParams(dimension_semantics=semantics,
                                vmem_limit_bytes=vmem_mb * 1024 * 1024)


def _split_bf16(x):
    hi = x.astype(BF16)
    lo = (x - hi.astype(F32)).astype(BF16)
    return hi, lo


def _group_mean(x2, gsize):
    t, c = x2.shape
    w = min(c, MXU_DIM)
    r = lax.broadcasted_iota(jnp.int32, (w, w), 0) // gsize
    cc = lax.broadcasted_iota(jnp.int32, (w, w), 1) // gsize
    ones = jnp.where(r == cc, 1.0, 0.0).astype(BF16)
    outs = []
    for s in range(c // w):
        hi, lo = _split_bf16(x2[:, s * w:(s + 1) * w])
        outs.append(jnp.dot(hi, ones, preferred_element_type=F32)
                    + jnp.dot(lo, ones, preferred_element_type=F32))
    out = outs[0] if len(outs) == 1 else jnp.concatenate(outs, axis=1)
    return out * (1.0 / gsize)


def _tile_lanes(t, width):
    rep = width // t.shape[1]
    return t if rep == 1 else jnp.concatenate([t] * rep, axis=1)


def _rope(x, cos_t, s1_t, s2_t, half):
    width = x.shape[1]
    up = pltpu.roll(x, width - half, 1)
    dn = pltpu.roll(x, half, 1)
    return (x * _tile_lanes(cos_t, width) + up * _tile_lanes(s1_t, width)
            + dn * _tile_lanes(s2_t, width))


def _matmul_cols(h, w_ref, lo, hi, out_ref=None):
    r = jnp.dot(h, w_ref[:, lo:hi], preferred_element_type=F32)
    if out_ref is None:
        return r
    out_ref[:, lo:hi] = r


def _rms_rows(x, ng_ref):
    ms = jnp.mean(x * x, axis=-1, keepdims=True)
    return (x * lax.rsqrt(ms + EPS) * ng_ref[...]).astype(BF16)


def _scores_t(k, q, valid, st_ref, idx):
    st = lax.dot_general(k, q, NT_DIMS, preferred_element_type=F32)
    if isinstance(valid, (list, tuple)):
        blk = st.shape[0] // len(valid)
        st = jnp.concatenate([jnp.where(v, st[c * blk:(c + 1) * blk, :], -jnp.inf)
                              for c, v in enumerate(valid)], axis=0)
    elif valid is not None:
        st = jnp.where(valid, st, -jnp.inf)
    st_ref[idx] = st
    return jnp.max(st, axis=0, keepdims=True)


def _accumulate_t(st_ref, idx, cmax, m, acc_ref, aidx, vt, guard_empty):
    m_new = jnp.maximum(m, cmax)
    m_use = jnp.where(m_new == -jnp.inf, 0.0, m_new) if guard_empty else m_new
    alpha = jnp.exp2(m - m_use)
    pt = jnp.exp2((st_ref[idx] - m_use).astype(BF16))
    acc_ref[aidx] = alpha * acc_ref[aidx] + jnp.dot(vt, pt, preferred_element_type=F32)
    return m_new


def _pipelined_sweep(n, score_fn, accum_fn, unroll=4):
    score_fn(0, 0)
    stages = n - 1

    def stage(e, parity):
        score_fn(e + 1, 1 - parity)
        accum_fn(e, parity)

    def body(i, c):
        for k in range(unroll):
            stage(unroll * i + k, k % 2)
        return c

    if isinstance(n, int):
        iters, left = divmod(stages, unroll)
        if iters:
            lax.fori_loop(0, iters, body, 0)
        for k in range(left):
            stage(iters * unroll + k, k % 2)
        accum_fn(n - 1, left % 2)
        return
    lax.fori_loop(0, stages // unroll, body, 0)
    base = stages // unroll * unroll
    left = stages - base
    for k in range(unroll - 1):
        pl.when(left > k)(functools.partial(stage, base + k, k % 2))
    for parity in range(2):
        pl.when(left % 2 == parity)(functools.partial(accum_fn, n - 1, parity))


def _mix_out(ya_ref, yb_ref, ym_ref, w_ref):
    wa = ya_ref.shape[1]
    wb = yb_ref.shape[1]
    acc = jnp.dot(ya_ref[...], w_ref[0:wa, :], preferred_element_type=F32)
    acc += jnp.dot(yb_ref[...], w_ref[wa:wa + wb, :], preferred_element_type=F32)
    acc += jnp.dot(ym_ref[...], w_ref[wa + wb:, :], preferred_element_type=F32)
    return acc


def _sgu_group(u, v, gate, w_ref, bs, vgain, o_ref, cols):
    tm = u.shape[0]
    row = lax.broadcasted_iota(jnp.int32, (A_CHUNK, A_CHUNK), 0)
    col = lax.broadcasted_iota(jnp.int32, (A_CHUNK, A_CHUNK), 1)
    w = jnp.where(col <= row, w_ref, 0.0).astype(BF16)
    gv = jax.nn.gelu(v)
    ms = jnp.mean(gv * gv, axis=-1, keepdims=True)
    vn = (gv * lax.rsqrt(ms + EPS) * vgain).astype(BF16)
    gu = jax.nn.gelu(u) * jax.nn.silu(gate)
    for c in range(tm // A_CHUNK):
        rows = slice(c * A_CHUNK, (c + 1) * A_CHUNK)
        z = jnp.dot(w, vn[rows, :], preferred_element_type=F32) + bs
        o_ref[rows, cols] = (gu[rows, :] * z).astype(BF16)


def _layer0_proj_kernel(x_ref, ng_ref, wa_ref, wqkv_ref, w_ref, c_ref, s1_ref, s2_ref, qg_ref, kg_ref,
                        ws_ref, bs_ref, vg_ref, p_ref, ya_ref, qo_ref, ko_ref, vto_ref):
    h = _rms_rows(x_ref[...], ng_ref)
    aw = vg_ref.shape[1]
    gdim = aw // A_GROUPS
    width = qg_ref.shape[1]
    half = B_QK_DIM // 4 // 2
    pw = p_ref.shape[1]

    a = _matmul_cols(h, wa_ref, 0, 3 * aw)

    def sgu(g):
        cols = slice(g * gdim, (g + 1) * gdim)
        _sgu_group(a[:, cols], a[:, aw + g * gdim:aw + (g + 1) * gdim],
                   a[:, 2 * aw + g * gdim:2 * aw + (g + 1) * gdim],
                   ws_ref[g], bs_ref[:, cols], vg_ref[:, cols], ya_ref, cols)

    def prep(t, gain, dst, mul):
        tn = t * lax.rsqrt(_group_mean(t * t, B_QK_DIM) + EPS) * gain[...]
        tr = _rope(tn, c_ref[...], s1_ref[...], s2_ref[...], half)
        dst[...] = (tr * mul if mul != 1.0 else tr).astype(BF16)

    q = _matmul_cols(h, wqkv_ref, 0, width)
    sgu(0)
    k = _matmul_cols(h, wqkv_ref, width, 2 * width)
    sgu(1)
    v = _matmul_cols(h, wqkv_ref, 2 * width, 3 * width)
    sgu(2)
    _matmul_cols(h, w_ref, 0, pw // 2, p_ref)
    sgu(3)
    prep(q, qg_ref, qo_ref, B_QK_DIM ** -0.5 * LOG2E)
    _matmul_cols(h, w_ref, pw // 2, pw, p_ref)
    prep(k, kg_ref, ko_ref, 1.0)
    vt = v.T
    dv = 2 * B_QK_DIM
    ones = jnp.ones((ONES_ROWS, vt.shape[1]), F32)
    parts = []
    for hd in range(B_HEADS):
        parts += [vt[hd * dv:(hd + 1) * dv, :], ones]
    vt_all = jnp.concatenate(parts, axis=0).astype(BF16)
    tk = vto_ref.shape[3]
    for c in range(vto_ref.shape[1]):
        vto_ref[0, c] = vt_all[:, c * tk:(c + 1) * tk]


def _layer0_proj(x, norm_g, w_a, w_qkv, w_rest, tables, qgain, kgain, a_ws, bs_exp, vgain, batch, seq, tm, tk):
    n, d = x.shape
    width = qgain.shape[1]
    aw = vgain.shape[1]
    tiles_per_seq = seq // tm
    tab_spec = pl.BlockSpec((tm, LANES), lambda i: (i % tiles_per_seq, 0))
    vrows = B_HEADS * (2 * B_QK_DIM + ONES_ROWS)

    def full(a):
        return pl.BlockSpec(a.shape, lambda i: (0,) * a.ndim)

    def rows(w):
        return pl.BlockSpec((tm, w), lambda i: (i, 0))

    return pl.pallas_call(
        _layer0_proj_kernel,
        grid=(n // tm,),
        in_specs=[rows(d), pl.BlockSpec((1, d), lambda i: (0, 0)), full(w_a), full(w_qkv), full(w_rest),
                  tab_spec, tab_spec, tab_spec, full(qgain), full(kgain),
                  full(a_ws), full(bs_exp), full(vgain)],
        out_specs=[rows(w_rest.shape[1]), rows(aw), rows(width), rows(width),
                   pl.BlockSpec((1, tm // tk, vrows, tk),
                                lambda i: (i // tiles_per_seq, i % tiles_per_seq, 0, 0))],
        out_shape=[jax.ShapeDtypeStruct((n, w_rest.shape[1]), F32),
                   jax.ShapeDtypeStruct((n, aw), BF16),
                   jax.ShapeDtypeStruct((n, width), BF16), jax.ShapeDtypeStruct((n, width), BF16),
                   jax.ShapeDtypeStruct((batch, seq // tk, vrows, tk), BF16)],
        compiler_params=_cparams(("parallel",), 52),
        name="layer0_proj",
    )(x, norm_g.reshape(1, d), w_a, w_qkv, w_rest, *tables, qgain, kgain, a_ws, bs_exp, vgain)


def _diff_attn_kernel(qi_tab, tile_tab, lam_ref, q_ref, k_ref, vt_ref, g_ref, sub_ref, o_ref,
                      acc_ref, sta_ref, stb_ref, cmax_ref, m_ref, *, tq, qt, lam_init):
    dv = 2 * B_QK_DIM
    vrows = dv + ONES_ROWS
    nmap = 4
    lane = lax.broadcasted_iota(jnp.int32, (1, LANES), 1)
    bufs = (sta_ref, stb_ref)
    acc_ref[...] = jnp.zeros_like(acc_ref)
    m_ref[...] = jnp.full(m_ref.shape, -jnp.inf, F32)
    krow = lax.broadcasted_iota(jnp.int32, (tq, tq), 0)
    qcol = lax.broadcasted_iota(jnp.int32, (tq, tq), 1)
    lam_p = lam_ref[...]
    lam = (jnp.exp(jnp.sum(lam_p[0:1] * lam_p[1:2], axis=-1, keepdims=True))
           - jnp.exp(jnp.sum(lam_p[2:3] * lam_p[3:4], axis=-1, keepdims=True)) + lam_init)

    def rows(tile):
        return pl.ds(pl.multiple_of(tile * tq, tq), tq)

    def scores(qi, tile, buf, valid):
        q = q_ref[rows(qi), :]
        k = k_ref[rows(tile), :]
        zero = jnp.zeros_like(q)
        for j in range(nmap):
            qm = jnp.where(lane // B_QK_DIM == j, q, zero)
            row = buf * nmap + j
            cmax_ref[row:row + 1, :] = _scores_t(k, qm, valid, bufs[buf], j)

    def accumulate(qi, tile, buf):
        for j in range(nmap):
            vt = vt_ref[0, tile, (j // 2) * vrows:(j // 2 + 1) * vrows, :]
            cm, sr = buf * nmap + j, qi * nmap + j
            m_ref[pl.ds(sr, 1), :] = _accumulate_t(bufs[buf], j, cmax_ref[cm:cm + 1, :],
                                                   m_ref[pl.ds(sr, 1), :], acc_ref, sr, vt, False)

    def finish(qi):
        halves = []
        for h in range(2):
            o = [acc_ref[qi * nmap + 2 * h + mp, 0:dv, :] / acc_ref[qi * nmap + 2 * h + mp, dv:dv + 1, :]
                 for mp in range(2)]
            pd = o[0] - lam * o[1]
            ms2 = jnp.mean(pd * pd, axis=0, keepdims=True)
            halves.append(pd * lax.rsqrt(ms2 + EPS))
        ob = jnp.concatenate(halves, axis=0).T * sub_ref[...] * (1.0 - lam_init)
        o_ref[rows(qi), :] = (ob * jax.nn.silu(g_ref[rows(qi), :])).astype(BF16)

    def below_score(e, buf):
        scores(qi_tab[e], tile_tab[e], buf, None)

    def below_accum(e, buf):
        accumulate(qi_tab[e], tile_tab[e], buf)

    def diag_score(e, buf):
        scores(e, e, buf, krow <= qcol)

    def diag_accum(e, buf):
        accumulate(e, e, buf)
        finish(e)

    _pipelined_sweep(qt * (qt - 1) // 2, below_score, below_accum, unroll=16)
    _pipelined_sweep(qt, diag_score, diag_accum)


def _diff_attn(lam_p, qn, kn, vt, p0, subln, batch, seq, tq, lam_init, g_off):
    n, width = qn.shape
    pairs = width // LANES
    qt = seq // tq
    vrows = 2 * B_QK_DIM + ONES_ROWS
    below = [(qi, t) for qi in range(qt) for t in range(qi)]
    qi_tab = jnp.asarray([e[0] for e in below], jnp.int32)
    tile_tab = jnp.asarray([e[1] for e in below], jnp.int32)
    kernel = functools.partial(_diff_attn_kernel, tq=tq, qt=qt, lam_init=lam_init)
    seq_spec = pl.BlockSpec((seq, LANES), lambda b, p, *_: (b, p))
    return pl.pallas_call(
        kernel,
        grid_spec=pltpu.PrefetchScalarGridSpec(
            num_scalar_prefetch=2,
            grid=(batch, pairs),
            in_specs=[pl.BlockSpec(lam_p.shape, lambda b, p, *_: (0, 0)),
                      seq_spec, seq_spec,
                      pl.BlockSpec((1, qt, 2 * vrows, tq), lambda b, p, *_: (b, 0, p, 0)),
                      pl.BlockSpec((seq, LANES), lambda b, p, *_: (b, g_off + p)),
                      pl.BlockSpec((1, LANES), lambda b, p, *_: (0, 0))],
            out_specs=seq_spec,
            scratch_shapes=[pltpu.VMEM((qt * 4, vrows, tq), F32), pltpu.VMEM((4, tq, tq), F32),
                            pltpu.VMEM((4, tq, tq), F32), pltpu.VMEM((8, tq), F32),
                            pltpu.VMEM((qt * 4, tq), F32)]),
        out_shape=jax.ShapeDtypeStruct((n, width), BF16),
        compiler_params=_cparams(("parallel", "parallel"), 40),
        name="diff_attn",
    )(qi_tab, tile_tab, lam_p, qn, kn, vt, p0, subln)


def _mem_kv_kernel(mem_ref, g_ref, w_ref, kg_ref, kh_ref, vt_ref):
    x = mem_ref[0]
    ms = jnp.mean(x * x, axis=-1, keepdims=True)
    h = (x * lax.rsqrt(ms + EPS) * g_ref[...]).astype(BF16)
    kv = jnp.dot(h, w_ref[...], preferred_element_type=F32)
    mw = kv.shape[1] // 4
    lane = lax.broadcasted_iota(jnp.int32, (1, mw), 1)
    ones = jnp.ones((ONES_ROWS, kv.shape[0]), F32)
    for layer in range(2):
        k = kv[:, (2 * layer) * mw:(2 * layer + 1) * mw]
        kn = k * lax.rsqrt(_group_mean(k * k, HEAD_DIM) + EPS) * kg_ref[layer:layer + 1, :]
        vt = kv[:, (2 * layer + 1) * mw:(2 * layer + 2) * mw].T
        parts = []
        for hd in range(M_HEADS):
            kh_ref[layer, 0, hd] = jnp.where(lane // HEAD_DIM == hd, kn, 0.0).astype(BF16)
            parts += [vt[hd * HEAD_DIM:(hd + 1) * HEAD_DIM, :], ones]
        vt_ref[layer, 0] = jnp.concatenate(parts, axis=0).astype(BF16)


def _mem_kv(mem, mem_norm, wkv_both, kgains):
    batch, mtok, d = mem.shape
    mw = wkv_both.shape[1] // 4
    vrows = M_HEADS * (HEAD_DIM + ONES_ROWS)
    return pl.pallas_call(
        _mem_kv_kernel,
        grid=(batch,),
        in_specs=[pl.BlockSpec((1, mtok, d), lambda b: (b, 0, 0)),
                  pl.BlockSpec((1, d), lambda b: (0, 0)),
                  pl.BlockSpec(wkv_both.shape, lambda b: (0, 0)),
                  pl.BlockSpec(kgains.shape, lambda b: (0, 0))],
        out_specs=[pl.BlockSpec((2, 1, M_HEADS, mtok, mw), lambda b: (0, b, 0, 0, 0)),
                   pl.BlockSpec((2, 1, vrows, mtok), lambda b: (0, b, 0, 0))],
        out_shape=[jax.ShapeDtypeStruct((2, batch, M_HEADS, mtok, mw), BF16),
                   jax.ShapeDtypeStruct((2, batch, vrows, mtok), BF16)],
        compiler_params=_cparams(("parallel",), 32),
        name="mem_kv",
    )(mem, mem_norm.reshape(1, d), wkv_both, kgains)


def _mem_attn_tile(q_ref, g_ref, kh_ref, vt_ref, qg_ref, st_ref):
    vrows = HEAD_DIM + ONES_ROWS
    x = q_ref[...]
    qn = (x * lax.rsqrt(_group_mean(x * x, HEAD_DIM) + EPS) * qg_ref[...]
          * (HEAD_DIM ** -0.5 * LOG2E)).astype(BF16)
    cmax = [_scores_t(kh_ref[0, 0, h], qn, None, st_ref, h) for h in range(M_HEADS)]
    outs = []
    for h in range(M_HEADS):
        pt = jnp.exp2((st_ref[h] - cmax[h]).astype(BF16))
        acc = jnp.dot(vt_ref[0, 0, h * vrows:(h + 1) * vrows, :], pt, preferred_element_type=F32)
        outs.append(acc[0:HEAD_DIM, :] / acc[HEAD_DIM:HEAD_DIM + 1, :])
    out = jnp.concatenate(outs, axis=0).T
    return (out * jax.nn.silu(g_ref[...])).astype(BF16)


def _mem_attn_kernel(q_ref, g_ref, kh_ref, vt_ref, qg_ref, o_ref, st_ref):
    o_ref[...] = _mem_attn_tile(q_ref, g_ref, kh_ref, vt_ref, qg_ref, st_ref)


def _mem_out_kernel(x_ref, ya_ref, yb_ref, q_ref, g_ref, kh_ref, vt_ref, qg_ref, w_ref, o_ref, st_ref):
    ym = _mem_attn_tile(q_ref, g_ref, kh_ref, vt_ref, qg_ref, st_ref)
    wa = ya_ref.shape[1]
    wb = yb_ref.shape[1]
    acc = jnp.dot(ya_ref[...], w_ref[0:wa, :], preferred_element_type=F32)
    acc += jnp.dot(yb_ref[...], w_ref[wa:wa + wb, :], preferred_element_type=F32)
    acc += jnp.dot(ym, w_ref[wa + wb:, :], preferred_element_type=F32)
    o_ref[...] = x_ref[...] + acc


def _mem_out(x, ya, yb, p, kh, vt, layer, qgain, q_blk, g_blk, w, seq, tq):
    n, d = x.shape
    _, batch, heads, mtok, mw = kh.shape
    qt = seq // tq

    def rows(width, blk=0):
        return pl.BlockSpec((tq, width), lambda i: (i, blk))

    return pl.pallas_call(
        _mem_out_kernel,
        grid=(n // tq,),
        in_specs=[rows(d), rows(ya.shape[1]), rows(yb.shape[1]), rows(mw, q_blk), rows(mw, g_blk),
                  pl.BlockSpec((1, 1, heads, mtok, mw), lambda i: (layer, i // qt, 0, 0, 0)),
                  pl.BlockSpec((1, 1, vt.shape[2], mtok), lambda i: (layer, i // qt, 0, 0)),
                  pl.BlockSpec((1, mw), lambda i: (0, 0)),
                  pl.BlockSpec(w.shape, lambda i: (0, 0))],
        out_specs=rows(d),
        out_shape=jax.ShapeDtypeStruct((n, d), F32),
        scratch_shapes=[pltpu.VMEM((heads, mtok, tq), F32)],
        compiler_params=_cparams(("parallel",), 40),
        name="mem_out_proj",
    )(x, ya, yb, p, p, kh, vt, qgain, w)


def _mem_attn(p, kh, vt, layer, qgain, q_blk, g_blk, seq, tq):
    n = p.shape[0]
    _, batch, heads, mtok, mw = kh.shape
    qt = seq // tq
    return pl.pallas_call(
        _mem_attn_kernel,
        grid=(n // tq,),
        in_specs=[pl.BlockSpec((tq, mw), lambda i: (i, q_blk)),
                  pl.BlockSpec((tq, mw), lambda i: (i, g_blk)),
                  pl.BlockSpec((1, 1, heads, mtok, mw), lambda i: (layer, i // qt, 0, 0, 0)),
                  pl.BlockSpec((1, 1, vt.shape[2], mtok), lambda i: (layer, i // qt, 0, 0)),
                  pl.BlockSpec((1, mw), lambda i: (0, 0))],
        out_specs=pl.BlockSpec((tq, mw), lambda i: (i, 0)),
        out_shape=jax.ShapeDtypeStruct((n, mw), BF16),
        scratch_shapes=[pltpu.VMEM((heads, mtok, tq), F32)],
        compiler_params=_cparams(("parallel",), 32),
        name="mem_attn",
    )(p, p, kh, vt, qgain)


def _layer1_proj_kernel(x_ref, ya_ref, yb_ref, ym_ref, wo_ref, ng_ref, wc_ref, wd_ref, w_ref,
                        c_ref, s1_ref, s2_ref, qg_ref, kg_ref, cw_ref, cb_ref, cn_ref,
                        x1_ref, p_ref, yc_ref, qo_ref, ko_ref, vto_ref, gto_ref,
                        hbuf_ref, shift_ref, *, tiles_per_seq):
    i = pl.program_id(0)
    ts = x_ref.shape[0]
    span = ts + HALO
    cwid = cn_ref.shape[1]
    qw = qg_ref.shape[1]
    pw = p_ref.shape[1]
    half = HEAD_DIM // 4 // 2
    x1 = x_ref[...] + _mix_out(ya_ref, yb_ref, ym_ref, wo_ref)
    x1_ref[...] = x1
    h = _rms_rows(x1, ng_ref)

    @pl.when(i % tiles_per_seq == 0)
    def _():
        hbuf_ref[0:HALO, :] = jnp.zeros((HALO, cwid), F32)

    ca = _matmul_cols(h, wc_ref, 0, cwid)
    cbv = _matmul_cols(h, wc_ref, cwid, 2 * cwid)
    q = _matmul_cols(h, wd_ref, 0, qw)
    hbuf_ref[HALO:HALO + ts, :] = ca * jax.nn.sigmoid(cbv)
    for ph in range(SUBLANES):
        shift_ref[ph, 0:span - ph, :] = hbuf_ref[ph:span, :]
    cg = _matmul_cols(h, wc_ref, 2 * cwid, 3 * cwid)
    d = _matmul_cols(h, wd_ref, qw, wd_ref.shape[1])
    first = HALO - (C_KERNEL - 1)
    y = cb_ref[...]
    for j in range(C_KERNEL):
        ph, base = (first + j) % SUBLANES, (first + j) // SUBLANES * SUBLANES
        y = y + shift_ref[ph, base:base + ts, :] * cw_ref[j:j + 1, :]
    psplit = -(-pw // (2 * MXU_DIM)) * MXU_DIM
    _matmul_cols(h, w_ref, 0, psplit, p_ref)
    ms = jnp.mean(y * y, axis=-1, keepdims=True)
    yn = y * lax.rsqrt(ms + EPS) * cn_ref[...]
    yc_ref[...] = (jax.nn.silu(yn) * jax.nn.silu(cg)).astype(BF16)
    hbuf_ref[0:HALO, :] = hbuf_ref[ts:ts + HALO, :]
    _matmul_cols(h, w_ref, psplit, pw, p_ref)

    tabs = (c_ref[...], s1_ref[...], s2_ref[...])
    tn = q * lax.rsqrt(_group_mean(q * q, HEAD_DIM) + EPS) * qg_ref[...]
    qo_ref[...] = (_rope(tn, *tabs, half) * (HEAD_DIM ** -0.5 * LOG2E)).astype(BF16)
    ones = jnp.ones((ONES_ROWS, ts), F32)
    lane = lax.broadcasted_iota(jnp.int32, (1, LANES), 1)
    for br in range(2):
        k = d[:, 2 * br * LANES:(2 * br + 1) * LANES]
        kn = k * lax.rsqrt(_group_mean(k * k, HEAD_DIM) + EPS) * kg_ref[...]
        kr = _rope(kn, *tabs, half)
        swapped = pltpu.roll(kr, HEAD_DIM, 1)
        for g in range(D_KV_GROUPS):
            own = (lane // HEAD_DIM) == g
            ko_ref[br, :, g * LANES:(g + 1) * LANES] = jnp.where(own, kr, swapped).astype(BF16)
        vt = d[:, (2 * br + 1) * LANES:(2 * br + 2) * LANES].T
        parts = []
        for g in range(D_KV_GROUPS):
            parts += [vt[g * HEAD_DIM:(g + 1) * HEAD_DIM, :], ones]
        vto_ref[0, br, 0] = jnp.concatenate(parts, axis=0).astype(BF16)
    gto_ref[0, 0] = jax.nn.sigmoid(d[:, 4 * LANES:5 * LANES]).T


def _layer1_proj(x, ya, yb, ym, w_out, norm_g, w_c, w_d, w_rest, tables, qgain, kgain,
                 conv_w, conv_b, conv_norm, batch, seq, tm):
    n, dm = x.shape
    cwid = conv_w.shape[1]
    tiles_per_seq = seq // tm
    tab_spec = pl.BlockSpec((tm, LANES), lambda i: (i % tiles_per_seq, 0))
    vrows = D_KV_GROUPS * (HEAD_DIM + ONES_ROWS)

    def full(a):
        return pl.BlockSpec(a.shape, lambda i: (0,) * a.ndim)

    def rows(w):
        return pl.BlockSpec((tm, w), lambda i: (i, 0))

    kernel = functools.partial(_layer1_proj_kernel, tiles_per_seq=tiles_per_seq)
    return pl.pallas_call(
        kernel,
        grid=(n // tm,),
        in_specs=[rows(dm), rows(ya.shape[1]), rows(yb.shape[1]), rows(ym.shape[1]), full(w_out),
                  pl.BlockSpec((1, dm), lambda i: (0, 0)), full(w_c), full(w_d), full(w_rest),
                  tab_spec, tab_spec, tab_spec, full(qgain), full(kgain),
                  full(conv_w), pl.BlockSpec((1, cwid), lambda i: (0, 0)),
                  pl.BlockSpec((1, cwid), lambda i: (0, 0))],
        out_specs=[rows(dm), rows(w_rest.shape[1]), rows(cwid), rows(qgain.shape[1]),
                   pl.BlockSpec((2, tm, D_KV_GROUPS * LANES), lambda i: (0, i, 0)),
                   pl.BlockSpec((1, 2, 1, vrows, tm),
                                lambda i: (i // tiles_per_seq, 0, i % tiles_per_seq, 0, 0)),
                   pl.BlockSpec((1, 1, LANES, tm),
                                lambda i: (i // tiles_per_seq, i % tiles_per_seq, 0, 0))],
        out_shape=[jax.ShapeDtypeStruct((n, dm), F32),
                   jax.ShapeDtypeStruct((n, w_rest.shape[1]), F32),
                   jax.ShapeDtypeStruct((n, cwid), BF16),
                   jax.ShapeDtypeStruct((n, qgain.shape[1]), BF16),
                   jax.ShapeDtypeStruct((2, n, D_KV_GROUPS * LANES), BF16),
                   jax.ShapeDtypeStruct((batch, 2, tiles_per_seq, vrows, tm), BF16),
                   jax.ShapeDtypeStruct((batch, tiles_per_seq, LANES, tm), F32)],
        scratch_shapes=[pltpu.VMEM((tm + HALO, cwid), F32),
                        pltpu.VMEM((SUBLANES, tm + HALO, cwid), F32)],
        compiler_params=_cparams(("arbitrary",), 52),
        name="layer1_proj",
    )(x, ya, yb, ym, w_out, norm_g.reshape(1, dm), w_c, w_d, w_rest, *tables, qgain, kgain, conv_w,
      conv_b.reshape(1, cwid), conv_norm.reshape(1, cwid))


def _compress_kernel(tk_ref, tv_ref, pk_ref, pv_ref, w1k_ref, w2k_ref, w1v_ref, w2v_ref,
                     kg_ref, c_ref, s1_ref, s2_ref, kc_ref, vct_ref):
    half = HEAD_DIM // 4 // 2
    ncp = kc_ref.shape[2]

    def mlp(t_ref, p_ref, w1_ref, w2):
        a = jnp.zeros((ncp, w1_ref.shape[3]), F32)
        b = jnp.zeros((ncp, w1_ref.shape[3]), F32)
        for l in range(CMP_STRIDE):
            x = t_ref[pl.ds(l, ncp, stride=CMP_STRIDE), :]
            a += jnp.dot((x + p_ref[l:l + 1, :]).astype(BF16), w1_ref[0, l],
                         preferred_element_type=F32)
            b += jnp.dot((x + p_ref[CMP_STRIDE + l:CMP_STRIDE + l + 1, :]).astype(BF16),
                         w1_ref[0, CMP_STRIDE + l], preferred_element_type=F32)
        h = a + pltpu.roll(b, ncp - 1, 0)
        return jnp.dot(jax.nn.silu(h).astype(BF16), w2, preferred_element_type=F32)

    kc = mlp(tk_ref, pk_ref, w1k_ref, w2k_ref[...])
    kn = kc * lax.rsqrt(_group_mean(kc * kc, HEAD_DIM) + EPS) * kg_ref[...]
    kc_ref[0, 0] = _rope(kn, c_ref[...], s1_ref[...], s2_ref[...], half).astype(BF16)
    vc = mlp(tv_ref, pv_ref, w1v_ref, w2v_ref[...])
    vct_ref[0, 0] = vc.T[0:HEAD_DIM, :].astype(BF16)


def _compress(p1, k_blk, pk, pv, w1k, w2k, w1v, w2v, kgain, tables, batch, seq):
    groups = w1k.shape[0]
    ncp = seq // CMP_STRIDE

    def full(a):
        return pl.BlockSpec(a.shape, lambda b, g: (0,) * a.ndim)

    def per_group(a):
        return pl.BlockSpec((1,) + a.shape[1:], lambda b, g: (g,) + (0,) * (a.ndim - 1))

    return pl.pallas_call(
        _compress_kernel,
        grid=(batch, groups),
        in_specs=[pl.BlockSpec((seq, LANES), lambda b, g: (b, k_blk)),
                  pl.BlockSpec((seq, LANES), lambda b, g: (b, k_blk + 1)),
                  full(pk), full(pv), per_group(w1k), full(w2k), per_group(w1v), full(w2v),
                  full(kgain), full(tables[0]), full(tables[1]), full(tables[2])],
        out_specs=[pl.BlockSpec((1, 1, ncp, LANES), lambda b, g: (b, g, 0, 0)),
                   pl.BlockSpec((1, 1, HEAD_DIM, ncp), lambda b, g: (b, g, 0, 0))],
        out_shape=[jax.ShapeDtypeStruct((batch, groups, ncp, LANES), BF16),
                   jax.ShapeDtypeStruct((batch, groups, HEAD_DIM, ncp), BF16)],
        compiler_params=_cparams(("parallel", "parallel"), 32),
        name="nsa_compress",
    )(p1, p1, pk, pv, w1k, w2k, w1v, w2v, kgain, *tables)


def _nsa_kernel(qi_tab, tile_tab, br_tab, q_ref, kc_ref, vct_ref, k_ref, vt_ref, gt_ref,
                dg_ref, ovt_ref, o_ref, acc_ref, oc_ref, sel_ref, imp_ref, sta_ref, stb_ref, cst_ref,
                cmax_ref, m_ref, qh_ref, *, tq, qt, nslc, nsel, nbelow, nentries):
    g = pl.program_id(1)
    rheads = D_HEADS // D_KV_GROUPS
    seq = qt * tq
    tiny = float(np.finfo(np.float32).tiny)
    qcol = lax.broadcasted_iota(jnp.int32, (tq, tq), 1)
    krow = lax.broadcasted_iota(jnp.int32, (tq, tq), 0)
    bufs = (sta_ref, stb_ref)
    blocks_per_tile = tq // SLC_BLOCK
    nslab = -(-nslc // SUBLANES)
    sel_rows = nslab * SUBLANES

    def rows(tile):
        return pl.ds(tile * tq if isinstance(tile, int) else pl.multiple_of(tile * tq, tq), tq)

    half_lane = lax.broadcasted_iota(jnp.int32, (1, LANES), 1) // HEAD_DIM

    def split_heads(t, c):
        for r in range(rheads):
            blk = q_ref[rows(t), (r // 2) * LANES:(r // 2 + 1) * LANES]
            qh_ref[rows(t), r * LANES:(r + 1) * LANES] = jnp.where(half_lane == r % 2, blk,
                                                                  jnp.zeros_like(blk))
        return c

    lax.fori_loop(0, qt, split_heads, 0)

    def heads_q(qi):
        return [qh_ref[rows(qi), r * LANES:(r + 1) * LANES] for r in range(rheads)]

    kc = kc_ref[0, 0]
    vct = vct_ref[0, 0]
    ncp = kc.shape[0]
    ovt = ovt_ref[...]

    def select_blocks(qi):
        q = heads_q(qi)
        ncq = min(ncp, -(-(((qi + 1) * tq - CMP_BLOCK) // CMP_STRIDE + 1) // LANES) * LANES)
        cend = lax.broadcasted_iota(jnp.int32, (ncq, tq), 0) * CMP_STRIDE + (CMP_BLOCK - 1)
        cvalid = cend <= qi * tq + lax.broadcasted_iota(jnp.int32, (ncq, tq), 1)
        psum = jnp.zeros((ncq, tq), F32)
        cmax = []
        for r in range(rheads):
            st = lax.dot_general(kc[0:ncq, :], q[r], NT_DIMS, preferred_element_type=F32)
            st = jnp.where(cvalid, st, -jnp.inf)
            cst_ref[r, 0:ncq, :] = st
            cmax.append(jnp.max(st, axis=0, keepdims=True))
        for r in range(rheads):
            m = jnp.where(jnp.isfinite(cmax[r]), cmax[r], 0.0)
            e = jnp.exp2(cst_ref[r, 0:ncq, :] - m)
            p = e / jnp.maximum(jnp.sum(e, axis=0, keepdims=True), tiny)
            psum += p
            oc_ref[qi * rheads + r] = jnp.dot(vct[:, 0:ncq], p.astype(BF16), preferred_element_type=F32)
        hi, lo = _split_bf16(psum)
        imp = (jnp.dot(ovt[:, 0:ncq], hi, preferred_element_type=F32)
               + jnp.dot(ovt[:, 0:ncq], lo, preferred_element_type=F32))
        blk = lax.broadcasted_iota(jnp.int32, (NSLC_PAD, tq), 0)
        cur = (qi * tq + lax.broadcasted_iota(jnp.int32, (NSLC_PAD, tq), 1)) // SLC_BLOCK
        imp = jnp.where(blk > cur, -jnp.inf, imp)
        forced = (blk == 0) | (blk == cur) | (blk == cur - 1)
        imp = jnp.where(forced, jnp.inf, imp)
        imp_ref[...] = imp
        nvis = min(nslc, (qi + 1) * blocks_per_tile)
        vslab = -(-nvis // SUBLANES)
        slabs = [imp[s * SUBLANES:(s + 1) * SUBLANES, :] for s in range(vslab)]
        ranks = [jnp.zeros((SUBLANES, tq), F32) for _ in range(vslab)]
        sub = lax.broadcasted_iota(jnp.int32, (SUBLANES, tq), 0)
        for jp in range(nvis):
            row = imp_ref[jp:jp + 1, :]
            for s in range(vslab):
                if s * SUBLANES > jp:
                    ahead = jnp.where(row >= slabs[s], 1.0, 0.0)
                elif (s + 1) * SUBLANES - 1 < jp:
                    ahead = jnp.where(row > slabs[s], 1.0, 0.0)
                else:
                    ahead = jnp.where(sub > jp - s * SUBLANES, jnp.where(row >= slabs[s], 1.0, 0.0),
                                      jnp.where(row > slabs[s], 1.0, 0.0))
                ranks[s] = ranks[s] + ahead
        for s in range(vslab):
            start = qi * sel_rows + s * SUBLANES
            sel_ref[start:start + SUBLANES, :] = jnp.where(ranks[s] < nsel, 1.0, 0.0)

    for qi in range(qt):
        select_blocks(qi)
    acc_ref[...] = jnp.zeros_like(acc_ref)
    m_ref[...] = jnp.full(m_ref.shape, -jnp.inf, F32)

    def chosen_rows(qi, tile):
        base = qi * sel_rows + tile * blocks_per_tile
        return [sel_ref[pl.ds(base + c, 1), :] for c in range(blocks_per_tile)]

    def scores(qi, tile, br, buf, valid):
        q = heads_q(qi)
        k = k_ref[br, rows(tile), :]
        for r in range(rheads):
            row = buf * rheads + r
            cmax_ref[row:row + 1, :] = _scores_t(k, q[r], valid, bufs[buf], r)

    def below_score_fn(e, buf):
        qi, tile = qi_tab[e], tile_tab[e]
        scores(qi, tile, 0, buf, [c > 0.5 for c in chosen_rows(qi, tile)])

    def edge_score_fn(e, buf):
        qi, tile, br = qi_tab[e], tile_tab[e], br_tab[e]
        chosen = jnp.concatenate([jnp.broadcast_to(c, (SLC_BLOCK, tq)) for c in chosen_rows(qi, tile)],
                                 axis=0)
        dpos = (qi - tile) * tq + (qcol - krow)
        reach = jnp.where(br == 0, seq, WINDOW).astype(jnp.uint32)
        kept = jnp.where(jnp.maximum(chosen, br.astype(F32)) > 0.5, dpos, -1)
        scores(qi, tile, br, buf, kept.astype(jnp.uint32) < reach)

    def accum_fn(e, buf):
        qi, tile, br = qi_tab[e], tile_tab[e], br_tab[e]
        vt = vt_ref[0, br, tile]
        for r in range(rheads):
            cm, sr = buf * rheads + r, (qi * 2 + br) * rheads + r
            m_ref[pl.ds(sr, 1), :] = _accumulate_t(bufs[buf], r, cmax_ref[cm:cm + 1, :],
                                                   m_ref[pl.ds(sr, 1), :], acc_ref, sr, vt, True)

    _pipelined_sweep(nbelow, below_score_fn, accum_fn, unroll=16)
    _pipelined_sweep(nentries - nbelow, lambda e, buf: edge_score_fn(e + nbelow, buf),
                     lambda e, buf: accum_fn(e + nbelow, buf), unroll=8)

    def finish(qi, c):
        outs = []
        for r in range(rheads):
            row0 = (g * rheads + r) * 3
            o = [oc_ref[qi * rheads + r]]
            for br in range(2):
                sr = (qi * 2 + br) * rheads + r
                o.append(acc_ref[sr, 0:HEAD_DIM, :]
                         / jnp.maximum(acc_ref[sr, HEAD_DIM:HEAD_DIM + 1, :], tiny))
            outs.append(sum(gt_ref[0, qi, pl.ds(row0 + j, 1), :] * o[j] for j in range(3)))
        out = jnp.concatenate(outs, axis=0).T
        o_ref[rows(qi), :] = (out * jax.nn.silu(dg_ref[rows(qi), :])).astype(BF16)
        return c

    lax.fori_loop(0, qt, finish, 0)


def _nsa_attn(qe, kc, vct, kboth, vtboth, gt, p1, ovt, batch, seq, tq, dg_blk):
    n = qe.shape[0]
    groups = D_KV_GROUPS
    rheads = D_HEADS // groups
    width = rheads * HEAD_DIM
    qt = seq // tq
    ncp = kc.shape[2]
    nslc = seq // SLC_BLOCK
    sel_rows = -(-nslc // SUBLANES) * SUBLANES
    wtiles = (WINDOW + tq - 1) // tq
    below = [(qi, t, 0) for qi in range(qt) for t in range(qi)]
    edge = []
    for qi in range(qt):
        edge += [(qi, qi, 0)] + [(qi, t, 1) for t in range(max(qi - wtiles, 0), qi + 1)]
    entries = below + edge
    tabs = [jnp.asarray([e[c] for e in entries], jnp.int32) for c in range(3)]
    kernel = functools.partial(_nsa_kernel, tq=tq, qt=qt, nslc=nslc, nsel=min(SLC_TOPK, nslc),
                               nbelow=len(below), nentries=len(entries))
    return pl.pallas_call(
        kernel,
        grid_spec=pltpu.PrefetchScalarGridSpec(
            num_scalar_prefetch=3,
            grid=(batch, groups),
            in_specs=[pl.BlockSpec((seq, width), lambda b, g, *_: (b, g)),
                      pl.BlockSpec((1, 1, ncp, LANES), lambda b, g, *_: (b, g, 0, 0)),
                      pl.BlockSpec((1, 1, HEAD_DIM, ncp), lambda b, g, *_: (b, g, 0, 0)),
                      pl.BlockSpec((2, seq, LANES), lambda b, g, *_: (0, b, g)),
                      pl.BlockSpec((1, 2, qt, HEAD_DIM + ONES_ROWS, tq), lambda b, g, *_: (b, 0, 0, g, 0)),
                      pl.BlockSpec((1, qt, LANES, tq), lambda b, g, *_: (b, 0, 0, 0)),
                      pl.BlockSpec((seq, width), lambda b, g, *_: (b, dg_blk + g)),
                      pl.BlockSpec(ovt.shape, lambda b, g, *_: (0, 0))],
            out_specs=pl.BlockSpec((seq, width), lambda b, g, *_: (b, g)),
            scratch_shapes=[pltpu.VMEM((qt * 2 * rheads, HEAD_DIM + ONES_ROWS, tq), F32),
                            pltpu.VMEM((qt * rheads, HEAD_DIM, tq), F32),
                            pltpu.VMEM((qt * sel_rows, tq), F32),
                            pltpu.VMEM((NSLC_PAD, tq), F32),
                            pltpu.VMEM((rheads, tq, tq), F32),
                            pltpu.VMEM((rheads, tq, tq), F32),
                            pltpu.VMEM((rheads, ncp, tq), F32),
                            pltpu.VMEM((2 * rheads, tq), F32),
                            pltpu.VMEM((qt * 2 * rheads, tq), F32),
                            pltpu.VMEM((seq, rheads * LANES), BF16)]),
        out_shape=jax.ShapeDtypeStruct((n, groups * width), BF16),
        compiler_params=_cparams(("parallel", "parallel"), 56),
        name="nsa_attn",
    )(*tabs, qe, kc, vct, kboth, vtboth, gt, p1, ovt)


def _rope_tables(pos, d):
    rd = d // 4
    half = rd // 2
    inv = ROPE_THETA ** (-jnp.arange(half, dtype=F32) / half)
    ang = pos.astype(F32)[:, None] * inv[None, :]
    cos, sin = jnp.cos(ang), jnp.sin(ang)
    npos = pos.shape[0]
    zeros = jnp.zeros((npos, d), F32)
    c = jnp.concatenate([cos, cos, jnp.ones((npos, d - rd), F32)], axis=1)
    s1 = zeros.at[:, :half].set(-sin)
    s2 = zeros.at[:, half:rd].set(sin)
    return tuple(jnp.tile(t, (1, LANES // d)) for t in (c, s1, s2))


def _overlap_t(ncp, nslc):
    start = np.arange(ncp) * CMP_STRIDE
    s0 = np.arange(nslc) * SLC_BLOCK
    lo = np.maximum(start[:, None], s0[None, :])
    hi = np.minimum(start[:, None] + CMP_BLOCK, s0[None, :] + SLC_BLOCK)
    ov = np.clip(hi - lo, 0, None) / CMP_BLOCK
    ov[ncp - 1] = 0.0
    out = np.zeros((NSLC_PAD, ncp), np.float32)
    out[:nslc] = ov.T
    return jnp.asarray(out, BF16)


def kernel(x, mem, mem_norm, l0_norm, l0_w_in, l0_a_vnorm, l0_a_ws, l0_a_bs, l0_b_qnorm, l0_b_knorm, l0_b_lq1, l0_b_lk1, l0_b_lq2, l0_b_lk2, l0_b_subln, l0_m_wkv, l0_m_qnorm, l0_m_knorm, l0_w_out, l1_norm, l1_w_in, l1_c_conv_w, l1_c_conv_b, l1_c_norm, l1_d_qnorm, l1_d_knorm, l1_d_cmp_pos_k, l1_d_cmp_w1_k, l1_d_cmp_w2_k, l1_d_cmp_pos_v, l1_d_cmp_w1_v, l1_d_cmp_w2_v, l1_m_wkv, l1_m_qnorm, l1_m_knorm, l1_w_out):
    batch, seq, d_model = x.shape
    n = batch * seq
    tq = 256
    mw = M_HEADS * HEAD_DIM
    pos = jnp.arange(seq, dtype=jnp.int32)
    x0 = x.reshape(n, d_model)

    wkv_both = jnp.concatenate([l0_m_wkv, l1_m_wkv], axis=1).astype(BF16)
    kgains = jnp.stack([jnp.tile(l0_m_knorm, M_HEADS), jnp.tile(l1_m_knorm, M_HEADS)])
    mem_kh, mem_vt = _mem_kv(mem, mem_norm, wkv_both, kgains)

    w = l0_w_in
    tab32 = _rope_tables(pos, B_QK_DIM)
    b_width = 2 * B_HEADS * B_QK_DIM
    a_width = l0_a_vnorm.shape[0]
    bs_exp = jnp.repeat(l0_a_bs.T, a_width // A_GROUPS, axis=1)
    p0, y_a, qn, kn, vt = _layer0_proj(
        x0, l0_norm, w[:, 0:1536].astype(BF16), w[:, 1536:3072].astype(BF16), w[:, 3072:4096].astype(BF16),
        tab32, jnp.tile(l0_b_qnorm, b_width // B_QK_DIM).reshape(1, b_width),
        jnp.tile(l0_b_knorm, b_width // B_QK_DIM).reshape(1, b_width),
        l0_a_ws, bs_exp, l0_a_vnorm.reshape(1, a_width), batch, seq, 2 * tq, tq)
    lam_init = 0.8 - 0.6 * math.exp(-0.3 * 1)
    lam_p = jnp.stack([l0_b_lq1, l0_b_lk1, l0_b_lq2, l0_b_lk2])
    y_b = _diff_attn(lam_p, qn, kn, vt, p0, jnp.tile(l0_b_subln, 2).reshape(1, LANES),
                     batch, seq, tq, lam_init, 0)
    y_m = _mem_attn(p0, mem_kh, mem_vt, 0, jnp.tile(l0_m_qnorm, M_HEADS).reshape(1, mw),
                    512 // mw, 768 // mw, seq, 2 * tq)

    w = l1_w_in
    n_bg = 3 * D_HEADS
    w_rest = jnp.concatenate([w[:, 2840:3352], w[:, 2048:2304], w[:, 3352:3864]], axis=1)
    w_d = jnp.concatenate([w[:, 1536:2048], w[:, 2304:2816], w[:, 2816:2840],
                           jnp.zeros((d_model, LANES - n_bg), F32)], axis=1)
    groups, rheads = D_KV_GROUPS, D_HEADS // D_KV_GROUPS
    d_width = D_HEADS * HEAD_DIM
    tab64 = _rope_tables(pos, HEAD_DIM)
    x1, p1, y_c, qe, kboth, vtboth, gt = _layer1_proj(
        x0, y_a, y_b, y_m, l0_w_out.astype(BF16),
        l1_norm, w[:, 0:1536].astype(BF16), w_d.astype(BF16), w_rest.astype(BF16), tab64,
        jnp.tile(l1_d_qnorm, D_HEADS).reshape(1, d_width),
        jnp.tile(l1_d_knorm, LANES // HEAD_DIM).reshape(1, LANES),
        l1_c_conv_w, l1_c_conv_b, l1_c_norm, batch, seq, tq)
    kv_blk = 512 // LANES

    ncp = seq // CMP_STRIDE
    cmp_pos = jnp.arange(ncp, dtype=jnp.int32) * CMP_STRIDE + (CMP_BLOCK - 1)
    hidden = l1_d_cmp_w2_k.shape[0]

    def group_w1(w1):
        w = w1.reshape(CMP_BLOCK, HEAD_DIM, hidden)
        return jnp.stack([jnp.zeros((CMP_BLOCK, LANES, hidden), F32)
                          .at[:, g * HEAD_DIM:(g + 1) * HEAD_DIM, :].set(w) for g in range(groups)])

    w2k = jnp.tile(l1_d_cmp_w2_k, (1, LANES // HEAD_DIM))
    w2v = jnp.zeros((hidden, LANES), F32).at[:, :HEAD_DIM].set(l1_d_cmp_w2_v)
    kc, vct = _compress(
        p1, kv_blk, jnp.tile(l1_d_cmp_pos_k, (1, groups)), jnp.tile(l1_d_cmp_pos_v, (1, groups)),
        group_w1(l1_d_cmp_w1_k).astype(BF16), w2k.astype(BF16),
        group_w1(l1_d_cmp_w1_v).astype(BF16), w2v.astype(BF16),
        jnp.tile(l1_d_knorm, LANES // HEAD_DIM).reshape(1, LANES), _rope_tables(cmp_pos, HEAD_DIM),
        batch, seq)

    nslc = seq // SLC_BLOCK
    y_d = _nsa_attn(qe, kc, vct, kboth, vtboth, gt, p1, _overlap_t(ncp, nslc),
                    batch, seq, tq, 0)
    out = _mem_out(x1, y_c, y_d, p1, mem_kh, mem_vt, 1, jnp.tile(l1_m_qnorm, M_HEADS).reshape(1, mw),
                   768 // mw, 1024 // mw, l1_w_out.astype(BF16), seq, 2 * tq)
    return out.reshape(batch, seq, d_model)
```

```python
import functools
import math

import numpy as np
import jax
import jax.numpy as jnp
from jax import lax
from jax.experimental import pallas as pl
from jax.experimental.pallas import tpu as pltpu

F32 = jnp.float32
BF16 = jnp.bfloat16

EPS = 1e-6
ROPE_THETA = 500000.0
HEAD_DIM = 64
A_CHUNK = 128
A_GROUPS = 4
B_HEADS = 8
B_QK_DIM = 32
C_KERNEL = 31
D_HEADS = 8
D_KV_GROUPS = 2
CMP_BLOCK = 32
CMP_STRIDE = 16
SLC_BLOCK = 64
SLC_TOPK = 16
WINDOW = 512
M_HEADS = 4

LANES = 128
SUBLANES = 8
MXU_DIM = 256
HALO = 32
NSLC_PAD = 128
ONES_ROWS = 16

NT_DIMS = (((1,), (1,)), ((), ()))
LOG2E = math.log2(math.e)


def _cparams(semantics, vmem_mb):
    return pltpu.CompilerParams(dimension_semantics=semantics,
                                vmem_limit_bytes=vmem_mb * 1024 * 1024)


def _split_bf16(x):
    hi = x.astype(BF16)
    lo = (x - hi.astype(F32)).astype(BF16)
    return hi, lo


def _group_mean(x2, gsize):
    t, c = x2.shape
    w = min(c, MXU_DIM)
    r = lax.broadcasted_iota(jnp.int32, (w, w), 0) // gsize
    cc = lax.broadcasted_iota(jnp.int32, (w, w), 1) // gsize
    ones = jnp.where(r == cc, 1.0, 0.0).astype(BF16)
    outs = []
    for s in range(c // w):
        hi, lo = _split_bf16(x2[:, s * w:(s + 1) * w])
        outs.append(jnp.dot(hi, ones, preferred_element_type=F32)
                    + jnp.dot(lo, ones, preferred_element_type=F32))
    out = outs[0] if len(outs) == 1 else jnp.concatenate(outs, axis=1)
    return out * (1.0 / gsize)


def _tile_lanes(t, width):
    rep = width // t.shape[1]
    return t if rep == 1 else jnp.concatenate([t] * rep, axis=1)


def _rope(x, cos_t, s1_t, s2_t, half):
    width = x.shape[1]
    up = pltpu.roll(x, width - half, 1)
    dn = pltpu.roll(x, half, 1)
    return (x * _tile_lanes(cos_t, width) + up * _tile_lanes(s1_t, width)
            + dn * _tile_lanes(s2_t, width))


def _matmul_cols(h, w_ref, lo, hi, out_ref=None):
    r = jnp.dot(h, w_ref[:, lo:hi], preferred_element_type=F32)
    if out_ref is None:
        return r
    out_ref[:, lo:hi] = r


def _rms_rows(x, ng_ref):
    ms = jnp.mean(x * x, axis=-1, keepdims=True)
    return (x * lax.rsqrt(ms + EPS) * ng_ref[...]).astype(BF16)


def _scores_t(k, q, valid, st_ref, idx):
    st = lax.dot_general(k, q, NT_DIMS, preferred_element_type=F32)
    if isinstance(valid, (list, tuple)):
        blk = st.shape[0] // len(valid)
        st = jnp.concatenate([jnp.where(v, st[c * blk:(c + 1) * blk, :], -jnp.inf)
                              for c, v in enumerate(valid)], axis=0)
    elif valid is not None:
        st = jnp.where(valid, st, -jnp.inf)
    st_ref[idx] = st
    return jnp.max(st, axis=0, keepdims=True)


def _accumulate_t(st_ref, idx, cmax, m, acc_ref, aidx, vt, guard_empty):
    m_new = jnp.maximum(m, cmax)
    m_use = jnp.where(m_new == -jnp.inf, 0.0, m_new) if guard_empty else m_new
    alpha = jnp.exp2(m - m_use)
    pt = jnp.exp2((st_ref[idx] - m_use).astype(BF16))
    acc_ref[aidx] = alpha * acc_ref[aidx] + jnp.dot(vt, pt, preferred_element_type=F32)
    return m_new


def _pipelined_sweep(n, score_fn, accum_fn, unroll=4):
    score_fn(0, 0)
    stages = n - 1

    def stage(e, parity):
        score_fn(e + 1, 1 - parity)
        accum_fn(e, parity)

    def body(i, c):
        for k in range(unroll):
            stage(unroll * i + k, k % 2)
        return c

    if isinstance(n, int):
        iters, left = divmod(stages, unroll)
        if iters:
            lax.fori_loop(0, iters, body, 0)
        for k in range(left):
            stage(iters * unroll + k, k % 2)
        accum_fn(n - 1, left % 2)
        return
    lax.fori_loop(0, stages // unroll, body, 0)
    base = stages // unroll * unroll
    left = stages - base
    for k in range(unroll - 1):
        pl.when(left > k)(functools.partial(stage, base + k, k % 2))
    for parity in range(2):
        pl.when(left % 2 == parity)(functools.partial(accum_fn, n - 1, parity))


def _mix_out(ya_ref, yb_ref, ym_ref, w_ref):
    wa = ya_ref.shape[1]
    wb = yb_ref.shape[1]
    acc = jnp.dot(ya_ref[...], w_ref[0:wa, :], preferred_element_type=F32)
    acc += jnp.dot(yb_ref[...], w_ref[wa:wa + wb, :], preferred_element_type=F32)
    acc += jnp.dot(ym_ref[...], w_ref[wa + wb:, :], preferred_element_type=F32)
    return acc


def _sgu_group(u, v, gate, w_ref, bs, vgain, o_ref, cols):
    tm = u.shape[0]
    row = lax.broadcasted_iota(jnp.int32, (A_CHUNK, A_CHUNK), 0)
    col = lax.broadcasted_iota(jnp.int32, (A_CHUNK, A_CHUNK), 1)
    w = jnp.where(col <= row, w_ref, 0.0).astype(BF16)
    gv = jax.nn.gelu(v)
    ms = jnp.mean(gv * gv, axis=-1, keepdims=True)
    vn = (gv * lax.rsqrt(ms + EPS) * vgain).astype(BF16)
    gu = jax.nn.gelu(u) * jax.nn.silu(gate)
    for c in range(tm // A_CHUNK):
        rows = slice(c * A_CHUNK, (c + 1) * A_CHUNK)
        z = jnp.dot(w, vn[rows, :], preferred_element_type=F32) + bs
        o_ref[rows, cols] = (gu[rows, :] * z).astype(BF16)


def _layer0_proj_kernel(x_ref, ng_ref, wa_ref, wqkv_ref, w_ref, c_ref, s1_ref, s2_ref, qg_ref, kg_ref,
                        ws_ref, bs_ref, vg_ref, p_ref, ya_ref, qo_ref, ko_ref, vto_ref):
    h = _rms_rows(x_ref[...], ng_ref)
    aw = vg_ref.shape[1]
    gdim = aw // A_GROUPS
    width = qg_ref.shape[1]
    half = B_QK_DIM // 4 // 2
    pw = p_ref.shape[1]

    a = _matmul_cols(h, wa_ref, 0, 3 * aw)

    def sgu(g):
        cols = slice(g * gdim, (g + 1) * gdim)
        _sgu_group(a[:, cols], a[:, aw + g * gdim:aw + (g + 1) * gdim],
                   a[:, 2 * aw + g * gdim:2 * aw + (g + 1) * gdim],
                   ws_ref[g], bs_ref[:, cols], vg_ref[:, cols], ya_ref, cols)

    def prep(t, gain, dst, mul):
        tn = t * lax.rsqrt(_group_mean(t * t, B_QK_DIM) + EPS) * gain[...]
        tr = _rope(tn, c_ref[...], s1_ref[...], s2_ref[...], half)
        dst[...] = (tr * mul if mul != 1.0 else tr).astype(BF16)

    q = _matmul_cols(h, wqkv_ref, 0, width)
    sgu(0)
    k = _matmul_cols(h, wqkv_ref, width, 2 * width)
    sgu(1)
    v = _matmul_cols(h, wqkv_ref, 2 * width, 3 * width)
    sgu(2)
    _matmul_cols(h, w_ref, 0, pw // 2, p_ref)
    sgu(3)
    prep(q, qg_ref, qo_ref, B_QK_DIM ** -0.5 * LOG2E)
    _matmul_cols(h, w_ref, pw // 2, pw, p_ref)
    prep(k, kg_ref, ko_ref, 1.0)
    vt = v.T
    dv = 2 * B_QK_DIM
    ones = jnp.ones((ONES_ROWS, vt.shape[1]), F32)
    parts = []
    for hd in range(B_HEADS):
        parts += [vt[hd * dv:(hd + 1) * dv, :], ones]
    vt_all = jnp.concatenate(parts, axis=0).astype(BF16)
    tk = vto_ref.shape[3]
    for c in range(vto_ref.shape[1]):
        vto_ref[0, c] = vt_all[:, c * tk:(c + 1) * tk]


def _layer0_proj(x, norm_g, w_a, w_qkv, w_rest, tables, qgain, kgain, a_ws, bs_exp, vgain, batch, seq, tm, tk):
    n, d = x.shape
    width = qgain.shape[1]
    aw = vgain.shape[1]
    tiles_per_seq = seq // tm
    tab_spec = pl.BlockSpec((tm, LANES), lambda i: (i % tiles_per_seq, 0))
    vrows = B_HEADS * (2 * B_QK_DIM + ONES_ROWS)

    def full(a):
        return pl.BlockSpec(a.shape, lambda i: (0,) * a.ndim)

    def rows(w):
        return pl.BlockSpec((tm, w), lambda i: (i, 0))

    return pl.pallas_call(
        _layer0_proj_kernel,
        grid=(n // tm,),
        in_specs=[rows(d), pl.BlockSpec((1, d), lambda i: (0, 0)), full(w_a), full(w_qkv), full(w_rest),
                  tab_spec, tab_spec, tab_spec, full(qgain), full(kgain),
                  full(a_ws), full(bs_exp), full(vgain)],
        out_specs=[rows(w_rest.shape[1]), rows(aw), rows(width), rows(width),
                   pl.BlockSpec((1, tm // tk, vrows, tk),
                                lambda i: (i // tiles_per_seq, i % tiles_per_seq, 0, 0))],
        out_shape=[jax.ShapeDtypeStruct((n, w_rest.shape[1]), F32),
                   jax.ShapeDtypeStruct((n, aw), BF16),
                   jax.ShapeDtypeStruct((n, width), BF16), jax.ShapeDtypeStruct((n, width), BF16),
                   jax.ShapeDtypeStruct((batch, seq // tk, vrows, tk), BF16)],
        compiler_params=_cparams(("parallel",), 52),
        name="layer0_proj",
    )(x, norm_g.reshape(1, d), w_a, w_qkv, w_rest, *tables, qgain, kgain, a_ws, bs_exp, vgain)


def _diff_attn_kernel(qi_tab, tile_tab, lam_ref, q_ref, k_ref, vt_ref, g_ref, sub_ref, o_ref,
                      acc_ref, sta_ref, stb_ref, cmax_ref, m_ref, *, tq, qt, lam_init):
    dv = 2 * B_QK_DIM
    vrows = dv + ONES_ROWS
    nmap = 4
    lane = lax.broadcasted_iota(jnp.int32, (1, LANES), 1)
    bufs = (sta_ref, stb_ref)
    acc_ref[...] = jnp.zeros_like(acc_ref)
    m_ref[...] = jnp.full(m_ref.shape, -jnp.inf, F32)
    krow = lax.broadcasted_iota(jnp.int32, (tq, tq), 0)
    qcol = lax.broadcasted_iota(jnp.int32, (tq, tq), 1)
    lam_p = lam_ref[...]
    lam = (jnp.exp(jnp.sum(lam_p[0:1] * lam_p[1:2], axis=-1, keepdims=True))
           - jnp.exp(jnp.sum(lam_p[2:3] * lam_p[3:4], axis=-1, keepdims=True)) + lam_init)

    def rows(tile):
        return pl.ds(pl.multiple_of(tile * tq, tq), tq)

    def scores(qi, tile, buf, valid):
        q = q_ref[rows(qi), :]
        k = k_ref[rows(tile), :]
        zero = jnp.zeros_like(q)
        for j in range(nmap):
            qm = jnp.where(lane // B_QK_DIM == j, q, zero)
            row = buf * nmap + j
            cmax_ref[row:row + 1, :] = _scores_t(k, qm, valid, bufs[buf], j)

    def accumulate(qi, tile, buf):
        for j in range(nmap):
            vt = vt_ref[0, tile, (j // 2) * vrows:(j // 2 + 1) * vrows, :]
            cm, sr = buf * nmap + j, qi * nmap + j
            m_ref[pl.ds(sr, 1), :] = _accumulate_t(bufs[buf], j, cmax_ref[cm:cm + 1, :],
                                                   m_ref[pl.ds(sr, 1), :], acc_ref, sr, vt, False)

    def finish(qi):
        halves = []
        for h in range(2):
            o = [acc_ref[qi * nmap + 2 * h + mp, 0:dv, :] / acc_ref[qi * nmap + 2 * h + mp, dv:dv + 1, :]
                 for mp in range(2)]
            pd = o[0] - lam * o[1]
            ms2 = jnp.mean(pd * pd, axis=0, keepdims=True)
            halves.append(pd * lax.rsqrt(ms2 + EPS))
        ob = jnp.concatenate(halves, axis=0).T * sub_ref[...] * (1.0 - lam_init)
        o_ref[rows(qi), :] = (ob * jax.nn.silu(g_ref[rows(qi), :])).astype(BF16)

    def below_score(e, buf):
        scores(qi_tab[e], tile_tab[e], buf, None)

    def below_accum(e, buf):
        accumulate(qi_tab[e], tile_tab[e], buf)

    def diag_score(e, buf):
        scores(e, e, buf, krow <= qcol)

    def diag_accum(e, buf):
        accumulate(e, e, buf)
        finish(e)

    _pipelined_sweep(qt * (qt - 1) // 2, below_score, below_accum, unroll=16)
    _pipelined_sweep(qt, diag_score, diag_accum)


def _diff_attn(lam_p, qn, kn, vt, p0, subln, batch, seq, tq, lam_init, g_off):
    n, width = qn.shape
    pairs = width // LANES
    qt = seq // tq
    vrows = 2 * B_QK_DIM + ONES_ROWS
    below = [(qi, t) for qi in range(qt) for t in range(qi)]
    qi_tab = jnp.asarray([e[0] for e in below], jnp.int32)
    tile_tab = jnp.asarray([e[1] for e in below], jnp.int32)
    kernel = functools.partial(_diff_attn_kernel, tq=tq, qt=qt, lam_init=lam_init)
    seq_spec = pl.BlockSpec((seq, LANES), lambda b, p, *_: (b, p))
    return pl.pallas_call(
        kernel,
        grid_spec=pltpu.PrefetchScalarGridSpec(
            num_scalar_prefetch=2,
            grid=(batch, pairs),
            in_specs=[pl.BlockSpec(lam_p.shape, lambda b, p, *_: (0, 0)),
                      seq_spec, seq_spec,
                      pl.BlockSpec((1, qt, 2 * vrows, tq), lambda b, p, *_: (b, 0, p, 0)),
                      pl.BlockSpec((seq, LANES), lambda b, p, *_: (b, g_off + p)),
                      pl.BlockSpec((1, LANES), lambda b, p, *_: (0, 0))],
            out_specs=seq_spec,
            scratch_shapes=[pltpu.VMEM((qt * 4, vrows, tq), F32), pltpu.VMEM((4, tq, tq), F32),
                            pltpu.VMEM((4, tq, tq), F32), pltpu.VMEM((8, tq), F32),
                            pltpu.VMEM((qt * 4, tq), F32)]),
        out_shape=jax.ShapeDtypeStruct((n, width), BF16),
        compiler_params=_cparams(("parallel", "parallel"), 40),
        name="diff_attn",
    )(qi_tab, tile_tab, lam_p, qn, kn, vt, p0, subln)


def _mem_kv_kernel(mem_ref, g_ref, w_ref, kg_ref, kh_ref, vt_ref):
    x = mem_ref[0]
    ms = jnp.mean(x * x, axis=-1, keepdims=True)
    h = (x * lax.rsqrt(ms + EPS) * g_ref[...]).astype(BF16)
    kv = jnp.dot(h, w_ref[...], preferred_element_type=F32)
    mw = kv.shape[1] // 4
    lane = lax.broadcasted_iota(jnp.int32, (1, mw), 1)
    ones = jnp.ones((ONES_ROWS, kv.shape[0]), F32)
    for layer in range(2):
        k = kv[:, (2 * layer) * mw:(2 * layer + 1) * mw]
        kn = k * lax.rsqrt(_group_mean(k * k, HEAD_DIM) + EPS) * kg_ref[layer:layer + 1, :]
        vt = kv[:, (2 * layer + 1) * mw:(2 * layer + 2) * mw].T
        parts = []
        for hd in range(M_HEADS):
            kh_ref[layer, 0, hd] = jnp.where(lane // HEAD_DIM == hd, kn, 0.0).astype(BF16)
            parts += [vt[hd * HEAD_DIM:(hd + 1) * HEAD_DIM, :], ones]
        vt_ref[layer, 0] = jnp.concatenate(parts, axis=0).astype(BF16)


def _mem_kv(mem, mem_norm, wkv_both, kgains):
    batch, mtok, d = mem.shape
    mw = wkv_both.shape[1] // 4
    vrows = M_HEADS * (HEAD_DIM + ONES_ROWS)
    return pl.pallas_call(
        _mem_kv_kernel,
        grid=(batch,),
        in_specs=[pl.BlockSpec((1, mtok, d), lambda b: (b, 0, 0)),
                  pl.BlockSpec((1, d), lambda b: (0, 0)),
                  pl.BlockSpec(wkv_both.shape, lambda b: (0, 0)),
                  pl.BlockSpec(kgains.shape, lambda b: (0, 0))],
        out_specs=[pl.BlockSpec((2, 1, M_HEADS, mtok, mw), lambda b: (0, b, 0, 0, 0)),
                   pl.BlockSpec((2, 1, vrows, mtok), lambda b: (0, b, 0, 0))],
        out_shape=[jax.ShapeDtypeStruct((2, batch, M_HEADS, mtok, mw), BF16),
                   jax.ShapeDtypeStruct((2, batch, vrows, mtok), BF16)],
        compiler_params=_cparams(("parallel",), 32),
        name="mem_kv",
    )(mem, mem_norm.reshape(1, d), wkv_both, kgains)


def _mem_attn_tile(q_ref, g_ref, kh_ref, vt_ref, qg_ref, st_ref):
    vrows = HEAD_DIM + ONES_ROWS
    x = q_ref[...]
    qn = (x * lax.rsqrt(_group_mean(x * x, HEAD_DIM) + EPS) * qg_ref[...]
          * (HEAD_DIM ** -0.5 * LOG2E)).astype(BF16)
    cmax = [_scores_t(kh_ref[0, 0, h], qn, None, st_ref, h) for h in range(M_HEADS)]
    outs = []
    for h in range(M_HEADS):
        pt = jnp.exp2((st_ref[h] - cmax[h]).astype(BF16))
        acc = jnp.dot(vt_ref[0, 0, h * vrows:(h + 1) * vrows, :], pt, preferred_element_type=F32)
        outs.append(acc[0:HEAD_DIM, :] / acc[HEAD_DIM:HEAD_DIM + 1, :])
    out = jnp.concatenate(outs, axis=0).T
    return (out * jax.nn.silu(g_ref[...])).astype(BF16)


def _mem_attn_kernel(q_ref, g_ref, kh_ref, vt_ref, qg_ref, o_ref, st_ref):
    o_ref[...] = _mem_attn_tile(q_ref, g_ref, kh_ref, vt_ref, qg_ref, st_ref)


def _mem_out_kernel(x_ref, ya_ref, yb_ref, q_ref, g_ref, kh_ref, vt_ref, qg_ref, w_ref, o_ref, st_ref):
    ym = _mem_attn_tile(q_ref, g_ref, kh_ref, vt_ref, qg_ref, st_ref)
    wa = ya_ref.shape[1]
    wb = yb_ref.shape[1]
    acc = jnp.dot(ya_ref[...], w_ref[0:wa, :], preferred_element_type=F32)
    acc += jnp.dot(yb_ref[...], w_ref[wa:wa + wb, :], preferred_element_type=F32)
    acc += jnp.dot(ym, w_ref[wa + wb:, :], preferred_element_type=F32)
    o_ref[...] = x_ref[...] + acc


def _mem_out(x, ya, yb, p, kh, vt, layer, qgain, q_blk, g_blk, w, seq, tq):
    n, d = x.shape
    _, batch, heads, mtok, mw = kh.shape
    qt = seq // tq

    def rows(width, blk=0):
        return pl.BlockSpec((tq, width), lambda i: (i, blk))

    return pl.pallas_call(
        _mem_out_kernel,
        grid=(n // tq,),
        in_specs=[rows(d), rows(ya.shape[1]), rows(yb.shape[1]), rows(mw, q_blk), rows(mw, g_blk),
                  pl.BlockSpec((1, 1, heads, mtok, mw), lambda i: (layer, i // qt, 0, 0, 0)),
                  pl.BlockSpec((1, 1, vt.shape[2], mtok), lambda i: (layer, i // qt, 0, 0)),
                  pl.BlockSpec((1, mw), lambda i: (0, 0)),
                  pl.BlockSpec(w.shape, lambda i: (0, 0))],
        out_specs=rows(d),
        out_shape=jax.ShapeDtypeStruct((n, d), F32),
        scratch_shapes=[pltpu.VMEM((heads, mtok, tq), F32)],
        compiler_params=_cparams(("parallel",), 40),
        name="mem_out_proj",
    )(x, ya, yb, p, p, kh, vt, qgain, w)


def _mem_attn(p, kh, vt, layer, qgain, q_blk, g_blk, seq, tq):
    n = p.shape[0]
    _, batch, heads, mtok, mw = kh.shape
    qt = seq // tq
    return pl.pallas_call(
        _mem_attn_kernel,
        grid=(n // tq,),
        in_specs=[pl.BlockSpec((tq, mw), lambda i: (i, q_blk)),
                  pl.BlockSpec((tq, mw), lambda i: (i, g_blk)),
                  pl.BlockSpec((1, 1, heads, mtok, mw), lambda i: (layer, i // qt, 0, 0, 0)),
                  pl.BlockSpec((1, 1, vt.shape[2], mtok), lambda i: (layer, i // qt, 0, 0)),
                  pl.BlockSpec((1, mw), lambda i: (0, 0))],
        out_specs=pl.BlockSpec((tq, mw), lambda i: (i, 0)),
        out_shape=jax.ShapeDtypeStruct((n, mw), BF16),
        scratch_shapes=[pltpu.VMEM((heads, mtok, tq), F32)],
        compiler_params=_cparams(("parallel",), 32),
        name="mem_attn",
    )(p, p, kh, vt, qgain)


def _layer1_proj_kernel(x_ref, ya_ref, yb_ref, ym_ref, wo_ref, ng_ref, wc_ref, wd_ref, w_ref,
                        c_ref, s1_ref, s2_ref, qg_ref, kg_ref, cw_ref, cb_ref, cn_ref,
                        x1_ref, p_ref, yc_ref, qo_ref, ko_ref, vto_ref, gto_ref,
                        hbuf_ref, shift_ref, *, tiles_per_seq):
    i = pl.program_id(0)
    ts = x_ref.shape[0]
    cs = hbuf_ref.shape[0] - HALO
    span = cs + HALO
    cwid = cn_ref.shape[1]
    qw = qg_ref.shape[1]
    pw = p_ref.shape[1]
    half = HEAD_DIM // 4 // 2
    tk = gto_ref.shape[3]
    x1 = x_ref[...] + _mix_out(ya_ref, yb_ref, ym_ref, wo_ref)
    x1_ref[...] = x1
    h = _rms_rows(x1, ng_ref)

    @pl.when(i % tiles_per_seq == 0)
    def _():
        hbuf_ref[0:HALO, :] = jnp.zeros((HALO, cwid), F32)

    ca = _matmul_cols(h, wc_ref, 0, cwid)
    cbv = _matmul_cols(h, wc_ref, cwid, 2 * cwid)
    q = _matmul_cols(h, wd_ref, 0, qw)
    glu = ca * jax.nn.sigmoid(cbv)
    first = HALO - (C_KERNEL - 1)

    def stage_rows(r0):
        hbuf_ref[HALO:HALO + cs, :] = glu[r0:r0 + cs, :]
        for ph in range(SUBLANES):
            shift_ref[ph, 0:span - ph, :] = hbuf_ref[ph:span, :]

    def conv_rows(r0, gate):
        y = cb_ref[...]
        for j in range(C_KERNEL):
            ph, base = (first + j) % SUBLANES, (first + j) // SUBLANES * SUBLANES
            y = y + shift_ref[ph, base:base + cs, :] * cw_ref[j:j + 1, :]
        ms = jnp.mean(y * y, axis=-1, keepdims=True)
        yn = y * lax.rsqrt(ms + EPS) * cn_ref[...]
        yc_ref[r0:r0 + cs, :] = (jax.nn.silu(yn) * jax.nn.silu(gate[r0:r0 + cs, :])).astype(BF16)
        hbuf_ref[0:HALO, :] = hbuf_ref[cs:cs + HALO, :]

    stage_rows(0)
    cg = _matmul_cols(h, wc_ref, 2 * cwid, 3 * cwid)
    d = _matmul_cols(h, wd_ref, qw, wd_ref.shape[1])
    psplit = -(-pw // (2 * MXU_DIM)) * MXU_DIM
    for blk in range(ts // cs):
        if blk:
            stage_rows(blk * cs)
        conv_rows(blk * cs, cg)
        if blk == 0:
            _matmul_cols(h, w_ref, 0, psplit, p_ref)
    _matmul_cols(h, w_ref, psplit, pw, p_ref)

    tabs = (c_ref[...], s1_ref[...], s2_ref[...])
    tn = q * lax.rsqrt(_group_mean(q * q, HEAD_DIM) + EPS) * qg_ref[...]
    qo_ref[...] = (_rope(tn, *tabs, half) * (HEAD_DIM ** -0.5 * LOG2E)).astype(BF16)
    ones = jnp.ones((ONES_ROWS, ts), F32)
    lane = lax.broadcasted_iota(jnp.int32, (1, LANES), 1)
    for br in range(2):
        k = d[:, 2 * br * LANES:(2 * br + 1) * LANES]
        kn = k * lax.rsqrt(_group_mean(k * k, HEAD_DIM) + EPS) * kg_ref[...]
        kr = _rope(kn, *tabs, half)
        swapped = pltpu.roll(kr, HEAD_DIM, 1)
        for g in range(D_KV_GROUPS):
            own = (lane // HEAD_DIM) == g
            ko_ref[br, :, g * LANES:(g + 1) * LANES] = jnp.where(own, kr, swapped).astype(BF16)
        vt = d[:, (2 * br + 1) * LANES:(2 * br + 2) * LANES].T
        parts = []
        for g in range(D_KV_GROUPS):
            parts += [vt[g * HEAD_DIM:(g + 1) * HEAD_DIM, :], ones]
        vt_all = jnp.concatenate(parts, axis=0).astype(BF16)
        for c in range(ts // tk):
            vto_ref[0, br, c] = vt_all[:, c * tk:(c + 1) * tk]
    gates = jax.nn.sigmoid(d[:, 4 * LANES:5 * LANES]).T
    for c in range(ts // tk):
        gto_ref[0, c] = gates[:, c * tk:(c + 1) * tk]


def _layer1_proj(x, ya, yb, ym, w_out, norm_g, w_c, w_d, w_rest, tables, qgain, kgain,
                 conv_w, conv_b, conv_norm, batch, seq, tm, tk):
    n, dm = x.shape
    cwid = conv_w.shape[1]
    tiles_per_seq = seq // tm
    tab_spec = pl.BlockSpec((tm, LANES), lambda i: (i % tiles_per_seq, 0))
    vrows = D_KV_GROUPS * (HEAD_DIM + ONES_ROWS)

    def full(a):
        return pl.BlockSpec(a.shape, lambda i: (0,) * a.ndim)

    def rows(w):
        return pl.BlockSpec((tm, w), lambda i: (i, 0))

    kernel = functools.partial(_layer1_proj_kernel, tiles_per_seq=tiles_per_seq)
    return pl.pallas_call(
        kernel,
        grid=(n // tm,),
        in_specs=[rows(dm), rows(ya.shape[1]), rows(yb.shape[1]), rows(ym.shape[1]), full(w_out),
                  pl.BlockSpec((1, dm), lambda i: (0, 0)), full(w_c), full(w_d), full(w_rest),
                  tab_spec, tab_spec, tab_spec, full(qgain), full(kgain),
                  full(conv_w), pl.BlockSpec((1, cwid), lambda i: (0, 0)),
                  pl.BlockSpec((1, cwid), lambda i: (0, 0))],
        out_specs=[rows(dm), rows(w_rest.shape[1]), rows(cwid), rows(qgain.shape[1]),
                   pl.BlockSpec((2, tm, D_KV_GROUPS * LANES), lambda i: (0, i, 0)),
                   pl.BlockSpec((1, 2, tm // tk, vrows, tk),
                                lambda i: (i // tiles_per_seq, 0, i % tiles_per_seq, 0, 0)),
                   pl.BlockSpec((1, tm // tk, LANES, tk),
                                lambda i: (i // tiles_per_seq, i % tiles_per_seq, 0, 0))],
        out_shape=[jax.ShapeDtypeStruct((n, dm), F32),
                   jax.ShapeDtypeStruct((n, w_rest.shape[1]), F32),
                   jax.ShapeDtypeStruct((n, cwid), BF16),
                   jax.ShapeDtypeStruct((n, qgain.shape[1]), BF16),
                   jax.ShapeDtypeStruct((2, n, D_KV_GROUPS * LANES), BF16),
                   jax.ShapeDtypeStruct((batch, 2, seq // tk, vrows, tk), BF16),
                   jax.ShapeDtypeStruct((batch, seq // tk, LANES, tk), F32)],
        scratch_shapes=[pltpu.VMEM((tk + HALO, cwid), F32),
                        pltpu.VMEM((SUBLANES, tk + HALO, cwid), F32)],
        compiler_params=_cparams(("arbitrary",), 56),
        name="layer1_proj",
    )(x, ya, yb, ym, w_out, norm_g.reshape(1, dm), w_c, w_d, w_rest, *tables, qgain, kgain, conv_w,
      conv_b.reshape(1, cwid), conv_norm.reshape(1, cwid))


def _compress_kernel(tk_ref, tv_ref, pk_ref, pv_ref, w1k_ref, w2k_ref, w1v_ref, w2v_ref,
                     kg_ref, c_ref, s1_ref, s2_ref, kc_ref, vct_ref):
    half = HEAD_DIM // 4 // 2
    ncp = kc_ref.shape[2]

    def mlp(t_ref, p_ref, w1_ref, w2):
        a = jnp.zeros((ncp, w1_ref.shape[3]), F32)
        b = jnp.zeros((ncp, w1_ref.shape[3]), F32)
        for l in range(CMP_STRIDE):
            x = t_ref[pl.ds(l, ncp, stride=CMP_STRIDE), :]
            a += jnp.dot((x + p_ref[l:l + 1, :]).astype(BF16), w1_ref[0, l],
                         preferred_element_type=F32)
            b += jnp.dot((x + p_ref[CMP_STRIDE + l:CMP_STRIDE + l + 1, :]).astype(BF16),
                         w1_ref[0, CMP_STRIDE + l], preferred_element_type=F32)
        h = a + pltpu.roll(b, ncp - 1, 0)
        return jnp.dot(jax.nn.silu(h).astype(BF16), w2, preferred_element_type=F32)

    kc = mlp(tk_ref, pk_ref, w1k_ref, w2k_ref[...])
    kn = kc * lax.rsqrt(_group_mean(kc * kc, HEAD_DIM) + EPS) * kg_ref[...]
    kc_ref[0, 0] = _rope(kn, c_ref[...], s1_ref[...], s2_ref[...], half).astype(BF16)
    vc = mlp(tv_ref, pv_ref, w1v_ref, w2v_ref[...])
    vct_ref[0, 0] = vc.T[0:HEAD_DIM, :].astype(BF16)


def _compress(p1, k_blk, pk, pv, w1k, w2k, w1v, w2v, kgain, tables, batch, seq):
    groups = w1k.shape[0]
    ncp = seq // CMP_STRIDE

    def full(a):
        return pl.BlockSpec(a.shape, lambda b, g: (0,) * a.ndim)

    def per_group(a):
        return pl.BlockSpec((1,) + a.shape[1:], lambda b, g: (g,) + (0,) * (a.ndim - 1))

    return pl.pallas_call(
        _compress_kernel,
        grid=(batch, groups),
        in_specs=[pl.BlockSpec((seq, LANES), lambda b, g: (b, k_blk)),
                  pl.BlockSpec((seq, LANES), lambda b, g: (b, k_blk + 1)),
                  full(pk), full(pv), per_group(w1k), full(w2k), per_group(w1v), full(w2v),
                  full(kgain), full(tables[0]), full(tables[1]), full(tables[2])],
        out_specs=[pl.BlockSpec((1, 1, ncp, LANES), lambda b, g: (b, g, 0, 0)),
                   pl.BlockSpec((1, 1, HEAD_DIM, ncp), lambda b, g: (b, g, 0, 0))],
        out_shape=[jax.ShapeDtypeStruct((batch, groups, ncp, LANES), BF16),
                   jax.ShapeDtypeStruct((batch, groups, HEAD_DIM, ncp), BF16)],
        compiler_params=_cparams(("parallel", "parallel"), 32),
        name="nsa_compress",
    )(p1, p1, pk, pv, w1k, w2k, w1v, w2v, kgain, *tables)


def _nsa_kernel(qi_tab, tile_tab, br_tab, q_ref, kc_ref, vct_ref, k_ref, vt_ref, gt_ref,
                dg_ref, ovt_ref, o_ref, acc_ref, oc_ref, sel_ref, imp_ref, sta_ref, stb_ref, cst_ref,
                cmax_ref, m_ref, qh_ref, *, tq, qt, nslc, nsel, nbelow, nentries):
    g = pl.program_id(1)
    rheads = D_HEADS // D_KV_GROUPS
    seq = qt * tq
    tiny = float(np.finfo(np.float32).tiny)
    qcol = lax.broadcasted_iota(jnp.int32, (tq, tq), 1)
    krow = lax.broadcasted_iota(jnp.int32, (tq, tq), 0)
    bufs = (sta_ref, stb_ref)
    blocks_per_tile = tq // SLC_BLOCK
    nslab = -(-nslc // SUBLANES)
    sel_rows = nslab * SUBLANES

    def rows(tile):
        return pl.ds(tile * tq if isinstance(tile, int) else pl.multiple_of(tile * tq, tq), tq)

    half_lane = lax.broadcasted_iota(jnp.int32, (1, LANES), 1) // HEAD_DIM

    def split_heads(t, c):
        for r in range(rheads):
            blk = q_ref[rows(t), (r // 2) * LANES:(r // 2 + 1) * LANES]
            qh_ref[rows(t), r * LANES:(r + 1) * LANES] = jnp.where(half_lane == r % 2, blk,
                                                                  jnp.zeros_like(blk))
        return c

    lax.fori_loop(0, qt, split_heads, 0)

    def heads_q(qi):
        return [qh_ref[rows(qi), r * LANES:(r + 1) * LANES] for r in range(rheads)]

    kc = kc_ref[0, 0]
    vct = vct_ref[0, 0]
    ncp = kc.shape[0]
    ovt = ovt_ref[...]

    def select_blocks(qi):
        q = heads_q(qi)
        ncq = min(ncp, -(-(((qi + 1) * tq - CMP_BLOCK) // CMP_STRIDE + 1) // LANES) * LANES)
        cend = lax.broadcasted_iota(jnp.int32, (ncq, tq), 0) * CMP_STRIDE + (CMP_BLOCK - 1)
        cvalid = cend <= qi * tq + lax.broadcasted_iota(jnp.int32, (ncq, tq), 1)
        psum = jnp.zeros((ncq, tq), F32)
        cmax = []
        for r in range(rheads):
            st = lax.dot_general(kc[0:ncq, :], q[r], NT_DIMS, preferred_element_type=F32)
            st = jnp.where(cvalid, st, -jnp.inf)
            cst_ref[r, 0:ncq, :] = st
            cmax.append(jnp.max(st, axis=0, keepdims=True))
        for r in range(rheads):
            m = jnp.where(jnp.isfinite(cmax[r]), cmax[r], 0.0)
            e = jnp.exp2(cst_ref[r, 0:ncq, :] - m)
            p = e / jnp.maximum(jnp.sum(e, axis=0, keepdims=True), tiny)
            psum += p
            oc_ref[qi * rheads + r] = jnp.dot(vct[:, 0:ncq], p.astype(BF16), preferred_element_type=F32)
        hi, lo = _split_bf16(psum)
        imp = (jnp.dot(ovt[:, 0:ncq], hi, preferred_element_type=F32)
               + jnp.dot(ovt[:, 0:ncq], lo, preferred_element_type=F32))
        blk = lax.broadcasted_iota(jnp.int32, (NSLC_PAD, tq), 0)
        cur = (qi * tq + lax.broadcasted_iota(jnp.int32, (NSLC_PAD, tq), 1)) // SLC_BLOCK
        imp = jnp.where(blk > cur, -jnp.inf, imp)
        forced = (blk == 0) | (blk == cur) | (blk == cur - 1)
        imp = jnp.where(forced, jnp.inf, imp)
        imp_ref[...] = imp
        nvis = min(nslc, (qi + 1) * blocks_per_tile)
        vslab = -(-nvis // SUBLANES)
        slabs = [imp[s * SUBLANES:(s + 1) * SUBLANES, :] for s in range(vslab)]
        ranks = [jnp.zeros((SUBLANES, tq), F32) for _ in range(vslab)]
        sub = lax.broadcasted_iota(jnp.int32, (SUBLANES, tq), 0)
        for jp in range(nvis):
            row = imp_ref[jp:jp + 1, :]
            for s in range(vslab):
                if s * SUBLANES > jp:
                    ahead = jnp.where(row >= slabs[s], 1.0, 0.0)
                elif (s + 1) * SUBLANES - 1 < jp:
                    ahead = jnp.where(row > slabs[s], 1.0, 0.0)
                else:
                    ahead = jnp.where(sub > jp - s * SUBLANES, jnp.where(row >= slabs[s], 1.0, 0.0),
                                      jnp.where(row > slabs[s], 1.0, 0.0))
                ranks[s] = ranks[s] + ahead
        for s in range(vslab):
            start = qi * sel_rows + s * SUBLANES
            sel_ref[start:start + SUBLANES, :] = jnp.where(ranks[s] < nsel, 1.0, 0.0)

    for qi in range(qt):
        select_blocks(qi)
    acc_ref[...] = jnp.zeros_like(acc_ref)
    m_ref[...] = jnp.full(m_ref.shape, -jnp.inf, F32)

    def chosen_rows(qi, tile):
        base = qi * sel_rows + tile * blocks_per_tile
        return [sel_ref[pl.ds(base + c, 1), :] for c in range(blocks_per_tile)]

    def scores(qi, tile, br, buf, valid):
        q = heads_q(qi)
        k = k_ref[br, rows(tile), :]
        for r in range(rheads):
            row = buf * rheads + r
            cmax_ref[row:row + 1, :] = _scores_t(k, q[r], valid, bufs[buf], r)

    def below_score_fn(e, buf):
        qi, tile = qi_tab[e], tile_tab[e]
        scores(qi, tile, 0, buf, [c > 0.5 for c in chosen_rows(qi, tile)])

    def edge_score_fn(e, buf):
        qi, tile, br = qi_tab[e], tile_tab[e], br_tab[e]
        chosen = jnp.concatenate([jnp.broadcast_to(c, (SLC_BLOCK, tq)) for c in chosen_rows(qi, tile)],
                                 axis=0)
        dpos = (qi - tile) * tq + (qcol - krow)
        reach = jnp.where(br == 0, seq, WINDOW).astype(jnp.uint32)
        kept = jnp.where(jnp.maximum(chosen, br.astype(F32)) > 0.5, dpos, -1)
        scores(qi, tile, br, buf, kept.astype(jnp.uint32) < reach)

    def accum_fn(e, buf):
        qi, tile, br = qi_tab[e], tile_tab[e], br_tab[e]
        vt = vt_ref[0, br, tile]
        for r in range(rheads):
            cm, sr = buf * rheads + r, (qi * 2 + br) * rheads + r
            m_ref[pl.ds(sr, 1), :] = _accumulate_t(bufs[buf], r, cmax_ref[cm:cm + 1, :],
                                                   m_ref[pl.ds(sr, 1), :], acc_ref, sr, vt, True)

    _pipelined_sweep(nbelow, below_score_fn, accum_fn, unroll=16)
    _pipelined_sweep(nentries - nbelow, lambda e, buf: edge_score_fn(e + nbelow, buf),
                     lambda e, buf: accum_fn(e + nbelow, buf), unroll=8)

    def finish(qi, c):
        outs = []
        for r in range(rheads):
            row0 = (g * rheads + r) * 3
            o = [oc_ref[qi * rheads + r]]
            for br in range(2):
                sr = (qi * 2 + br) * rheads + r
                o.append(acc_ref[sr, 0:HEAD_DIM, :]
                         / jnp.maximum(acc_ref[sr, HEAD_DIM:HEAD_DIM + 1, :], tiny))
            outs.append(sum(gt_ref[0, qi, pl.ds(row0 + j, 1), :] * o[j] for j in range(3)))
        out = jnp.concatenate(outs, axis=0).T
        o_ref[rows(qi), :] = (out * jax.nn.silu(dg_ref[rows(qi), :])).astype(BF16)
        return c

    lax.fori_loop(0, qt, finish, 0)


def _nsa_attn(qe, kc, vct, kboth, vtboth, gt, p1, ovt, batch, seq, tq, dg_blk):
    n = qe.shape[0]
    groups = D_KV_GROUPS
    rheads = D_HEADS // groups
    width = rheads * HEAD_DIM
    qt = seq // tq
    ncp = kc.shape[2]
    nslc = seq // SLC_BLOCK
    sel_rows = -(-nslc // SUBLANES) * SUBLANES
    wtiles = (WINDOW + tq - 1) // tq
    below = [(qi, t, 0) for qi in range(qt) for t in range(qi)]
    edge = []
    for qi in range(qt):
        edge += [(qi, qi, 0)] + [(qi, t, 1) for t in range(max(qi - wtiles, 0), qi + 1)]
    entries = below + edge
    tabs = [jnp.asarray([e[c] for e in entries], jnp.int32) for c in range(3)]
    kernel = functools.partial(_nsa_kernel, tq=tq, qt=qt, nslc=nslc, nsel=min(SLC_TOPK, nslc),
                               nbelow=len(below), nentries=len(entries))
    return pl.pallas_call(
        kernel,
        grid_spec=pltpu.PrefetchScalarGridSpec(
            num_scalar_prefetch=3,
            grid=(batch, groups),
            in_specs=[pl.BlockSpec((seq, width), lambda b, g, *_: (b, g)),
                      pl.BlockSpec((1, 1, ncp, LANES), lambda b, g, *_: (b, g, 0, 0)),
                      pl.BlockSpec((1, 1, HEAD_DIM, ncp), lambda b, g, *_: (b, g, 0, 0)),
                      pl.BlockSpec((2, seq, LANES), lambda b, g, *_: (0, b, g)),
                      pl.BlockSpec((1, 2, qt, HEAD_DIM + ONES_ROWS, tq), lambda b, g, *_: (b, 0, 0, g, 0)),
                      pl.BlockSpec((1, qt, LANES, tq), lambda b, g, *_: (b, 0, 0, 0)),
                      pl.BlockSpec((seq, width), lambda b, g, *_: (b, dg_blk + g)),
                      pl.BlockSpec(ovt.shape, lambda b, g, *_: (0, 0))],
            out_specs=pl.BlockSpec((seq, width), lambda b, g, *_: (b, g)),
            scratch_shapes=[pltpu.VMEM((qt * 2 * rheads, HEAD_DIM + ONES_ROWS, tq), F32),
                            pltpu.VMEM((qt * rheads, HEAD_DIM, tq), F32),
                            pltpu.VMEM((qt * sel_rows, tq), F32),
                            pltpu.VMEM((NSLC_PAD, tq), F32),
                            pltpu.VMEM((rheads, tq, tq), F32),
                            pltpu.VMEM((rheads, tq, tq), F32),
                            pltpu.VMEM((rheads, ncp, tq), F32),
                            pltpu.VMEM((2 * rheads, tq), F32),
                            pltpu.VMEM((qt * 2 * rheads, tq), F32),
                            pltpu.VMEM((seq, rheads * LANES), BF16)]),
        out_shape=jax.ShapeDtypeStruct((n, groups * width), BF16),
        compiler_params=_cparams(("parallel", "parallel"), 56),
        name="nsa_attn",
    )(*tabs, qe, kc, vct, kboth, vtboth, gt, p1, ovt)


def _rope_tables(pos, d):
    rd = d // 4
    half = rd // 2
    inv = ROPE_THETA ** (-jnp.arange(half, dtype=F32) / half)
    ang = pos.astype(F32)[:, None] * inv[None, :]
    cos, sin = jnp.cos(ang), jnp.sin(ang)
    npos = pos.shape[0]
    zeros = jnp.zeros((npos, d), F32)
    c = jnp.concatenate([cos, cos, jnp.ones((npos, d - rd), F32)], axis=1)
    s1 = zeros.at[:, :half].set(-sin)
    s2 = zeros.at[:, half:rd].set(sin)
    return tuple(jnp.tile(t, (1, LANES // d)) for t in (c, s1, s2))


def _overlap_t(ncp, nslc):
    start = np.arange(ncp) * CMP_STRIDE
    s0 = np.arange(nslc) * SLC_BLOCK
    lo = np.maximum(start[:, None], s0[None, :])
    hi = np.minimum(start[:, None] + CMP_BLOCK, s0[None, :] + SLC_BLOCK)
    ov = np.clip(hi - lo, 0, None) / CMP_BLOCK
    ov[ncp - 1] = 0.0
    out = np.zeros((NSLC_PAD, ncp), np.float32)
    out[:nslc] = ov.T
    return jnp.asarray(out, BF16)


def kernel(x, mem, mem_norm, l0_norm, l0_w_in, l0_a_vnorm, l0_a_ws, l0_a_bs, l0_b_qnorm, l0_b_knorm, l0_b_lq1, l0_b_lk1, l0_b_lq2, l0_b_lk2, l0_b_subln, l0_m_wkv, l0_m_qnorm, l0_m_knorm, l0_w_out, l1_norm, l1_w_in, l1_c_conv_w, l1_c_conv_b, l1_c_norm, l1_d_qnorm, l1_d_knorm, l1_d_cmp_pos_k, l1_d_cmp_w1_k, l1_d_cmp_w2_k, l1_d_cmp_pos_v, l1_d_cmp_w1_v, l1_d_cmp_w2_v, l1_m_wkv, l1_m_qnorm, l1_m_knorm, l1_w_out):
    batch, seq, d_model = x.shape
    n = batch * seq
    tq = 256
    mw = M_HEADS * HEAD_DIM
    pos = jnp.arange(seq, dtype=jnp.int32)
    x0 = x.reshape(n, d_model)

    wkv_both = jnp.concatenate([l0_m_wkv, l1_m_wkv], axis=1).astype(BF16)
    kgains = jnp.stack([jnp.tile(l0_m_knorm, M_HEADS), jnp.tile(l1_m_knorm, M_HEADS)])
    mem_kh, mem_vt = _mem_kv(mem, mem_norm, wkv_both, kgains)

    w = l0_w_in
    tab32 = _rope_tables(pos, B_QK_DIM)
    b_width = 2 * B_HEADS * B_QK_DIM
    a_width = l0_a_vnorm.shape[0]
    bs_exp = jnp.repeat(l0_a_bs.T, a_width // A_GROUPS, axis=1)
    p0, y_a, qn, kn, vt = _layer0_proj(
        x0, l0_norm, w[:, 0:1536].astype(BF16), w[:, 1536:3072].astype(BF16), w[:, 3072:4096].astype(BF16),
        tab32, jnp.tile(l0_b_qnorm, b_width // B_QK_DIM).reshape(1, b_width),
        jnp.tile(l0_b_knorm, b_width // B_QK_DIM).reshape(1, b_width),
        l0_a_ws, bs_exp, l0_a_vnorm.reshape(1, a_width), batch, seq, 2 * tq, tq)
    lam_init = 0.8 - 0.6 * math.exp(-0.3 * 1)
    lam_p = jnp.stack([l0_b_lq1, l0_b_lk1, l0_b_lq2, l0_b_lk2])
    y_b = _diff_attn(lam_p, qn, kn, vt, p0, jnp.tile(l0_b_subln, 2).reshape(1, LANES),
                     batch, seq, tq, lam_init, 0)
    y_m = _mem_attn(p0, mem_kh, mem_vt, 0, jnp.tile(l0_m_qnorm, M_HEADS).reshape(1, mw),
                    512 // mw, 768 // mw, seq, 2 * tq)

    w = l1_w_in
    n_bg = 3 * D_HEADS
    w_rest = jnp.concatenate([w[:, 2840:3352], w[:, 2048:2304], w[:, 3352:3864]], axis=1)
    w_d = jnp.concatenate([w[:, 1536:2048], w[:, 2304:2816], w[:, 2816:2840],
                           jnp.zeros((d_model, LANES - n_bg), F32)], axis=1)
    groups, rheads = D_KV_GROUPS, D_HEADS // D_KV_GROUPS
    d_width = D_HEADS * HEAD_DIM
    tab64 = _rope_tables(pos, HEAD_DIM)
    x1, p1, y_c, qe, kboth, vtboth, gt = _layer1_proj(
        x0, y_a, y_b, y_m, l0_w_out.astype(BF16),
        l1_norm, w[:, 0:1536].astype(BF16), w_d.astype(BF16), w_rest.astype(BF16), tab64,
        jnp.tile(l1_d_qnorm, D_HEADS).reshape(1, d_width),
        jnp.tile(l1_d_knorm, LANES // HEAD_DIM).reshape(1, LANES),
        l1_c_conv_w, l1_c_conv_b, l1_c_norm, batch, seq, 2 * tq, tq)
    kv_blk = 512 // LANES

    ncp = seq // CMP_STRIDE
    cmp_pos = jnp.arange(ncp, dtype=jnp.int32) * CMP_STRIDE + (CMP_BLOCK - 1)
    hidden = l1_d_cmp_w2_k.shape[0]

    def group_w1(w1):
        w = w1.reshape(CMP_BLOCK, HEAD_DIM, hidden)
        return jnp.stack([jnp.zeros((CMP_BLOCK, LANES, hidden), F32)
                          .at[:, g * HEAD_DIM:(g + 1) * HEAD_DIM, :].set(w) for g in range(groups)])

    w2k = jnp.tile(l1_d_cmp_w2_k, (1, LANES // HEAD_DIM))
    w2v = jnp.zeros((hidden, LANES), F32).at[:, :HEAD_DIM].set(l1_d_cmp_w2_v)
    kc, vct = _compress(
        p1, kv_blk, jnp.tile(l1_d_cmp_pos_k, (1, groups)), jnp.tile(l1_d_cmp_pos_v, (1, groups)),
        group_w1(l1_d_cmp_w1_k).astype(BF16), w2k.astype(BF16),
        group_w1(l1_d_cmp_w1_v).astype(BF16), w2v.astype(BF16),
        jnp.tile(l1_d_knorm, LANES // HEAD_DIM).reshape(1, LANES), _rope_tables(cmp_pos, HEAD_DIM),
        batch, seq)

    nslc = seq // SLC_BLOCK
    y_d = _nsa_attn(qe, kc, vct, kboth, vtboth, gt, p1, _overlap_t(ncp, nslc),
                    batch, seq, tq, 0)
    out = _mem_out(x1, y_c, y_d, p1, mem_kh, mem_vt, 1, jnp.tile(l1_m_qnorm, M_HEADS).reshape(1, mw),
                   768 // mw, 1024 // mw, l1_w_out.astype(BF16), seq, 2 * tq)
    return out.reshape(batch, seq, d_model)
```

```python
import functools
import math

import numpy as np
import jax
import jax.numpy as jnp
from jax import lax
from jax.experimental import pallas as pl
from jax.experimental.pallas import tpu as pltpu

F32 = jnp.float32
BF16 = jnp.bfloat16

EPS = 1e-6
ROPE_THETA = 500000.0
HEAD_DIM = 64
A_CHUNK = 128
A_GROUPS = 4
B_HEADS = 8
B_QK_DIM = 32
C_KERNEL = 31
D_HEADS = 8
D_KV_GROUPS = 2
CMP_BLOCK = 32
CMP_STRIDE = 16
SLC_BLOCK = 64
SLC_TOPK = 16
WINDOW = 512
M_HEADS = 4

LANES = 128
SUBLANES = 8
MXU_DIM = 256
HALO = 32
NSLC_PAD = 128
ONES_ROWS = 16

NT_DIMS = (((1,), (1,)), ((), ()))
LOG2E = math.log2(math.e)


def _cparams(semantics, vmem_mb):
    return pltpu.CompilerParams(dimension_semantics=semantics,
                                vmem_limit_bytes=vmem_mb * 1024 * 1024)


def _split_bf16(x):
    hi = x.astype(BF16)
    lo = (x - hi.astype(F32)).astype(BF16)
    return hi, lo


def _group_mean(x2, gsize):
    t, c = x2.shape
    w = min(c, MXU_DIM)
    r = lax.broadcasted_iota(jnp.int32, (w, w), 0) // gsize
    cc = lax.broadcasted_iota(jnp.int32, (w, w), 1) // gsize
    ones = jnp.where(r == cc, 1.0, 0.0).astype(BF16)
    outs = []
    for s in range(c // w):
        hi, lo = _split_bf16(x2[:, s * w:(s + 1) * w])
        outs.append(jnp.dot(hi, ones, preferred_element_type=F32)
                    + jnp.dot(lo, ones, preferred_element_type=F32))
    out = outs[0] if len(outs) == 1 else jnp.concatenate(outs, axis=1)
    return out * (1.0 / gsize)


def _tile_lanes(t, width):
    rep = width // t.shape[1]
    return t if rep == 1 else jnp.concatenate([t] * rep, axis=1)


def _rope(x, cos_t, s1_t, s2_t, half):
    width = x.shape[1]
    up = pltpu.roll(x, width - half, 1)
    dn = pltpu.roll(x, half, 1)
    return (x * _tile_lanes(cos_t, width) + up * _tile_lanes(s1_t, width)
            + dn * _tile_lanes(s2_t, width))


def _matmul_cols(h, w_ref, lo, hi, out_ref=None):
    r = jnp.dot(h, w_ref[:, lo:hi], preferred_element_type=F32)
    if out_ref is None:
        return r
    out_ref[:, lo:hi] = r


def _rms_rows(x, ng_ref):
    ms = jnp.mean(x * x, axis=-1, keepdims=True)
    return (x * lax.rsqrt(ms + EPS) * ng_ref[...]).astype(BF16)


def _scores_t(k, q, valid, st_ref, idx):
    st = lax.dot_general(k, q, NT_DIMS, preferred_element_type=F32)
    if isinstance(valid, (list, tuple)):
        blk = st.shape[0] // len(valid)
        st = jnp.concatenate([jnp.where(v, st[c * blk:(c + 1) * blk, :], -jnp.inf)
                              for c, v in enumerate(valid)], axis=0)
    elif valid is not None:
        st = jnp.where(valid, st, -jnp.inf)
    st_ref[idx] = st
    return jnp.max(st, axis=0, keepdims=True)


def _accumulate_t(st_ref, idx, cmax, m, acc_ref, aidx, vt, guard_empty):
    m_new = jnp.maximum(m, cmax)
    m_use = jnp.where(m_new == -jnp.inf, 0.0, m_new) if guard_empty else m_new
    alpha = jnp.exp2(m - m_use)
    pt = jnp.exp2((st_ref[idx] - m_use).astype(BF16))
    acc_ref[aidx] = alpha * acc_ref[aidx] + jnp.dot(vt, pt, preferred_element_type=F32)
    return m_new


def _pipelined_sweep(n, score_fn, accum_fn, unroll=4):
    score_fn(0, 0)
    stages = n - 1

    def stage(e, parity):
        score_fn(e + 1, 1 - parity)
        accum_fn(e, parity)

    def body(i, c):
        for k in range(unroll):
            stage(unroll * i + k, k % 2)
        return c

    if isinstance(n, int):
        iters, left = divmod(stages, unroll)
        if iters:
            lax.fori_loop(0, iters, body, 0)
        for k in range(left):
            stage(iters * unroll + k, k % 2)
        accum_fn(n - 1, left % 2)
        return
    lax.fori_loop(0, stages // unroll, body, 0)
    base = stages // unroll * unroll
    left = stages - base
    for k in range(unroll - 1):
        pl.when(left > k)(functools.partial(stage, base + k, k % 2))
    for parity in range(2):
        pl.when(left % 2 == parity)(functools.partial(accum_fn, n - 1, parity))


def _mix_out(ya_ref, yb_ref, ym_ref, w_ref):
    wa = ya_ref.shape[1]
    wb = yb_ref.shape[1]
    acc = jnp.dot(ya_ref[...], w_ref[0:wa, :], preferred_element_type=F32)
    acc += jnp.dot(yb_ref[...], w_ref[wa:wa + wb, :], preferred_element_type=F32)
    acc += jnp.dot(ym_ref[...], w_ref[wa + wb:, :], preferred_element_type=F32)
    return acc


def _sgu_group(u, v, gate, w_ref, bs, vgain, o_ref, cols):
    tm = u.shape[0]
    row = lax.broadcasted_iota(jnp.int32, (A_CHUNK, A_CHUNK), 0)
    col = lax.broadcasted_iota(jnp.int32, (A_CHUNK, A_CHUNK), 1)
    w = jnp.where(col <= row, w_ref, 0.0).astype(BF16)
    gv = jax.nn.gelu(v)
    ms = jnp.mean(gv * gv, axis=-1, keepdims=True)
    vn = (gv * lax.rsqrt(ms + EPS) * vgain).astype(BF16)
    gu = jax.nn.gelu(u) * jax.nn.silu(gate)
    for c in range(tm // A_CHUNK):
        rows = slice(c * A_CHUNK, (c + 1) * A_CHUNK)
        z = jnp.dot(w, vn[rows, :], preferred_element_type=F32) + bs
        o_ref[rows, cols] = (gu[rows, :] * z).astype(BF16)


def _layer0_proj_kernel(x_ref, ng_ref, wa_ref, wqkv_ref, w_ref, c_ref, s1_ref, s2_ref, qg_ref, kg_ref,
                        ws_ref, bs_ref, vg_ref, p_ref, ya_ref, qo_ref, ko_ref, vto_ref):
    h = _rms_rows(x_ref[...], ng_ref)
    aw = vg_ref.shape[1]
    gdim = aw // A_GROUPS
    width = qg_ref.shape[1]
    half = B_QK_DIM // 4 // 2
    pw = p_ref.shape[1]

    a = _matmul_cols(h, wa_ref, 0, 3 * aw)

    def sgu(g):
        cols = slice(g * gdim, (g + 1) * gdim)
        _sgu_group(a[:, cols], a[:, aw + g * gdim:aw + (g + 1) * gdim],
                   a[:, 2 * aw + g * gdim:2 * aw + (g + 1) * gdim],
                   ws_ref[g], bs_ref[:, cols], vg_ref[:, cols], ya_ref, cols)

    def prep(t, gain, dst, mul):
        tn = t * lax.rsqrt(_group_mean(t * t, B_QK_DIM) + EPS) * gain[...]
        tr = _rope(tn, c_ref[...], s1_ref[...], s2_ref[...], half)
        dst[...] = (tr * mul if mul != 1.0 else tr).astype(BF16)

    q = _matmul_cols(h, wqkv_ref, 0, width)
    sgu(0)
    k = _matmul_cols(h, wqkv_ref, width, 2 * width)
    sgu(1)
    v = _matmul_cols(h, wqkv_ref, 2 * width, 3 * width)
    sgu(2)
    _matmul_cols(h, w_ref, 0, pw // 2, p_ref)
    sgu(3)
    prep(q, qg_ref, qo_ref, B_QK_DIM ** -0.5 * LOG2E)
    _matmul_cols(h, w_ref, pw // 2, pw, p_ref)
    prep(k, kg_ref, ko_ref, 1.0)
    vt = v.T
    dv = 2 * B_QK_DIM
    ones = jnp.ones((ONES_ROWS, vt.shape[1]), F32)
    parts = []
    for hd in range(B_HEADS):
        parts += [vt[hd * dv:(hd + 1) * dv, :], ones]
    vt_all = jnp.concatenate(parts, axis=0).astype(BF16)
    tk = vto_ref.shape[3]
    for c in range(vto_ref.shape[1]):
        vto_ref[0, c] = vt_all[:, c * tk:(c + 1) * tk]


def _layer0_proj(x, norm_g, w_a, w_qkv, w_rest, tables, qgain, kgain, a_ws, bs_exp, vgain, batch, seq, tm, tk):
    n, d = x.shape
    width = qgain.shape[1]
    aw = vgain.shape[1]
    tiles_per_seq = seq // tm
    tab_spec = pl.BlockSpec((tm, LANES), lambda i: (i % tiles_per_seq, 0))
    vrows = B_HEADS * (2 * B_QK_DIM + ONES_ROWS)

    def full(a):
        return pl.BlockSpec(a.shape, lambda i: (0,) * a.ndim)

    def rows(w):
        return pl.BlockSpec((tm, w), lambda i: (i, 0))

    return pl.pallas_call(
        _layer0_proj_kernel,
        grid=(n // tm,),
        in_specs=[rows(d), pl.BlockSpec((1, d), lambda i: (0, 0)), full(w_a), full(w_qkv), full(w_rest),
                  tab_spec, tab_spec, tab_spec, full(qgain), full(kgain),
                  full(a_ws), full(bs_exp), full(vgain)],
        out_specs=[rows(w_rest.shape[1]), rows(aw), rows(width), rows(width),
                   pl.BlockSpec((1, tm // tk, vrows, tk),
                                lambda i: (i // tiles_per_seq, i % tiles_per_seq, 0, 0))],
        out_shape=[jax.ShapeDtypeStruct((n, w_rest.shape[1]), F32),
                   jax.ShapeDtypeStruct((n, aw), BF16),
                   jax.ShapeDtypeStruct((n, width), BF16), jax.ShapeDtypeStruct((n, width), BF16),
                   jax.ShapeDtypeStruct((batch, seq // tk, vrows, tk), BF16)],
        compiler_params=_cparams(("parallel",), 52),
        name="layer0_proj",
    )(x, norm_g.reshape(1, d), w_a, w_qkv, w_rest, *tables, qgain, kgain, a_ws, bs_exp, vgain)


def _diff_attn_kernel(qi_tab, tile_tab, lam_ref, q_ref, k_ref, vt_ref, g_ref, sub_ref, o_ref,
                      acc_ref, sta_ref, stb_ref, cmax_ref, m_ref, *, tq, qt, lam_init):
    dv = 2 * B_QK_DIM
    vrows = dv + ONES_ROWS
    nmap = 4
    lane = lax.broadcasted_iota(jnp.int32, (1, LANES), 1)
    bufs = (sta_ref, stb_ref)
    acc_ref[...] = jnp.zeros_like(acc_ref)
    m_ref[...] = jnp.full(m_ref.shape, -jnp.inf, F32)
    krow = lax.broadcasted_iota(jnp.int32, (tq, tq), 0)
    qcol = lax.broadcasted_iota(jnp.int32, (tq, tq), 1)
    lam_p = lam_ref[...]
    lam = (jnp.exp(jnp.sum(lam_p[0:1] * lam_p[1:2], axis=-1, keepdims=True))
           - jnp.exp(jnp.sum(lam_p[2:3] * lam_p[3:4], axis=-1, keepdims=True)) + lam_init)

    def rows(tile):
        return pl.ds(pl.multiple_of(tile * tq, tq), tq)

    def scores(qi, tile, buf, valid):
        q = q_ref[rows(qi), :]
        k = k_ref[rows(tile), :]
        zero = jnp.zeros_like(q)
        for j in range(nmap):
            qm = jnp.where(lane // B_QK_DIM == j, q, zero)
            row = buf * nmap + j
            cmax_ref[row:row + 1, :] = _scores_t(k, qm, valid, bufs[buf], j)

    def accumulate(qi, tile, buf):
        for j in range(nmap):
            vt = vt_ref[0, tile, (j // 2) * vrows:(j // 2 + 1) * vrows, :]
            cm, sr = buf * nmap + j, qi * nmap + j
            m_ref[pl.ds(sr, 1), :] = _accumulate_t(bufs[buf], j, cmax_ref[cm:cm + 1, :],
                                                   m_ref[pl.ds(sr, 1), :], acc_ref, sr, vt, False)

    def finish(qi):
        halves = []
        for h in range(2):
            o = [acc_ref[qi * nmap + 2 * h + mp, 0:dv, :] / acc_ref[qi * nmap + 2 * h + mp, dv:dv + 1, :]
                 for mp in range(2)]
            pd = o[0] - lam * o[1]
            ms2 = jnp.mean(pd * pd, axis=0, keepdims=True)
            halves.append(pd * lax.rsqrt(ms2 + EPS))
        ob = jnp.concatenate(halves, axis=0).T * sub_ref[...] * (1.0 - lam_init)
        o_ref[rows(qi), :] = (ob * jax.nn.silu(g_ref[rows(qi), :])).astype(BF16)

    def below_score(e, buf):
        scores(qi_tab[e], tile_tab[e], buf, None)

    def below_accum(e, buf):
        accumulate(qi_tab[e], tile_tab[e], buf)

    def diag_score(e, buf):
        scores(e, e, buf, krow <= qcol)

    def diag_accum(e, buf):
        accumulate(e, e, buf)
        finish(e)

    _pipelined_sweep(qt * (qt - 1) // 2, below_score, below_accum, unroll=16)
    _pipelined_sweep(qt, diag_score, diag_accum)


def _diff_attn(lam_p, qn, kn, vt, p0, subln, batch, seq, tq, lam_init, g_off):
    n, width = qn.shape
    pairs = width // LANES
    qt = seq // tq
    vrows = 2 * B_QK_DIM + ONES_ROWS
    below = [(qi, t) for qi in range(qt) for t in range(qi)]
    qi_tab = jnp.asarray([e[0] for e in below], jnp.int32)
    tile_tab = jnp.asarray([e[1] for e in below], jnp.int32)
    kernel = functools.partial(_diff_attn_kernel, tq=tq, qt=qt, lam_init=lam_init)
    seq_spec = pl.BlockSpec((seq, LANES), lambda b, p, *_: (b, p))
    return pl.pallas_call(
        kernel,
        grid_spec=pltpu.PrefetchScalarGridSpec(
            num_scalar_prefetch=2,
            grid=(batch, pairs),
            in_specs=[pl.BlockSpec(lam_p.shape, lambda b, p, *_: (0, 0)),
                      seq_spec, seq_spec,
                      pl.BlockSpec((1, qt, 2 * vrows, tq), lambda b, p, *_: (b, 0, p, 0)),
                      pl.BlockSpec((seq, LANES), lambda b, p, *_: (b, g_off + p)),
                      pl.BlockSpec((1, LANES), lambda b, p, *_: (0, 0))],
            out_specs=seq_spec,
            scratch_shapes=[pltpu.VMEM((qt * 4, vrows, tq), F32), pltpu.VMEM((4, tq, tq), F32),
                            pltpu.VMEM((4, tq, tq), F32), pltpu.VMEM((8, tq), F32),
                            pltpu.VMEM((qt * 4, tq), F32)]),
        out_shape=jax.ShapeDtypeStruct((n, width), BF16),
        compiler_params=_cparams(("parallel", "parallel"), 40),
        name="diff_attn",
    )(qi_tab, tile_tab, lam_p, qn, kn, vt, p0, subln)


def _mem_kv_kernel(mem_ref, g_ref, w_ref, kg_ref, kh_ref, vt_ref):
    x = mem_ref[0]
    ms = jnp.mean(x * x, axis=-1, keepdims=True)
    h = (x * lax.rsqrt(ms + EPS) * g_ref[...]).astype(BF16)
    kv = jnp.dot(h, w_ref[...], preferred_element_type=F32)
    mw = kv.shape[1] // 4
    lane = lax.broadcasted_iota(jnp.int32, (1, mw), 1)
    ones = jnp.ones((ONES_ROWS, kv.shape[0]), F32)
    for layer in range(2):
        k = kv[:, (2 * layer) * mw:(2 * layer + 1) * mw]
        kn = k * lax.rsqrt(_group_mean(k * k, HEAD_DIM) + EPS) * kg_ref[layer:layer + 1, :]
        vt = kv[:, (2 * layer + 1) * mw:(2 * layer + 2) * mw].T
        parts = []
        for hd in range(M_HEADS):
            kh_ref[layer, 0, hd] = jnp.where(lane // HEAD_DIM == hd, kn, 0.0).astype(BF16)
            parts += [vt[hd * HEAD_DIM:(hd + 1) * HEAD_DIM, :], ones]
        vt_ref[layer, 0] = jnp.concatenate(parts, axis=0).astype(BF16)


def _mem_kv(mem, mem_norm, wkv_both, kgains):
    batch, mtok, d = mem.shape
    mw = wkv_both.shape[1] // 4
    vrows = M_HEADS * (HEAD_DIM + ONES_ROWS)
    return pl.pallas_call(
        _mem_kv_kernel,
        grid=(batch,),
        in_specs=[pl.BlockSpec((1, mtok, d), lambda b: (b, 0, 0)),
                  pl.BlockSpec((1, d), lambda b: (0, 0)),
                  pl.BlockSpec(wkv_both.shape, lambda b: (0, 0)),
                  pl.BlockSpec(kgains.shape, lambda b: (0, 0))],
        out_specs=[pl.BlockSpec((2, 1, M_HEADS, mtok, mw), lambda b: (0, b, 0, 0, 0)),
                   pl.BlockSpec((2, 1, vrows, mtok), lambda b: (0, b, 0, 0))],
        out_shape=[jax.ShapeDtypeStruct((2, batch, M_HEADS, mtok, mw), BF16),
                   jax.ShapeDtypeStruct((2, batch, vrows, mtok), BF16)],
        compiler_params=_cparams(("parallel",), 32),
        name="mem_kv",
    )(mem, mem_norm.reshape(1, d), wkv_both, kgains)


def _mem_attn_tile(q_ref, g_ref, kh_ref, vt_ref, qg_ref, st_ref):
    vrows = HEAD_DIM + ONES_ROWS
    x = q_ref[...]
    qn = (x * lax.rsqrt(_group_mean(x * x, HEAD_DIM) + EPS) * qg_ref[...]
          * (HEAD_DIM ** -0.5 * LOG2E)).astype(BF16)
    cmax = [_scores_t(kh_ref[0, 0, h], qn, None, st_ref, h) for h in range(M_HEADS)]
    outs = []
    for h in range(M_HEADS):
        pt = jnp.exp2((st_ref[h] - cmax[h]).astype(BF16))
        acc = jnp.dot(vt_ref[0, 0, h * vrows:(h + 1) * vrows, :], pt, preferred_element_type=F32)
        outs.append(acc[0:HEAD_DIM, :] / acc[HEAD_DIM:HEAD_DIM + 1, :])
    out = jnp.concatenate(outs, axis=0).T
    return (out * jax.nn.silu(g_ref[...])).astype(BF16)


def _mem_attn_kernel(q_ref, g_ref, kh_ref, vt_ref, qg_ref, o_ref, st_ref):
    o_ref[...] = _mem_attn_tile(q_ref, g_ref, kh_ref, vt_ref, qg_ref, st_ref)


def _mem_out_kernel(x_ref, ya_ref, yb_ref, q_ref, g_ref, kh_ref, vt_ref, qg_ref, w_ref, o_ref, st_ref):
    ym = _mem_attn_tile(q_ref, g_ref, kh_ref, vt_ref, qg_ref, st_ref)
    wa = ya_ref.shape[1]
    wb = yb_ref.shape[1]
    acc = jnp.dot(ya_ref[...], w_ref[0:wa, :], preferred_element_type=F32)
    acc += jnp.dot(yb_ref[...], w_ref[wa:wa + wb, :], preferred_element_type=F32)
    acc += jnp.dot(ym, w_ref[wa + wb:, :], preferred_element_type=F32)
    o_ref[...] = x_ref[...] + acc


def _mem_out(x, ya, yb, p, kh, vt, layer, qgain, q_blk, g_blk, w, seq, tq):
    n, d = x.shape
    _, batch, heads, mtok, mw = kh.shape
    qt = seq // tq

    def rows(width, blk=0):
        return pl.BlockSpec((tq, width), lambda i: (i, blk))

    return pl.pallas_call(
        _mem_out_kernel,
        grid=(n // tq,),
        in_specs=[rows(d), rows(ya.shape[1]), rows(yb.shape[1]), rows(mw, q_blk), rows(mw, g_blk),
                  pl.BlockSpec((1, 1, heads, mtok, mw), lambda i: (layer, i // qt, 0, 0, 0)),
                  pl.BlockSpec((1, 1, vt.shape[2], mtok), lambda i: (layer, i // qt, 0, 0)),
                  pl.BlockSpec((1, mw), lambda i: (0, 0)),
                  pl.BlockSpec(w.shape, lambda i: (0, 0))],
        out_specs=rows(d),
        out_shape=jax.ShapeDtypeStruct((n, d), F32),
        scratch_shapes=[pltpu.VMEM((heads, mtok, tq), F32)],
        compiler_params=_cparams(("parallel",), 40),
        name="mem_out_proj",
    )(x, ya, yb, p, p, kh, vt, qgain, w)


def _mem_attn(p, kh, vt, layer, qgain, q_blk, g_blk, seq, tq):
    n = p.shape[0]
    _, batch, heads, mtok, mw = kh.shape
    qt = seq // tq
    return pl.pallas_call(
        _mem_attn_kernel,
        grid=(n // tq,),
        in_specs=[pl.BlockSpec((tq, mw), lambda i: (i, q_blk)),
                  pl.BlockSpec((tq, mw), lambda i: (i, g_blk)),
                  pl.BlockSpec((1, 1, heads, mtok, mw), lambda i: (layer, i // qt, 0, 0, 0)),
                  pl.BlockSpec((1, 1, vt.shape[2], mtok), lambda i: (layer, i // qt, 0, 0)),
                  pl.BlockSpec((1, mw), lambda i: (0, 0))],
        out_specs=pl.BlockSpec((tq, mw), lambda i: (i, 0)),
        out_shape=jax.ShapeDtypeStruct((n, mw), BF16),
        scratch_shapes=[pltpu.VMEM((heads, mtok, tq), F32)],
        compiler_params=_cparams(("parallel",), 32),
        name="mem_attn",
    )(p, p, kh, vt, qgain)


def _layer1_proj_kernel(x_ref, ya_ref, yb_ref, ym_ref, wo_ref, ng_ref, wc_ref, wd_ref, w_ref,
                        c_ref, s1_ref, s2_ref, qg_ref, kg_ref, cw_ref, cb_ref, cn_ref,
                        x1_ref, p_ref, yc_ref, qo_ref, ko_ref, vto_ref, gto_ref,
                        hbuf_ref, shift_ref, *, tiles_per_seq):
    i = pl.program_id(0)
    ts = x_ref.shape[0]
    cs = hbuf_ref.shape[0] - HALO
    span = cs + HALO
    cwid = cn_ref.shape[1]
    qw = qg_ref.shape[1]
    pw = p_ref.shape[1]
    half = HEAD_DIM // 4 // 2
    tk = gto_ref.shape[3]
    x1 = x_ref[...] + _mix_out(ya_ref, yb_ref, ym_ref, wo_ref)
    x1_ref[...] = x1
    h = _rms_rows(x1, ng_ref)

    @pl.when(i % tiles_per_seq == 0)
    def _():
        hbuf_ref[0:HALO, :] = jnp.zeros((HALO, cwid), F32)

    ca = _matmul_cols(h, wc_ref, 0, cwid)
    cbv = _matmul_cols(h, wc_ref, cwid, 2 * cwid)
    q = _matmul_cols(h, wd_ref, 0, qw)
    glu = ca * jax.nn.sigmoid(cbv)
    first = HALO - (C_KERNEL - 1)

    def stage_rows(r0):
        hbuf_ref[HALO:HALO + cs, :] = glu[r0:r0 + cs, :]
        for ph in range(SUBLANES):
            shift_ref[ph, 0:span - ph, :] = hbuf_ref[ph:span, :]

    def conv_rows(r0, gate):
        y = cb_ref[...]
        for j in range(C_KERNEL):
            ph, base = (first + j) % SUBLANES, (first + j) // SUBLANES * SUBLANES
            y = y + shift_ref[ph, base:base + cs, :] * cw_ref[j:j + 1, :]
        ms = jnp.mean(y * y, axis=-1, keepdims=True)
        yn = y * lax.rsqrt(ms + EPS) * cn_ref[...]
        yc_ref[r0:r0 + cs, :] = (jax.nn.silu(yn) * jax.nn.silu(gate[r0:r0 + cs, :])).astype(BF16)
        hbuf_ref[0:HALO, :] = hbuf_ref[cs:cs + HALO, :]

    stage_rows(0)
    cg = _matmul_cols(h, wc_ref, 2 * cwid, 3 * cwid)
    d = _matmul_cols(h, wd_ref, qw, wd_ref.shape[1])
    psplit = -(-pw // (2 * MXU_DIM)) * MXU_DIM
    for blk in range(ts // cs):
        if blk:
            stage_rows(blk * cs)
        conv_rows(blk * cs, cg)
        if blk == 0:
            _matmul_cols(h, w_ref, 0, psplit, p_ref)
    _matmul_cols(h, w_ref, psplit, pw, p_ref)

    tabs = (c_ref[...], s1_ref[...], s2_ref[...])
    tn = q * lax.rsqrt(_group_mean(q * q, HEAD_DIM) + EPS) * qg_ref[...]
    qo_ref[...] = (_rope(tn, *tabs, half) * (HEAD_DIM ** -0.5 * LOG2E)).astype(BF16)
    ones = jnp.ones((ONES_ROWS, ts), F32)
    lane = lax.broadcasted_iota(jnp.int32, (1, LANES), 1)
    for br in range(2):
        k = d[:, 2 * br * LANES:(2 * br + 1) * LANES]
        kn = k * lax.rsqrt(_group_mean(k * k, HEAD_DIM) + EPS) * kg_ref[...]
        kr = _rope(kn, *tabs, half)
        swapped = pltpu.roll(kr, HEAD_DIM, 1)
        for g in range(D_KV_GROUPS):
            own = (lane // HEAD_DIM) == g
            ko_ref[br, :, g * LANES:(g + 1) * LANES] = jnp.where(own, kr, swapped).astype(BF16)
        vt = d[:, (2 * br + 1) * LANES:(2 * br + 2) * LANES].T
        parts = []
        for g in range(D_KV_GROUPS):
            parts += [vt[g * HEAD_DIM:(g + 1) * HEAD_DIM, :], ones]
        vt_all = jnp.concatenate(parts, axis=0).astype(BF16)
        for c in range(ts // tk):
            vto_ref[0, br, c] = vt_all[:, c * tk:(c + 1) * tk]
    gates = jax.nn.sigmoid(d[:, 4 * LANES:5 * LANES]).T
    for c in range(ts // tk):
        gto_ref[0, c] = gates[:, c * tk:(c + 1) * tk]


def _layer1_proj(x, ya, yb, ym, w_out, norm_g, w_c, w_d, w_rest, tables, qgain, kgain,
                 conv_w, conv_b, conv_norm, batch, seq, tm, tk):
    n, dm = x.shape
    cwid = conv_w.shape[1]
    tiles_per_seq = seq // tm
    tab_spec = pl.BlockSpec((tm, LANES), lambda i: (i % tiles_per_seq, 0))
    vrows = D_KV_GROUPS * (HEAD_DIM + ONES_ROWS)

    def full(a):
        return pl.BlockSpec(a.shape, lambda i: (0,) * a.ndim)

    def rows(w):
        return pl.BlockSpec((tm, w), lambda i: (i, 0))

    kernel = functools.partial(_layer1_proj_kernel, tiles_per_seq=tiles_per_seq)
    return pl.pallas_call(
        kernel,
        grid=(n // tm,),
        in_specs=[rows(dm), rows(ya.shape[1]), rows(yb.shape[1]), rows(ym.shape[1]), full(w_out),
                  pl.BlockSpec((1, dm), lambda i: (0, 0)), full(w_c), full(w_d), full(w_rest),
                  tab_spec, tab_spec, tab_spec, full(qgain), full(kgain),
                  full(conv_w), pl.BlockSpec((1, cwid), lambda i: (0, 0)),
                  pl.BlockSpec((1, cwid), lambda i: (0, 0))],
        out_specs=[rows(dm), rows(w_rest.shape[1]), rows(cwid), rows(qgain.shape[1]),
                   pl.BlockSpec((2, tm, D_KV_GROUPS * LANES), lambda i: (0, i, 0)),
                   pl.BlockSpec((1, 2, tm // tk, vrows, tk),
                                lambda i: (i // tiles_per_seq, 0, i % tiles_per_seq, 0, 0)),
                   pl.BlockSpec((1, tm // tk, LANES, tk),
                                lambda i: (i // tiles_per_seq, i % tiles_per_seq, 0, 0))],
        out_shape=[jax.ShapeDtypeStruct((n, dm), F32),
                   jax.ShapeDtypeStruct((n, w_rest.shape[1]), F32),
                   jax.ShapeDtypeStruct((n, cwid), BF16),
                   jax.ShapeDtypeStruct((n, qgain.shape[1]), BF16),
                   jax.ShapeDtypeStruct((2, n, D_KV_GROUPS * LANES), BF16),
                   jax.ShapeDtypeStruct((batch, 2, seq // tk, vrows, tk), BF16),
                   jax.ShapeDtypeStruct((batch, seq // tk, LANES, tk), F32)],
        scratch_shapes=[pltpu.VMEM((tk + HALO, cwid), F32),
                        pltpu.VMEM((SUBLANES, tk + HALO, cwid), F32)],
        compiler_params=_cparams(("arbitrary",), 56),
        name="layer1_proj",
    )(x, ya, yb, ym, w_out, norm_g.reshape(1, dm), w_c, w_d, w_rest, *tables, qgain, kgain, conv_w,
      conv_b.reshape(1, cwid), conv_norm.reshape(1, cwid))


def _compress_kernel(tk_ref, tv_ref, pk_ref, pv_ref, w1k_ref, w2k_ref, w1v_ref, w2v_ref,
                     kg_ref, c_ref, s1_ref, s2_ref, kc_ref, vct_ref):
    half = HEAD_DIM // 4 // 2
    groups, ncp = kc_ref.shape[1], kc_ref.shape[2]

    def mlp(t_ref, p_ref, w1_ref, w2_ref):
        a = jnp.zeros((ncp, w1_ref.shape[2]), F32)
        b = jnp.zeros((ncp, w1_ref.shape[2]), F32)
        for l in range(CMP_STRIDE):
            x = t_ref[pl.ds(l, ncp, stride=CMP_STRIDE), :]
            a += jnp.dot((x + p_ref[l:l + 1, :]).astype(BF16), w1_ref[l],
                         preferred_element_type=F32)
            b += jnp.dot((x + p_ref[CMP_STRIDE + l:CMP_STRIDE + l + 1, :]).astype(BF16),
                         w1_ref[CMP_STRIDE + l], preferred_element_type=F32)
        h = a + pltpu.roll(b, ncp - 1, 0)
        return jnp.dot(jax.nn.silu(h).astype(BF16), w2_ref[...], preferred_element_type=F32)

    kc = mlp(tk_ref, pk_ref, w1k_ref, w2k_ref)
    kn = kc * lax.rsqrt(_group_mean(kc * kc, HEAD_DIM) + EPS) * kg_ref[...]
    kr = _rope(kn, c_ref[...], s1_ref[...], s2_ref[...], half).astype(BF16)
    vc = mlp(tv_ref, pv_ref, w1v_ref, w2v_ref)
    for g in range(groups):
        kc_ref[0, g] = kr[:, g * LANES:(g + 1) * LANES]
        vct_ref[0, g] = vc[:, g * LANES:(g + 1) * LANES].T[0:HEAD_DIM, :].astype(BF16)


def _compress(p1, k_blk, pk, pv, w1k, w2k, w1v, w2v, kgain, tables, batch, seq, groups):
    ncp = seq // CMP_STRIDE

    def full(a):
        return pl.BlockSpec(a.shape, lambda b: (0,) * a.ndim)

    return pl.pallas_call(
        _compress_kernel,
        grid=(batch,),
        in_specs=[pl.BlockSpec((seq, LANES), lambda b: (b, k_blk)),
                  pl.BlockSpec((seq, LANES), lambda b: (b, k_blk + 1)),
                  full(pk), full(pv), full(w1k), full(w2k), full(w1v), full(w2v),
                  full(kgain), full(tables[0]), full(tables[1]), full(tables[2])],
        out_specs=[pl.BlockSpec((1, groups, ncp, LANES), lambda b: (b, 0, 0, 0)),
                   pl.BlockSpec((1, groups, HEAD_DIM, ncp), lambda b: (b, 0, 0, 0))],
        out_shape=[jax.ShapeDtypeStruct((batch, groups, ncp, LANES), BF16),
                   jax.ShapeDtypeStruct((batch, groups, HEAD_DIM, ncp), BF16)],
        compiler_params=_cparams(("parallel",), 32),
        name="nsa_compress",
    )(p1, p1, pk, pv, w1k, w2k, w1v, w2v, kgain, *tables)


def _nsa_kernel(qi_tab, tile_tab, br_tab, q_ref, kc_ref, vct_ref, k_ref, vt_ref, gt_ref,
                dg_ref, ovt_ref, o_ref, acc_ref, oc_ref, sel_ref, imp_ref, sta_ref, stb_ref, cst_ref,
                cmax_ref, m_ref, qh_ref, *, tq, qt, nslc, nsel, nbelow, nentries):
    g = pl.program_id(1)
    rheads = D_HEADS // D_KV_GROUPS
    seq = qt * tq
    tiny = float(np.finfo(np.float32).tiny)
    qcol = lax.broadcasted_iota(jnp.int32, (tq, tq), 1)
    krow = lax.broadcasted_iota(jnp.int32, (tq, tq), 0)
    bufs = (sta_ref, stb_ref)
    blocks_per_tile = tq // SLC_BLOCK
    nslab = -(-nslc // SUBLANES)
    sel_rows = nslab * SUBLANES

    def rows(tile):
        return pl.ds(tile * tq if isinstance(tile, int) else pl.multiple_of(tile * tq, tq), tq)

    half_lane = lax.broadcasted_iota(jnp.int32, (1, LANES), 1) // HEAD_DIM

    def split_heads(t, c):
        for r in range(rheads):
            blk = q_ref[rows(t), (r // 2) * LANES:(r // 2 + 1) * LANES]
            qh_ref[rows(t), r * LANES:(r + 1) * LANES] = jnp.where(half_lane == r % 2, blk,
                                                                  jnp.zeros_like(blk))
        return c

    lax.fori_loop(0, qt, split_heads, 0)

    def heads_q(qi):
        return [qh_ref[rows(qi), r * LANES:(r + 1) * LANES] for r in range(rheads)]

    kc = kc_ref[0, 0]
    vct = vct_ref[0, 0]
    ncp = kc.shape[0]
    ovt = ovt_ref[...]

    def select_blocks(qi):
        q = heads_q(qi)
        ncq = min(ncp, -(-(((qi + 1) * tq - CMP_BLOCK) // CMP_STRIDE + 1) // LANES) * LANES)
        cend = lax.broadcasted_iota(jnp.int32, (ncq, tq), 0) * CMP_STRIDE + (CMP_BLOCK - 1)
        cvalid = cend <= qi * tq + lax.broadcasted_iota(jnp.int32, (ncq, tq), 1)
        psum = jnp.zeros((ncq, tq), F32)
        cmax = []
        for r in range(rheads):
            st = lax.dot_general(kc[0:ncq, :], q[r], NT_DIMS, preferred_element_type=F32)
            st = jnp.where(cvalid, st, -jnp.inf)
            cst_ref[r, 0:ncq, :] = st
            cmax.append(jnp.max(st, axis=0, keepdims=True))
        for r in range(rheads):
            m = jnp.where(jnp.isfinite(cmax[r]), cmax[r], 0.0)
            e = jnp.exp2(cst_ref[r, 0:ncq, :] - m)
            p = e / jnp.maximum(jnp.sum(e, axis=0, keepdims=True), tiny)
            psum += p
            oc_ref[qi * rheads + r] = jnp.dot(vct[:, 0:ncq], p.astype(BF16), preferred_element_type=F32)
        hi, lo = _split_bf16(psum)
        imp = (jnp.dot(ovt[:, 0:ncq], hi, preferred_element_type=F32)
               + jnp.dot(ovt[:, 0:ncq], lo, preferred_element_type=F32))
        blk = lax.broadcasted_iota(jnp.int32, (NSLC_PAD, tq), 0)
        cur = (qi * tq + lax.broadcasted_iota(jnp.int32, (NSLC_PAD, tq), 1)) // SLC_BLOCK
        imp = jnp.where(blk > cur, -jnp.inf, imp)
        forced = (blk == 0) | (blk == cur) | (blk == cur - 1)
        imp = jnp.where(forced, jnp.inf, imp)
        imp_ref[...] = imp
        nvis = min(nslc, (qi + 1) * blocks_per_tile)
        vslab = -(-nvis // SUBLANES)
        slabs = [imp[s * SUBLANES:(s + 1) * SUBLANES, :] for s in range(vslab)]
        ranks = [jnp.zeros((SUBLANES, tq), F32) for _ in range(vslab)]
        sub = lax.broadcasted_iota(jnp.int32, (SUBLANES, tq), 0)
        for jp in range(nvis):
            row = imp_ref[jp:jp + 1, :]
            for s in range(vslab):
                if s * SUBLANES > jp:
                    ahead = jnp.where(row >= slabs[s], 1.0, 0.0)
                elif (s + 1) * SUBLANES - 1 < jp:
                    ahead = jnp.where(row > slabs[s], 1.0, 0.0)
                else:
                    ahead = jnp.where(sub > jp - s * SUBLANES, jnp.where(row >= slabs[s], 1.0, 0.0),
                                      jnp.where(row > slabs[s], 1.0, 0.0))
                ranks[s] = ranks[s] + ahead
        for s in range(vslab):
            start = qi * sel_rows + s * SUBLANES
            sel_ref[start:start + SUBLANES, :] = jnp.where(ranks[s] < nsel, 1.0, 0.0)

    for qi in range(qt):
        select_blocks(qi)
    acc_ref[...] = jnp.zeros_like(acc_ref)
    m_ref[...] = jnp.full(m_ref.shape, -jnp.inf, F32)

    def chosen_rows(qi, tile):
        base = qi * sel_rows + tile * blocks_per_tile
        return [sel_ref[pl.ds(base + c, 1), :] for c in range(blocks_per_tile)]

    def scores(qi, tile, br, buf, valid):
        q = heads_q(qi)
        k = k_ref[br, rows(tile), :]
        for r in range(rheads):
            row = buf * rheads + r
            cmax_ref[row:row + 1, :] = _scores_t(k, q[r], valid, bufs[buf], r)

    def below_score_fn(e, buf):
        qi, tile = qi_tab[e], tile_tab[e]
        scores(qi, tile, 0, buf, [c > 0.5 for c in chosen_rows(qi, tile)])

    def edge_score_fn(e, buf):
        qi, tile, br = qi_tab[e], tile_tab[e], br_tab[e]
        chosen = jnp.concatenate([jnp.broadcast_to(c, (SLC_BLOCK, tq)) for c in chosen_rows(qi, tile)],
                                 axis=0)
        dpos = (qi - tile) * tq + (qcol - krow)
        reach = jnp.where(br == 0, seq, WINDOW).astype(jnp.uint32)
        kept = jnp.where(jnp.maximum(chosen, br.astype(F32)) > 0.5, dpos, -1)
        scores(qi, tile, br, buf, kept.astype(jnp.uint32) < reach)

    def accum_fn(e, buf):
        qi, tile, br = qi_tab[e], tile_tab[e], br_tab[e]
        vt = vt_ref[0, br, tile]
        for r in range(rheads):
            cm, sr = buf * rheads + r, (qi * 2 + br) * rheads + r
            m_ref[pl.ds(sr, 1), :] = _accumulate_t(bufs[buf], r, cmax_ref[cm:cm + 1, :],
                                                   m_ref[pl.ds(sr, 1), :], acc_ref, sr, vt, True)

    _pipelined_sweep(nbelow, below_score_fn, accum_fn, unroll=16)
    _pipelined_sweep(nentries - nbelow, lambda e, buf: edge_score_fn(e + nbelow, buf),
                     lambda e, buf: accum_fn(e + nbelow, buf), unroll=16)

    def finish(qi, c):
        outs = []
        for r in range(rheads):
            row0 = (g * rheads + r) * 3
            o = [oc_ref[qi * rheads + r]]
            for br in range(2):
                sr = (qi * 2 + br) * rheads + r
                o.append(acc_ref[sr, 0:HEAD_DIM, :]
                         / jnp.maximum(acc_ref[sr, HEAD_DIM:HEAD_DIM + 1, :], tiny))
            outs.append(sum(gt_ref[0, qi, pl.ds(row0 + j, 1), :] * o[j] for j in range(3)))
        out = jnp.concatenate(outs, axis=0).T
        o_ref[rows(qi), :] = (out * jax.nn.silu(dg_ref[rows(qi), :])).astype(BF16)
        return c

    lax.fori_loop(0, qt, finish, 0)


def _nsa_attn(qe, kc, vct, kboth, vtboth, gt, p1, ovt, batch, seq, tq, dg_blk):
    n = qe.shape[0]
    groups = D_KV_GROUPS
    rheads = D_HEADS // groups
    width = rheads * HEAD_DIM
    qt = seq // tq
    ncp = kc.shape[2]
    nslc = seq // SLC_BLOCK
    sel_rows = -(-nslc // SUBLANES) * SUBLANES
    wtiles = (WINDOW + tq - 1) // tq
    below = [(qi, t, 0) for qi in range(qt) for t in range(qi)]
    edge = []
    for qi in range(qt):
        edge += [(qi, qi, 0)] + [(qi, t, 1) for t in range(max(qi - wtiles, 0), qi + 1)]
    entries = below + edge
    tabs = [jnp.asarray([e[c] for e in entries], jnp.int32) for c in range(3)]
    kernel = functools.partial(_nsa_kernel, tq=tq, qt=qt, nslc=nslc, nsel=min(SLC_TOPK, nslc),
                               nbelow=len(below), nentries=len(entries))
    return pl.pallas_call(
        kernel,
        grid_spec=pltpu.PrefetchScalarGridSpec(
            num_scalar_prefetch=3,
            grid=(batch, groups),
            in_specs=[pl.BlockSpec((seq, width), lambda b, g, *_: (b, g)),
                      pl.BlockSpec((1, 1, ncp, LANES), lambda b, g, *_: (b, g, 0, 0)),
                      pl.BlockSpec((1, 1, HEAD_DIM, ncp), lambda b, g, *_: (b, g, 0, 0)),
                      pl.BlockSpec((2, seq, LANES), lambda b, g, *_: (0, b, g)),
                      pl.BlockSpec((1, 2, qt, HEAD_DIM + ONES_ROWS, tq), lambda b, g, *_: (b, 0, 0, g, 0)),
                      pl.BlockSpec((1, qt, LANES, tq), lambda b, g, *_: (b, 0, 0, 0)),
                      pl.BlockSpec((seq, width), lambda b, g, *_: (b, dg_blk + g)),
                      pl.BlockSpec(ovt.shape, lambda b, g, *_: (0, 0))],
            out_specs=pl.BlockSpec((seq, width), lambda b, g, *_: (b, g)),
            scratch_shapes=[pltpu.VMEM((qt * 2 * rheads, HEAD_DIM + ONES_ROWS, tq), F32),
                            pltpu.VMEM((qt * rheads, HEAD_DIM, tq), F32),
                            pltpu.VMEM((qt * sel_rows, tq), F32),
                            pltpu.VMEM((NSLC_PAD, tq), F32),
                            pltpu.VMEM((rheads, tq, tq), F32),
                            pltpu.VMEM((rheads, tq, tq), F32),
                            pltpu.VMEM((rheads, ncp, tq), F32),
                            pltpu.VMEM((2 * rheads, tq), F32),
                            pltpu.VMEM((qt * 2 * rheads, tq), F32),
                            pltpu.VMEM((seq, rheads * LANES), BF16)]),
        out_shape=jax.ShapeDtypeStruct((n, groups * width), BF16),
        compiler_params=_cparams(("parallel", "parallel"), 56),
        name="nsa_attn",
    )(*tabs, qe, kc, vct, kboth, vtboth, gt, p1, ovt)


def _rope_tables(pos, d):
    rd = d // 4
    half = rd // 2
    inv = ROPE_THETA ** (-jnp.arange(half, dtype=F32) / half)
    ang = pos.astype(F32)[:, None] * inv[None, :]
    cos, sin = jnp.cos(ang), jnp.sin(ang)
    npos = pos.shape[0]
    zeros = jnp.zeros((npos, d), F32)
    c = jnp.concatenate([cos, cos, jnp.ones((npos, d - rd), F32)], axis=1)
    s1 = zeros.at[:, :half].set(-sin)
    s2 = zeros.at[:, half:rd].set(sin)
    return tuple(jnp.tile(t, (1, LANES // d)) for t in (c, s1, s2))


def _overlap_t(ncp, nslc):
    start = np.arange(ncp) * CMP_STRIDE
    s0 = np.arange(nslc) * SLC_BLOCK
    lo = np.maximum(start[:, None], s0[None, :])
    hi = np.minimum(start[:, None] + CMP_BLOCK, s0[None, :] + SLC_BLOCK)
    ov = np.clip(hi - lo, 0, None) / CMP_BLOCK
    ov[ncp - 1] = 0.0
    out = np.zeros((NSLC_PAD, ncp), np.float32)
    out[:nslc] = ov.T
    return jnp.asarray(out, BF16)


def kernel(x, mem, mem_norm, l0_norm, l0_w_in, l0_a_vnorm, l0_a_ws, l0_a_bs, l0_b_qnorm, l0_b_knorm, l0_b_lq1, l0_b_lk1, l0_b_lq2, l0_b_lk2, l0_b_subln, l0_m_wkv, l0_m_qnorm, l0_m_knorm, l0_w_out, l1_norm, l1_w_in, l1_c_conv_w, l1_c_conv_b, l1_c_norm, l1_d_qnorm, l1_d_knorm, l1_d_cmp_pos_k, l1_d_cmp_w1_k, l1_d_cmp_w2_k, l1_d_cmp_pos_v, l1_d_cmp_w1_v, l1_d_cmp_w2_v, l1_m_wkv, l1_m_qnorm, l1_m_knorm, l1_w_out):
    batch, seq, d_model = x.shape
    n = batch * seq
    tq = 256
    mw = M_HEADS * HEAD_DIM
    pos = jnp.arange(seq, dtype=jnp.int32)
    x0 = x.reshape(n, d_model)

    wkv_both = jnp.concatenate([l0_m_wkv, l1_m_wkv], axis=1).astype(BF16)
    kgains = jnp.stack([jnp.tile(l0_m_knorm, M_HEADS), jnp.tile(l1_m_knorm, M_HEADS)])
    mem_kh, mem_vt = _mem_kv(mem, mem_norm, wkv_both, kgains)

    w = l0_w_in
    tab32 = _rope_tables(pos, B_QK_DIM)
    b_width = 2 * B_HEADS * B_QK_DIM
    a_width = l0_a_vnorm.shape[0]
    bs_exp = jnp.repeat(l0_a_bs.T, a_width // A_GROUPS, axis=1)
    p0, y_a, qn, kn, vt = _layer0_proj(
        x0, l0_norm, w[:, 0:1536].astype(BF16), w[:, 1536:3072].astype(BF16), w[:, 3072:4096].astype(BF16),
        tab32, jnp.tile(l0_b_qnorm, b_width // B_QK_DIM).reshape(1, b_width),
        jnp.tile(l0_b_knorm, b_width // B_QK_DIM).reshape(1, b_width),
        l0_a_ws, bs_exp, l0_a_vnorm.reshape(1, a_width), batch, seq, 2 * tq, tq)
    lam_init = 0.8 - 0.6 * math.exp(-0.3 * 1)
    lam_p = jnp.stack([l0_b_lq1, l0_b_lk1, l0_b_lq2, l0_b_lk2])
    y_b = _diff_attn(lam_p, qn, kn, vt, p0, jnp.tile(l0_b_subln, 2).reshape(1, LANES),
                     batch, seq, tq, lam_init, 0)
    y_m = _mem_attn(p0, mem_kh, mem_vt, 0, jnp.tile(l0_m_qnorm, M_HEADS).reshape(1, mw),
                    512 // mw, 768 // mw, seq, 2 * tq)

    w = l1_w_in
    n_bg = 3 * D_HEADS
    w_rest = jnp.concatenate([w[:, 2840:3352], w[:, 2048:2304], w[:, 3352:3864]], axis=1)
    w_d = jnp.concatenate([w[:, 1536:2048], w[:, 2304:2816], w[:, 2816:2840],
                           jnp.zeros((d_model, LANES - n_bg), F32)], axis=1)
    groups, rheads = D_KV_GROUPS, D_HEADS // D_KV_GROUPS
    d_width = D_HEADS * HEAD_DIM
    tab64 = _rope_tables(pos, HEAD_DIM)
    x1, p1, y_c, qe, kboth, vtboth, gt = _layer1_proj(
        x0, y_a, y_b, y_m, l0_w_out.astype(BF16),
        l1_norm, w[:, 0:1536].astype(BF16), w_d.astype(BF16), w_rest.astype(BF16), tab64,
        jnp.tile(l1_d_qnorm, D_HEADS).reshape(1, d_width),
        jnp.tile(l1_d_knorm, LANES // HEAD_DIM).reshape(1, LANES),
        l1_c_conv_w, l1_c_conv_b, l1_c_norm, batch, seq, 2 * tq, tq)
    kv_blk = 512 // LANES

    ncp = seq // CMP_STRIDE
    cmp_pos = jnp.arange(ncp, dtype=jnp.int32) * CMP_STRIDE + (CMP_BLOCK - 1)
    hidden = l1_d_cmp_w2_k.shape[0]

    def both_w1(w1):
        w = w1.reshape(CMP_BLOCK, HEAD_DIM, hidden)
        out = jnp.zeros((CMP_BLOCK, LANES, groups * hidden), F32)
        for g in range(groups):
            out = out.at[:, g * HEAD_DIM:(g + 1) * HEAD_DIM, g * hidden:(g + 1) * hidden].set(w)
        return out.astype(BF16)

    def both_w2(w2):
        out = jnp.zeros((groups * hidden, groups * LANES), F32)
        for g in range(groups):
            out = out.at[g * hidden:(g + 1) * hidden, g * LANES:(g + 1) * LANES].set(w2)
        return out.astype(BF16)

    w2k = jnp.tile(l1_d_cmp_w2_k, (1, LANES // HEAD_DIM))
    w2v = jnp.zeros((hidden, LANES), F32).at[:, :HEAD_DIM].set(l1_d_cmp_w2_v)
    kc, vct = _compress(
        p1, kv_blk, jnp.tile(l1_d_cmp_pos_k, (1, groups)), jnp.tile(l1_d_cmp_pos_v, (1, groups)),
        both_w1(l1_d_cmp_w1_k), both_w2(w2k), both_w1(l1_d_cmp_w1_v), both_w2(w2v),
        jnp.tile(l1_d_knorm, groups * LANES // HEAD_DIM).reshape(1, groups * LANES),
        _rope_tables(cmp_pos, HEAD_DIM), batch, seq, groups)

    nslc = seq // SLC_BLOCK
    y_d = _nsa_attn(qe, kc, vct, kboth, vtboth, gt, p1, _overlap_t(ncp, nslc),
                    batch, seq, tq, 0)
    out = _mem_out(x1, y_c, y_d, p1, mem_kh, mem_vt, 1, jnp.tile(l1_m_qnorm, M_HEADS).reshape(1, mw),
                   768 // mw, 1024 // mw, l1_w_out.astype(BF16), seq, 2 * tq)
    return out.reshape(batch, seq, d_model)
```

```python
import functools
import math

import numpy as np
import jax
import jax.numpy as jnp
from jax import lax
from jax.experimental import pallas as pl
from jax.experimental.pallas import tpu as pltpu

F32 = jnp.float32
BF16 = jnp.bfloat16

EPS = 1e-6
ROPE_THETA = 500000.0
HEAD_DIM = 64
A_CHUNK = 128
A_GROUPS = 4
B_HEADS = 8
B_QK_DIM = 32
C_KERNEL = 31
D_HEADS = 8
D_KV_GROUPS = 2
CMP_BLOCK = 32
CMP_STRIDE = 16
SLC_BLOCK = 64
SLC_TOPK = 16
WINDOW = 512
M_HEADS = 4

LANES = 128
SUBLANES = 8
MXU_DIM = 256
HALO = 32
NSLC_PAD = 128
ONES_ROWS = 16

NT_DIMS = (((1,), (1,)), ((), ()))
LOG2E = math.log2(math.e)


def _cparams(semantics, vmem_mb):
    return pltpu.CompilerParams(dimension_semantics=semantics,
                                vmem_limit_bytes=vmem_mb * 1024 * 1024)


def _split_bf16(x):
    hi = x.astype(BF16)
    lo = (x - hi.astype(F32)).astype(BF16)
    return hi, lo


def _group_mean(x2, gsize):
    t, c = x2.shape
    w = min(c, MXU_DIM)
    r = lax.broadcasted_iota(jnp.int32, (w, w), 0) // gsize
    cc = lax.broadcasted_iota(jnp.int32, (w, w), 1) // gsize
    ones = jnp.where(r == cc, 1.0, 0.0).astype(BF16)
    outs = []
    for s in range(c // w):
        hi, lo = _split_bf16(x2[:, s * w:(s + 1) * w])
        outs.append(jnp.dot(hi, ones, preferred_element_type=F32)
                    + jnp.dot(lo, ones, preferred_element_type=F32))
    out = outs[0] if len(outs) == 1 else jnp.concatenate(outs, axis=1)
    return out * (1.0 / gsize)


def _tile_lanes(t, width):
    rep = width // t.shape[1]
    return t if rep == 1 else jnp.concatenate([t] * rep, axis=1)


def _rope(x, cos_t, s1_t, s2_t, half):
    width = x.shape[1]
    up = pltpu.roll(x, width - half, 1)
    dn = pltpu.roll(x, half, 1)
    return (x * _tile_lanes(cos_t, width) + up * _tile_lanes(s1_t, width)
            + dn * _tile_lanes(s2_t, width))


def _matmul_cols(h, w_ref, lo, hi, out_ref=None):
    r = jnp.dot(h, w_ref[:, lo:hi], preferred_element_type=F32)
    if out_ref is None:
        return r
    out_ref[:, lo:hi] = r


def _rms_rows(x, ng_ref):
    ms = jnp.mean(x * x, axis=-1, keepdims=True)
    return (x * lax.rsqrt(ms + EPS) * ng_ref[...]).astype(BF16)


def _scores_t(k, q, valid, st_ref, idx):
    st = lax.dot_general(k, q, NT_DIMS, preferred_element_type=F32)
    if isinstance(valid, (list, tuple)):
        blk = st.shape[0] // len(valid)
        st = jnp.concatenate([jnp.where(v, st[c * blk:(c + 1) * blk, :], -jnp.inf)
                              for c, v in enumerate(valid)], axis=0)
    elif valid is not None:
        st = jnp.where(valid, st, -jnp.inf)
    st_ref[idx] = st
    return jnp.max(st, axis=0, keepdims=True)


def _accumulate_t(st_ref, idx, cmax, m, acc_ref, aidx, vt, guard_empty):
    m_new = jnp.maximum(m, cmax)
    m_use = jnp.where(m_new == -jnp.inf, 0.0, m_new) if guard_empty else m_new
    alpha = jnp.exp2(m - m_use)
    pt = jnp.exp2((st_ref[idx] - m_use).astype(BF16))
    acc_ref[aidx] = alpha * acc_ref[aidx] + jnp.dot(vt, pt, preferred_element_type=F32)
    return m_new


def _pipelined_sweep(n, score_fn, accum_fn, unroll=4):
    score_fn(0, 0)
    stages = n - 1

    def stage(e, parity):
        score_fn(e + 1, 1 - parity)
        accum_fn(e, parity)

    def body(i, c):
        for k in range(unroll):
            stage(unroll * i + k, k % 2)
        return c

    if isinstance(n, int):
        iters, left = divmod(stages, unroll)
        if iters:
            lax.fori_loop(0, iters, body, 0)
        for k in range(left):
            stage(iters * unroll + k, k % 2)
        accum_fn(n - 1, left % 2)
        return
    lax.fori_loop(0, stages // unroll, body, 0)
    base = stages // unroll * unroll
    left = stages - base
    for k in range(unroll - 1):
        pl.when(left > k)(functools.partial(stage, base + k, k % 2))
    for parity in range(2):
        pl.when(left % 2 == parity)(functools.partial(accum_fn, n - 1, parity))


def _mix_out(ya_ref, yb_ref, ym_ref, w_ref):
    wa = ya_ref.shape[1]
    wb = yb_ref.shape[1]
    acc = jnp.dot(ya_ref[...], w_ref[0:wa, :], preferred_element_type=F32)
    acc += jnp.dot(yb_ref[...], w_ref[wa:wa + wb, :], preferred_element_type=F32)
    acc += jnp.dot(ym_ref[...], w_ref[wa + wb:, :], preferred_element_type=F32)
    return acc


def _sgu_group(u, v, gate, w_ref, bs, vgain, o_ref, cols):
    tm = u.shape[0]
    row = lax.broadcasted_iota(jnp.int32, (A_CHUNK, A_CHUNK), 0)
    col = lax.broadcasted_iota(jnp.int32, (A_CHUNK, A_CHUNK), 1)
    w = jnp.where(col <= row, w_ref, 0.0).astype(BF16)
    gv = jax.nn.gelu(v)
    ms = jnp.mean(gv * gv, axis=-1, keepdims=True)
    vn = (gv * lax.rsqrt(ms + EPS) * vgain).astype(BF16)
    gu = jax.nn.gelu(u) * jax.nn.silu(gate)
    for c in range(tm // A_CHUNK):
        rows = slice(c * A_CHUNK, (c + 1) * A_CHUNK)
        z = jnp.dot(w, vn[rows, :], preferred_element_type=F32) + bs
        o_ref[rows, cols] = (gu[rows, :] * z).astype(BF16)


def _layer0_proj_kernel(x_ref, ng_ref, wa_ref, wqkv_ref, w_ref, c_ref, s1_ref, s2_ref, qg_ref, kg_ref,
                        ws_ref, bs_ref, vg_ref, p_ref, ya_ref, qo_ref, ko_ref, vto_ref):
    h = _rms_rows(x_ref[...], ng_ref)
    aw = vg_ref.shape[1]
    gdim = aw // A_GROUPS
    width = qg_ref.shape[1]
    half = B_QK_DIM // 4 // 2
    pw = p_ref.shape[1]

    a = _matmul_cols(h, wa_ref, 0, 3 * aw)

    def sgu(g):
        cols = slice(g * gdim, (g + 1) * gdim)
        _sgu_group(a[:, cols], a[:, aw + g * gdim:aw + (g + 1) * gdim],
                   a[:, 2 * aw + g * gdim:2 * aw + (g + 1) * gdim],
                   ws_ref[g], bs_ref[:, cols], vg_ref[:, cols], ya_ref, cols)

    def prep(t, gain, dst, mul):
        tn = t * lax.rsqrt(_group_mean(t * t, B_QK_DIM) + EPS) * gain[...]
        tr = _rope(tn, c_ref[...], s1_ref[...], s2_ref[...], half)
        dst[...] = (tr * mul if mul != 1.0 else tr).astype(BF16)

    q = _matmul_cols(h, wqkv_ref, 0, width)
    sgu(0)
    k = _matmul_cols(h, wqkv_ref, width, 2 * width)
    sgu(1)
    v = _matmul_cols(h, wqkv_ref, 2 * width, 3 * width)
    sgu(2)
    _matmul_cols(h, w_ref, 0, pw // 2, p_ref)
    sgu(3)
    prep(q, qg_ref, qo_ref, B_QK_DIM ** -0.5 * LOG2E)
    _matmul_cols(h, w_ref, pw // 2, pw, p_ref)
    prep(k, kg_ref, ko_ref, 1.0)
    vt = v.T
    dv = 2 * B_QK_DIM
    ones = jnp.ones((ONES_ROWS, vt.shape[1]), F32)
    parts = []
    for hd in range(B_HEADS):
        parts += [vt[hd * dv:(hd + 1) * dv, :], ones]
    vt_all = jnp.concatenate(parts, axis=0).astype(BF16)
    tk = vto_ref.shape[3]
    for c in range(vto_ref.shape[1]):
        vto_ref[0, c] = vt_all[:, c * tk:(c + 1) * tk]


def _layer0_proj(x, norm_g, w_a, w_qkv, w_rest, tables, qgain, kgain, a_ws, bs_exp, vgain, batch, seq, tm, tk):
    n, d = x.shape
    width = qgain.shape[1]
    aw = vgain.shape[1]
    tiles_per_seq = seq // tm
    tab_spec = pl.BlockSpec((tm, LANES), lambda i: (i % tiles_per_seq, 0))
    vrows = B_HEADS * (2 * B_QK_DIM + ONES_ROWS)

    def full(a):
        return pl.BlockSpec(a.shape, lambda i: (0,) * a.ndim)

    def rows(w):
        return pl.BlockSpec((tm, w), lambda i: (i, 0))

    return pl.pallas_call(
        _layer0_proj_kernel,
        grid=(n // tm,),
        in_specs=[rows(d), pl.BlockSpec((1, d), lambda i: (0, 0)), full(w_a), full(w_qkv), full(w_rest),
                  tab_spec, tab_spec, tab_spec, full(qgain), full(kgain),
                  full(a_ws), full(bs_exp), full(vgain)],
        out_specs=[rows(w_rest.shape[1]), rows(aw), rows(width), rows(width),
                   pl.BlockSpec((1, tm // tk, vrows, tk),
                                lambda i: (i // tiles_per_seq, i % tiles_per_seq, 0, 0))],
        out_shape=[jax.ShapeDtypeStruct((n, w_rest.shape[1]), F32),
                   jax.ShapeDtypeStruct((n, aw), BF16),
                   jax.ShapeDtypeStruct((n, width), BF16), jax.ShapeDtypeStruct((n, width), BF16),
                   jax.ShapeDtypeStruct((batch, seq // tk, vrows, tk), BF16)],
        compiler_params=_cparams(("parallel",), 52),
        name="layer0_proj",
    )(x, norm_g.reshape(1, d), w_a, w_qkv, w_rest, *tables, qgain, kgain, a_ws, bs_exp, vgain)


def _diff_attn_kernel(qi_tab, tile_tab, lam_ref, q_ref, k_ref, vt_ref, g_ref, sub_ref, o_ref,
                      acc_ref, sta_ref, stb_ref, cmax_ref, m_ref, *, tq, qt, lam_init):
    dv = 2 * B_QK_DIM
    vrows = dv + ONES_ROWS
    nmap = 4
    lane = lax.broadcasted_iota(jnp.int32, (1, LANES), 1)
    bufs = (sta_ref, stb_ref)
    acc_ref[...] = jnp.zeros_like(acc_ref)
    m_ref[...] = jnp.full(m_ref.shape, -jnp.inf, F32)
    krow = lax.broadcasted_iota(jnp.int32, (tq, tq), 0)
    qcol = lax.broadcasted_iota(jnp.int32, (tq, tq), 1)
    lam_p = lam_ref[...]
    lam = (jnp.exp(jnp.sum(lam_p[0:1] * lam_p[1:2], axis=-1, keepdims=True))
           - jnp.exp(jnp.sum(lam_p[2:3] * lam_p[3:4], axis=-1, keepdims=True)) + lam_init)

    def rows(tile):
        return pl.ds(pl.multiple_of(tile * tq, tq), tq)

    def scores(qi, tile, buf, valid):
        q = q_ref[rows(qi), :]
        k = k_ref[rows(tile), :]
        zero = jnp.zeros_like(q)
        for j in range(nmap):
            qm = jnp.where(lane // B_QK_DIM == j, q, zero)
            row = buf * nmap + j
            cmax_ref[row:row + 1, :] = _scores_t(k, qm, valid, bufs[buf], j)

    def accumulate(qi, tile, buf):
        for j in range(nmap):
            vt = vt_ref[0, tile, (j // 2) * vrows:(j // 2 + 1) * vrows, :]
            cm, sr = buf * nmap + j, qi * nmap + j
            m_ref[pl.ds(sr, 1), :] = _accumulate_t(bufs[buf], j, cmax_ref[cm:cm + 1, :],
                                                   m_ref[pl.ds(sr, 1), :], acc_ref, sr, vt, False)

    def finish(qi):
        halves = []
        for h in range(2):
            o = [acc_ref[qi * nmap + 2 * h + mp, 0:dv, :] / acc_ref[qi * nmap + 2 * h + mp, dv:dv + 1, :]
                 for mp in range(2)]
            pd = o[0] - lam * o[1]
            ms2 = jnp.mean(pd * pd, axis=0, keepdims=True)
            halves.append(pd * lax.rsqrt(ms2 + EPS))
        ob = jnp.concatenate(halves, axis=0).T * sub_ref[...] * (1.0 - lam_init)
        o_ref[rows(qi), :] = (ob * jax.nn.silu(g_ref[rows(qi), :])).astype(BF16)

    def below_score(e, buf):
        scores(qi_tab[e], tile_tab[e], buf, None)

    def below_accum(e, buf):
        accumulate(qi_tab[e], tile_tab[e], buf)

    def diag_score(e, buf):
        scores(e, e, buf, krow <= qcol)

    def diag_accum(e, buf):
        accumulate(e, e, buf)
        finish(e)

    _pipelined_sweep(qt * (qt - 1) // 2, below_score, below_accum, unroll=16)
    _pipelined_sweep(qt, diag_score, diag_accum, unroll=16)


def _diff_attn(lam_p, qn, kn, vt, p0, subln, batch, seq, tq, lam_init, g_off):
    n, width = qn.shape
    pairs = width // LANES
    qt = seq // tq
    vrows = 2 * B_QK_DIM + ONES_ROWS
    below = [(qi, t) for qi in range(qt) for t in range(qi)]
    qi_tab = jnp.asarray([e[0] for e in below], jnp.int32)
    tile_tab = jnp.asarray([e[1] for e in below], jnp.int32)
    kernel = functools.partial(_diff_attn_kernel, tq=tq, qt=qt, lam_init=lam_init)
    seq_spec = pl.BlockSpec((seq, LANES), lambda b, p, *_: (b, p))
    return pl.pallas_call(
        kernel,
        grid_spec=pltpu.PrefetchScalarGridSpec(
            num_scalar_prefetch=2,
            grid=(batch, pairs),
            in_specs=[pl.BlockSpec(lam_p.shape, lambda b, p, *_: (0, 0)),
                      seq_spec, seq_spec,
                      pl.BlockSpec((1, qt, 2 * vrows, tq), lambda b, p, *_: (b, 0, p, 0)),
                      pl.BlockSpec((seq, LANES), lambda b, p, *_: (b, g_off + p)),
                      pl.BlockSpec((1, LANES), lambda b, p, *_: (0, 0))],
            out_specs=seq_spec,
            scratch_shapes=[pltpu.VMEM((qt * 4, vrows, tq), F32), pltpu.VMEM((4, tq, tq), F32),
                            pltpu.VMEM((4, tq, tq), F32), pltpu.VMEM((8, tq), F32),
                            pltpu.VMEM((qt * 4, tq), F32)]),
        out_shape=jax.ShapeDtypeStruct((n, width), BF16),
        compiler_params=_cparams(("parallel", "parallel"), 40),
        name="diff_attn",
    )(qi_tab, tile_tab, lam_p, qn, kn, vt, p0, subln)


def _mem_kv_kernel(mem_ref, g_ref, w_ref, kg_ref, kh_ref, vt_ref):
    x = mem_ref[0]
    ms = jnp.mean(x * x, axis=-1, keepdims=True)
    h = (x * lax.rsqrt(ms + EPS) * g_ref[...]).astype(BF16)
    kv = jnp.dot(h, w_ref[...], preferred_element_type=F32)
    mw = kv.shape[1] // 4
    lane = lax.broadcasted_iota(jnp.int32, (1, mw), 1)
    ones = jnp.ones((ONES_ROWS, kv.shape[0]), F32)
    for layer in range(2):
        k = kv[:, (2 * layer) * mw:(2 * layer + 1) * mw]
        kn = k * lax.rsqrt(_group_mean(k * k, HEAD_DIM) + EPS) * kg_ref[layer:layer + 1, :]
        vt = kv[:, (2 * layer + 1) * mw:(2 * layer + 2) * mw].T
        parts = []
        for hd in range(M_HEADS):
            kh_ref[layer, 0, hd] = jnp.where(lane // HEAD_DIM == hd, kn, 0.0).astype(BF16)
            parts += [vt[hd * HEAD_DIM:(hd + 1) * HEAD_DIM, :], ones]
        vt_ref[layer, 0] = jnp.concatenate(parts, axis=0).astype(BF16)


def _mem_kv(mem, mem_norm, wkv_both, kgains):
    batch, mtok, d = mem.shape
    mw = wkv_both.shape[1] // 4
    vrows = M_HEADS * (HEAD_DIM + ONES_ROWS)
    return pl.pallas_call(
        _mem_kv_kernel,
        grid=(batch,),
        in_specs=[pl.BlockSpec((1, mtok, d), lambda b: (b, 0, 0)),
                  pl.BlockSpec((1, d), lambda b: (0, 0)),
                  pl.BlockSpec(wkv_both.shape, lambda b: (0, 0)),
                  pl.BlockSpec(kgains.shape, lambda b: (0, 0))],
        out_specs=[pl.BlockSpec((2, 1, M_HEADS, mtok, mw), lambda b: (0, b, 0, 0, 0)),
                   pl.BlockSpec((2, 1, vrows, mtok), lambda b: (0, b, 0, 0))],
        out_shape=[jax.ShapeDtypeStruct((2, batch, M_HEADS, mtok, mw), BF16),
                   jax.ShapeDtypeStruct((2, batch, vrows, mtok), BF16)],
        compiler_params=_cparams(("parallel",), 32),
        name="mem_kv",
    )(mem, mem_norm.reshape(1, d), wkv_both, kgains)


def _mem_attn_tile(q_ref, g_ref, kh_ref, vt_ref, qg_ref, st_ref):
    vrows = HEAD_DIM + ONES_ROWS
    x = q_ref[...]
    qn = (x * lax.rsqrt(_group_mean(x * x, HEAD_DIM) + EPS) * qg_ref[...]
          * (HEAD_DIM ** -0.5 * LOG2E)).astype(BF16)
    cmax = [_scores_t(kh_ref[0, 0, h], qn, None, st_ref, h) for h in range(M_HEADS)]
    outs = []
    for h in range(M_HEADS):
        pt = jnp.exp2((st_ref[h] - cmax[h]).astype(BF16))
        acc = jnp.dot(vt_ref[0, 0, h * vrows:(h + 1) * vrows, :], pt, preferred_element_type=F32)
        outs.append(acc[0:HEAD_DIM, :] / acc[HEAD_DIM:HEAD_DIM + 1, :])
    out = jnp.concatenate(outs, axis=0).T
    return (out * jax.nn.silu(g_ref[...])).astype(BF16)


def _mem_attn_kernel(q_ref, g_ref, kh_ref, vt_ref, qg_ref, o_ref, st_ref):
    o_ref[...] = _mem_attn_tile(q_ref, g_ref, kh_ref, vt_ref, qg_ref, st_ref)


def _mem_out_kernel(x_ref, ya_ref, yb_ref, q_ref, g_ref, kh_ref, vt_ref, qg_ref, w_ref, o_ref, st_ref):
    ym = _mem_attn_tile(q_ref, g_ref, kh_ref, vt_ref, qg_ref, st_ref)
    wa = ya_ref.shape[1]
    wb = yb_ref.shape[1]
    acc = jnp.dot(ya_ref[...], w_ref[0:wa, :], preferred_element_type=F32)
    acc += jnp.dot(yb_ref[...], w_ref[wa:wa + wb, :], preferred_element_type=F32)
    acc += jnp.dot(ym, w_ref[wa + wb:, :], preferred_element_type=F32)
    o_ref[...] = x_ref[...] + acc


def _mem_out(x, ya, yb, p, kh, vt, layer, qgain, q_blk, g_blk, w, seq, tq):
    n, d = x.shape
    _, batch, heads, mtok, mw = kh.shape
    qt = seq // tq

    def rows(width, blk=0):
        return pl.BlockSpec((tq, width), lambda i: (i, blk))

    return pl.pallas_call(
        _mem_out_kernel,
        grid=(n // tq,),
        in_specs=[rows(d), rows(ya.shape[1]), rows(yb.shape[1]), rows(mw, q_blk), rows(mw, g_blk),
                  pl.BlockSpec((1, 1, heads, mtok, mw), lambda i: (layer, i // qt, 0, 0, 0)),
                  pl.BlockSpec((1, 1, vt.shape[2], mtok), lambda i: (layer, i // qt, 0, 0)),
                  pl.BlockSpec((1, mw), lambda i: (0, 0)),
                  pl.BlockSpec(w.shape, lambda i: (0, 0))],
        out_specs=rows(d),
        out_shape=jax.ShapeDtypeStruct((n, d), F32),
        scratch_shapes=[pltpu.VMEM((heads, mtok, tq), F32)],
        compiler_params=_cparams(("parallel",), 40),
        name="mem_out_proj",
    )(x, ya, yb, p, p, kh, vt, qgain, w)


def _mem_attn(p, kh, vt, layer, qgain, q_blk, g_blk, seq, tq):
    n = p.shape[0]
    _, batch, heads, mtok, mw = kh.shape
    qt = seq // tq
    return pl.pallas_call(
        _mem_attn_kernel,
        grid=(n // tq,),
        in_specs=[pl.BlockSpec((tq, mw), lambda i: (i, q_blk)),
                  pl.BlockSpec((tq, mw), lambda i: (i, g_blk)),
                  pl.BlockSpec((1, 1, heads, mtok, mw), lambda i: (layer, i // qt, 0, 0, 0)),
                  pl.BlockSpec((1, 1, vt.shape[2], mtok), lambda i: (layer, i // qt, 0, 0)),
                  pl.BlockSpec((1, mw), lambda i: (0, 0))],
        out_specs=pl.BlockSpec((tq, mw), lambda i: (i, 0)),
        out_shape=jax.ShapeDtypeStruct((n, mw), BF16),
        scratch_shapes=[pltpu.VMEM((heads, mtok, tq), F32)],
        compiler_params=_cparams(("parallel",), 32),
        name="mem_attn",
    )(p, p, kh, vt, qgain)


def _layer1_proj_kernel(x_ref, ya_ref, yb_ref, ym_ref, wo_ref, ng_ref, wc_ref, wd_ref, w_ref,
                        c_ref, s1_ref, s2_ref, qg_ref, kg_ref, cw_ref, cb_ref, cn_ref,
                        x1_ref, p_ref, yc_ref, qo_ref, ko_ref, vto_ref, gto_ref,
                        hbuf_ref, shift_ref, *, tiles_per_seq):
    i = pl.program_id(0)
    ts = x_ref.shape[0]
    cs = hbuf_ref.shape[0] - HALO
    span = cs + HALO
    cwid = cn_ref.shape[1]
    qw = qg_ref.shape[1]
    pw = p_ref.shape[1]
    half = HEAD_DIM // 4 // 2
    tk = gto_ref.shape[3]
    x1 = x_ref[...] + _mix_out(ya_ref, yb_ref, ym_ref, wo_ref)
    x1_ref[...] = x1
    h = _rms_rows(x1, ng_ref)

    @pl.when(i % tiles_per_seq == 0)
    def _():
        hbuf_ref[0:HALO, :] = jnp.zeros((HALO, cwid), F32)

    ca = _matmul_cols(h, wc_ref, 0, cwid)
    cbv = _matmul_cols(h, wc_ref, cwid, 2 * cwid)
    q = _matmul_cols(h, wd_ref, 0, qw)
    glu = ca * jax.nn.sigmoid(cbv)
    first = HALO - (C_KERNEL - 1)

    def stage_rows(r0):
        hbuf_ref[HALO:HALO + cs, :] = glu[r0:r0 + cs, :]
        for ph in range(SUBLANES):
            shift_ref[ph, 0:span - ph, :] = hbuf_ref[ph:span, :]

    def conv_rows(r0, gate):
        y = cb_ref[...]
        for j in range(C_KERNEL):
            ph, base = (first + j) % SUBLANES, (first + j) // SUBLANES * SUBLANES
            y = y + shift_ref[ph, base:base + cs, :] * cw_ref[j:j + 1, :]
        ms = jnp.mean(y * y, axis=-1, keepdims=True)
        yn = y * lax.rsqrt(ms + EPS) * cn_ref[...]
        yc_ref[r0:r0 + cs, :] = (jax.nn.silu(yn) * jax.nn.silu(gate[r0:r0 + cs, :])).astype(BF16)
        hbuf_ref[0:HALO, :] = hbuf_ref[cs:cs + HALO, :]

    stage_rows(0)
    cg = _matmul_cols(h, wc_ref, 2 * cwid, 3 * cwid)
    d = _matmul_cols(h, wd_ref, qw, wd_ref.shape[1])
    psplit = -(-pw // (2 * MXU_DIM)) * MXU_DIM
    for blk in range(ts // cs):
        if blk:
            stage_rows(blk * cs)
        conv_rows(blk * cs, cg)
        if blk == 0:
            _matmul_cols(h, w_ref, 0, psplit, p_ref)
    _matmul_cols(h, w_ref, psplit, pw, p_ref)

    tabs = (c_ref[...], s1_ref[...], s2_ref[...])
    tn = q * lax.rsqrt(_group_mean(q * q, HEAD_DIM) + EPS) * qg_ref[...]
    qo_ref[...] = (_rope(tn, *tabs, half) * (HEAD_DIM ** -0.5 * LOG2E)).astype(BF16)
    ones = jnp.ones((ONES_ROWS, ts), F32)
    lane = lax.broadcasted_iota(jnp.int32, (1, LANES), 1)
    for br in range(2):
        k = d[:, 2 * br * LANES:(2 * br + 1) * LANES]
        kn = k * lax.rsqrt(_group_mean(k * k, HEAD_DIM) + EPS) * kg_ref[...]
        kr = _rope(kn, *tabs, half)
        swapped = pltpu.roll(kr, HEAD_DIM, 1)
        for g in range(D_KV_GROUPS):
            own = (lane // HEAD_DIM) == g
            ko_ref[br, :, g * LANES:(g + 1) * LANES] = jnp.where(own, kr, swapped).astype(BF16)
        vt = d[:, (2 * br + 1) * LANES:(2 * br + 2) * LANES].T
        parts = []
        for g in range(D_KV_GROUPS):
            parts += [vt[g * HEAD_DIM:(g + 1) * HEAD_DIM, :], ones]
        vt_all = jnp.concatenate(parts, axis=0).astype(BF16)
        for c in range(ts // tk):
            vto_ref[0, br, c] = vt_all[:, c * tk:(c + 1) * tk]
    gates = jax.nn.sigmoid(d[:, 4 * LANES:5 * LANES]).T
    for c in range(ts // tk):
        gto_ref[0, c] = gates[:, c * tk:(c + 1) * tk]


def _layer1_proj(x, ya, yb, ym, w_out, norm_g, w_c, w_d, w_rest, tables, qgain, kgain,
                 conv_w, conv_b, conv_norm, batch, seq, tm, tk):
    n, dm = x.shape
    cwid = conv_w.shape[1]
    tiles_per_seq = seq // tm
    tab_spec = pl.BlockSpec((tm, LANES), lambda i: (i % tiles_per_seq, 0))
    vrows = D_KV_GROUPS * (HEAD_DIM + ONES_ROWS)

    def full(a):
        return pl.BlockSpec(a.shape, lambda i: (0,) * a.ndim)

    def rows(w):
        return pl.BlockSpec((tm, w), lambda i: (i, 0))

    kernel = functools.partial(_layer1_proj_kernel, tiles_per_seq=tiles_per_seq)
    return pl.pallas_call(
        kernel,
        grid=(n // tm,),
        in_specs=[rows(dm), rows(ya.shape[1]), rows(yb.shape[1]), rows(ym.shape[1]), full(w_out),
                  pl.BlockSpec((1, dm), lambda i: (0, 0)), full(w_c), full(w_d), full(w_rest),
                  tab_spec, tab_spec, tab_spec, full(qgain), full(kgain),
                  full(conv_w), pl.BlockSpec((1, cwid), lambda i: (0, 0)),
                  pl.BlockSpec((1, cwid), lambda i: (0, 0))],
        out_specs=[rows(dm), rows(w_rest.shape[1]), rows(cwid), rows(qgain.shape[1]),
                   pl.BlockSpec((2, tm, D_KV_GROUPS * LANES), lambda i: (0, i, 0)),
                   pl.BlockSpec((1, 2, tm // tk, vrows, tk),
                                lambda i: (i // tiles_per_seq, 0, i % tiles_per_seq, 0, 0)),
                   pl.BlockSpec((1, tm // tk, LANES, tk),
                                lambda i: (i // tiles_per_seq, i % tiles_per_seq, 0, 0))],
        out_shape=[jax.ShapeDtypeStruct((n, dm), F32),
                   jax.ShapeDtypeStruct((n, w_rest.shape[1]), F32),
                   jax.ShapeDtypeStruct((n, cwid), BF16),
                   jax.ShapeDtypeStruct((n, qgain.shape[1]), BF16),
                   jax.ShapeDtypeStruct((2, n, D_KV_GROUPS * LANES), BF16),
                   jax.ShapeDtypeStruct((batch, 2, seq // tk, vrows, tk), BF16),
                   jax.ShapeDtypeStruct((batch, seq // tk, LANES, tk), F32)],
        scratch_shapes=[pltpu.VMEM((tk + HALO, cwid), F32),
                        pltpu.VMEM((SUBLANES, tk + HALO, cwid), F32)],
        compiler_params=_cparams(("arbitrary",), 56),
        name="layer1_proj",
    )(x, ya, yb, ym, w_out, norm_g.reshape(1, dm), w_c, w_d, w_rest, *tables, qgain, kgain, conv_w,
      conv_b.reshape(1, cwid), conv_norm.reshape(1, cwid))


def _compress_kernel(tk_ref, tv_ref, pk_ref, pv_ref, w1k_ref, w2k_ref, w1v_ref, w2v_ref,
                     kg_ref, c_ref, s1_ref, s2_ref, kc_ref, vct_ref):
    half = HEAD_DIM // 4 // 2
    groups, ncp = kc_ref.shape[1], kc_ref.shape[2]

    def mlp(t_ref, p_ref, w1_ref, w2_ref):
        a = jnp.zeros((ncp, w1_ref.shape[2]), F32)
        b = jnp.zeros((ncp, w1_ref.shape[2]), F32)
        for l in range(CMP_STRIDE):
            x = t_ref[pl.ds(l, ncp, stride=CMP_STRIDE), :]
            a += jnp.dot((x + p_ref[l:l + 1, :]).astype(BF16), w1_ref[l],
                         preferred_element_type=F32)
            b += jnp.dot((x + p_ref[CMP_STRIDE + l:CMP_STRIDE + l + 1, :]).astype(BF16),
                         w1_ref[CMP_STRIDE + l], preferred_element_type=F32)
        h = a + pltpu.roll(b, ncp - 1, 0)
        return jnp.dot(jax.nn.silu(h).astype(BF16), w2_ref[...], preferred_element_type=F32)

    kc = mlp(tk_ref, pk_ref, w1k_ref, w2k_ref)
    kn = kc * lax.rsqrt(_group_mean(kc * kc, HEAD_DIM) + EPS) * kg_ref[...]
    kr = _rope(kn, c_ref[...], s1_ref[...], s2_ref[...], half).astype(BF16)
    vc = mlp(tv_ref, pv_ref, w1v_ref, w2v_ref)
    for g in range(groups):
        kc_ref[0, g] = kr[:, g * LANES:(g + 1) * LANES]
        vct_ref[0, g] = vc[:, g * LANES:(g + 1) * LANES].T[0:HEAD_DIM, :].astype(BF16)


def _compress(p1, k_blk, pk, pv, w1k, w2k, w1v, w2v, kgain, tables, batch, seq, groups):
    ncp = seq // CMP_STRIDE

    def full(a):
        return pl.BlockSpec(a.shape, lambda b: (0,) * a.ndim)

    return pl.pallas_call(
        _compress_kernel,
        grid=(batch,),
        in_specs=[pl.BlockSpec((seq, LANES), lambda b: (b, k_blk)),
                  pl.BlockSpec((seq, LANES), lambda b: (b, k_blk + 1)),
                  full(pk), full(pv), full(w1k), full(w2k), full(w1v), full(w2v),
                  full(kgain), full(tables[0]), full(tables[1]), full(tables[2])],
        out_specs=[pl.BlockSpec((1, groups, ncp, LANES), lambda b: (b, 0, 0, 0)),
                   pl.BlockSpec((1, groups, HEAD_DIM, ncp), lambda b: (b, 0, 0, 0))],
        out_shape=[jax.ShapeDtypeStruct((batch, groups, ncp, LANES), BF16),
                   jax.ShapeDtypeStruct((batch, groups, HEAD_DIM, ncp), BF16)],
        compiler_params=_cparams(("parallel",), 32),
        name="nsa_compress",
    )(p1, p1, pk, pv, w1k, w2k, w1v, w2v, kgain, *tables)


def _nsa_kernel(qi_tab, tile_tab, br_tab, q_ref, kc_ref, vct_ref, k_ref, vt_ref, gt_ref,
                dg_ref, ovt_ref, o_ref, acc_ref, oc_ref, sel_ref, imp_ref, sta_ref, stb_ref, cst_ref,
                cmax_ref, m_ref, qh_ref, *, tq, qt, nslc, nsel, nbelow, nentries):
    g = pl.program_id(1)
    rheads = D_HEADS // D_KV_GROUPS
    seq = qt * tq
    tiny = float(np.finfo(np.float32).tiny)
    qcol = lax.broadcasted_iota(jnp.int32, (tq, tq), 1)
    krow = lax.broadcasted_iota(jnp.int32, (tq, tq), 0)
    bufs = (sta_ref, stb_ref)
    blocks_per_tile = tq // SLC_BLOCK
    nslab = -(-nslc // SUBLANES)
    sel_rows = nslab * SUBLANES

    def rows(tile):
        return pl.ds(tile * tq if isinstance(tile, int) else pl.multiple_of(tile * tq, tq), tq)

    half_lane = lax.broadcasted_iota(jnp.int32, (1, LANES), 1) // HEAD_DIM

    def split_heads(t, c):
        for r in range(rheads):
            blk = q_ref[rows(t), (r // 2) * LANES:(r // 2 + 1) * LANES]
            qh_ref[rows(t), r * LANES:(r + 1) * LANES] = jnp.where(half_lane == r % 2, blk,
                                                                  jnp.zeros_like(blk))
        return c

    lax.fori_loop(0, qt, split_heads, 0)

    def heads_q(qi):
        return [qh_ref[rows(qi), r * LANES:(r + 1) * LANES] for r in range(rheads)]

    kc = kc_ref[0, 0]
    vct = vct_ref[0, 0]
    ncp = kc.shape[0]
    ovt = ovt_ref[...]

    def select_blocks(qi):
        q = heads_q(qi)
        ncq = min(ncp, -(-(((qi + 1) * tq - CMP_BLOCK) // CMP_STRIDE + 1) // LANES) * LANES)
        cend = lax.broadcasted_iota(jnp.int32, (ncq, tq), 0) * CMP_STRIDE + (CMP_BLOCK - 1)
        cvalid = cend <= qi * tq + lax.broadcasted_iota(jnp.int32, (ncq, tq), 1)
        psum = jnp.zeros((ncq, tq), F32)
        cmax = []
        for r in range(rheads):
            st = lax.dot_general(kc[0:ncq, :], q[r], NT_DIMS, preferred_element_type=F32)
            st = jnp.where(cvalid, st, -jnp.inf)
            cst_ref[r, 0:ncq, :] = st
            cmax.append(jnp.max(st, axis=0, keepdims=True))
        for r in range(rheads):
            m = jnp.where(jnp.isfinite(cmax[r]), cmax[r], 0.0)
            e = jnp.exp2(cst_ref[r, 0:ncq, :] - m)
            p = e / jnp.maximum(jnp.sum(e, axis=0, keepdims=True), tiny)
            psum += p
            oc_ref[qi * rheads + r] = jnp.dot(vct[:, 0:ncq], p.astype(BF16), preferred_element_type=F32)
        hi, lo = _split_bf16(psum)
        imp = (jnp.dot(ovt[:, 0:ncq], hi, preferred_element_type=F32)
               + jnp.dot(ovt[:, 0:ncq], lo, preferred_element_type=F32))
        blk = lax.broadcasted_iota(jnp.int32, (NSLC_PAD, tq), 0)
        cur = (qi * tq + lax.broadcasted_iota(jnp.int32, (NSLC_PAD, tq), 1)) // SLC_BLOCK
        imp = jnp.where(blk > cur, -jnp.inf, imp)
        forced = (blk == 0) | (blk == cur) | (blk == cur - 1)
        imp = jnp.where(forced, jnp.inf, imp)
        imp_ref[...] = imp
        nvis = min(nslc, (qi + 1) * blocks_per_tile)
        vslab = -(-nvis // SUBLANES)
        slabs = [imp[s * SUBLANES:(s + 1) * SUBLANES, :] for s in range(vslab)]
        ranks = [jnp.zeros((SUBLANES, tq), F32) for _ in range(vslab)]
        sub = lax.broadcasted_iota(jnp.int32, (SUBLANES, tq), 0)
        for jp in range(nvis):
            row = imp_ref[jp:jp + 1, :]
            for s in range(vslab):
                if s * SUBLANES > jp:
                    ahead = jnp.where(row >= slabs[s], 1.0, 0.0)
                elif (s + 1) * SUBLANES - 1 < jp:
                    ahead = jnp.where(row > slabs[s], 1.0, 0.0)
                else:
                    ahead = jnp.where(sub > jp - s * SUBLANES, jnp.where(row >= slabs[s], 1.0, 0.0),
                                      jnp.where(row > slabs[s], 1.0, 0.0))
                ranks[s] = ranks[s] + ahead
        for s in range(vslab):
            start = qi * sel_rows + s * SUBLANES
            sel_ref[start:start + SUBLANES, :] = jnp.where(ranks[s] < nsel, 1.0, 0.0)

    for qi in range(qt):
        select_blocks(qi)
    acc_ref[...] = jnp.zeros_like(acc_ref)
    m_ref[...] = jnp.full(m_ref.shape, -jnp.inf, F32)

    def chosen_rows(qi, tile):
        base = qi * sel_rows + tile * blocks_per_tile
        return [sel_ref[pl.ds(base + c, 1), :] for c in range(blocks_per_tile)]

    def scores(qi, tile, br, buf, valid):
        q = heads_q(qi)
        k = k_ref[br, rows(tile), :]
        for r in range(rheads):
            row = buf * rheads + r
            cmax_ref[row:row + 1, :] = _scores_t(k, q[r], valid, bufs[buf], r)

    def below_score_fn(e, buf):
        qi, tile = qi_tab[e], tile_tab[e]
        scores(qi, tile, 0, buf, [c > 0.5 for c in chosen_rows(qi, tile)])

    def edge_score_fn(e, buf):
        qi, tile, br = qi_tab[e], tile_tab[e], br_tab[e]
        chosen = jnp.concatenate([jnp.broadcast_to(c, (SLC_BLOCK, tq)) for c in chosen_rows(qi, tile)],
                                 axis=0)
        dpos = (qi - tile) * tq + (qcol - krow)
        reach = jnp.where(br == 0, seq, WINDOW).astype(jnp.uint32)
        kept = jnp.where(jnp.maximum(chosen, br.astype(F32)) > 0.5, dpos, -1)
        scores(qi, tile, br, buf, kept.astype(jnp.uint32) < reach)

    def accum_fn(e, buf):
        qi, tile, br = qi_tab[e], tile_tab[e], br_tab[e]
        vt = vt_ref[0, br, tile]
        for r in range(rheads):
            cm, sr = buf * rheads + r, (qi * 2 + br) * rheads + r
            m_ref[pl.ds(sr, 1), :] = _accumulate_t(bufs[buf], r, cmax_ref[cm:cm + 1, :],
                                                   m_ref[pl.ds(sr, 1), :], acc_ref, sr, vt, True)

    _pipelined_sweep(nbelow, below_score_fn, accum_fn, unroll=16)
    _pipelined_sweep(nentries - nbelow, lambda e, buf: edge_score_fn(e + nbelow, buf),
                     lambda e, buf: accum_fn(e + nbelow, buf), unroll=16)

    def finish(qi, c):
        outs = []
        for r in range(rheads):
            row0 = (g * rheads + r) * 3
            o = [oc_ref[qi * rheads + r]]
            for br in range(2):
                sr = (qi * 2 + br) * rheads + r
                o.append(acc_ref[sr, 0:HEAD_DIM, :]
                         / jnp.maximum(acc_ref[sr, HEAD_DIM:HEAD_DIM + 1, :], tiny))
            outs.append(sum(gt_ref[0, qi, pl.ds(row0 + j, 1), :] * o[j] for j in range(3)))
        out = jnp.concatenate(outs, axis=0).T
        o_ref[rows(qi), :] = (out * jax.nn.silu(dg_ref[rows(qi), :])).astype(BF16)
        return c

    lax.fori_loop(0, qt, finish, 0)


def _nsa_attn(qe, kc, vct, kboth, vtboth, gt, p1, ovt, batch, seq, tq, dg_blk):
    n = qe.shape[0]
    groups = D_KV_GROUPS
    rheads = D_HEADS // groups
    width = rheads * HEAD_DIM
    qt = seq // tq
    ncp = kc.shape[2]
    nslc = seq // SLC_BLOCK
    sel_rows = -(-nslc // SUBLANES) * SUBLANES
    wtiles = (WINDOW + tq - 1) // tq
    below = [(qi, t, 0) for qi in range(qt) for t in range(qi)]
    edge = []
    for qi in range(qt):
        edge += [(qi, qi, 0)] + [(qi, t, 1) for t in range(max(qi - wtiles, 0), qi + 1)]
    entries = below + edge
    tabs = [jnp.asarray([e[c] for e in entries], jnp.int32) for c in range(3)]
    kernel = functools.partial(_nsa_kernel, tq=tq, qt=qt, nslc=nslc, nsel=min(SLC_TOPK, nslc),
                               nbelow=len(below), nentries=len(entries))
    return pl.pallas_call(
        kernel,
        grid_spec=pltpu.PrefetchScalarGridSpec(
            num_scalar_prefetch=3,
            grid=(batch, groups),
            in_specs=[pl.BlockSpec((seq, width), lambda b, g, *_: (b, g)),
                      pl.BlockSpec((1, 1, ncp, LANES), lambda b, g, *_: (b, g, 0, 0)),
                      pl.BlockSpec((1, 1, HEAD_DIM, ncp), lambda b, g, *_: (b, g, 0, 0)),
                      pl.BlockSpec((2, seq, LANES), lambda b, g, *_: (0, b, g)),
                      pl.BlockSpec((1, 2, qt, HEAD_DIM + ONES_ROWS, tq), lambda b, g, *_: (b, 0, 0, g, 0)),
                      pl.BlockSpec((1, qt, LANES, tq), lambda b, g, *_: (b, 0, 0, 0)),
                      pl.BlockSpec((seq, width), lambda b, g, *_: (b, dg_blk + g)),
                      pl.BlockSpec(ovt.shape, lambda b, g, *_: (0, 0))],
            out_specs=pl.BlockSpec((seq, width), lambda b, g, *_: (b, g)),
            scratch_shapes=[pltpu.VMEM((qt * 2 * rheads, HEAD_DIM + ONES_ROWS, tq), F32),
                            pltpu.VMEM((qt * rheads, HEAD_DIM, tq), F32),
                            pltpu.VMEM((qt * sel_rows, tq), F32),
                            pltpu.VMEM((NSLC_PAD, tq), F32),
                            pltpu.VMEM((rheads, tq, tq), F32),
                            pltpu.VMEM((rheads, tq, tq), F32),
                            pltpu.VMEM((rheads, ncp, tq), F32),
                            pltpu.VMEM((2 * rheads, tq), F32),
                            pltpu.VMEM((qt * 2 * rheads, tq), F32),
                            pltpu.VMEM((seq, rheads * LANES), BF16)]),
        out_shape=jax.ShapeDtypeStruct((n, groups * width), BF16),
        compiler_params=_cparams(("parallel", "parallel"), 56),
        name="nsa_attn",
    )(*tabs, qe, kc, vct, kboth, vtboth, gt, p1, ovt)


def _rope_tables(pos, d):
    rd = d // 4
    half = rd // 2
    inv = ROPE_THETA ** (-jnp.arange(half, dtype=F32) / half)
    ang = pos.astype(F32)[:, None] * inv[None, :]
    cos, sin = jnp.cos(ang), jnp.sin(ang)
    npos = pos.shape[0]
    zeros = jnp.zeros((npos, d), F32)
    c = jnp.concatenate([cos, cos, jnp.ones((npos, d - rd), F32)], axis=1)
    s1 = zeros.at[:, :half].set(-sin)
    s2 = zeros.at[:, half:rd].set(sin)
    return tuple(jnp.tile(t, (1, LANES // d)) for t in (c, s1, s2))


def _overlap_t(ncp, nslc):
    start = np.arange(ncp) * CMP_STRIDE
    s0 = np.arange(nslc) * SLC_BLOCK
    lo = np.maximum(start[:, None], s0[None, :])
    hi = np.minimum(start[:, None] + CMP_BLOCK, s0[None, :] + SLC_BLOCK)
    ov = np.clip(hi - lo, 0, None) / CMP_BLOCK
    ov[ncp - 1] = 0.0
    out = np.zeros((NSLC_PAD, ncp), np.float32)
    out[:nslc] = ov.T
    return jnp.asarray(out, BF16)


def kernel(x, mem, mem_norm, l0_norm, l0_w_in, l0_a_vnorm, l0_a_ws, l0_a_bs, l0_b_qnorm, l0_b_knorm, l0_b_lq1, l0_b_lk1, l0_b_lq2, l0_b_lk2, l0_b_subln, l0_m_wkv, l0_m_qnorm, l0_m_knorm, l0_w_out, l1_norm, l1_w_in, l1_c_conv_w, l1_c_conv_b, l1_c_norm, l1_d_qnorm, l1_d_knorm, l1_d_cmp_pos_k, l1_d_cmp_w1_k, l1_d_cmp_w2_k, l1_d_cmp_pos_v, l1_d_cmp_w1_v, l1_d_cmp_w2_v, l1_m_wkv, l1_m_qnorm, l1_m_knorm, l1_w_out):
    batch, seq, d_model = x.shape
    n = batch * seq
    tq = 256
    mw = M_HEADS * HEAD_DIM
    pos = jnp.arange(seq, dtype=jnp.int32)
    x0 = x.reshape(n, d_model)

    wkv_both = jnp.concatenate([l0_m_wkv, l1_m_wkv], axis=1).astype(BF16)
    kgains = jnp.stack([jnp.tile(l0_m_knorm, M_HEADS), jnp.tile(l1_m_knorm, M_HEADS)])
    mem_kh, mem_vt = _mem_kv(mem, mem_norm, wkv_both, kgains)

    w = l0_w_in
    tab32 = _rope_tables(pos, B_QK_DIM)
    b_width = 2 * B_HEADS * B_QK_DIM
    a_width = l0_a_vnorm.shape[0]
    bs_exp = jnp.repeat(l0_a_bs.T, a_width // A_GROUPS, axis=1)
    p0, y_a, qn, kn, vt = _layer0_proj(
        x0, l0_norm, w[:, 0:1536].astype(BF16), w[:, 1536:3072].astype(BF16), w[:, 3072:4096].astype(BF16),
        tab32, jnp.tile(l0_b_qnorm, b_width // B_QK_DIM).reshape(1, b_width),
        jnp.tile(l0_b_knorm, b_width // B_QK_DIM).reshape(1, b_width),
        l0_a_ws, bs_exp, l0_a_vnorm.reshape(1, a_width), batch, seq, 2 * tq, tq)
    lam_init = 0.8 - 0.6 * math.exp(-0.3 * 1)
    lam_p = jnp.stack([l0_b_lq1, l0_b_lk1, l0_b_lq2, l0_b_lk2])
    y_b = _diff_attn(lam_p, qn, kn, vt, p0, jnp.tile(l0_b_subln, 2).reshape(1, LANES),
                     batch, seq, tq, lam_init, 0)
    y_m = _mem_attn(p0, mem_kh, mem_vt, 0, jnp.tile(l0_m_qnorm, M_HEADS).reshape(1, mw),
                    512 // mw, 768 // mw, seq, 2 * tq)

    w = l1_w_in
    n_bg = 3 * D_HEADS
    w_rest = jnp.concatenate([w[:, 2840:3352], w[:, 2048:2304], w[:, 3352:3864]], axis=1)
    w_d = jnp.concatenate([w[:, 1536:2048], w[:, 2304:2816], w[:, 2816:2840],
                           jnp.zeros((d_model, LANES - n_bg), F32)], axis=1)
    groups, rheads = D_KV_GROUPS, D_HEADS // D_KV_GROUPS
    d_width = D_HEADS * HEAD_DIM
    tab64 = _rope_tables(pos, HEAD_DIM)
    x1, p1, y_c, qe, kboth, vtboth, gt = _layer1_proj(
        x0, y_a, y_b, y_m, l0_w_out.astype(BF16),
        l1_norm, w[:, 0:1536].astype(BF16), w_d.astype(BF16), w_rest.astype(BF16), tab64,
        jnp.tile(l1_d_qnorm, D_HEADS).reshape(1, d_width),
        jnp.tile(l1_d_knorm, LANES // HEAD_DIM).reshape(1, LANES),
        l1_c_conv_w, l1_c_conv_b, l1_c_norm, batch, seq, 2 * tq, tq)
    kv_blk = 512 // LANES

    ncp = seq // CMP_STRIDE
    cmp_pos = jnp.arange(ncp, dtype=jnp.int32) * CMP_STRIDE + (CMP_BLOCK - 1)
    hidden = l1_d_cmp_w2_k.shape[0]

    def both_w1(w1):
        w = w1.reshape(CMP_BLOCK, HEAD_DIM, hidden)
        out = jnp.zeros((CMP_BLOCK, LANES, groups * hidden), F32)
        for g in range(groups):
            out = out.at[:, g * HEAD_DIM:(g + 1) * HEAD_DIM, g * hidden:(g + 1) * hidden].set(w)
        return out.astype(BF16)

    def both_w2(w2):
        out = jnp.zeros((groups * hidden, groups * LANES), F32)
        for g in range(groups):
            out = out.at[g * hidden:(g + 1) * hidden, g * LANES:(g + 1) * LANES].set(w2)
        return out.astype(BF16)

    w2k = jnp.tile(l1_d_cmp_w2_k, (1, LANES // HEAD_DIM))
    w2v = jnp.zeros((hidden, LANES), F32).at[:, :HEAD_DIM].set(l1_d_cmp_w2_v)
    kc, vct = _compress(
        p1, kv_blk, jnp.tile(l1_d_cmp_pos_k, (1, groups)), jnp.tile(l1_d_cmp_pos_v, (1, groups)),
        both_w1(l1_d_cmp_w1_k), both_w2(w2k), both_w1(l1_d_cmp_w1_v), both_w2(w2v),
        jnp.tile(l1_d_knorm, groups * LANES // HEAD_DIM).reshape(1, groups * LANES),
        _rope_tables(cmp_pos, HEAD_DIM), batch, seq, groups)

    nslc = seq // SLC_BLOCK
    y_d = _nsa_attn(qe, kc, vct, kboth, vtboth, gt, p1, _overlap_t(ncp, nslc),
                    batch, seq, tq, 0)
    out = _mem_out(x1, y_c, y_d, p1, mem_kh, mem_vt, 1, jnp.tile(l1_m_qnorm, M_HEADS).reshape(1, mw),
                   768 // mw, 1024 // mw, l1_w_out.astype(BF16), seq, 2 * tq)
    return out.reshape(batch, seq, d_model)
```

```python
import functools
import math

import numpy as np
import jax
import jax.numpy as jnp
from jax import lax
from jax.experimental import pallas as pl
from jax.experimental.pallas import tpu as pltpu

F32 = jnp.float32
BF16 = jnp.bfloat16

EPS = 1e-6
ROPE_THETA = 500000.0
HEAD_DIM = 64
A_CHUNK = 128
A_GROUPS = 4
B_HEADS = 8
B_QK_DIM = 32
C_KERNEL = 31
D_HEADS = 8
D_KV_GROUPS = 2
CMP_BLOCK = 32
CMP_STRIDE = 16
SLC_BLOCK = 64
SLC_TOPK = 16
WINDOW = 512
M_HEADS = 4

LANES = 128
SUBLANES = 8
MXU_DIM = 256
HALO = 32
NSLC_PAD = 128
ONES_ROWS = 16

NT_DIMS = (((1,), (1,)), ((), ()))
LOG2E = math.log2(math.e)


def _cparams(semantics, vmem_mb):
    return pltpu.CompilerParams(dimension_semantics=semantics,
                                vmem_limit_bytes=vmem_mb * 1024 * 1024)


def _split_bf16(x):
    hi = x.astype(BF16)
    lo = (x - hi.astype(F32)).astype(BF16)
    return hi, lo


def _group_mean(x2, gsize):
    t, c = x2.shape
    w = min(c, MXU_DIM)
    r = lax.broadcasted_iota(jnp.int32, (w, w), 0) // gsize
    cc = lax.broadcasted_iota(jnp.int32, (w, w), 1) // gsize
    ones = jnp.where(r == cc, 1.0, 0.0).astype(BF16)
    outs = []
    for s in range(c // w):
        hi, lo = _split_bf16(x2[:, s * w:(s + 1) * w])
        outs.append(jnp.dot(hi, ones, preferred_element_type=F32)
                    + jnp.dot(lo, ones, preferred_element_type=F32))
    out = outs[0] if len(outs) == 1 else jnp.concatenate(outs, axis=1)
    return out * (1.0 / gsize)


def _tile_lanes(t, width):
    rep = width // t.shape[1]
    return t if rep == 1 else jnp.concatenate([t] * rep, axis=1)


def _rope(x, cos_t, s1_t, s2_t, half):
    width = x.shape[1]
    up = pltpu.roll(x, width - half, 1)
    dn = pltpu.roll(x, half, 1)
    return (x * _tile_lanes(cos_t, width) + up * _tile_lanes(s1_t, width)
            + dn * _tile_lanes(s2_t, width))


def _matmul_cols(h, w_ref, lo, hi, out_ref=None):
    r = jnp.dot(h, w_ref[:, lo:hi], preferred_element_type=F32)
    if out_ref is None:
        return r
    out_ref[:, lo:hi] = r


def _rms_rows(x, ng_ref):
    ms = jnp.mean(x * x, axis=-1, keepdims=True)
    return (x * lax.rsqrt(ms + EPS) * ng_ref[...]).astype(BF16)


def _scores_t(k, q, valid, st_ref, idx):
    st = lax.dot_general(k, q, NT_DIMS, preferred_element_type=F32)
    if isinstance(valid, (list, tuple)):
        blk = st.shape[0] // len(valid)
        st = jnp.concatenate([jnp.where(v, st[c * blk:(c + 1) * blk, :], -jnp.inf)
                              for c, v in enumerate(valid)], axis=0)
    elif valid is not None:
        st = jnp.where(valid, st, -jnp.inf)
    st_ref[idx] = st
    return jnp.max(st, axis=0, keepdims=True)


def _accumulate_t(st_ref, idx, cmax, m, acc_ref, aidx, vt, guard_empty):
    m_new = jnp.maximum(m, cmax)
    m_use = jnp.where(m_new == -jnp.inf, 0.0, m_new) if guard_empty else m_new
    alpha = jnp.exp2(m - m_use)
    pt = jnp.exp2((st_ref[idx] - m_use).astype(BF16))
    acc_ref[aidx] = alpha * acc_ref[aidx] + jnp.dot(vt, pt, preferred_element_type=F32)
    return m_new


def _pipelined_sweep(n, score_fn, accum_fn, unroll=4):
    score_fn(0, 0)
    stages = n - 1

    def stage(e, parity):
        score_fn(e + 1, 1 - parity)
        accum_fn(e, parity)

    def body(i, c):
        for k in range(unroll):
            stage(unroll * i + k, k % 2)
        return c

    if isinstance(n, int):
        iters, left = divmod(stages, unroll)
        if iters:
            lax.fori_loop(0, iters, body, 0)
        for k in range(left):
            stage(iters * unroll + k, k % 2)
        accum_fn(n - 1, left % 2)
        return
    lax.fori_loop(0, stages // unroll, body, 0)
    base = stages // unroll * unroll
    left = stages - base
    for k in range(unroll - 1):
        pl.when(left > k)(functools.partial(stage, base + k, k % 2))
    for parity in range(2):
        pl.when(left % 2 == parity)(functools.partial(accum_fn, n - 1, parity))


def _mix_out(ya_ref, yb_ref, ym_ref, w_ref):
    wa = ya_ref.shape[1]
    wb = yb_ref.shape[1]
    acc = jnp.dot(ya_ref[...], w_ref[0:wa, :], preferred_element_type=F32)
    acc += jnp.dot(yb_ref[...], w_ref[wa:wa + wb, :], preferred_element_type=F32)
    acc += jnp.dot(ym_ref[...], w_ref[wa + wb:, :], preferred_element_type=F32)
    return acc


def _sgu_group(u, v, gate, w_ref, bs, vgain, o_ref, cols):
    tm = u.shape[0]
    row = lax.broadcasted_iota(jnp.int32, (A_CHUNK, A_CHUNK), 0)
    col = lax.broadcasted_iota(jnp.int32, (A_CHUNK, A_CHUNK), 1)
    w = jnp.where(col <= row, w_ref, 0.0).astype(BF16)
    gv = jax.nn.gelu(v)
    ms = jnp.mean(gv * gv, axis=-1, keepdims=True)
    vn = (gv * lax.rsqrt(ms + EPS) * vgain).astype(BF16)
    gu = jax.nn.gelu(u) * jax.nn.silu(gate)
    for c in range(tm // A_CHUNK):
        rows = slice(c * A_CHUNK, (c + 1) * A_CHUNK)
        z = jnp.dot(w, vn[rows, :], preferred_element_type=F32) + bs
        o_ref[rows, cols] = (gu[rows, :] * z).astype(BF16)


def _layer0_proj_kernel(x_ref, ng_ref, wa_ref, wqkv_ref, w_ref, c_ref, s1_ref, s2_ref, qg_ref, kg_ref,
                        ws_ref, bs_ref, vg_ref, p_ref, ya_ref, qo_ref, ko_ref, vto_ref):
    h = _rms_rows(x_ref[...], ng_ref)
    aw = vg_ref.shape[1]
    gdim = aw // A_GROUPS
    width = qg_ref.shape[1]
    half = B_QK_DIM // 4 // 2
    pw = p_ref.shape[1]

    a = _matmul_cols(h, wa_ref, 0, 3 * aw)

    def sgu(g):
        cols = slice(g * gdim, (g + 1) * gdim)
        _sgu_group(a[:, cols], a[:, aw + g * gdim:aw + (g + 1) * gdim],
                   a[:, 2 * aw + g * gdim:2 * aw + (g + 1) * gdim],
                   ws_ref[g], bs_ref[:, cols], vg_ref[:, cols], ya_ref, cols)

    def prep(t, gain, dst, mul):
        tn = t * lax.rsqrt(_group_mean(t * t, B_QK_DIM) + EPS) * gain[...]
        tr = _rope(tn, c_ref[...], s1_ref[...], s2_ref[...], half)
        dst[...] = (tr * mul if mul != 1.0 else tr).astype(BF16)

    q = _matmul_cols(h, wqkv_ref, 0, width)
    sgu(0)
    k = _matmul_cols(h, wqkv_ref, width, 2 * width)
    sgu(1)
    v = _matmul_cols(h, wqkv_ref, 2 * width, 3 * width)
    sgu(2)
    _matmul_cols(h, w_ref, 0, pw // 2, p_ref)
    sgu(3)
    prep(q, qg_ref, qo_ref, B_QK_DIM ** -0.5 * LOG2E)
    _matmul_cols(h, w_ref, pw // 2, pw, p_ref)
    prep(k, kg_ref, ko_ref, 1.0)
    vt = v.T
    dv = 2 * B_QK_DIM
    ones = jnp.ones((ONES_ROWS, vt.shape[1]), F32)
    parts = []
    for hd in range(B_HEADS):
        parts += [vt[hd * dv:(hd + 1) * dv, :], ones]
    vt_all = jnp.concatenate(parts, axis=0).astype(BF16)
    tk = vto_ref.shape[3]
    for c in range(vto_ref.shape[1]):
        vto_ref[0, c] = vt_all[:, c * tk:(c + 1) * tk]


def _layer0_proj(x, norm_g, w_a, w_qkv, w_rest, tables, qgain, kgain, a_ws, bs_exp, vgain, batch, seq, tm, tk):
    n, d = x.shape
    width = qgain.shape[1]
    aw = vgain.shape[1]
    tiles_per_seq = seq // tm
    tab_spec = pl.BlockSpec((tm, LANES), lambda i: (i % tiles_per_seq, 0))
    vrows = B_HEADS * (2 * B_QK_DIM + ONES_ROWS)

    def full(a):
        return pl.BlockSpec(a.shape, lambda i: (0,) * a.ndim)

    def rows(w):
        return pl.BlockSpec((tm, w), lambda i: (i, 0))

    return pl.pallas_call(
        _layer0_proj_kernel,
        grid=(n // tm,),
        in_specs=[rows(d), pl.BlockSpec((1, d), lambda i: (0, 0)), full(w_a), full(w_qkv), full(w_rest),
                  tab_spec, tab_spec, tab_spec, full(qgain), full(kgain),
                  full(a_ws), full(bs_exp), full(vgain)],
        out_specs=[rows(w_rest.shape[1]), rows(aw), rows(width), rows(width),
                   pl.BlockSpec((1, tm // tk, vrows, tk),
                                lambda i: (i // tiles_per_seq, i % tiles_per_seq, 0, 0))],
        out_shape=[jax.ShapeDtypeStruct((n, w_rest.shape[1]), F32),
                   jax.ShapeDtypeStruct((n, aw), BF16),
                   jax.ShapeDtypeStruct((n, width), BF16), jax.ShapeDtypeStruct((n, width), BF16),
                   jax.ShapeDtypeStruct((batch, seq // tk, vrows, tk), BF16)],
        compiler_params=_cparams(("parallel",), 52),
        name="layer0_proj",
    )(x, norm_g.reshape(1, d), w_a, w_qkv, w_rest, *tables, qgain, kgain, a_ws, bs_exp, vgain)


def _diff_attn_kernel(qi_tab, tile_tab, lam_ref, q_ref, k_ref, vt_ref, g_ref, sub_ref, o_ref,
                      acc_ref, sta_ref, stb_ref, cmax_ref, m_ref, *, tq, qt, lam_init):
    dv = 2 * B_QK_DIM
    vrows = dv + ONES_ROWS
    nmap = 4
    lane = lax.broadcasted_iota(jnp.int32, (1, LANES), 1)
    bufs = (sta_ref, stb_ref)
    acc_ref[...] = jnp.zeros_like(acc_ref)
    m_ref[...] = jnp.full(m_ref.shape, -jnp.inf, F32)
    krow = lax.broadcasted_iota(jnp.int32, (tq, tq), 0)
    qcol = lax.broadcasted_iota(jnp.int32, (tq, tq), 1)
    lam_p = lam_ref[...]
    lam = (jnp.exp(jnp.sum(lam_p[0:1] * lam_p[1:2], axis=-1, keepdims=True))
           - jnp.exp(jnp.sum(lam_p[2:3] * lam_p[3:4], axis=-1, keepdims=True)) + lam_init)

    def rows(tile):
        return pl.ds(pl.multiple_of(tile * tq, tq), tq)

    def scores(qi, tile, buf, valid):
        q = q_ref[rows(qi), :]
        k = k_ref[rows(tile), :]
        zero = jnp.zeros_like(q)
        for j in range(nmap):
            qm = jnp.where(lane // B_QK_DIM == j, q, zero)
            row = buf * nmap + j
            cmax_ref[row:row + 1, :] = _scores_t(k, qm, valid, bufs[buf], j)

    def accumulate(qi, tile, buf):
        for j in range(nmap):
            vt = vt_ref[0, tile, (j // 2) * vrows:(j // 2 + 1) * vrows, :]
            cm, sr = buf * nmap + j, qi * nmap + j
            m_ref[pl.ds(sr, 1), :] = _accumulate_t(bufs[buf], j, cmax_ref[cm:cm + 1, :],
                                                   m_ref[pl.ds(sr, 1), :], acc_ref, sr, vt, False)

    def finish(qi):
        halves = []
        for h in range(2):
            o = [acc_ref[qi * nmap + 2 * h + mp, 0:dv, :] / acc_ref[qi * nmap + 2 * h + mp, dv:dv + 1, :]
                 for mp in range(2)]
            pd = o[0] - lam * o[1]
            ms2 = jnp.mean(pd * pd, axis=0, keepdims=True)
            halves.append(pd * lax.rsqrt(ms2 + EPS))
        ob = jnp.concatenate(halves, axis=0).T * sub_ref[...] * (1.0 - lam_init)
        o_ref[rows(qi), :] = (ob * jax.nn.silu(g_ref[rows(qi), :])).astype(BF16)

    def below_score(e, buf):
        scores(qi_tab[e], tile_tab[e], buf, None)

    def below_accum(e, buf):
        accumulate(qi_tab[e], tile_tab[e], buf)

    def diag_score(e, buf):
        scores(e, e, buf, krow <= qcol)

    def diag_accum(e, buf):
        accumulate(e, e, buf)
        finish(e)

    _pipelined_sweep(qt * (qt - 1) // 2, below_score, below_accum, unroll=16)
    _pipelined_sweep(qt, diag_score, diag_accum, unroll=16)


def _diff_attn(lam_p, qn, kn, vt, p0, subln, batch, seq, tq, lam_init, g_off):
    n, width = qn.shape
    pairs = width // LANES
    qt = seq // tq
    vrows = 2 * B_QK_DIM + ONES_ROWS
    below = [(qi, t) for qi in range(qt) for t in range(qi)]
    qi_tab = jnp.asarray([e[0] for e in below], jnp.int32)
    tile_tab = jnp.asarray([e[1] for e in below], jnp.int32)
    kernel = functools.partial(_diff_attn_kernel, tq=tq, qt=qt, lam_init=lam_init)
    seq_spec = pl.BlockSpec((seq, LANES), lambda b, p, *_: (b, p))
    return pl.pallas_call(
        kernel,
        grid_spec=pltpu.PrefetchScalarGridSpec(
            num_scalar_prefetch=2,
            grid=(batch, pairs),
            in_specs=[pl.BlockSpec(lam_p.shape, lambda b, p, *_: (0, 0)),
                      seq_spec, seq_spec,
                      pl.BlockSpec((1, qt, 2 * vrows, tq), lambda b, p, *_: (b, 0, p, 0)),
                      pl.BlockSpec((seq, LANES), lambda b, p, *_: (b, g_off + p)),
                      pl.BlockSpec((1, LANES), lambda b, p, *_: (0, 0))],
            out_specs=seq_spec,
            scratch_shapes=[pltpu.VMEM((qt * 4, vrows, tq), F32), pltpu.VMEM((4, tq, tq), F32),
                            pltpu.VMEM((4, tq, tq), F32), pltpu.VMEM((8, tq), F32),
                            pltpu.VMEM((qt * 4, tq), F32)]),
        out_shape=jax.ShapeDtypeStruct((n, width), BF16),
        compiler_params=_cparams(("parallel", "parallel"), 40),
        name="diff_attn",
    )(qi_tab, tile_tab, lam_p, qn, kn, vt, p0, subln)


def _mem_kv_kernel(mem_ref, g_ref, w_ref, kg_ref, kh_ref, vt_ref):
    x = mem_ref[0]
    ms = jnp.mean(x * x, axis=-1, keepdims=True)
    h = (x * lax.rsqrt(ms + EPS) * g_ref[...]).astype(BF16)
    kv = jnp.dot(h, w_ref[...], preferred_element_type=F32)
    mw = kv.shape[1] // 4
    lane = lax.broadcasted_iota(jnp.int32, (1, mw), 1)
    ones = jnp.ones((ONES_ROWS, kv.shape[0]), F32)
    for layer in range(2):
        k = kv[:, (2 * layer) * mw:(2 * layer + 1) * mw]
        kn = k * lax.rsqrt(_group_mean(k * k, HEAD_DIM) + EPS) * kg_ref[layer:layer + 1, :]
        vt = kv[:, (2 * layer + 1) * mw:(2 * layer + 2) * mw].T
        parts = []
        for hd in range(M_HEADS):
            kh_ref[layer, 0, hd] = jnp.where(lane // HEAD_DIM == hd, kn, 0.0).astype(BF16)
            parts += [vt[hd * HEAD_DIM:(hd + 1) * HEAD_DIM, :], ones]
        vt_ref[layer, 0] = jnp.concatenate(parts, axis=0).astype(BF16)


def _mem_kv(mem, mem_norm, wkv_both, kgains):
    batch, mtok, d = mem.shape
    mw = wkv_both.shape[1] // 4
    vrows = M_HEADS * (HEAD_DIM + ONES_ROWS)
    return pl.pallas_call(
        _mem_kv_kernel,
        grid=(batch,),
        in_specs=[pl.BlockSpec((1, mtok, d), lambda b: (b, 0, 0)),
                  pl.BlockSpec((1, d), lambda b: (0, 0)),
                  pl.BlockSpec(wkv_both.shape, lambda b: (0, 0)),
                  pl.BlockSpec(kgains.shape, lambda b: (0, 0))],
        out_specs=[pl.BlockSpec((2, 1, M_HEADS, mtok, mw), lambda b: (0, b, 0, 0, 0)),
                   pl.BlockSpec((2, 1, vrows, mtok), lambda b: (0, b, 0, 0))],
        out_shape=[jax.ShapeDtypeStruct((2, batch, M_HEADS, mtok, mw), BF16),
                   jax.ShapeDtypeStruct((2, batch, vrows, mtok), BF16)],
        compiler_params=_cparams(("parallel",), 32),
        name="mem_kv",
    )(mem, mem_norm.reshape(1, d), wkv_both, kgains)


def _mem_attn_tile(q_ref, g_ref, kh_ref, vt_ref, qg_ref, st_ref):
    vrows = HEAD_DIM + ONES_ROWS
    x = q_ref[...]
    qn = (x * lax.rsqrt(_group_mean(x * x, HEAD_DIM) + EPS) * qg_ref[...]
          * (HEAD_DIM ** -0.5 * LOG2E)).astype(BF16)
    cmax = [_scores_t(kh_ref[0, 0, h], qn, None, st_ref, h) for h in range(M_HEADS)]
    outs = []
    for h in range(M_HEADS):
        pt = jnp.exp2((st_ref[h] - cmax[h]).astype(BF16))
        acc = jnp.dot(vt_ref[0, 0, h * vrows:(h + 1) * vrows, :], pt, preferred_element_type=F32)
        outs.append(acc[0:HEAD_DIM, :] / acc[HEAD_DIM:HEAD_DIM + 1, :])
    out = jnp.concatenate(outs, axis=0).T
    return (out * jax.nn.silu(g_ref[...])).astype(BF16)


def _mem_attn_kernel(q_ref, g_ref, kh_ref, vt_ref, qg_ref, o_ref, st_ref):
    o_ref[...] = _mem_attn_tile(q_ref, g_ref, kh_ref, vt_ref, qg_ref, st_ref)


def _mem_out_kernel(x_ref, ya_ref, yb_ref, q_ref, g_ref, kh_ref, vt_ref, qg_ref, w_ref, o_ref, st_ref):
    ym = _mem_attn_tile(q_ref, g_ref, kh_ref, vt_ref, qg_ref, st_ref)
    wa = ya_ref.shape[1]
    wb = yb_ref.shape[1]
    acc = jnp.dot(ya_ref[...], w_ref[0:wa, :], preferred_element_type=F32)
    acc += jnp.dot(yb_ref[...], w_ref[wa:wa + wb, :], preferred_element_type=F32)
    acc += jnp.dot(ym, w_ref[wa + wb:, :], preferred_element_type=F32)
    o_ref[...] = x_ref[...] + acc


def _mem_out(x, ya, yb, p, kh, vt, layer, qgain, q_blk, g_blk, w, seq, tq):
    n, d = x.shape
    _, batch, heads, mtok, mw = kh.shape
    qt = seq // tq

    def rows(width, blk=0):
        return pl.BlockSpec((tq, width), lambda i: (i, blk), pipeline_mode=pl.Buffered(3))

    in_specs = [rows(d), rows(ya.shape[1]), rows(yb.shape[1]), rows(mw, q_blk), rows(mw, g_blk),
                pl.BlockSpec((1, 1, heads, mtok, mw), lambda i: (layer, i // qt, 0, 0, 0)),
                pl.BlockSpec((1, 1, vt.shape[2], mtok), lambda i: (layer, i // qt, 0, 0)),
                pl.BlockSpec((1, mw), lambda i: (0, 0)),
                pl.BlockSpec(w.shape, lambda i: (0, 0))]

    def streamed(*refs):
        *hbm_refs, st_ref = refs
        pltpu.emit_pipeline(
            functools.partial(_mem_out_kernel, st_ref=st_ref),
            grid=(n // tq,),
            in_specs=in_specs,
            out_specs=[pl.BlockSpec((tq, d), lambda i: (i, 0))],
        )(*hbm_refs)

    return pl.pallas_call(
        streamed,
        in_specs=[pl.BlockSpec(memory_space=pl.ANY)] * len(in_specs),
        out_specs=pl.BlockSpec(memory_space=pl.ANY),
        out_shape=jax.ShapeDtypeStruct((n, d), F32),
        scratch_shapes=[pltpu.VMEM((heads, mtok, tq), F32)],
        compiler_params=pltpu.CompilerParams(vmem_limit_bytes=48 * 1024 * 1024),
        name="mem_out_proj",
    )(x, ya, yb, p, p, kh, vt, qgain, w)


def _mem_attn(p, kh, vt, layer, qgain, q_blk, g_blk, seq, tq):
    n = p.shape[0]
    _, batch, heads, mtok, mw = kh.shape
    qt = seq // tq
    return pl.pallas_call(
        _mem_attn_kernel,
        grid=(n // tq,),
        in_specs=[pl.BlockSpec((tq, mw), lambda i: (i, q_blk)),
                  pl.BlockSpec((tq, mw), lambda i: (i, g_blk)),
                  pl.BlockSpec((1, 1, heads, mtok, mw), lambda i: (layer, i // qt, 0, 0, 0)),
                  pl.BlockSpec((1, 1, vt.shape[2], mtok), lambda i: (layer, i // qt, 0, 0)),
                  pl.BlockSpec((1, mw), lambda i: (0, 0))],
        out_specs=pl.BlockSpec((tq, mw), lambda i: (i, 0)),
        out_shape=jax.ShapeDtypeStruct((n, mw), BF16),
        scratch_shapes=[pltpu.VMEM((heads, mtok, tq), F32)],
        compiler_params=_cparams(("parallel",), 32),
        name="mem_attn",
    )(p, p, kh, vt, qgain)


def _layer1_proj_kernel(x_ref, ya_ref, yb_ref, ym_ref, wo_ref, ng_ref, wc_ref, wd_ref, w_ref,
                        c_ref, s1_ref, s2_ref, qg_ref, kg_ref, cw_ref, cb_ref, cn_ref,
                        x1_ref, p_ref, yc_ref, qo_ref, ko_ref, vto_ref, gto_ref,
                        hbuf_ref, shift_ref, *, tiles_per_seq):
    i = pl.program_id(0)
    ts = x_ref.shape[0]
    cs = hbuf_ref.shape[0] - HALO
    span = cs + HALO
    cwid = cn_ref.shape[1]
    qw = qg_ref.shape[1]
    pw = p_ref.shape[1]
    half = HEAD_DIM // 4 // 2
    tk = gto_ref.shape[3]
    x1 = x_ref[...] + _mix_out(ya_ref, yb_ref, ym_ref, wo_ref)
    x1_ref[...] = x1
    h = _rms_rows(x1, ng_ref)

    @pl.when(i % tiles_per_seq == 0)
    def _():
        hbuf_ref[0:HALO, :] = jnp.zeros((HALO, cwid), F32)

    ca = _matmul_cols(h, wc_ref, 0, cwid)
    cbv = _matmul_cols(h, wc_ref, cwid, 2 * cwid)
    q = _matmul_cols(h, wd_ref, 0, qw)
    glu = ca * jax.nn.sigmoid(cbv)
    first = HALO - (C_KERNEL - 1)

    def stage_rows(r0):
        hbuf_ref[HALO:HALO + cs, :] = glu[r0:r0 + cs, :]
        for ph in range(SUBLANES):
            shift_ref[ph, 0:span - ph, :] = hbuf_ref[ph:span, :]

    def conv_rows(r0, gate):
        y = cb_ref[...]
        for j in range(C_KERNEL):
            ph, base = (first + j) % SUBLANES, (first + j) // SUBLANES * SUBLANES
            y = y + shift_ref[ph, base:base + cs, :] * cw_ref[j:j + 1, :]
        ms = jnp.mean(y * y, axis=-1, keepdims=True)
        yn = y * lax.rsqrt(ms + EPS) * cn_ref[...]
        yc_ref[r0:r0 + cs, :] = (jax.nn.silu(yn) * jax.nn.silu(gate[r0:r0 + cs, :])).astype(BF16)
        hbuf_ref[0:HALO, :] = hbuf_ref[cs:cs + HALO, :]

    stage_rows(0)
    cg = _matmul_cols(h, wc_ref, 2 * cwid, 3 * cwid)
    d = _matmul_cols(h, wd_ref, qw, wd_ref.shape[1])
    psplit = -(-pw // (2 * MXU_DIM)) * MXU_DIM
    for blk in range(ts // cs):
        if blk:
            stage_rows(blk * cs)
        conv_rows(blk * cs, cg)
        if blk == 0:
            _matmul_cols(h, w_ref, 0, psplit, p_ref)
    _matmul_cols(h, w_ref, psplit, pw, p_ref)

    tabs = (c_ref[...], s1_ref[...], s2_ref[...])
    tn = q * lax.rsqrt(_group_mean(q * q, HEAD_DIM) + EPS) * qg_ref[...]
    qo_ref[...] = (_rope(tn, *tabs, half) * (HEAD_DIM ** -0.5 * LOG2E)).astype(BF16)
    ones = jnp.ones((ONES_ROWS, ts), F32)
    lane = lax.broadcasted_iota(jnp.int32, (1, LANES), 1)
    for br in range(2):
        k = d[:, 2 * br * LANES:(2 * br + 1) * LANES]
        kn = k * lax.rsqrt(_group_mean(k * k, HEAD_DIM) + EPS) * kg_ref[...]
        kr = _rope(kn, *tabs, half)
        swapped = pltpu.roll(kr, HEAD_DIM, 1)
        for g in range(D_KV_GROUPS):
            own = (lane // HEAD_DIM) == g
            ko_ref[br, :, g * LANES:(g + 1) * LANES] = jnp.where(own, kr, swapped).astype(BF16)
        vt = d[:, (2 * br + 1) * LANES:(2 * br + 2) * LANES].T
        parts = []
        for g in range(D_KV_GROUPS):
            parts += [vt[g * HEAD_DIM:(g + 1) * HEAD_DIM, :], ones]
        vt_all = jnp.concatenate(parts, axis=0).astype(BF16)
        for c in range(ts // tk):
            vto_ref[0, br, c] = vt_all[:, c * tk:(c + 1) * tk]
    gates = jax.nn.sigmoid(d[:, 4 * LANES:5 * LANES]).T
    for c in range(ts // tk):
        gto_ref[0, c] = gates[:, c * tk:(c + 1) * tk]


def _layer1_proj(x, ya, yb, ym, w_out, norm_g, w_c, w_d, w_rest, tables, qgain, kgain,
                 conv_w, conv_b, conv_norm, batch, seq, tm, tk):
    n, dm = x.shape
    cwid = conv_w.shape[1]
    tiles_per_seq = seq // tm
    tab_spec = pl.BlockSpec((tm, LANES), lambda i: (i % tiles_per_seq, 0))
    vrows = D_KV_GROUPS * (HEAD_DIM + ONES_ROWS)

    def full(a):
        return pl.BlockSpec(a.shape, lambda i: (0,) * a.ndim)

    def rows(w):
        return pl.BlockSpec((tm, w), lambda i: (i, 0))

    kernel = functools.partial(_layer1_proj_kernel, tiles_per_seq=tiles_per_seq)
    return pl.pallas_call(
        kernel,
        grid=(n // tm,),
        in_specs=[rows(dm), rows(ya.shape[1]), rows(yb.shape[1]), rows(ym.shape[1]), full(w_out),
                  pl.BlockSpec((1, dm), lambda i: (0, 0)), full(w_c), full(w_d), full(w_rest),
                  tab_spec, tab_spec, tab_spec, full(qgain), full(kgain),
                  full(conv_w), pl.BlockSpec((1, cwid), lambda i: (0, 0)),
                  pl.BlockSpec((1, cwid), lambda i: (0, 0))],
        out_specs=[rows(dm), rows(w_rest.shape[1]), rows(cwid), rows(qgain.shape[1]),
                   pl.BlockSpec((2, tm, D_KV_GROUPS * LANES), lambda i: (0, i, 0)),
                   pl.BlockSpec((1, 2, tm // tk, vrows, tk),
                                lambda i: (i // tiles_per_seq, 0, i % tiles_per_seq, 0, 0)),
                   pl.BlockSpec((1, tm // tk, LANES, tk),
                                lambda i: (i // tiles_per_seq, i % tiles_per_seq, 0, 0))],
        out_shape=[jax.ShapeDtypeStruct((n, dm), F32),
                   jax.ShapeDtypeStruct((n, w_rest.shape[1]), F32),
                   jax.ShapeDtypeStruct((n, cwid), BF16),
                   jax.ShapeDtypeStruct((n, qgain.shape[1]), BF16),
                   jax.ShapeDtypeStruct((2, n, D_KV_GROUPS * LANES), BF16),
                   jax.ShapeDtypeStruct((batch, 2, seq // tk, vrows, tk), BF16),
                   jax.ShapeDtypeStruct((batch, seq // tk, LANES, tk), F32)],
        scratch_shapes=[pltpu.VMEM((tk + HALO, cwid), F32),
                        pltpu.VMEM((SUBLANES, tk + HALO, cwid), F32)],
        compiler_params=_cparams(("arbitrary",), 56),
        name="layer1_proj",
    )(x, ya, yb, ym, w_out, norm_g.reshape(1, dm), w_c, w_d, w_rest, *tables, qgain, kgain, conv_w,
      conv_b.reshape(1, cwid), conv_norm.reshape(1, cwid))


def _compress_kernel(tk_ref, tv_ref, pk_ref, pv_ref, w1k_ref, w2k_ref, w1v_ref, w2v_ref,
                     kg_ref, c_ref, s1_ref, s2_ref, kc_ref, vct_ref):
    half = HEAD_DIM // 4 // 2
    groups, ncp = kc_ref.shape[1], kc_ref.shape[2]

    def mlp(t_ref, p_ref, w1_ref, w2_ref):
        a = jnp.zeros((ncp, w1_ref.shape[2]), F32)
        b = jnp.zeros((ncp, w1_ref.shape[2]), F32)
        for l in range(CMP_STRIDE):
            x = t_ref[pl.ds(l, ncp, stride=CMP_STRIDE), :]
            a += jnp.dot((x + p_ref[l:l + 1, :]).astype(BF16), w1_ref[l],
                         preferred_element_type=F32)
            b += jnp.dot((x + p_ref[CMP_STRIDE + l:CMP_STRIDE + l + 1, :]).astype(BF16),
                         w1_ref[CMP_STRIDE + l], preferred_element_type=F32)
        h = a + pltpu.roll(b, ncp - 1, 0)
        return jnp.dot(jax.nn.silu(h).astype(BF16), w2_ref[...], preferred_element_type=F32)

    kc = mlp(tk_ref, pk_ref, w1k_ref, w2k_ref)
    kn = kc * lax.rsqrt(_group_mean(kc * kc, HEAD_DIM) + EPS) * kg_ref[...]
    kr = _rope(kn, c_ref[...], s1_ref[...], s2_ref[...], half).astype(BF16)
    vc = mlp(tv_ref, pv_ref, w1v_ref, w2v_ref)
    for g in range(groups):
        kc_ref[0, g] = kr[:, g * LANES:(g + 1) * LANES]
        vct_ref[0, g] = vc[:, g * LANES:(g + 1) * LANES].T[0:HEAD_DIM, :].astype(BF16)


def _compress(p1, k_blk, pk, pv, w1k, w2k, w1v, w2v, kgain, tables, batch, seq, groups):
    ncp = seq // CMP_STRIDE

    def full(a):
        return pl.BlockSpec(a.shape, lambda b: (0,) * a.ndim)

    return pl.pallas_call(
        _compress_kernel,
        grid=(batch,),
        in_specs=[pl.BlockSpec((seq, LANES), lambda b: (b, k_blk)),
                  pl.BlockSpec((seq, LANES), lambda b: (b, k_blk + 1)),
                  full(pk), full(pv), full(w1k), full(w2k), full(w1v), full(w2v),
                  full(kgain), full(tables[0]), full(tables[1]), full(tables[2])],
        out_specs=[pl.BlockSpec((1, groups, ncp, LANES), lambda b: (b, 0, 0, 0)),
                   pl.BlockSpec((1, groups, HEAD_DIM, ncp), lambda b: (b, 0, 0, 0))],
        out_shape=[jax.ShapeDtypeStruct((batch, groups, ncp, LANES), BF16),
                   jax.ShapeDtypeStruct((batch, groups, HEAD_DIM, ncp), BF16)],
        compiler_params=_cparams(("parallel",), 32),
        name="nsa_compress",
    )(p1, p1, pk, pv, w1k, w2k, w1v, w2v, kgain, *tables)


def _nsa_kernel(qi_tab, tile_tab, br_tab, q_ref, kc_ref, vct_ref, k_ref, vt_ref, gt_ref,
                dg_ref, ovt_ref, o_ref, acc_ref, oc_ref, sel_ref, imp_ref, sta_ref, stb_ref, cst_ref,
                cmax_ref, m_ref, qh_ref, *, tq, qt, nslc, nsel, nbelow, nentries):
    g = pl.program_id(1)
    rheads = D_HEADS // D_KV_GROUPS
    seq = qt * tq
    tiny = float(np.finfo(np.float32).tiny)
    qcol = lax.broadcasted_iota(jnp.int32, (tq, tq), 1)
    krow = lax.broadcasted_iota(jnp.int32, (tq, tq), 0)
    bufs = (sta_ref, stb_ref)
    blocks_per_tile = tq // SLC_BLOCK
    nslab = -(-nslc // SUBLANES)
    sel_rows = nslab * SUBLANES

    def rows(tile):
        return pl.ds(tile * tq if isinstance(tile, int) else pl.multiple_of(tile * tq, tq), tq)

    half_lane = lax.broadcasted_iota(jnp.int32, (1, LANES), 1) // HEAD_DIM

    def split_heads(t, c):
        for r in range(rheads):
            blk = q_ref[rows(t), (r // 2) * LANES:(r // 2 + 1) * LANES]
            qh_ref[rows(t), r * LANES:(r + 1) * LANES] = jnp.where(half_lane == r % 2, blk,
                                                                  jnp.zeros_like(blk))
        return c

    lax.fori_loop(0, qt, split_heads, 0)

    def heads_q(qi):
        return [qh_ref[rows(qi), r * LANES:(r + 1) * LANES] for r in range(rheads)]

    kc = kc_ref[0, 0]
    vct = vct_ref[0, 0]
    ncp = kc.shape[0]
    ovt = ovt_ref[...]

    def select_blocks(qi):
        q = heads_q(qi)
        ncq = min(ncp, -(-(((qi + 1) * tq - CMP_BLOCK) // CMP_STRIDE + 1) // LANES) * LANES)
        cend = lax.broadcasted_iota(jnp.int32, (ncq, tq), 0) * CMP_STRIDE + (CMP_BLOCK - 1)
        cvalid = cend <= qi * tq + lax.broadcasted_iota(jnp.int32, (ncq, tq), 1)
        psum = jnp.zeros((ncq, tq), F32)
        cmax = []
        for r in range(rheads):
            st = lax.dot_general(kc[0:ncq, :], q[r], NT_DIMS, preferred_element_type=F32)
            st = jnp.where(cvalid, st, -jnp.inf)
            cst_ref[r, 0:ncq, :] = st
            cmax.append(jnp.max(st, axis=0, keepdims=True))
        for r in range(rheads):
            m = jnp.where(jnp.isfinite(cmax[r]), cmax[r], 0.0)
            e = jnp.exp2(cst_ref[r, 0:ncq, :] - m)
            p = e / jnp.maximum(jnp.sum(e, axis=0, keepdims=True), tiny)
            psum += p
            oc_ref[qi * rheads + r] = jnp.dot(vct[:, 0:ncq], p.astype(BF16), preferred_element_type=F32)
        hi, lo = _split_bf16(psum)
        imp = (jnp.dot(ovt[:, 0:ncq], hi, preferred_element_type=F32)
               + jnp.dot(ovt[:, 0:ncq], lo, preferred_element_type=F32))
        blk = lax.broadcasted_iota(jnp.int32, (NSLC_PAD, tq), 0)
        cur = (qi * tq + lax.broadcasted_iota(jnp.int32, (NSLC_PAD, tq), 1)) // SLC_BLOCK
        imp = jnp.where(blk > cur, -jnp.inf, imp)
        forced = (blk == 0) | (blk == cur) | (blk == cur - 1)
        imp = jnp.where(forced, jnp.inf, imp)
        imp_ref[...] = imp
        nvis = min(nslc, (qi + 1) * blocks_per_tile)
        vslab = -(-nvis // SUBLANES)
        slabs = [imp[s * SUBLANES:(s + 1) * SUBLANES, :] for s in range(vslab)]
        ranks = [jnp.zeros((SUBLANES, tq), F32) for _ in range(vslab)]
        sub = lax.broadcasted_iota(jnp.int32, (SUBLANES, tq), 0)
        for jp in range(nvis):
            row = imp_ref[jp:jp + 1, :]
            for s in range(vslab):
                if s * SUBLANES > jp:
                    ahead = jnp.where(row >= slabs[s], 1.0, 0.0)
                elif (s + 1) * SUBLANES - 1 < jp:
                    ahead = jnp.where(row > slabs[s], 1.0, 0.0)
                else:
                    ahead = jnp.where(sub > jp - s * SUBLANES, jnp.where(row >= slabs[s], 1.0, 0.0),
                                      jnp.where(row > slabs[s], 1.0, 0.0))
                ranks[s] = ranks[s] + ahead
        for s in range(vslab):
            start = qi * sel_rows + s * SUBLANES
            sel_ref[start:start + SUBLANES, :] = jnp.where(ranks[s] < nsel, 1.0, 0.0)

    for qi in range(qt):
        select_blocks(qi)
    acc_ref[...] = jnp.zeros_like(acc_ref)
    m_ref[...] = jnp.full(m_ref.shape, -jnp.inf, F32)

    def chosen_rows(qi, tile):
        base = qi * sel_rows + tile * blocks_per_tile
        return [sel_ref[pl.ds(base + c, 1), :] for c in range(blocks_per_tile)]

    def scores(qi, tile, br, buf, valid):
        q = heads_q(qi)
        k = k_ref[br, rows(tile), :]
        for r in range(rheads):
            row = buf * rheads + r
            cmax_ref[row:row + 1, :] = _scores_t(k, q[r], valid, bufs[buf], r)

    def below_score_fn(e, buf):
        qi, tile = qi_tab[e], tile_tab[e]
        scores(qi, tile, 0, buf, [c > 0.5 for c in chosen_rows(qi, tile)])

    def edge_score_fn(e, buf):
        qi, tile, br = qi_tab[e], tile_tab[e], br_tab[e]
        chosen = jnp.concatenate([jnp.broadcast_to(c, (SLC_BLOCK, tq)) for c in chosen_rows(qi, tile)],
                                 axis=0)
        dpos = (qi - tile) * tq + (qcol - krow)
        reach = jnp.where(br == 0, seq, WINDOW).astype(jnp.uint32)
        kept = jnp.where(jnp.maximum(chosen, br.astype(F32)) > 0.5, dpos, -1)
        scores(qi, tile, br, buf, kept.astype(jnp.uint32) < reach)

    def accum_fn(e, buf):
        qi, tile, br = qi_tab[e], tile_tab[e], br_tab[e]
        vt = vt_ref[0, br, tile]
        for r in range(rheads):
            cm, sr = buf * rheads + r, (qi * 2 + br) * rheads + r
            m_ref[pl.ds(sr, 1), :] = _accumulate_t(bufs[buf], r, cmax_ref[cm:cm + 1, :],
                                                   m_ref[pl.ds(sr, 1), :], acc_ref, sr, vt, True)

    _pipelined_sweep(nbelow, below_score_fn, accum_fn, unroll=16)
    _pipelined_sweep(nentries - nbelow, lambda e, buf: edge_score_fn(e + nbelow, buf),
                     lambda e, buf: accum_fn(e + nbelow, buf), unroll=16)

    def finish(qi, c):
        outs = []
        for r in range(rheads):
            row0 = (g * rheads + r) * 3
            o = [oc_ref[qi * rheads + r]]
            for br in range(2):
                sr = (qi * 2 + br) * rheads + r
                o.append(acc_ref[sr, 0:HEAD_DIM, :]
                         / jnp.maximum(acc_ref[sr, HEAD_DIM:HEAD_DIM + 1, :], tiny))
            outs.append(sum(gt_ref[0, qi, pl.ds(row0 + j, 1), :] * o[j] for j in range(3)))
        out = jnp.concatenate(outs, axis=0).T
        o_ref[rows(qi), :] = (out * jax.nn.silu(dg_ref[rows(qi), :])).astype(BF16)
        return c

    lax.fori_loop(0, qt, finish, 0)


def _nsa_attn(qe, kc, vct, kboth, vtboth, gt, p1, ovt, batch, seq, tq, dg_blk):
    n = qe.shape[0]
    groups = D_KV_GROUPS
    rheads = D_HEADS // groups
    width = rheads * HEAD_DIM
    qt = seq // tq
    ncp = kc.shape[2]
    nslc = seq // SLC_BLOCK
    sel_rows = -(-nslc // SUBLANES) * SUBLANES
    wtiles = (WINDOW + tq - 1) // tq
    below = [(qi, t, 0) for qi in range(qt) for t in range(qi)]
    edge = []
    for qi in range(qt):
        edge += [(qi, qi, 0)] + [(qi, t, 1) for t in range(max(qi - wtiles, 0), qi + 1)]
    entries = below + edge
    tabs = [jnp.asarray([e[c] for e in entries], jnp.int32) for c in range(3)]
    kernel = functools.partial(_nsa_kernel, tq=tq, qt=qt, nslc=nslc, nsel=min(SLC_TOPK, nslc),
                               nbelow=len(below), nentries=len(entries))
    return pl.pallas_call(
        kernel,
        grid_spec=pltpu.PrefetchScalarGridSpec(
            num_scalar_prefetch=3,
            grid=(batch, groups),
            in_specs=[pl.BlockSpec((seq, width), lambda b, g, *_: (b, g)),
                      pl.BlockSpec((1, 1, ncp, LANES), lambda b, g, *_: (b, g, 0, 0)),
                      pl.BlockSpec((1, 1, HEAD_DIM, ncp), lambda b, g, *_: (b, g, 0, 0)),
                      pl.BlockSpec((2, seq, LANES), lambda b, g, *_: (0, b, g)),
                      pl.BlockSpec((1, 2, qt, HEAD_DIM + ONES_ROWS, tq), lambda b, g, *_: (b, 0, 0, g, 0)),
                      pl.BlockSpec((1, qt, LANES, tq), lambda b, g, *_: (b, 0, 0, 0)),
                      pl.BlockSpec((seq, width), lambda b, g, *_: (b, dg_blk + g)),
                      pl.BlockSpec(ovt.shape, lambda b, g, *_: (0, 0))],
            out_specs=pl.BlockSpec((seq, width), lambda b, g, *_: (b, g)),
            scratch_shapes=[pltpu.VMEM((qt * 2 * rheads, HEAD_DIM + ONES_ROWS, tq), F32),
                            pltpu.VMEM((qt * rheads, HEAD_DIM, tq), F32),
                            pltpu.VMEM((qt * sel_rows, tq), F32),
                            pltpu.VMEM((NSLC_PAD, tq), F32),
                            pltpu.VMEM((rheads, tq, tq), F32),
                            pltpu.VMEM((rheads, tq, tq), F32),
                            pltpu.VMEM((rheads, ncp, tq), F32),
                            pltpu.VMEM((2 * rheads, tq), F32),
                            pltpu.VMEM((qt * 2 * rheads, tq), F32),
                            pltpu.VMEM((seq, rheads * LANES), BF16)]),
        out_shape=jax.ShapeDtypeStruct((n, groups * width), BF16),
        compiler_params=_cparams(("parallel", "parallel"), 56),
        name="nsa_attn",
    )(*tabs, qe, kc, vct, kboth, vtboth, gt, p1, ovt)


def _rope_tables(pos, d):
    rd = d // 4
    half = rd // 2
    inv = ROPE_THETA ** (-jnp.arange(half, dtype=F32) / half)
    ang = pos.astype(F32)[:, None] * inv[None, :]
    cos, sin = jnp.cos(ang), jnp.sin(ang)
    npos = pos.shape[0]
    zeros = jnp.zeros((npos, d), F32)
    c = jnp.concatenate([cos, cos, jnp.ones((npos, d - rd), F32)], axis=1)
    s1 = zeros.at[:, :half].set(-sin)
    s2 = zeros.at[:, half:rd].set(sin)
    return tuple(jnp.tile(t, (1, LANES // d)) for t in (c, s1, s2))


def _overlap_t(ncp, nslc):
    start = np.arange(ncp) * CMP_STRIDE
    s0 = np.arange(nslc) * SLC_BLOCK
    lo = np.maximum(start[:, None], s0[None, :])
    hi = np.minimum(start[:, None] + CMP_BLOCK, s0[None, :] + SLC_BLOCK)
    ov = np.clip(hi - lo, 0, None) / CMP_BLOCK
    ov[ncp - 1] = 0.0
    out = np.zeros((NSLC_PAD, ncp), np.float32)
    out[:nslc] = ov.T
    return jnp.asarray(out, BF16)


def kernel(x, mem, mem_norm, l0_norm, l0_w_in, l0_a_vnorm, l0_a_ws, l0_a_bs, l0_b_qnorm, l0_b_knorm, l0_b_lq1, l0_b_lk1, l0_b_lq2, l0_b_lk2, l0_b_subln, l0_m_wkv, l0_m_qnorm, l0_m_knorm, l0_w_out, l1_norm, l1_w_in, l1_c_conv_w, l1_c_conv_b, l1_c_norm, l1_d_qnorm, l1_d_knorm, l1_d_cmp_pos_k, l1_d_cmp_w1_k, l1_d_cmp_w2_k, l1_d_cmp_pos_v, l1_d_cmp_w1_v, l1_d_cmp_w2_v, l1_m_wkv, l1_m_qnorm, l1_m_knorm, l1_w_out):
    batch, seq, d_model = x.shape
    n = batch * seq
    tq = 256
    mw = M_HEADS * HEAD_DIM
    pos = jnp.arange(seq, dtype=jnp.int32)
    x0 = x.reshape(n, d_model)

    wkv_both = jnp.concatenate([l0_m_wkv, l1_m_wkv], axis=1).astype(BF16)
    kgains = jnp.stack([jnp.tile(l0_m_knorm, M_HEADS), jnp.tile(l1_m_knorm, M_HEADS)])
    mem_kh, mem_vt = _mem_kv(mem, mem_norm, wkv_both, kgains)

    w = l0_w_in
    tab32 = _rope_tables(pos, B_QK_DIM)
    b_width = 2 * B_HEADS * B_QK_DIM
    a_width = l0_a_vnorm.shape[0]
    bs_exp = jnp.repeat(l0_a_bs.T, a_width // A_GROUPS, axis=1)
    p0, y_a, qn, kn, vt = _layer0_proj(
        x0, l0_norm, w[:, 0:1536].astype(BF16), w[:, 1536:3072].astype(BF16), w[:, 3072:4096].astype(BF16),
        tab32, jnp.tile(l0_b_qnorm, b_width // B_QK_DIM).reshape(1, b_width),
        jnp.tile(l0_b_knorm, b_width // B_QK_DIM).reshape(1, b_width),
        l0_a_ws, bs_exp, l0_a_vnorm.reshape(1, a_width), batch, seq, 2 * tq, tq)
    lam_init = 0.8 - 0.6 * math.exp(-0.3 * 1)
    lam_p = jnp.stack([l0_b_lq1, l0_b_lk1, l0_b_lq2, l0_b_lk2])
    y_b = _diff_attn(lam_p, qn, kn, vt, p0, jnp.tile(l0_b_subln, 2).reshape(1, LANES),
                     batch, seq, tq, lam_init, 0)
    y_m = _mem_attn(p0, mem_kh, mem_vt, 0, jnp.tile(l0_m_qnorm, M_HEADS).reshape(1, mw),
                    512 // mw, 768 // mw, seq, 2 * tq)

    w = l1_w_in
    n_bg = 3 * D_HEADS
    w_rest = jnp.concatenate([w[:, 2840:3352], w[:, 2048:2304], w[:, 3352:3864]], axis=1)
    w_d = jnp.concatenate([w[:, 1536:2048], w[:, 2304:2816], w[:, 2816:2840],
                           jnp.zeros((d_model, LANES - n_bg), F32)], axis=1)
    groups, rheads = D_KV_GROUPS, D_HEADS // D_KV_GROUPS
    d_width = D_HEADS * HEAD_DIM
    tab64 = _rope_tables(pos, HEAD_DIM)
    x1, p1, y_c, qe, kboth, vtboth, gt = _layer1_proj(
        x0, y_a, y_b, y_m, l0_w_out.astype(BF16),
        l1_norm, w[:, 0:1536].astype(BF16), w_d.astype(BF16), w_rest.astype(BF16), tab64,
        jnp.tile(l1_d_qnorm, D_HEADS).reshape(1, d_width),
        jnp.tile(l1_d_knorm, LANES // HEAD_DIM).reshape(1, LANES),
        l1_c_conv_w, l1_c_conv_b, l1_c_norm, batch, seq, 2 * tq, tq)
    kv_blk = 512 // LANES

    ncp = seq // CMP_STRIDE
    cmp_pos = jnp.arange(ncp, dtype=jnp.int32) * CMP_STRIDE + (CMP_BLOCK - 1)
    hidden = l1_d_cmp_w2_k.shape[0]

    def both_w1(w1):
        w = w1.reshape(CMP_BLOCK, HEAD_DIM, hidden)
        out = jnp.zeros((CMP_BLOCK, LANES, groups * hidden), F32)
        for g in range(groups):
            out = out.at[:, g * HEAD_DIM:(g + 1) * HEAD_DIM, g * hidden:(g + 1) * hidden].set(w)
        return out.astype(BF16)

    def both_w2(w2):
        out = jnp.zeros((groups * hidden, groups * LANES), F32)
        for g in range(groups):
            out = out.at[g * hidden:(g + 1) * hidden, g * LANES:(g + 1) * LANES].set(w2)
        return out.astype(BF16)

    w2k = jnp.tile(l1_d_cmp_w2_k, (1, LANES // HEAD_DIM))
    w2v = jnp.zeros((hidden, LANES), F32).at[:, :HEAD_DIM].set(l1_d_cmp_w2_v)
    kc, vct = _compress(
        p1, kv_blk, jnp.tile(l1_d_cmp_pos_k, (1, groups)), jnp.tile(l1_d_cmp_pos_v, (1, groups)),
        both_w1(l1_d_cmp_w1_k), both_w2(w2k), both_w1(l1_d_cmp_w1_v), both_w2(w2v),
        jnp.tile(l1_d_knorm, groups * LANES // HEAD_DIM).reshape(1, groups * LANES),
        _rope_tables(cmp_pos, HEAD_DIM), batch, seq, groups)

    nslc = seq // SLC_BLOCK
    y_d = _nsa_attn(qe, kc, vct, kboth, vtboth, gt, p1, _overlap_t(ncp, nslc),
                    batch, seq, tq, 0)
    out = _mem_out(x1, y_c, y_d, p1, mem_kh, mem_vt, 1, jnp.tile(l1_m_qnorm, M_HEADS).reshape(1, mw),
                   768 // mw, 1024 // mw, l1_w_out.astype(BF16), seq, 2 * tq)
    return out.reshape(batch, seq, d_model)
```
